```python
import jax, jax.numpy as jnp
from jax import lax
import numpy as np

D_MODEL = 2048
BATCH = 4
SEQ = 2048
DEPTH = 1
DEC_BATCH = 128
DEC_SEQ = 1
PAST_LEN = 16384
PAGE_SIZE = 128

N_HEADS_A = D_MODEL // 256
HEAD_K = 128
HEAD_V = 128
QK_WIDTH = N_HEADS_A * HEAD_K
V_WIDTH = N_HEADS_A * HEAD_V
QKV_WIDTH = 2 * QK_WIDTH + V_WIDTH
CONV_A = 4
CHUNK = 64
SC_WIDTH = D_MODEL // 2
SC_GROUPS = 8
CONV_B = 3
IN_WIDTH = QKV_WIDTH + V_WIDTH + 2 * N_HEADS_A + 3 * SC_WIDTH + 2 * D_MODEL
N_GROUPS = 8
EXPERTS_PER_GROUP = 8
N_EXPERTS = N_GROUPS * EXPERTS_PER_GROUP
TOP_K = 2
D_FF_EXPERT = D_MODEL // 4
MOE_BLOCK = 128
EPS = 1e-6

kernel_name = 'hybrid_deltanet_shortconv_hmoe_step'


def rms_norm(x, w):
    xf = x.astype(jnp.float32)
    y = xf * lax.rsqrt(jnp.mean(xf * xf, axis=-1, keepdims=True) + EPS)
    return (y * w.astype(jnp.float32)).astype(x.dtype)


def l2_normalize(x):
    return x * lax.rsqrt(jnp.sum(x * x, axis=-1, keepdims=True) + EPS)


def causal_depthwise_conv(x, buf, w):
    k = w.shape[0]
    t = x.shape[1]
    xp = jnp.concatenate([buf.astype(x.dtype), x], axis=1)
    out = xp[:, 0:t] * w[0]
    for j in range(1, k):
        out = out + xp[:, j:j + t] * w[j]
    return out, xp[:, t:]


def gated_delta_recurrent(q, k, v, beta, g, s0):
    def step(s, inp):
        qt, kt, vt, bt, gt = inp
        s = s * jnp.exp(gt)[..., None, None]
        kv = jnp.einsum('bhk,bhkv->bhv', kt, s)
        s = s + jnp.einsum('bhk,bhv->bhkv', kt, (vt - kv) * bt[..., None])
        return s, jnp.einsum('bhk,bhkv->bhv', qt, s)
    xs = tuple(jnp.moveaxis(a, 1, 0) for a in (q, k, v, beta, g))
    s, o = lax.scan(step, s0, xs)
    return jnp.moveaxis(o, 0, 1), s


def gated_delta_chunked(q, k, v, beta, g, s0):
    b, t, h, _ = q.shape
    n = t // CHUNK

    def chunks(a):
        a = a.reshape((b, n, CHUNK) + a.shape[2:])
        return jnp.transpose(a, (1, 0, 3, 2) + tuple(range(4, a.ndim)))

    qc, kc, vc, bc, gc = (chunks(a) for a in (q, k, v, beta, g))
    gcum = jnp.cumsum(gc, axis=-1)
    idx = jnp.arange(CHUNK)
    causal = idx[:, None] >= idx[None, :]
    strict = idx[:, None] > idx[None, :]
    diff = gcum[..., :, None] - gcum[..., None, :]
    decay = jnp.where(causal, jnp.exp(jnp.where(causal, diff, 0.0)), 0.0)
    kb = kc * bc[..., None]
    m = jnp.where(strict, jnp.einsum('nbhik,nbhjk->nbhij', kb, kc) * decay, 0.0)
    eye = jnp.eye(CHUNK, dtype=jnp.float32)
    tinv = lax.linalg.triangular_solve(eye + m, jnp.broadcast_to(eye, m.shape), left_side=True, lower=True)
    u = tinv @ (vc * bc[..., None])
    w = tinv @ (kb * jnp.exp(gcum)[..., None])
    qk = jnp.einsum('nbhik,nbhjk->nbhij', qc, kc) * decay
    q_dec = qc * jnp.exp(gcum)[..., None]
    k_dec = kc * jnp.exp(gcum[..., -1:] - gcum)[..., None]
    g_last = jnp.exp(gcum[..., -1])

    def step(s, inp):
        u_n, w_n, qk_n, qd_n, kd_n, gl_n = inp
        v_new = u_n - jnp.einsum('bhck,bhkv->bhcv', w_n, s)
        o = jnp.einsum('bhck,bhkv->bhcv', qd_n, s) + jnp.einsum('bhij,bhjv->bhiv', qk_n, v_new)
        s = s * gl_n[..., None, None] + jnp.einsum('bhck,bhcv->bhkv', kd_n, v_new)
        return s, o

    s, o = lax.scan(step, s0, (u, w, qk, q_dec, k_dec, g_last))
    o = jnp.transpose(o, (1, 0, 3, 2, 4)).reshape(b, t, h, -1)
    return o, s


def parallel_mixers(h, s0, buf_a, buf_b, chunked, w_in, conv_a_w, a_log, dt_bias, out_norm_w,
                    w_branch_a, conv_b_w, w_branch_b, w_o):
    f32 = jnp.float32
    bsz, t, _ = h.shape
    proj = h @ w_in
    widths = (QKV_WIDTH, V_WIDTH, N_HEADS_A, N_HEADS_A, SC_WIDTH, SC_WIDTH, SC_WIDTH, D_MODEL, D_MODEL)
    parts = []
    off = 0
    for wd in widths:
        parts.append(proj[..., off:off + wd])
        off += wd
    qkv, z, beta_raw, alpha_raw, b_gate, c_gate, x_b, gate_a, gate_b = parts

    qkv, new_buf_a = causal_depthwise_conv(qkv, buf_a, conv_a_w)
    qkv = jax.nn.silu(qkv).astype(f32)
    q = qkv[..., :QK_WIDTH].reshape(bsz, t, N_HEADS_A, HEAD_K)
    k = qkv[..., QK_WIDTH:2 * QK_WIDTH].reshape(bsz, t, N_HEADS_A, HEAD_K)
    v = qkv[..., 2 * QK_WIDTH:].reshape(bsz, t, N_HEADS_A, HEAD_V)
    q = l2_normalize(q) * (HEAD_K ** -0.5)
    k = l2_normalize(k)
    beta = jax.nn.sigmoid(beta_raw.astype(f32))
    g = -jnp.exp(a_log.astype(f32)) * jax.nn.softplus(alpha_raw.astype(f32) + dt_bias.astype(f32))
    s0f = s0.astype(f32)
    if chunked:
        o, s_new = gated_delta_chunked(q, k, v, beta, g, s0f)
    else:
        o, s_new = gated_delta_recurrent(q, k, v, beta, g, s0f)
    zf = z.astype(f32).reshape(bsz, t, N_HEADS_A, HEAD_V)
    o = o * lax.rsqrt(jnp.mean(o * o, axis=-1, keepdims=True) + EPS) * out_norm_w.astype(f32) * jax.nn.silu(zf)
    o_a = o.reshape(bsz, t, V_WIDTH).astype(h.dtype) @ w_branch_a

    conv_out, new_buf_b = causal_depthwise_conv(c_gate * x_b, buf_b, conv_b_w)
    o_b = (b_gate * conv_out) @ w_branch_b

    merged = jax.nn.sigmoid(gate_a) * o_a + jax.nn.sigmoid(gate_b) * o_b
    return merged @ w_o, s_new, new_buf_a, new_buf_b


def grouped_experts(hf, expert_idx, comb_w, w_gate, w_up, w_down):
    n_tok, d = hf.shape
    n_assign = n_tok * TOP_K
    flat_e = expert_idx.reshape(-1).astype(jnp.int32)
    flat_tok = jnp.arange(n_assign, dtype=jnp.int32) // TOP_K
    flat_w = comb_w.reshape(-1)
    order = jnp.argsort(flat_e)
    sorted_e = flat_e[order]
    counts = jnp.bincount(flat_e, length=N_EXPERTS).astype(jnp.int32)
    start = jnp.cumsum(counts) - counts
    padded = (counts + MOE_BLOCK - 1) // MOE_BLOCK * MOE_BLOCK
    pad_end = jnp.cumsum(padded)
    pad_start = pad_end - padded
    rank = jnp.arange(n_assign, dtype=jnp.int32) - start[sorted_e]
    dest = pad_start[sorted_e] + rank
    n_blocks = -(-n_assign // MOE_BLOCK) + N_EXPERTS
    n_rows = n_blocks * MOE_BLOCK
    row_tok = jnp.full((n_rows,), n_tok, jnp.int32).at[dest].set(flat_tok[order])
    row_w = jnp.zeros((n_rows,), flat_w.dtype).at[dest].set(flat_w[order])
    block_start = jnp.arange(n_blocks, dtype=jnp.int32) * MOE_BLOCK
    block_e = jnp.minimum(jnp.searchsorted(pad_end, block_start, side='right'), N_EXPERTS - 1)
    h_pad = jnp.concatenate([hf, jnp.zeros((1, d), hf.dtype)], axis=0)
    x_rows = h_pad[row_tok].reshape(n_blocks, MOE_BLOCK, d)

    def expert_block(args):
        xb, e = args
        a = xb @ w_gate[e]
        u = xb @ w_up[e]
        return (jax.nn.silu(a) * u) @ w_down[e]

    y_rows = lax.map(expert_block, (x_rows, block_e)).reshape(n_rows, d)
    y = jax.ops.segment_sum(y_rows * row_w[:, None].astype(y_rows.dtype), row_tok, num_segments=n_tok + 1)
    return y[:n_tok]


def hier_moe(h, rgw, rgb, rew, reb, w_gate, w_up, w_down):
    bsz, t, d = h.shape
    hf = h.reshape(bsz * t, d)
    g_logits = (hf @ rgw + rgb).astype(jnp.float32)
    g_prob = jax.nn.softmax(g_logits, axis=-1)
    _, grp = lax.top_k(g_logits, 1)
    e_logits = (hf @ rew + reb).astype(jnp.float32).reshape(-1, N_GROUPS, EXPERTS_PER_GROUP)
    e_in_grp = jnp.take_along_axis(e_logits, grp[:, :, None], axis=1)[:, 0]
    e_prob = jax.nn.softmax(e_in_grp, axis=-1)
    top_p, top_i = lax.top_k(e_prob, TOP_K)
    comb = top_p / jnp.sum(top_p, axis=-1, keepdims=True) * jnp.take_along_axis(g_prob, grp, axis=1)
    expert_idx = grp * EXPERTS_PER_GROUP + top_i
    y = grouped_experts(hf, expert_idx, comb.astype(hf.dtype), w_gate, w_up, w_down)
    return y.reshape(bsz, t, d)


def decoder_layer(x, s0, buf_a, buf_b, chunked, norm1_w, w_in, conv_a_w, a_log, dt_bias, out_norm_w,
                  w_branch_a, conv_b_w, w_branch_b, w_o, norm2_w, rgw, rgb, rew, reb, w_gate, w_up, w_down):
    h = rms_norm(x, norm1_w)
    mix, s_new, nba, nbb = parallel_mixers(h, s0, buf_a, buf_b, chunked, w_in, conv_a_w, a_log, dt_bias,
                                           out_norm_w, w_branch_a, conv_b_w, w_branch_b, w_o)
    x = x + mix.astype(x.dtype)
    h = rms_norm(x, norm2_w)
    x = x + hier_moe(h, rgw, rgb, rew, reb, w_gate, w_up, w_down).astype(x.dtype)
    return x, s_new, nba, nbb


def setup_inputs(seed: int = 0) -> dict:
    key = jax.random.key(seed)
    ks = jax.random.split(key, 26)
    f32 = jnp.float32

    def nrm(k, shape, scale):
        return jax.random.normal(k, shape, f32) * scale

    return {
        'x_prompt': nrm(ks[0], (BATCH, SEQ, D_MODEL), 1.0),
        'x_sample': nrm(ks[1], (DEC_BATCH, DEC_SEQ, D_MODEL), 1.0),
        'state_delta': nrm(ks[2], (DEPTH, DEC_BATCH, N_HEADS_A, HEAD_K, HEAD_V), HEAD_K ** -0.5),
        'state_qkv_conv': nrm(ks[3], (DEPTH, DEC_BATCH, CONV_A - 1, QKV_WIDTH), 1.0),
        'state_short_conv': nrm(ks[4], (DEPTH, DEC_BATCH, CONV_B - 1, SC_WIDTH), 1.0),
        'norm1_w': 1.0 + nrm(ks[5], (DEPTH, D_MODEL), 0.02),
        'w_in': nrm(ks[6], (DEPTH, D_MODEL, IN_WIDTH), D_MODEL ** -0.5),
        'conv_a_w': nrm(ks[7], (DEPTH, CONV_A, QKV_WIDTH), CONV_A ** -0.5),
        'a_log': jnp.log(jax.random.uniform(ks[8], (DEPTH, N_HEADS_A), f32, 1.0, 16.0)),
        'dt_bias': nrm(ks[9], (DEPTH, N_HEADS_A), 0.1),
        'out_norm_w': 1.0 + nrm(ks[10], (DEPTH, HEAD_V), 0.02),
        'w_branch_a': nrm(ks[11], (DEPTH, V_WIDTH, D_MODEL), V_WIDTH ** -0.5),
        'conv_b_w': nrm(ks[12], (DEPTH, CONV_B, SC_WIDTH), CONV_B ** -0.5),
        'w_branch_b': nrm(ks[13], (DEPTH, SC_WIDTH, D_MODEL), SC_WIDTH ** -0.5),
        'w_o': nrm(ks[14], (DEPTH, D_MODEL, D_MODEL), D_MODEL ** -0.5),
        'norm2_w': 1.0 + nrm(ks[15], (DEPTH, D_MODEL), 0.02),
        'router_group_w': nrm(ks[16], (DEPTH, D_MODEL, N_GROUPS), D_MODEL ** -0.5),
        'router_group_b': nrm(ks[17], (DEPTH, N_GROUPS), 0.01),
        'router_expert_w': nrm(ks[18], (DEPTH, D_MODEL, N_EXPERTS), D_MODEL ** -0.5),
        'router_expert_b': nrm(ks[19], (DEPTH, N_EXPERTS), 0.01),
        'w_gate': nrm(ks[20], (DEPTH, N_EXPERTS, D_MODEL, D_FF_EXPERT), D_MODEL ** -0.5),
        'w_up': nrm(ks[21], (DEPTH, N_EXPERTS, D_MODEL, D_FF_EXPERT), D_MODEL ** -0.5),
        'w_down': nrm(ks[22], (DEPTH, N_EXPERTS, D_FF_EXPERT, D_MODEL), D_FF_EXPERT ** -0.5),
        'final_norm_w': 1.0 + nrm(ks[23], (D_MODEL,), 0.02),
    }


def reference(x_prompt, x_sample, state_delta, state_qkv_conv, state_short_conv, norm1_w, w_in, conv_a_w,
              a_log, dt_bias, out_norm_w, w_branch_a, conv_b_w, w_branch_b, w_o, norm2_w, router_group_w,
              router_group_b, router_expert_w, router_expert_b, w_gate, w_up, w_down, final_norm_w):
    xp = x_prompt
    xs = x_sample
    bp = xp.shape[0]
    dp_list, ap_list, cp_list = [], [], []
    ds_list, as_list, cs_list = [], [], []
    for l in range(DEPTH):
        params = (norm1_w[l], w_in[l], conv_a_w[l], a_log[l], dt_bias[l], out_norm_w[l], w_branch_a[l],
                  conv_b_w[l], w_branch_b[l], w_o[l], norm2_w[l], router_group_w[l], router_group_b[l],
                  router_expert_w[l], router_expert_b[l], w_gate[l], w_up[l], w_down[l])
        s0p = jnp.zeros((bp, N_HEADS_A, HEAD_K, HEAD_V), jnp.float32)
        ba0 = jnp.zeros((bp, CONV_A - 1, QKV_WIDTH), xp.dtype)
        bb0 = jnp.zeros((bp, CONV_B - 1, SC_WIDTH), xp.dtype)
        xp, sp, nap, nbp = decoder_layer(xp, s0p, ba0, bb0, True, *params)
        dp_list.append(sp.astype(xp.dtype))
        ap_list.append(nap)
        cp_list.append(nbp)
        xs, ss, nas, nbs = decoder_layer(xs, state_delta[l], state_qkv_conv[l], state_short_conv[l], False, *params)
        ds_list.append(ss.astype(state_delta.dtype))
        as_list.append(nas.astype(state_qkv_conv.dtype))
        cs_list.append(nbs.astype(state_short_conv.dtype))
    y_prompt = rms_norm(xp, final_norm_w)
    y_sample = rms_norm(xs, final_norm_w)
    new_delta_prompt = jnp.stack(dp_list, axis=0)
    new_qkv_conv_prompt = jnp.stack(ap_list, axis=0)
    new_short_conv_prompt = jnp.stack(cp_list, axis=0)
    new_delta_sample = jnp.stack(ds_list, axis=0)
    new_qkv_conv_sample = jnp.stack(as_list, axis=0)
    new_short_conv_sample = jnp.stack(cs_list, axis=0)
    return (y_prompt, y_sample, new_delta_prompt, new_qkv_conv_prompt, new_short_conv_prompt, new_delta_sample, new_qkv_conv_sample, new_short_conv_sample)
```

```python
import functools

import jax
import jax.numpy as jnp
from jax import lax
from jax.experimental import pallas as pl
from jax.experimental.pallas import tpu as pltpu

f32 = jnp.float32
bf16 = jnp.bfloat16
i32 = jnp.int32

EPS = 1e-6
LANE = 128
D_MODEL = 2048
N_HEADS = 8
HEAD = 128
QK_W = N_HEADS * HEAD
QKV_W = 3 * QK_W
SC_W = 1024
CONV_A = 4
CONV_B = 3
CHUNK = 64
GROUP_HEADS = 4
N_EXPERTS = 64
N_GROUPS = 8
EXPERTS_PER_GROUP = 8
D_FF = 512
MOE_ROWS = 128

COL_QKV = 0
COL_BCX = 3072
COL_GA = 6144
COL_GB = 8192
COL_Z = 10240
COL_BA = 11264
BA_W = 256
PROJ_W = 11520
PROJ_TN = 1280

VMEM_LIMIT = 56 * 1024 * 1024


def _dot(a, b):
    return jnp.dot(a, b, preferred_element_type=f32)


def _dot_nt(a, b):
    return lax.dot_general(a, b, (((1,), (1,)), ((), ())), preferred_element_type=f32)


def _split(x, n):
    parts = []
    r = x
    for i in range(n):
        p = r.astype(bf16)
        parts.append(p)
        if i + 1 < n:
            r = r - p.astype(f32)
    return parts


def _dot_lsplit(x, m, n=3):
    acc = None
    for p in _split(x, n):
        d = _dot(p, m)
        acc = d if acc is None else acc + d
    return acc


def _dot_rsplit(m, x, n=3):
    acc = None
    for p in _split(x, n):
        d = _dot(m, p)
        acc = d if acc is None else acc + d
    return acc


def _silu(x):
    return x * jax.nn.sigmoid(x)


def _softplus(x):
    return jnp.maximum(x, 0.0) + jnp.log(1.0 + jnp.exp(-jnp.abs(x)))


def _cparams(sem):
    return pltpu.CompilerParams(dimension_semantics=sem, vmem_limit_bytes=VMEM_LIMIT)


def _inproj_kernel(x_ref, nw_ref, w_ref, o_ref, h_ref, *, rows):
    @pl.when(pl.program_id(1) == 0)
    def _():
        def body(r, c):
            sl = pl.ds(pl.multiple_of(r * rows, rows), rows)
            x = x_ref[sl, :]
            ms = jnp.mean(x * x, axis=-1, keepdims=True)
            h_ref[sl, :] = (x * lax.rsqrt(ms + EPS) * nw_ref[...]).astype(bf16)
            return c
        lax.fori_loop(0, x_ref.shape[0] // rows, body, 0)

    o_ref[...] = _dot(h_ref[...], w_ref[...])


def _inproj(x2d, norm_w, w_bf16):
    n = x2d.shape[0]
    tm = min(1024, n)
    assert n % tm == 0 and PROJ_W % PROJ_TN == 0
    return pl.pallas_call(
        functools.partial(_inproj_kernel, rows=min(128, tm)),
        out_shape=jax.ShapeDtypeStruct((n, PROJ_W), f32),
        grid=(n // tm, PROJ_W // PROJ_TN),
        in_specs=[pl.BlockSpec((tm, D_MODEL), lambda i, j: (i, 0)),
                  pl.BlockSpec((1, D_MODEL), lambda i, j: (0, 0)),
                  pl.BlockSpec((D_MODEL, PROJ_TN), lambda i, j: (0, j))],
        out_specs=pl.BlockSpec((tm, PROJ_TN), lambda i, j: (i, j)),
        scratch_shapes=[pltpu.VMEM((tm, D_MODEL), bf16)],
        compiler_params=_cparams(("arbitrary", "arbitrary")),
        name="inproj",
    )(x2d, norm_w.reshape(1, D_MODEL), w_bf16)


def _head_l2norm(a, scale):
    outs = []
    for h in range(N_HEADS):
        ah = a[:, h * HEAD:(h + 1) * HEAD]
        ss = jnp.sum(ah * ah, axis=-1, keepdims=True)
        n = ah * lax.rsqrt(ss + EPS)
        outs.append(n * scale if scale != 1.0 else n)
    return outs


def _delta_prompt_kernel(qkv_ref, bcx_ref, z_ref, ba_ref, cwa_ref, cwb_ref, alog_ref, dtb_ref, onw_ref,
                         eb_ref, eg_ref, e64_ref,
                         o_ref, y_ref, snew_ref, nca_ref, ncb_ref,
                         s_ref, xa_ref, xb_ref, *, nb_step):
    C = CHUNK
    G = GROUP_HEADS
    R = G * C
    t = pl.program_id(1)
    nt = pl.num_programs(1)

    @pl.when(t == 0)
    def _():
        s_ref[...] = jnp.zeros(s_ref.shape, f32)
        xa_ref[:, 0:8, :] = jnp.zeros((nb_step, 8, QKV_W), f32)
        xb_ref[:, 0:8, :] = jnp.zeros((nb_step, 8, SC_W), f32)

    rr = lax.broadcasted_iota(i32, (R, R), 0)
    cc = lax.broadcasted_iota(i32, (R, R), 1)
    same = (rr >> 6) == (cc >> 6)
    incl = same & (rr >= cc)
    strict = same & (rr > cc)
    eye = jnp.where(rr == cc, 1.0, 0.0).astype(f32)
    r2 = lax.broadcasted_iota(i32, (R, G * HEAD), 0)
    c2 = lax.broadcasted_iota(i32, (R, G * HEAD), 1)
    bdmask = (r2 >> 6) == (c2 >> 7)
    r3 = lax.broadcasted_iota(i32, (C, C), 0)
    c3 = lax.broadcasted_iota(i32, (C, C), 1)
    ltri = jnp.where(r3 >= c3, 1.0, 0.0).astype(bf16)
    r4 = lax.broadcasted_iota(i32, (C, R), 0)
    c4 = lax.broadcasted_iota(i32, (C, R), 1)
    ident_t = r4 == (c4 & (C - 1))
    ones8 = jnp.ones((8, C), bf16)

    for nb in range(nb_step):
        raw = qkv_ref[nb]
        xa_ref[nb, 8:8 + C, :] = raw

        def conv_sec(lo):
            hi = lo + QK_W
            acc = xa_ref[nb, 5:5 + C, lo:hi] * cwa_ref[0:1, lo:hi]
            acc = acc + xa_ref[nb, 6:6 + C, lo:hi] * cwa_ref[1:2, lo:hi]
            acc = acc + xa_ref[nb, 7:7 + C, lo:hi] * cwa_ref[2:3, lo:hi]
            acc = acc + raw[:, lo:hi] * cwa_ref[3:4, lo:hi]
            return _silu(acc)

        qn = _head_l2norm(conv_sec(0), HEAD ** -0.5)
        kn = _head_l2norm(conv_sec(QK_W), 1.0)
        vv = conv_sec(2 * QK_W)
        xa_ref[nb, 0:8, :] = xa_ref[nb, C:C + 8, :]

        bt = ba_ref[nb]
        beta_all = jax.nn.sigmoid(bt)
        g_all = -(jnp.exp(alog_ref[...]) * _softplus(bt + dtb_ref[...]))
        beta_exp = _dot_lsplit(beta_all, eb_ref[...])
        g_exp = _dot_lsplit(g_all, eg_ref[...])
        gc_exp = _dot_rsplit(ltri, g_exp)
        gc_small = _dot_rsplit(ltri, g_all)
        gl_exp = gc_exp[C - 1:C, :]

        o_heads = [None] * N_HEADS
        for g in range(N_HEADS // G):
            hs = list(range(g * G, (g + 1) * G))

            def stack(a):
                return jnp.concatenate([a[:, h * HEAD:(h + 1) * HEAD] for h in hs], axis=0)

            k_st = jnp.concatenate([kn[h] for h in hs], axis=0)
            q_st = jnp.concatenate([qn[h] for h in hs], axis=0)
            v_st = stack(vv)
            beta_st = stack(beta_exp)
            gc_st = stack(gc_exp)
            gl_st = jnp.concatenate(
                [jnp.broadcast_to(gl_exp[:, h * HEAD:(h + 1) * HEAD], (C, HEAD)) for h in hs], axis=0)
            kb = k_st * beta_st
            vb = v_st * beta_st
            egc = jnp.exp(gc_st)
            kbg = kb * egc
            qd = q_st * egc
            kd = k_st * jnp.exp(gl_st - gc_st)

            gx = _dot_lsplit(gc_small, e64_ref[g])
            crow = _dot_rsplit(ones8, jnp.where(ident_t, gx, 0.0))[0:1, :]
            diff = jnp.concatenate([gc_st, gc_st], axis=1) - crow
            dec = jnp.where(incl, jnp.exp(jnp.where(incl, diff, 0.0)), 0.0)

            a = _dot_nt(jnp.concatenate([kb, q_st], axis=0).astype(bf16), k_st.astype(bf16))
            nm = jnp.where(strict, -(a[0:R] * dec), 0.0)
            qkm = a[R:2 * R] * dec

            p = eye + nm
            nmb = nm.astype(bf16)
            nk = _dot(nmb, nmb)
            for _ in range(4):
                nkb = nk.astype(bf16)
                x = _dot(jnp.concatenate([p, nk], axis=0).astype(bf16), nkb)
                p = p + x[0:R]
                nk = x[R:2 * R]
            p = p + _dot(p.astype(bf16), nk.astype(bf16))

            uw = _dot(p.astype(bf16), jnp.concatenate([vb, kbg], axis=1).astype(bf16))
            u = uw[:, 0:HEAD]
            w = uw[:, HEAD:2 * HEAD]

            vnew = []
            ointer = []
            for j, h in enumerate(hs):
                sh = s_ref[nb, :, h * HEAD:(h + 1) * HEAD]
                lhs = jnp.concatenate([w[j * C:(j + 1) * C], qd[j * C:(j + 1) * C]], axis=0)
                ws = _dot(lhs.astype(bf16), sh.astype(bf16))
                vnew.append(u[j * C:(j + 1) * C] - ws[0:C])
                ointer.append(ws[C:2 * C])
            vnew_st = jnp.concatenate(vnew, axis=0)
            o_st = jnp.concatenate(ointer, axis=0) + _dot(qkm.astype(bf16), vnew_st.astype(bf16))

            vbd = jnp.where(bdmask, jnp.concatenate([vnew_st] * G, axis=1), 0.0)
            lo = g * G * HEAD
            hi = lo + G * HEAD
            sg = s_ref[nb, :, lo:hi]
            s_ref[nb, :, lo:hi] = sg * jnp.exp(gl_exp[:, lo:hi]) + _dot(kd.T.astype(bf16), vbd.astype(bf16))
            for j, h in enumerate(hs):
                o_heads[h] = o_st[j * C:(j + 1) * C]

        zt = z_ref[nb]
        for h in range(N_HEADS):
            oh = o_heads[h]
            ms = jnp.mean(oh * oh, axis=-1, keepdims=True)
            zh = zt[:, h * HEAD:(h + 1) * HEAD]
            on = oh * lax.rsqrt(ms + EPS) * onw_ref[...] * _silu(zh)
            o_ref[nb, :, h * HEAD:(h + 1) * HEAD] = on.astype(bf16)

        bcx = bcx_ref[nb]
        cx = bcx[:, SC_W:2 * SC_W] * bcx[:, 2 * SC_W:3 * SC_W]
        xb_ref[nb, 8:8 + C, :] = cx
        cv = xb_ref[nb, 6:6 + C, :] * cwb_ref[0:1, :]
        cv = cv + xb_ref[nb, 7:7 + C, :] * cwb_ref[1:2, :]
        cv = cv + cx * cwb_ref[2:3, :]
        y_ref[nb] = (bcx[:, 0:SC_W] * cv).astype(bf16)
        xb_ref[nb, 0:8, :] = xb_ref[nb, C:C + 8, :]

    @pl.when(t == nt - 1)
    def _():
        for nb in range(nb_step):
            for h in range(N_HEADS):
                snew_ref[nb, h] = s_ref[nb, :, h * HEAD:(h + 1) * HEAD]
            nca_ref[nb] = xa_ref[nb, 5:8, :]
            ncb_ref[nb] = xb_ref[nb, 6:8, :]


def _expand_consts():
    lane = jnp.arange(BA_W)[:, None]
    col = jnp.arange(QK_W)[None, :]
    eb = (lane == (col >> 7)).astype(bf16)
    eg = (lane == (8 + (col >> 7))).astype(bf16)
    col64 = jnp.arange(GROUP_HEADS * CHUNK)[None, :]
    e64 = jnp.stack([(lane == (8 + g * GROUP_HEADS + (col64 >> 6))).astype(bf16)
                     for g in range(N_HEADS // GROUP_HEADS)], axis=0)
    return eb, eg, e64


def _delta_prompt(proj3, cwa, cwb, alog_row, dtb_row, onw_row, nb_step):
    b, t, _ = proj3.shape
    assert t % CHUNK == 0 and b % nb_step == 0
    eb, eg, e64 = _expand_consts()
    c = CHUNK
    const2 = lambda bi, ti: (0, 0)
    outs = pl.pallas_call(
        functools.partial(_delta_prompt_kernel, nb_step=nb_step),
        out_shape=(jax.ShapeDtypeStruct((b, t, QK_W), bf16),
                   jax.ShapeDtypeStruct((b, t, SC_W), bf16),
                   jax.ShapeDtypeStruct((b, N_HEADS, HEAD, HEAD), f32),
                   jax.ShapeDtypeStruct((b, CONV_A - 1, QKV_W), f32),
                   jax.ShapeDtypeStruct((b, CONV_B - 1, SC_W), f32)),
        grid=(b // nb_step, t // c),
        in_specs=[pl.BlockSpec((nb_step, c, QKV_W), lambda bi, ti: (bi, ti, COL_QKV // QKV_W)),
                  pl.BlockSpec((nb_step, c, QKV_W), lambda bi, ti: (bi, ti, COL_BCX // QKV_W)),
                  pl.BlockSpec((nb_step, c, QK_W), lambda bi, ti: (bi, ti, COL_Z // QK_W)),
                  pl.BlockSpec((nb_step, c, BA_W), lambda bi, ti: (bi, ti, COL_BA // BA_W)),
                  pl.BlockSpec((CONV_A, QKV_W), const2),
                  pl.BlockSpec((CONV_B, SC_W), const2),
                  pl.BlockSpec((1, BA_W), const2),
                  pl.BlockSpec((1, BA_W), const2),
                  pl.BlockSpec((1, HEAD), const2),
                  pl.BlockSpec((BA_W, QK_W), const2),
                  pl.BlockSpec((BA_W, QK_W), const2),
                  pl.BlockSpec((N_HEADS // GROUP_HEADS, BA_W, GROUP_HEADS * CHUNK), lambda bi, ti: (0, 0, 0))],
        out_specs=(pl.BlockSpec((nb_step, c, QK_W), lambda bi, ti: (bi, ti, 0)),
                   pl.BlockSpec((nb_step, c, SC_W), lambda bi, ti: (bi, ti, 0)),
                   pl.BlockSpec((nb_step, N_HEADS, HEAD, HEAD), lambda bi, ti: (bi, 0, 0, 0)),
                   pl.BlockSpec((nb_step, CONV_A - 1, QKV_W), lambda bi, ti: (bi, 0, 0)),
                   pl.BlockSpec((nb_step, CONV_B - 1, SC_W), lambda bi, ti: (bi, 0, 0))),
        scratch_shapes=[pltpu.VMEM((nb_step, HEAD, QK_W), f32),
                        pltpu.VMEM((nb_step, 8 + c, QKV_W), f32),
                        pltpu.VMEM((nb_step, 8 + c, SC_W), f32)],
        compiler_params=_cparams(("arbitrary", "arbitrary")),
        name="delta_prompt",
    )(proj3, proj3, proj3, proj3, cwa, cwb, alog_row, dtb_row, onw_row, eb, eg, e64)
    return outs


def _sample_prep_kernel(p_ref, bufa_ref, bufb_ref, cwa_ref, cwb_ref, alog_ref, dtb_ref, eb_ref, eg_ref,
                        q_ref, k_ref, v_ref, beta_ref, eg_out_ref, y_ref, nbufa_ref, nbufb_ref):
    def conv_sec(lo):
        hi = lo + QK_W
        raw = p_ref[:, COL_QKV + lo:COL_QKV + hi]
        acc = bufa_ref[0, :, lo:hi] * cwa_ref[0:1, lo:hi]
        acc = acc + bufa_ref[1, :, lo:hi] * cwa_ref[1:2, lo:hi]
        acc = acc + bufa_ref[2, :, lo:hi] * cwa_ref[2:3, lo:hi]
        acc = acc + raw * cwa_ref[3:4, lo:hi]
        nbufa_ref[0, :, lo:hi] = bufa_ref[1, :, lo:hi]
        nbufa_ref[1, :, lo:hi] = bufa_ref[2, :, lo:hi]
        nbufa_ref[2, :, lo:hi] = raw
        return _silu(acc)

    qn = _head_l2norm(conv_sec(0), HEAD ** -0.5)
    kn = _head_l2norm(conv_sec(QK_W), 1.0)
    for h in range(N_HEADS):
        q_ref[:, h * HEAD:(h + 1) * HEAD] = qn[h]
        k_ref[:, h * HEAD:(h + 1) * HEAD] = kn[h]
    v_ref[...] = conv_sec(2 * QK_W)

    bt = p_ref[:, COL_BA:COL_BA + BA_W]
    beta_all = jax.nn.sigmoid(bt)
    g_all = -(jnp.exp(alog_ref[...]) * _softplus(bt + dtb_ref[...]))
    beta_ref[...] = _dot_lsplit(beta_all, eb_ref[...])
    eg_out_ref[...] = jnp.exp(_dot_lsplit(g_all, eg_ref[...]))

    bg = p_ref[:, COL_BCX:COL_BCX + SC_W]
    cx = p_ref[:, COL_BCX + SC_W:COL_BCX + 2 * SC_W] * p_ref[:, COL_BCX + 2 * SC_W:COL_BCX + 3 * SC_W]
    cv = bufb_ref[0] * cwb_ref[0:1, :]
    cv = cv + bufb_ref[1] * cwb_ref[1:2, :]
    cv = cv + cx * cwb_ref[2:3, :]
    y_ref[...] = (bg * cv).astype(bf16)
    nbufb_ref[0] = bufb_ref[1]
    nbufb_ref[1] = cx


def _sample_prep(proj_s, bufa_t, bufb_t, cwa, cwb, alog_row, dtb_row):
    n = proj_s.shape[0]
    eb, eg, _ = _expand_consts()
    row = jax.ShapeDtypeStruct((n, QK_W), f32)
    return pl.pallas_call(
        _sample_prep_kernel,
        out_shape=(row, row, row, row, row,
                   jax.ShapeDtypeStruct((n, SC_W), bf16),
                   jax.ShapeDtypeStruct((CONV_A - 1, n, QKV_W), f32),
                   jax.ShapeDtypeStruct((CONV_B - 1, n, SC_W), f32)),
        compiler_params=pltpu.CompilerParams(vmem_limit_bytes=VMEM_LIMIT),
        name="sample_prep",
    )(proj_s, bufa_t, bufb_t, cwa, cwb, alog_row, dtb_row, eb, eg)


def _sample_step_kernel(s_ref, q_ref, k_ref, v_ref, beta_ref, eg_ref, z_ref, onw_ref,
                        snew_ref, o_ref, *, bb):
    w = N_HEADS * HEAD
    r8 = lax.broadcasted_iota(i32, (N_HEADS, w), 0)
    c8 = lax.broadcasted_iota(i32, (N_HEADS, w), 1)
    mask8 = r8 == (c8 >> 7)
    zpad_k = jnp.zeros((HEAD - N_HEADS, HEAD), f32)
    zpad_d = jnp.zeros((HEAD - N_HEADS, w), f32)
    for b in range(bb):
        s_all = jnp.concatenate([s_ref[b, h] for h in range(N_HEADS)], axis=1)
        eg8 = eg_ref[b]
        eg_row = jnp.concatenate([eg8[h:h + 1, :] for h in range(N_HEADS)], axis=1)
        s_dec = s_all * eg_row
        k8 = k_ref[b]
        x = _dot(k8.astype(bf16), s_dec.astype(bf16))
        v_t = jnp.concatenate([v_ref[b]] * N_HEADS, axis=1)
        b_t = jnp.concatenate([beta_ref[b]] * N_HEADS, axis=1)
        d_bd = jnp.where(mask8, (v_t - x) * b_t, 0.0)
        kt = jnp.concatenate([k8, zpad_k], axis=0).T
        d_pad = jnp.concatenate([d_bd, zpad_d], axis=0)
        k_hi, k_lo = _split(kt, 2)
        d_hi, d_lo = _split(d_pad, 2)
        s_new = s_dec + (_dot(k_hi, d_hi) + _dot(k_hi, d_lo) + _dot(k_lo, d_hi))
        yv = jnp.where(mask8, _dot(q_ref[b].astype(bf16), s_new.astype(bf16)), 0.0)
        o8 = yv[:, 0:HEAD]
        for j in range(1, N_HEADS):
            o8 = o8 + yv[:, j * HEAD:(j + 1) * HEAD]
        ms = jnp.mean(o8 * o8, axis=-1, keepdims=True)
        o_ref[b] = o8 * lax.rsqrt(ms + EPS) * onw_ref[...] * _silu(z_ref[b])
        for h in range(N_HEADS):
            snew_ref[b, h] = s_new[:, h * HEAD:(h + 1) * HEAD]


def _sample_step(state, q, k, v, beta, eg, z, onw_row, bb=4):
    n = state.shape[0]
    assert n % bb == 0
    hspec = pl.BlockSpec((bb, N_HEADS, HEAD), lambda i: (i, 0, 0))
    sspec = pl.BlockSpec((bb, N_HEADS, HEAD, HEAD), lambda i: (i, 0, 0, 0))
    return pl.pallas_call(
        functools.partial(_sample_step_kernel, bb=bb),
        out_shape=(jax.ShapeDtypeStruct(state.shape, f32),
                   jax.ShapeDtypeStruct((n, N_HEADS, HEAD), f32)),
        grid=(n // bb,),
        in_specs=[sspec, hspec, hspec, hspec, hspec, hspec, hspec, pl.BlockSpec((1, HEAD), lambda i: (0, 0))],
        out_specs=(sspec, hspec),
        compiler_params=_cparams(("arbitrary",)),
        name="sample_step",
    )(state, q, k, v, beta, eg, z, onw_row)


def _mix_route_kernel(x_ref, o_ref, y_ref, ga_ref, gb_ref, wa_ref, wb_ref, wo_ref, n2_ref,
                      rwh_ref, rwl_ref, rb_ref, cnt_in_ref, x1_ref, h2_ref, mi_ref, mw_ref, cnt_ref):
    i = pl.program_id(0)
    tm = x_ref.shape[0]

    @pl.when(i == 0)
    def _():
        cnt_ref[...] = cnt_in_ref[...]

    oa = _dot(o_ref[...], wa_ref[...])
    ob = _dot(y_ref[...], wb_ref[...])
    merged = jax.nn.sigmoid(ga_ref[...]) * oa + jax.nn.sigmoid(gb_ref[...]) * ob
    x1 = x_ref[...] + _dot(merged.astype(bf16), wo_ref[...])
    x1_ref[...] = x1
    ms = jnp.mean(x1 * x1, axis=-1, keepdims=True)
    h2 = x1 * lax.rsqrt(ms + EPS) * n2_ref[...]
    h2_ref[...] = h2

    h_hi, h_lo = _split(h2, 2)
    logits = _dot(h_hi, rwh_ref[...]) + _dot(h_hi, rwl_ref[...]) + _dot(h_lo, rwh_ref[...]) + rb_ref[...]

    lane = lax.broadcasted_iota(i32, (tm, LANE), 1)
    lanef = lane.astype(f32)
    neg = jnp.float32(-jnp.inf)
    big = jnp.float32(1e9)
    gmask = (lane >= N_EXPERTS) & (lane < N_EXPERTS + N_GROUPS)
    gl = jnp.where(gmask, logits, neg)
    gmax = jnp.max(gl, axis=-1, keepdims=True)
    gidx = jnp.min(jnp.where(gl == gmax, lanef - N_EXPERTS, big), axis=-1, keepdims=True)
    gsum = jnp.sum(jnp.where(gmask, jnp.exp(gl - gmax), 0.0), axis=-1, keepdims=True)
    gprob = 1.0 / gsum

    emask = (lane < N_EXPERTS) & ((lane >> 3).astype(f32) == gidx)
    el = jnp.where(emask, logits, neg)
    emax = jnp.max(el, axis=-1, keepdims=True)
    pe = jnp.where(emask, jnp.exp(el - emax), 0.0)
    eprob = pe / jnp.sum(pe, axis=-1, keepdims=True)
    p1m = jnp.where(emask, eprob, -1.0)
    m1 = jnp.max(p1m, axis=-1, keepdims=True)
    i1 = jnp.min(jnp.where(p1m == m1, lanef, big), axis=-1, keepdims=True)
    p2m = jnp.where(lanef == i1, -1.0, p1m)
    m2 = jnp.max(p2m, axis=-1, keepdims=True)
    i2 = jnp.min(jnp.where(p2m == m2, lanef, big), axis=-1, keepdims=True)
    tot = m1 + m2
    c1 = m1 / tot * gprob
    c2 = m2 / tot * gprob

    oh1 = jnp.where(lanef == i1, 1.0, 0.0)
    oh2 = jnp.where(lanef == i2, 1.0, 0.0)
    ohs = oh1 + oh2
    rt = lax.broadcasted_iota(i32, (tm, tm), 0)
    ct = lax.broadcasted_iota(i32, (tm, tm), 1)
    lstrict = jnp.where(rt > ct, 1.0, 0.0).astype(bf16)
    cs = _dot(lstrict, ohs.astype(bf16)) + cnt_ref[...]
    rank1 = jnp.sum(cs * oh1, axis=-1, keepdims=True)
    rank2 = jnp.sum(cs * oh2, axis=-1, keepdims=True)
    cnt_ref[...] = cnt_ref[...] + jnp.sum(ohs, axis=0, keepdims=True)

    mi = jnp.where(lane == 0, i1, jnp.where(lane == 1, i2, jnp.where(lane == 2, rank1,
                                                                     jnp.where(lane == 3, rank2, 0.0))))
    mi_ref[...] = mi.astype(i32)
    mw_ref[...] = jnp.where(lane == 0, c1, jnp.where(lane == 1, c2, 0.0))


def _mix_route(x2d, o2d, y2d, proj2d, wa, wb, wo, n2_row, rwh, rwl, rb_row, cnt_in):
    n = x2d.shape[0]
    tm = min(256, n)
    assert n % tm == 0
    tok = lambda width: pl.BlockSpec((tm, width), lambda i: (i, 0))
    full = lambda a: pl.BlockSpec(a.shape, lambda i: (0,) * a.ndim)
    in_specs = [tok(D_MODEL), tok(QK_W), tok(SC_W),
                pl.BlockSpec((tm, D_MODEL), lambda i: (i, COL_GA // D_MODEL)),
                pl.BlockSpec((tm, D_MODEL), lambda i: (i, COL_GB // D_MODEL)),
                full(wa), full(wb), full(wo), full(n2_row), full(rwh), full(rwl), full(rb_row), full(cnt_in)]
    out_shape = (jax.ShapeDtypeStruct((n, D_MODEL), f32),
                 jax.ShapeDtypeStruct((n, D_MODEL), f32),
                 jax.ShapeDtypeStruct((n, LANE), i32),
                 jax.ShapeDtypeStruct((n, LANE), f32),
                 jax.ShapeDtypeStruct((1, LANE), f32))
    out_specs = (tok(D_MODEL), tok(D_MODEL), tok(LANE), tok(LANE),
                 pl.BlockSpec((1, LANE), lambda i: (0, 0)))
    return pl.pallas_call(
        _mix_route_kernel,
        out_shape=out_shape,
        grid=(n // tm,),
        in_specs=in_specs,
        out_specs=out_specs,
        compiler_params=_cparams(("arbitrary",)),
        name="mix_route",
    )(x2d, o2d, y2d, proj2d, proj2d, wa, wb, wo, n2_row, rwh, rwl, rb_row, cnt_in)


def _row_copies(n_rows, make, act):
    def body(r, c):
        for cp in make(r):
            act(cp)
        return c
    lax.fori_loop(0, n_rows, body, 0)


def _dispatch_kernel(dest_ref, hp_ref, hs_ref, xs_ref, sem, *, np_tiles):
    tm = hp_ref.shape[0]

    def scatter_rows(h_ref):
        def make(r):
            return [pltpu.make_async_copy(h_ref.at[pl.ds(r, 1)], xs_ref.at[pl.ds(dest_ref[2 * r + k], 1)], sem)
                    for k in range(2)]
        _row_copies(tm, make, lambda cp: cp.start())
        _row_copies(tm, make, lambda cp: cp.wait())

    @pl.when(pl.program_id(0) < np_tiles)
    def _():
        scatter_rows(hp_ref)

    @pl.when(pl.program_id(0) >= np_tiles)
    def _():
        scatter_rows(hs_ref)


def _dispatch(h2_p, h2_s, dest_flat):
    tm = MOE_ROWS
    n_p, n_s = h2_p.shape[0], h2_s.shape[0]
    assert n_p % tm == 0 and n_s == tm
    np_tiles = n_p // tm
    return pl.pallas_call(
        functools.partial(_dispatch_kernel, np_tiles=np_tiles),
        out_shape=jax.ShapeDtypeStruct((2 * (n_p + n_s), D_MODEL), f32),
        grid=(np_tiles + 1,),
        in_specs=[pl.BlockSpec((2 * tm,), lambda i: (i,), memory_space=pltpu.SMEM),
                  pl.BlockSpec((tm, D_MODEL), lambda i: (jnp.minimum(i, np_tiles - 1), 0)),
                  pl.BlockSpec((tm, D_MODEL), lambda i: (0, 0))],
        out_specs=pl.BlockSpec(memory_space=pl.ANY),
        scratch_shapes=[pltpu.SemaphoreType.DMA(())],
        compiler_params=_cparams(("arbitrary",)),
        name="moe_dispatch",
    )(dest_flat, h2_p, h2_s)


def _moe_kernel(blk_ref, e_ref, lo_ref, hi_ref, first_ref, x_ref, wg_ref, wu_ref, wd_ref, o_ref):
    i = pl.program_id(0)
    lo = lo_ref[i]
    hi = hi_ref[i]

    @pl.when(hi > lo)
    def _():
        x = x_ref[...].astype(bf16)
        a = _dot(x, wg_ref[...].astype(bf16))
        u = _dot(x, wu_ref[...].astype(bf16))
        y = _dot((_silu(a) * u).astype(bf16), wd_ref[...].astype(bf16))
        row = lax.broadcasted_iota(i32, y.shape, 0)
        ym = jnp.where((row >= lo) & (row < hi), y, 0.0)

        @pl.when(first_ref[i] == 1)
        def _():
            o_ref[...] = ym

        @pl.when(first_ref[i] == 0)
        def _():
            o_ref[...] = o_ref[...] + ym


def _moe(xs, w_gate, w_up, w_down, item_blk, item_e, item_lo, item_hi, item_first):
    n_items = item_blk.shape[0]
    rows = xs.shape[0]
    grid_spec = pltpu.PrefetchScalarGridSpec(
        num_scalar_prefetch=5,
        grid=(n_items,),
        in_specs=[pl.BlockSpec((MOE_ROWS, D_MODEL), lambda i, blk, e, lo, hi, fi: (blk[i], 0)),
                  pl.BlockSpec((None, D_MODEL, D_FF), lambda i, blk, e, lo, hi, fi: (e[i], 0, 0)),
                  pl.BlockSpec((None, D_MODEL, D_FF), lambda i, blk, e, lo, hi, fi: (e[i], 0, 0)),
                  pl.BlockSpec((None, D_FF, D_MODEL), lambda i, blk, e, lo, hi, fi: (e[i], 0, 0))],
        out_specs=pl.BlockSpec((MOE_ROWS, D_MODEL), lambda i, blk, e, lo, hi, fi: (blk[i], 0)),
    )
    return pl.pallas_call(
        _moe_kernel,
        out_shape=jax.ShapeDtypeStruct((rows, D_MODEL), f32),
        grid_spec=grid_spec,
        compiler_params=_cparams(("arbitrary",)),
        name="moe_experts",
    )(item_blk, item_e, item_lo, item_hi, item_first, xs, w_gate, w_up, w_down)


def _combine_kernel(dest_ref, x1p_ref, mwp_ref, x1s_ref, mws_ref, fnw_ref, ys_ref, yp_ref, ysm_ref,
                    g0_ref, g1_ref, sem, *, np_tiles):
    i = pl.program_id(0)
    tm = x1p_ref.shape[0]

    def make(r):
        return [pltpu.make_async_copy(ys_ref.at[pl.ds(dest_ref[2 * r], 1)], g0_ref.at[pl.ds(r, 1)], sem),
                pltpu.make_async_copy(ys_ref.at[pl.ds(dest_ref[2 * r + 1], 1)], g1_ref.at[pl.ds(r, 1)], sem)]

    _row_copies(tm, make, lambda cp: cp.start())
    _row_copies(tm, make, lambda cp: cp.wait())

    def finish(x1_ref, mw_ref, out_ref):
        mw = mw_ref[...]
        x2 = x1_ref[...] + (g0_ref[...] * mw[:, 0:1] + g1_ref[...] * mw[:, 1:2])
        ms = jnp.mean(x2 * x2, axis=-1, keepdims=True)
        out_ref[...] = x2 * lax.rsqrt(ms + EPS) * fnw_ref[...]

    @pl.when(i < np_tiles)
    def _():
        finish(x1p_ref, mwp_ref, yp_ref)

    @pl.when(i >= np_tiles)
    def _():
        finish(x1s_ref, mws_ref, ysm_ref)


def _combine(x1_p, mw_p, x1_s, mw_s, fnw_row, ys, dest_flat):
    tm = MOE_ROWS
    n_p, n_s = x1_p.shape[0], x1_s.shape[0]
    assert n_p % tm == 0 and n_s == tm
    np_tiles = n_p // tm
    ptile = lambda width: pl.BlockSpec((tm, width), lambda i: (jnp.minimum(i, np_tiles - 1), 0))
    stile = lambda width: pl.BlockSpec((tm, width), lambda i: (0, 0))
    return pl.pallas_call(
        functools.partial(_combine_kernel, np_tiles=np_tiles),
        out_shape=(jax.ShapeDtypeStruct((n_p, D_MODEL), f32),
                   jax.ShapeDtypeStruct((n_s, D_MODEL), f32)),
        grid=(np_tiles + 1,),
        in_specs=[pl.BlockSpec((2 * tm,), lambda i: (i,), memory_space=pltpu.SMEM),
                  ptile(D_MODEL), ptile(LANE), stile(D_MODEL), stile(LANE),
                  pl.BlockSpec((1, D_MODEL), lambda i: (0, 0)),
                  pl.BlockSpec(memory_space=pl.ANY)],
        out_specs=(ptile(D_MODEL), stile(D_MODEL)),
        scratch_shapes=[pltpu.VMEM((tm, D_MODEL), f32), pltpu.VMEM((tm, D_MODEL), f32),
                        pltpu.SemaphoreType.DMA(())],
        compiler_params=_cparams(("arbitrary",)),
        name="moe_combine",
    )(dest_flat, x1_p, mw_p, x1_s, mw_s, fnw_row, ys)


def _work_items(counts, n_rows):
    nblk = n_rows // MOE_ROWS
    n_items = nblk + N_EXPERTS - 1
    ends = jnp.cumsum(counts)
    starts = ends - counts
    first_blk = starts // MOE_ROWS
    last_blk = jnp.maximum(ends - 1, 0) // MOE_ROWS
    nvis = jnp.where(counts > 0, last_blk - first_blk + 1, 0)
    vis_end = jnp.cumsum(nvis)
    vis_start = vis_end - nvis
    total = vis_end[-1]
    idx = jnp.arange(n_items, dtype=i32)
    e = jnp.minimum(jnp.searchsorted(vis_end, idx, side='right'), N_EXPERTS - 1).astype(i32)
    blk = first_blk[e] + idx - vis_start[e]
    lo = jnp.maximum(starts[e], blk * MOE_ROWS) - blk * MOE_ROWS
    hi = jnp.minimum(ends[e], (blk + 1) * MOE_ROWS) - blk * MOE_ROWS
    valid = idx < total
    e_last = e[jnp.maximum(total - 1, 0)]
    blk = jnp.where(valid, blk, nblk - 1).astype(i32)
    e = jnp.where(valid, e, e_last).astype(i32)
    lo = jnp.where(valid, lo, 0).astype(i32)
    hi = jnp.where(valid, hi, 0).astype(i32)
    prev_blk = jnp.concatenate([jnp.full((1,), -1, i32), blk[:-1]])
    first = (valid & (blk != prev_blk)).astype(i32)
    return starts, blk, e, lo, hi, first


def kernel(x_prompt, x_sample, state_delta, state_qkv_conv, state_short_conv, norm1_w, w_in, conv_a_w, a_log, dt_bias, out_norm_w, w_branch_a, conv_b_w, w_branch_b, w_o, norm2_w, router_group_w, router_group_b, router_expert_w, router_expert_b, w_gate, w_up, w_down, final_norm_w):
    assert norm1_w.shape[0] == 1, "single-layer trunk"
    bp, tp, d = x_prompt.shape
    bs, ts, _ = x_sample.shape
    assert d == D_MODEL and ts == 1
    n_p = bp * tp
    n_s = bs
    n_all = n_p + n_s

    wi = w_in[0]
    w_perm = jnp.concatenate(
        [wi[:, 0:QKV_W], wi[:, 4112:7184], wi[:, 7184:11280], wi[:, 3072:4096], wi[:, 4096:4112],
         jnp.zeros((D_MODEL, PROJ_W - 11280), f32)], axis=1).astype(bf16)
    wa = w_branch_a[0].astype(bf16)
    wb = w_branch_b[0].astype(bf16)
    wo = w_o[0].astype(bf16)
    pad = lambda v: jnp.zeros((1, BA_W), f32).at[0, N_HEADS:2 * N_HEADS].set(v)
    alog_row = pad(a_log[0])
    dtb_row = pad(dt_bias[0])
    onw_row = out_norm_w[0].reshape(1, HEAD)
    cwa = conv_a_w[0]
    cwb = conv_b_w[0]
    rw = jnp.zeros((D_MODEL, LANE), f32)
    rw = rw.at[:, 0:N_EXPERTS].set(router_expert_w[0]).at[:, N_EXPERTS:N_EXPERTS + N_GROUPS].set(router_group_w[0])
    rwh = rw.astype(bf16)
    rwl = (rw - rwh.astype(f32)).astype(bf16)
    rb_row = jnp.zeros((1, LANE), f32)
    rb_row = rb_row.at[0, 0:N_EXPERTS].set(router_expert_b[0]).at[0, N_EXPERTS:N_EXPERTS + N_GROUPS].set(router_group_b[0])
    n2_row = norm2_w[0].reshape(1, D_MODEL)

    xp2 = x_prompt.reshape(n_p, D_MODEL)
    proj_p = _inproj(xp2, norm1_w[0], w_perm)
    o_p, y_p, sd_p, nca_p, ncb_p = _delta_prompt(proj_p.reshape(bp, tp, PROJ_W), cwa, cwb, alog_row, dtb_row,
                                                 onw_row, nb_step=2 if bp % 2 == 0 else 1)
    cnt0 = jnp.zeros((1, LANE), f32)
    x1_p, h2_p, mi_p, mw_p, cnt_p = _mix_route(xp2, o_p.reshape(n_p, QK_W), y_p.reshape(n_p, SC_W), proj_p,
                                               wa, wb, wo, n2_row, rwh, rwl, rb_row, cnt0)

    xs2 = x_sample.reshape(n_s, D_MODEL)
    proj_s = _inproj(xs2, norm1_w[0], w_perm)
    bufa_t = jnp.transpose(state_qkv_conv[0], (1, 0, 2))
    bufb_t = jnp.transpose(state_short_conv[0], (1, 0, 2))
    q_s, k_s, v_s, beta_s, eg_s, y_s, nbufa_t, nbufb_t = _sample_prep(proj_s, bufa_t, bufb_t, cwa, cwb,
                                                                      alog_row, dtb_row)
    h3 = lambda a: a.reshape(n_s, N_HEADS, HEAD)
    z_s = proj_s[:, COL_Z:COL_Z + QK_W]
    sd_s, o_s = _sample_step(state_delta[0], h3(q_s), h3(k_s), h3(v_s), h3(beta_s), h3(eg_s), h3(z_s), onw_row)
    o_s2 = o_s.reshape(n_s, QK_W).astype(bf16)
    x1_s, h2_s, mi_s, mw_s, cnt = _mix_route(xs2, o_s2, y_s, proj_s, wa, wb, wo, n2_row, rwh, rwl, rb_row, cnt_p)

    counts = cnt[0, 0:N_EXPERTS].astype(i32)
    starts, item_blk, item_e, item_lo, item_hi, item_first = _work_items(counts, 2 * n_all)
    mi = jnp.concatenate([mi_p[:, 0:4], mi_s[:, 0:4]], axis=0)
    dest = (starts[mi[:, 0:2]] + mi[:, 2:4]).reshape(2 * n_all).astype(i32)
    xs_sorted = _dispatch(h2_p, h2_s, dest)
    ys = _moe(xs_sorted, w_gate[0], w_up[0], w_down[0], item_blk, item_e, item_lo, item_hi, item_first)
    y_prompt, y_sample = _combine(x1_p, mw_p, x1_s, mw_s, final_norm_w.reshape(1, D_MODEL), ys, dest)

    return (y_prompt.reshape(bp, tp, D_MODEL),
            y_sample.reshape(bs, ts, D_MODEL),
            sd_p[None],
            nca_p[None],
            ncb_p[None],
            sd_s[None],
            jnp.transpose(nbufa_t, (1, 0, 2))[None],
            jnp.transpose(nbufb_t, (1, 0, 2))[None])
```

```python
import functools

import jax
import jax.numpy as jnp
from jax import lax
from jax.experimental import pallas as pl
from jax.experimental.pallas import tpu as pltpu

f32 = jnp.float32
bf16 = jnp.bfloat16
i32 = jnp.int32

EPS = 1e-6
LANE = 128
D_MODEL = 2048
N_HEADS = 8
HEAD = 128
QK_W = N_HEADS * HEAD
QKV_W = 3 * QK_W
SC_W = 1024
CONV_A = 4
CONV_B = 3
CHUNK = 64
GROUP_HEADS = 4
N_EXPERTS = 64
N_GROUPS = 8
EXPERTS_PER_GROUP = 8
D_FF = 512
MOE_ROWS = 128

COL_QKV = 0
COL_BCX = 3072
COL_GA = 6144
COL_GB = 8192
COL_Z = 10240
COL_BA = 11264
BA_W = 256
PROJ_W = 11520
PROJ_TN = 1280

VMEM_LIMIT = 56 * 1024 * 1024


def _dot(a, b):
    return jnp.dot(a, b, preferred_element_type=f32)


def _dot_nt(a, b):
    return lax.dot_general(a, b, (((1,), (1,)), ((), ())), preferred_element_type=f32)


def _split(x, n):
    parts = []
    r = x
    for i in range(n):
        p = r.astype(bf16)
        parts.append(p)
        if i + 1 < n:
            r = r - p.astype(f32)
    return parts


def _dot_lsplit(x, m, n=3):
    rows = x.shape[0]
    d = _dot(jnp.concatenate(_split(x, n), axis=0), m)
    acc = d[0:rows]
    for i in range(1, n):
        acc = acc + d[i * rows:(i + 1) * rows]
    return acc


def _dot_rsplit(m, x, n=3):
    cols = x.shape[1]
    d = _dot(m, jnp.concatenate(_split(x, n), axis=1))
    acc = d[:, 0:cols]
    for i in range(1, n):
        acc = acc + d[:, i * cols:(i + 1) * cols]
    return acc


def _silu(x):
    return x * jax.nn.sigmoid(x)


def _softplus(x):
    return jnp.maximum(x, 0.0) + jnp.log(1.0 + jnp.exp(-jnp.abs(x)))


def _cparams(sem):
    return pltpu.CompilerParams(dimension_semantics=sem, vmem_limit_bytes=VMEM_LIMIT)


def _inproj_kernel(x_ref, nw_ref, w_ref, o_ref, h_ref, *, rows):
    @pl.when(pl.program_id(1) == 0)
    def _():
        def body(r, c):
            sl = pl.ds(pl.multiple_of(r * rows, rows), rows)
            x = x_ref[sl, :]
            ms = jnp.mean(x * x, axis=-1, keepdims=True)
            h_ref[sl, :] = (x * lax.rsqrt(ms + EPS) * nw_ref[...]).astype(bf16)
            return c
        lax.fori_loop(0, x_ref.shape[0] // rows, body, 0)

    o_ref[...] = _dot(h_ref[...], w_ref[...])


def _inproj(x2d, norm_w, w_bf16):
    n = x2d.shape[0]
    tm = min(1024, n)
    assert n % tm == 0 and PROJ_W % PROJ_TN == 0
    return pl.pallas_call(
        functools.partial(_inproj_kernel, rows=min(128, tm)),
        out_shape=jax.ShapeDtypeStruct((n, PROJ_W), f32),
        grid=(n // tm, PROJ_W // PROJ_TN),
        in_specs=[pl.BlockSpec((tm, D_MODEL), lambda i, j: (i, 0)),
                  pl.BlockSpec((1, D_MODEL), lambda i, j: (0, 0)),
                  pl.BlockSpec((D_MODEL, PROJ_TN), lambda i, j: (0, j))],
        out_specs=pl.BlockSpec((tm, PROJ_TN), lambda i, j: (i, j)),
        scratch_shapes=[pltpu.VMEM((tm, D_MODEL), bf16)],
        compiler_params=_cparams(("arbitrary", "arbitrary")),
        name="inproj",
    )(x2d, norm_w.reshape(1, D_MODEL), w_bf16)


W_IN_COLS = 11280
WPREP_TN = 1024
WPREP_SHIFT = 16


def _wprep_kernel(a_ref, b_ref, o_ref, *, rows):
    j = pl.program_id(0)

    def for_rows(fn):
        def body(r, c):
            sl = pl.ds(pl.multiple_of(r * rows, rows), rows)
            o_ref[sl, :] = fn(a_ref[sl, :], b_ref[sl, :]).astype(bf16)
            return c
        lax.fori_loop(0, a_ref.shape[0] // rows, body, 0)

    @pl.when((j < 3) | (j == 10))
    def _():
        for_rows(lambda a, b: a)

    @pl.when((j >= 3) & (j < 10))
    def _():
        width = WPREP_TN + LANE
        for_rows(lambda a, b: pltpu.roll(jnp.concatenate([a, b], axis=1), width - WPREP_SHIFT, axis=1)[:, 0:WPREP_TN])

    @pl.when(j == 11)
    def _():
        lane = lax.broadcasted_iota(i32, (rows, WPREP_TN), 1)
        for_rows(lambda a, b: jnp.where(lane < 2 * N_HEADS, a, 0.0))


def _wprep(w_in2d):
    assert w_in2d.shape == (D_MODEL, W_IN_COLS)
    n_blk = pl.cdiv(PROJ_W, WPREP_TN)
    last_b = W_IN_COLS // LANE

    def a_map(j):
        return (0, jnp.where(j < 3, j, jnp.where(j < 10, j + 1, jnp.where(j == 10, 3, 4))))

    def b_map(j):
        return (0, jnp.minimum((WPREP_TN // LANE) * (j + 2), last_b))

    return pl.pallas_call(
        functools.partial(_wprep_kernel, rows=256),
        out_shape=jax.ShapeDtypeStruct((D_MODEL, PROJ_W), bf16),
        grid=(n_blk,),
        in_specs=[pl.BlockSpec((D_MODEL, WPREP_TN), a_map),
                  pl.BlockSpec((D_MODEL, LANE), b_map)],
        out_specs=pl.BlockSpec((D_MODEL, WPREP_TN), lambda j: (0, j)),
        compiler_params=_cparams(("arbitrary",)),
        name="wprep",
    )(w_in2d, w_in2d)


def _head_l2norm(a, scale):
    outs = []
    for h in range(N_HEADS):
        ah = a[:, h * HEAD:(h + 1) * HEAD]
        ss = jnp.sum(ah * ah, axis=-1, keepdims=True)
        n = ah * lax.rsqrt(ss + EPS)
        outs.append(n * scale if scale != 1.0 else n)
    return outs


def _delta_prompt_kernel(qkv_ref, bcx_ref, z_ref, ba_ref, cwa_ref, cwb_ref, alog_ref, dtb_ref, onw_ref,
                         e64_ref,
                         o_ref, y_ref, snew_ref, nca_ref, ncb_ref,
                         s_ref, xa_ref, xb_ref, *, nb_step):
    C = CHUNK
    G = GROUP_HEADS
    R = G * C
    t = pl.program_id(1)
    nt = pl.num_programs(1)

    @pl.when(t == 0)
    def _():
        s_ref[...] = jnp.zeros(s_ref.shape, f32)
        xa_ref[:, 0:8, :] = jnp.zeros((nb_step, 8, QKV_W), f32)
        xb_ref[:, 0:8, :] = jnp.zeros((nb_step, 8, SC_W), f32)

    rr = lax.broadcasted_iota(i32, (R, R), 0)
    cc = lax.broadcasted_iota(i32, (R, R), 1)
    same = (rr >> 6) == (cc >> 6)
    incl = same & (rr >= cc)
    strict = same & (rr > cc)
    eye = jnp.where(rr == cc, 1.0, 0.0).astype(f32)
    r2 = lax.broadcasted_iota(i32, (R, G * HEAD), 0)
    c2 = lax.broadcasted_iota(i32, (R, G * HEAD), 1)
    bdmask = (r2 >> 6) == (c2 >> 7)
    r3 = lax.broadcasted_iota(i32, (C, C), 0)
    c3 = lax.broadcasted_iota(i32, (C, C), 1)
    ltri = jnp.where(r3 >= c3, 1.0, 0.0).astype(bf16)
    r4 = lax.broadcasted_iota(i32, (C, R), 0)
    c4 = lax.broadcasted_iota(i32, (C, R), 1)
    ident_t = r4 == (c4 & (C - 1))
    ones8 = jnp.ones((8, C), bf16)

    nbs = range(nb_step)
    units = [(nb, g) for nb in nbs for g in range(N_HEADS // G)]
    heads = lambda g: range(g * G, (g + 1) * G)

    for nb in nbs:
        xa_ref[nb, 8:8 + C, :] = qkv_ref[nb]

    def conv_sec(nb, lo):
        hi = lo + QK_W
        xe = xa_ref[nb, :, lo:hi]
        acc = pltpu.roll(xe, 3, axis=0)[8:8 + C] * cwa_ref[0:1, lo:hi]
        acc = acc + pltpu.roll(xe, 2, axis=0)[8:8 + C] * cwa_ref[1:2, lo:hi]
        acc = acc + pltpu.roll(xe, 1, axis=0)[8:8 + C] * cwa_ref[2:3, lo:hi]
        acc = acc + xe[8:8 + C] * cwa_ref[3:4, lo:hi]
        return _silu(acc)

    kn = [_head_l2norm(conv_sec(nb, QK_W), 1.0) for nb in nbs]
    qn = [_head_l2norm(conv_sec(nb, 0), HEAD ** -0.5) for nb in nbs]
    vv = [conv_sec(nb, 2 * QK_W) for nb in nbs]
    for nb in nbs:
        xa_ref[nb, 0:8, :] = xa_ref[nb, C:C + 8, :]

    bts = [ba_ref[nb] for nb in nbs]
    beta_all = [jax.nn.sigmoid(bt) for bt in bts]
    g_all = [-(jnp.exp(alog_ref[...]) * _softplus(bt + dtb_ref[...])) for bt in bts]
    gc_small = [_dot_rsplit(ltri, ga) for ga in g_all]
    gl_small = [gc[C - 1:C, :] for gc in gc_small]

    k_st, q_st, kb, vb, kbg, qd, kd, gc_col = ({} for _ in range(8))
    for u in units:
        nb, g = u
        hs = heads(g)
        k_st[u] = jnp.concatenate([kn[nb][h] for h in hs], axis=0)
        q_st[u] = jnp.concatenate([qn[nb][h] for h in hs], axis=0)
        v_st = jnp.concatenate([vv[nb][:, h * HEAD:(h + 1) * HEAD] for h in hs], axis=0)
        beta_col = jnp.concatenate([beta_all[nb][:, h:h + 1] for h in hs], axis=0)
        gc_col[u] = jnp.concatenate([gc_small[nb][:, 8 + h:9 + h] for h in hs], axis=0)
        gl_col = jnp.concatenate(
            [jnp.broadcast_to(gl_small[nb][:, 8 + h:9 + h], (C, 1)) for h in hs], axis=0)
        kb[u] = k_st[u] * beta_col
        vb[u] = v_st * beta_col
        egc = jnp.exp(gc_col[u])
        kbg[u] = kb[u] * egc
        qd[u] = q_st[u] * egc
        kd[u] = k_st[u] * jnp.exp(gl_col - gc_col[u])

    gx = {u: _dot_lsplit(gc_small[u[0]], e64_ref[u[1]]) for u in units}
    crow = {u: _dot_rsplit(ones8, jnp.where(ident_t, gx[u], 0.0))[0:1, :] for u in units}
    a = {u: _dot_nt(jnp.concatenate([kb[u], q_st[u]], axis=0).astype(bf16), k_st[u].astype(bf16))
         for u in units}
    dec = {u: jnp.where(incl, jnp.exp(jnp.where(incl, gc_col[u] - crow[u], 0.0)), 0.0) for u in units}
    nm = {u: jnp.where(strict, -(a[u][0:R] * dec[u]), 0.0) for u in units}
    qkm = {u: a[u][R:2 * R] * dec[u] for u in units}

    p = {u: eye + nm[u] for u in units}
    nk = {}
    for u in units:
        nmb = nm[u].astype(bf16)
        nk[u] = _dot(nmb, nmb)
    for _ in range(4):
        for u in units:
            x = _dot(jnp.concatenate([p[u], nk[u]], axis=0).astype(bf16), nk[u].astype(bf16))
            p[u] = p[u] + x[0:R]
            nk[u] = x[R:2 * R]
    for u in units:
        p[u] = p[u] + _dot(p[u].astype(bf16), nk[u].astype(bf16))
    uw = {u: _dot(p[u].astype(bf16), jnp.concatenate([vb[u], kbg[u]], axis=1).astype(bf16)) for u in units}

    ws = {}
    for u in units:
        nb, g = u
        for j, h in enumerate(heads(g)):
            sh = s_ref[nb, :, h * HEAD:(h + 1) * HEAD]
            lhs = jnp.concatenate([uw[u][j * C:(j + 1) * C, HEAD:2 * HEAD], qd[u][j * C:(j + 1) * C]], axis=0)
            ws[u, j] = _dot(lhs.astype(bf16), sh.astype(bf16))
    o_heads = {}
    for u in units:
        nb, g = u
        vnew_st = jnp.concatenate([uw[u][j * C:(j + 1) * C, 0:HEAD] - ws[u, j][0:C] for j in range(G)], axis=0)
        o_st = (jnp.concatenate([ws[u, j][C:2 * C] for j in range(G)], axis=0)
                + _dot(qkm[u].astype(bf16), vnew_st.astype(bf16)))
        vbd = jnp.where(bdmask, jnp.concatenate([vnew_st] * G, axis=1), 0.0)
        lo = g * G * HEAD
        hi = lo + G * HEAD
        gl_row = jnp.concatenate(
            [jnp.broadcast_to(jnp.exp(gl_small[nb][:, 8 + h:9 + h]), (1, HEAD)) for h in heads(g)], axis=1)
        s_ref[nb, :, lo:hi] = s_ref[nb, :, lo:hi] * gl_row + _dot(kd[u].T.astype(bf16), vbd.astype(bf16))
        for j, h in enumerate(heads(g)):
            o_heads[nb, h] = o_st[j * C:(j + 1) * C]

    for nb in nbs:
        zt = z_ref[nb]
        for h in range(N_HEADS):
            oh = o_heads[nb, h]
            ms = jnp.mean(oh * oh, axis=-1, keepdims=True)
            zh = zt[:, h * HEAD:(h + 1) * HEAD]
            on = oh * lax.rsqrt(ms + EPS) * onw_ref[...] * _silu(zh)
            o_ref[nb, :, h * HEAD:(h + 1) * HEAD] = on.astype(bf16)

    for nb in nbs:
        bcx = bcx_ref[nb]
        cx = bcx[:, SC_W:2 * SC_W] * bcx[:, 2 * SC_W:3 * SC_W]
        xb_ref[nb, 8:8 + C, :] = cx
        ce = xb_ref[nb]
        cv = pltpu.roll(ce, 2, axis=0)[8:8 + C] * cwb_ref[0:1, :]
        cv = cv + pltpu.roll(ce, 1, axis=0)[8:8 + C] * cwb_ref[1:2, :]
        cv = cv + cx * cwb_ref[2:3, :]
        y_ref[nb] = (bcx[:, 0:SC_W] * cv).astype(bf16)
        xb_ref[nb, 0:8, :] = xb_ref[nb, C:C + 8, :]

    @pl.when(t == nt - 1)
    def _():
        for nb in range(nb_step):
            for h in range(N_HEADS):
                snew_ref[nb, h] = s_ref[nb, :, h * HEAD:(h + 1) * HEAD]
            nca_ref[nb] = xa_ref[nb, 5:8, :]
            ncb_ref[nb] = xb_ref[nb, 6:8, :]


def _expand_consts():
    lane = jnp.arange(BA_W)[:, None]
    col = jnp.arange(QK_W)[None, :]
    eb = (lane == (col >> 7)).astype(bf16)
    eg = (lane == (8 + (col >> 7))).astype(bf16)
    col64 = jnp.arange(GROUP_HEADS * CHUNK)[None, :]
    e64 = jnp.stack([(lane == (8 + g * GROUP_HEADS + (col64 >> 6))).astype(bf16)
                     for g in range(N_HEADS // GROUP_HEADS)], axis=0)
    return eb, eg, e64


def _delta_prompt(proj3, cwa, cwb, alog_row, dtb_row, onw_row, nb_step):
    b, t, _ = proj3.shape
    assert t % CHUNK == 0 and b % nb_step == 0
    _, _, e64 = _expand_consts()
    c = CHUNK
    const2 = lambda bi, ti: (0, 0)
    outs = pl.pallas_call(
        functools.partial(_delta_prompt_kernel, nb_step=nb_step),
        out_shape=(jax.ShapeDtypeStruct((b, t, QK_W), bf16),
                   jax.ShapeDtypeStruct((b, t, SC_W), bf16),
                   jax.ShapeDtypeStruct((b, N_HEADS, HEAD, HEAD), f32),
                   jax.ShapeDtypeStruct((b, CONV_A - 1, QKV_W), f32),
                   jax.ShapeDtypeStruct((b, CONV_B - 1, SC_W), f32)),
        grid=(b // nb_step, t // c),
        in_specs=[pl.BlockSpec((nb_step, c, QKV_W), lambda bi, ti: (bi, ti, COL_QKV // QKV_W)),
                  pl.BlockSpec((nb_step, c, QKV_W), lambda bi, ti: (bi, ti, COL_BCX // QKV_W)),
                  pl.BlockSpec((nb_step, c, QK_W), lambda bi, ti: (bi, ti, COL_Z // QK_W)),
                  pl.BlockSpec((nb_step, c, BA_W), lambda bi, ti: (bi, ti, COL_BA // BA_W)),
                  pl.BlockSpec((CONV_A, QKV_W), const2),
                  pl.BlockSpec((CONV_B, SC_W), const2),
                  pl.BlockSpec((1, BA_W), const2),
                  pl.BlockSpec((1, BA_W), const2),
                  pl.BlockSpec((1, HEAD), const2),
                  pl.BlockSpec((N_HEADS // GROUP_HEADS, BA_W, GROUP_HEADS * CHUNK), lambda bi, ti: (0, 0, 0))],
        out_specs=(pl.BlockSpec((nb_step, c, QK_W), lambda bi, ti: (bi, ti, 0)),
                   pl.BlockSpec((nb_step, c, SC_W), lambda bi, ti: (bi, ti, 0)),
                   pl.BlockSpec((nb_step, N_HEADS, HEAD, HEAD), lambda bi, ti: (bi, 0, 0, 0)),
                   pl.BlockSpec((nb_step, CONV_A - 1, QKV_W), lambda bi, ti: (bi, 0, 0)),
                   pl.BlockSpec((nb_step, CONV_B - 1, SC_W), lambda bi, ti: (bi, 0, 0))),
        scratch_shapes=[pltpu.VMEM((nb_step, HEAD, QK_W), f32),
                        pltpu.VMEM((nb_step, 8 + c, QKV_W), f32),
                        pltpu.VMEM((nb_step, 8 + c, SC_W), f32)],
        compiler_params=_cparams(("arbitrary", "arbitrary")),
        name="delta_prompt",
    )(proj3, proj3, proj3, proj3, cwa, cwb, alog_row, dtb_row, onw_row, e64)
    return outs


def _sample_prep_kernel(p_ref, bufa_ref, bufb_ref, cwa_ref, cwb_ref, alog_ref, dtb_ref, eb_ref, eg_ref,
                        q_ref, k_ref, v_ref, beta_ref, eg_out_ref, y_ref, nbufa_ref, nbufb_ref):
    def conv_sec(lo):
        hi = lo + QK_W
        raw = p_ref[:, COL_QKV + lo:COL_QKV + hi]
        acc = bufa_ref[0, :, lo:hi] * cwa_ref[0:1, lo:hi]
        acc = acc + bufa_ref[1, :, lo:hi] * cwa_ref[1:2, lo:hi]
        acc = acc + bufa_ref[2, :, lo:hi] * cwa_ref[2:3, lo:hi]
        acc = acc + raw * cwa_ref[3:4, lo:hi]
        nbufa_ref[0, :, lo:hi] = bufa_ref[1, :, lo:hi]
        nbufa_ref[1, :, lo:hi] = bufa_ref[2, :, lo:hi]
        nbufa_ref[2, :, lo:hi] = raw
        return _silu(acc)

    qn = _head_l2norm(conv_sec(0), HEAD ** -0.5)
    kn = _head_l2norm(conv_sec(QK_W), 1.0)
    for h in range(N_HEADS):
        q_ref[:, h * HEAD:(h + 1) * HEAD] = qn[h]
        k_ref[:, h * HEAD:(h + 1) * HEAD] = kn[h]
    v_ref[...] = conv_sec(2 * QK_W)

    bt = p_ref[:, COL_BA:COL_BA + BA_W]
    beta_all = jax.nn.sigmoid(bt)
    g_all = -(jnp.exp(alog_ref[...]) * _softplus(bt + dtb_ref[...]))
    beta_ref[...] = _dot_lsplit(beta_all, eb_ref[...])
    eg_out_ref[...] = jnp.exp(_dot_lsplit(g_all, eg_ref[...]))

    bg = p_ref[:, COL_BCX:COL_BCX + SC_W]
    cx = p_ref[:, COL_BCX + SC_W:COL_BCX + 2 * SC_W] * p_ref[:, COL_BCX + 2 * SC_W:COL_BCX + 3 * SC_W]
    cv = bufb_ref[0] * cwb_ref[0:1, :]
    cv = cv + bufb_ref[1] * cwb_ref[1:2, :]
    cv = cv + cx * cwb_ref[2:3, :]
    y_ref[...] = (bg * cv).astype(bf16)
    nbufb_ref[0] = bufb_ref[1]
    nbufb_ref[1] = cx


def _sample_prep(proj_s, bufa_t, bufb_t, cwa, cwb, alog_row, dtb_row):
    n = proj_s.shape[0]
    eb, eg, _ = _expand_consts()
    row = jax.ShapeDtypeStruct((n, QK_W), f32)
    return pl.pallas_call(
        _sample_prep_kernel,
        out_shape=(row, row, row, row, row,
                   jax.ShapeDtypeStruct((n, SC_W), bf16),
                   jax.ShapeDtypeStruct((CONV_A - 1, n, QKV_W), f32),
                   jax.ShapeDtypeStruct((CONV_B - 1, n, SC_W), f32)),
        compiler_params=pltpu.CompilerParams(vmem_limit_bytes=VMEM_LIMIT),
        name="sample_prep",
    )(proj_s, bufa_t, bufb_t, cwa, cwb, alog_row, dtb_row, eb, eg)


def _sample_step_kernel(s_ref, q_ref, k_ref, v_ref, beta_ref, eg_ref, z_ref, onw_ref,
                        snew_ref, o_ref, *, bb):
    w = N_HEADS * HEAD
    r8 = lax.broadcasted_iota(i32, (N_HEADS, w), 0)
    c8 = lax.broadcasted_iota(i32, (N_HEADS, w), 1)
    mask8 = r8 == (c8 >> 7)
    zpad_k = jnp.zeros((HEAD - N_HEADS, HEAD), f32)
    zpad_d = jnp.zeros((HEAD - N_HEADS, w), f32)
    for b in range(bb):
        s_all = jnp.concatenate([s_ref[b, h] for h in range(N_HEADS)], axis=1)
        eg8 = eg_ref[b]
        eg_row = jnp.concatenate([eg8[h:h + 1, :] for h in range(N_HEADS)], axis=1)
        s_dec = s_all * eg_row
        k8 = k_ref[b]
        x = _dot(k8.astype(bf16), s_dec.astype(bf16))
        v_t = jnp.concatenate([v_ref[b]] * N_HEADS, axis=1)
        b_t = jnp.concatenate([beta_ref[b]] * N_HEADS, axis=1)
        d_bd = jnp.where(mask8, (v_t - x) * b_t, 0.0)
        kt = jnp.concatenate([k8, zpad_k], axis=0).T
        d_pad = jnp.concatenate([d_bd, zpad_d], axis=0)
        k_hi, k_lo = _split(kt, 2)
        d_hi, d_lo = _split(d_pad, 2)
        s_new = s_dec + (_dot(k_hi, d_hi) + _dot(k_hi, d_lo) + _dot(k_lo, d_hi))
        yv = jnp.where(mask8, _dot(q_ref[b].astype(bf16), s_new.astype(bf16)), 0.0)
        o8 = yv[:, 0:HEAD]
        for j in range(1, N_HEADS):
            o8 = o8 + yv[:, j * HEAD:(j + 1) * HEAD]
        ms = jnp.mean(o8 * o8, axis=-1, keepdims=True)
        o_ref[b] = o8 * lax.rsqrt(ms + EPS) * onw_ref[...] * _silu(z_ref[b])
        for h in range(N_HEADS):
            snew_ref[b, h] = s_new[:, h * HEAD:(h + 1) * HEAD]


def _sample_step(state, q, k, v, beta, eg, z, onw_row, bb=4):
    n = state.shape[0]
    assert n % bb == 0
    hspec = pl.BlockSpec((bb, N_HEADS, HEAD), lambda i: (i, 0, 0))
    sspec = pl.BlockSpec((bb, N_HEADS, HEAD, HEAD), lambda i: (i, 0, 0, 0))
    return pl.pallas_call(
        functools.partial(_sample_step_kernel, bb=bb),
        out_shape=(jax.ShapeDtypeStruct(state.shape, f32),
                   jax.ShapeDtypeStruct((n, N_HEADS, HEAD), f32)),
        grid=(n // bb,),
        in_specs=[sspec, hspec, hspec, hspec, hspec, hspec, hspec, pl.BlockSpec((1, HEAD), lambda i: (0, 0))],
        out_specs=(sspec, hspec),
        compiler_params=_cparams(("arbitrary",)),
        name="sample_step",
    )(state, q, k, v, beta, eg, z, onw_row)


def _mix_route_kernel(x_ref, o_ref, y_ref, ga_ref, gb_ref, wa_ref, wb_ref, wo_ref, n2_ref,
                      rwh_ref, rwl_ref, rb_ref, cnt_in_ref, x1_ref, h2_ref, mi_ref, mw_ref, cnt_ref):
    i = pl.program_id(0)
    tm = x_ref.shape[0]

    @pl.when(i == 0)
    def _():
        cnt_ref[...] = cnt_in_ref[...]

    oa = _dot(o_ref[...], wa_ref[...])
    ob = _dot(y_ref[...], wb_ref[...])
    merged = jax.nn.sigmoid(ga_ref[...]) * oa + jax.nn.sigmoid(gb_ref[...]) * ob
    x1 = x_ref[...] + _dot(merged.astype(bf16), wo_ref[...])
    x1_ref[...] = x1
    ms = jnp.mean(x1 * x1, axis=-1, keepdims=True)
    h2 = x1 * lax.rsqrt(ms + EPS) * n2_ref[...]
    h2_ref[...] = h2

    h_hi, h_lo = _split(h2, 2)
    logits = _dot(h_hi, rwh_ref[...]) + _dot(h_hi, rwl_ref[...]) + _dot(h_lo, rwh_ref[...]) + rb_ref[...]

    lane = lax.broadcasted_iota(i32, (tm, LANE), 1)
    lanef = lane.astype(f32)
    neg = jnp.float32(-jnp.inf)
    big = jnp.float32(1e9)
    gmask = (lane >= N_EXPERTS) & (lane < N_EXPERTS + N_GROUPS)
    gl = jnp.where(gmask, logits, neg)
    gmax = jnp.max(gl, axis=-1, keepdims=True)
    gidx = jnp.min(jnp.where(gl == gmax, lanef - N_EXPERTS, big), axis=-1, keepdims=True)
    gsum = jnp.sum(jnp.where(gmask, jnp.exp(gl - gmax), 0.0), axis=-1, keepdims=True)
    gprob = 1.0 / gsum

    emask = (lane < N_EXPERTS) & ((lane >> 3).astype(f32) == gidx)
    el = jnp.where(emask, logits, neg)
    emax = jnp.max(el, axis=-1, keepdims=True)
    pe = jnp.where(emask, jnp.exp(el - emax), 0.0)
    eprob = pe / jnp.sum(pe, axis=-1, keepdims=True)
    p1m = jnp.where(emask, eprob, -1.0)
    m1 = jnp.max(p1m, axis=-1, keepdims=True)
    i1 = jnp.min(jnp.where(p1m == m1, lanef, big), axis=-1, keepdims=True)
    p2m = jnp.where(lanef == i1, -1.0, p1m)
    m2 = jnp.max(p2m, axis=-1, keepdims=True)
    i2 = jnp.min(jnp.where(p2m == m2, lanef, big), axis=-1, keepdims=True)
    tot = m1 + m2
    c1 = m1 / tot * gprob
    c2 = m2 / tot * gprob

    oh1 = jnp.where(lanef == i1, 1.0, 0.0)
    oh2 = jnp.where(lanef == i2, 1.0, 0.0)
    ohs = oh1 + oh2
    rt = lax.broadcasted_iota(i32, (tm, tm), 0)
    ct = lax.broadcasted_iota(i32, (tm, tm), 1)
    lstrict = jnp.where(rt > ct, 1.0, 0.0).astype(bf16)
    cs = _dot(lstrict, ohs.astype(bf16)) + cnt_ref[...]
    rank1 = jnp.sum(cs * oh1, axis=-1, keepdims=True)
    rank2 = jnp.sum(cs * oh2, axis=-1, keepdims=True)
    cnt_ref[...] = cnt_ref[...] + jnp.sum(ohs, axis=0, keepdims=True)

    mi = jnp.where(lane == 0, i1, jnp.where(lane == 1, i2, jnp.where(lane == 2, rank1,
                                                                     jnp.where(lane == 3, rank2, 0.0))))
    mi_ref[...] = mi.astype(i32)
    mw_ref[...] = jnp.where(lane == 0, c1, jnp.where(lane == 1, c2, 0.0))


def _mix_route(x2d, o2d, y2d, proj2d, wa, wb, wo, n2_row, rwh, rwl, rb_row, cnt_in):
    n = x2d.shape[0]
    tm = min(256, n)
    assert n % tm == 0
    tok = lambda width: pl.BlockSpec((tm, width), lambda i: (i, 0))
    full = lambda a: pl.BlockSpec(a.shape, lambda i: (0,) * a.ndim)
    in_specs = [tok(D_MODEL), tok(QK_W), tok(SC_W),
                pl.BlockSpec((tm, D_MODEL), lambda i: (i, COL_GA // D_MODEL)),
                pl.BlockSpec((tm, D_MODEL), lambda i: (i, COL_GB // D_MODEL)),
                full(wa), full(wb), full(wo), full(n2_row), full(rwh), full(rwl), full(rb_row), full(cnt_in)]
    out_shape = (jax.ShapeDtypeStruct((n, D_MODEL), f32),
                 jax.ShapeDtypeStruct((n, D_MODEL), f32),
                 jax.ShapeDtypeStruct((n, LANE), i32),
                 jax.ShapeDtypeStruct((n, LANE), f32),
                 jax.ShapeDtypeStruct((1, LANE), f32))
    out_specs = (tok(D_MODEL), tok(D_MODEL), tok(LANE), tok(LANE),
                 pl.BlockSpec((1, LANE), lambda i: (0, 0)))
    return pl.pallas_call(
        _mix_route_kernel,
        out_shape=out_shape,
        grid=(n // tm,),
        in_specs=in_specs,
        out_specs=out_specs,
        compiler_params=_cparams(("arbitrary",)),
        name="mix_route",
    )(x2d, o2d, y2d, proj2d, proj2d, wa, wb, wo, n2_row, rwh, rwl, rb_row, cnt_in)


MI_W = 4


def _dest_row(mi_ref, starts_ref, r, k):
    return starts_ref[mi_ref[MI_W * r + k]] + mi_ref[MI_W * r + 2 + k]


def _dispatch_kernel(mi_ref, starts_ref, hp_ref, hs_ref, xs_ref, sem, *, np_tiles, tm):
    i = pl.program_id(0)

    def scatter_rows(h_ref, base):
        def body(r, c):
            for k in range(2):
                pltpu.make_async_copy(h_ref.at[pl.ds(base + r, 1)],
                                      xs_ref.at[pl.ds(_dest_row(mi_ref, starts_ref, r, k), 1)],
                                      sem).start(priority=k)
            return c
        lax.fori_loop(0, tm, body, 0)

    def wait_tile():
        def body(r, c):
            for k in range(2):
                pltpu.make_async_copy(hs_ref.at[pl.ds(0, 1)], xs_ref.at[pl.ds(0, 1)], sem).wait()
            return c
        lax.fori_loop(0, tm, body, 0)

    @pl.when(i < np_tiles)
    def _():
        scatter_rows(hp_ref, i * tm)

    @pl.when(i >= np_tiles)
    def _():
        scatter_rows(hs_ref, 0)

    @pl.when(i > 0)
    def _():
        wait_tile()

    @pl.when(i == pl.num_programs(0) - 1)
    def _():
        wait_tile()


def _dispatch(h2_p, h2_s, mi_flat, starts):
    tm = MOE_ROWS
    n_p, n_s = h2_p.shape[0], h2_s.shape[0]
    assert n_p % tm == 0 and n_s == tm
    np_tiles = n_p // tm
    return pl.pallas_call(
        functools.partial(_dispatch_kernel, np_tiles=np_tiles, tm=tm),
        out_shape=jax.ShapeDtypeStruct((2 * (n_p + n_s), D_MODEL), f32),
        grid=(np_tiles + 1,),
        in_specs=[pl.BlockSpec((MI_W * tm,), lambda i: (i,), memory_space=pltpu.SMEM),
                  pl.BlockSpec(memory_space=pltpu.SMEM),
                  pl.BlockSpec(memory_space=pl.ANY),
                  pl.BlockSpec(memory_space=pl.ANY)],
        out_specs=pl.BlockSpec(memory_space=pl.ANY),
        scratch_shapes=[pltpu.SemaphoreType.DMA(())],
        compiler_params=_cparams(("arbitrary",)),
        name="moe_dispatch",
    )(mi_flat, starts, h2_p, h2_s)


def _cast_rows(src_ref, dst_ref, rows=256):
    def body(r, c):
        sl = pl.ds(pl.multiple_of(r * rows, rows), rows)
        dst_ref[sl, :] = src_ref[sl, :].astype(bf16)
        return c
    lax.fori_loop(0, dst_ref.shape[0] // rows, body, 0)


def _moe_kernel(blk_ref, lo_ref, hi_ref, first_ref, newe_ref, slot_ref, pre_ref, init_ref,
                x_ref, wg_hbm, wu_hbm, wd_hbm, o_ref,
                wg_f, wu_f, wd_f, wg_b, wu_b, wd_b, sem):
    i = pl.program_id(0)
    lo = lo_ref[i]
    hi = hi_ref[i]

    def weight_copies(e, slot):
        return [pltpu.make_async_copy(wg_hbm.at[e], wg_f.at[slot], sem.at[slot, 0]),
                pltpu.make_async_copy(wu_hbm.at[e], wu_f.at[slot], sem.at[slot, 1]),
                pltpu.make_async_copy(wd_hbm.at[e], wd_f.at[slot], sem.at[slot, 2])]

    @pl.when(i == 0)
    def _():
        for cp in weight_copies(init_ref[0], 0):
            cp.start()

        @pl.when(init_ref[1] >= 0)
        def _():
            for cp in weight_copies(init_ref[1], 1):
                cp.start()

    @pl.when(newe_ref[i] == 1)
    def _():
        slot = slot_ref[i]
        for cp in weight_copies(0, slot):
            cp.wait()
        _cast_rows(wg_f.at[slot], wg_b)
        _cast_rows(wu_f.at[slot], wu_b)
        _cast_rows(wd_f.at[slot], wd_b)

        @pl.when(pre_ref[i] >= 0)
        def _():
            for cp in weight_copies(pre_ref[i], slot):
                cp.start()

    @pl.when(hi > lo)
    def _():
        x = x_ref[...].astype(bf16)
        a = _dot(x, wg_b[...])
        u = _dot(x, wu_b[...])
        y = _dot((_silu(a) * u).astype(bf16), wd_b[...])
        row = lax.broadcasted_iota(i32, y.shape, 0)
        ym = jnp.where((row >= lo) & (row < hi), y, 0.0)

        @pl.when(first_ref[i] == 1)
        def _():
            o_ref[...] = ym

        @pl.when(first_ref[i] == 0)
        def _():
            o_ref[...] = o_ref[...] + ym


def _moe(xs, w_gate, w_up, w_down, items):
    n_items = items[0].shape[0]
    rows = xs.shape[0]
    n_pref = len(items)
    xmap = lambda i, blk, *_: (blk[i], 0)
    grid_spec = pltpu.PrefetchScalarGridSpec(
        num_scalar_prefetch=n_pref,
        grid=(n_items,),
        in_specs=[pl.BlockSpec((MOE_ROWS, D_MODEL), xmap),
                  pl.BlockSpec(memory_space=pl.ANY),
                  pl.BlockSpec(memory_space=pl.ANY),
                  pl.BlockSpec(memory_space=pl.ANY)],
        out_specs=pl.BlockSpec((MOE_ROWS, D_MODEL), xmap),
        scratch_shapes=[pltpu.VMEM((2, D_MODEL, D_FF), f32), pltpu.VMEM((2, D_MODEL, D_FF), f32),
                        pltpu.VMEM((2, D_FF, D_MODEL), f32),
                        pltpu.VMEM((D_MODEL, D_FF), bf16), pltpu.VMEM((D_MODEL, D_FF), bf16),
                        pltpu.VMEM((D_FF, D_MODEL), bf16),
                        pltpu.SemaphoreType.DMA((2, 3))],
    )
    return pl.pallas_call(
        _moe_kernel,
        out_shape=jax.ShapeDtypeStruct((rows, D_MODEL), f32),
        grid_spec=grid_spec,
        compiler_params=_cparams(("arbitrary",)),
        name="moe_experts",
    )(*items, xs, w_gate, w_up, w_down)


def _combine_kernel(mi_ref, mi_next_ref, starts_ref, x1p_ref, mwp_ref, x1s_ref, mws_ref, fnw_ref, ys_ref,
                    yp_ref, ysm_ref, g_ref, sem, *, np_tiles):
    i = pl.program_id(0)
    n = pl.num_programs(0)
    tm = x1p_ref.shape[0]
    slot = lax.rem(i, 2)

    def gather_rows(m_ref, dst_slot):
        def body(r, c):
            for k in range(2):
                pltpu.make_async_copy(ys_ref.at[pl.ds(_dest_row(m_ref, starts_ref, r, k), 1)],
                                      g_ref.at[dst_slot, k, pl.ds(r, 1)], sem.at[dst_slot]).start(priority=k)
            return c
        lax.fori_loop(0, tm, body, 0)

    @pl.when(i == 0)
    def _():
        gather_rows(mi_ref, 0)

    @pl.when(i + 1 < n)
    def _():
        gather_rows(mi_next_ref, 1 - slot)

    def wait_body(r, c):
        for k in range(2):
            pltpu.make_async_copy(ys_ref.at[pl.ds(0, 1)], g_ref.at[slot, k, pl.ds(r, 1)], sem.at[slot]).wait()
        return c
    lax.fori_loop(0, tm, wait_body, 0)

    def finish(x1_ref, mw_ref, out_ref):
        mw = mw_ref[...]
        x2 = x1_ref[...] + (g_ref[slot, 0] * mw[:, 0:1] + g_ref[slot, 1] * mw[:, 1:2])
        ms = jnp.mean(x2 * x2, axis=-1, keepdims=True)
        out_ref[...] = x2 * lax.rsqrt(ms + EPS) * fnw_ref[...]

    @pl.when(i < np_tiles)
    def _():
        finish(x1p_ref, mwp_ref, yp_ref)

    @pl.when(i >= np_tiles)
    def _():
        finish(x1s_ref, mws_ref, ysm_ref)


def _combine(x1_p, mw_p, x1_s, mw_s, fnw_row, ys, mi_flat, starts):
    tm = MOE_ROWS
    n_p, n_s = x1_p.shape[0], x1_s.shape[0]
    assert n_p % tm == 0 and n_s == tm
    np_tiles = n_p // tm
    ptile = lambda width: pl.BlockSpec((tm, width), lambda i: (jnp.minimum(i, np_tiles - 1), 0))
    stile = lambda width: pl.BlockSpec((tm, width), lambda i: (0, 0))
    return pl.pallas_call(
        functools.partial(_combine_kernel, np_tiles=np_tiles),
        out_shape=(jax.ShapeDtypeStruct((n_p, D_MODEL), f32),
                   jax.ShapeDtypeStruct((n_s, D_MODEL), f32)),
        grid=(np_tiles + 1,),
        in_specs=[pl.BlockSpec((MI_W * tm,), lambda i: (i,), memory_space=pltpu.SMEM),
                  pl.BlockSpec((MI_W * tm,), lambda i: (jnp.minimum(i + 1, np_tiles),), memory_space=pltpu.SMEM),
                  pl.BlockSpec(memory_space=pltpu.SMEM),
                  ptile(D_MODEL), ptile(LANE), stile(D_MODEL), stile(LANE),
                  pl.BlockSpec((1, D_MODEL), lambda i: (0, 0)),
                  pl.BlockSpec(memory_space=pl.ANY)],
        out_specs=(ptile(D_MODEL), stile(D_MODEL)),
        scratch_shapes=[pltpu.VMEM((2, 2, tm, D_MODEL), f32), pltpu.SemaphoreType.DMA((2,))],
        compiler_params=_cparams(("arbitrary",)),
        name="moe_combine",
    )(mi_flat, mi_flat, starts, x1_p, mw_p, x1_s, mw_s, fnw_row, ys)


def _work_items(counts, n_rows):
    nblk = n_rows // MOE_ROWS
    n_items = nblk + N_EXPERTS - 1
    ends = jnp.cumsum(counts)
    starts = ends - counts
    first_blk = starts // MOE_ROWS
    last_blk = jnp.maximum(ends - 1, 0) // MOE_ROWS
    nvis = jnp.where(counts > 0, last_blk - first_blk + 1, 0)
    vis_end = jnp.cumsum(nvis)
    vis_start = vis_end - nvis
    total = vis_end[-1]
    idx = jnp.arange(n_items, dtype=i32)
    e = jnp.minimum(jnp.sum((vis_end[None, :] <= idx[:, None]).astype(i32), axis=1), N_EXPERTS - 1)
    onehot = (e[:, None] == jnp.arange(N_EXPERTS, dtype=i32)[None, :]).astype(i32)
    look = lambda tbl: jnp.sum(onehot * tbl[None, :], axis=1)
    blk = look(first_blk) + idx - look(vis_start)
    lo = jnp.maximum(look(starts), blk * MOE_ROWS) - blk * MOE_ROWS
    hi = jnp.minimum(look(ends), (blk + 1) * MOE_ROWS) - blk * MOE_ROWS
    valid = idx < total
    blk = jnp.where(valid, blk, nblk - 1).astype(i32)
    lo = jnp.where(valid, lo, 0).astype(i32)
    hi = jnp.where(valid, hi, 0).astype(i32)
    prev_blk = jnp.concatenate([jnp.full((1,), -1, i32), blk[:-1]])
    first = (valid & (blk != prev_blk)).astype(i32)
    prev_e = jnp.concatenate([jnp.full((1,), -1, i32), e[:-1]])
    newe = (valid & (e != prev_e)).astype(i32)
    order = jnp.cumsum(newe) - 1
    slot = jnp.where(newe == 1, order % 2, 0).astype(i32)
    cum_act = jnp.cumsum((counts > 0).astype(i32))
    n_uniq = cum_act[-1]
    kk = jnp.arange(N_EXPERTS + 2, dtype=i32)
    uniq_e = jnp.sum((cum_act[None, :] <= kk[:, None]).astype(i32), axis=1)
    uniq_e = jnp.where(kk < n_uniq, uniq_e, -1)
    ahead = jnp.sum((kk[None, :] == (order + 2)[:, None]).astype(i32) * uniq_e[None, :], axis=1)
    pre = jnp.where(newe == 1, ahead, -1).astype(i32)
    init = uniq_e[0:2].astype(i32)
    return starts.astype(i32), (blk, lo, hi, first, newe, slot, pre, init)


def kernel(x_prompt, x_sample, state_delta, state_qkv_conv, state_short_conv, norm1_w, w_in, conv_a_w, a_log, dt_bias, out_norm_w, w_branch_a, conv_b_w, w_branch_b, w_o, norm2_w, router_group_w, router_group_b, router_expert_w, router_expert_b, w_gate, w_up, w_down, final_norm_w):
    assert norm1_w.shape[0] == 1, "single-layer trunk"
    bp, tp, d = x_prompt.shape
    bs, ts, _ = x_sample.shape
    assert d == D_MODEL and ts == 1
    n_p = bp * tp
    n_s = bs
    n_all = n_p + n_s

    w_perm = _wprep(w_in[0])
    wa = w_branch_a[0].astype(bf16)
    wb = w_branch_b[0].astype(bf16)
    wo = w_o[0].astype(bf16)
    pad = lambda v: jnp.zeros((1, BA_W), f32).at[0, N_HEADS:2 * N_HEADS].set(v)
    alog_row = pad(a_log[0])
    dtb_row = pad(dt_bias[0])
    onw_row = out_norm_w[0].reshape(1, HEAD)
    cwa = conv_a_w[0]
    cwb = conv_b_w[0]
    rw = jnp.zeros((D_MODEL, LANE), f32)
    rw = rw.at[:, 0:N_EXPERTS].set(router_expert_w[0]).at[:, N_EXPERTS:N_EXPERTS + N_GROUPS].set(router_group_w[0])
    rwh = rw.astype(bf16)
    rwl = (rw - rwh.astype(f32)).astype(bf16)
    rb_row = jnp.zeros((1, LANE), f32)
    rb_row = rb_row.at[0, 0:N_EXPERTS].set(router_expert_b[0]).at[0, N_EXPERTS:N_EXPERTS + N_GROUPS].set(router_group_b[0])
    n2_row = norm2_w[0].reshape(1, D_MODEL)

    xp2 = x_prompt.reshape(n_p, D_MODEL)
    proj_p = _inproj(xp2, norm1_w[0], w_perm)
    o_p, y_p, sd_p, nca_p, ncb_p = _delta_prompt(proj_p.reshape(bp, tp, PROJ_W), cwa, cwb, alog_row, dtb_row,
                                                 onw_row, nb_step=2 if bp % 2 == 0 else 1)
    cnt0 = jnp.zeros((1, LANE), f32)
    x1_p, h2_p, mi_p, mw_p, cnt_p = _mix_route(xp2, o_p.reshape(n_p, QK_W), y_p.reshape(n_p, SC_W), proj_p,
                                               wa, wb, wo, n2_row, rwh, rwl, rb_row, cnt0)

    xs2 = x_sample.reshape(n_s, D_MODEL)
    proj_s = _inproj(xs2, norm1_w[0], w_perm)
    bufa_t = jnp.transpose(state_qkv_conv[0], (1, 0, 2))
    bufb_t = jnp.transpose(state_short_conv[0], (1, 0, 2))
    q_s, k_s, v_s, beta_s, eg_s, y_s, nbufa_t, nbufb_t = _sample_prep(proj_s, bufa_t, bufb_t, cwa, cwb,
                                                                      alog_row, dtb_row)
    h3 = lambda a: a.reshape(n_s, N_HEADS, HEAD)
    z_s = proj_s[:, COL_Z:COL_Z + QK_W]
    sd_s, o_s = _sample_step(state_delta[0], h3(q_s), h3(k_s), h3(v_s), h3(beta_s), h3(eg_s), h3(z_s), onw_row)
    o_s2 = o_s.reshape(n_s, QK_W).astype(bf16)
    x1_s, h2_s, mi_s, mw_s, cnt = _mix_route(xs2, o_s2, y_s, proj_s, wa, wb, wo, n2_row, rwh, rwl, rb_row, cnt_p)

    counts = cnt[0, 0:N_EXPERTS].astype(i32)
    starts, items = _work_items(counts, 2 * n_all)
    mi_flat = jnp.concatenate([mi_p[:, 0:MI_W], mi_s[:, 0:MI_W]], axis=0).reshape(MI_W * n_all)
    xs_sorted = _dispatch(h2_p, h2_s, mi_flat, starts)
    ys = _moe(xs_sorted, w_gate[0], w_up[0], w_down[0], items)
    y_prompt, y_sample = _combine(x1_p, mw_p, x1_s, mw_s, final_norm_w.reshape(1, D_MODEL), ys, mi_flat, starts)

    return (y_prompt.reshape(bp, tp, D_MODEL),
            y_sample.reshape(bs, ts, D_MODEL),
            sd_p[None],
            nca_p[None],
            ncb_p[None],
            sd_s[None],
            jnp.transpose(nbufa_t, (1, 0, 2))[None],
            jnp.transpose(nbufb_t, (1, 0, 2))[None])
```

```python
import functools

import jax
import jax.numpy as jnp
from jax import lax
from jax.experimental import pallas as pl
from jax.experimental.pallas import tpu as pltpu

f32 = jnp.float32
bf16 = jnp.bfloat16
i32 = jnp.int32

EPS = 1e-6
LANE = 128
D_MODEL = 2048
N_HEADS = 8
HEAD = 128
QK_W = N_HEADS * HEAD
QKV_W = 3 * QK_W
SC_W = 1024
CONV_A = 4
CONV_B = 3
CHUNK = 64
GROUP_HEADS = 4
N_EXPERTS = 64
N_GROUPS = 8
EXPERTS_PER_GROUP = 8
D_FF = 512
MOE_ROWS = 128

COL_QKV = 0
COL_BCX = 3072
COL_GA = 6144
COL_GB = 8192
COL_Z = 10240
COL_BA = 11264
BA_W = 256
PROJ_W = 11520
PROJ_TN = 1280

VMEM_LIMIT = 56 * 1024 * 1024


def _dot(a, b):
    return jnp.dot(a, b, preferred_element_type=f32)


def _dot_nt(a, b):
    return lax.dot_general(a, b, (((1,), (1,)), ((), ())), preferred_element_type=f32)


def _split(x, n):
    parts = []
    r = x
    for i in range(n):
        p = r.astype(bf16)
        parts.append(p)
        if i + 1 < n:
            r = r - p.astype(f32)
    return parts


def _dot_lsplit(x, m, n=3):
    rows = x.shape[0]
    d = _dot(jnp.concatenate(_split(x, n), axis=0), m)
    acc = d[0:rows]
    for i in range(1, n):
        acc = acc + d[i * rows:(i + 1) * rows]
    return acc


def _dot_rsplit(m, x, n=3):
    cols = x.shape[1]
    d = _dot(m, jnp.concatenate(_split(x, n), axis=1))
    acc = d[:, 0:cols]
    for i in range(1, n):
        acc = acc + d[:, i * cols:(i + 1) * cols]
    return acc


def _silu(x):
    return x * jax.nn.sigmoid(x)


def _softplus(x):
    return jnp.maximum(x, 0.0) + jnp.log(1.0 + jnp.exp(-jnp.abs(x)))


def _cparams(sem):
    return pltpu.CompilerParams(dimension_semantics=sem, vmem_limit_bytes=VMEM_LIMIT)


def _inproj_kernel(x_ref, nw_ref, w_ref, o_ref, h_ref, *, rows):
    @pl.when(pl.program_id(1) == 0)
    def _():
        def body(r, c):
            sl = pl.ds(pl.multiple_of(r * rows, rows), rows)
            x = x_ref[sl, :]
            ms = jnp.mean(x * x, axis=-1, keepdims=True)
            h_ref[sl, :] = (x * lax.rsqrt(ms + EPS) * nw_ref[...]).astype(bf16)
            return c
        lax.fori_loop(0, x_ref.shape[0] // rows, body, 0)

    o_ref[...] = _dot_nt(h_ref[...], w_ref[...])


def _inproj(x2d, norm_w, w_bf16):
    n = x2d.shape[0]
    tm = min(1024, n)
    assert n % tm == 0 and PROJ_W % PROJ_TN == 0
    return pl.pallas_call(
        functools.partial(_inproj_kernel, rows=min(128, tm)),
        out_shape=jax.ShapeDtypeStruct((n, PROJ_W), f32),
        grid=(n // tm, PROJ_W // PROJ_TN),
        in_specs=[pl.BlockSpec((tm, D_MODEL), lambda i, j: (i, 0)),
                  pl.BlockSpec((1, D_MODEL), lambda i, j: (0, 0)),
                  pl.BlockSpec((PROJ_TN, D_MODEL), lambda i, j: (j, 0))],
        out_specs=pl.BlockSpec((tm, PROJ_TN), lambda i, j: (i, j)),
        scratch_shapes=[pltpu.VMEM((tm, D_MODEL), bf16)],
        compiler_params=_cparams(("arbitrary", "arbitrary")),
        name="inproj",
    )(x2d, norm_w.reshape(1, D_MODEL), w_bf16)


W_IN_COLS = 11280
WPREP_TN = 1024
WPREP_SHIFT = 16


def _wprep_kernel(a_ref, b_ref, o_ref):
    j = pl.program_id(0)
    keep = WPREP_TN - WPREP_SHIFT

    @pl.when((j < 3) | (j == 10))
    def _():
        o_ref[...] = a_ref[...].astype(bf16)

    @pl.when((j >= 3) & (j < 10))
    def _():
        o_ref[0:keep, :] = a_ref[WPREP_SHIFT:WPREP_TN, :].astype(bf16)
        o_ref[keep:WPREP_TN, :] = b_ref[...].astype(bf16)

    @pl.when(j == 11)
    def _():
        o_ref[0:WPREP_SHIFT, :] = a_ref[0:WPREP_SHIFT, :].astype(bf16)
        o_ref[WPREP_SHIFT:WPREP_TN, :] = jnp.zeros((keep, D_MODEL), bf16)


def _wprep(w_in_t):
    assert w_in_t.shape == (W_IN_COLS, D_MODEL) and 2 * N_HEADS == WPREP_SHIFT
    n_blk = pl.cdiv(PROJ_W, WPREP_TN)

    def a_map(j):
        return (jnp.where(j < 3, j, jnp.where(j < 10, j + 1, jnp.where(j == 10, 3, 4))), 0)

    def b_map(j):
        return (jnp.minimum((WPREP_TN // WPREP_SHIFT) * (j + 2), W_IN_COLS // WPREP_SHIFT - 1), 0)

    return pl.pallas_call(
        _wprep_kernel,
        out_shape=jax.ShapeDtypeStruct((PROJ_W, D_MODEL), bf16),
        grid=(n_blk,),
        in_specs=[pl.BlockSpec((WPREP_TN, D_MODEL), a_map),
                  pl.BlockSpec((WPREP_SHIFT, D_MODEL), b_map)],
        out_specs=pl.BlockSpec((WPREP_TN, D_MODEL), lambda j: (j, 0)),
        compiler_params=_cparams(("arbitrary",)),
        name="wprep",
    )(w_in_t, w_in_t)


def _head_l2norm(a, scale):
    outs = []
    for h in range(N_HEADS):
        ah = a[:, h * HEAD:(h + 1) * HEAD]
        ss = jnp.sum(ah * ah, axis=-1, keepdims=True)
        n = ah * lax.rsqrt(ss + EPS)
        outs.append(n * scale if scale != 1.0 else n)
    return outs


def _delta_prompt_kernel(qkv_ref, bcx_ref, z_ref, ba_ref, cwa_ref, cwb_ref, alog_ref, dtb_ref, onw_ref,
                         e64_ref,
                         o_ref, y_ref, snew_ref, nca_ref, ncb_ref,
                         s_ref, xa_ref, xb_ref, *, nb_step):
    C = CHUNK
    G = GROUP_HEADS
    R = G * C
    t = pl.program_id(1)
    nt = pl.num_programs(1)

    @pl.when(t == 0)
    def _():
        s_ref[...] = jnp.zeros(s_ref.shape, f32)
        xa_ref[:, 0:8, :] = jnp.zeros((nb_step, 8, QKV_W), f32)
        xb_ref[:, 0:8, :] = jnp.zeros((nb_step, 8, SC_W), f32)

    rr = lax.broadcasted_iota(i32, (R, R), 0)
    cc = lax.broadcasted_iota(i32, (R, R), 1)
    same = (rr >> 6) == (cc >> 6)
    incl = same & (rr >= cc)
    strict = same & (rr > cc)
    eye = jnp.where(rr == cc, 1.0, 0.0).astype(f32)
    r2 = lax.broadcasted_iota(i32, (R, G * HEAD), 0)
    c2 = lax.broadcasted_iota(i32, (R, G * HEAD), 1)
    bdmask = (r2 >> 6) == (c2 >> 7)
    r3 = lax.broadcasted_iota(i32, (C, C), 0)
    c3 = lax.broadcasted_iota(i32, (C, C), 1)
    ltri = jnp.where(r3 >= c3, 1.0, 0.0).astype(bf16)
    r4 = lax.broadcasted_iota(i32, (C, R), 0)
    c4 = lax.broadcasted_iota(i32, (C, R), 1)
    ident_t = r4 == (c4 & (C - 1))
    ones8 = jnp.ones((8, C), bf16)

    nbs = range(nb_step)
    units = [(nb, g) for nb in nbs for g in range(N_HEADS // G)]
    heads = lambda g: range(g * G, (g + 1) * G)

    for nb in nbs:
        xa_ref[nb, 8:8 + C, :] = qkv_ref[nb]

    def conv_sec(nb, lo):
        hi = lo + QK_W
        xe = xa_ref[nb, :, lo:hi]
        acc = pltpu.roll(xe, 3, axis=0)[8:8 + C] * cwa_ref[0:1, lo:hi]
        acc = acc + pltpu.roll(xe, 2, axis=0)[8:8 + C] * cwa_ref[1:2, lo:hi]
        acc = acc + pltpu.roll(xe, 1, axis=0)[8:8 + C] * cwa_ref[2:3, lo:hi]
        acc = acc + xe[8:8 + C] * cwa_ref[3:4, lo:hi]
        return _silu(acc)

    kn = [_head_l2norm(conv_sec(nb, QK_W), 1.0) for nb in nbs]
    qn = [_head_l2norm(conv_sec(nb, 0), HEAD ** -0.5) for nb in nbs]
    vv = [conv_sec(nb, 2 * QK_W) for nb in nbs]
    for nb in nbs:
        xa_ref[nb, 0:8, :] = xa_ref[nb, C:C + 8, :]

    bts = [ba_ref[nb] for nb in nbs]
    beta_all = [jax.nn.sigmoid(bt) for bt in bts]
    g_all = [-(jnp.exp(alog_ref[...]) * _softplus(bt + dtb_ref[...])) for bt in bts]
    gc_small = [_dot_rsplit(ltri, ga) for ga in g_all]
    gl_small = [gc[C - 1:C, :] for gc in gc_small]

    k_st, q_st, kb, vb, kbg, qd, kd, gc_col = ({} for _ in range(8))
    for u in units:
        nb, g = u
        hs = heads(g)
        k_st[u] = jnp.concatenate([kn[nb][h] for h in hs], axis=0)
        q_st[u] = jnp.concatenate([qn[nb][h] for h in hs], axis=0)
        v_st = jnp.concatenate([vv[nb][:, h * HEAD:(h + 1) * HEAD] for h in hs], axis=0)
        beta_col = jnp.concatenate([beta_all[nb][:, h:h + 1] for h in hs], axis=0)
        gc_col[u] = jnp.concatenate([gc_small[nb][:, 8 + h:9 + h] for h in hs], axis=0)
        gl_col = jnp.concatenate(
            [jnp.broadcast_to(gl_small[nb][:, 8 + h:9 + h], (C, 1)) for h in hs], axis=0)
        kb[u] = k_st[u] * beta_col
        vb[u] = v_st * beta_col
        egc = jnp.exp(gc_col[u])
        kbg[u] = kb[u] * egc
        qd[u] = q_st[u] * egc
        kd[u] = k_st[u] * jnp.exp(gl_col - gc_col[u])

    gx = {u: _dot_lsplit(gc_small[u[0]], e64_ref[u[1]]) for u in units}
    crow = {u: _dot_rsplit(ones8, jnp.where(ident_t, gx[u], 0.0))[0:1, :] for u in units}
    a = {u: _dot_nt(jnp.concatenate([kb[u], q_st[u]], axis=0).astype(bf16), k_st[u].astype(bf16))
         for u in units}
    dec = {u: jnp.where(incl, jnp.exp(jnp.where(incl, gc_col[u] - crow[u], 0.0)), 0.0) for u in units}
    nm = {u: jnp.where(strict, -(a[u][0:R] * dec[u]), 0.0) for u in units}
    qkm = {u: a[u][R:2 * R] * dec[u] for u in units}

    p = {u: eye + nm[u] for u in units}
    nk = {}
    for u in units:
        nmb = nm[u].astype(bf16)
        nk[u] = _dot(nmb, nmb)
    for _ in range(4):
        for u in units:
            x = _dot(jnp.concatenate([p[u], nk[u]], axis=0).astype(bf16), nk[u].astype(bf16))
            p[u] = p[u] + x[0:R]
            nk[u] = x[R:2 * R]
    for u in units:
        p[u] = p[u] + _dot(p[u].astype(bf16), nk[u].astype(bf16))
    uw = {u: _dot(p[u].astype(bf16), jnp.concatenate([vb[u], kbg[u]], axis=1).astype(bf16)) for u in units}

    ws = {}
    for u in units:
        nb, g = u
        for j, h in enumerate(heads(g)):
            sh = s_ref[nb, :, h * HEAD:(h + 1) * HEAD]
            lhs = jnp.concatenate([uw[u][j * C:(j + 1) * C, HEAD:2 * HEAD], qd[u][j * C:(j + 1) * C]], axis=0)
            ws[u, j] = _dot(lhs.astype(bf16), sh.astype(bf16))
    o_heads = {}
    for u in units:
        nb, g = u
        vnew_st = jnp.concatenate([uw[u][j * C:(j + 1) * C, 0:HEAD] - ws[u, j][0:C] for j in range(G)], axis=0)
        o_st = (jnp.concatenate([ws[u, j][C:2 * C] for j in range(G)], axis=0)
                + _dot(qkm[u].astype(bf16), vnew_st.astype(bf16)))
        vbd = jnp.where(bdmask, jnp.concatenate([vnew_st] * G, axis=1), 0.0)
        lo = g * G * HEAD
        hi = lo + G * HEAD
        gl_row = jnp.concatenate(
            [jnp.broadcast_to(jnp.exp(gl_small[nb][:, 8 + h:9 + h]), (1, HEAD)) for h in heads(g)], axis=1)
        s_ref[nb, :, lo:hi] = s_ref[nb, :, lo:hi] * gl_row + _dot(kd[u].T.astype(bf16), vbd.astype(bf16))
        for j, h in enumerate(heads(g)):
            o_heads[nb, h] = o_st[j * C:(j + 1) * C]

    for nb in nbs:
        zt = z_ref[nb]
        for h in range(N_HEADS):
            oh = o_heads[nb, h]
            ms = jnp.mean(oh * oh, axis=-1, keepdims=True)
            zh = zt[:, h * HEAD:(h + 1) * HEAD]
            on = oh * lax.rsqrt(ms + EPS) * onw_ref[...] * _silu(zh)
            o_ref[nb, :, h * HEAD:(h + 1) * HEAD] = on.astype(bf16)

    for nb in nbs:
        bcx = bcx_ref[nb]
        cx = bcx[:, SC_W:2 * SC_W] * bcx[:, 2 * SC_W:3 * SC_W]
        xb_ref[nb, 8:8 + C, :] = cx
        ce = xb_ref[nb]
        cv = pltpu.roll(ce, 2, axis=0)[8:8 + C] * cwb_ref[0:1, :]
        cv = cv + pltpu.roll(ce, 1, axis=0)[8:8 + C] * cwb_ref[1:2, :]
        cv = cv + cx * cwb_ref[2:3, :]
        y_ref[nb] = (bcx[:, 0:SC_W] * cv).astype(bf16)
        xb_ref[nb, 0:8, :] = xb_ref[nb, C:C + 8, :]

    @pl.when(t == nt - 1)
    def _():
        for nb in range(nb_step):
            for h in range(N_HEADS):
                snew_ref[nb, h] = s_ref[nb, :, h * HEAD:(h + 1) * HEAD]
            nca_ref[nb] = xa_ref[nb, 5:8, :]
            ncb_ref[nb] = xb_ref[nb, 6:8, :]


def _expand_consts():
    lane = jnp.arange(BA_W)[:, None]
    col = jnp.arange(QK_W)[None, :]
    eb = (lane == (col >> 7)).astype(bf16)
    eg = (lane == (8 + (col >> 7))).astype(bf16)
    col64 = jnp.arange(GROUP_HEADS * CHUNK)[None, :]
    e64 = jnp.stack([(lane == (8 + g * GROUP_HEADS + (col64 >> 6))).astype(bf16)
                     for g in range(N_HEADS // GROUP_HEADS)], axis=0)
    return eb, eg, e64


def _delta_prompt(proj3, cwa, cwb, alog_row, dtb_row, onw_row, nb_step):
    b, t, _ = proj3.shape
    assert t % CHUNK == 0 and b % nb_step == 0
    _, _, e64 = _expand_consts()
    c = CHUNK
    const2 = lambda bi, ti: (0, 0)
    outs = pl.pallas_call(
        functools.partial(_delta_prompt_kernel, nb_step=nb_step),
        out_shape=(jax.ShapeDtypeStruct((b, t, QK_W), bf16),
                   jax.ShapeDtypeStruct((b, t, SC_W), bf16),
                   jax.ShapeDtypeStruct((b, N_HEADS, HEAD, HEAD), f32),
                   jax.ShapeDtypeStruct((b, CONV_A - 1, QKV_W), f32),
                   jax.ShapeDtypeStruct((b, CONV_B - 1, SC_W), f32)),
        grid=(b // nb_step, t // c),
        in_specs=[pl.BlockSpec((nb_step, c, QKV_W), lambda bi, ti: (bi, ti, COL_QKV // QKV_W)),
                  pl.BlockSpec((nb_step, c, QKV_W), lambda bi, ti: (bi, ti, COL_BCX // QKV_W)),
                  pl.BlockSpec((nb_step, c, QK_W), lambda bi, ti: (bi, ti, COL_Z // QK_W)),
                  pl.BlockSpec((nb_step, c, BA_W), lambda bi, ti: (bi, ti, COL_BA // BA_W)),
                  pl.BlockSpec((CONV_A, QKV_W), const2),
                  pl.BlockSpec((CONV_B, SC_W), const2),
                  pl.BlockSpec((1, BA_W), const2),
                  pl.BlockSpec((1, BA_W), const2),
                  pl.BlockSpec((1, HEAD), const2),
                  pl.BlockSpec((N_HEADS // GROUP_HEADS, BA_W, GROUP_HEADS * CHUNK), lambda bi, ti: (0, 0, 0))],
        out_specs=(pl.BlockSpec((nb_step, c, QK_W), lambda bi, ti: (bi, ti, 0)),
                   pl.BlockSpec((nb_step, c, SC_W), lambda bi, ti: (bi, ti, 0)),
                   pl.BlockSpec((nb_step, N_HEADS, HEAD, HEAD), lambda bi, ti: (bi, 0, 0, 0)),
                   pl.BlockSpec((nb_step, CONV_A - 1, QKV_W), lambda bi, ti: (bi, 0, 0)),
                   pl.BlockSpec((nb_step, CONV_B - 1, SC_W), lambda bi, ti: (bi, 0, 0))),
        scratch_shapes=[pltpu.VMEM((nb_step, HEAD, QK_W), f32),
                        pltpu.VMEM((nb_step, 8 + c, QKV_W), f32),
                        pltpu.VMEM((nb_step, 8 + c, SC_W), f32)],
        compiler_params=_cparams(("arbitrary", "arbitrary")),
        name="delta_prompt",
    )(proj3, proj3, proj3, proj3, cwa, cwb, alog_row, dtb_row, onw_row, e64)
    return outs


def _sample_prep_kernel(p_ref, bufa_ref, bufb_ref, cwa_ref, cwb_ref, alog_ref, dtb_ref, eb_ref, eg_ref,
                        q_ref, k_ref, v_ref, beta_ref, eg_out_ref, y_ref, nbufa_ref, nbufb_ref):
    def conv_sec(lo):
        hi = lo + QK_W
        raw = p_ref[:, COL_QKV + lo:COL_QKV + hi]
        acc = bufa_ref[0, :, lo:hi] * cwa_ref[0:1, lo:hi]
        acc = acc + bufa_ref[1, :, lo:hi] * cwa_ref[1:2, lo:hi]
        acc = acc + bufa_ref[2, :, lo:hi] * cwa_ref[2:3, lo:hi]
        acc = acc + raw * cwa_ref[3:4, lo:hi]
        nbufa_ref[0, :, lo:hi] = bufa_ref[1, :, lo:hi]
        nbufa_ref[1, :, lo:hi] = bufa_ref[2, :, lo:hi]
        nbufa_ref[2, :, lo:hi] = raw
        return _silu(acc)

    qn = _head_l2norm(conv_sec(0), HEAD ** -0.5)
    kn = _head_l2norm(conv_sec(QK_W), 1.0)
    for h in range(N_HEADS):
        q_ref[:, h * HEAD:(h + 1) * HEAD] = qn[h]
        k_ref[:, h * HEAD:(h + 1) * HEAD] = kn[h]
    v_ref[...] = conv_sec(2 * QK_W)

    bt = p_ref[:, COL_BA:COL_BA + BA_W]
    beta_all = jax.nn.sigmoid(bt)
    g_all = -(jnp.exp(alog_ref[...]) * _softplus(bt + dtb_ref[...]))
    beta_ref[...] = _dot_lsplit(beta_all, eb_ref[...])
    eg_out_ref[...] = jnp.exp(_dot_lsplit(g_all, eg_ref[...]))

    bg = p_ref[:, COL_BCX:COL_BCX + SC_W]
    cx = p_ref[:, COL_BCX + SC_W:COL_BCX + 2 * SC_W] * p_ref[:, COL_BCX + 2 * SC_W:COL_BCX + 3 * SC_W]
    cv = bufb_ref[0] * cwb_ref[0:1, :]
    cv = cv + bufb_ref[1] * cwb_ref[1:2, :]
    cv = cv + cx * cwb_ref[2:3, :]
    y_ref[...] = (bg * cv).astype(bf16)
    nbufb_ref[0] = bufb_ref[1]
    nbufb_ref[1] = cx


def _sample_prep(proj_s, bufa_t, bufb_t, cwa, cwb, alog_row, dtb_row):
    n = proj_s.shape[0]
    eb, eg, _ = _expand_consts()
    row = jax.ShapeDtypeStruct((n, QK_W), f32)
    return pl.pallas_call(
        _sample_prep_kernel,
        out_shape=(row, row, row, row, row,
                   jax.ShapeDtypeStruct((n, SC_W), bf16),
                   jax.ShapeDtypeStruct((CONV_A - 1, n, QKV_W), f32),
                   jax.ShapeDtypeStruct((CONV_B - 1, n, SC_W), f32)),
        compiler_params=pltpu.CompilerParams(vmem_limit_bytes=VMEM_LIMIT),
        name="sample_prep",
    )(proj_s, bufa_t, bufb_t, cwa, cwb, alog_row, dtb_row, eb, eg)


def _sample_step_kernel(s_ref, q_ref, k_ref, v_ref, beta_ref, eg_ref, z_ref, onw_ref,
                        snew_ref, o_ref, *, bb):
    w = N_HEADS * HEAD
    r8 = lax.broadcasted_iota(i32, (N_HEADS, w), 0)
    c8 = lax.broadcasted_iota(i32, (N_HEADS, w), 1)
    mask8 = r8 == (c8 >> 7)
    zpad_k = jnp.zeros((HEAD - N_HEADS, HEAD), f32)
    zpad_d = jnp.zeros((HEAD - N_HEADS, w), f32)
    for b in range(bb):
        s_all = jnp.concatenate([s_ref[b, h] for h in range(N_HEADS)], axis=1)
        eg8 = eg_ref[b]
        eg_row = jnp.concatenate([eg8[h:h + 1, :] for h in range(N_HEADS)], axis=1)
        s_dec = s_all * eg_row
        k8 = k_ref[b]
        x = _dot(k8.astype(bf16), s_dec.astype(bf16))
        v_t = jnp.concatenate([v_ref[b]] * N_HEADS, axis=1)
        b_t = jnp.concatenate([beta_ref[b]] * N_HEADS, axis=1)
        d_bd = jnp.where(mask8, (v_t - x) * b_t, 0.0)
        kt = jnp.concatenate([k8, zpad_k], axis=0).T
        d_pad = jnp.concatenate([d_bd, zpad_d], axis=0)
        k_hi, k_lo = _split(kt, 2)
        d_hi, d_lo = _split(d_pad, 2)
        s_new = s_dec + (_dot(k_hi, d_hi) + _dot(k_hi, d_lo) + _dot(k_lo, d_hi))
        yv = jnp.where(mask8, _dot(q_ref[b].astype(bf16), s_new.astype(bf16)), 0.0)
        o8 = yv[:, 0:HEAD]
        for j in range(1, N_HEADS):
            o8 = o8 + yv[:, j * HEAD:(j + 1) * HEAD]
        ms = jnp.mean(o8 * o8, axis=-1, keepdims=True)
        o_ref[b] = o8 * lax.rsqrt(ms + EPS) * onw_ref[...] * _silu(z_ref[b])
        for h in range(N_HEADS):
            snew_ref[b, h] = s_new[:, h * HEAD:(h + 1) * HEAD]


def _sample_step(state, q, k, v, beta, eg, z, onw_row, bb=4):
    n = state.shape[0]
    assert n % bb == 0
    hspec = pl.BlockSpec((bb, N_HEADS, HEAD), lambda i: (i, 0, 0))
    sspec = pl.BlockSpec((bb, N_HEADS, HEAD, HEAD), lambda i: (i, 0, 0, 0))
    return pl.pallas_call(
        functools.partial(_sample_step_kernel, bb=bb),
        out_shape=(jax.ShapeDtypeStruct(state.shape, f32),
                   jax.ShapeDtypeStruct((n, N_HEADS, HEAD), f32)),
        grid=(n // bb,),
        in_specs=[sspec, hspec, hspec, hspec, hspec, hspec, hspec, pl.BlockSpec((1, HEAD), lambda i: (0, 0))],
        out_specs=(sspec, hspec),
        compiler_params=_cparams(("arbitrary",)),
        name="sample_step",
    )(state, q, k, v, beta, eg, z, onw_row)


def _mix_route_kernel(x_ref, o_ref, y_ref, ga_ref, gb_ref, wa_ref, wb_ref, wo_ref, n2_ref,
                      rwh_ref, rwl_ref, rb_ref, cnt_in_ref, x1_ref, h2_ref, mi_ref, mw_ref, cnt_ref):
    i = pl.program_id(0)
    tm = x_ref.shape[0]

    @pl.when(i == 0)
    def _():
        cnt_ref[...] = cnt_in_ref[...]

    oa = _dot(o_ref[...], wa_ref[...])
    ob = _dot(y_ref[...], wb_ref[...])
    merged = jax.nn.sigmoid(ga_ref[...]) * oa + jax.nn.sigmoid(gb_ref[...]) * ob
    x1 = x_ref[...] + _dot(merged.astype(bf16), wo_ref[...])
    x1_ref[...] = x1
    ms = jnp.mean(x1 * x1, axis=-1, keepdims=True)
    h2 = x1 * lax.rsqrt(ms + EPS) * n2_ref[...]
    h2_ref[...] = h2

    h_hi, h_lo = _split(h2, 2)
    logits = _dot(h_hi, rwh_ref[...]) + _dot(h_hi, rwl_ref[...]) + _dot(h_lo, rwh_ref[...]) + rb_ref[...]

    lane = lax.broadcasted_iota(i32, (tm, LANE), 1)
    lanef = lane.astype(f32)
    neg = jnp.float32(-jnp.inf)
    big = jnp.float32(1e9)
    gmask = (lane >= N_EXPERTS) & (lane < N_EXPERTS + N_GROUPS)
    gl = jnp.where(gmask, logits, neg)
    gmax = jnp.max(gl, axis=-1, keepdims=True)
    gidx = jnp.min(jnp.where(gl == gmax, lanef - N_EXPERTS, big), axis=-1, keepdims=True)
    gsum = jnp.sum(jnp.where(gmask, jnp.exp(gl - gmax), 0.0), axis=-1, keepdims=True)
    gprob = 1.0 / gsum

    emask = (lane < N_EXPERTS) & ((lane >> 3).astype(f32) == gidx)
    el = jnp.where(emask, logits, neg)
    emax = jnp.max(el, axis=-1, keepdims=True)
    pe = jnp.where(emask, jnp.exp(el - emax), 0.0)
    eprob = pe / jnp.sum(pe, axis=-1, keepdims=True)
    p1m = jnp.where(emask, eprob, -1.0)
    m1 = jnp.max(p1m, axis=-1, keepdims=True)
    i1 = jnp.min(jnp.where(p1m == m1, lanef, big), axis=-1, keepdims=True)
    p2m = jnp.where(lanef == i1, -1.0, p1m)
    m2 = jnp.max(p2m, axis=-1, keepdims=True)
    i2 = jnp.min(jnp.where(p2m == m2, lanef, big), axis=-1, keepdims=True)
    tot = m1 + m2
    c1 = m1 / tot * gprob
    c2 = m2 / tot * gprob

    oh1 = jnp.where(lanef == i1, 1.0, 0.0)
    oh2 = jnp.where(lanef == i2, 1.0, 0.0)
    ohs = oh1 + oh2
    rt = lax.broadcasted_iota(i32, (tm, tm), 0)
    ct = lax.broadcasted_iota(i32, (tm, tm), 1)
    lstrict = jnp.where(rt > ct, 1.0, 0.0).astype(bf16)
    cs = _dot(lstrict, ohs.astype(bf16)) + cnt_ref[...]
    rank1 = jnp.sum(cs * oh1, axis=-1, keepdims=True)
    rank2 = jnp.sum(cs * oh2, axis=-1, keepdims=True)
    cnt_ref[...] = cnt_ref[...] + jnp.sum(ohs, axis=0, keepdims=True)

    mi = jnp.where(lane == 0, i1, jnp.where(lane == 1, i2, jnp.where(lane == 2, rank1,
                                                                     jnp.where(lane == 3, rank2, 0.0))))
    mi_ref[...] = mi.astype(i32)
    mw_ref[...] = jnp.where(lane == 0, c1, jnp.where(lane == 1, c2, 0.0))


def _mix_route(x2d, o2d, y2d, proj2d, wa, wb, wo, n2_row, rwh, rwl, rb_row, cnt_in):
    n = x2d.shape[0]
    tm = min(256, n)
    assert n % tm == 0
    tok = lambda width: pl.BlockSpec((tm, width), lambda i: (i, 0))
    full = lambda a: pl.BlockSpec(a.shape, lambda i: (0,) * a.ndim)
    in_specs = [tok(D_MODEL), tok(QK_W), tok(SC_W),
                pl.BlockSpec((tm, D_MODEL), lambda i: (i, COL_GA // D_MODEL)),
                pl.BlockSpec((tm, D_MODEL), lambda i: (i, COL_GB // D_MODEL)),
                full(wa), full(wb), full(wo), full(n2_row), full(rwh), full(rwl), full(rb_row), full(cnt_in)]
    out_shape = (jax.ShapeDtypeStruct((n, D_MODEL), f32),
                 jax.ShapeDtypeStruct((n, D_MODEL), f32),
                 jax.ShapeDtypeStruct((n, LANE), i32),
                 jax.ShapeDtypeStruct((n, LANE), f32),
                 jax.ShapeDtypeStruct((1, LANE), f32))
    out_specs = (tok(D_MODEL), tok(D_MODEL), tok(LANE), tok(LANE),
                 pl.BlockSpec((1, LANE), lambda i: (0, 0)))
    return pl.pallas_call(
        _mix_route_kernel,
        out_shape=out_shape,
        grid=(n // tm,),
        in_specs=in_specs,
        out_specs=out_specs,
        compiler_params=_cparams(("arbitrary",)),
        name="mix_route",
    )(x2d, o2d, y2d, proj2d, proj2d, wa, wb, wo, n2_row, rwh, rwl, rb_row, cnt_in)


MI_W = 4


def _dest_row(mi_ref, starts_ref, r, k):
    return starts_ref[mi_ref[MI_W * r + k]] + mi_ref[MI_W * r + 2 + k]


def _dispatch_kernel(mi_ref, starts_ref, hp_ref, hs_ref, xs_ref, sem, *, np_tiles, tm):
    i = pl.program_id(0)

    def scatter_rows(h_ref):
        def copy(r, k):
            return pltpu.make_async_copy(h_ref.at[pl.ds(r, 1)],
                                         xs_ref.at[pl.ds(_dest_row(mi_ref, starts_ref, r, k), 1)], sem)

        def start(r, c):
            for k in range(2):
                copy(r, k).start(priority=k)
            return c

        def wait(r, c):
            for k in range(2):
                copy(r, k).wait()
            return c

        lax.fori_loop(0, tm, start, 0)
        lax.fori_loop(0, tm, wait, 0)

    @pl.when(i < np_tiles)
    def _():
        scatter_rows(hp_ref)

    @pl.when(i >= np_tiles)
    def _():
        scatter_rows(hs_ref)


def _dispatch(h2_p, h2_s, mi_flat, starts):
    tm = MOE_ROWS
    n_p, n_s = h2_p.shape[0], h2_s.shape[0]
    assert n_p % tm == 0 and n_s == tm
    np_tiles = n_p // tm
    return pl.pallas_call(
        functools.partial(_dispatch_kernel, np_tiles=np_tiles, tm=tm),
        out_shape=jax.ShapeDtypeStruct((2 * (n_p + n_s), D_MODEL), f32),
        grid=(np_tiles + 1,),
        in_specs=[pl.BlockSpec((MI_W * tm,), lambda i: (i,), memory_space=pltpu.SMEM),
                  pl.BlockSpec(memory_space=pltpu.SMEM),
                  pl.BlockSpec((tm, D_MODEL), lambda i: (jnp.minimum(i, np_tiles - 1), 0)),
                  pl.BlockSpec((tm, D_MODEL), lambda i: (0, 0))],
        out_specs=pl.BlockSpec(memory_space=pl.ANY),
        scratch_shapes=[pltpu.SemaphoreType.DMA(())],
        compiler_params=_cparams(("arbitrary",)),
        name="moe_dispatch",
    )(mi_flat, starts, h2_p, h2_s)


def _cast_rows(src_ref, dst_ref, rows=256):
    def body(r, c):
        sl = pl.ds(pl.multiple_of(r * rows, rows), rows)
        dst_ref[sl, :] = src_ref[sl, :].astype(bf16)
        return c
    lax.fori_loop(0, dst_ref.shape[0] // rows, body, 0)


def _moe_kernel(blk_ref, lo_ref, hi_ref, first_ref, newe_ref, slot_ref, pre_ref, init_ref,
                x_ref, wg_hbm, wu_hbm, wd_hbm, o_ref,
                wg_f, wu_f, wd_f, wg_b, wu_b, wd_b, sem):
    i = pl.program_id(0)
    lo = lo_ref[i]
    hi = hi_ref[i]

    def weight_copies(e, slot):
        return [pltpu.make_async_copy(wg_hbm.at[e], wg_f.at[slot], sem.at[slot, 0]),
                pltpu.make_async_copy(wu_hbm.at[e], wu_f.at[slot], sem.at[slot, 1]),
                pltpu.make_async_copy(wd_hbm.at[e], wd_f.at[slot], sem.at[slot, 2])]

    @pl.when(i == 0)
    def _():
        for cp in weight_copies(init_ref[0], 0):
            cp.start(priority=1)

        @pl.when(init_ref[1] >= 0)
        def _():
            for cp in weight_copies(init_ref[1], 1):
                cp.start(priority=1)

    @pl.when(newe_ref[i] == 1)
    def _():
        slot = slot_ref[i]
        for cp in weight_copies(0, slot):
            cp.wait()
        _cast_rows(wg_f.at[slot], wg_b)
        _cast_rows(wu_f.at[slot], wu_b)
        _cast_rows(wd_f.at[slot], wd_b)

        @pl.when(pre_ref[i] >= 0)
        def _():
            for cp in weight_copies(pre_ref[i], slot):
                cp.start(priority=1)

    @pl.when(hi > lo)
    def _():
        x = x_ref[...].astype(bf16)
        a = _dot(x, wg_b[...])
        u = _dot(x, wu_b[...])
        y = _dot((_silu(a) * u).astype(bf16), wd_b[...])
        row = lax.broadcasted_iota(i32, y.shape, 0)
        ym = jnp.where((row >= lo) & (row < hi), y, 0.0)

        @pl.when(first_ref[i] == 1)
        def _():
            o_ref[...] = ym

        @pl.when(first_ref[i] == 0)
        def _():
            o_ref[...] = o_ref[...] + ym


def _moe(xs, w_gate, w_up, w_down, items):
    n_items = items[0].shape[0]
    rows = xs.shape[0]
    n_pref = len(items)
    xmap = lambda i, blk, *_: (blk[i], 0)
    grid_spec = pltpu.PrefetchScalarGridSpec(
        num_scalar_prefetch=n_pref,
        grid=(n_items,),
        in_specs=[pl.BlockSpec((MOE_ROWS, D_MODEL), xmap),
                  pl.BlockSpec(memory_space=pl.ANY),
                  pl.BlockSpec(memory_space=pl.ANY),
                  pl.BlockSpec(memory_space=pl.ANY)],
        out_specs=pl.BlockSpec((MOE_ROWS, D_MODEL), xmap),
        scratch_shapes=[pltpu.VMEM((2, D_MODEL, D_FF), f32), pltpu.VMEM((2, D_MODEL, D_FF), f32),
                        pltpu.VMEM((2, D_FF, D_MODEL), f32),
                        pltpu.VMEM((D_MODEL, D_FF), bf16), pltpu.VMEM((D_MODEL, D_FF), bf16),
                        pltpu.VMEM((D_FF, D_MODEL), bf16),
                        pltpu.SemaphoreType.DMA((2, 3))],
    )
    return pl.pallas_call(
        _moe_kernel,
        out_shape=jax.ShapeDtypeStruct((rows, D_MODEL), f32),
        grid_spec=grid_spec,
        compiler_params=_cparams(("arbitrary",)),
        name="moe_experts",
    )(*items, xs, w_gate, w_up, w_down)


def _combine_kernel(mi_ref, mi_next_ref, starts_ref, x1p_ref, mwp_ref, x1s_ref, mws_ref, fnw_ref, ys_ref,
                    yp_ref, ysm_ref, g_ref, sem, *, np_tiles):
    i = pl.program_id(0)
    n = pl.num_programs(0)
    tm = x1p_ref.shape[0]
    slot = lax.rem(i, 2)

    def gather_rows(m_ref, dst_slot):
        def body(r, c):
            for k in range(2):
                pltpu.make_async_copy(ys_ref.at[pl.ds(_dest_row(m_ref, starts_ref, r, k), 1)],
                                      g_ref.at[dst_slot, k, pl.ds(r, 1)], sem.at[dst_slot]).start(priority=k)
            return c
        lax.fori_loop(0, tm, body, 0)

    @pl.when(i == 0)
    def _():
        gather_rows(mi_ref, 0)

    @pl.when(i + 1 < n)
    def _():
        gather_rows(mi_next_ref, 1 - slot)

    def wait_body(r, c):
        for k in range(2):
            pltpu.make_async_copy(ys_ref.at[pl.ds(0, 1)], g_ref.at[slot, k, pl.ds(r, 1)], sem.at[slot]).wait()
        return c
    lax.fori_loop(0, tm, wait_body, 0)

    def finish(x1_ref, mw_ref, out_ref):
        mw = mw_ref[...]
        x2 = x1_ref[...] + (g_ref[slot, 0] * mw[:, 0:1] + g_ref[slot, 1] * mw[:, 1:2])
        ms = jnp.mean(x2 * x2, axis=-1, keepdims=True)
        out_ref[...] = x2 * lax.rsqrt(ms + EPS) * fnw_ref[...]

    @pl.when(i < np_tiles)
    def _():
        finish(x1p_ref, mwp_ref, yp_ref)

    @pl.when(i >= np_tiles)
    def _():
        finish(x1s_ref, mws_ref, ysm_ref)


def _combine(x1_p, mw_p, x1_s, mw_s, fnw_row, ys, mi_flat, starts):
    tm = MOE_ROWS
    n_p, n_s = x1_p.shape[0], x1_s.shape[0]
    assert n_p % tm == 0 and n_s == tm
    np_tiles = n_p // tm
    ptile = lambda width: pl.BlockSpec((tm, width), lambda i: (jnp.minimum(i, np_tiles - 1), 0))
    stile = lambda width: pl.BlockSpec((tm, width), lambda i: (0, 0))
    return pl.pallas_call(
        functools.partial(_combine_kernel, np_tiles=np_tiles),
        out_shape=(jax.ShapeDtypeStruct((n_p, D_MODEL), f32),
                   jax.ShapeDtypeStruct((n_s, D_MODEL), f32)),
        grid=(np_tiles + 1,),
        in_specs=[pl.BlockSpec((MI_W * tm,), lambda i: (i,), memory_space=pltpu.SMEM),
                  pl.BlockSpec((MI_W * tm,), lambda i: (jnp.minimum(i + 1, np_tiles),), memory_space=pltpu.SMEM),
                  pl.BlockSpec(memory_space=pltpu.SMEM),
                  ptile(D_MODEL), ptile(LANE), stile(D_MODEL), stile(LANE),
                  pl.BlockSpec((1, D_MODEL), lambda i: (0, 0)),
                  pl.BlockSpec(memory_space=pl.ANY)],
        out_specs=(ptile(D_MODEL), stile(D_MODEL)),
        scratch_shapes=[pltpu.VMEM((2, 2, tm, D_MODEL), f32), pltpu.SemaphoreType.DMA((2,))],
        compiler_params=_cparams(("arbitrary",)),
        name="moe_combine",
    )(mi_flat, mi_flat, starts, x1_p, mw_p, x1_s, mw_s, fnw_row, ys)


def _work_items(counts, n_rows):
    nblk = n_rows // MOE_ROWS
    n_items = nblk + N_EXPERTS - 1
    ends = jnp.cumsum(counts)
    starts = ends - counts
    first_blk = starts // MOE_ROWS
    last_blk = jnp.maximum(ends - 1, 0) // MOE_ROWS
    nvis = jnp.where(counts > 0, last_blk - first_blk + 1, 0)
    vis_end = jnp.cumsum(nvis)
    vis_start = vis_end - nvis
    total = vis_end[-1]
    idx = jnp.arange(n_items, dtype=i32)
    e = jnp.minimum(jnp.sum((vis_end[None, :] <= idx[:, None]).astype(i32), axis=1), N_EXPERTS - 1)
    onehot = (e[:, None] == jnp.arange(N_EXPERTS, dtype=i32)[None, :]).astype(i32)
    look = lambda tbl: jnp.sum(onehot * tbl[None, :], axis=1)
    blk = look(first_blk) + idx - look(vis_start)
    lo = jnp.maximum(look(starts), blk * MOE_ROWS) - blk * MOE_ROWS
    hi = jnp.minimum(look(ends), (blk + 1) * MOE_ROWS) - blk * MOE_ROWS
    valid = idx < total
    blk = jnp.where(valid, blk, nblk - 1).astype(i32)
    lo = jnp.where(valid, lo, 0).astype(i32)
    hi = jnp.where(valid, hi, 0).astype(i32)
    prev_blk = jnp.concatenate([jnp.full((1,), -1, i32), blk[:-1]])
    first = (valid & (blk != prev_blk)).astype(i32)
    prev_e = jnp.concatenate([jnp.full((1,), -1, i32), e[:-1]])
    newe = (valid & (e != prev_e)).astype(i32)
    order = jnp.cumsum(newe) - 1
    slot = jnp.where(newe == 1, order % 2, 0).astype(i32)
    cum_act = jnp.cumsum((counts > 0).astype(i32))
    n_uniq = cum_act[-1]
    kk = jnp.arange(N_EXPERTS + 2, dtype=i32)
    uniq_e = jnp.sum((cum_act[None, :] <= kk[:, None]).astype(i32), axis=1)
    uniq_e = jnp.where(kk < n_uniq, uniq_e, -1)
    ahead = jnp.sum((kk[None, :] == (order + 2)[:, None]).astype(i32) * uniq_e[None, :], axis=1)
    pre = jnp.where(newe == 1, ahead, -1).astype(i32)
    init = uniq_e[0:2].astype(i32)
    return starts.astype(i32), (blk, lo, hi, first, newe, slot, pre, init)


def kernel(x_prompt, x_sample, state_delta, state_qkv_conv, state_short_conv, norm1_w, w_in, conv_a_w, a_log, dt_bias, out_norm_w, w_branch_a, conv_b_w, w_branch_b, w_o, norm2_w, router_group_w, router_group_b, router_expert_w, router_expert_b, w_gate, w_up, w_down, final_norm_w):
    assert norm1_w.shape[0] == 1, "single-layer trunk"
    bp, tp, d = x_prompt.shape
    bs, ts, _ = x_sample.shape
    assert d == D_MODEL and ts == 1
    n_p = bp * tp
    n_s = bs
    n_all = n_p + n_s

    w_perm = _wprep(jnp.transpose(w_in[0]))
    wa = w_branch_a[0].astype(bf16)
    wb = w_branch_b[0].astype(bf16)
    wo = w_o[0].astype(bf16)
    pad = lambda v: jnp.zeros((1, BA_W), f32).at[0, N_HEADS:2 * N_HEADS].set(v)
    alog_row = pad(a_log[0])
    dtb_row = pad(dt_bias[0])
    onw_row = out_norm_w[0].reshape(1, HEAD)
    cwa = conv_a_w[0]
    cwb = conv_b_w[0]
    rw = jnp.zeros((D_MODEL, LANE), f32)
    rw = rw.at[:, 0:N_EXPERTS].set(router_expert_w[0]).at[:, N_EXPERTS:N_EXPERTS + N_GROUPS].set(router_group_w[0])
    rwh = rw.astype(bf16)
    rwl = (rw - rwh.astype(f32)).astype(bf16)
    rb_row = jnp.zeros((1, LANE), f32)
    rb_row = rb_row.at[0, 0:N_EXPERTS].set(router_expert_b[0]).at[0, N_EXPERTS:N_EXPERTS + N_GROUPS].set(router_group_b[0])
    n2_row = norm2_w[0].reshape(1, D_MODEL)

    xp2 = x_prompt.reshape(n_p, D_MODEL)
    proj_p = _inproj(xp2, norm1_w[0], w_perm)
    o_p, y_p, sd_p, nca_p, ncb_p = _delta_prompt(proj_p.reshape(bp, tp, PROJ_W), cwa, cwb, alog_row, dtb_row,
                                                 onw_row, nb_step=2 if bp % 2 == 0 else 1)
    cnt0 = jnp.zeros((1, LANE), f32)
    x1_p, h2_p, mi_p, mw_p, cnt_p = _mix_route(xp2, o_p.reshape(n_p, QK_W), y_p.reshape(n_p, SC_W), proj_p,
                                               wa, wb, wo, n2_row, rwh, rwl, rb_row, cnt0)

    xs2 = x_sample.reshape(n_s, D_MODEL)
    proj_s = _inproj(xs2, norm1_w[0], w_perm)
    bufa_t = jnp.transpose(state_qkv_conv[0], (1, 0, 2))
    bufb_t = jnp.transpose(state_short_conv[0], (1, 0, 2))
    q_s, k_s, v_s, beta_s, eg_s, y_s, nbufa_t, nbufb_t = _sample_prep(proj_s, bufa_t, bufb_t, cwa, cwb,
                                                                      alog_row, dtb_row)
    h3 = lambda a: a.reshape(n_s, N_HEADS, HEAD)
    z_s = proj_s[:, COL_Z:COL_Z + QK_W]
    sd_s, o_s = _sample_step(state_delta[0], h3(q_s), h3(k_s), h3(v_s), h3(beta_s), h3(eg_s), h3(z_s), onw_row)
    o_s2 = o_s.reshape(n_s, QK_W).astype(bf16)
    x1_s, h2_s, mi_s, mw_s, cnt = _mix_route(xs2, o_s2, y_s, proj_s, wa, wb, wo, n2_row, rwh, rwl, rb_row, cnt_p)

    counts = cnt[0, 0:N_EXPERTS].astype(i32)
    starts, items = _work_items(counts, 2 * n_all)
    mi_flat = jnp.concatenate([mi_p[:, 0:MI_W], mi_s[:, 0:MI_W]], axis=0).reshape(MI_W * n_all)
    xs_sorted = _dispatch(h2_p, h2_s, mi_flat, starts)
    ys = _moe(xs_sorted, w_gate[0], w_up[0], w_down[0], items)
    y_prompt, y_sample = _combine(x1_p, mw_p, x1_s, mw_s, final_norm_w.reshape(1, D_MODEL), ys, mi_flat, starts)

    return (y_prompt.reshape(bp, tp, D_MODEL),
            y_sample.reshape(bs, ts, D_MODEL),
            sd_p[None],
            nca_p[None],
            ncb_p[None],
            sd_s[None],
            jnp.transpose(nbufa_t, (1, 0, 2))[None],
            jnp.transpose(nbufb_t, (1, 0, 2))[None])
```

```python
import functools

import jax
import jax.numpy as jnp
from jax import lax
from jax.experimental import pallas as pl
from jax.experimental.pallas import tpu as pltpu

f32 = jnp.float32
bf16 = jnp.bfloat16
i32 = jnp.int32

EPS = 1e-6
LANE = 128
D_MODEL = 2048
N_HEADS = 8
HEAD = 128
QK_W = N_HEADS * HEAD
QKV_W = 3 * QK_W
SC_W = 1024
CONV_A = 4
CONV_B = 3
CHUNK = 64
GROUP_HEADS = 4
N_EXPERTS = 64
N_GROUPS = 8
EXPERTS_PER_GROUP = 8
D_FF = 512
MOE_ROWS = 128

COL_QKV = 0
COL_BCX = 3072
COL_GA = 6144
COL_GB = 8192
COL_Z = 10240
COL_BA = 11264
BA_W = 256
PROJ_W = 11520
PROJ_TN = 1280

VMEM_LIMIT = 56 * 1024 * 1024


def _dot(a, b):
    return jnp.dot(a, b, preferred_element_type=f32)


def _dot_nt(a, b):
    return lax.dot_general(a, b, (((1,), (1,)), ((), ())), preferred_element_type=f32)


def _split(x, n):
    parts = []
    r = x
    for i in range(n):
        p = r.astype(bf16)
        parts.append(p)
        if i + 1 < n:
            r = r - p.astype(f32)
    return parts


def _dot_lsplit(x, m, n=3):
    rows = x.shape[0]
    d = _dot(jnp.concatenate(_split(x, n), axis=0), m)
    acc = d[0:rows]
    for i in range(1, n):
        acc = acc + d[i * rows:(i + 1) * rows]
    return acc


def _dot_rsplit(m, x, n=3):
    cols = x.shape[1]
    d = _dot(m, jnp.concatenate(_split(x, n), axis=1))
    acc = d[:, 0:cols]
    for i in range(1, n):
        acc = acc + d[:, i * cols:(i + 1) * cols]
    return acc


def _silu(x):
    return x * jax.nn.sigmoid(x)


def _softplus(x):
    return jnp.maximum(x, 0.0) + jnp.log(1.0 + jnp.exp(-jnp.abs(x)))


def _cparams(sem):
    return pltpu.CompilerParams(dimension_semantics=sem, vmem_limit_bytes=VMEM_LIMIT)


def _inproj_kernel(x_ref, nw_ref, w_ref, o_ref, h_ref, *, rows):
    @pl.when(pl.program_id(1) == 0)
    def _():
        def body(r, c):
            sl = pl.ds(pl.multiple_of(r * rows, rows), rows)
            x = x_ref[sl, :]
            ms = jnp.mean(x * x, axis=-1, keepdims=True)
            h_ref[sl, :] = (x * lax.rsqrt(ms + EPS) * nw_ref[...]).astype(bf16)
            return c
        lax.fori_loop(0, x_ref.shape[0] // rows, body, 0)

    o_ref[...] = _dot_nt(h_ref[...], w_ref[...])


def _inproj(x2d, norm_w, w_bf16):
    n = x2d.shape[0]
    tm = min(1024, n)
    assert n % tm == 0 and PROJ_W % PROJ_TN == 0
    return pl.pallas_call(
        functools.partial(_inproj_kernel, rows=min(128, tm)),
        out_shape=jax.ShapeDtypeStruct((n, PROJ_W), f32),
        grid=(n // tm, PROJ_W // PROJ_TN),
        in_specs=[pl.BlockSpec((tm, D_MODEL), lambda i, j: (i, 0)),
                  pl.BlockSpec((1, D_MODEL), lambda i, j: (0, 0)),
                  pl.BlockSpec((PROJ_TN, D_MODEL), lambda i, j: (j, 0))],
        out_specs=pl.BlockSpec((tm, PROJ_TN), lambda i, j: (i, j)),
        scratch_shapes=[pltpu.VMEM((tm, D_MODEL), bf16)],
        compiler_params=_cparams(("arbitrary", "arbitrary")),
        name="inproj",
    )(x2d, norm_w.reshape(1, D_MODEL), w_bf16)


W_IN_COLS = 11280
WPREP_TN = 1024
WPREP_SHIFT = 16


def _wprep_kernel(a_ref, b_ref, o_ref):
    j = pl.program_id(0)
    keep = WPREP_TN - WPREP_SHIFT

    @pl.when((j < 3) | (j == 10))
    def _():
        o_ref[...] = a_ref[...].astype(bf16)

    @pl.when((j >= 3) & (j < 10))
    def _():
        o_ref[0:keep, :] = a_ref[WPREP_SHIFT:WPREP_TN, :].astype(bf16)
        o_ref[keep:WPREP_TN, :] = b_ref[...].astype(bf16)

    @pl.when(j == 11)
    def _():
        o_ref[0:WPREP_SHIFT, :] = a_ref[0:WPREP_SHIFT, :].astype(bf16)
        o_ref[WPREP_SHIFT:WPREP_TN, :] = jnp.zeros((keep, D_MODEL), bf16)


def _wprep(w_in_t):
    assert w_in_t.shape == (W_IN_COLS, D_MODEL) and 2 * N_HEADS == WPREP_SHIFT
    n_blk = pl.cdiv(PROJ_W, WPREP_TN)

    def a_map(j):
        return (jnp.where(j < 3, j, jnp.where(j < 10, j + 1, jnp.where(j == 10, 3, 4))), 0)

    def b_map(j):
        return (jnp.minimum((WPREP_TN // WPREP_SHIFT) * (j + 2), W_IN_COLS // WPREP_SHIFT - 1), 0)

    return pl.pallas_call(
        _wprep_kernel,
        out_shape=jax.ShapeDtypeStruct((PROJ_W, D_MODEL), bf16),
        grid=(n_blk,),
        in_specs=[pl.BlockSpec((WPREP_TN, D_MODEL), a_map),
                  pl.BlockSpec((WPREP_SHIFT, D_MODEL), b_map)],
        out_specs=pl.BlockSpec((WPREP_TN, D_MODEL), lambda j: (j, 0)),
        compiler_params=_cparams(("arbitrary",)),
        name="wprep",
    )(w_in_t, w_in_t)


def _head_l2norm(a, scale):
    outs = []
    for h in range(N_HEADS):
        ah = a[:, h * HEAD:(h + 1) * HEAD]
        ss = jnp.sum(ah * ah, axis=-1, keepdims=True)
        n = ah * lax.rsqrt(ss + EPS)
        outs.append(n * scale if scale != 1.0 else n)
    return outs


def _delta_prompt_kernel(qkv_ref, bcx_ref, z_ref, ba_ref, cwa_ref, cwb_ref, alog_ref, dtb_ref, onw_ref,
                         e64_ref,
                         o_ref, y_ref, snew_ref, nca_ref, ncb_ref,
                         s_ref, xa_ref, xb_ref, *, nb_step):
    C = CHUNK
    G = GROUP_HEADS
    R = G * C
    t = pl.program_id(1)
    nt = pl.num_programs(1)

    @pl.when(t == 0)
    def _():
        s_ref[...] = jnp.zeros(s_ref.shape, f32)
        xa_ref[:, 0:8, :] = jnp.zeros((nb_step, 8, QKV_W), f32)
        xb_ref[:, 0:8, :] = jnp.zeros((nb_step, 8, SC_W), f32)

    rr = lax.broadcasted_iota(i32, (R, R), 0)
    cc = lax.broadcasted_iota(i32, (R, R), 1)
    same = (rr >> 6) == (cc >> 6)
    incl = same & (rr >= cc)
    strict = same & (rr > cc)
    eye = jnp.where(rr == cc, 1.0, 0.0).astype(f32)
    r2 = lax.broadcasted_iota(i32, (R, G * HEAD), 0)
    c2 = lax.broadcasted_iota(i32, (R, G * HEAD), 1)
    bdmask = (r2 >> 6) == (c2 >> 7)
    r3 = lax.broadcasted_iota(i32, (C, C), 0)
    c3 = lax.broadcasted_iota(i32, (C, C), 1)
    ltri = jnp.where(r3 >= c3, 1.0, 0.0).astype(bf16)
    r4 = lax.broadcasted_iota(i32, (C, R), 0)
    c4 = lax.broadcasted_iota(i32, (C, R), 1)
    ident_t = r4 == (c4 & (C - 1))
    ones8 = jnp.ones((8, C), bf16)

    nbs = range(nb_step)
    units = [(nb, g) for nb in nbs for g in range(N_HEADS // G)]
    heads = lambda g: range(g * G, (g + 1) * G)

    for nb in nbs:
        xa_ref[nb, 8:8 + C, :] = qkv_ref[nb]

    def conv_sec(nb, lo):
        hi = lo + QK_W
        xe = xa_ref[nb, :, lo:hi]
        acc = pltpu.roll(xe, 3, axis=0)[8:8 + C] * cwa_ref[0:1, lo:hi]
        acc = acc + pltpu.roll(xe, 2, axis=0)[8:8 + C] * cwa_ref[1:2, lo:hi]
        acc = acc + pltpu.roll(xe, 1, axis=0)[8:8 + C] * cwa_ref[2:3, lo:hi]
        acc = acc + xe[8:8 + C] * cwa_ref[3:4, lo:hi]
        return _silu(acc)

    kn = [_head_l2norm(conv_sec(nb, QK_W), 1.0) for nb in nbs]
    qn = [_head_l2norm(conv_sec(nb, 0), HEAD ** -0.5) for nb in nbs]
    vv = [conv_sec(nb, 2 * QK_W) for nb in nbs]
    for nb in nbs:
        xa_ref[nb, 0:8, :] = xa_ref[nb, C:C + 8, :]

    bts = [ba_ref[nb] for nb in nbs]
    beta_all = [jax.nn.sigmoid(bt) for bt in bts]
    g_all = [-(jnp.exp(alog_ref[...]) * _softplus(bt + dtb_ref[...])) for bt in bts]
    gc_small = [_dot_rsplit(ltri, ga) for ga in g_all]
    gl_small = [gc[C - 1:C, :] for gc in gc_small]

    k_st, q_st, kb, vb, kbg, qd, kd, gc_col = ({} for _ in range(8))
    for u in units:
        nb, g = u
        hs = heads(g)
        k_st[u] = jnp.concatenate([kn[nb][h] for h in hs], axis=0)
        q_st[u] = jnp.concatenate([qn[nb][h] for h in hs], axis=0)
        v_st = jnp.concatenate([vv[nb][:, h * HEAD:(h + 1) * HEAD] for h in hs], axis=0)
        beta_col = jnp.concatenate([beta_all[nb][:, h:h + 1] for h in hs], axis=0)
        gc_col[u] = jnp.concatenate([gc_small[nb][:, 8 + h:9 + h] for h in hs], axis=0)
        gl_col = jnp.concatenate(
            [jnp.broadcast_to(gl_small[nb][:, 8 + h:9 + h], (C, 1)) for h in hs], axis=0)
        kb[u] = k_st[u] * beta_col
        vb[u] = v_st * beta_col
        egc = jnp.exp(gc_col[u])
        kbg[u] = kb[u] * egc
        qd[u] = q_st[u] * egc
        kd[u] = k_st[u] * jnp.exp(gl_col - gc_col[u])

    gx = {u: _dot_lsplit(gc_small[u[0]], e64_ref[u[1]]) for u in units}
    crow = {u: _dot_rsplit(ones8, jnp.where(ident_t, gx[u], 0.0))[0:1, :] for u in units}
    a = {u: _dot_nt(jnp.concatenate([kb[u], q_st[u]], axis=0).astype(bf16), k_st[u].astype(bf16))
         for u in units}
    dec = {u: jnp.where(incl, jnp.exp(jnp.where(incl, gc_col[u] - crow[u], 0.0)), 0.0) for u in units}
    nm = {u: jnp.where(strict, -(a[u][0:R] * dec[u]), 0.0) for u in units}
    qkm = {u: a[u][R:2 * R] * dec[u] for u in units}

    p = {u: eye + nm[u] for u in units}
    nk = {}
    for u in units:
        nmb = nm[u].astype(bf16)
        nk[u] = _dot(nmb, nmb)
    for _ in range(4):
        for u in units:
            x = _dot(jnp.concatenate([p[u], nk[u]], axis=0).astype(bf16), nk[u].astype(bf16))
            p[u] = p[u] + x[0:R]
            nk[u] = x[R:2 * R]
    for u in units:
        p[u] = p[u] + _dot(p[u].astype(bf16), nk[u].astype(bf16))
    uw = {u: _dot(p[u].astype(bf16), jnp.concatenate([vb[u], kbg[u]], axis=1).astype(bf16)) for u in units}

    ws = {}
    for u in units:
        nb, g = u
        for j, h in enumerate(heads(g)):
            sh = s_ref[nb, :, h * HEAD:(h + 1) * HEAD]
            lhs = jnp.concatenate([uw[u][j * C:(j + 1) * C, HEAD:2 * HEAD], qd[u][j * C:(j + 1) * C]], axis=0)
            ws[u, j] = _dot(lhs.astype(bf16), sh.astype(bf16))
    o_heads = {}
    for u in units:
        nb, g = u
        vnew_st = jnp.concatenate([uw[u][j * C:(j + 1) * C, 0:HEAD] - ws[u, j][0:C] for j in range(G)], axis=0)
        o_st = (jnp.concatenate([ws[u, j][C:2 * C] for j in range(G)], axis=0)
                + _dot(qkm[u].astype(bf16), vnew_st.astype(bf16)))
        vbd = jnp.where(bdmask, jnp.concatenate([vnew_st] * G, axis=1), 0.0)
        lo = g * G * HEAD
        hi = lo + G * HEAD
        gl_row = jnp.concatenate(
            [jnp.broadcast_to(jnp.exp(gl_small[nb][:, 8 + h:9 + h]), (1, HEAD)) for h in heads(g)], axis=1)
        s_ref[nb, :, lo:hi] = s_ref[nb, :, lo:hi] * gl_row + _dot(kd[u].T.astype(bf16), vbd.astype(bf16))
        for j, h in enumerate(heads(g)):
            o_heads[nb, h] = o_st[j * C:(j + 1) * C]

    for nb in nbs:
        zt = z_ref[nb]
        for h in range(N_HEADS):
            oh = o_heads[nb, h]
            ms = jnp.mean(oh * oh, axis=-1, keepdims=True)
            zh = zt[:, h * HEAD:(h + 1) * HEAD]
            on = oh * lax.rsqrt(ms + EPS) * onw_ref[...] * _silu(zh)
            o_ref[nb, :, h * HEAD:(h + 1) * HEAD] = on.astype(bf16)

    for nb in nbs:
        bcx = bcx_ref[nb]
        cx = bcx[:, SC_W:2 * SC_W] * bcx[:, 2 * SC_W:3 * SC_W]
        xb_ref[nb, 8:8 + C, :] = cx
        ce = xb_ref[nb]
        cv = pltpu.roll(ce, 2, axis=0)[8:8 + C] * cwb_ref[0:1, :]
        cv = cv + pltpu.roll(ce, 1, axis=0)[8:8 + C] * cwb_ref[1:2, :]
        cv = cv + cx * cwb_ref[2:3, :]
        y_ref[nb] = (bcx[:, 0:SC_W] * cv).astype(bf16)
        xb_ref[nb, 0:8, :] = xb_ref[nb, C:C + 8, :]

    @pl.when(t == nt - 1)
    def _():
        for nb in range(nb_step):
            for h in range(N_HEADS):
                snew_ref[nb, h] = s_ref[nb, :, h * HEAD:(h + 1) * HEAD]
            nca_ref[nb] = xa_ref[nb, 5:8, :]
            ncb_ref[nb] = xb_ref[nb, 6:8, :]


def _expand_consts():
    lane = jnp.arange(BA_W)[:, None]
    col = jnp.arange(QK_W)[None, :]
    eb = (lane == (col >> 7)).astype(bf16)
    eg = (lane == (8 + (col >> 7))).astype(bf16)
    col64 = jnp.arange(GROUP_HEADS * CHUNK)[None, :]
    e64 = jnp.stack([(lane == (8 + g * GROUP_HEADS + (col64 >> 6))).astype(bf16)
                     for g in range(N_HEADS // GROUP_HEADS)], axis=0)
    return eb, eg, e64


def _delta_prompt(proj3, cwa, cwb, alog_row, dtb_row, onw_row, nb_step):
    b, t, _ = proj3.shape
    assert t % CHUNK == 0 and b % nb_step == 0
    _, _, e64 = _expand_consts()
    c = CHUNK
    const2 = lambda bi, ti: (0, 0)
    outs = pl.pallas_call(
        functools.partial(_delta_prompt_kernel, nb_step=nb_step),
        out_shape=(jax.ShapeDtypeStruct((b, t, QK_W), bf16),
                   jax.ShapeDtypeStruct((b, t, SC_W), bf16),
                   jax.ShapeDtypeStruct((b, N_HEADS, HEAD, HEAD), f32),
                   jax.ShapeDtypeStruct((b, CONV_A - 1, QKV_W), f32),
                   jax.ShapeDtypeStruct((b, CONV_B - 1, SC_W), f32)),
        grid=(b // nb_step, t // c),
        in_specs=[pl.BlockSpec((nb_step, c, QKV_W), lambda bi, ti: (bi, ti, COL_QKV // QKV_W)),
                  pl.BlockSpec((nb_step, c, QKV_W), lambda bi, ti: (bi, ti, COL_BCX // QKV_W)),
                  pl.BlockSpec((nb_step, c, QK_W), lambda bi, ti: (bi, ti, COL_Z // QK_W)),
                  pl.BlockSpec((nb_step, c, BA_W), lambda bi, ti: (bi, ti, COL_BA // BA_W)),
                  pl.BlockSpec((CONV_A, QKV_W), const2),
                  pl.BlockSpec((CONV_B, SC_W), const2),
                  pl.BlockSpec((1, BA_W), const2),
                  pl.BlockSpec((1, BA_W), const2),
                  pl.BlockSpec((1, HEAD), const2),
                  pl.BlockSpec((N_HEADS // GROUP_HEADS, BA_W, GROUP_HEADS * CHUNK), lambda bi, ti: (0, 0, 0))],
        out_specs=(pl.BlockSpec((nb_step, c, QK_W), lambda bi, ti: (bi, ti, 0)),
                   pl.BlockSpec((nb_step, c, SC_W), lambda bi, ti: (bi, ti, 0)),
                   pl.BlockSpec((nb_step, N_HEADS, HEAD, HEAD), lambda bi, ti: (bi, 0, 0, 0)),
                   pl.BlockSpec((nb_step, CONV_A - 1, QKV_W), lambda bi, ti: (bi, 0, 0)),
                   pl.BlockSpec((nb_step, CONV_B - 1, SC_W), lambda bi, ti: (bi, 0, 0))),
        scratch_shapes=[pltpu.VMEM((nb_step, HEAD, QK_W), f32),
                        pltpu.VMEM((nb_step, 8 + c, QKV_W), f32),
                        pltpu.VMEM((nb_step, 8 + c, SC_W), f32)],
        compiler_params=_cparams(("arbitrary", "arbitrary")),
        name="delta_prompt",
    )(proj3, proj3, proj3, proj3, cwa, cwb, alog_row, dtb_row, onw_row, e64)
    return outs


def _sample_prep_kernel(p_ref, bufa_ref, bufb_ref, cwa_ref, cwb_ref, alog_ref, dtb_ref, eb_ref, eg_ref,
                        q_ref, k_ref, v_ref, beta_ref, eg_out_ref, y_ref, nbufa_ref, nbufb_ref):
    def conv_sec(lo):
        hi = lo + QK_W
        raw = p_ref[:, COL_QKV + lo:COL_QKV + hi]
        acc = bufa_ref[0, :, lo:hi] * cwa_ref[0:1, lo:hi]
        acc = acc + bufa_ref[1, :, lo:hi] * cwa_ref[1:2, lo:hi]
        acc = acc + bufa_ref[2, :, lo:hi] * cwa_ref[2:3, lo:hi]
        acc = acc + raw * cwa_ref[3:4, lo:hi]
        nbufa_ref[0, :, lo:hi] = bufa_ref[1, :, lo:hi]
        nbufa_ref[1, :, lo:hi] = bufa_ref[2, :, lo:hi]
        nbufa_ref[2, :, lo:hi] = raw
        return _silu(acc)

    qn = _head_l2norm(conv_sec(0), HEAD ** -0.5)
    kn = _head_l2norm(conv_sec(QK_W), 1.0)
    for h in range(N_HEADS):
        q_ref[:, h * HEAD:(h + 1) * HEAD] = qn[h]
        k_ref[:, h * HEAD:(h + 1) * HEAD] = kn[h]
    v_ref[...] = conv_sec(2 * QK_W)

    bt = p_ref[:, COL_BA:COL_BA + BA_W]
    beta_all = jax.nn.sigmoid(bt)
    g_all = -(jnp.exp(alog_ref[...]) * _softplus(bt + dtb_ref[...]))
    beta_ref[...] = _dot_lsplit(beta_all, eb_ref[...])
    eg_out_ref[...] = jnp.exp(_dot_lsplit(g_all, eg_ref[...]))

    bg = p_ref[:, COL_BCX:COL_BCX + SC_W]
    cx = p_ref[:, COL_BCX + SC_W:COL_BCX + 2 * SC_W] * p_ref[:, COL_BCX + 2 * SC_W:COL_BCX + 3 * SC_W]
    cv = bufb_ref[0] * cwb_ref[0:1, :]
    cv = cv + bufb_ref[1] * cwb_ref[1:2, :]
    cv = cv + cx * cwb_ref[2:3, :]
    y_ref[...] = (bg * cv).astype(bf16)
    nbufb_ref[0] = bufb_ref[1]
    nbufb_ref[1] = cx


def _sample_prep(proj_s, bufa_t, bufb_t, cwa, cwb, alog_row, dtb_row):
    n = proj_s.shape[0]
    eb, eg, _ = _expand_consts()
    row = jax.ShapeDtypeStruct((n, QK_W), f32)
    return pl.pallas_call(
        _sample_prep_kernel,
        out_shape=(row, row, row, row, row,
                   jax.ShapeDtypeStruct((n, SC_W), bf16),
                   jax.ShapeDtypeStruct((CONV_A - 1, n, QKV_W), f32),
                   jax.ShapeDtypeStruct((CONV_B - 1, n, SC_W), f32)),
        compiler_params=pltpu.CompilerParams(vmem_limit_bytes=VMEM_LIMIT),
        name="sample_prep",
    )(proj_s, bufa_t, bufb_t, cwa, cwb, alog_row, dtb_row, eb, eg)


def _sample_step_kernel(s_ref, q_ref, k_ref, v_ref, beta_ref, eg_ref, z_ref, onw_ref,
                        snew_ref, o_ref, *, bb):
    w = N_HEADS * HEAD
    r8 = lax.broadcasted_iota(i32, (N_HEADS, w), 0)
    c8 = lax.broadcasted_iota(i32, (N_HEADS, w), 1)
    mask8 = r8 == (c8 >> 7)
    zpad_k = jnp.zeros((HEAD - N_HEADS, HEAD), f32)
    zpad_d = jnp.zeros((HEAD - N_HEADS, w), f32)
    for b in range(bb):
        s_all = jnp.concatenate([s_ref[b, h] for h in range(N_HEADS)], axis=1)
        eg8 = eg_ref[b]
        eg_row = jnp.concatenate([eg8[h:h + 1, :] for h in range(N_HEADS)], axis=1)
        s_dec = s_all * eg_row
        k8 = k_ref[b]
        x = _dot(k8.astype(bf16), s_dec.astype(bf16))
        v_t = jnp.concatenate([v_ref[b]] * N_HEADS, axis=1)
        b_t = jnp.concatenate([beta_ref[b]] * N_HEADS, axis=1)
        d_bd = jnp.where(mask8, (v_t - x) * b_t, 0.0)
        kt = jnp.concatenate([k8, zpad_k], axis=0).T
        d_pad = jnp.concatenate([d_bd, zpad_d], axis=0)
        k_hi, k_lo = _split(kt, 2)
        d_hi, d_lo = _split(d_pad, 2)
        s_new = s_dec + (_dot(k_hi, d_hi) + _dot(k_hi, d_lo) + _dot(k_lo, d_hi))
        yv = jnp.where(mask8, _dot(q_ref[b].astype(bf16), s_new.astype(bf16)), 0.0)
        o8 = yv[:, 0:HEAD]
        for j in range(1, N_HEADS):
            o8 = o8 + yv[:, j * HEAD:(j + 1) * HEAD]
        ms = jnp.mean(o8 * o8, axis=-1, keepdims=True)
        o_ref[b] = o8 * lax.rsqrt(ms + EPS) * onw_ref[...] * _silu(z_ref[b])
        for h in range(N_HEADS):
            snew_ref[b, h] = s_new[:, h * HEAD:(h + 1) * HEAD]


def _sample_step(state, q, k, v, beta, eg, z, onw_row, bb=4):
    n = state.shape[0]
    assert n % bb == 0
    hspec = pl.BlockSpec((bb, N_HEADS, HEAD), lambda i: (i, 0, 0))
    sspec = pl.BlockSpec((bb, N_HEADS, HEAD, HEAD), lambda i: (i, 0, 0, 0))
    return pl.pallas_call(
        functools.partial(_sample_step_kernel, bb=bb),
        out_shape=(jax.ShapeDtypeStruct(state.shape, f32),
                   jax.ShapeDtypeStruct((n, N_HEADS, HEAD), f32)),
        grid=(n // bb,),
        in_specs=[sspec, hspec, hspec, hspec, hspec, hspec, hspec, pl.BlockSpec((1, HEAD), lambda i: (0, 0))],
        out_specs=(sspec, hspec),
        compiler_params=_cparams(("arbitrary",)),
        name="sample_step",
    )(state, q, k, v, beta, eg, z, onw_row)


def _mix_route_kernel(x_ref, o_ref, y_ref, ga_ref, gb_ref, wa_ref, wb_ref, wo_ref, n2_ref,
                      rwh_ref, rwl_ref, rb_ref, cnt_in_ref, x1_ref, h2_ref, mi_ref, mw_ref, cnt_ref):
    i = pl.program_id(0)
    tm = x_ref.shape[0]

    @pl.when(i == 0)
    def _():
        cnt_ref[...] = cnt_in_ref[...]

    oa = _dot(o_ref[...], wa_ref[...])
    ob = _dot(y_ref[...], wb_ref[...])
    merged = jax.nn.sigmoid(ga_ref[...]) * oa + jax.nn.sigmoid(gb_ref[...]) * ob
    x1 = x_ref[...] + _dot(merged.astype(bf16), wo_ref[...])
    x1_ref[...] = x1
    ms = jnp.mean(x1 * x1, axis=-1, keepdims=True)
    h2 = x1 * lax.rsqrt(ms + EPS) * n2_ref[...]
    h2_ref[...] = h2

    h_hi, h_lo = _split(h2, 2)
    logits = _dot(h_hi, rwh_ref[...]) + _dot(h_hi, rwl_ref[...]) + _dot(h_lo, rwh_ref[...]) + rb_ref[...]

    lane = lax.broadcasted_iota(i32, (tm, LANE), 1)
    lanef = lane.astype(f32)
    neg = jnp.float32(-jnp.inf)
    big = jnp.float32(1e9)
    gmask = (lane >= N_EXPERTS) & (lane < N_EXPERTS + N_GROUPS)
    gl = jnp.where(gmask, logits, neg)
    gmax = jnp.max(gl, axis=-1, keepdims=True)
    gidx = jnp.min(jnp.where(gl == gmax, lanef - N_EXPERTS, big), axis=-1, keepdims=True)
    gsum = jnp.sum(jnp.where(gmask, jnp.exp(gl - gmax), 0.0), axis=-1, keepdims=True)
    gprob = 1.0 / gsum

    emask = (lane < N_EXPERTS) & ((lane >> 3).astype(f32) == gidx)
    el = jnp.where(emask, logits, neg)
    emax = jnp.max(el, axis=-1, keepdims=True)
    pe = jnp.where(emask, jnp.exp(el - emax), 0.0)
    eprob = pe / jnp.sum(pe, axis=-1, keepdims=True)
    p1m = jnp.where(emask, eprob, -1.0)
    m1 = jnp.max(p1m, axis=-1, keepdims=True)
    i1 = jnp.min(jnp.where(p1m == m1, lanef, big), axis=-1, keepdims=True)
    p2m = jnp.where(lanef == i1, -1.0, p1m)
    m2 = jnp.max(p2m, axis=-1, keepdims=True)
    i2 = jnp.min(jnp.where(p2m == m2, lanef, big), axis=-1, keepdims=True)
    tot = m1 + m2
    c1 = m1 / tot * gprob
    c2 = m2 / tot * gprob

    oh1 = jnp.where(lanef == i1, 1.0, 0.0)
    oh2 = jnp.where(lanef == i2, 1.0, 0.0)
    ohs = oh1 + oh2
    rt = lax.broadcasted_iota(i32, (tm, tm), 0)
    ct = lax.broadcasted_iota(i32, (tm, tm), 1)
    lstrict = jnp.where(rt > ct, 1.0, 0.0).astype(bf16)
    cs = _dot(lstrict, ohs.astype(bf16)) + cnt_ref[...]
    rank1 = jnp.sum(cs * oh1, axis=-1, keepdims=True)
    rank2 = jnp.sum(cs * oh2, axis=-1, keepdims=True)
    cnt_ref[...] = cnt_ref[...] + jnp.sum(ohs, axis=0, keepdims=True)

    mi = jnp.where(lane == 0, i1, jnp.where(lane == 1, i2, jnp.where(lane == 2, rank1,
                                                                     jnp.where(lane == 3, rank2, 0.0))))
    mi_ref[...] = mi.astype(i32)
    mw_ref[...] = jnp.where(lane == 0, c1, jnp.where(lane == 1, c2, 0.0))


def _mix_route(x2d, o2d, y2d, proj2d, wa, wb, wo, n2_row, rwh, rwl, rb_row, cnt_in):
    n = x2d.shape[0]
    tm = min(256, n)
    assert n % tm == 0
    tok = lambda width: pl.BlockSpec((tm, width), lambda i: (i, 0))
    full = lambda a: pl.BlockSpec(a.shape, lambda i: (0,) * a.ndim)
    in_specs = [tok(D_MODEL), tok(QK_W), tok(SC_W),
                pl.BlockSpec((tm, D_MODEL), lambda i: (i, COL_GA // D_MODEL)),
                pl.BlockSpec((tm, D_MODEL), lambda i: (i, COL_GB // D_MODEL)),
                full(wa), full(wb), full(wo), full(n2_row), full(rwh), full(rwl), full(rb_row), full(cnt_in)]
    out_shape = (jax.ShapeDtypeStruct((n, D_MODEL), f32),
                 jax.ShapeDtypeStruct((n, D_MODEL), f32),
                 jax.ShapeDtypeStruct((n, LANE), i32),
                 jax.ShapeDtypeStruct((n, LANE), f32),
                 jax.ShapeDtypeStruct((1, LANE), f32))
    out_specs = (tok(D_MODEL), tok(D_MODEL), tok(LANE), tok(LANE),
                 pl.BlockSpec((1, LANE), lambda i: (0, 0)))
    return pl.pallas_call(
        _mix_route_kernel,
        out_shape=out_shape,
        grid=(n // tm,),
        in_specs=in_specs,
        out_specs=out_specs,
        compiler_params=_cparams(("arbitrary",)),
        name="mix_route",
    )(x2d, o2d, y2d, proj2d, proj2d, wa, wb, wo, n2_row, rwh, rwl, rb_row, cnt_in)


MI_W = 4
ROW_DMA_UNROLL = 8


def _dest_row(mi_ref, starts_ref, r, k):
    return starts_ref[mi_ref[MI_W * r + k]] + mi_ref[MI_W * r + 2 + k]


def _dispatch_kernel(mi_ref, starts_ref, hp_ref, hs_ref, xs_ref, sem, *, np_tiles, tm):
    i = pl.program_id(0)

    def scatter_rows(h_ref):
        def copy(r, k):
            return pltpu.make_async_copy(h_ref.at[pl.ds(r, 1)],
                                         xs_ref.at[pl.ds(_dest_row(mi_ref, starts_ref, r, k), 1)], sem)

        def start(r, c):
            for k in range(2):
                copy(r, k).start(priority=k)
            return c

        lax.fori_loop(0, tm, start, 0, unroll=ROW_DMA_UNROLL)
        for k in range(2):
            pltpu.make_async_copy(h_ref, xs_ref.at[pl.ds(0, tm)], sem).wait()

    @pl.when(i < np_tiles)
    def _():
        scatter_rows(hp_ref)

    @pl.when(i >= np_tiles)
    def _():
        scatter_rows(hs_ref)


def _dispatch(h2_p, h2_s, mi_flat, starts):
    tm = MOE_ROWS
    n_p, n_s = h2_p.shape[0], h2_s.shape[0]
    assert n_p % tm == 0 and n_s == tm
    np_tiles = n_p // tm
    return pl.pallas_call(
        functools.partial(_dispatch_kernel, np_tiles=np_tiles, tm=tm),
        out_shape=jax.ShapeDtypeStruct((2 * (n_p + n_s), D_MODEL), f32),
        grid=(np_tiles + 1,),
        in_specs=[pl.BlockSpec((MI_W * tm,), lambda i: (i,), memory_space=pltpu.SMEM),
                  pl.BlockSpec(memory_space=pltpu.SMEM),
                  pl.BlockSpec((tm, D_MODEL), lambda i: (jnp.minimum(i, np_tiles - 1), 0)),
                  pl.BlockSpec((tm, D_MODEL), lambda i: (0, 0))],
        out_specs=pl.BlockSpec(memory_space=pl.ANY),
        scratch_shapes=[pltpu.SemaphoreType.DMA(())],
        compiler_params=_cparams(("arbitrary",)),
        name="moe_dispatch",
    )(mi_flat, starts, h2_p, h2_s)


def _cast_rows(src_ref, dst_ref, rows=256):
    def body(r, c):
        sl = pl.ds(pl.multiple_of(r * rows, rows), rows)
        dst_ref[sl, :] = src_ref[sl, :].astype(bf16)
        return c
    lax.fori_loop(0, dst_ref.shape[0] // rows, body, 0)


def _moe_kernel(blk_ref, lo_ref, hi_ref, first_ref, newe_ref, slot_ref, pre_ref, init_ref,
                x_ref, wg_hbm, wu_hbm, wd_hbm, o_ref,
                wg_f, wu_f, wd_f, wg_b, wu_b, wd_b, sem):
    i = pl.program_id(0)
    lo = lo_ref[i]
    hi = hi_ref[i]

    def weight_copies(e, slot):
        return [pltpu.make_async_copy(wg_hbm.at[e], wg_f.at[slot], sem.at[slot, 0]),
                pltpu.make_async_copy(wu_hbm.at[e], wu_f.at[slot], sem.at[slot, 1]),
                pltpu.make_async_copy(wd_hbm.at[e], wd_f.at[slot], sem.at[slot, 2])]

    def start_weights(e, slot):
        for cp, prio in zip(weight_copies(e, slot), (0, 1, 1)):
            cp.start(priority=prio)

    @pl.when(i == 0)
    def _():
        start_weights(init_ref[0], 0)

        @pl.when(init_ref[1] >= 0)
        def _():
            start_weights(init_ref[1], 1)

    @pl.when(newe_ref[i] == 1)
    def _():
        slot = slot_ref[i]
        cg, cu, cd = weight_copies(0, slot)
        cg.wait()
        _cast_rows(wg_f.at[slot], wg_b)
        cu.wait()
        _cast_rows(wu_f.at[slot], wu_b)
        cd.wait()
        _cast_rows(wd_f.at[slot], wd_b)

        @pl.when(pre_ref[i] >= 0)
        def _():
            start_weights(pre_ref[i], slot)

    @pl.when(hi > lo)
    def _():
        x = x_ref[...].astype(bf16)
        a = _dot(x, wg_b[...])
        u = _dot(x, wu_b[...])
        y = _dot((_silu(a) * u).astype(bf16), wd_b[...])
        row = lax.broadcasted_iota(i32, y.shape, 0)
        ym = jnp.where((row >= lo) & (row < hi), y, 0.0)

        @pl.when(first_ref[i] == 1)
        def _():
            o_ref[...] = ym

        @pl.when(first_ref[i] == 0)
        def _():
            o_ref[...] = o_ref[...] + ym


def _moe(xs, w_gate, w_up, w_down, items):
    n_items = items[0].shape[0]
    rows = xs.shape[0]
    n_pref = len(items)
    xmap = lambda i, blk, *_: (blk[i], 0)
    grid_spec = pltpu.PrefetchScalarGridSpec(
        num_scalar_prefetch=n_pref,
        grid=(n_items,),
        in_specs=[pl.BlockSpec((MOE_ROWS, D_MODEL), xmap),
                  pl.BlockSpec(memory_space=pl.ANY),
                  pl.BlockSpec(memory_space=pl.ANY),
                  pl.BlockSpec(memory_space=pl.ANY)],
        out_specs=pl.BlockSpec((MOE_ROWS, D_MODEL), xmap),
        scratch_shapes=[pltpu.VMEM((2, D_MODEL, D_FF), f32), pltpu.VMEM((2, D_MODEL, D_FF), f32),
                        pltpu.VMEM((2, D_FF, D_MODEL), f32),
                        pltpu.VMEM((D_MODEL, D_FF), bf16), pltpu.VMEM((D_MODEL, D_FF), bf16),
                        pltpu.VMEM((D_FF, D_MODEL), bf16),
                        pltpu.SemaphoreType.DMA((2, 3))],
    )
    return pl.pallas_call(
        _moe_kernel,
        out_shape=jax.ShapeDtypeStruct((rows, D_MODEL), f32),
        grid_spec=grid_spec,
        compiler_params=_cparams(("arbitrary",)),
        name="moe_experts",
    )(*items, xs, w_gate, w_up, w_down)


def _combine_kernel(mi_ref, mi_next_ref, starts_ref, x1p_ref, mwp_ref, x1s_ref, mws_ref, fnw_ref, ys_ref,
                    yp_ref, ysm_ref, g_ref, sem, *, np_tiles):
    i = pl.program_id(0)
    n = pl.num_programs(0)
    tm = x1p_ref.shape[0]
    slot = lax.rem(i, 2)

    def gather_rows(m_ref, dst_slot):
        def body(r, c):
            for k in range(2):
                pltpu.make_async_copy(ys_ref.at[pl.ds(_dest_row(m_ref, starts_ref, r, k), 1)],
                                      g_ref.at[dst_slot, k, pl.ds(r, 1)], sem.at[dst_slot]).start(priority=k)
            return c
        lax.fori_loop(0, tm, body, 0, unroll=ROW_DMA_UNROLL)

    @pl.when(i == 0)
    def _():
        gather_rows(mi_ref, 0)

    @pl.when(i + 1 < n)
    def _():
        gather_rows(mi_next_ref, 1 - slot)

    for k in range(2):
        pltpu.make_async_copy(ys_ref.at[pl.ds(0, tm)], g_ref.at[slot, k], sem.at[slot]).wait()

    def finish(x1_ref, mw_ref, out_ref):
        mw = mw_ref[...]
        x2 = x1_ref[...] + (g_ref[slot, 0] * mw[:, 0:1] + g_ref[slot, 1] * mw[:, 1:2])
        ms = jnp.mean(x2 * x2, axis=-1, keepdims=True)
        out_ref[...] = x2 * lax.rsqrt(ms + EPS) * fnw_ref[...]

    @pl.when(i < np_tiles)
    def _():
        finish(x1p_ref, mwp_ref, yp_ref)

    @pl.when(i >= np_tiles)
    def _():
        finish(x1s_ref, mws_ref, ysm_ref)


def _combine(x1_p, mw_p, x1_s, mw_s, fnw_row, ys, mi_flat, starts):
    tm = MOE_ROWS
    n_p, n_s = x1_p.shape[0], x1_s.shape[0]
    assert n_p % tm == 0 and n_s == tm
    np_tiles = n_p // tm
    ptile = lambda width: pl.BlockSpec((tm, width), lambda i: (jnp.minimum(i, np_tiles - 1), 0))
    stile = lambda width: pl.BlockSpec((tm, width), lambda i: (0, 0))
    return pl.pallas_call(
        functools.partial(_combine_kernel, np_tiles=np_tiles),
        out_shape=(jax.ShapeDtypeStruct((n_p, D_MODEL), f32),
                   jax.ShapeDtypeStruct((n_s, D_MODEL), f32)),
        grid=(np_tiles + 1,),
        in_specs=[pl.BlockSpec((MI_W * tm,), lambda i: (i,), memory_space=pltpu.SMEM),
                  pl.BlockSpec((MI_W * tm,), lambda i: (jnp.minimum(i + 1, np_tiles),), memory_space=pltpu.SMEM),
                  pl.BlockSpec(memory_space=pltpu.SMEM),
                  ptile(D_MODEL), ptile(LANE), stile(D_MODEL), stile(LANE),
                  pl.BlockSpec((1, D_MODEL), lambda i: (0, 0)),
                  pl.BlockSpec(memory_space=pl.ANY)],
        out_specs=(ptile(D_MODEL), stile(D_MODEL)),
        scratch_shapes=[pltpu.VMEM((2, 2, tm, D_MODEL), f32), pltpu.SemaphoreType.DMA((2,))],
        compiler_params=_cparams(("arbitrary",)),
        name="moe_combine",
    )(mi_flat, mi_flat, starts, x1_p, mw_p, x1_s, mw_s, fnw_row, ys)


def _work_items(counts, n_rows):
    nblk = n_rows // MOE_ROWS
    n_items = nblk + N_EXPERTS - 1
    ends = jnp.cumsum(counts)
    starts = ends - counts
    first_blk = starts // MOE_ROWS
    last_blk = jnp.maximum(ends - 1, 0) // MOE_ROWS
    nvis = jnp.where(counts > 0, last_blk - first_blk + 1, 0)
    vis_end = jnp.cumsum(nvis)
    vis_start = vis_end - nvis
    total = vis_end[-1]
    idx = jnp.arange(n_items, dtype=i32)
    e = jnp.minimum(jnp.sum((vis_end[None, :] <= idx[:, None]).astype(i32), axis=1), N_EXPERTS - 1)
    onehot = (e[:, None] == jnp.arange(N_EXPERTS, dtype=i32)[None, :]).astype(i32)
    look = lambda tbl: jnp.sum(onehot * tbl[None, :], axis=1)
    blk = look(first_blk) + idx - look(vis_start)
    lo = jnp.maximum(look(starts), blk * MOE_ROWS) - blk * MOE_ROWS
    hi = jnp.minimum(look(ends), (blk + 1) * MOE_ROWS) - blk * MOE_ROWS
    valid = idx < total
    blk = jnp.where(valid, blk, nblk - 1).astype(i32)
    lo = jnp.where(valid, lo, 0).astype(i32)
    hi = jnp.where(valid, hi, 0).astype(i32)
    prev_blk = jnp.concatenate([jnp.full((1,), -1, i32), blk[:-1]])
    first = (valid & (blk != prev_blk)).astype(i32)
    prev_e = jnp.concatenate([jnp.full((1,), -1, i32), e[:-1]])
    newe = (valid & (e != prev_e)).astype(i32)
    order = jnp.cumsum(newe) - 1
    slot = jnp.where(newe == 1, order % 2, 0).astype(i32)
    cum_act = jnp.cumsum((counts > 0).astype(i32))
    n_uniq = cum_act[-1]
    kk = jnp.arange(N_EXPERTS + 2, dtype=i32)
    uniq_e = jnp.sum((cum_act[None, :] <= kk[:, None]).astype(i32), axis=1)
    uniq_e = jnp.where(kk < n_uniq, uniq_e, -1)
    ahead = jnp.sum((kk[None, :] == (order + 2)[:, None]).astype(i32) * uniq_e[None, :], axis=1)
    pre = jnp.where(newe == 1, ahead, -1).astype(i32)
    init = uniq_e[0:2].astype(i32)
    return starts.astype(i32), (blk, lo, hi, first, newe, slot, pre, init)


def kernel(x_prompt, x_sample, state_delta, state_qkv_conv, state_short_conv, norm1_w, w_in, conv_a_w, a_log, dt_bias, out_norm_w, w_branch_a, conv_b_w, w_branch_b, w_o, norm2_w, router_group_w, router_group_b, router_expert_w, router_expert_b, w_gate, w_up, w_down, final_norm_w):
    assert norm1_w.shape[0] == 1, "single-layer trunk"
    bp, tp, d = x_prompt.shape
    bs, ts, _ = x_sample.shape
    assert d == D_MODEL and ts == 1
    n_p = bp * tp
    n_s = bs
    n_all = n_p + n_s

    w_perm = _wprep(jnp.transpose(w_in[0]))
    wa = w_branch_a[0].astype(bf16)
    wb = w_branch_b[0].astype(bf16)
    wo = w_o[0].astype(bf16)
    pad = lambda v: jnp.zeros((1, BA_W), f32).at[0, N_HEADS:2 * N_HEADS].set(v)
    alog_row = pad(a_log[0])
    dtb_row = pad(dt_bias[0])
    onw_row = out_norm_w[0].reshape(1, HEAD)
    cwa = conv_a_w[0]
    cwb = conv_b_w[0]
    rw = jnp.zeros((D_MODEL, LANE), f32)
    rw = rw.at[:, 0:N_EXPERTS].set(router_expert_w[0]).at[:, N_EXPERTS:N_EXPERTS + N_GROUPS].set(router_group_w[0])
    rwh = rw.astype(bf16)
    rwl = (rw - rwh.astype(f32)).astype(bf16)
    rb_row = jnp.zeros((1, LANE), f32)
    rb_row = rb_row.at[0, 0:N_EXPERTS].set(router_expert_b[0]).at[0, N_EXPERTS:N_EXPERTS + N_GROUPS].set(router_group_b[0])
    n2_row = norm2_w[0].reshape(1, D_MODEL)

    xp2 = x_prompt.reshape(n_p, D_MODEL)
    proj_p = _inproj(xp2, norm1_w[0], w_perm)
    o_p, y_p, sd_p, nca_p, ncb_p = _delta_prompt(proj_p.reshape(bp, tp, PROJ_W), cwa, cwb, alog_row, dtb_row,
                                                 onw_row, nb_step=4 if bp % 4 == 0 else (2 if bp % 2 == 0 else 1))
    cnt0 = jnp.zeros((1, LANE), f32)
    x1_p, h2_p, mi_p, mw_p, cnt_p = _mix_route(xp2, o_p.reshape(n_p, QK_W), y_p.reshape(n_p, SC_W), proj_p,
                                               wa, wb, wo, n2_row, rwh, rwl, rb_row, cnt0)

    xs2 = x_sample.reshape(n_s, D_MODEL)
    proj_s = _inproj(xs2, norm1_w[0], w_perm)
    bufa_t = jnp.transpose(state_qkv_conv[0], (1, 0, 2))
    bufb_t = jnp.transpose(state_short_conv[0], (1, 0, 2))
    q_s, k_s, v_s, beta_s, eg_s, y_s, nbufa_t, nbufb_t = _sample_prep(proj_s, bufa_t, bufb_t, cwa, cwb,
                                                                      alog_row, dtb_row)
    h3 = lambda a: a.reshape(n_s, N_HEADS, HEAD)
    z_s = proj_s[:, COL_Z:COL_Z + QK_W]
    sd_s, o_s = _sample_step(state_delta[0], h3(q_s), h3(k_s), h3(v_s), h3(beta_s), h3(eg_s), h3(z_s), onw_row)
    o_s2 = o_s.reshape(n_s, QK_W).astype(bf16)
    x1_s, h2_s, mi_s, mw_s, cnt = _mix_route(xs2, o_s2, y_s, proj_s, wa, wb, wo, n2_row, rwh, rwl, rb_row, cnt_p)

    counts = cnt[0, 0:N_EXPERTS].astype(i32)
    starts, items = _work_items(counts, 2 * n_all)
    mi_flat = jnp.concatenate([mi_p[:, 0:MI_W], mi_s[:, 0:MI_W]], axis=0).reshape(MI_W * n_all)
    xs_sorted = _dispatch(h2_p, h2_s, mi_flat, starts)
    ys = _moe(xs_sorted, w_gate[0], w_up[0], w_down[0], items)
    y_prompt, y_sample = _combine(x1_p, mw_p, x1_s, mw_s, final_norm_w.reshape(1, D_MODEL), ys, mi_flat, starts)

    return (y_prompt.reshape(bp, tp, D_MODEL),
            y_sample.reshape(bs, ts, D_MODEL),
            sd_p[None],
            nca_p[None],
            ncb_p[None],
            sd_s[None],
            jnp.transpose(nbufa_t, (1, 0, 2))[None],
            jnp.transpose(nbufb_t, (1, 0, 2))[None])
```

```python
import functools

import jax
import jax.numpy as jnp
from jax import lax
from jax.experimental import pallas as pl
from jax.experimental.pallas import tpu as pltpu

f32 = jnp.float32
bf16 = jnp.bfloat16
i32 = jnp.int32

EPS = 1e-6
LANE = 128
D_MODEL = 2048
N_HEADS = 8
HEAD = 128
QK_W = N_HEADS * HEAD
QKV_W = 3 * QK_W
SC_W = 1024
CONV_A = 4
CONV_B = 3
CHUNK = 64
GROUP_HEADS = 4
N_EXPERTS = 64
N_GROUPS = 8
EXPERTS_PER_GROUP = 8
D_FF = 512
MOE_ROWS = 128

COL_QKV = 0
COL_BCX = 3072
COL_GA = 6144
COL_GB = 8192
COL_Z = 10240
COL_BA = 11264
BA_W = 256
PROJ_W = 11520
PROJ_TN = 1280

VMEM_LIMIT = 56 * 1024 * 1024


def _dot(a, b):
    return jnp.dot(a, b, preferred_element_type=f32)


def _dot_nt(a, b):
    return lax.dot_general(a, b, (((1,), (1,)), ((), ())), preferred_element_type=f32)


def _split(x, n):
    parts = []
    r = x
    for i in range(n):
        p = r.astype(bf16)
        parts.append(p)
        if i + 1 < n:
            r = r - p.astype(f32)
    return parts


def _dot_lsplit(x, m, n=3):
    rows = x.shape[0]
    d = _dot(jnp.concatenate(_split(x, n), axis=0), m)
    acc = d[0:rows]
    for i in range(1, n):
        acc = acc + d[i * rows:(i + 1) * rows]
    return acc


def _dot_rsplit(m, x, n=3):
    cols = x.shape[1]
    d = _dot(m, jnp.concatenate(_split(x, n), axis=1))
    acc = d[:, 0:cols]
    for i in range(1, n):
        acc = acc + d[:, i * cols:(i + 1) * cols]
    return acc


def _silu(x):
    return x * jax.nn.sigmoid(x)


def _softplus(x):
    return jnp.maximum(x, 0.0) + jnp.log(1.0 + jnp.exp(-jnp.abs(x)))


def _cparams(sem):
    return pltpu.CompilerParams(dimension_semantics=sem, vmem_limit_bytes=VMEM_LIMIT)


def _inproj_kernel(x_ref, nw_ref, w_ref, o_ref, h_ref, *, rows):
    @pl.when(pl.program_id(1) == 0)
    def _():
        def body(r, c):
            sl = pl.ds(pl.multiple_of(r * rows, rows), rows)
            x = x_ref[sl, :]
            ms = jnp.mean(x * x, axis=-1, keepdims=True)
            h_ref[sl, :] = (x * lax.rsqrt(ms + EPS) * nw_ref[...]).astype(bf16)
            return c
        lax.fori_loop(0, x_ref.shape[0] // rows, body, 0)

    o_ref[...] = _dot_nt(h_ref[...], w_ref[...])


def _inproj(x2d, norm_w, w_bf16):
    n = x2d.shape[0]
    tm = min(1024, n)
    assert n % tm == 0 and PROJ_W % PROJ_TN == 0
    return pl.pallas_call(
        functools.partial(_inproj_kernel, rows=min(128, tm)),
        out_shape=jax.ShapeDtypeStruct((n, PROJ_W), f32),
        grid=(n // tm, PROJ_W // PROJ_TN),
        in_specs=[pl.BlockSpec((tm, D_MODEL), lambda i, j: (i, 0)),
                  pl.BlockSpec((1, D_MODEL), lambda i, j: (0, 0)),
                  pl.BlockSpec((PROJ_TN, D_MODEL), lambda i, j: (j, 0))],
        out_specs=pl.BlockSpec((tm, PROJ_TN), lambda i, j: (i, j)),
        scratch_shapes=[pltpu.VMEM((tm, D_MODEL), bf16)],
        compiler_params=_cparams(("arbitrary", "arbitrary")),
        name="inproj",
    )(x2d, norm_w.reshape(1, D_MODEL), w_bf16)


CONV_TILES = 3
CONV_COLS = 2 * HEAD
CONV_ROWS = 128
CONV_PAD_W = CONV_TILES * PROJ_TN


def _qkv_kind(col):
    return "q" if col < QK_W else "k" if col < 2 * QK_W else "v" if col < QKV_W else "raw"


def _inproj_conv_kernel(x_ref, nw_ref, w_ref, cw_ref, o_ref, tail_ref, h_ref, hist_ref, raw_ref, *,
                        rows, tiles_per_seq):
    i = pl.program_id(0)
    j = pl.program_id(1)
    tm = x_ref.shape[0]

    @pl.when(j == 0)
    def _():
        def body(r, c):
            sl = pl.ds(pl.multiple_of(r * rows, rows), rows)
            x = x_ref[sl, :]
            ms = jnp.mean(x * x, axis=-1, keepdims=True)
            h_ref[sl, :] = (x * lax.rsqrt(ms + EPS) * nw_ref[...]).astype(bf16)
            return c
        lax.fori_loop(0, tm // rows, body, 0)

    @pl.when((i == 0) & (j == 0))
    def _():
        hist_ref[...] = jnp.zeros(hist_ref.shape, f32)

    @pl.when(j >= CONV_TILES)
    def _():
        o_ref[...] = _dot_nt(h_ref[...], w_ref[...])

    seq_start = lax.rem(i, tiles_per_seq) == 0
    for jj in range(CONV_TILES):
        @pl.when(j == jj)
        def _():
            def matmul_chunk(idx, c0):
                raw_ref[idx % 2] = _dot_nt(h_ref[...], w_ref[c0:c0 + CONV_COLS, :])

            def conv_chunk(idx, c0):
                cs = slice(c0, c0 + CONV_COLS)
                raw = raw_ref.at[idx % 2]
                tail = raw[tm - 8:tm, :]
                tail_ref[0, :, cs] = tail
                kinds = [_qkv_kind(jj * PROJ_TN + c0 + g * HEAD) for g in range(CONV_COLS // HEAD)]
                if kinds[0] == "raw":
                    o_ref[:, cs] = raw[...]
                    return
                hist = jnp.where(seq_start, 0.0, hist_ref[jj, :, cs])
                for rc in range(tm // CONV_ROWS):
                    r0 = rc * CONV_ROWS
                    if rc > 0:
                        xe = raw[r0 - 8:r0 + CONV_ROWS, :]
                    else:
                        xe = jnp.concatenate([hist, raw[0:CONV_ROWS, :]], axis=0)
                    acc = pltpu.roll(xe, 3, axis=0)[8:] * cw_ref[0:1, cs]
                    acc = acc + pltpu.roll(xe, 2, axis=0)[8:] * cw_ref[1:2, cs]
                    acc = acc + pltpu.roll(xe, 1, axis=0)[8:] * cw_ref[2:3, cs]
                    acc = acc + xe[8:] * cw_ref[3:4, cs]
                    act = _silu(acc)
                    for g, kind in enumerate(kinds):
                        ah = act[:, g * HEAD:(g + 1) * HEAD]
                        if kind != "v":
                            ss = jnp.sum(ah * ah, axis=-1, keepdims=True)
                            ah = ah * lax.rsqrt(ss + EPS)
                            if kind == "q":
                                ah = ah * (HEAD ** -0.5)
                        o_ref[r0:r0 + CONV_ROWS, c0 + g * HEAD:c0 + (g + 1) * HEAD] = ah
                hist_ref[jj, :, cs] = tail

            chunks = list(range(0, PROJ_TN, CONV_COLS))
            matmul_chunk(0, chunks[0])
            for idx in range(1, len(chunks)):
                matmul_chunk(idx, chunks[idx])
                conv_chunk(idx - 1, chunks[idx - 1])
            conv_chunk(len(chunks) - 1, chunks[-1])


def _inproj_conv(x2d, norm_w, w_bf16, cwa, seq_len):
    n = x2d.shape[0]
    tm = min(1024, seq_len)
    assert n % tm == 0 and seq_len % tm == 0 and PROJ_W % PROJ_TN == 0 and tm % CONV_ROWS == 0
    assert QKV_W % CONV_COLS == 0 and PROJ_TN % CONV_COLS == 0
    cw_pad = jnp.zeros((CONV_A, CONV_PAD_W), f32).at[:, 0:QKV_W].set(cwa)
    last = CONV_TILES - 1
    return pl.pallas_call(
        functools.partial(_inproj_conv_kernel, rows=min(128, tm), tiles_per_seq=seq_len // tm),
        out_shape=(jax.ShapeDtypeStruct((n, PROJ_W), f32),
                   jax.ShapeDtypeStruct((n // tm, 8, CONV_PAD_W), f32)),
        grid=(n // tm, PROJ_W // PROJ_TN),
        in_specs=[pl.BlockSpec((tm, D_MODEL), lambda i, j: (i, 0)),
                  pl.BlockSpec((1, D_MODEL), lambda i, j: (0, 0)),
                  pl.BlockSpec((PROJ_TN, D_MODEL), lambda i, j: (j, 0)),
                  pl.BlockSpec((CONV_A, PROJ_TN), lambda i, j: (0, jnp.minimum(j, last)))],
        out_specs=(pl.BlockSpec((tm, PROJ_TN), lambda i, j: (i, j)),
                   pl.BlockSpec((1, 8, PROJ_TN), lambda i, j: (i, 0, jnp.minimum(j, last)))),
        scratch_shapes=[pltpu.VMEM((tm, D_MODEL), bf16), pltpu.VMEM((CONV_TILES, 8, PROJ_TN), f32),
                        pltpu.VMEM((2, tm, CONV_COLS), f32)],
        compiler_params=_cparams(("arbitrary", "arbitrary")),
        name="inproj_conv",
    )(x2d, norm_w.reshape(1, D_MODEL), w_bf16, cw_pad)


W_IN_COLS = 11280
WPREP_TN = 1024
WPREP_SHIFT = 16


def _wprep_kernel(a_ref, b_ref, o_ref):
    j = pl.program_id(0)
    keep = WPREP_TN - WPREP_SHIFT

    @pl.when((j < 3) | (j == 10))
    def _():
        o_ref[...] = a_ref[...].astype(bf16)

    @pl.when((j >= 3) & (j < 10))
    def _():
        o_ref[0:keep, :] = a_ref[WPREP_SHIFT:WPREP_TN, :].astype(bf16)
        o_ref[keep:WPREP_TN, :] = b_ref[...].astype(bf16)

    @pl.when(j == 11)
    def _():
        o_ref[0:WPREP_SHIFT, :] = a_ref[0:WPREP_SHIFT, :].astype(bf16)
        o_ref[WPREP_SHIFT:WPREP_TN, :] = jnp.zeros((keep, D_MODEL), bf16)


def _wprep(w_in_t):
    assert w_in_t.shape == (W_IN_COLS, D_MODEL) and 2 * N_HEADS == WPREP_SHIFT
    n_blk = pl.cdiv(PROJ_W, WPREP_TN)

    def a_map(j):
        return (jnp.where(j < 3, j, jnp.where(j < 10, j + 1, jnp.where(j == 10, 3, 4))), 0)

    def b_map(j):
        return (jnp.minimum((WPREP_TN // WPREP_SHIFT) * (j + 2), W_IN_COLS // WPREP_SHIFT - 1), 0)

    return pl.pallas_call(
        _wprep_kernel,
        out_shape=jax.ShapeDtypeStruct((PROJ_W, D_MODEL), bf16),
        grid=(n_blk,),
        in_specs=[pl.BlockSpec((WPREP_TN, D_MODEL), a_map),
                  pl.BlockSpec((WPREP_SHIFT, D_MODEL), b_map)],
        out_specs=pl.BlockSpec((WPREP_TN, D_MODEL), lambda j: (j, 0)),
        compiler_params=_cparams(("arbitrary",)),
        name="wprep",
    )(w_in_t, w_in_t)


def _head_l2norm(a, scale):
    outs = []
    for h in range(N_HEADS):
        ah = a[:, h * HEAD:(h + 1) * HEAD]
        ss = jnp.sum(ah * ah, axis=-1, keepdims=True)
        n = ah * lax.rsqrt(ss + EPS)
        outs.append(n * scale if scale != 1.0 else n)
    return outs


def _delta_prompt_kernel(qkv_ref, bcx_ref, z_ref, ba_ref, cwb_ref, alog_ref, dtb_ref, onw_ref,
                         e64_ref,
                         o_ref, y_ref, snew_ref, ncb_ref,
                         s_ref, xb_ref, *, nb_step):
    C = CHUNK
    G = GROUP_HEADS
    R = G * C
    t = pl.program_id(1)
    nt = pl.num_programs(1)

    @pl.when(t == 0)
    def _():
        s_ref[...] = jnp.zeros(s_ref.shape, f32)
        xb_ref[:, 0:8, :] = jnp.zeros((nb_step, 8, SC_W), f32)

    rr = lax.broadcasted_iota(i32, (R, R), 0)
    cc = lax.broadcasted_iota(i32, (R, R), 1)
    same = (rr >> 6) == (cc >> 6)
    incl = same & (rr >= cc)
    strict = same & (rr > cc)
    eye = jnp.where(rr == cc, 1.0, 0.0).astype(f32)
    r2 = lax.broadcasted_iota(i32, (R, G * HEAD), 0)
    c2 = lax.broadcasted_iota(i32, (R, G * HEAD), 1)
    bdmask = (r2 >> 6) == (c2 >> 7)
    r3 = lax.broadcasted_iota(i32, (C, C), 0)
    c3 = lax.broadcasted_iota(i32, (C, C), 1)
    ltri = jnp.where(r3 >= c3, 1.0, 0.0).astype(bf16)
    r4 = lax.broadcasted_iota(i32, (C, R), 0)
    c4 = lax.broadcasted_iota(i32, (C, R), 1)
    ident_t = r4 == (c4 & (C - 1))
    ones8 = jnp.ones((8, C), bf16)

    nbs = range(nb_step)
    units = [(nb, g) for nb in nbs for g in range(N_HEADS // G)]
    heads = lambda g: range(g * G, (g + 1) * G)

    qn = [[qkv_ref[nb, :, h * HEAD:(h + 1) * HEAD] for h in range(N_HEADS)] for nb in nbs]
    kn = [[qkv_ref[nb, :, QK_W + h * HEAD:QK_W + (h + 1) * HEAD] for h in range(N_HEADS)] for nb in nbs]
    vv = [qkv_ref[nb, :, 2 * QK_W:3 * QK_W] for nb in nbs]

    bts = [ba_ref[nb] for nb in nbs]
    beta_all = [jax.nn.sigmoid(bt) for bt in bts]
    g_all = [-(jnp.exp(alog_ref[...]) * _softplus(bt + dtb_ref[...])) for bt in bts]
    gc_small = [_dot_rsplit(ltri, ga) for ga in g_all]
    gl_small = [gc[C - 1:C, :] for gc in gc_small]

    k_st, q_st, kb, vb, kbg, qd, kd, gc_col = ({} for _ in range(8))
    for u in units:
        nb, g = u
        hs = heads(g)
        k_st[u] = jnp.concatenate([kn[nb][h] for h in hs], axis=0)
        q_st[u] = jnp.concatenate([qn[nb][h] for h in hs], axis=0)
        v_st = jnp.concatenate([vv[nb][:, h * HEAD:(h + 1) * HEAD] for h in hs], axis=0)
        beta_col = jnp.concatenate([beta_all[nb][:, h:h + 1] for h in hs], axis=0)
        gc_col[u] = jnp.concatenate([gc_small[nb][:, 8 + h:9 + h] for h in hs], axis=0)
        gl_col = jnp.concatenate(
            [jnp.broadcast_to(gl_small[nb][:, 8 + h:9 + h], (C, 1)) for h in hs], axis=0)
        kb[u] = k_st[u] * beta_col
        vb[u] = v_st * beta_col
        egc = jnp.exp(gc_col[u])
        kbg[u] = kb[u] * egc
        qd[u] = q_st[u] * egc
        kd[u] = k_st[u] * jnp.exp(gl_col - gc_col[u])

    gx = {u: _dot_lsplit(gc_small[u[0]], e64_ref[u[1]]) for u in units}
    crow = {u: _dot_rsplit(ones8, jnp.where(ident_t, gx[u], 0.0))[0:1, :] for u in units}
    a = {u: _dot_nt(jnp.concatenate([kb[u], q_st[u]], axis=0).astype(bf16), k_st[u].astype(bf16))
         for u in units}
    dec = {u: jnp.where(incl, jnp.exp(jnp.where(incl, gc_col[u] - crow[u], 0.0)), 0.0) for u in units}
    nm = {u: jnp.where(strict, -(a[u][0:R] * dec[u]), 0.0) for u in units}
    qkm = {u: a[u][R:2 * R] * dec[u] for u in units}

    p = {u: eye + nm[u] for u in units}
    nk = {}
    for u in units:
        nmb = nm[u].astype(bf16)
        nk[u] = _dot(nmb, nmb)
    for _ in range(4):
        for u in units:
            x = _dot(jnp.concatenate([p[u], nk[u]], axis=0).astype(bf16), nk[u].astype(bf16))
            p[u] = p[u] + x[0:R]
            nk[u] = x[R:2 * R]
    for u in units:
        p[u] = p[u] + _dot(p[u].astype(bf16), nk[u].astype(bf16))
    uw = {u: _dot(p[u].astype(bf16), jnp.concatenate([vb[u], kbg[u]], axis=1).astype(bf16)) for u in units}

    ws = {}
    for u in units:
        nb, g = u
        for j, h in enumerate(heads(g)):
            sh = s_ref[nb, :, h * HEAD:(h + 1) * HEAD]
            lhs = jnp.concatenate([uw[u][j * C:(j + 1) * C, HEAD:2 * HEAD], qd[u][j * C:(j + 1) * C]], axis=0)
            ws[u, j] = _dot(lhs.astype(bf16), sh.astype(bf16))
    o_heads = {}
    for u in units:
        nb, g = u
        vnew_st = jnp.concatenate([uw[u][j * C:(j + 1) * C, 0:HEAD] - ws[u, j][0:C] for j in range(G)], axis=0)
        o_st = (jnp.concatenate([ws[u, j][C:2 * C] for j in range(G)], axis=0)
                + _dot(qkm[u].astype(bf16), vnew_st.astype(bf16)))
        vbd = jnp.where(bdmask, jnp.concatenate([vnew_st] * G, axis=1), 0.0)
        lo = g * G * HEAD
        hi = lo + G * HEAD
        gl_row = jnp.concatenate(
            [jnp.broadcast_to(jnp.exp(gl_small[nb][:, 8 + h:9 + h]), (1, HEAD)) for h in heads(g)], axis=1)
        s_ref[nb, :, lo:hi] = s_ref[nb, :, lo:hi] * gl_row + _dot(kd[u].T.astype(bf16), vbd.astype(bf16))
        for j, h in enumerate(heads(g)):
            o_heads[nb, h] = o_st[j * C:(j + 1) * C]

    for nb in nbs:
        zt = z_ref[nb]
        for h in range(N_HEADS):
            oh = o_heads[nb, h]
            ms = jnp.mean(oh * oh, axis=-1, keepdims=True)
            zh = zt[:, h * HEAD:(h + 1) * HEAD]
            on = oh * lax.rsqrt(ms + EPS) * onw_ref[...] * _silu(zh)
            o_ref[nb, :, h * HEAD:(h + 1) * HEAD] = on.astype(bf16)

    for nb in nbs:
        bcx = bcx_ref[nb]
        cx = bcx[:, SC_W:2 * SC_W] * bcx[:, 2 * SC_W:3 * SC_W]
        xb_ref[nb, 8:8 + C, :] = cx
        ce = xb_ref[nb]
        cv = pltpu.roll(ce, 2, axis=0)[8:8 + C] * cwb_ref[0:1, :]
        cv = cv + pltpu.roll(ce, 1, axis=0)[8:8 + C] * cwb_ref[1:2, :]
        cv = cv + cx * cwb_ref[2:3, :]
        y_ref[nb] = (bcx[:, 0:SC_W] * cv).astype(bf16)
        xb_ref[nb, 0:8, :] = xb_ref[nb, C:C + 8, :]

    @pl.when(t == nt - 1)
    def _():
        for nb in range(nb_step):
            for h in range(N_HEADS):
                snew_ref[nb, h] = s_ref[nb, :, h * HEAD:(h + 1) * HEAD]
            ncb_ref[nb] = xb_ref[nb, 6:8, :]


def _expand_consts():
    lane = jnp.arange(BA_W)[:, None]
    col = jnp.arange(QK_W)[None, :]
    eb = (lane == (col >> 7)).astype(bf16)
    eg = (lane == (8 + (col >> 7))).astype(bf16)
    col64 = jnp.arange(GROUP_HEADS * CHUNK)[None, :]
    e64 = jnp.stack([(lane == (8 + g * GROUP_HEADS + (col64 >> 6))).astype(bf16)
                     for g in range(N_HEADS // GROUP_HEADS)], axis=0)
    return eb, eg, e64


def _delta_prompt(proj3, cwb, alog_row, dtb_row, onw_row, nb_step):
    b, t, _ = proj3.shape
    assert t % CHUNK == 0 and b % nb_step == 0
    _, _, e64 = _expand_consts()
    c = CHUNK
    const2 = lambda bi, ti: (0, 0)
    outs = pl.pallas_call(
        functools.partial(_delta_prompt_kernel, nb_step=nb_step),
        out_shape=(jax.ShapeDtypeStruct((b, t, QK_W), bf16),
                   jax.ShapeDtypeStruct((b, t, SC_W), bf16),
                   jax.ShapeDtypeStruct((b, N_HEADS, HEAD, HEAD), f32),
                   jax.ShapeDtypeStruct((b, CONV_B - 1, SC_W), f32)),
        grid=(b // nb_step, t // c),
        in_specs=[pl.BlockSpec((nb_step, c, QKV_W), lambda bi, ti: (bi, ti, COL_QKV // QKV_W)),
                  pl.BlockSpec((nb_step, c, QKV_W), lambda bi, ti: (bi, ti, COL_BCX // QKV_W)),
                  pl.BlockSpec((nb_step, c, QK_W), lambda bi, ti: (bi, ti, COL_Z // QK_W)),
                  pl.BlockSpec((nb_step, c, BA_W), lambda bi, ti: (bi, ti, COL_BA // BA_W)),
                  pl.BlockSpec((CONV_B, SC_W), const2),
                  pl.BlockSpec((1, BA_W), const2),
                  pl.BlockSpec((1, BA_W), const2),
                  pl.BlockSpec((1, HEAD), const2),
                  pl.BlockSpec((N_HEADS // GROUP_HEADS, BA_W, GROUP_HEADS * CHUNK), lambda bi, ti: (0, 0, 0))],
        out_specs=(pl.BlockSpec((nb_step, c, QK_W), lambda bi, ti: (bi, ti, 0)),
                   pl.BlockSpec((nb_step, c, SC_W), lambda bi, ti: (bi, ti, 0)),
                   pl.BlockSpec((nb_step, N_HEADS, HEAD, HEAD), lambda bi, ti: (bi, 0, 0, 0)),
                   pl.BlockSpec((nb_step, CONV_B - 1, SC_W), lambda bi, ti: (bi, 0, 0))),
        scratch_shapes=[pltpu.VMEM((nb_step, HEAD, QK_W), f32),
                        pltpu.VMEM((nb_step, 8 + c, SC_W), f32)],
        compiler_params=_cparams(("arbitrary", "arbitrary")),
        name="delta_prompt",
    )(proj3, proj3, proj3, proj3, cwb, alog_row, dtb_row, onw_row, e64)
    return outs


def _sample_prep_kernel(p_ref, bufa_ref, bufb_ref, cwa_ref, cwb_ref, alog_ref, dtb_ref, eb_ref, eg_ref,
                        q_ref, k_ref, v_ref, beta_ref, eg_out_ref, y_ref, nbufa_ref, nbufb_ref):
    def conv_sec(lo):
        hi = lo + QK_W
        raw = p_ref[:, COL_QKV + lo:COL_QKV + hi]
        acc = bufa_ref[0, :, lo:hi] * cwa_ref[0:1, lo:hi]
        acc = acc + bufa_ref[1, :, lo:hi] * cwa_ref[1:2, lo:hi]
        acc = acc + bufa_ref[2, :, lo:hi] * cwa_ref[2:3, lo:hi]
        acc = acc + raw * cwa_ref[3:4, lo:hi]
        nbufa_ref[0, :, lo:hi] = bufa_ref[1, :, lo:hi]
        nbufa_ref[1, :, lo:hi] = bufa_ref[2, :, lo:hi]
        nbufa_ref[2, :, lo:hi] = raw
        return _silu(acc)

    qn = _head_l2norm(conv_sec(0), HEAD ** -0.5)
    kn = _head_l2norm(conv_sec(QK_W), 1.0)
    for h in range(N_HEADS):
        q_ref[:, h * HEAD:(h + 1) * HEAD] = qn[h]
        k_ref[:, h * HEAD:(h + 1) * HEAD] = kn[h]
    v_ref[...] = conv_sec(2 * QK_W)

    bt = p_ref[:, COL_BA:COL_BA + BA_W]
    beta_all = jax.nn.sigmoid(bt)
    g_all = -(jnp.exp(alog_ref[...]) * _softplus(bt + dtb_ref[...]))
    beta_ref[...] = _dot_lsplit(beta_all, eb_ref[...])
    eg_out_ref[...] = jnp.exp(_dot_lsplit(g_all, eg_ref[...]))

    bg = p_ref[:, COL_BCX:COL_BCX + SC_W]
    cx = p_ref[:, COL_BCX + SC_W:COL_BCX + 2 * SC_W] * p_ref[:, COL_BCX + 2 * SC_W:COL_BCX + 3 * SC_W]
    cv = bufb_ref[0] * cwb_ref[0:1, :]
    cv = cv + bufb_ref[1] * cwb_ref[1:2, :]
    cv = cv + cx * cwb_ref[2:3, :]
    y_ref[...] = (bg * cv).astype(bf16)
    nbufb_ref[0] = bufb_ref[1]
    nbufb_ref[1] = cx


def _sample_prep(proj_s, bufa_t, bufb_t, cwa, cwb, alog_row, dtb_row):
    n = proj_s.shape[0]
    eb, eg, _ = _expand_consts()
    row = jax.ShapeDtypeStruct((n, QK_W), f32)
    return pl.pallas_call(
        _sample_prep_kernel,
        out_shape=(row, row, row, row, row,
                   jax.ShapeDtypeStruct((n, SC_W), bf16),
                   jax.ShapeDtypeStruct((CONV_A - 1, n, QKV_W), f32),
                   jax.ShapeDtypeStruct((CONV_B - 1, n, SC_W), f32)),
        compiler_params=pltpu.CompilerParams(vmem_limit_bytes=VMEM_LIMIT),
        name="sample_prep",
    )(proj_s, bufa_t, bufb_t, cwa, cwb, alog_row, dtb_row, eb, eg)


def _sample_step_kernel(s_ref, q_ref, k_ref, v_ref, beta_ref, eg_ref, z_ref, onw_ref,
                        snew_ref, o_ref, *, bb):
    w = N_HEADS * HEAD
    r8 = lax.broadcasted_iota(i32, (N_HEADS, w), 0)
    c8 = lax.broadcasted_iota(i32, (N_HEADS, w), 1)
    mask8 = r8 == (c8 >> 7)
    zpad_k = jnp.zeros((HEAD - N_HEADS, HEAD), f32)
    zpad_d = jnp.zeros((HEAD - N_HEADS, w), f32)
    for b in range(bb):
        s_all = jnp.concatenate([s_ref[b, h] for h in range(N_HEADS)], axis=1)
        eg8 = eg_ref[b]
        eg_row = jnp.concatenate([eg8[h:h + 1, :] for h in range(N_HEADS)], axis=1)
        s_dec = s_all * eg_row
        k8 = k_ref[b]
        x = _dot(k8.astype(bf16), s_dec.astype(bf16))
        v_t = jnp.concatenate([v_ref[b]] * N_HEADS, axis=1)
        b_t = jnp.concatenate([beta_ref[b]] * N_HEADS, axis=1)
        d_bd = jnp.where(mask8, (v_t - x) * b_t, 0.0)
        kt = jnp.concatenate([k8, zpad_k], axis=0).T
        d_pad = jnp.concatenate([d_bd, zpad_d], axis=0)
        k_hi, k_lo = _split(kt, 2)
        d_hi, d_lo = _split(d_pad, 2)
        s_new = s_dec + (_dot(k_hi, d_hi) + _dot(k_hi, d_lo) + _dot(k_lo, d_hi))
        yv = jnp.where(mask8, _dot(q_ref[b].astype(bf16), s_new.astype(bf16)), 0.0)
        o8 = yv[:, 0:HEAD]
        for j in range(1, N_HEADS):
            o8 = o8 + yv[:, j * HEAD:(j + 1) * HEAD]
        ms = jnp.mean(o8 * o8, axis=-1, keepdims=True)
        o_ref[b] = o8 * lax.rsqrt(ms + EPS) * onw_ref[...] * _silu(z_ref[b])
        for h in range(N_HEADS):
            snew_ref[b, h] = s_new[:, h * HEAD:(h + 1) * HEAD]


def _sample_step(state, q, k, v, beta, eg, z, onw_row, bb=4):
    n = state.shape[0]
    assert n % bb == 0
    hspec = pl.BlockSpec((bb, N_HEADS, HEAD), lambda i: (i, 0, 0))
    sspec = pl.BlockSpec((bb, N_HEADS, HEAD, HEAD), lambda i: (i, 0, 0, 0))
    return pl.pallas_call(
        functools.partial(_sample_step_kernel, bb=bb),
        out_shape=(jax.ShapeDtypeStruct(state.shape, f32),
                   jax.ShapeDtypeStruct((n, N_HEADS, HEAD), f32)),
        grid=(n // bb,),
        in_specs=[sspec, hspec, hspec, hspec, hspec, hspec, hspec, pl.BlockSpec((1, HEAD), lambda i: (0, 0))],
        out_specs=(sspec, hspec),
        compiler_params=_cparams(("arbitrary",)),
        name="sample_step",
    )(state, q, k, v, beta, eg, z, onw_row)


def _mix_route_kernel(x_ref, o_ref, y_ref, ga_ref, gb_ref, wa_ref, wb_ref, wo_ref, n2_ref,
                      rwh_ref, rwl_ref, rb_ref, cnt_in_ref, x1_ref, h2_ref, mi_ref, mw_ref, cnt_ref):
    i = pl.program_id(0)
    tm = x_ref.shape[0]

    @pl.when(i == 0)
    def _():
        cnt_ref[...] = cnt_in_ref[...]

    oa = _dot(o_ref[...], wa_ref[...])
    ob = _dot(y_ref[...], wb_ref[...])
    merged = jax.nn.sigmoid(ga_ref[...]) * oa + jax.nn.sigmoid(gb_ref[...]) * ob
    x1 = x_ref[...] + _dot(merged.astype(bf16), wo_ref[...])
    x1_ref[...] = x1
    ms = jnp.mean(x1 * x1, axis=-1, keepdims=True)
    h2 = x1 * lax.rsqrt(ms + EPS) * n2_ref[...]
    h2_ref[...] = h2

    h_hi, h_lo = _split(h2, 2)
    logits = _dot(h_hi, rwh_ref[...]) + _dot(h_hi, rwl_ref[...]) + _dot(h_lo, rwh_ref[...]) + rb_ref[...]

    lane = lax.broadcasted_iota(i32, (tm, LANE), 1)
    lanef = lane.astype(f32)
    neg = jnp.float32(-jnp.inf)
    big = jnp.float32(1e9)
    gmask = (lane >= N_EXPERTS) & (lane < N_EXPERTS + N_GROUPS)
    gl = jnp.where(gmask, logits, neg)
    gmax = jnp.max(gl, axis=-1, keepdims=True)
    gidx = jnp.min(jnp.where(gl == gmax, lanef - N_EXPERTS, big), axis=-1, keepdims=True)
    gsum = jnp.sum(jnp.where(gmask, jnp.exp(gl - gmax), 0.0), axis=-1, keepdims=True)
    gprob = 1.0 / gsum

    emask = (lane < N_EXPERTS) & ((lane >> 3).astype(f32) == gidx)
    el = jnp.where(emask, logits, neg)
    emax = jnp.max(el, axis=-1, keepdims=True)
    pe = jnp.where(emask, jnp.exp(el - emax), 0.0)
    eprob = pe / jnp.sum(pe, axis=-1, keepdims=True)
    p1m = jnp.where(emask, eprob, -1.0)
    m1 = jnp.max(p1m, axis=-1, keepdims=True)
    i1 = jnp.min(jnp.where(p1m == m1, lanef, big), axis=-1, keepdims=True)
    p2m = jnp.where(lanef == i1, -1.0, p1m)
    m2 = jnp.max(p2m, axis=-1, keepdims=True)
    i2 = jnp.min(jnp.where(p2m == m2, lanef, big), axis=-1, keepdims=True)
    tot = m1 + m2
    c1 = m1 / tot * gprob
    c2 = m2 / tot * gprob

    oh1 = jnp.where(lanef == i1, 1.0, 0.0)
    oh2 = jnp.where(lanef == i2, 1.0, 0.0)
    ohs = oh1 + oh2
    rt = lax.broadcasted_iota(i32, (tm, tm), 0)
    ct = lax.broadcasted_iota(i32, (tm, tm), 1)
    lstrict = jnp.where(rt > ct, 1.0, 0.0).astype(bf16)
    cs = _dot(lstrict, ohs.astype(bf16)) + cnt_ref[...]
    rank1 = jnp.sum(cs * oh1, axis=-1, keepdims=True)
    rank2 = jnp.sum(cs * oh2, axis=-1, keepdims=True)
    cnt_ref[...] = cnt_ref[...] + jnp.sum(ohs, axis=0, keepdims=True)

    mi = jnp.where(lane == 0, i1, jnp.where(lane == 1, i2, jnp.where(lane == 2, rank1,
                                                                     jnp.where(lane == 3, rank2, 0.0))))
    mi_ref[...] = mi.astype(i32)
    mw_ref[...] = jnp.where(lane == 0, c1, jnp.where(lane == 1, c2, 0.0))


def _mix_route(x2d, o2d, y2d, proj2d, wa, wb, wo, n2_row, rwh, rwl, rb_row, cnt_in):
    n = x2d.shape[0]
    tm = min(256, n)
    assert n % tm == 0
    tok = lambda width: pl.BlockSpec((tm, width), lambda i: (i, 0))
    full = lambda a: pl.BlockSpec(a.shape, lambda i: (0,) * a.ndim)
    in_specs = [tok(D_MODEL), tok(QK_W), tok(SC_W),
                pl.BlockSpec((tm, D_MODEL), lambda i: (i, COL_GA // D_MODEL)),
                pl.BlockSpec((tm, D_MODEL), lambda i: (i, COL_GB // D_MODEL)),
                full(wa), full(wb), full(wo), full(n2_row), full(rwh), full(rwl), full(rb_row), full(cnt_in)]
    out_shape = (jax.ShapeDtypeStruct((n, D_MODEL), f32),
                 jax.ShapeDtypeStruct((n, D_MODEL), f32),
                 jax.ShapeDtypeStruct((n, LANE), i32),
                 jax.ShapeDtypeStruct((n, LANE), f32),
                 jax.ShapeDtypeStruct((1, LANE), f32))
    out_specs = (tok(D_MODEL), tok(D_MODEL), tok(LANE), tok(LANE),
                 pl.BlockSpec((1, LANE), lambda i: (0, 0)))
    return pl.pallas_call(
        _mix_route_kernel,
        out_shape=out_shape,
        grid=(n // tm,),
        in_specs=in_specs,
        out_specs=out_specs,
        compiler_params=_cparams(("arbitrary",)),
        name="mix_route",
    )(x2d, o2d, y2d, proj2d, proj2d, wa, wb, wo, n2_row, rwh, rwl, rb_row, cnt_in)


MI_W = 4
ROW_DMA_UNROLL = 8


def _dest_row(mi_ref, starts_ref, r, k):
    return starts_ref[mi_ref[MI_W * r + k]] + mi_ref[MI_W * r + 2 + k]


def _dispatch_kernel(mi_ref, starts_ref, hp_ref, hs_ref, xs_ref, sem, *, np_tiles, tm):
    i = pl.program_id(0)

    def scatter_rows(h_ref):
        def copy(r, k):
            return pltpu.make_async_copy(h_ref.at[pl.ds(r, 1)],
                                         xs_ref.at[pl.ds(_dest_row(mi_ref, starts_ref, r, k), 1)], sem)

        def start(r, c):
            for k in range(2):
                copy(r, k).start(priority=k)
            return c

        lax.fori_loop(0, tm, start, 0, unroll=ROW_DMA_UNROLL)
        for k in range(2):
            pltpu.make_async_copy(h_ref, xs_ref.at[pl.ds(0, tm)], sem).wait()

    @pl.when(i < np_tiles)
    def _():
        scatter_rows(hp_ref)

    @pl.when(i >= np_tiles)
    def _():
        scatter_rows(hs_ref)


def _dispatch(h2_p, h2_s, mi_flat, starts):
    tm = MOE_ROWS
    n_p, n_s = h2_p.shape[0], h2_s.shape[0]
    assert n_p % tm == 0 and n_s == tm
    np_tiles = n_p // tm
    return pl.pallas_call(
        functools.partial(_dispatch_kernel, np_tiles=np_tiles, tm=tm),
        out_shape=jax.ShapeDtypeStruct((2 * (n_p + n_s), D_MODEL), f32),
        grid=(np_tiles + 1,),
        in_specs=[pl.BlockSpec((MI_W * tm,), lambda i: (i,), memory_space=pltpu.SMEM),
                  pl.BlockSpec(memory_space=pltpu.SMEM),
                  pl.BlockSpec((tm, D_MODEL), lambda i: (jnp.minimum(i, np_tiles - 1), 0)),
                  pl.BlockSpec((tm, D_MODEL), lambda i: (0, 0))],
        out_specs=pl.BlockSpec(memory_space=pl.ANY),
        scratch_shapes=[pltpu.SemaphoreType.DMA(())],
        compiler_params=_cparams(("arbitrary",)),
        name="moe_dispatch",
    )(mi_flat, starts, h2_p, h2_s)


def _cast_rows(src_ref, dst_ref, rows=256):
    def body(r, c):
        sl = pl.ds(pl.multiple_of(r * rows, rows), rows)
        dst_ref[sl, :] = src_ref[sl, :].astype(bf16)
        return c
    lax.fori_loop(0, dst_ref.shape[0] // rows, body, 0)


def _moe_kernel(blk_ref, lo_ref, hi_ref, first_ref, newe_ref, slot_ref, pre_ref, init_ref,
                x_ref, wg_hbm, wu_hbm, wd_hbm, o_ref,
                wg_f, wu_f, wd_f, wg_b, wu_b, wd_b, sem):
    i = pl.program_id(0)
    lo = lo_ref[i]
    hi = hi_ref[i]

    def weight_copies(e, slot):
        return [pltpu.make_async_copy(wg_hbm.at[e], wg_f.at[slot], sem.at[slot, 0]),
                pltpu.make_async_copy(wu_hbm.at[e], wu_f.at[slot], sem.at[slot, 1]),
                pltpu.make_async_copy(wd_hbm.at[e], wd_f.at[slot], sem.at[slot, 2])]

    def start_weights(e, slot):
        for cp, prio in zip(weight_copies(e, slot), (0, 1, 1)):
            cp.start(priority=prio)

    @pl.when(i == 0)
    def _():
        start_weights(init_ref[0], 0)

        @pl.when(init_ref[1] >= 0)
        def _():
            start_weights(init_ref[1], 1)

    @pl.when(newe_ref[i] == 1)
    def _():
        slot = slot_ref[i]
        cg, cu, cd = weight_copies(0, slot)
        cg.wait()
        _cast_rows(wg_f.at[slot], wg_b)
        cu.wait()
        _cast_rows(wu_f.at[slot], wu_b)
        cd.wait()
        _cast_rows(wd_f.at[slot], wd_b)

        @pl.when(pre_ref[i] >= 0)
        def _():
            start_weights(pre_ref[i], slot)

    @pl.when(hi > lo)
    def _():
        x = x_ref[...].astype(bf16)
        a = _dot(x, wg_b[...])
        u = _dot(x, wu_b[...])
        y = _dot((_silu(a) * u).astype(bf16), wd_b[...])
        row = lax.broadcasted_iota(i32, y.shape, 0)
        ym = jnp.where((row >= lo) & (row < hi), y, 0.0)

        @pl.when(first_ref[i] == 1)
        def _():
            o_ref[...] = ym

        @pl.when(first_ref[i] == 0)
        def _():
            o_ref[...] = o_ref[...] + ym


def _moe(xs, w_gate, w_up, w_down, items):
    n_items = items[0].shape[0]
    rows = xs.shape[0]
    n_pref = len(items)
    xmap = lambda i, blk, *_: (blk[i], 0)
    grid_spec = pltpu.PrefetchScalarGridSpec(
        num_scalar_prefetch=n_pref,
        grid=(n_items,),
        in_specs=[pl.BlockSpec((MOE_ROWS, D_MODEL), xmap),
                  pl.BlockSpec(memory_space=pl.ANY),
                  pl.BlockSpec(memory_space=pl.ANY),
                  pl.BlockSpec(memory_space=pl.ANY)],
        out_specs=pl.BlockSpec((MOE_ROWS, D_MODEL), xmap),
        scratch_shapes=[pltpu.VMEM((2, D_MODEL, D_FF), f32), pltpu.VMEM((2, D_MODEL, D_FF), f32),
                        pltpu.VMEM((2, D_FF, D_MODEL), f32),
                        pltpu.VMEM((D_MODEL, D_FF), bf16), pltpu.VMEM((D_MODEL, D_FF), bf16),
                        pltpu.VMEM((D_FF, D_MODEL), bf16),
                        pltpu.SemaphoreType.DMA((2, 3))],
    )
    return pl.pallas_call(
        _moe_kernel,
        out_shape=jax.ShapeDtypeStruct((rows, D_MODEL), f32),
        grid_spec=grid_spec,
        compiler_params=_cparams(("arbitrary",)),
        name="moe_experts",
    )(*items, xs, w_gate, w_up, w_down)


def _combine_kernel(mi_ref, mi_next_ref, starts_ref, x1p_ref, mwp_ref, x1s_ref, mws_ref, fnw_ref, ys_ref,
                    yp_ref, ysm_ref, g_ref, sem, *, np_tiles):
    i = pl.program_id(0)
    n = pl.num_programs(0)
    tm = x1p_ref.shape[0]
    slot = lax.rem(i, 2)

    def gather_rows(m_ref, dst_slot):
        def body(r, c):
            for k in range(2):
                pltpu.make_async_copy(ys_ref.at[pl.ds(_dest_row(m_ref, starts_ref, r, k), 1)],
                                      g_ref.at[dst_slot, k, pl.ds(r, 1)], sem.at[dst_slot]).start(priority=k)
            return c
        lax.fori_loop(0, tm, body, 0, unroll=ROW_DMA_UNROLL)

    @pl.when(i == 0)
    def _():
        gather_rows(mi_ref, 0)

    @pl.when(i + 1 < n)
    def _():
        gather_rows(mi_next_ref, 1 - slot)

    for k in range(2):
        pltpu.make_async_copy(ys_ref.at[pl.ds(0, tm)], g_ref.at[slot, k], sem.at[slot]).wait()

    def finish(x1_ref, mw_ref, out_ref):
        mw = mw_ref[...]
        x2 = x1_ref[...] + (g_ref[slot, 0] * mw[:, 0:1] + g_ref[slot, 1] * mw[:, 1:2])
        ms = jnp.mean(x2 * x2, axis=-1, keepdims=True)
        out_ref[...] = x2 * lax.rsqrt(ms + EPS) * fnw_ref[...]

    @pl.when(i < np_tiles)
    def _():
        finish(x1p_ref, mwp_ref, yp_ref)

    @pl.when(i >= np_tiles)
    def _():
        finish(x1s_ref, mws_ref, ysm_ref)


def _combine(x1_p, mw_p, x1_s, mw_s, fnw_row, ys, mi_flat, starts):
    tm = MOE_ROWS
    n_p, n_s = x1_p.shape[0], x1_s.shape[0]
    assert n_p % tm == 0 and n_s == tm
    np_tiles = n_p // tm
    ptile = lambda width: pl.BlockSpec((tm, width), lambda i: (jnp.minimum(i, np_tiles - 1), 0))
    stile = lambda width: pl.BlockSpec((tm, width), lambda i: (0, 0))
    return pl.pallas_call(
        functools.partial(_combine_kernel, np_tiles=np_tiles),
        out_shape=(jax.ShapeDtypeStruct((n_p, D_MODEL), f32),
                   jax.ShapeDtypeStruct((n_s, D_MODEL), f32)),
        grid=(np_tiles + 1,),
        in_specs=[pl.BlockSpec((MI_W * tm,), lambda i: (i,), memory_space=pltpu.SMEM),
                  pl.BlockSpec((MI_W * tm,), lambda i: (jnp.minimum(i + 1, np_tiles),), memory_space=pltpu.SMEM),
                  pl.BlockSpec(memory_space=pltpu.SMEM),
                  ptile(D_MODEL), ptile(LANE), stile(D_MODEL), stile(LANE),
                  pl.BlockSpec((1, D_MODEL), lambda i: (0, 0)),
                  pl.BlockSpec(memory_space=pl.ANY)],
        out_specs=(ptile(D_MODEL), stile(D_MODEL)),
        scratch_shapes=[pltpu.VMEM((2, 2, tm, D_MODEL), f32), pltpu.SemaphoreType.DMA((2,))],
        compiler_params=_cparams(("arbitrary",)),
        name="moe_combine",
    )(mi_flat, mi_flat, starts, x1_p, mw_p, x1_s, mw_s, fnw_row, ys)


def _work_items(counts, n_rows):
    nblk = n_rows // MOE_ROWS
    n_items = nblk + N_EXPERTS - 1
    ends = jnp.cumsum(counts)
    starts = ends - counts
    first_blk = starts // MOE_ROWS
    last_blk = jnp.maximum(ends - 1, 0) // MOE_ROWS
    nvis = jnp.where(counts > 0, last_blk - first_blk + 1, 0)
    vis_end = jnp.cumsum(nvis)
    vis_start = vis_end - nvis
    total = vis_end[-1]
    idx = jnp.arange(n_items, dtype=i32)
    e = jnp.minimum(jnp.sum((vis_end[None, :] <= idx[:, None]).astype(i32), axis=1), N_EXPERTS - 1)
    onehot = (e[:, None] == jnp.arange(N_EXPERTS, dtype=i32)[None, :]).astype(i32)
    look = lambda tbl: jnp.sum(onehot * tbl[None, :], axis=1)
    blk = look(first_blk) + idx - look(vis_start)
    lo = jnp.maximum(look(starts), blk * MOE_ROWS) - blk * MOE_ROWS
    hi = jnp.minimum(look(ends), (blk + 1) * MOE_ROWS) - blk * MOE_ROWS
    valid = idx < total
    blk = jnp.where(valid, blk, nblk - 1).astype(i32)
    lo = jnp.where(valid, lo, 0).astype(i32)
    hi = jnp.where(valid, hi, 0).astype(i32)
    prev_blk = jnp.concatenate([jnp.full((1,), -1, i32), blk[:-1]])
    first = (valid & (blk != prev_blk)).astype(i32)
    prev_e = jnp.concatenate([jnp.full((1,), -1, i32), e[:-1]])
    newe = (valid & (e != prev_e)).astype(i32)
    order = jnp.cumsum(newe) - 1
    slot = jnp.where(newe == 1, order % 2, 0).astype(i32)
    cum_act = jnp.cumsum((counts > 0).astype(i32))
    n_uniq = cum_act[-1]
    kk = jnp.arange(N_EXPERTS + 2, dtype=i32)
    uniq_e = jnp.sum((cum_act[None, :] <= kk[:, None]).astype(i32), axis=1)
    uniq_e = jnp.where(kk < n_uniq, uniq_e, -1)
    ahead = jnp.sum((kk[None, :] == (order + 2)[:, None]).astype(i32) * uniq_e[None, :], axis=1)
    pre = jnp.where(newe == 1, ahead, -1).astype(i32)
    init = uniq_e[0:2].astype(i32)
    return starts.astype(i32), (blk, lo, hi, first, newe, slot, pre, init)


def kernel(x_prompt, x_sample, state_delta, state_qkv_conv, state_short_conv, norm1_w, w_in, conv_a_w, a_log, dt_bias, out_norm_w, w_branch_a, conv_b_w, w_branch_b, w_o, norm2_w, router_group_w, router_group_b, router_expert_w, router_expert_b, w_gate, w_up, w_down, final_norm_w):
    assert norm1_w.shape[0] == 1, "single-layer trunk"
    bp, tp, d = x_prompt.shape
    bs, ts, _ = x_sample.shape
    assert d == D_MODEL and ts == 1
    n_p = bp * tp
    n_s = bs
    n_all = n_p + n_s

    w_perm = _wprep(jnp.transpose(w_in[0]))
    wa = w_branch_a[0].astype(bf16)
    wb = w_branch_b[0].astype(bf16)
    wo = w_o[0].astype(bf16)
    pad = lambda v: jnp.zeros((1, BA_W), f32).at[0, N_HEADS:2 * N_HEADS].set(v)
    alog_row = pad(a_log[0])
    dtb_row = pad(dt_bias[0])
    onw_row = out_norm_w[0].reshape(1, HEAD)
    cwa = conv_a_w[0]
    cwb = conv_b_w[0]
    rw = jnp.zeros((D_MODEL, LANE), f32)
    rw = rw.at[:, 0:N_EXPERTS].set(router_expert_w[0]).at[:, N_EXPERTS:N_EXPERTS + N_GROUPS].set(router_group_w[0])
    rwh = rw.astype(bf16)
    rwl = (rw - rwh.astype(f32)).astype(bf16)
    rb_row = jnp.zeros((1, LANE), f32)
    rb_row = rb_row.at[0, 0:N_EXPERTS].set(router_expert_b[0]).at[0, N_EXPERTS:N_EXPERTS + N_GROUPS].set(router_group_b[0])
    n2_row = norm2_w[0].reshape(1, D_MODEL)

    xp2 = x_prompt.reshape(n_p, D_MODEL)
    proj_p, tails = _inproj_conv(xp2, norm1_w[0], w_perm, cwa, tp)
    tiles_per_seq = tails.shape[0] // bp
    nca_p = tails.reshape(bp, tiles_per_seq, 8, CONV_PAD_W)[:, -1, 8 - (CONV_A - 1):8, 0:QKV_W]
    o_p, y_p, sd_p, ncb_p = _delta_prompt(proj_p.reshape(bp, tp, PROJ_W), cwb, alog_row, dtb_row,
                                          onw_row, nb_step=4 if bp % 4 == 0 else (2 if bp % 2 == 0 else 1))
    cnt0 = jnp.zeros((1, LANE), f32)
    x1_p, h2_p, mi_p, mw_p, cnt_p = _mix_route(xp2, o_p.reshape(n_p, QK_W), y_p.reshape(n_p, SC_W), proj_p,
                                               wa, wb, wo, n2_row, rwh, rwl, rb_row, cnt0)

    xs2 = x_sample.reshape(n_s, D_MODEL)
    proj_s = _inproj(xs2, norm1_w[0], w_perm)
    bufa_t = jnp.transpose(state_qkv_conv[0], (1, 0, 2))
    bufb_t = jnp.transpose(state_short_conv[0], (1, 0, 2))
    q_s, k_s, v_s, beta_s, eg_s, y_s, nbufa_t, nbufb_t = _sample_prep(proj_s, bufa_t, bufb_t, cwa, cwb,
                                                                      alog_row, dtb_row)
    h3 = lambda a: a.reshape(n_s, N_HEADS, HEAD)
    z_s = proj_s[:, COL_Z:COL_Z + QK_W]
    sd_s, o_s = _sample_step(state_delta[0], h3(q_s), h3(k_s), h3(v_s), h3(beta_s), h3(eg_s), h3(z_s), onw_row)
    o_s2 = o_s.reshape(n_s, QK_W).astype(bf16)
    x1_s, h2_s, mi_s, mw_s, cnt = _mix_route(xs2, o_s2, y_s, proj_s, wa, wb, wo, n2_row, rwh, rwl, rb_row, cnt_p)

    counts = cnt[0, 0:N_EXPERTS].astype(i32)
    starts, items = _work_items(counts, 2 * n_all)
    mi_flat = jnp.concatenate([mi_p[:, 0:MI_W], mi_s[:, 0:MI_W]], axis=0).reshape(MI_W * n_all)
    xs_sorted = _dispatch(h2_p, h2_s, mi_flat, starts)
    ys = _moe(xs_sorted, w_gate[0], w_up[0], w_down[0], items)
    y_prompt, y_sample = _combine(x1_p, mw_p, x1_s, mw_s, final_norm_w.reshape(1, D_MODEL), ys, mi_flat, starts)

    return (y_prompt.reshape(bp, tp, D_MODEL),
            y_sample.reshape(bs, ts, D_MODEL),
            sd_p[None],
            nca_p[None],
            ncb_p[None],
            sd_s[None],
            jnp.transpose(nbufa_t, (1, 0, 2))[None],
            jnp.transpose(nbufb_t, (1, 0, 2))[None])
```

```python
import functools

import jax
import jax.numpy as jnp
from jax import lax
from jax.experimental import pallas as pl
from jax.experimental.pallas import tpu as pltpu

f32 = jnp.float32
bf16 = jnp.bfloat16
i32 = jnp.int32

EPS = 1e-6
LANE = 128
D_MODEL = 2048
N_HEADS = 8
HEAD = 128
QK_W = N_HEADS * HEAD
QKV_W = 3 * QK_W
SC_W = 1024
CONV_A = 4
CONV_B = 3
CHUNK = 64
GROUP_HEADS = 4
N_EXPERTS = 64
N_GROUPS = 8
EXPERTS_PER_GROUP = 8
D_FF = 512
MOE_ROWS = 128

COL_QKV = 0
COL_BCX = 3072
COL_GA = 6144
COL_GB = 8192
COL_Z = 10240
COL_BA = 11264
BA_W = 256
PROJ_W = 11520
PROJ_TN = 1280

VMEM_LIMIT = 56 * 1024 * 1024


def _dot(a, b):
    return jnp.dot(a, b, preferred_element_type=f32)


def _dot_nt(a, b):
    return lax.dot_general(a, b, (((1,), (1,)), ((), ())), preferred_element_type=f32)


def _split(x, n):
    parts = []
    r = x
    for i in range(n):
        p = r.astype(bf16)
        parts.append(p)
        if i + 1 < n:
            r = r - p.astype(f32)
    return parts


def _dot_lsplit(x, m, n=3):
    rows = x.shape[0]
    d = _dot(jnp.concatenate(_split(x, n), axis=0), m)
    acc = d[0:rows]
    for i in range(1, n):
        acc = acc + d[i * rows:(i + 1) * rows]
    return acc


def _dot_rsplit(m, x, n=3):
    cols = x.shape[1]
    d = _dot(m, jnp.concatenate(_split(x, n), axis=1))
    acc = d[:, 0:cols]
    for i in range(1, n):
        acc = acc + d[:, i * cols:(i + 1) * cols]
    return acc


def _silu(x):
    return x * jax.nn.sigmoid(x)


def _softplus(x):
    return jnp.maximum(x, 0.0) + jnp.log(1.0 + jnp.exp(-jnp.abs(x)))


def _cparams(sem):
    return pltpu.CompilerParams(dimension_semantics=sem, vmem_limit_bytes=VMEM_LIMIT)


def _inproj_kernel(x_ref, nw_ref, w_ref, o_ref, h_ref, *, rows):
    @pl.when(pl.program_id(1) == 0)
    def _():
        def body(r, c):
            sl = pl.ds(pl.multiple_of(r * rows, rows), rows)
            x = x_ref[sl, :]
            ms = jnp.mean(x * x, axis=-1, keepdims=True)
            h_ref[sl, :] = (x * lax.rsqrt(ms + EPS) * nw_ref[...]).astype(bf16)
            return c
        lax.fori_loop(0, x_ref.shape[0] // rows, body, 0)

    o_ref[...] = _dot_nt(h_ref[...], w_ref[...])


def _inproj(x2d, norm_w, w_bf16):
    n = x2d.shape[0]
    tm = min(1024, n)
    assert n % tm == 0 and PROJ_W % PROJ_TN == 0
    return pl.pallas_call(
        functools.partial(_inproj_kernel, rows=min(128, tm)),
        out_shape=jax.ShapeDtypeStruct((n, PROJ_W), f32),
        grid=(n // tm, PROJ_W // PROJ_TN),
        in_specs=[pl.BlockSpec((tm, D_MODEL), lambda i, j: (i, 0)),
                  pl.BlockSpec((1, D_MODEL), lambda i, j: (0, 0)),
                  pl.BlockSpec((PROJ_TN, D_MODEL), lambda i, j: (j, 0))],
        out_specs=pl.BlockSpec((tm, PROJ_TN), lambda i, j: (i, j)),
        scratch_shapes=[pltpu.VMEM((tm, D_MODEL), bf16)],
        compiler_params=_cparams(("arbitrary", "arbitrary")),
        name="inproj",
    )(x2d, norm_w.reshape(1, D_MODEL), w_bf16)


CONV_TILES = 3
CONV_COLS = 2 * HEAD
CONV_ROWS = 128
CONV_PAD_W = CONV_TILES * PROJ_TN


def _qkv_kind(col):
    return "q" if col < QK_W else "k" if col < 2 * QK_W else "v" if col < QKV_W else "raw"


def _inproj_conv_kernel(x_ref, nw_ref, w_ref, cw_ref, o_ref, tail_ref, h_ref, hist_ref, raw_ref, *,
                        rows, tiles_per_seq):
    i = pl.program_id(0)
    j = pl.program_id(1)
    tm = x_ref.shape[0]

    @pl.when(j == 0)
    def _():
        def body(r, c):
            sl = pl.ds(pl.multiple_of(r * rows, rows), rows)
            x = x_ref[sl, :]
            ms = jnp.mean(x * x, axis=-1, keepdims=True)
            h_ref[sl, :] = (x * lax.rsqrt(ms + EPS) * nw_ref[...]).astype(bf16)
            return c
        lax.fori_loop(0, tm // rows, body, 0)

    @pl.when((i == 0) & (j == 0))
    def _():
        hist_ref[...] = jnp.zeros(hist_ref.shape, f32)

    @pl.when(j >= CONV_TILES)
    def _():
        o_ref[...] = _dot_nt(h_ref[...], w_ref[...])

    seq_start = lax.rem(i, tiles_per_seq) == 0
    for jj in range(CONV_TILES):
        @pl.when(j == jj)
        def _():
            def matmul_chunk(idx, c0):
                raw_ref[idx % 2] = _dot_nt(h_ref[...], w_ref[c0:c0 + CONV_COLS, :])

            def conv_chunk(idx, c0):
                cs = slice(c0, c0 + CONV_COLS)
                raw = raw_ref.at[idx % 2]
                tail = raw[tm - 8:tm, :]
                tail_ref[0, :, cs] = tail
                kinds = [_qkv_kind(jj * PROJ_TN + c0 + g * HEAD) for g in range(CONV_COLS // HEAD)]
                if kinds[0] == "raw":
                    o_ref[:, cs] = raw[...]
                    return
                hist = jnp.where(seq_start, 0.0, hist_ref[jj, :, cs])
                for rc in range(tm // CONV_ROWS):
                    r0 = rc * CONV_ROWS
                    if rc > 0:
                        xe = raw[r0 - 8:r0 + CONV_ROWS, :]
                    else:
                        xe = jnp.concatenate([hist, raw[0:CONV_ROWS, :]], axis=0)
                    acc = pltpu.roll(xe, 3, axis=0)[8:] * cw_ref[0:1, cs]
                    acc = acc + pltpu.roll(xe, 2, axis=0)[8:] * cw_ref[1:2, cs]
                    acc = acc + pltpu.roll(xe, 1, axis=0)[8:] * cw_ref[2:3, cs]
                    acc = acc + xe[8:] * cw_ref[3:4, cs]
                    act = _silu(acc)
                    for g, kind in enumerate(kinds):
                        ah = act[:, g * HEAD:(g + 1) * HEAD]
                        if kind != "v":
                            ss = jnp.sum(ah * ah, axis=-1, keepdims=True)
                            inv = lax.rsqrt(ss + EPS)
                            ah = ah * (inv * (HEAD ** -0.5) if kind == "q" else inv)
                        o_ref[r0:r0 + CONV_ROWS, c0 + g * HEAD:c0 + (g + 1) * HEAD] = ah
                hist_ref[jj, :, cs] = tail

            chunks = list(range(0, PROJ_TN, CONV_COLS))
            matmul_chunk(0, chunks[0])
            for idx in range(1, len(chunks)):
                matmul_chunk(idx, chunks[idx])
                conv_chunk(idx - 1, chunks[idx - 1])
            conv_chunk(len(chunks) - 1, chunks[-1])


def _inproj_conv(x2d, norm_w, w_bf16, cwa, seq_len):
    n = x2d.shape[0]
    tm = min(1024, seq_len)
    assert n % tm == 0 and seq_len % tm == 0 and PROJ_W % PROJ_TN == 0 and tm % CONV_ROWS == 0
    assert QKV_W % CONV_COLS == 0 and PROJ_TN % CONV_COLS == 0
    cw_pad = jnp.zeros((CONV_A, CONV_PAD_W), f32).at[:, 0:QKV_W].set(cwa)
    last = CONV_TILES - 1
    return pl.pallas_call(
        functools.partial(_inproj_conv_kernel, rows=min(128, tm), tiles_per_seq=seq_len // tm),
        out_shape=(jax.ShapeDtypeStruct((n, PROJ_W), f32),
                   jax.ShapeDtypeStruct((n // tm, 8, CONV_PAD_W), f32)),
        grid=(n // tm, PROJ_W // PROJ_TN),
        in_specs=[pl.BlockSpec((tm, D_MODEL), lambda i, j: (i, 0)),
                  pl.BlockSpec((1, D_MODEL), lambda i, j: (0, 0)),
                  pl.BlockSpec((PROJ_TN, D_MODEL), lambda i, j: (j, 0)),
                  pl.BlockSpec((CONV_A, PROJ_TN), lambda i, j: (0, jnp.minimum(j, last)))],
        out_specs=(pl.BlockSpec((tm, PROJ_TN), lambda i, j: (i, j)),
                   pl.BlockSpec((1, 8, PROJ_TN), lambda i, j: (i, 0, jnp.minimum(j, last)))),
        scratch_shapes=[pltpu.VMEM((tm, D_MODEL), bf16), pltpu.VMEM((CONV_TILES, 8, PROJ_TN), f32),
                        pltpu.VMEM((2, tm, CONV_COLS), f32)],
        compiler_params=_cparams(("arbitrary", "arbitrary")),
        name="inproj_conv",
    )(x2d, norm_w.reshape(1, D_MODEL), w_bf16, cw_pad)


W_IN_COLS = 11280
WPREP_TN = 1024
WPREP_SHIFT = 16


def _wprep_kernel(a_ref, b_ref, o_ref):
    j = pl.program_id(0)
    keep = WPREP_TN - WPREP_SHIFT

    @pl.when((j < 3) | (j == 10))
    def _():
        o_ref[...] = a_ref[...].astype(bf16)

    @pl.when((j >= 3) & (j < 10))
    def _():
        o_ref[0:keep, :] = a_ref[WPREP_SHIFT:WPREP_TN, :].astype(bf16)
        o_ref[keep:WPREP_TN, :] = b_ref[...].astype(bf16)

    @pl.when(j == 11)
    def _():
        o_ref[0:WPREP_SHIFT, :] = a_ref[0:WPREP_SHIFT, :].astype(bf16)
        o_ref[WPREP_SHIFT:WPREP_TN, :] = jnp.zeros((keep, D_MODEL), bf16)


def _wprep(w_in_t):
    assert w_in_t.shape == (W_IN_COLS, D_MODEL) and 2 * N_HEADS == WPREP_SHIFT
    n_blk = pl.cdiv(PROJ_W, WPREP_TN)

    def a_map(j):
        return (jnp.where(j < 3, j, jnp.where(j < 10, j + 1, jnp.where(j == 10, 3, 4))), 0)

    def b_map(j):
        return (jnp.minimum((WPREP_TN // WPREP_SHIFT) * (j + 2), W_IN_COLS // WPREP_SHIFT - 1), 0)

    return pl.pallas_call(
        _wprep_kernel,
        out_shape=jax.ShapeDtypeStruct((PROJ_W, D_MODEL), bf16),
        grid=(n_blk,),
        in_specs=[pl.BlockSpec((WPREP_TN, D_MODEL), a_map),
                  pl.BlockSpec((WPREP_SHIFT, D_MODEL), b_map)],
        out_specs=pl.BlockSpec((WPREP_TN, D_MODEL), lambda j: (j, 0)),
        compiler_params=_cparams(("arbitrary",)),
        name="wprep",
    )(w_in_t, w_in_t)


def _head_l2norm(a, scale):
    outs = []
    for h in range(N_HEADS):
        ah = a[:, h * HEAD:(h + 1) * HEAD]
        ss = jnp.sum(ah * ah, axis=-1, keepdims=True)
        n = ah * lax.rsqrt(ss + EPS)
        outs.append(n * scale if scale != 1.0 else n)
    return outs


def _delta_prompt_kernel(qkv_ref, bcx_ref, z_ref, ba_ref, cwb_ref, alog_ref, dtb_ref, onw_ref,
                         e64_ref,
                         o_ref, y_ref, snew_ref, ncb_ref,
                         s_ref, xb_ref, *, nb_step):
    C = CHUNK
    G = GROUP_HEADS
    R = G * C
    t = pl.program_id(1)
    nt = pl.num_programs(1)

    @pl.when(t == 0)
    def _():
        s_ref[...] = jnp.zeros(s_ref.shape, f32)
        xb_ref[:, 0:8, :] = jnp.zeros((nb_step, 8, SC_W), f32)

    rr = lax.broadcasted_iota(i32, (R, R), 0)
    cc = lax.broadcasted_iota(i32, (R, R), 1)
    same = (rr >> 6) == (cc >> 6)
    incl = same & (rr >= cc)
    strict = same & (rr > cc)
    eye = jnp.where(rr == cc, 1.0, 0.0).astype(f32)
    r2 = lax.broadcasted_iota(i32, (R, G * HEAD), 0)
    c2 = lax.broadcasted_iota(i32, (R, G * HEAD), 1)
    bdmask = (r2 >> 6) == (c2 >> 7)
    r3 = lax.broadcasted_iota(i32, (C, C), 0)
    c3 = lax.broadcasted_iota(i32, (C, C), 1)
    ltri = jnp.where(r3 >= c3, 1.0, 0.0).astype(bf16)
    r4 = lax.broadcasted_iota(i32, (C, R), 0)
    c4 = lax.broadcasted_iota(i32, (C, R), 1)
    ident_t = r4 == (c4 & (C - 1))
    ones8 = jnp.ones((8, C), bf16)

    nbs = range(nb_step)
    units = [(nb, g) for nb in nbs for g in range(N_HEADS // G)]
    heads = lambda g: range(g * G, (g + 1) * G)

    qn = [[qkv_ref[nb, :, h * HEAD:(h + 1) * HEAD] for h in range(N_HEADS)] for nb in nbs]
    kn = [[qkv_ref[nb, :, QK_W + h * HEAD:QK_W + (h + 1) * HEAD] for h in range(N_HEADS)] for nb in nbs]
    vv = [qkv_ref[nb, :, 2 * QK_W:3 * QK_W] for nb in nbs]

    bts = [ba_ref[nb] for nb in nbs]
    beta_all = [jax.nn.sigmoid(bt) for bt in bts]
    g_all = [-(jnp.exp(alog_ref[...]) * _softplus(bt + dtb_ref[...])) for bt in bts]
    gc_small = [_dot_rsplit(ltri, ga) for ga in g_all]
    gl_small = [gc[C - 1:C, :] for gc in gc_small]

    k_st, q_st, kb, vb, kbg, qd, kd, gc_col = ({} for _ in range(8))
    for u in units:
        nb, g = u
        hs = heads(g)
        k_st[u] = jnp.concatenate([kn[nb][h] for h in hs], axis=0)
        q_st[u] = jnp.concatenate([qn[nb][h] for h in hs], axis=0)
        v_st = jnp.concatenate([vv[nb][:, h * HEAD:(h + 1) * HEAD] for h in hs], axis=0)
        beta_col = jnp.concatenate([beta_all[nb][:, h:h + 1] for h in hs], axis=0)
        gc_col[u] = jnp.concatenate([gc_small[nb][:, 8 + h:9 + h] for h in hs], axis=0)
        gl_col = jnp.concatenate(
            [jnp.broadcast_to(gl_small[nb][:, 8 + h:9 + h], (C, 1)) for h in hs], axis=0)
        kb[u] = k_st[u] * beta_col
        vb[u] = v_st * beta_col
        egc = jnp.exp(gc_col[u])
        kbg[u] = kb[u] * egc
        qd[u] = q_st[u] * egc
        kd[u] = k_st[u] * jnp.exp(gl_col - gc_col[u])

    gx = {u: _dot_lsplit(gc_small[u[0]], e64_ref[u[1]]) for u in units}
    crow = {u: _dot_rsplit(ones8, jnp.where(ident_t, gx[u], 0.0))[0:1, :] for u in units}
    a = {u: _dot_nt(jnp.concatenate([kb[u], q_st[u]], axis=0).astype(bf16), k_st[u].astype(bf16))
         for u in units}
    dec = {u: jnp.where(incl, jnp.exp(jnp.where(incl, gc_col[u] - crow[u], 0.0)), 0.0) for u in units}
    nm = {u: jnp.where(strict, -(a[u][0:R] * dec[u]), 0.0) for u in units}
    qkm = {u: a[u][R:2 * R] * dec[u] for u in units}

    p = {u: eye + nm[u] for u in units}
    nk = {}
    for u in units:
        nmb = nm[u].astype(bf16)
        nk[u] = _dot(nmb, nmb)
    for _ in range(4):
        for u in units:
            x = _dot(jnp.concatenate([p[u], nk[u]], axis=0).astype(bf16), nk[u].astype(bf16))
            p[u] = p[u] + x[0:R]
            nk[u] = x[R:2 * R]
    for u in units:
        p[u] = p[u] + _dot(p[u].astype(bf16), nk[u].astype(bf16))
    uw = {u: _dot(p[u].astype(bf16), jnp.concatenate([vb[u], kbg[u]], axis=1).astype(bf16)) for u in units}

    ws = {}
    for u in units:
        nb, g = u
        for j, h in enumerate(heads(g)):
            sh = s_ref[nb, :, h * HEAD:(h + 1) * HEAD]
            lhs = jnp.concatenate([uw[u][j * C:(j + 1) * C, HEAD:2 * HEAD], qd[u][j * C:(j + 1) * C]], axis=0)
            ws[u, j] = _dot(lhs.astype(bf16), sh.astype(bf16))
    o_heads = {}
    for u in units:
        nb, g = u
        vnew_st = jnp.concatenate([uw[u][j * C:(j + 1) * C, 0:HEAD] - ws[u, j][0:C] for j in range(G)], axis=0)
        o_st = (jnp.concatenate([ws[u, j][C:2 * C] for j in range(G)], axis=0)
                + _dot(qkm[u].astype(bf16), vnew_st.astype(bf16)))
        vbd = jnp.where(bdmask, jnp.concatenate([vnew_st] * G, axis=1), 0.0)
        lo = g * G * HEAD
        hi = lo + G * HEAD
        gl_row = jnp.concatenate(
            [jnp.broadcast_to(jnp.exp(gl_small[nb][:, 8 + h:9 + h]), (1, HEAD)) for h in heads(g)], axis=1)
        s_ref[nb, :, lo:hi] = s_ref[nb, :, lo:hi] * gl_row + _dot(kd[u].T.astype(bf16), vbd.astype(bf16))
        for j, h in enumerate(heads(g)):
            o_heads[nb, h] = o_st[j * C:(j + 1) * C]

    for nb in nbs:
        zt = z_ref[nb]
        for h in range(N_HEADS):
            oh = o_heads[nb, h]
            ms = jnp.mean(oh * oh, axis=-1, keepdims=True)
            zh = zt[:, h * HEAD:(h + 1) * HEAD]
            on = oh * lax.rsqrt(ms + EPS) * onw_ref[...] * _silu(zh)
            o_ref[nb, :, h * HEAD:(h + 1) * HEAD] = on.astype(bf16)

    for nb in nbs:
        bcx = bcx_ref[nb]
        cx = bcx[:, SC_W:2 * SC_W] * bcx[:, 2 * SC_W:3 * SC_W]
        xb_ref[nb, 8:8 + C, :] = cx
        ce = xb_ref[nb]
        cv = pltpu.roll(ce, 2, axis=0)[8:8 + C] * cwb_ref[0:1, :]
        cv = cv + pltpu.roll(ce, 1, axis=0)[8:8 + C] * cwb_ref[1:2, :]
        cv = cv + cx * cwb_ref[2:3, :]
        y_ref[nb] = (bcx[:, 0:SC_W] * cv).astype(bf16)
        xb_ref[nb, 0:8, :] = xb_ref[nb, C:C + 8, :]

    @pl.when(t == nt - 1)
    def _():
        for nb in range(nb_step):
            for h in range(N_HEADS):
                snew_ref[nb, h] = s_ref[nb, :, h * HEAD:(h + 1) * HEAD]
            ncb_ref[nb] = xb_ref[nb, 6:8, :]


def _expand_consts():
    lane = jnp.arange(BA_W)[:, None]
    col = jnp.arange(QK_W)[None, :]
    eb = (lane == (col >> 7)).astype(bf16)
    eg = (lane == (8 + (col >> 7))).astype(bf16)
    col64 = jnp.arange(GROUP_HEADS * CHUNK)[None, :]
    e64 = jnp.stack([(lane == (8 + g * GROUP_HEADS + (col64 >> 6))).astype(bf16)
                     for g in range(N_HEADS // GROUP_HEADS)], axis=0)
    return eb, eg, e64


def _delta_prompt(proj3, cwb, alog_row, dtb_row, onw_row, nb_step):
    b, t, _ = proj3.shape
    assert t % CHUNK == 0 and b % nb_step == 0
    _, _, e64 = _expand_consts()
    c = CHUNK
    const2 = lambda bi, ti: (0, 0)
    outs = pl.pallas_call(
        functools.partial(_delta_prompt_kernel, nb_step=nb_step),
        out_shape=(jax.ShapeDtypeStruct((b, t, QK_W), bf16),
                   jax.ShapeDtypeStruct((b, t, SC_W), bf16),
                   jax.ShapeDtypeStruct((b, N_HEADS, HEAD, HEAD), f32),
                   jax.ShapeDtypeStruct((b, CONV_B - 1, SC_W), f32)),
        grid=(b // nb_step, t // c),
        in_specs=[pl.BlockSpec((nb_step, c, QKV_W), lambda bi, ti: (bi, ti, COL_QKV // QKV_W)),
                  pl.BlockSpec((nb_step, c, QKV_W), lambda bi, ti: (bi, ti, COL_BCX // QKV_W)),
                  pl.BlockSpec((nb_step, c, QK_W), lambda bi, ti: (bi, ti, COL_Z // QK_W)),
                  pl.BlockSpec((nb_step, c, BA_W), lambda bi, ti: (bi, ti, COL_BA // BA_W)),
                  pl.BlockSpec((CONV_B, SC_W), const2),
                  pl.BlockSpec((1, BA_W), const2),
                  pl.BlockSpec((1, BA_W), const2),
                  pl.BlockSpec((1, HEAD), const2),
                  pl.BlockSpec((N_HEADS // GROUP_HEADS, BA_W, GROUP_HEADS * CHUNK), lambda bi, ti: (0, 0, 0))],
        out_specs=(pl.BlockSpec((nb_step, c, QK_W), lambda bi, ti: (bi, ti, 0)),
                   pl.BlockSpec((nb_step, c, SC_W), lambda bi, ti: (bi, ti, 0)),
                   pl.BlockSpec((nb_step, N_HEADS, HEAD, HEAD), lambda bi, ti: (bi, 0, 0, 0)),
                   pl.BlockSpec((nb_step, CONV_B - 1, SC_W), lambda bi, ti: (bi, 0, 0))),
        scratch_shapes=[pltpu.VMEM((nb_step, HEAD, QK_W), f32),
                        pltpu.VMEM((nb_step, 8 + c, SC_W), f32)],
        compiler_params=_cparams(("arbitrary", "arbitrary")),
        name="delta_prompt",
    )(proj3, proj3, proj3, proj3, cwb, alog_row, dtb_row, onw_row, e64)
    return outs


def _sample_prep_kernel(p_ref, bufa_ref, bufb_ref, cwa_ref, cwb_ref, alog_ref, dtb_ref, eb_ref, eg_ref,
                        q_ref, k_ref, v_ref, beta_ref, eg_out_ref, y_ref, nbufa_ref, nbufb_ref):
    def conv_sec(lo):
        hi = lo + QK_W
        raw = p_ref[:, COL_QKV + lo:COL_QKV + hi]
        acc = bufa_ref[0, :, lo:hi] * cwa_ref[0:1, lo:hi]
        acc = acc + bufa_ref[1, :, lo:hi] * cwa_ref[1:2, lo:hi]
        acc = acc + bufa_ref[2, :, lo:hi] * cwa_ref[2:3, lo:hi]
        acc = acc + raw * cwa_ref[3:4, lo:hi]
        nbufa_ref[0, :, lo:hi] = bufa_ref[1, :, lo:hi]
        nbufa_ref[1, :, lo:hi] = bufa_ref[2, :, lo:hi]
        nbufa_ref[2, :, lo:hi] = raw
        return _silu(acc)

    qn = _head_l2norm(conv_sec(0), HEAD ** -0.5)
    kn = _head_l2norm(conv_sec(QK_W), 1.0)
    for h in range(N_HEADS):
        q_ref[:, h * HEAD:(h + 1) * HEAD] = qn[h]
        k_ref[:, h * HEAD:(h + 1) * HEAD] = kn[h]
    v_ref[...] = conv_sec(2 * QK_W)

    bt = p_ref[:, COL_BA:COL_BA + BA_W]
    beta_all = jax.nn.sigmoid(bt)
    g_all = -(jnp.exp(alog_ref[...]) * _softplus(bt + dtb_ref[...]))
    beta_ref[...] = _dot_lsplit(beta_all, eb_ref[...])
    eg_out_ref[...] = jnp.exp(_dot_lsplit(g_all, eg_ref[...]))

    bg = p_ref[:, COL_BCX:COL_BCX + SC_W]
    cx = p_ref[:, COL_BCX + SC_W:COL_BCX + 2 * SC_W] * p_ref[:, COL_BCX + 2 * SC_W:COL_BCX + 3 * SC_W]
    cv = bufb_ref[0] * cwb_ref[0:1, :]
    cv = cv + bufb_ref[1] * cwb_ref[1:2, :]
    cv = cv + cx * cwb_ref[2:3, :]
    y_ref[...] = (bg * cv).astype(bf16)
    nbufb_ref[0] = bufb_ref[1]
    nbufb_ref[1] = cx


def _sample_prep(proj_s, bufa_t, bufb_t, cwa, cwb, alog_row, dtb_row):
    n = proj_s.shape[0]
    eb, eg, _ = _expand_consts()
    row = jax.ShapeDtypeStruct((n, QK_W), f32)
    return pl.pallas_call(
        _sample_prep_kernel,
        out_shape=(row, row, row, row, row,
                   jax.ShapeDtypeStruct((n, SC_W), bf16),
                   jax.ShapeDtypeStruct((CONV_A - 1, n, QKV_W), f32),
                   jax.ShapeDtypeStruct((CONV_B - 1, n, SC_W), f32)),
        compiler_params=pltpu.CompilerParams(vmem_limit_bytes=VMEM_LIMIT),
        name="sample_prep",
    )(proj_s, bufa_t, bufb_t, cwa, cwb, alog_row, dtb_row, eb, eg)


def _sample_step_kernel(s_ref, q_ref, k_ref, v_ref, beta_ref, eg_ref, z_ref, onw_ref,
                        snew_ref, o_ref, *, bb):
    w = N_HEADS * HEAD
    r8 = lax.broadcasted_iota(i32, (N_HEADS, w), 0)
    c8 = lax.broadcasted_iota(i32, (N_HEADS, w), 1)
    mask8 = r8 == (c8 >> 7)
    zpad_k = jnp.zeros((HEAD - N_HEADS, HEAD), f32)
    zpad_d = jnp.zeros((HEAD - N_HEADS, w), f32)
    for b in range(bb):
        s_all = jnp.concatenate([s_ref[b, h] for h in range(N_HEADS)], axis=1)
        eg8 = eg_ref[b]
        eg_row = jnp.concatenate([eg8[h:h + 1, :] for h in range(N_HEADS)], axis=1)
        s_dec = s_all * eg_row
        k8 = k_ref[b]
        x = _dot(k8.astype(bf16), s_dec.astype(bf16))
        v_t = jnp.concatenate([v_ref[b]] * N_HEADS, axis=1)
        b_t = jnp.concatenate([beta_ref[b]] * N_HEADS, axis=1)
        d_bd = jnp.where(mask8, (v_t - x) * b_t, 0.0)
        kt = jnp.concatenate([k8, zpad_k], axis=0).T
        k_hi, k_lo = _split(kt, 2)
        dh = d_bd.astype(bf16).astype(f32)
        d_hi = jnp.concatenate([dh, zpad_d], axis=0).astype(bf16)
        d_lo = jnp.concatenate([d_bd - dh, zpad_d], axis=0).astype(bf16)
        s_new = s_dec + (_dot(k_hi, d_hi) + _dot(k_hi, d_lo) + _dot(k_lo, d_hi))
        yv = jnp.where(mask8, _dot(q_ref[b].astype(bf16), s_new.astype(bf16)), 0.0)
        o8 = yv[:, 0:HEAD]
        for j in range(1, N_HEADS):
            o8 = o8 + yv[:, j * HEAD:(j + 1) * HEAD]
        ms = jnp.mean(o8 * o8, axis=-1, keepdims=True)
        o_ref[b] = o8 * lax.rsqrt(ms + EPS) * onw_ref[...] * _silu(z_ref[b])
        for h in range(N_HEADS):
            snew_ref[b, h] = s_new[:, h * HEAD:(h + 1) * HEAD]


def _sample_step(state, q, k, v, beta, eg, z, onw_row, bb=4):
    n = state.shape[0]
    assert n % bb == 0
    hspec = pl.BlockSpec((bb, N_HEADS, HEAD), lambda i: (i, 0, 0))
    sspec = pl.BlockSpec((bb, N_HEADS, HEAD, HEAD), lambda i: (i, 0, 0, 0))
    return pl.pallas_call(
        functools.partial(_sample_step_kernel, bb=bb),
        out_shape=(jax.ShapeDtypeStruct(state.shape, f32),
                   jax.ShapeDtypeStruct((n, N_HEADS, HEAD), f32)),
        grid=(n // bb,),
        in_specs=[sspec, hspec, hspec, hspec, hspec, hspec, hspec, pl.BlockSpec((1, HEAD), lambda i: (0, 0))],
        out_specs=(sspec, hspec),
        compiler_params=_cparams(("arbitrary",)),
        name="sample_step",
    )(state, q, k, v, beta, eg, z, onw_row)


def _mix_route_kernel(x_ref, o_ref, y_ref, ga_ref, gb_ref, wa_ref, wb_ref, wo_ref, n2_ref,
                      rwh_ref, rwl_ref, rb_ref, cnt_in_ref, x1_ref, h2_ref, mi_ref, mw_ref, cnt_ref):
    i = pl.program_id(0)
    tm = x_ref.shape[0]

    @pl.when(i == 0)
    def _():
        cnt_ref[...] = cnt_in_ref[...]

    oa = _dot(o_ref[...], wa_ref[...])
    ob = _dot(y_ref[...], wb_ref[...])
    merged = jax.nn.sigmoid(ga_ref[...]) * oa + jax.nn.sigmoid(gb_ref[...]) * ob
    x1 = x_ref[...] + _dot(merged.astype(bf16), wo_ref[...])
    x1_ref[...] = x1
    ms = jnp.mean(x1 * x1, axis=-1, keepdims=True)
    h2 = x1 * lax.rsqrt(ms + EPS) * n2_ref[...]
    h2_ref[...] = h2

    h_hi, h_lo = _split(h2, 2)
    logits = _dot(h_hi, rwh_ref[...]) + _dot(h_hi, rwl_ref[...]) + _dot(h_lo, rwh_ref[...]) + rb_ref[...]

    lane = lax.broadcasted_iota(i32, (tm, LANE), 1)
    lanef = lane.astype(f32)
    neg = jnp.float32(-jnp.inf)
    big = jnp.float32(1e9)
    gmask = (lane >= N_EXPERTS) & (lane < N_EXPERTS + N_GROUPS)
    gl = jnp.where(gmask, logits, neg)
    gmax = jnp.max(gl, axis=-1, keepdims=True)
    gidx = jnp.min(jnp.where(gl == gmax, lanef - N_EXPERTS, big), axis=-1, keepdims=True)
    gsum = jnp.sum(jnp.where(gmask, jnp.exp(gl - gmax), 0.0), axis=-1, keepdims=True)
    gprob = 1.0 / gsum

    emask = (lane < N_EXPERTS) & ((lane >> 3).astype(f32) == gidx)
    el = jnp.where(emask, logits, neg)
    emax = jnp.max(el, axis=-1, keepdims=True)
    pe = jnp.where(emask, jnp.exp(el - emax), 0.0)
    eprob = pe / jnp.sum(pe, axis=-1, keepdims=True)
    p1m = jnp.where(emask, eprob, -1.0)
    m1 = jnp.max(p1m, axis=-1, keepdims=True)
    i1 = jnp.min(jnp.where(p1m == m1, lanef, big), axis=-1, keepdims=True)
    p2m = jnp.where(lanef == i1, -1.0, p1m)
    m2 = jnp.max(p2m, axis=-1, keepdims=True)
    i2 = jnp.min(jnp.where(p2m == m2, lanef, big), axis=-1, keepdims=True)
    tot = m1 + m2
    c1 = m1 / tot * gprob
    c2 = m2 / tot * gprob

    oh1 = jnp.where(lanef == i1, 1.0, 0.0)
    oh2 = jnp.where(lanef == i2, 1.0, 0.0)
    ohs = oh1 + oh2
    rt = lax.broadcasted_iota(i32, (tm, tm), 0)
    ct = lax.broadcasted_iota(i32, (tm, tm), 1)
    lstrict = jnp.where(rt > ct, 1.0, 0.0).astype(bf16)
    cs = _dot(lstrict, ohs.astype(bf16)) + cnt_ref[...]
    rank1 = jnp.sum(cs * oh1, axis=-1, keepdims=True)
    rank2 = jnp.sum(cs * oh2, axis=-1, keepdims=True)
    cnt_ref[...] = cnt_ref[...] + jnp.sum(ohs, axis=0, keepdims=True)

    mi = jnp.where(lane == 0, i1, jnp.where(lane == 1, i2, jnp.where(lane == 2, rank1,
                                                                     jnp.where(lane == 3, rank2, 0.0))))
    mi_ref[...] = mi.astype(i32)
    mw_ref[...] = jnp.where(lane == 0, c1, jnp.where(lane == 1, c2, 0.0))


def _mix_route(x2d, o2d, y2d, proj2d, wa, wb, wo, n2_row, rwh, rwl, rb_row, cnt_in):
    n = x2d.shape[0]
    tm = min(256, n)
    assert n % tm == 0
    tok = lambda width: pl.BlockSpec((tm, width), lambda i: (i, 0))
    full = lambda a: pl.BlockSpec(a.shape, lambda i: (0,) * a.ndim)
    in_specs = [tok(D_MODEL), tok(QK_W), tok(SC_W),
                pl.BlockSpec((tm, D_MODEL), lambda i: (i, COL_GA // D_MODEL)),
                pl.BlockSpec((tm, D_MODEL), lambda i: (i, COL_GB // D_MODEL)),
                full(wa), full(wb), full(wo), full(n2_row), full(rwh), full(rwl), full(rb_row), full(cnt_in)]
    out_shape = (jax.ShapeDtypeStruct((n, D_MODEL), f32),
                 jax.ShapeDtypeStruct((n, D_MODEL), f32),
                 jax.ShapeDtypeStruct((n, LANE), i32),
                 jax.ShapeDtypeStruct((n, LANE), f32),
                 jax.ShapeDtypeStruct((1, LANE), f32))
    out_specs = (tok(D_MODEL), tok(D_MODEL), tok(LANE), tok(LANE),
                 pl.BlockSpec((1, LANE), lambda i: (0, 0)))
    return pl.pallas_call(
        _mix_route_kernel,
        out_shape=out_shape,
        grid=(n // tm,),
        in_specs=in_specs,
        out_specs=out_specs,
        compiler_params=_cparams(("arbitrary",)),
        name="mix_route",
    )(x2d, o2d, y2d, proj2d, proj2d, wa, wb, wo, n2_row, rwh, rwl, rb_row, cnt_in)


MI_W = 4
SUBLANE = 8
ROW_DMA_UNROLL = 8


def _dest_kernel(mi_ref, starts_ref, o_ref):
    mi = mi_ref[...]
    lane = lax.broadcasted_iota(i32, mi.shape, 1)
    st = starts_ref[...]

    def first_row(e_col):
        return jnp.sum(jnp.where(lane == e_col, st, 0.0), axis=-1, keepdims=True).astype(i32)

    d0 = first_row(mi[:, 0:1]) + mi[:, 2:3]
    d1 = first_row(mi[:, 1:2]) + mi[:, 3:4]
    o_ref[...] = jnp.where(lane == 0, d0 >> 3, jnp.where(lane == 1, d0 & (SUBLANE - 1),
                           jnp.where(lane == 2, d1 >> 3, jnp.where(lane == 3, d1 & (SUBLANE - 1), 0))))


def _dest_rows(mi, starts_row):
    n = mi.shape[0]
    tm = min(1024, n)
    assert n % tm == 0
    return pl.pallas_call(
        _dest_kernel,
        out_shape=jax.ShapeDtypeStruct((n, LANE), i32),
        grid=(n // tm,),
        in_specs=[pl.BlockSpec((tm, LANE), lambda i: (i, 0)), pl.BlockSpec((1, LANE), lambda i: (0, 0))],
        out_specs=pl.BlockSpec((tm, LANE), lambda i: (i, 0)),
        compiler_params=_cparams(("arbitrary",)),
        name="moe_dest",
    )(mi, starts_row)


def _dispatch_kernel(mi_ref, hp_ref, hs_ref, xs_ref, sem, *, np_tiles, tm):
    i = pl.program_id(0)

    def scatter_rows(h_ref):
        def start(t, c):
            for u in range(SUBLANE):
                rec = MI_W * (SUBLANE * t + u)
                for k in range(2):
                    dst = xs_ref.at[mi_ref[rec + 2 * k], pl.ds(mi_ref[rec + 2 * k + 1], 1)]
                    pltpu.make_async_copy(h_ref.at[t, pl.ds(u, 1)], dst, sem).start(priority=k)
            return c

        lax.fori_loop(0, tm // SUBLANE, start, 0)
        for k in range(2):
            pltpu.make_async_copy(h_ref, xs_ref.at[pl.ds(0, tm // SUBLANE)], sem).wait()

    @pl.when(i < np_tiles)
    def _():
        scatter_rows(hp_ref)

    @pl.when(i >= np_tiles)
    def _():
        scatter_rows(hs_ref)


def _dispatch(h2_p, h2_s, mi_flat):
    tm = MOE_ROWS
    n_p, n_s = h2_p.shape[0], h2_s.shape[0]
    assert n_p % tm == 0 and n_s == tm and tm % SUBLANE == 0
    np_tiles = n_p // tm
    tiled = lambda a: a.reshape(a.shape[0] // SUBLANE, SUBLANE, D_MODEL)
    blk = (tm // SUBLANE, SUBLANE, D_MODEL)
    return pl.pallas_call(
        functools.partial(_dispatch_kernel, np_tiles=np_tiles, tm=tm),
        out_shape=jax.ShapeDtypeStruct((2 * (n_p + n_s) // SUBLANE, SUBLANE, D_MODEL), f32),
        grid=(np_tiles + 1,),
        in_specs=[pl.BlockSpec((MI_W * tm,), lambda i: (i,), memory_space=pltpu.SMEM),
                  pl.BlockSpec(blk, lambda i: (jnp.minimum(i, np_tiles - 1), 0, 0)),
                  pl.BlockSpec(blk, lambda i: (0, 0, 0))],
        out_specs=pl.BlockSpec(memory_space=pl.ANY),
        scratch_shapes=[pltpu.SemaphoreType.DMA(())],
        compiler_params=_cparams(("arbitrary",)),
        name="moe_dispatch",
    )(mi_flat, tiled(h2_p), tiled(h2_s))


def _cast_rows(src_ref, dst_ref, col0=0, rows=256):
    width = src_ref.shape[1]

    def body(r, c):
        sl = pl.ds(pl.multiple_of(r * rows, rows), rows)
        dst_ref[sl, col0:col0 + width] = src_ref[sl, :].astype(bf16)
        return c
    lax.fori_loop(0, src_ref.shape[0] // rows, body, 0)


def _moe_kernel(blk_ref, lo_ref, hi_ref, first_ref, newe_ref, slot_ref, pre_ref, init_ref,
                x_ref, wg_hbm, wu_hbm, wd_hbm, o_ref,
                wg_f, wu_f, wd_f, wgu_b, wd_b, sem):
    i = pl.program_id(0)
    lo = lo_ref[i]
    hi = hi_ref[i]

    def weight_copies(e, slot):
        return [pltpu.make_async_copy(wg_hbm.at[e], wg_f.at[slot], sem.at[slot, 0]),
                pltpu.make_async_copy(wu_hbm.at[e], wu_f.at[slot], sem.at[slot, 1]),
                pltpu.make_async_copy(wd_hbm.at[e], wd_f.at[slot], sem.at[slot, 2])]

    def start_weights(e, slot):
        for cp, prio in zip(weight_copies(e, slot), (0, 1, 1)):
            cp.start(priority=prio)

    @pl.when(i == 0)
    def _():
        start_weights(init_ref[0], 0)

        @pl.when(init_ref[1] >= 0)
        def _():
            start_weights(init_ref[1], 1)

    @pl.when(newe_ref[i] == 1)
    def _():
        slot = slot_ref[i]
        cg, cu, cd = weight_copies(0, slot)
        cg.wait()
        _cast_rows(wg_f.at[slot], wgu_b, 0)
        cu.wait()
        _cast_rows(wu_f.at[slot], wgu_b, D_FF)
        cd.wait()
        _cast_rows(wd_f.at[slot], wd_b)

        @pl.when(pre_ref[i] >= 0)
        def _():
            start_weights(pre_ref[i], slot)

    @pl.when(hi > lo)
    def _():
        x = x_ref[...].astype(bf16)
        au = _dot(x, wgu_b[...])
        y = _dot((_silu(au[:, 0:D_FF]) * au[:, D_FF:2 * D_FF]).astype(bf16), wd_b[...])
        row = lax.broadcasted_iota(i32, y.shape, 0)
        ym = jnp.where((row >= lo) & (row < hi), y, 0.0)

        @pl.when(first_ref[i] == 1)
        def _():
            o_ref[...] = ym

        @pl.when(first_ref[i] == 0)
        def _():
            o_ref[...] = o_ref[...] + ym


def _moe(xs, w_gate, w_up, w_down, items):
    n_items = items[0].shape[0]
    rows = xs.shape[0]
    n_pref = len(items)
    xmap = lambda i, blk, *_: (blk[i], 0)
    grid_spec = pltpu.PrefetchScalarGridSpec(
        num_scalar_prefetch=n_pref,
        grid=(n_items,),
        in_specs=[pl.BlockSpec((MOE_ROWS, D_MODEL), xmap),
                  pl.BlockSpec(memory_space=pl.ANY),
                  pl.BlockSpec(memory_space=pl.ANY),
                  pl.BlockSpec(memory_space=pl.ANY)],
        out_specs=pl.BlockSpec((MOE_ROWS, D_MODEL), xmap),
        scratch_shapes=[pltpu.VMEM((2, D_MODEL, D_FF), f32), pltpu.VMEM((2, D_MODEL, D_FF), f32),
                        pltpu.VMEM((2, D_FF, D_MODEL), f32),
                        pltpu.VMEM((D_MODEL, 2 * D_FF), bf16), pltpu.VMEM((D_FF, D_MODEL), bf16),
                        pltpu.SemaphoreType.DMA((2, 3))],
    )
    return pl.pallas_call(
        _moe_kernel,
        out_shape=jax.ShapeDtypeStruct((rows, D_MODEL), f32),
        grid_spec=grid_spec,
        compiler_params=_cparams(("arbitrary",)),
        name="moe_experts",
    )(*items, xs, w_gate, w_up, w_down)


def _combine_kernel(mi_ref, mi_next_ref, x1p_ref, mwp_ref, x1s_ref, mws_ref, fnw_ref, ys_ref,
                    yp_ref, ysm_ref, g_ref, sem, *, np_tiles):
    i = pl.program_id(0)
    n = pl.num_programs(0)
    tm = x1p_ref.shape[0]
    slot = lax.rem(i, 2)

    def gather_rows(m_ref, dst_slot):
        def body(t, c):
            for u in range(SUBLANE):
                rec = MI_W * (SUBLANE * t + u)
                for k in range(2):
                    src = ys_ref.at[m_ref[rec + 2 * k], pl.ds(m_ref[rec + 2 * k + 1], 1)]
                    pltpu.make_async_copy(src, g_ref.at[dst_slot, k, t, pl.ds(u, 1)],
                                          sem.at[dst_slot]).start(priority=k)
            return c
        lax.fori_loop(0, tm // SUBLANE, body, 0)

    @pl.when(i == 0)
    def _():
        gather_rows(mi_ref, 0)

    @pl.when(i + 1 < n)
    def _():
        gather_rows(mi_next_ref, 1 - slot)

    for k in range(2):
        pltpu.make_async_copy(ys_ref.at[pl.ds(0, tm // SUBLANE)], g_ref.at[slot, k], sem.at[slot]).wait()

    def finish(x1_ref, mw_ref, out_ref):
        mw = mw_ref[...]
        g0 = g_ref[slot, 0].reshape(tm, D_MODEL)
        g1 = g_ref[slot, 1].reshape(tm, D_MODEL)
        x2 = x1_ref[...] + (g0 * mw[:, 0:1] + g1 * mw[:, 1:2])
        ms = jnp.mean(x2 * x2, axis=-1, keepdims=True)
        out_ref[...] = x2 * lax.rsqrt(ms + EPS) * fnw_ref[...]

    @pl.when(i < np_tiles)
    def _():
        finish(x1p_ref, mwp_ref, yp_ref)

    @pl.when(i >= np_tiles)
    def _():
        finish(x1s_ref, mws_ref, ysm_ref)


def _combine(x1_p, mw_p, x1_s, mw_s, fnw_row, ys3, mi_flat):
    tm = MOE_ROWS
    n_p, n_s = x1_p.shape[0], x1_s.shape[0]
    assert n_p % tm == 0 and n_s == tm
    np_tiles = n_p // tm
    ptile = lambda width: pl.BlockSpec((tm, width), lambda i: (jnp.minimum(i, np_tiles - 1), 0))
    stile = lambda width: pl.BlockSpec((tm, width), lambda i: (0, 0))
    return pl.pallas_call(
        functools.partial(_combine_kernel, np_tiles=np_tiles),
        out_shape=(jax.ShapeDtypeStruct((n_p, D_MODEL), f32),
                   jax.ShapeDtypeStruct((n_s, D_MODEL), f32)),
        grid=(np_tiles + 1,),
        in_specs=[pl.BlockSpec((MI_W * tm,), lambda i: (i,), memory_space=pltpu.SMEM),
                  pl.BlockSpec((MI_W * tm,), lambda i: (jnp.minimum(i + 1, np_tiles),), memory_space=pltpu.SMEM),
                  ptile(D_MODEL), ptile(LANE), stile(D_MODEL), stile(LANE),
                  pl.BlockSpec((1, D_MODEL), lambda i: (0, 0)),
                  pl.BlockSpec(memory_space=pl.ANY)],
        out_specs=(ptile(D_MODEL), stile(D_MODEL)),
        scratch_shapes=[pltpu.VMEM((2, 2, tm // SUBLANE, SUBLANE, D_MODEL), f32), pltpu.SemaphoreType.DMA((2,))],
        compiler_params=_cparams(("arbitrary",)),
        name="moe_combine",
    )(mi_flat, mi_flat, x1_p, mw_p, x1_s, mw_s, fnw_row, ys3)


def _work_items(counts, n_rows):
    nblk = n_rows // MOE_ROWS
    n_items = nblk + N_EXPERTS - 1
    ends = jnp.cumsum(counts)
    starts = ends - counts
    first_blk = starts // MOE_ROWS
    last_blk = jnp.maximum(ends - 1, 0) // MOE_ROWS
    nvis = jnp.where(counts > 0, last_blk - first_blk + 1, 0)
    vis_end = jnp.cumsum(nvis)
    vis_start = vis_end - nvis
    total = vis_end[-1]
    idx = jnp.arange(n_items, dtype=i32)
    e = jnp.minimum(jnp.sum((vis_end[None, :] <= idx[:, None]).astype(i32), axis=1), N_EXPERTS - 1)
    onehot = (e[:, None] == jnp.arange(N_EXPERTS, dtype=i32)[None, :]).astype(i32)
    look = lambda tbl: jnp.sum(onehot * tbl[None, :], axis=1)
    blk = look(first_blk) + idx - look(vis_start)
    lo = jnp.maximum(look(starts), blk * MOE_ROWS) - blk * MOE_ROWS
    hi = jnp.minimum(look(ends), (blk + 1) * MOE_ROWS) - blk * MOE_ROWS
    valid = idx < total
    blk = jnp.where(valid, blk, nblk - 1).astype(i32)
    lo = jnp.where(valid, lo, 0).astype(i32)
    hi = jnp.where(valid, hi, 0).astype(i32)
    prev_blk = jnp.concatenate([jnp.full((1,), -1, i32), blk[:-1]])
    first = (valid & (blk != prev_blk)).astype(i32)
    prev_e = jnp.concatenate([jnp.full((1,), -1, i32), e[:-1]])
    newe = (valid & (e != prev_e)).astype(i32)
    order = jnp.cumsum(newe) - 1
    slot = jnp.where(newe == 1, order % 2, 0).astype(i32)
    cum_act = jnp.cumsum((counts > 0).astype(i32))
    n_uniq = cum_act[-1]
    kk = jnp.arange(N_EXPERTS + 2, dtype=i32)
    uniq_e = jnp.sum((cum_act[None, :] <= kk[:, None]).astype(i32), axis=1)
    uniq_e = jnp.where(kk < n_uniq, uniq_e, -1)
    ahead = jnp.sum((kk[None, :] == (order + 2)[:, None]).astype(i32) * uniq_e[None, :], axis=1)
    pre = jnp.where(newe == 1, ahead, -1).astype(i32)
    init = uniq_e[0:2].astype(i32)
    return starts.astype(i32), (blk, lo, hi, first, newe, slot, pre, init)


def kernel(x_prompt, x_sample, state_delta, state_qkv_conv, state_short_conv, norm1_w, w_in, conv_a_w, a_log, dt_bias, out_norm_w, w_branch_a, conv_b_w, w_branch_b, w_o, norm2_w, router_group_w, router_group_b, router_expert_w, router_expert_b, w_gate, w_up, w_down, final_norm_w):
    assert norm1_w.shape[0] == 1, "single-layer trunk"
    bp, tp, d = x_prompt.shape
    bs, ts, _ = x_sample.shape
    assert d == D_MODEL and ts == 1
    n_p = bp * tp
    n_s = bs
    n_all = n_p + n_s

    w_perm = _wprep(jnp.transpose(w_in[0]))
    wa = w_branch_a[0].astype(bf16)
    wb = w_branch_b[0].astype(bf16)
    wo = w_o[0].astype(bf16)
    pad = lambda v: jnp.zeros((1, BA_W), f32).at[0, N_HEADS:2 * N_HEADS].set(v)
    alog_row = pad(a_log[0])
    dtb_row = pad(dt_bias[0])
    onw_row = out_norm_w[0].reshape(1, HEAD)
    cwa = conv_a_w[0]
    cwb = conv_b_w[0]
    rw = jnp.zeros((D_MODEL, LANE), f32)
    rw = rw.at[:, 0:N_EXPERTS].set(router_expert_w[0]).at[:, N_EXPERTS:N_EXPERTS + N_GROUPS].set(router_group_w[0])
    rwh = rw.astype(bf16)
    rwl = (rw - rwh.astype(f32)).astype(bf16)
    rb_row = jnp.zeros((1, LANE), f32)
    rb_row = rb_row.at[0, 0:N_EXPERTS].set(router_expert_b[0]).at[0, N_EXPERTS:N_EXPERTS + N_GROUPS].set(router_group_b[0])
    n2_row = norm2_w[0].reshape(1, D_MODEL)

    xp2 = x_prompt.reshape(n_p, D_MODEL)
    proj_p, tails = _inproj_conv(xp2, norm1_w[0], w_perm, cwa, tp)
    tiles_per_seq = tails.shape[0] // bp
    nca_p = tails.reshape(bp, tiles_per_seq, 8, CONV_PAD_W)[:, -1, 8 - (CONV_A - 1):8, 0:QKV_W]
    o_p, y_p, sd_p, ncb_p = _delta_prompt(proj_p.reshape(bp, tp, PROJ_W), cwb, alog_row, dtb_row,
                                          onw_row, nb_step=4 if bp % 4 == 0 else (2 if bp % 2 == 0 else 1))
    cnt0 = jnp.zeros((1, LANE), f32)
    x1_p, h2_p, mi_p, mw_p, cnt_p = _mix_route(xp2, o_p.reshape(n_p, QK_W), y_p.reshape(n_p, SC_W), proj_p,
                                               wa, wb, wo, n2_row, rwh, rwl, rb_row, cnt0)

    xs2 = x_sample.reshape(n_s, D_MODEL)
    proj_s = _inproj(xs2, norm1_w[0], w_perm)
    bufa_t = jnp.transpose(state_qkv_conv[0], (1, 0, 2))
    bufb_t = jnp.transpose(state_short_conv[0], (1, 0, 2))
    q_s, k_s, v_s, beta_s, eg_s, y_s, nbufa_t, nbufb_t = _sample_prep(proj_s, bufa_t, bufb_t, cwa, cwb,
                                                                      alog_row, dtb_row)
    h3 = lambda a: a.reshape(n_s, N_HEADS, HEAD)
    z_s = proj_s[:, COL_Z:COL_Z + QK_W]
    sd_s, o_s = _sample_step(state_delta[0], h3(q_s), h3(k_s), h3(v_s), h3(beta_s), h3(eg_s), h3(z_s), onw_row)
    o_s2 = o_s.reshape(n_s, QK_W).astype(bf16)
    x1_s, h2_s, mi_s, mw_s, cnt = _mix_route(xs2, o_s2, y_s, proj_s, wa, wb, wo, n2_row, rwh, rwl, rb_row, cnt_p)

    counts = cnt[0, 0:N_EXPERTS].astype(i32)
    starts, items = _work_items(counts, 2 * n_all)
    starts_row = jnp.zeros((1, LANE), f32).at[0, 0:N_EXPERTS].set(starts.astype(f32))
    mi_flat = jnp.concatenate([_dest_rows(mi_p, starts_row)[:, 0:MI_W], _dest_rows(mi_s, starts_row)[:, 0:MI_W]],
                              axis=0).reshape(MI_W * n_all)
    xs_sorted = _dispatch(h2_p, h2_s, mi_flat)
    ys = _moe(xs_sorted.reshape(2 * n_all, D_MODEL), w_gate[0], w_up[0], w_down[0], items)
    y_prompt, y_sample = _combine(x1_p, mw_p, x1_s, mw_s, final_norm_w.reshape(1, D_MODEL),
                                  ys.reshape(2 * n_all // SUBLANE, SUBLANE, D_MODEL), mi_flat)

    return (y_prompt.reshape(bp, tp, D_MODEL),
            y_sample.reshape(bs, ts, D_MODEL),
            sd_p[None],
            nca_p[None],
            ncb_p[None],
            sd_s[None],
            jnp.transpose(nbufa_t, (1, 0, 2))[None],
            jnp.transpose(nbufb_t, (1, 0, 2))[None])
```

```python
import functools

import jax
import jax.numpy as jnp
from jax import lax
from jax.experimental import pallas as pl
from jax.experimental.pallas import tpu as pltpu

f32 = jnp.float32
bf16 = jnp.bfloat16
i32 = jnp.int32

EPS = 1e-6
LANE = 128
D_MODEL = 2048
N_HEADS = 8
HEAD = 128
QK_W = N_HEADS * HEAD
QKV_W = 3 * QK_W
SC_W = 1024
CONV_A = 4
CONV_B = 3
CHUNK = 64
GROUP_HEADS = 4
N_EXPERTS = 64
N_GROUPS = 8
EXPERTS_PER_GROUP = 8
D_FF = 512
MOE_ROWS = 128

COL_QKV = 0
COL_BCX = 3072
COL_GA = 6144
COL_GB = 8192
COL_Z = 10240
COL_BA = 11264
BA_W = 256
PROJ_W = 11520
PROJ_TN = 1280

VMEM_LIMIT = 56 * 1024 * 1024


def _dot(a, b):
    return jnp.dot(a, b, preferred_element_type=f32)


def _dot_nt(a, b):
    return lax.dot_general(a, b, (((1,), (1,)), ((), ())), preferred_element_type=f32)


def _split(x, n):
    parts = []
    r = x
    for i in range(n):
        p = r.astype(bf16)
        parts.append(p)
        if i + 1 < n:
            r = r - p.astype(f32)
    return parts


def _dot_lsplit(x, m, n=3):
    rows = x.shape[0]
    d = _dot(jnp.concatenate(_split(x, n), axis=0), m)
    acc = d[0:rows]
    for i in range(1, n):
        acc = acc + d[i * rows:(i + 1) * rows]
    return acc


def _dot_rsplit(m, x, n=3):
    cols = x.shape[1]
    d = _dot(m, jnp.concatenate(_split(x, n), axis=1))
    acc = d[:, 0:cols]
    for i in range(1, n):
        acc = acc + d[:, i * cols:(i + 1) * cols]
    return acc


def _silu(x):
    return x * jax.nn.sigmoid(x)


def _softplus(x):
    return jnp.maximum(x, 0.0) + jnp.log(1.0 + jnp.exp(-jnp.abs(x)))


def _cparams(sem):
    return pltpu.CompilerParams(dimension_semantics=sem, vmem_limit_bytes=VMEM_LIMIT)


def _inproj_kernel(x_ref, nw_ref, w_ref, o_ref, h_ref, *, rows):
    @pl.when(pl.program_id(1) == 0)
    def _():
        def body(r, c):
            sl = pl.ds(pl.multiple_of(r * rows, rows), rows)
            x = x_ref[sl, :]
            ms = jnp.mean(x * x, axis=-1, keepdims=True)
            h_ref[sl, :] = (x * lax.rsqrt(ms + EPS) * nw_ref[...]).astype(bf16)
            return c
        lax.fori_loop(0, x_ref.shape[0] // rows, body, 0)

    o_ref[...] = _dot_nt(h_ref[...], w_ref[...])


def _inproj(x2d, norm_w, w_bf16):
    n = x2d.shape[0]
    tm = min(1024, n)
    assert n % tm == 0 and PROJ_W % PROJ_TN == 0
    return pl.pallas_call(
        functools.partial(_inproj_kernel, rows=min(128, tm)),
        out_shape=jax.ShapeDtypeStruct((n, PROJ_W), f32),
        grid=(n // tm, PROJ_W // PROJ_TN),
        in_specs=[pl.BlockSpec((tm, D_MODEL), lambda i, j: (i, 0)),
                  pl.BlockSpec((1, D_MODEL), lambda i, j: (0, 0)),
                  pl.BlockSpec((PROJ_TN, D_MODEL), lambda i, j: (j, 0))],
        out_specs=pl.BlockSpec((tm, PROJ_TN), lambda i, j: (i, j)),
        scratch_shapes=[pltpu.VMEM((tm, D_MODEL), bf16)],
        compiler_params=_cparams(("arbitrary", "arbitrary")),
        name="inproj",
    )(x2d, norm_w.reshape(1, D_MODEL), w_bf16)


CONV_TILES = 3
CONV_COLS = 2 * HEAD
CONV_ROWS = 128
CONV_PAD_W = CONV_TILES * PROJ_TN


def _qkv_kind(col):
    return "q" if col < QK_W else "k" if col < 2 * QK_W else "v" if col < QKV_W else "raw"


def _inproj_conv_kernel(x_ref, nw_ref, w_ref, cw_ref, o_ref, tail_ref, h_ref, hist_ref, raw_ref, *,
                        rows, tiles_per_seq):
    i = pl.program_id(0)
    j = pl.program_id(1)
    tm = x_ref.shape[0]

    @pl.when(j == 0)
    def _():
        def body(r, c):
            sl = pl.ds(pl.multiple_of(r * rows, rows), rows)
            x = x_ref[sl, :]
            ms = jnp.mean(x * x, axis=-1, keepdims=True)
            h_ref[sl, :] = (x * lax.rsqrt(ms + EPS) * nw_ref[...]).astype(bf16)
            return c
        lax.fori_loop(0, tm // rows, body, 0)

    @pl.when((i == 0) & (j == 0))
    def _():
        hist_ref[...] = jnp.zeros(hist_ref.shape, f32)

    @pl.when(j >= CONV_TILES)
    def _():
        o_ref[...] = _dot_nt(h_ref[...], w_ref[...])

    seq_start = lax.rem(i, tiles_per_seq) == 0
    for jj in range(CONV_TILES):
        @pl.when(j == jj)
        def _():
            def matmul_chunk(idx, c0):
                raw_ref[idx % 2] = _dot_nt(h_ref[...], w_ref[c0:c0 + CONV_COLS, :])

            def conv_chunk(idx, c0):
                cs = slice(c0, c0 + CONV_COLS)
                raw = raw_ref.at[idx % 2]
                tail = raw[tm - 8:tm, :]
                tail_ref[0, :, cs] = tail
                kinds = [_qkv_kind(jj * PROJ_TN + c0 + g * HEAD) for g in range(CONV_COLS // HEAD)]
                if kinds[0] == "raw":
                    o_ref[:, cs] = raw[...]
                    return
                hist = jnp.where(seq_start, 0.0, hist_ref[jj, :, cs])
                for rc in range(tm // CONV_ROWS):
                    r0 = rc * CONV_ROWS
                    if rc > 0:
                        xe = raw[r0 - 8:r0 + CONV_ROWS, :]
                    else:
                        xe = jnp.concatenate([hist, raw[0:CONV_ROWS, :]], axis=0)
                    acc = pltpu.roll(xe, 3, axis=0)[8:] * cw_ref[0:1, cs]
                    acc = acc + pltpu.roll(xe, 2, axis=0)[8:] * cw_ref[1:2, cs]
                    acc = acc + pltpu.roll(xe, 1, axis=0)[8:] * cw_ref[2:3, cs]
                    acc = acc + xe[8:] * cw_ref[3:4, cs]
                    act = _silu(acc)
                    for g, kind in enumerate(kinds):
                        ah = act[:, g * HEAD:(g + 1) * HEAD]
                        if kind != "v":
                            ss = jnp.sum(ah * ah, axis=-1, keepdims=True)
                            inv = lax.rsqrt(ss + EPS)
                            ah = ah * (inv * (HEAD ** -0.5) if kind == "q" else inv)
                        o_ref[r0:r0 + CONV_ROWS, c0 + g * HEAD:c0 + (g + 1) * HEAD] = ah
                hist_ref[jj, :, cs] = tail

            chunks = list(range(0, PROJ_TN, CONV_COLS))
            matmul_chunk(0, chunks[0])
            for idx in range(1, len(chunks)):
                matmul_chunk(idx, chunks[idx])
                conv_chunk(idx - 1, chunks[idx - 1])
            conv_chunk(len(chunks) - 1, chunks[-1])


def _inproj_conv(x2d, norm_w, w_bf16, cwa, seq_len):
    n = x2d.shape[0]
    tm = min(1024, seq_len)
    assert n % tm == 0 and seq_len % tm == 0 and PROJ_W % PROJ_TN == 0 and tm % CONV_ROWS == 0
    assert QKV_W % CONV_COLS == 0 and PROJ_TN % CONV_COLS == 0
    cw_pad = jnp.zeros((CONV_A, CONV_PAD_W), f32).at[:, 0:QKV_W].set(cwa)
    last = CONV_TILES - 1
    return pl.pallas_call(
        functools.partial(_inproj_conv_kernel, rows=min(128, tm), tiles_per_seq=seq_len // tm),
        out_shape=(jax.ShapeDtypeStruct((n, PROJ_W), f32),
                   jax.ShapeDtypeStruct((n // tm, 8, CONV_PAD_W), f32)),
        grid=(n // tm, PROJ_W // PROJ_TN),
        in_specs=[pl.BlockSpec((tm, D_MODEL), lambda i, j: (i, 0)),
                  pl.BlockSpec((1, D_MODEL), lambda i, j: (0, 0)),
                  pl.BlockSpec((PROJ_TN, D_MODEL), lambda i, j: (j, 0)),
                  pl.BlockSpec((CONV_A, PROJ_TN), lambda i, j: (0, jnp.minimum(j, last)))],
        out_specs=(pl.BlockSpec((tm, PROJ_TN), lambda i, j: (i, j)),
                   pl.BlockSpec((1, 8, PROJ_TN), lambda i, j: (i, 0, jnp.minimum(j, last)))),
        scratch_shapes=[pltpu.VMEM((tm, D_MODEL), bf16), pltpu.VMEM((CONV_TILES, 8, PROJ_TN), f32),
                        pltpu.VMEM((2, tm, CONV_COLS), f32)],
        compiler_params=_cparams(("arbitrary", "arbitrary")),
        name="inproj_conv",
    )(x2d, norm_w.reshape(1, D_MODEL), w_bf16, cw_pad)


W_IN_COLS = 11280
WPREP_TN = 1024
WPREP_SHIFT = 16


def _wprep_kernel(a_ref, b_ref, o_ref):
    j = pl.program_id(0)
    keep = WPREP_TN - WPREP_SHIFT

    @pl.when((j < 3) | (j == 10))
    def _():
        o_ref[...] = a_ref[...].astype(bf16)

    @pl.when((j >= 3) & (j < 10))
    def _():
        o_ref[0:keep, :] = a_ref[WPREP_SHIFT:WPREP_TN, :].astype(bf16)
        o_ref[keep:WPREP_TN, :] = b_ref[...].astype(bf16)

    @pl.when(j == 11)
    def _():
        o_ref[0:WPREP_SHIFT, :] = a_ref[0:WPREP_SHIFT, :].astype(bf16)
        o_ref[WPREP_SHIFT:WPREP_TN, :] = jnp.zeros((keep, D_MODEL), bf16)


def _wprep(w_in_t):
    assert w_in_t.shape == (W_IN_COLS, D_MODEL) and 2 * N_HEADS == WPREP_SHIFT
    n_blk = pl.cdiv(PROJ_W, WPREP_TN)

    def a_map(j):
        return (jnp.where(j < 3, j, jnp.where(j < 10, j + 1, jnp.where(j == 10, 3, 4))), 0)

    def b_map(j):
        return (jnp.minimum((WPREP_TN // WPREP_SHIFT) * (j + 2), W_IN_COLS // WPREP_SHIFT - 1), 0)

    return pl.pallas_call(
        _wprep_kernel,
        out_shape=jax.ShapeDtypeStruct((PROJ_W, D_MODEL), bf16),
        grid=(n_blk,),
        in_specs=[pl.BlockSpec((WPREP_TN, D_MODEL), a_map),
                  pl.BlockSpec((WPREP_SHIFT, D_MODEL), b_map)],
        out_specs=pl.BlockSpec((WPREP_TN, D_MODEL), lambda j: (j, 0)),
        compiler_params=_cparams(("arbitrary",)),
        name="wprep",
    )(w_in_t, w_in_t)


def _head_l2norm(a, scale):
    outs = []
    for h in range(N_HEADS):
        ah = a[:, h * HEAD:(h + 1) * HEAD]
        ss = jnp.sum(ah * ah, axis=-1, keepdims=True)
        n = ah * lax.rsqrt(ss + EPS)
        outs.append(n * scale if scale != 1.0 else n)
    return outs


def _delta_prompt_kernel(qkv_ref, bcx_ref, z_ref, ba_ref, cwb_ref, alog_ref, dtb_ref, onw_ref,
                         e64_ref,
                         o_ref, y_ref, snew_ref, ncb_ref,
                         s_ref, xb_ref, *, nb_step):
    C = CHUNK
    G = GROUP_HEADS
    R = G * C
    t = pl.program_id(1)
    nt = pl.num_programs(1)

    @pl.when(t == 0)
    def _():
        s_ref[...] = jnp.zeros(s_ref.shape, f32)
        xb_ref[:, 0:8, :] = jnp.zeros((nb_step, 8, SC_W), f32)

    rr = lax.broadcasted_iota(i32, (R, R), 0)
    cc = lax.broadcasted_iota(i32, (R, R), 1)
    same = (rr >> 6) == (cc >> 6)
    incl = same & (rr >= cc)
    strict = same & (rr > cc)
    eye = jnp.where(rr == cc, 1.0, 0.0).astype(f32)
    r2 = lax.broadcasted_iota(i32, (R, G * HEAD), 0)
    c2 = lax.broadcasted_iota(i32, (R, G * HEAD), 1)
    bdmask = (r2 >> 6) == (c2 >> 7)
    r3 = lax.broadcasted_iota(i32, (C, C), 0)
    c3 = lax.broadcasted_iota(i32, (C, C), 1)
    ltri = jnp.where(r3 >= c3, 1.0, 0.0).astype(bf16)
    r4 = lax.broadcasted_iota(i32, (C, R), 0)
    c4 = lax.broadcasted_iota(i32, (C, R), 1)
    ident_t = r4 == (c4 & (C - 1))
    ones8 = jnp.ones((8, C), bf16)

    nbs = range(nb_step)
    units = [(nb, g) for nb in nbs for g in range(N_HEADS // G)]
    heads = lambda g: range(g * G, (g + 1) * G)

    qn = [[qkv_ref[nb, :, h * HEAD:(h + 1) * HEAD] for h in range(N_HEADS)] for nb in nbs]
    kn = [[qkv_ref[nb, :, QK_W + h * HEAD:QK_W + (h + 1) * HEAD] for h in range(N_HEADS)] for nb in nbs]
    vv = [qkv_ref[nb, :, 2 * QK_W:3 * QK_W] for nb in nbs]

    bts = [ba_ref[nb] for nb in nbs]
    beta_all = [jax.nn.sigmoid(bt) for bt in bts]
    g_all = [-(jnp.exp(alog_ref[...]) * _softplus(bt + dtb_ref[...])) for bt in bts]
    gc_small = [_dot_rsplit(ltri, ga) for ga in g_all]
    gl_small = [gc[C - 1:C, :] for gc in gc_small]

    k_st, q_st, kb, vb, kbg, qd, kd, gc_col = ({} for _ in range(8))
    for u in units:
        nb, g = u
        hs = heads(g)
        k_st[u] = jnp.concatenate([kn[nb][h] for h in hs], axis=0)
        q_st[u] = jnp.concatenate([qn[nb][h] for h in hs], axis=0)
        v_st = jnp.concatenate([vv[nb][:, h * HEAD:(h + 1) * HEAD] for h in hs], axis=0)
        beta_col = jnp.concatenate([beta_all[nb][:, h:h + 1] for h in hs], axis=0)
        gc_col[u] = jnp.concatenate([gc_small[nb][:, 8 + h:9 + h] for h in hs], axis=0)
        gl_col = jnp.concatenate(
            [jnp.broadcast_to(gl_small[nb][:, 8 + h:9 + h], (C, 1)) for h in hs], axis=0)
        kb[u] = k_st[u] * beta_col
        vb[u] = v_st * beta_col
        egc = jnp.exp(gc_col[u])
        kbg[u] = kb[u] * egc
        qd[u] = q_st[u] * egc
        kd[u] = k_st[u] * jnp.exp(gl_col - gc_col[u])

    gx = {u: _dot_lsplit(gc_small[u[0]], e64_ref[u[1]]) for u in units}
    crow = {u: _dot_rsplit(ones8, jnp.where(ident_t, gx[u], 0.0))[0:1, :] for u in units}
    a = {u: _dot_nt(jnp.concatenate([kb[u], q_st[u]], axis=0).astype(bf16), k_st[u].astype(bf16))
         for u in units}
    dec = {u: jnp.where(incl, jnp.exp(jnp.where(incl, gc_col[u] - crow[u], 0.0)), 0.0) for u in units}
    nm = {u: jnp.where(strict, -(a[u][0:R] * dec[u]), 0.0) for u in units}
    qkm = {u: a[u][R:2 * R] * dec[u] for u in units}

    p = {u: eye + nm[u] for u in units}
    nk = {}
    for u in units:
        nmb = nm[u].astype(bf16)
        nk[u] = _dot(nmb, nmb)
    for _ in range(4):
        for u in units:
            x = _dot(jnp.concatenate([p[u], nk[u]], axis=0).astype(bf16), nk[u].astype(bf16))
            p[u] = p[u] + x[0:R]
            nk[u] = x[R:2 * R]
    for u in units:
        p[u] = p[u] + _dot(p[u].astype(bf16), nk[u].astype(bf16))
    uw = {u: _dot(p[u].astype(bf16), jnp.concatenate([vb[u], kbg[u]], axis=1).astype(bf16)) for u in units}

    ws = {}
    for u in units:
        nb, g = u
        for j, h in enumerate(heads(g)):
            sh = s_ref[nb, :, h * HEAD:(h + 1) * HEAD]
            lhs = jnp.concatenate([uw[u][j * C:(j + 1) * C, HEAD:2 * HEAD], qd[u][j * C:(j + 1) * C]], axis=0)
            ws[u, j] = _dot(lhs.astype(bf16), sh.astype(bf16))
    o_heads = {}
    for u in units:
        nb, g = u
        vnew_st = jnp.concatenate([uw[u][j * C:(j + 1) * C, 0:HEAD] - ws[u, j][0:C] for j in range(G)], axis=0)
        o_st = (jnp.concatenate([ws[u, j][C:2 * C] for j in range(G)], axis=0)
                + _dot(qkm[u].astype(bf16), vnew_st.astype(bf16)))
        vbd = jnp.where(bdmask, jnp.concatenate([vnew_st] * G, axis=1), 0.0)
        lo = g * G * HEAD
        hi = lo + G * HEAD
        gl_row = jnp.concatenate(
            [jnp.broadcast_to(jnp.exp(gl_small[nb][:, 8 + h:9 + h]), (1, HEAD)) for h in heads(g)], axis=1)
        s_ref[nb, :, lo:hi] = s_ref[nb, :, lo:hi] * gl_row + _dot(kd[u].T.astype(bf16), vbd.astype(bf16))
        for j, h in enumerate(heads(g)):
            o_heads[nb, h] = o_st[j * C:(j + 1) * C]

    for nb in nbs:
        zt = z_ref[nb]
        for h in range(N_HEADS):
            oh = o_heads[nb, h]
            ms = jnp.mean(oh * oh, axis=-1, keepdims=True)
            zh = zt[:, h * HEAD:(h + 1) * HEAD]
            on = oh * lax.rsqrt(ms + EPS) * onw_ref[...] * _silu(zh)
            o_ref[nb, :, h * HEAD:(h + 1) * HEAD] = on.astype(bf16)

    for nb in nbs:
        bcx = bcx_ref[nb]
        cx = bcx[:, SC_W:2 * SC_W] * bcx[:, 2 * SC_W:3 * SC_W]
        xb_ref[nb, 8:8 + C, :] = cx
        ce = xb_ref[nb]
        cv = pltpu.roll(ce, 2, axis=0)[8:8 + C] * cwb_ref[0:1, :]
        cv = cv + pltpu.roll(ce, 1, axis=0)[8:8 + C] * cwb_ref[1:2, :]
        cv = cv + cx * cwb_ref[2:3, :]
        y_ref[nb] = (bcx[:, 0:SC_W] * cv).astype(bf16)
        xb_ref[nb, 0:8, :] = xb_ref[nb, C:C + 8, :]

    @pl.when(t == nt - 1)
    def _():
        for nb in range(nb_step):
            for h in range(N_HEADS):
                snew_ref[nb, h] = s_ref[nb, :, h * HEAD:(h + 1) * HEAD]
            ncb_ref[nb] = xb_ref[nb, 6:8, :]


def _expand_consts():
    lane = jnp.arange(BA_W)[:, None]
    col = jnp.arange(QK_W)[None, :]
    eb = (lane == (col >> 7)).astype(bf16)
    eg = (lane == (8 + (col >> 7))).astype(bf16)
    col64 = jnp.arange(GROUP_HEADS * CHUNK)[None, :]
    e64 = jnp.stack([(lane == (8 + g * GROUP_HEADS + (col64 >> 6))).astype(bf16)
                     for g in range(N_HEADS // GROUP_HEADS)], axis=0)
    return eb, eg, e64


def _delta_prompt(proj3, cwb, alog_row, dtb_row, onw_row, nb_step):
    b, t, _ = proj3.shape
    assert t % CHUNK == 0 and b % nb_step == 0
    _, _, e64 = _expand_consts()
    c = CHUNK
    const2 = lambda bi, ti: (0, 0)
    outs = pl.pallas_call(
        functools.partial(_delta_prompt_kernel, nb_step=nb_step),
        out_shape=(jax.ShapeDtypeStruct((b, t, QK_W), bf16),
                   jax.ShapeDtypeStruct((b, t, SC_W), bf16),
                   jax.ShapeDtypeStruct((b, N_HEADS, HEAD, HEAD), f32),
                   jax.ShapeDtypeStruct((b, CONV_B - 1, SC_W), f32)),
        grid=(b // nb_step, t // c),
        in_specs=[pl.BlockSpec((nb_step, c, QKV_W), lambda bi, ti: (bi, ti, COL_QKV // QKV_W)),
                  pl.BlockSpec((nb_step, c, QKV_W), lambda bi, ti: (bi, ti, COL_BCX // QKV_W)),
                  pl.BlockSpec((nb_step, c, QK_W), lambda bi, ti: (bi, ti, COL_Z // QK_W)),
                  pl.BlockSpec((nb_step, c, BA_W), lambda bi, ti: (bi, ti, COL_BA // BA_W)),
                  pl.BlockSpec((CONV_B, SC_W), const2),
                  pl.BlockSpec((1, BA_W), const2),
                  pl.BlockSpec((1, BA_W), const2),
                  pl.BlockSpec((1, HEAD), const2),
                  pl.BlockSpec((N_HEADS // GROUP_HEADS, BA_W, GROUP_HEADS * CHUNK), lambda bi, ti: (0, 0, 0))],
        out_specs=(pl.BlockSpec((nb_step, c, QK_W), lambda bi, ti: (bi, ti, 0)),
                   pl.BlockSpec((nb_step, c, SC_W), lambda bi, ti: (bi, ti, 0)),
                   pl.BlockSpec((nb_step, N_HEADS, HEAD, HEAD), lambda bi, ti: (bi, 0, 0, 0)),
                   pl.BlockSpec((nb_step, CONV_B - 1, SC_W), lambda bi, ti: (bi, 0, 0))),
        scratch_shapes=[pltpu.VMEM((nb_step, HEAD, QK_W), f32),
                        pltpu.VMEM((nb_step, 8 + c, SC_W), f32)],
        compiler_params=_cparams(("arbitrary", "arbitrary")),
        name="delta_prompt",
    )(proj3, proj3, proj3, proj3, cwb, alog_row, dtb_row, onw_row, e64)
    return outs


def _sample_prep_kernel(p_ref, bufa_ref, bufb_ref, cwa_ref, cwb_ref, alog_ref, dtb_ref, eb_ref, eg_ref,
                        q_ref, k_ref, v_ref, beta_ref, eg_out_ref, y_ref, nbufa_ref, nbufb_ref):
    def conv_sec(lo):
        hi = lo + QK_W
        raw = p_ref[:, COL_QKV + lo:COL_QKV + hi]
        acc = bufa_ref[0, :, lo:hi] * cwa_ref[0:1, lo:hi]
        acc = acc + bufa_ref[1, :, lo:hi] * cwa_ref[1:2, lo:hi]
        acc = acc + bufa_ref[2, :, lo:hi] * cwa_ref[2:3, lo:hi]
        acc = acc + raw * cwa_ref[3:4, lo:hi]
        nbufa_ref[0, :, lo:hi] = bufa_ref[1, :, lo:hi]
        nbufa_ref[1, :, lo:hi] = bufa_ref[2, :, lo:hi]
        nbufa_ref[2, :, lo:hi] = raw
        return _silu(acc)

    qn = _head_l2norm(conv_sec(0), HEAD ** -0.5)
    kn = _head_l2norm(conv_sec(QK_W), 1.0)
    for h in range(N_HEADS):
        q_ref[:, h * HEAD:(h + 1) * HEAD] = qn[h]
        k_ref[:, h * HEAD:(h + 1) * HEAD] = kn[h]
    v_ref[...] = conv_sec(2 * QK_W)

    bt = p_ref[:, COL_BA:COL_BA + BA_W]
    beta_all = jax.nn.sigmoid(bt)
    g_all = -(jnp.exp(alog_ref[...]) * _softplus(bt + dtb_ref[...]))
    beta_ref[...] = _dot_lsplit(beta_all, eb_ref[...])
    eg_out_ref[...] = jnp.exp(_dot_lsplit(g_all, eg_ref[...]))

    bg = p_ref[:, COL_BCX:COL_BCX + SC_W]
    cx = p_ref[:, COL_BCX + SC_W:COL_BCX + 2 * SC_W] * p_ref[:, COL_BCX + 2 * SC_W:COL_BCX + 3 * SC_W]
    cv = bufb_ref[0] * cwb_ref[0:1, :]
    cv = cv + bufb_ref[1] * cwb_ref[1:2, :]
    cv = cv + cx * cwb_ref[2:3, :]
    y_ref[...] = (bg * cv).astype(bf16)
    nbufb_ref[0] = bufb_ref[1]
    nbufb_ref[1] = cx


def _sample_prep(proj_s, bufa_t, bufb_t, cwa, cwb, alog_row, dtb_row):
    n = proj_s.shape[0]
    eb, eg, _ = _expand_consts()
    row = jax.ShapeDtypeStruct((n, QK_W), f32)
    return pl.pallas_call(
        _sample_prep_kernel,
        out_shape=(row, row, row, row, row,
                   jax.ShapeDtypeStruct((n, SC_W), bf16),
                   jax.ShapeDtypeStruct((CONV_A - 1, n, QKV_W), f32),
                   jax.ShapeDtypeStruct((CONV_B - 1, n, SC_W), f32)),
        compiler_params=pltpu.CompilerParams(vmem_limit_bytes=VMEM_LIMIT),
        name="sample_prep",
    )(proj_s, bufa_t, bufb_t, cwa, cwb, alog_row, dtb_row, eb, eg)


def _sample_step_kernel(s_ref, q_ref, k_ref, v_ref, beta_ref, eg_ref, z_ref, onw_ref,
                        snew_ref, o_ref, *, bb):
    w = N_HEADS * HEAD
    r8 = lax.broadcasted_iota(i32, (N_HEADS, w), 0)
    c8 = lax.broadcasted_iota(i32, (N_HEADS, w), 1)
    mask8 = r8 == (c8 >> 7)
    zpad_k = jnp.zeros((HEAD - N_HEADS, HEAD), f32)
    zpad_d = jnp.zeros((HEAD - N_HEADS, w), f32)
    for b in range(bb):
        s_all = jnp.concatenate([s_ref[b, h] for h in range(N_HEADS)], axis=1)
        eg8 = eg_ref[b]
        eg_row = jnp.concatenate([eg8[h:h + 1, :] for h in range(N_HEADS)], axis=1)
        s_dec = s_all * eg_row
        k8 = k_ref[b]
        x = _dot(k8.astype(bf16), s_dec.astype(bf16))
        v_t = jnp.concatenate([v_ref[b]] * N_HEADS, axis=1)
        b_t = jnp.concatenate([beta_ref[b]] * N_HEADS, axis=1)
        d_bd = jnp.where(mask8, (v_t - x) * b_t, 0.0)
        kt = jnp.concatenate([k8, zpad_k], axis=0).T
        k_hi, k_lo = _split(kt, 2)
        dh = d_bd.astype(bf16).astype(f32)
        d_hi = jnp.concatenate([dh, zpad_d], axis=0).astype(bf16)
        d_lo = jnp.concatenate([d_bd - dh, zpad_d], axis=0).astype(bf16)
        s_new = s_dec + (_dot(k_hi, d_hi) + _dot(k_hi, d_lo) + _dot(k_lo, d_hi))
        yv = jnp.where(mask8, _dot(q_ref[b].astype(bf16), s_new.astype(bf16)), 0.0)
        o8 = yv[:, 0:HEAD]
        for j in range(1, N_HEADS):
            o8 = o8 + yv[:, j * HEAD:(j + 1) * HEAD]
        ms = jnp.mean(o8 * o8, axis=-1, keepdims=True)
        o_ref[b] = o8 * lax.rsqrt(ms + EPS) * onw_ref[...] * _silu(z_ref[b])
        for h in range(N_HEADS):
            snew_ref[b, h] = s_new[:, h * HEAD:(h + 1) * HEAD]


def _sample_step(state, q, k, v, beta, eg, z, onw_row, bb=4):
    n = state.shape[0]
    assert n % bb == 0
    hspec = pl.BlockSpec((bb, N_HEADS, HEAD), lambda i: (i, 0, 0))
    sspec = pl.BlockSpec((bb, N_HEADS, HEAD, HEAD), lambda i: (i, 0, 0, 0))
    return pl.pallas_call(
        functools.partial(_sample_step_kernel, bb=bb),
        out_shape=(jax.ShapeDtypeStruct(state.shape, f32),
                   jax.ShapeDtypeStruct((n, N_HEADS, HEAD), f32)),
        grid=(n // bb,),
        in_specs=[sspec, hspec, hspec, hspec, hspec, hspec, hspec, pl.BlockSpec((1, HEAD), lambda i: (0, 0))],
        out_specs=(sspec, hspec),
        compiler_params=_cparams(("arbitrary",)),
        name="sample_step",
    )(state, q, k, v, beta, eg, z, onw_row)


def _mix_route_kernel(x_ref, o_ref, y_ref, ga_ref, gb_ref, wa_ref, wb_ref, wo_ref, n2_ref,
                      rwh_ref, rwl_ref, rb_ref, cnt_in_ref, x1_ref, h2_ref, mi_ref, mw_ref, cnt_ref):
    i = pl.program_id(0)
    tm = x_ref.shape[0]

    @pl.when(i == 0)
    def _():
        cnt_ref[...] = cnt_in_ref[...]

    oa = _dot(o_ref[...], wa_ref[...])
    ob = _dot(y_ref[...], wb_ref[...])
    merged = jax.nn.sigmoid(ga_ref[...]) * oa + jax.nn.sigmoid(gb_ref[...]) * ob
    x1 = x_ref[...] + _dot(merged.astype(bf16), wo_ref[...])
    x1_ref[...] = x1
    ms = jnp.mean(x1 * x1, axis=-1, keepdims=True)
    h2 = x1 * lax.rsqrt(ms + EPS) * n2_ref[...]
    h2_ref[...] = h2

    h_hi, h_lo = _split(h2, 2)
    logits = _dot(h_hi, rwh_ref[...]) + _dot(h_hi, rwl_ref[...]) + _dot(h_lo, rwh_ref[...]) + rb_ref[...]

    lane = lax.broadcasted_iota(i32, (tm, LANE), 1)
    lanef = lane.astype(f32)
    neg = jnp.float32(-jnp.inf)
    big = jnp.float32(1e9)
    gmask = (lane >= N_EXPERTS) & (lane < N_EXPERTS + N_GROUPS)
    gl = jnp.where(gmask, logits, neg)
    gmax = jnp.max(gl, axis=-1, keepdims=True)
    gidx = jnp.min(jnp.where(gl == gmax, lanef - N_EXPERTS, big), axis=-1, keepdims=True)
    gsum = jnp.sum(jnp.where(gmask, jnp.exp(gl - gmax), 0.0), axis=-1, keepdims=True)
    gprob = 1.0 / gsum

    emask = (lane < N_EXPERTS) & ((lane >> 3).astype(f32) == gidx)
    el = jnp.where(emask, logits, neg)
    emax = jnp.max(el, axis=-1, keepdims=True)
    pe = jnp.where(emask, jnp.exp(el - emax), 0.0)
    eprob = pe / jnp.sum(pe, axis=-1, keepdims=True)
    p1m = jnp.where(emask, eprob, -1.0)
    m1 = jnp.max(p1m, axis=-1, keepdims=True)
    i1 = jnp.min(jnp.where(p1m == m1, lanef, big), axis=-1, keepdims=True)
    p2m = jnp.where(lanef == i1, -1.0, p1m)
    m2 = jnp.max(p2m, axis=-1, keepdims=True)
    i2 = jnp.min(jnp.where(p2m == m2, lanef, big), axis=-1, keepdims=True)
    tot = m1 + m2
    c1 = m1 / tot * gprob
    c2 = m2 / tot * gprob

    oh1 = jnp.where(lanef == i1, 1.0, 0.0)
    oh2 = jnp.where(lanef == i2, 1.0, 0.0)
    ohs = oh1 + oh2
    rt = lax.broadcasted_iota(i32, (tm, tm), 0)
    ct = lax.broadcasted_iota(i32, (tm, tm), 1)
    lstrict = jnp.where(rt > ct, 1.0, 0.0).astype(bf16)
    cs = _dot(lstrict, ohs.astype(bf16)) + cnt_ref[...]
    rank1 = jnp.sum(cs * oh1, axis=-1, keepdims=True)
    rank2 = jnp.sum(cs * oh2, axis=-1, keepdims=True)
    cnt_ref[...] = cnt_ref[...] + jnp.sum(ohs, axis=0, keepdims=True)

    mi = jnp.where(lane == 0, i1, jnp.where(lane == 1, i2, jnp.where(lane == 2, rank1,
                                                                     jnp.where(lane == 3, rank2, 0.0))))
    mi_ref[...] = mi.astype(i32)
    mw_ref[...] = jnp.where(lane == 0, c1, jnp.where(lane == 1, c2, 0.0))


def _mix_route(x2d, o2d, y2d, proj2d, wa, wb, wo, n2_row, rwh, rwl, rb_row, cnt_in):
    n = x2d.shape[0]
    tm = min(256, n)
    assert n % tm == 0
    tok = lambda width: pl.BlockSpec((tm, width), lambda i: (i, 0))
    full = lambda a: pl.BlockSpec(a.shape, lambda i: (0,) * a.ndim)
    in_specs = [tok(D_MODEL), tok(QK_W), tok(SC_W),
                pl.BlockSpec((tm, D_MODEL), lambda i: (i, COL_GA // D_MODEL)),
                pl.BlockSpec((tm, D_MODEL), lambda i: (i, COL_GB // D_MODEL)),
                full(wa), full(wb), full(wo), full(n2_row), full(rwh), full(rwl), full(rb_row), full(cnt_in)]
    out_shape = (jax.ShapeDtypeStruct((n, D_MODEL), f32),
                 jax.ShapeDtypeStruct((n, D_MODEL), f32),
                 jax.ShapeDtypeStruct((n, LANE), i32),
                 jax.ShapeDtypeStruct((n, LANE), f32),
                 jax.ShapeDtypeStruct((1, LANE), f32))
    out_specs = (tok(D_MODEL), tok(D_MODEL), tok(LANE), tok(LANE),
                 pl.BlockSpec((1, LANE), lambda i: (0, 0)))
    return pl.pallas_call(
        _mix_route_kernel,
        out_shape=out_shape,
        grid=(n // tm,),
        in_specs=in_specs,
        out_specs=out_specs,
        compiler_params=_cparams(("arbitrary",)),
        name="mix_route",
    )(x2d, o2d, y2d, proj2d, proj2d, wa, wb, wo, n2_row, rwh, rwl, rb_row, cnt_in)


MI_W = 4
SUBLANE = 8
ROW_DMA_UNROLL = 8


def _dest_kernel(mi_ref, starts_ref, o_ref):
    mi = mi_ref[...]
    lane = lax.broadcasted_iota(i32, mi.shape, 1)
    st = starts_ref[...]

    def first_row(e_col):
        return jnp.sum(jnp.where(lane == e_col, st, 0.0), axis=-1, keepdims=True).astype(i32)

    d0 = first_row(mi[:, 0:1]) + mi[:, 2:3]
    d1 = first_row(mi[:, 1:2]) + mi[:, 3:4]
    o_ref[...] = jnp.where(lane == 0, d0 >> 3, jnp.where(lane == 1, d0 & (SUBLANE - 1),
                           jnp.where(lane == 2, d1 >> 3, jnp.where(lane == 3, d1 & (SUBLANE - 1), 0))))


def _dest_rows(mi, starts_row):
    n = mi.shape[0]
    tm = min(1024, n)
    assert n % tm == 0
    return pl.pallas_call(
        _dest_kernel,
        out_shape=jax.ShapeDtypeStruct((n, LANE), i32),
        grid=(n // tm,),
        in_specs=[pl.BlockSpec((tm, LANE), lambda i: (i, 0)), pl.BlockSpec((1, LANE), lambda i: (0, 0))],
        out_specs=pl.BlockSpec((tm, LANE), lambda i: (i, 0)),
        compiler_params=_cparams(("arbitrary",)),
        name="moe_dest",
    )(mi, starts_row)


def _dispatch_kernel(mi_ref, hp_ref, hs_ref, xs_ref, sem, *, np_tiles, tm):
    i = pl.program_id(0)

    def scatter_rows(h_ref):
        def start(t, c):
            for u in range(SUBLANE):
                rec = MI_W * (SUBLANE * t + u)
                for k in range(2):
                    dst = xs_ref.at[mi_ref[rec + 2 * k], pl.ds(mi_ref[rec + 2 * k + 1], 1)]
                    pltpu.make_async_copy(h_ref.at[t, pl.ds(u, 1)], dst, sem).start(priority=k)
            return c

        lax.fori_loop(0, tm // SUBLANE, start, 0)
        for k in range(2):
            pltpu.make_async_copy(h_ref, xs_ref.at[pl.ds(0, tm // SUBLANE)], sem).wait()

    @pl.when(i < np_tiles)
    def _():
        scatter_rows(hp_ref)

    @pl.when(i >= np_tiles)
    def _():
        scatter_rows(hs_ref)


def _dispatch(h2_p, h2_s, mi_flat):
    tm = MOE_ROWS
    n_p, n_s = h2_p.shape[0], h2_s.shape[0]
    assert n_p % tm == 0 and n_s == tm and tm % SUBLANE == 0
    np_tiles = n_p // tm
    tiled = lambda a: a.reshape(a.shape[0] // SUBLANE, SUBLANE, D_MODEL)
    blk = (tm // SUBLANE, SUBLANE, D_MODEL)
    return pl.pallas_call(
        functools.partial(_dispatch_kernel, np_tiles=np_tiles, tm=tm),
        out_shape=jax.ShapeDtypeStruct((2 * (n_p + n_s) // SUBLANE, SUBLANE, D_MODEL), f32),
        grid=(np_tiles + 1,),
        in_specs=[pl.BlockSpec((MI_W * tm,), lambda i: (i,), memory_space=pltpu.SMEM),
                  pl.BlockSpec(blk, lambda i: (jnp.minimum(i, np_tiles - 1), 0, 0)),
                  pl.BlockSpec(blk, lambda i: (0, 0, 0))],
        out_specs=pl.BlockSpec(memory_space=pl.ANY),
        scratch_shapes=[pltpu.SemaphoreType.DMA(())],
        compiler_params=_cparams(("arbitrary",)),
        name="moe_dispatch",
    )(mi_flat, tiled(h2_p), tiled(h2_s))


def _cast_rows(src_ref, dst_ref, col0=0, rows=256):
    width = src_ref.shape[1]

    def body(r, c):
        sl = pl.ds(pl.multiple_of(r * rows, rows), rows)
        dst_ref[sl, col0:col0 + width] = src_ref[sl, :].astype(bf16)
        return c
    lax.fori_loop(0, src_ref.shape[0] // rows, body, 0)


def _moe_kernel(blk_ref, lo_ref, hi_ref, first_ref, newe_ref, slot_ref, pre_ref, init_ref,
                x_ref, wg_hbm, wu_hbm, wd_hbm, o_ref,
                wg_f, wu_f, wd_f, wgu_b, wd_b, sem):
    i = pl.program_id(0)
    lo = lo_ref[i]
    hi = hi_ref[i]

    def weight_copies(e, slot):
        return [pltpu.make_async_copy(wg_hbm.at[e], wg_f.at[slot], sem.at[slot, 0]),
                pltpu.make_async_copy(wu_hbm.at[e], wu_f.at[slot], sem.at[slot, 1]),
                pltpu.make_async_copy(wd_hbm.at[e], wd_f.at[slot], sem.at[slot, 2])]

    def start_weights(e, slot):
        for cp, prio in zip(weight_copies(e, slot), (0, 1, 1)):
            cp.start(priority=prio)

    @pl.when(i == 0)
    def _():
        start_weights(init_ref[0], 0)

        @pl.when(init_ref[1] >= 0)
        def _():
            start_weights(init_ref[1], 1)

    @pl.when(newe_ref[i] == 1)
    def _():
        slot = slot_ref[i]
        cg, cu, cd = weight_copies(0, slot)
        cg.wait()
        _cast_rows(wg_f.at[slot], wgu_b, 0)
        cu.wait()
        _cast_rows(wu_f.at[slot], wgu_b, D_FF)
        cd.wait()
        _cast_rows(wd_f.at[slot], wd_b)

        @pl.when(pre_ref[i] >= 0)
        def _():
            start_weights(pre_ref[i], slot)

    @pl.when(hi > lo)
    def _():
        x = x_ref[...].astype(bf16)
        au = _dot(x, wgu_b[...])
        y = _dot((_silu(au[:, 0:D_FF]) * au[:, D_FF:2 * D_FF]).astype(bf16), wd_b[...])
        row = lax.broadcasted_iota(i32, y.shape, 0)
        ym = jnp.where((row >= lo) & (row < hi), y, 0.0)

        @pl.when(first_ref[i] == 1)
        def _():
            o_ref[...] = ym

        @pl.when(first_ref[i] == 0)
        def _():
            o_ref[...] = o_ref[...] + ym


def _moe(xs, w_gate, w_up, w_down, items):
    n_items = items[0].shape[0]
    rows = xs.shape[0]
    n_pref = len(items)
    xmap = lambda i, blk, *_: (blk[i], 0)
    grid_spec = pltpu.PrefetchScalarGridSpec(
        num_scalar_prefetch=n_pref,
        grid=(n_items,),
        in_specs=[pl.BlockSpec((MOE_ROWS, D_MODEL), xmap),
                  pl.BlockSpec(memory_space=pl.ANY),
                  pl.BlockSpec(memory_space=pl.ANY),
                  pl.BlockSpec(memory_space=pl.ANY)],
        out_specs=pl.BlockSpec((MOE_ROWS, D_MODEL), xmap),
        scratch_shapes=[pltpu.VMEM((2, D_MODEL, D_FF), f32), pltpu.VMEM((2, D_MODEL, D_FF), f32),
                        pltpu.VMEM((2, D_FF, D_MODEL), f32),
                        pltpu.VMEM((D_MODEL, 2 * D_FF), bf16), pltpu.VMEM((D_FF, D_MODEL), bf16),
                        pltpu.SemaphoreType.DMA((2, 3))],
    )
    return pl.pallas_call(
        _moe_kernel,
        out_shape=jax.ShapeDtypeStruct((rows, D_MODEL), f32),
        grid_spec=grid_spec,
        compiler_params=_cparams(("arbitrary",)),
        name="moe_experts",
    )(*items, xs, w_gate, w_up, w_down)


def _combine_kernel(mi_ref, mi_next_ref, x1p_ref, mwp_ref, x1s_ref, mws_ref, fnw_ref, ys_ref,
                    yp_ref, ysm_ref, g_ref, sem, *, np_tiles):
    i = pl.program_id(0)
    n = pl.num_programs(0)
    tm = x1p_ref.shape[0]
    slot = lax.rem(i, 2)

    def gather_rows(m_ref, dst_slot):
        def body(t, c):
            for u in range(SUBLANE):
                rec = MI_W * (SUBLANE * t + u)
                for k in range(2):
                    src = ys_ref.at[m_ref[rec + 2 * k], pl.ds(m_ref[rec + 2 * k + 1], 1)]
                    pltpu.make_async_copy(src, g_ref.at[dst_slot, k, t, pl.ds(u, 1)],
                                          sem.at[dst_slot]).start(priority=k)
            return c
        lax.fori_loop(0, tm // SUBLANE, body, 0)

    @pl.when(i == 0)
    def _():
        gather_rows(mi_ref, 0)

    @pl.when(i + 1 < n)
    def _():
        gather_rows(mi_next_ref, 1 - slot)

    for k in range(2):
        pltpu.make_async_copy(ys_ref.at[pl.ds(0, tm // SUBLANE)], g_ref.at[slot, k], sem.at[slot]).wait()

    def finish(x1_ref, mw_ref, out_ref):
        mw = mw_ref[...]
        g0 = g_ref[slot, 0].reshape(tm, D_MODEL)
        g1 = g_ref[slot, 1].reshape(tm, D_MODEL)
        x2 = x1_ref[...] + (g0 * mw[:, 0:1] + g1 * mw[:, 1:2])
        ms = jnp.mean(x2 * x2, axis=-1, keepdims=True)
        out_ref[...] = x2 * lax.rsqrt(ms + EPS) * fnw_ref[...]

    @pl.when(i < np_tiles)
    def _():
        finish(x1p_ref, mwp_ref, yp_ref)

    @pl.when(i >= np_tiles)
    def _():
        finish(x1s_ref, mws_ref, ysm_ref)


def _combine(x1_p, mw_p, x1_s, mw_s, fnw_row, ys3, mi_flat):
    tm = MOE_ROWS
    n_p, n_s = x1_p.shape[0], x1_s.shape[0]
    assert n_p % tm == 0 and n_s == tm
    np_tiles = n_p // tm
    ptile = lambda width: pl.BlockSpec((tm, width), lambda i: (jnp.minimum(i, np_tiles - 1), 0))
    stile = lambda width: pl.BlockSpec((tm, width), lambda i: (0, 0))
    return pl.pallas_call(
        functools.partial(_combine_kernel, np_tiles=np_tiles),
        out_shape=(jax.ShapeDtypeStruct((n_p, D_MODEL), f32),
                   jax.ShapeDtypeStruct((n_s, D_MODEL), f32)),
        grid=(np_tiles + 1,),
        in_specs=[pl.BlockSpec((MI_W * tm,), lambda i: (i,), memory_space=pltpu.SMEM),
                  pl.BlockSpec((MI_W * tm,), lambda i: (jnp.minimum(i + 1, np_tiles),), memory_space=pltpu.SMEM),
                  ptile(D_MODEL), ptile(LANE), stile(D_MODEL), stile(LANE),
                  pl.BlockSpec((1, D_MODEL), lambda i: (0, 0)),
                  pl.BlockSpec(memory_space=pl.ANY)],
        out_specs=(ptile(D_MODEL), stile(D_MODEL)),
        scratch_shapes=[pltpu.VMEM((2, 2, tm // SUBLANE, SUBLANE, D_MODEL), f32), pltpu.SemaphoreType.DMA((2,))],
        compiler_params=_cparams(("arbitrary",)),
        name="moe_combine",
    )(mi_flat, mi_flat, x1_p, mw_p, x1_s, mw_s, fnw_row, ys3)


PLAN_ROWS = 256
N_ITEM_FIELDS = 7


def _plan_kernel(cnt_ref, items_ref, rows_ref, *, nblk):
    cnt = cnt_ref[...]
    lane1 = lax.broadcasted_iota(i32, (1, LANE), 1)
    in_e = lane1 < N_EXPERTS
    ri = lax.broadcasted_iota(i32, (LANE, LANE), 0)
    ci = lax.broadcasted_iota(i32, (LANE, LANE), 1)
    upper = jnp.where(ri <= ci, 1.0, 0.0).astype(bf16)

    def cumsum_lanes(v):
        return _dot_lsplit(jnp.broadcast_to(v, (8, LANE)), upper)[0:1, :]

    shift = MOE_ROWS.bit_length() - 1
    ends = cumsum_lanes(cnt)
    starts = ends - cnt
    act = cnt > 0.0
    first_blk = (starts.astype(i32) >> shift).astype(f32)
    last_blk = (jnp.maximum(ends - 1.0, 0.0).astype(i32) >> shift).astype(f32)
    nvis = jnp.where(act, last_blk - first_blk + 1.0, 0.0)
    vis_end = cumsum_lanes(nvis)
    vis_start = vis_end - nvis
    total = jnp.max(vis_end, axis=-1, keepdims=True)
    cum_act = cumsum_lanes(jnp.where(act, 1.0, 0.0))
    n_uniq = jnp.max(cum_act, axis=-1, keepdims=True)

    p = PLAN_ROWS
    lane = lax.broadcasted_iota(i32, (p, LANE), 1)
    idx = lax.broadcasted_iota(i32, (p, LANE), 0).astype(f32)
    idx1 = idx[:, 0:1]
    count_le = lambda row, col: jnp.sum(jnp.where((row <= col) & in_e, 1.0, 0.0), axis=-1, keepdims=True)
    e = jnp.minimum(count_le(vis_end, idx), N_EXPERTS - 1.0)
    onehot = lane.astype(f32) == e
    look = lambda tbl: jnp.sum(jnp.where(onehot, tbl, 0.0), axis=-1, keepdims=True)
    blk = look(first_blk) + idx1 - look(vis_start)
    lo = jnp.maximum(look(starts), blk * MOE_ROWS) - blk * MOE_ROWS
    hi = jnp.minimum(look(ends), (blk + 1.0) * MOE_ROWS) - blk * MOE_ROWS
    valid = idx1 < total
    blk = jnp.where(valid, blk, nblk - 1.0)
    lo = jnp.where(valid, lo, 0.0)
    hi = jnp.where(valid, hi, 0.0)
    rep = lambda c: jnp.broadcast_to(c, (p, LANE))
    prev = lambda c: pltpu.roll(rep(c), 1, axis=0)[:, 0:1]
    is0 = idx1 == 0.0
    first = valid & (is0 | (blk != prev(blk)))
    newe = valid & (is0 | (e != prev(e)))
    rp = lax.broadcasted_iota(i32, (p, p), 0)
    cp = lax.broadcasted_iota(i32, (p, p), 1)
    lower = jnp.where(rp >= cp, 1.0, 0.0).astype(bf16)
    order = _dot(lower, rep(jnp.where(newe, 1.0, 0.0)).astype(bf16))[:, 0:1] - 1.0
    slot = jnp.where(newe, (order.astype(i32) & 1).astype(f32), 0.0)
    k2 = order + 2.0
    pre = jnp.where(newe & (k2 < n_uniq), count_le(cum_act, rep(k2)), -1.0)
    out = jnp.zeros((p, LANE), f32)
    for c, v in enumerate([blk, lo, hi, jnp.where(first, 1.0, 0.0), jnp.where(newe, 1.0, 0.0), slot, pre]):
        out = jnp.where(lane == c, v, out)
    items_ref[...] = out.astype(i32)

    u0 = count_le(cum_act, 0.0)
    u1 = jnp.where(n_uniq > 1.0, count_le(cum_act, 1.0), -1.0)
    rows_ref[...] = jnp.zeros(rows_ref.shape, f32)
    rows_ref[0:1, :] = starts
    rows_ref[1:2, :] = jnp.where(lane1 == 0, u0, jnp.where(lane1 == 1, u1, 0.0))


def _work_items(cnt_row, n_rows):
    nblk = n_rows // MOE_ROWS
    n_items = nblk + N_EXPERTS - 1
    assert n_items <= PLAN_ROWS and n_rows % MOE_ROWS == 0
    items, rows = pl.pallas_call(
        functools.partial(_plan_kernel, nblk=nblk),
        out_shape=(jax.ShapeDtypeStruct((PLAN_ROWS, LANE), i32), jax.ShapeDtypeStruct((8, LANE), f32)),
        compiler_params=pltpu.CompilerParams(vmem_limit_bytes=VMEM_LIMIT),
        name="moe_plan",
    )(cnt_row)
    fields = tuple(items[0:n_items, c] for c in range(N_ITEM_FIELDS))
    return rows[0:1, :], fields + (rows[1, 0:2].astype(i32),)


def kernel(x_prompt, x_sample, state_delta, state_qkv_conv, state_short_conv, norm1_w, w_in, conv_a_w, a_log, dt_bias, out_norm_w, w_branch_a, conv_b_w, w_branch_b, w_o, norm2_w, router_group_w, router_group_b, router_expert_w, router_expert_b, w_gate, w_up, w_down, final_norm_w):
    assert norm1_w.shape[0] == 1, "single-layer trunk"
    bp, tp, d = x_prompt.shape
    bs, ts, _ = x_sample.shape
    assert d == D_MODEL and ts == 1
    n_p = bp * tp
    n_s = bs
    n_all = n_p + n_s

    w_perm = _wprep(jnp.transpose(w_in[0]))
    wa = w_branch_a[0].astype(bf16)
    wb = w_branch_b[0].astype(bf16)
    wo = w_o[0].astype(bf16)
    pad = lambda v: jnp.zeros((1, BA_W), f32).at[0, N_HEADS:2 * N_HEADS].set(v)
    alog_row = pad(a_log[0])
    dtb_row = pad(dt_bias[0])
    onw_row = out_norm_w[0].reshape(1, HEAD)
    cwa = conv_a_w[0]
    cwb = conv_b_w[0]
    r_pad = LANE - N_EXPERTS - N_GROUPS
    rw = jnp.concatenate([router_expert_w[0], router_group_w[0], jnp.zeros((D_MODEL, r_pad), f32)], axis=1)
    rwh = rw.astype(bf16)
    rwl = (rw - rwh.astype(f32)).astype(bf16)
    rb_row = jnp.concatenate([router_expert_b[0], router_group_b[0], jnp.zeros((r_pad,), f32)]).reshape(1, LANE)
    n2_row = norm2_w[0].reshape(1, D_MODEL)

    xp2 = x_prompt.reshape(n_p, D_MODEL)
    proj_p, tails = _inproj_conv(xp2, norm1_w[0], w_perm, cwa, tp)
    tiles_per_seq = tails.shape[0] // bp
    nca_p = tails.reshape(bp, tiles_per_seq, 8, CONV_PAD_W)[:, -1, 8 - (CONV_A - 1):8, 0:QKV_W]
    o_p, y_p, sd_p, ncb_p = _delta_prompt(proj_p.reshape(bp, tp, PROJ_W), cwb, alog_row, dtb_row,
                                          onw_row, nb_step=4 if bp % 4 == 0 else (2 if bp % 2 == 0 else 1))
    cnt0 = jnp.zeros((1, LANE), f32)
    x1_p, h2_p, mi_p, mw_p, cnt_p = _mix_route(xp2, o_p.reshape(n_p, QK_W), y_p.reshape(n_p, SC_W), proj_p,
                                               wa, wb, wo, n2_row, rwh, rwl, rb_row, cnt0)

    xs2 = x_sample.reshape(n_s, D_MODEL)
    proj_s = _inproj(xs2, norm1_w[0], w_perm)
    bufa_t = jnp.transpose(state_qkv_conv[0], (1, 0, 2))
    bufb_t = jnp.transpose(state_short_conv[0], (1, 0, 2))
    q_s, k_s, v_s, beta_s, eg_s, y_s, nbufa_t, nbufb_t = _sample_prep(proj_s, bufa_t, bufb_t, cwa, cwb,
                                                                      alog_row, dtb_row)
    h3 = lambda a: a.reshape(n_s, N_HEADS, HEAD)
    z_s = proj_s[:, COL_Z:COL_Z + QK_W]
    sd_s, o_s = _sample_step(state_delta[0], h3(q_s), h3(k_s), h3(v_s), h3(beta_s), h3(eg_s), h3(z_s), onw_row)
    o_s2 = o_s.reshape(n_s, QK_W).astype(bf16)
    x1_s, h2_s, mi_s, mw_s, cnt = _mix_route(xs2, o_s2, y_s, proj_s, wa, wb, wo, n2_row, rwh, rwl, rb_row, cnt_p)

    starts_row, items = _work_items(cnt, 2 * n_all)
    mi_flat = jnp.concatenate([_dest_rows(mi_p, starts_row)[:, 0:MI_W], _dest_rows(mi_s, starts_row)[:, 0:MI_W]],
                              axis=0).reshape(MI_W * n_all)
    xs_sorted = _dispatch(h2_p, h2_s, mi_flat)
    ys = _moe(xs_sorted.reshape(2 * n_all, D_MODEL), w_gate[0], w_up[0], w_down[0], items)
    y_prompt, y_sample = _combine(x1_p, mw_p, x1_s, mw_s, final_norm_w.reshape(1, D_MODEL),
                                  ys.reshape(2 * n_all // SUBLANE, SUBLANE, D_MODEL), mi_flat)

    return (y_prompt.reshape(bp, tp, D_MODEL),
            y_sample.reshape(bs, ts, D_MODEL),
            sd_p[None],
            nca_p[None],
            ncb_p[None],
            sd_s[None],
            jnp.transpose(nbufa_t, (1, 0, 2))[None],
            jnp.transpose(nbufb_t, (1, 0, 2))[None])
```

```python
import functools

import jax
import jax.numpy as jnp
from jax import lax
from jax.experimental import pallas as pl
from jax.experimental.pallas import tpu as pltpu

f32 = jnp.float32
bf16 = jnp.bfloat16
i32 = jnp.int32

EPS = 1e-6
LANE = 128
D_MODEL = 2048
N_HEADS = 8
HEAD = 128
QK_W = N_HEADS * HEAD
QKV_W = 3 * QK_W
SC_W = 1024
CONV_A = 4
CONV_B = 3
CHUNK = 64
GROUP_HEADS = 4
N_EXPERTS = 64
N_GROUPS = 8
EXPERTS_PER_GROUP = 8
D_FF = 512
MOE_ROWS = 128

COL_QKV = 0
COL_BCX = 3072
COL_GA = 6144
COL_GB = 8192
COL_Z = 10240
COL_BA = 11264
BA_W = 256
PROJ_W = 11520
PROJ_TN = 1280

VMEM_LIMIT = 56 * 1024 * 1024


def _dot(a, b):
    return jnp.dot(a, b, preferred_element_type=f32)


def _dot_nt(a, b):
    return lax.dot_general(a, b, (((1,), (1,)), ((), ())), preferred_element_type=f32)


def _split(x, n):
    parts = []
    r = x
    for i in range(n):
        p = r.astype(bf16)
        parts.append(p)
        if i + 1 < n:
            r = r - p.astype(f32)
    return parts


def _dot_lsplit(x, m, n=3):
    rows = x.shape[0]
    d = _dot(jnp.concatenate(_split(x, n), axis=0), m)
    acc = d[0:rows]
    for i in range(1, n):
        acc = acc + d[i * rows:(i + 1) * rows]
    return acc


def _dot_rsplit(m, x, n=3):
    cols = x.shape[1]
    d = _dot(m, jnp.concatenate(_split(x, n), axis=1))
    acc = d[:, 0:cols]
    for i in range(1, n):
        acc = acc + d[:, i * cols:(i + 1) * cols]
    return acc


def _silu(x):
    return x * jax.nn.sigmoid(x)


def _softplus(x):
    return jnp.maximum(x, 0.0) + jnp.log(1.0 + jnp.exp(-jnp.abs(x)))


def _cparams(sem):
    return pltpu.CompilerParams(dimension_semantics=sem, vmem_limit_bytes=VMEM_LIMIT)


def _inproj_kernel(x_ref, nw_ref, w_ref, o_ref, h_ref, *, rows):
    @pl.when(pl.program_id(1) == 0)
    def _():
        def body(r, c):
            sl = pl.ds(pl.multiple_of(r * rows, rows), rows)
            x = x_ref[sl, :]
            ms = jnp.mean(x * x, axis=-1, keepdims=True)
            h_ref[sl, :] = (x * lax.rsqrt(ms + EPS) * nw_ref[...]).astype(bf16)
            return c
        lax.fori_loop(0, x_ref.shape[0] // rows, body, 0)

    o_ref[...] = _dot_nt(h_ref[...], w_ref[...])


def _inproj(x2d, norm_w, w_bf16):
    n = x2d.shape[0]
    tm = min(1024, n)
    assert n % tm == 0 and PROJ_W % PROJ_TN == 0
    return pl.pallas_call(
        functools.partial(_inproj_kernel, rows=min(128, tm)),
        out_shape=jax.ShapeDtypeStruct((n, PROJ_W), f32),
        grid=(n // tm, PROJ_W // PROJ_TN),
        in_specs=[pl.BlockSpec((tm, D_MODEL), lambda i, j: (i, 0)),
                  pl.BlockSpec((1, D_MODEL), lambda i, j: (0, 0)),
                  pl.BlockSpec((PROJ_TN, D_MODEL), lambda i, j: (j, 0))],
        out_specs=pl.BlockSpec((tm, PROJ_TN), lambda i, j: (i, j)),
        scratch_shapes=[pltpu.VMEM((tm, D_MODEL), bf16)],
        compiler_params=_cparams(("arbitrary", "arbitrary")),
        name="inproj",
    )(x2d, norm_w.reshape(1, D_MODEL), w_bf16)


CONV_TILES = 3
CONV_COLS = 2 * HEAD
CONV_ROWS = 128
CONV_PAD_W = CONV_TILES * PROJ_TN


def _qkv_kind(col):
    return "q" if col < QK_W else "k" if col < 2 * QK_W else "v" if col < QKV_W else "raw"


def _inproj_conv_kernel(x_ref, nw_ref, w_ref, cw_ref, o_ref, tail_ref, h_ref, hist_ref, raw_ref, *,
                        rows, tiles_per_seq):
    i = pl.program_id(0)
    j = pl.program_id(1)
    tm = x_ref.shape[0]

    @pl.when(j == 0)
    def _():
        def body(r, c):
            sl = pl.ds(pl.multiple_of(r * rows, rows), rows)
            x = x_ref[sl, :]
            ms = jnp.mean(x * x, axis=-1, keepdims=True)
            h_ref[sl, :] = (x * lax.rsqrt(ms + EPS) * nw_ref[...]).astype(bf16)
            return c
        lax.fori_loop(0, tm // rows, body, 0)

    @pl.when((i == 0) & (j == 0))
    def _():
        hist_ref[...] = jnp.zeros(hist_ref.shape, f32)

    @pl.when(j >= CONV_TILES)
    def _():
        o_ref[...] = _dot_nt(h_ref[...], w_ref[...])

    seq_start = lax.rem(i, tiles_per_seq) == 0
    for jj in range(CONV_TILES):
        @pl.when(j == jj)
        def _():
            def matmul_chunk(idx, c0):
                raw_ref[idx % 2] = _dot_nt(h_ref[...], w_ref[c0:c0 + CONV_COLS, :])

            def conv_chunk(idx, c0):
                cs = slice(c0, c0 + CONV_COLS)
                raw = raw_ref.at[idx % 2]
                tail = raw[tm - 8:tm, :]
                tail_ref[0, :, cs] = tail
                kinds = [_qkv_kind(jj * PROJ_TN + c0 + g * HEAD) for g in range(CONV_COLS // HEAD)]
                if kinds[0] == "raw":
                    o_ref[:, cs] = raw[...]
                    return
                hist = jnp.where(seq_start, 0.0, hist_ref[jj, :, cs])
                for rc in range(tm // CONV_ROWS):
                    r0 = rc * CONV_ROWS
                    if rc > 0:
                        xe = raw[r0 - 8:r0 + CONV_ROWS, :]
                    else:
                        xe = jnp.concatenate([hist, raw[0:CONV_ROWS, :]], axis=0)
                    acc = pltpu.roll(xe, 3, axis=0)[8:] * cw_ref[0:1, cs]
                    acc = acc + pltpu.roll(xe, 2, axis=0)[8:] * cw_ref[1:2, cs]
                    acc = acc + pltpu.roll(xe, 1, axis=0)[8:] * cw_ref[2:3, cs]
                    acc = acc + xe[8:] * cw_ref[3:4, cs]
                    act = _silu(acc)
                    for g, kind in enumerate(kinds):
                        ah = act[:, g * HEAD:(g + 1) * HEAD]
                        if kind != "v":
                            ss = jnp.sum(ah * ah, axis=-1, keepdims=True)
                            inv = lax.rsqrt(ss + EPS)
                            ah = ah * (inv * (HEAD ** -0.5) if kind == "q" else inv)
                        o_ref[r0:r0 + CONV_ROWS, c0 + g * HEAD:c0 + (g + 1) * HEAD] = ah
                hist_ref[jj, :, cs] = tail

            chunks = list(range(0, PROJ_TN, CONV_COLS))
            matmul_chunk(0, chunks[0])
            for idx in range(1, len(chunks)):
                matmul_chunk(idx, chunks[idx])
                conv_chunk(idx - 1, chunks[idx - 1])
            conv_chunk(len(chunks) - 1, chunks[-1])


def _inproj_conv(x2d, norm_w, w_bf16, cwa, seq_len):
    n = x2d.shape[0]
    tm = min(1024, seq_len)
    assert n % tm == 0 and seq_len % tm == 0 and PROJ_W % PROJ_TN == 0 and tm % CONV_ROWS == 0
    assert QKV_W % CONV_COLS == 0 and PROJ_TN % CONV_COLS == 0
    cw_pad = jnp.zeros((CONV_A, CONV_PAD_W), f32).at[:, 0:QKV_W].set(cwa)
    last = CONV_TILES - 1
    return pl.pallas_call(
        functools.partial(_inproj_conv_kernel, rows=min(128, tm), tiles_per_seq=seq_len // tm),
        out_shape=(jax.ShapeDtypeStruct((n, PROJ_W), f32),
                   jax.ShapeDtypeStruct((n // tm, 8, CONV_PAD_W), f32)),
        grid=(n // tm, PROJ_W // PROJ_TN),
        in_specs=[pl.BlockSpec((tm, D_MODEL), lambda i, j: (i, 0)),
                  pl.BlockSpec((1, D_MODEL), lambda i, j: (0, 0)),
                  pl.BlockSpec((PROJ_TN, D_MODEL), lambda i, j: (j, 0)),
                  pl.BlockSpec((CONV_A, PROJ_TN), lambda i, j: (0, jnp.minimum(j, last)))],
        out_specs=(pl.BlockSpec((tm, PROJ_TN), lambda i, j: (i, j)),
                   pl.BlockSpec((1, 8, PROJ_TN), lambda i, j: (i, 0, jnp.minimum(j, last)))),
        scratch_shapes=[pltpu.VMEM((tm, D_MODEL), bf16), pltpu.VMEM((CONV_TILES, 8, PROJ_TN), f32),
                        pltpu.VMEM((2, tm, CONV_COLS), f32)],
        compiler_params=_cparams(("arbitrary", "arbitrary")),
        name="inproj_conv",
    )(x2d, norm_w.reshape(1, D_MODEL), w_bf16, cw_pad)


W_IN_COLS = 11280
WPREP_TN = 1024
WPREP_SHIFT = 16


def _wprep_kernel(a_ref, b_ref, o_ref):
    j = pl.program_id(0)
    keep = WPREP_TN - WPREP_SHIFT

    @pl.when((j < 3) | (j == 10))
    def _():
        o_ref[...] = a_ref[...].astype(bf16)

    @pl.when((j >= 3) & (j < 10))
    def _():
        o_ref[0:keep, :] = a_ref[WPREP_SHIFT:WPREP_TN, :].astype(bf16)
        o_ref[keep:WPREP_TN, :] = b_ref[...].astype(bf16)

    @pl.when(j == 11)
    def _():
        o_ref[0:WPREP_SHIFT, :] = a_ref[0:WPREP_SHIFT, :].astype(bf16)
        o_ref[WPREP_SHIFT:WPREP_TN, :] = jnp.zeros((keep, D_MODEL), bf16)


def _wprep(w_in_t):
    assert w_in_t.shape == (W_IN_COLS, D_MODEL) and 2 * N_HEADS == WPREP_SHIFT
    n_blk = pl.cdiv(PROJ_W, WPREP_TN)

    def a_map(j):
        return (jnp.where(j < 3, j, jnp.where(j < 10, j + 1, jnp.where(j == 10, 3, 4))), 0)

    def b_map(j):
        return (jnp.minimum((WPREP_TN // WPREP_SHIFT) * (j + 2), W_IN_COLS // WPREP_SHIFT - 1), 0)

    return pl.pallas_call(
        _wprep_kernel,
        out_shape=jax.ShapeDtypeStruct((PROJ_W, D_MODEL), bf16),
        grid=(n_blk,),
        in_specs=[pl.BlockSpec((WPREP_TN, D_MODEL), a_map),
                  pl.BlockSpec((WPREP_SHIFT, D_MODEL), b_map)],
        out_specs=pl.BlockSpec((WPREP_TN, D_MODEL), lambda j: (j, 0)),
        compiler_params=_cparams(("arbitrary",)),
        name="wprep",
    )(w_in_t, w_in_t)


def _head_l2norm(a, scale):
    outs = []
    for h in range(N_HEADS):
        ah = a[:, h * HEAD:(h + 1) * HEAD]
        ss = jnp.sum(ah * ah, axis=-1, keepdims=True)
        n = ah * lax.rsqrt(ss + EPS)
        outs.append(n * scale if scale != 1.0 else n)
    return outs


def _delta_prompt_kernel(qkv_ref, bcx_ref, z_ref, ba_ref, cwb_ref, alog_ref, dtb_ref, onw_ref,
                         e64_ref,
                         o_ref, y_ref, snew_ref, ncb_ref,
                         s_ref, xb_ref, *, nb_step):
    C = CHUNK
    G = GROUP_HEADS
    R = G * C
    t = pl.program_id(1)
    nt = pl.num_programs(1)

    @pl.when(t == 0)
    def _():
        s_ref[...] = jnp.zeros(s_ref.shape, f32)
        xb_ref[:, 0:8, :] = jnp.zeros((nb_step, 8, SC_W), f32)

    rr = lax.broadcasted_iota(i32, (R, R), 0)
    cc = lax.broadcasted_iota(i32, (R, R), 1)
    same_bf = jnp.where((rr >> 6) == (cc >> 6), 1.0, 0.0).astype(bf16)
    r2 = lax.broadcasted_iota(i32, (R, G * HEAD), 0)
    c2 = lax.broadcasted_iota(i32, (R, G * HEAD), 1)
    bdmask = (r2 >> 6) == (c2 >> 7)
    r3 = lax.broadcasted_iota(i32, (C, C), 0)
    c3 = lax.broadcasted_iota(i32, (C, C), 1)
    ltri = jnp.where(r3 >= c3, 1.0, 0.0).astype(bf16)
    r4 = lax.broadcasted_iota(i32, (C, R), 0)
    c4 = lax.broadcasted_iota(i32, (C, R), 1)
    ident_t = r4 == (c4 & (C - 1))
    incl_p = r4 >= (c4 & (C - 1))
    strict_p = r4 > (c4 & (C - 1))
    hblk = c4 >> 6
    ones8 = jnp.ones((8, C), bf16)

    nbs = range(nb_step)
    units = [(nb, g) for nb in nbs for g in range(N_HEADS // G)]
    heads = lambda g: range(g * G, (g + 1) * G)

    qn = [[qkv_ref[nb, :, h * HEAD:(h + 1) * HEAD] for h in range(N_HEADS)] for nb in nbs]
    kn = [[qkv_ref[nb, :, QK_W + h * HEAD:QK_W + (h + 1) * HEAD] for h in range(N_HEADS)] for nb in nbs]
    vv = [qkv_ref[nb, :, 2 * QK_W:3 * QK_W] for nb in nbs]

    bts = [ba_ref[nb, :, 0:LANE] for nb in nbs]
    beta_all = [jax.nn.sigmoid(bt) for bt in bts]
    g_all = [-(jnp.exp(alog_ref[:, 0:LANE]) * _softplus(bt + dtb_ref[:, 0:LANE])) for bt in bts]
    gc_small = [_dot_rsplit(ltri, ga) for ga in g_all]
    gl_small = [gc[C - 1:C, :] for gc in gc_small]

    k_st, q_st, kb, vb, kbg, qd, kd, gc_col = ({} for _ in range(8))
    for u in units:
        nb, g = u
        hs = heads(g)
        k_st[u] = jnp.concatenate([kn[nb][h] for h in hs], axis=0)
        q_st[u] = jnp.concatenate([qn[nb][h] for h in hs], axis=0)
        v_st = jnp.concatenate([vv[nb][:, h * HEAD:(h + 1) * HEAD] for h in hs], axis=0)
        beta_col = jnp.concatenate([beta_all[nb][:, h:h + 1] for h in hs], axis=0)
        gc_col[u] = jnp.concatenate([gc_small[nb][:, 8 + h:9 + h] for h in hs], axis=0)
        gl_col = jnp.concatenate(
            [jnp.broadcast_to(gl_small[nb][:, 8 + h:9 + h], (C, 1)) for h in hs], axis=0)
        kb[u] = k_st[u] * beta_col
        vb[u] = v_st * beta_col
        egc = jnp.exp(gc_col[u])
        kbg[u] = kb[u] * egc
        qd[u] = q_st[u] * egc
        kd[u] = k_st[u] * jnp.exp(gl_col - gc_col[u])

    gx = {u: _dot_lsplit(gc_small[u[0]], e64_ref[u[1], 0:LANE, :]) for u in units}
    crow = {u: _dot_rsplit(ones8, jnp.where(ident_t, gx[u], 0.0))[0:1, :] for u in units}
    a = {u: _dot_nt(jnp.concatenate([kb[u], q_st[u]], axis=0).astype(bf16), k_st[u].astype(bf16))
         for u in units}
    in_blk = [hblk == h for h in range(G - 1)]

    def pack(x):
        out = x[(G - 1) * C:G * C]
        for h in reversed(range(G - 1)):
            out = jnp.where(in_blk[h], x[h * C:(h + 1) * C], out)
        return out

    def expand(xp):
        return jnp.concatenate([xp.astype(bf16)] * G, axis=0) * same_bf

    dec = {u: jnp.where(incl_p, jnp.exp(jnp.where(incl_p, gx[u] - crow[u], 0.0)), 0.0) for u in units}
    nm = {u: jnp.where(strict_p, -(pack(a[u][0:R]) * dec[u]), 0.0) for u in units}
    qkm = {u: expand(pack(a[u][R:2 * R]) * dec[u]) for u in units}

    p = {u: jnp.where(ident_t, 1.0, 0.0) + nm[u] for u in units}
    nk = {u: _dot(nm[u].astype(bf16), expand(nm[u])) for u in units}
    for _ in range(4):
        for u in units:
            x = _dot(jnp.concatenate([p[u], nk[u]], axis=0).astype(bf16), expand(nk[u]))
            p[u] = p[u] + x[0:C]
            nk[u] = x[C:2 * C]
    for u in units:
        p[u] = p[u] + _dot(p[u].astype(bf16), expand(nk[u]))
    uw = {u: _dot(expand(p[u]), jnp.concatenate([vb[u], kbg[u]], axis=1).astype(bf16)) for u in units}

    ws = {}
    for u in units:
        nb, g = u
        for j, h in enumerate(heads(g)):
            sh = s_ref[nb, :, h * HEAD:(h + 1) * HEAD]
            lhs = jnp.concatenate([uw[u][j * C:(j + 1) * C, HEAD:2 * HEAD], qd[u][j * C:(j + 1) * C]], axis=0)
            ws[u, j] = _dot(lhs.astype(bf16), sh.astype(bf16))
    o_heads = {}
    for u in units:
        nb, g = u
        vnew_st = jnp.concatenate([uw[u][j * C:(j + 1) * C, 0:HEAD] - ws[u, j][0:C] for j in range(G)], axis=0)
        o_st = (jnp.concatenate([ws[u, j][C:2 * C] for j in range(G)], axis=0)
                + _dot(qkm[u], vnew_st.astype(bf16)))
        vbd = jnp.where(bdmask, jnp.concatenate([vnew_st] * G, axis=1), 0.0)
        lo = g * G * HEAD
        hi = lo + G * HEAD
        gl_row = jnp.concatenate(
            [jnp.broadcast_to(jnp.exp(gl_small[nb][:, 8 + h:9 + h]), (1, HEAD)) for h in heads(g)], axis=1)
        s_ref[nb, :, lo:hi] = s_ref[nb, :, lo:hi] * gl_row + _dot(kd[u].T.astype(bf16), vbd.astype(bf16))
        for j, h in enumerate(heads(g)):
            o_heads[nb, h] = o_st[j * C:(j + 1) * C]

    for nb in nbs:
        zt = z_ref[nb]
        for h in range(N_HEADS):
            oh = o_heads[nb, h]
            ms = jnp.mean(oh * oh, axis=-1, keepdims=True)
            zh = zt[:, h * HEAD:(h + 1) * HEAD]
            on = oh * lax.rsqrt(ms + EPS) * onw_ref[...] * _silu(zh)
            o_ref[nb, :, h * HEAD:(h + 1) * HEAD] = on.astype(bf16)

    for nb in nbs:
        bcx = bcx_ref[nb]
        cx = bcx[:, SC_W:2 * SC_W] * bcx[:, 2 * SC_W:3 * SC_W]
        xb_ref[nb, 8:8 + C, :] = cx
        ce = xb_ref[nb]
        cv = pltpu.roll(ce, 2, axis=0)[8:8 + C] * cwb_ref[0:1, :]
        cv = cv + pltpu.roll(ce, 1, axis=0)[8:8 + C] * cwb_ref[1:2, :]
        cv = cv + cx * cwb_ref[2:3, :]
        y_ref[nb] = (bcx[:, 0:SC_W] * cv).astype(bf16)
        xb_ref[nb, 0:8, :] = xb_ref[nb, C:C + 8, :]

    @pl.when(t == nt - 1)
    def _():
        for nb in range(nb_step):
            for h in range(N_HEADS):
                snew_ref[nb, h] = s_ref[nb, :, h * HEAD:(h + 1) * HEAD]
            ncb_ref[nb] = xb_ref[nb, 6:8, :]


def _expand_consts():
    lane = jnp.arange(BA_W)[:, None]
    col = jnp.arange(QK_W)[None, :]
    eb = (lane == (col >> 7)).astype(bf16)
    eg = (lane == (8 + (col >> 7))).astype(bf16)
    col64 = jnp.arange(GROUP_HEADS * CHUNK)[None, :]
    e64 = jnp.stack([(lane == (8 + g * GROUP_HEADS + (col64 >> 6))).astype(bf16)
                     for g in range(N_HEADS // GROUP_HEADS)], axis=0)
    return eb, eg, e64


def _delta_prompt(proj3, cwb, alog_row, dtb_row, onw_row, nb_step):
    b, t, _ = proj3.shape
    assert t % CHUNK == 0 and b % nb_step == 0
    _, _, e64 = _expand_consts()
    c = CHUNK
    const2 = lambda bi, ti: (0, 0)
    outs = pl.pallas_call(
        functools.partial(_delta_prompt_kernel, nb_step=nb_step),
        out_shape=(jax.ShapeDtypeStruct((b, t, QK_W), bf16),
                   jax.ShapeDtypeStruct((b, t, SC_W), bf16),
                   jax.ShapeDtypeStruct((b, N_HEADS, HEAD, HEAD), f32),
                   jax.ShapeDtypeStruct((b, CONV_B - 1, SC_W), f32)),
        grid=(b // nb_step, t // c),
        in_specs=[pl.BlockSpec((nb_step, c, QKV_W), lambda bi, ti: (bi, ti, COL_QKV // QKV_W)),
                  pl.BlockSpec((nb_step, c, QKV_W), lambda bi, ti: (bi, ti, COL_BCX // QKV_W)),
                  pl.BlockSpec((nb_step, c, QK_W), lambda bi, ti: (bi, ti, COL_Z // QK_W)),
                  pl.BlockSpec((nb_step, c, BA_W), lambda bi, ti: (bi, ti, COL_BA // BA_W)),
                  pl.BlockSpec((CONV_B, SC_W), const2),
                  pl.BlockSpec((1, BA_W), const2),
                  pl.BlockSpec((1, BA_W), const2),
                  pl.BlockSpec((1, HEAD), const2),
                  pl.BlockSpec((N_HEADS // GROUP_HEADS, BA_W, GROUP_HEADS * CHUNK), lambda bi, ti: (0, 0, 0))],
        out_specs=(pl.BlockSpec((nb_step, c, QK_W), lambda bi, ti: (bi, ti, 0)),
                   pl.BlockSpec((nb_step, c, SC_W), lambda bi, ti: (bi, ti, 0)),
                   pl.BlockSpec((nb_step, N_HEADS, HEAD, HEAD), lambda bi, ti: (bi, 0, 0, 0)),
                   pl.BlockSpec((nb_step, CONV_B - 1, SC_W), lambda bi, ti: (bi, 0, 0))),
        scratch_shapes=[pltpu.VMEM((nb_step, HEAD, QK_W), f32),
                        pltpu.VMEM((nb_step, 8 + c, SC_W), f32)],
        compiler_params=_cparams(("arbitrary", "arbitrary")),
        name="delta_prompt",
    )(proj3, proj3, proj3, proj3, cwb, alog_row, dtb_row, onw_row, e64)
    return outs


def _sample_prep_kernel(p_ref, bufa_ref, bufb_ref, cwa_ref, cwb_ref, alog_ref, dtb_ref, eb_ref, eg_ref,
                        q_ref, k_ref, v_ref, beta_ref, eg_out_ref, y_ref, nbufa_ref, nbufb_ref):
    def conv_sec(lo):
        hi = lo + QK_W
        raw = p_ref[:, COL_QKV + lo:COL_QKV + hi]
        acc = bufa_ref[0, :, lo:hi] * cwa_ref[0:1, lo:hi]
        acc = acc + bufa_ref[1, :, lo:hi] * cwa_ref[1:2, lo:hi]
        acc = acc + bufa_ref[2, :, lo:hi] * cwa_ref[2:3, lo:hi]
        acc = acc + raw * cwa_ref[3:4, lo:hi]
        nbufa_ref[0, :, lo:hi] = bufa_ref[1, :, lo:hi]
        nbufa_ref[1, :, lo:hi] = bufa_ref[2, :, lo:hi]
        nbufa_ref[2, :, lo:hi] = raw
        return _silu(acc)

    qn = _head_l2norm(conv_sec(0), HEAD ** -0.5)
    kn = _head_l2norm(conv_sec(QK_W), 1.0)
    for h in range(N_HEADS):
        q_ref[:, h * HEAD:(h + 1) * HEAD] = qn[h]
        k_ref[:, h * HEAD:(h + 1) * HEAD] = kn[h]
    v_ref[...] = conv_sec(2 * QK_W)

    bt = p_ref[:, COL_BA:COL_BA + BA_W]
    beta_all = jax.nn.sigmoid(bt)
    g_all = -(jnp.exp(alog_ref[...]) * _softplus(bt + dtb_ref[...]))
    beta_ref[...] = _dot_lsplit(beta_all, eb_ref[...])
    eg_out_ref[...] = jnp.exp(_dot_lsplit(g_all, eg_ref[...]))

    bg = p_ref[:, COL_BCX:COL_BCX + SC_W]
    cx = p_ref[:, COL_BCX + SC_W:COL_BCX + 2 * SC_W] * p_ref[:, COL_BCX + 2 * SC_W:COL_BCX + 3 * SC_W]
    cv = bufb_ref[0] * cwb_ref[0:1, :]
    cv = cv + bufb_ref[1] * cwb_ref[1:2, :]
    cv = cv + cx * cwb_ref[2:3, :]
    y_ref[...] = (bg * cv).astype(bf16)
    nbufb_ref[0] = bufb_ref[1]
    nbufb_ref[1] = cx


def _sample_prep(proj_s, bufa_t, bufb_t, cwa, cwb, alog_row, dtb_row):
    n = proj_s.shape[0]
    eb, eg, _ = _expand_consts()
    row = jax.ShapeDtypeStruct((n, QK_W), f32)
    return pl.pallas_call(
        _sample_prep_kernel,
        out_shape=(row, row, row, row, row,
                   jax.ShapeDtypeStruct((n, SC_W), bf16),
                   jax.ShapeDtypeStruct((CONV_A - 1, n, QKV_W), f32),
                   jax.ShapeDtypeStruct((CONV_B - 1, n, SC_W), f32)),
        compiler_params=pltpu.CompilerParams(vmem_limit_bytes=VMEM_LIMIT),
        name="sample_prep",
    )(proj_s, bufa_t, bufb_t, cwa, cwb, alog_row, dtb_row, eb, eg)


def _sample_step_kernel(s_ref, q_ref, k_ref, v_ref, beta_ref, eg_ref, z_ref, onw_ref,
                        snew_ref, o_ref, *, bb):
    w = N_HEADS * HEAD
    r8 = lax.broadcasted_iota(i32, (N_HEADS, w), 0)
    c8 = lax.broadcasted_iota(i32, (N_HEADS, w), 1)
    mask8 = r8 == (c8 >> 7)
    zpad_k = jnp.zeros((HEAD - N_HEADS, HEAD), f32)
    zpad_d = jnp.zeros((HEAD - N_HEADS, w), f32)
    for b in range(bb):
        s_all = jnp.concatenate([s_ref[b, h] for h in range(N_HEADS)], axis=1)
        eg8 = eg_ref[b]
        eg_row = jnp.concatenate([eg8[h:h + 1, :] for h in range(N_HEADS)], axis=1)
        s_dec = s_all * eg_row
        k8 = k_ref[b]
        x = _dot(k8.astype(bf16), s_dec.astype(bf16))
        v_t = jnp.concatenate([v_ref[b]] * N_HEADS, axis=1)
        b_t = jnp.concatenate([beta_ref[b]] * N_HEADS, axis=1)
        d_bd = jnp.where(mask8, (v_t - x) * b_t, 0.0)
        kt = jnp.concatenate([k8, zpad_k], axis=0).T
        k_hi, k_lo = _split(kt, 2)
        dh = d_bd.astype(bf16).astype(f32)
        d_hi = jnp.concatenate([dh, zpad_d], axis=0).astype(bf16)
        d_lo = jnp.concatenate([d_bd - dh, zpad_d], axis=0).astype(bf16)
        s_new = s_dec + (_dot(k_hi, d_hi) + _dot(k_hi, d_lo) + _dot(k_lo, d_hi))
        yv = jnp.where(mask8, _dot(q_ref[b].astype(bf16), s_new.astype(bf16)), 0.0)
        o8 = yv[:, 0:HEAD]
        for j in range(1, N_HEADS):
            o8 = o8 + yv[:, j * HEAD:(j + 1) * HEAD]
        ms = jnp.mean(o8 * o8, axis=-1, keepdims=True)
        o_ref[b] = o8 * lax.rsqrt(ms + EPS) * onw_ref[...] * _silu(z_ref[b])
        for h in range(N_HEADS):
            snew_ref[b, h] = s_new[:, h * HEAD:(h + 1) * HEAD]


def _sample_step(state, q, k, v, beta, eg, z, onw_row, bb=4):
    n = state.shape[0]
    assert n % bb == 0
    hspec = pl.BlockSpec((bb, N_HEADS, HEAD), lambda i: (i, 0, 0))
    sspec = pl.BlockSpec((bb, N_HEADS, HEAD, HEAD), lambda i: (i, 0, 0, 0))
    return pl.pallas_call(
        functools.partial(_sample_step_kernel, bb=bb),
        out_shape=(jax.ShapeDtypeStruct(state.shape, f32),
                   jax.ShapeDtypeStruct((n, N_HEADS, HEAD), f32)),
        grid=(n // bb,),
        in_specs=[sspec, hspec, hspec, hspec, hspec, hspec, hspec, pl.BlockSpec((1, HEAD), lambda i: (0, 0))],
        out_specs=(sspec, hspec),
        compiler_params=_cparams(("arbitrary",)),
        name="sample_step",
    )(state, q, k, v, beta, eg, z, onw_row)


def _mix_route_kernel(x_ref, o_ref, y_ref, ga_ref, gb_ref, wa_ref, wb_ref, wo_ref, n2_ref,
                      rwh_ref, rwl_ref, rb_ref, cnt_in_ref, x1_ref, h2_ref, mi_ref, mw_ref, cnt_ref):
    i = pl.program_id(0)
    tm = x_ref.shape[0]

    @pl.when(i == 0)
    def _():
        cnt_ref[...] = cnt_in_ref[...]

    oa = _dot(o_ref[...], wa_ref[...])
    ob = _dot(y_ref[...], wb_ref[...])
    merged = jax.nn.sigmoid(ga_ref[...]) * oa + jax.nn.sigmoid(gb_ref[...]) * ob
    x1 = x_ref[...] + _dot(merged.astype(bf16), wo_ref[...])
    x1_ref[...] = x1
    ms = jnp.mean(x1 * x1, axis=-1, keepdims=True)
    h2 = x1 * lax.rsqrt(ms + EPS) * n2_ref[...]
    h2_ref[...] = h2

    h_hi, h_lo = _split(h2, 2)
    logits = _dot(h_hi, rwh_ref[...]) + _dot(h_hi, rwl_ref[...]) + _dot(h_lo, rwh_ref[...]) + rb_ref[...]

    lane = lax.broadcasted_iota(i32, (tm, LANE), 1)
    lanef = lane.astype(f32)
    neg = jnp.float32(-jnp.inf)
    big = jnp.float32(1e9)
    gmask = (lane >= N_EXPERTS) & (lane < N_EXPERTS + N_GROUPS)
    gl = jnp.where(gmask, logits, neg)
    gmax = jnp.max(gl, axis=-1, keepdims=True)
    gidx = jnp.min(jnp.where(gl == gmax, lanef - N_EXPERTS, big), axis=-1, keepdims=True)
    gsum = jnp.sum(jnp.where(gmask, jnp.exp(gl - gmax), 0.0), axis=-1, keepdims=True)
    gprob = 1.0 / gsum

    emask = (lane < N_EXPERTS) & ((lane >> 3).astype(f32) == gidx)
    el = jnp.where(emask, logits, neg)
    emax = jnp.max(el, axis=-1, keepdims=True)
    pe = jnp.where(emask, jnp.exp(el - emax), 0.0)
    eprob = pe / jnp.sum(pe, axis=-1, keepdims=True)
    p1m = jnp.where(emask, eprob, -1.0)
    m1 = jnp.max(p1m, axis=-1, keepdims=True)
    i1 = jnp.min(jnp.where(p1m == m1, lanef, big), axis=-1, keepdims=True)
    p2m = jnp.where(lanef == i1, -1.0, p1m)
    m2 = jnp.max(p2m, axis=-1, keepdims=True)
    i2 = jnp.min(jnp.where(p2m == m2, lanef, big), axis=-1, keepdims=True)
    tot = m1 + m2
    c1 = m1 / tot * gprob
    c2 = m2 / tot * gprob

    oh1 = jnp.where(lanef == i1, 1.0, 0.0)
    oh2 = jnp.where(lanef == i2, 1.0, 0.0)
    ohs = oh1 + oh2
    rt = lax.broadcasted_iota(i32, (tm, tm), 0)
    ct = lax.broadcasted_iota(i32, (tm, tm), 1)
    lstrict = jnp.where(rt > ct, 1.0, 0.0).astype(bf16)
    cs = _dot(lstrict, ohs.astype(bf16)) + cnt_ref[...]
    rank1 = jnp.sum(cs * oh1, axis=-1, keepdims=True)
    rank2 = jnp.sum(cs * oh2, axis=-1, keepdims=True)
    cnt_ref[...] = cnt_ref[...] + jnp.sum(ohs, axis=0, keepdims=True)

    mi = jnp.where(lane == 0, i1, jnp.where(lane == 1, i2, jnp.where(lane == 2, rank1,
                                                                     jnp.where(lane == 3, rank2, 0.0))))
    mi_ref[...] = mi.astype(i32)
    mw_ref[...] = jnp.where(lane == 0, c1, jnp.where(lane == 1, c2, 0.0))


def _mix_route(x2d, o2d, y2d, proj2d, wa, wb, wo, n2_row, rwh, rwl, rb_row, cnt_in):
    n = x2d.shape[0]
    tm = min(256, n)
    assert n % tm == 0
    tok = lambda width: pl.BlockSpec((tm, width), lambda i: (i, 0))
    full = lambda a: pl.BlockSpec(a.shape, lambda i: (0,) * a.ndim)
    in_specs = [tok(D_MODEL), tok(QK_W), tok(SC_W),
                pl.BlockSpec((tm, D_MODEL), lambda i: (i, COL_GA // D_MODEL)),
                pl.BlockSpec((tm, D_MODEL), lambda i: (i, COL_GB // D_MODEL)),
                full(wa), full(wb), full(wo), full(n2_row), full(rwh), full(rwl), full(rb_row), full(cnt_in)]
    out_shape = (jax.ShapeDtypeStruct((n, D_MODEL), f32),
                 jax.ShapeDtypeStruct((n, D_MODEL), f32),
                 jax.ShapeDtypeStruct((n, LANE), i32),
                 jax.ShapeDtypeStruct((n, LANE), f32),
                 jax.ShapeDtypeStruct((1, LANE), f32))
    out_specs = (tok(D_MODEL), tok(D_MODEL), tok(LANE), tok(LANE),
                 pl.BlockSpec((1, LANE), lambda i: (0, 0)))
    return pl.pallas_call(
        _mix_route_kernel,
        out_shape=out_shape,
        grid=(n // tm,),
        in_specs=in_specs,
        out_specs=out_specs,
        compiler_params=_cparams(("arbitrary",)),
        name="mix_route",
    )(x2d, o2d, y2d, proj2d, proj2d, wa, wb, wo, n2_row, rwh, rwl, rb_row, cnt_in)


MI_W = 4
SUBLANE = 8
ROW_DMA_UNROLL = 8


def _dest_kernel(mi_ref, starts_ref, o_ref):
    mi = mi_ref[...]
    lane = lax.broadcasted_iota(i32, mi.shape, 1)
    st = starts_ref[...]

    def first_row(e_col):
        return jnp.sum(jnp.where(lane == e_col, st, 0.0), axis=-1, keepdims=True).astype(i32)

    d0 = first_row(mi[:, 0:1]) + mi[:, 2:3]
    d1 = first_row(mi[:, 1:2]) + mi[:, 3:4]
    o_ref[...] = jnp.where(lane == 0, d0 >> 3, jnp.where(lane == 1, d0 & (SUBLANE - 1),
                           jnp.where(lane == 2, d1 >> 3, jnp.where(lane == 3, d1 & (SUBLANE - 1), 0))))


def _dest_rows(mi, starts_row):
    n = mi.shape[0]
    tm = min(1024, n)
    assert n % tm == 0
    return pl.pallas_call(
        _dest_kernel,
        out_shape=jax.ShapeDtypeStruct((n, LANE), i32),
        grid=(n // tm,),
        in_specs=[pl.BlockSpec((tm, LANE), lambda i: (i, 0)), pl.BlockSpec((1, LANE), lambda i: (0, 0))],
        out_specs=pl.BlockSpec((tm, LANE), lambda i: (i, 0)),
        compiler_params=_cparams(("arbitrary",)),
        name="moe_dest",
    )(mi, starts_row)


def _dispatch_kernel(mi_ref, hp_ref, hs_ref, xs_ref, sem, *, np_tiles, tm):
    i = pl.program_id(0)

    def scatter_rows(h_ref):
        def start(t, c):
            for u in range(SUBLANE):
                rec = MI_W * (SUBLANE * t + u)
                for k in range(2):
                    dst = xs_ref.at[mi_ref[rec + 2 * k], pl.ds(mi_ref[rec + 2 * k + 1], 1)]
                    pltpu.make_async_copy(h_ref.at[t, pl.ds(u, 1)], dst, sem).start(priority=k)
            return c

        lax.fori_loop(0, tm // SUBLANE, start, 0)
        for k in range(2):
            pltpu.make_async_copy(h_ref, xs_ref.at[pl.ds(0, tm // SUBLANE)], sem).wait()

    @pl.when(i < np_tiles)
    def _():
        scatter_rows(hp_ref)

    @pl.when(i >= np_tiles)
    def _():
        scatter_rows(hs_ref)


def _dispatch(h2_p, h2_s, mi_flat):
    tm = MOE_ROWS
    n_p, n_s = h2_p.shape[0], h2_s.shape[0]
    assert n_p % tm == 0 and n_s == tm and tm % SUBLANE == 0
    np_tiles = n_p // tm
    tiled = lambda a: a.reshape(a.shape[0] // SUBLANE, SUBLANE, D_MODEL)
    blk = (tm // SUBLANE, SUBLANE, D_MODEL)
    return pl.pallas_call(
        functools.partial(_dispatch_kernel, np_tiles=np_tiles, tm=tm),
        out_shape=jax.ShapeDtypeStruct((2 * (n_p + n_s) // SUBLANE, SUBLANE, D_MODEL), f32),
        grid=(np_tiles + 1,),
        in_specs=[pl.BlockSpec((MI_W * tm,), lambda i: (i,), memory_space=pltpu.SMEM),
                  pl.BlockSpec(blk, lambda i: (jnp.minimum(i, np_tiles - 1), 0, 0)),
                  pl.BlockSpec(blk, lambda i: (0, 0, 0))],
        out_specs=pl.BlockSpec(memory_space=pl.ANY),
        scratch_shapes=[pltpu.SemaphoreType.DMA(())],
        compiler_params=_cparams(("arbitrary",)),
        name="moe_dispatch",
    )(mi_flat, tiled(h2_p), tiled(h2_s))


def _cast_rows(src_ref, dst_ref, col0=0, rows=256):
    width = src_ref.shape[1]

    def body(r, c):
        sl = pl.ds(pl.multiple_of(r * rows, rows), rows)
        dst_ref[sl, col0:col0 + width] = src_ref[sl, :].astype(bf16)
        return c
    lax.fori_loop(0, src_ref.shape[0] // rows, body, 0)


def _moe_kernel(blk_ref, lo_ref, hi_ref, first_ref, newe_ref, slot_ref, pre_ref, init_ref,
                x_ref, wg_hbm, wu_hbm, wd_hbm, o_ref,
                wg_f, wu_f, wd_f, wgu_b, wd_b, sem):
    i = pl.program_id(0)
    lo = lo_ref[i]
    hi = hi_ref[i]

    def weight_copies(e, slot):
        return [pltpu.make_async_copy(wg_hbm.at[e], wg_f.at[slot], sem.at[slot, 0]),
                pltpu.make_async_copy(wu_hbm.at[e], wu_f.at[slot], sem.at[slot, 1]),
                pltpu.make_async_copy(wd_hbm.at[e], wd_f.at[slot], sem.at[slot, 2])]

    def start_weights(e, slot):
        for cp, prio in zip(weight_copies(e, slot), (0, 1, 1)):
            cp.start(priority=prio)

    @pl.when(i == 0)
    def _():
        start_weights(init_ref[0], 0)

        @pl.when(init_ref[1] >= 0)
        def _():
            start_weights(init_ref[1], 1)

    @pl.when(newe_ref[i] == 1)
    def _():
        slot = slot_ref[i]
        cg, cu, cd = weight_copies(0, slot)
        cg.wait()
        _cast_rows(wg_f.at[slot], wgu_b, 0)
        cu.wait()
        _cast_rows(wu_f.at[slot], wgu_b, D_FF)
        cd.wait()
        _cast_rows(wd_f.at[slot], wd_b)

        @pl.when(pre_ref[i] >= 0)
        def _():
            start_weights(pre_ref[i], slot)

    @pl.when(hi > lo)
    def _():
        x = x_ref[...].astype(bf16)
        au = _dot(x, wgu_b[...])
        y = _dot((_silu(au[:, 0:D_FF]) * au[:, D_FF:2 * D_FF]).astype(bf16), wd_b[...])
        row = lax.broadcasted_iota(i32, y.shape, 0)
        ym = jnp.where((row >= lo) & (row < hi), y, 0.0)

        @pl.when(first_ref[i] == 1)
        def _():
            o_ref[...] = ym

        @pl.when(first_ref[i] == 0)
        def _():
            o_ref[...] = o_ref[...] + ym


def _moe(xs, w_gate, w_up, w_down, items):
    n_items = items[0].shape[0]
    rows = xs.shape[0]
    n_pref = len(items)
    xmap = lambda i, blk, *_: (blk[i], 0)
    grid_spec = pltpu.PrefetchScalarGridSpec(
        num_scalar_prefetch=n_pref,
        grid=(n_items,),
        in_specs=[pl.BlockSpec((MOE_ROWS, D_MODEL), xmap),
                  pl.BlockSpec(memory_space=pl.ANY),
                  pl.BlockSpec(memory_space=pl.ANY),
                  pl.BlockSpec(memory_space=pl.ANY)],
        out_specs=pl.BlockSpec((MOE_ROWS, D_MODEL), xmap),
        scratch_shapes=[pltpu.VMEM((2, D_MODEL, D_FF), f32), pltpu.VMEM((2, D_MODEL, D_FF), f32),
                        pltpu.VMEM((2, D_FF, D_MODEL), f32),
                        pltpu.VMEM((D_MODEL, 2 * D_FF), bf16), pltpu.VMEM((D_FF, D_MODEL), bf16),
                        pltpu.SemaphoreType.DMA((2, 3))],
    )
    return pl.pallas_call(
        _moe_kernel,
        out_shape=jax.ShapeDtypeStruct((rows, D_MODEL), f32),
        grid_spec=grid_spec,
        compiler_params=_cparams(("arbitrary",)),
        name="moe_experts",
    )(*items, xs, w_gate, w_up, w_down)


def _combine_kernel(mi_ref, mi_next_ref, x1p_ref, mwp_ref, x1s_ref, mws_ref, fnw_ref, ys_ref,
                    yp_ref, ysm_ref, g_ref, sem, *, np_tiles):
    i = pl.program_id(0)
    n = pl.num_programs(0)
    tm = x1p_ref.shape[0]
    slot = lax.rem(i, 2)

    def gather_rows(m_ref, dst_slot):
        def body(t, c):
            for u in range(SUBLANE):
                rec = MI_W * (SUBLANE * t + u)
                for k in range(2):
                    src = ys_ref.at[m_ref[rec + 2 * k], pl.ds(m_ref[rec + 2 * k + 1], 1)]
                    pltpu.make_async_copy(src, g_ref.at[dst_slot, k, t, pl.ds(u, 1)],
                                          sem.at[dst_slot]).start(priority=k)
            return c
        lax.fori_loop(0, tm // SUBLANE, body, 0)

    @pl.when(i == 0)
    def _():
        gather_rows(mi_ref, 0)

    @pl.when(i + 1 < n)
    def _():
        gather_rows(mi_next_ref, 1 - slot)

    for k in range(2):
        pltpu.make_async_copy(ys_ref.at[pl.ds(0, tm // SUBLANE)], g_ref.at[slot, k], sem.at[slot]).wait()

    def finish(x1_ref, mw_ref, out_ref):
        mw = mw_ref[...]
        g0 = g_ref[slot, 0].reshape(tm, D_MODEL)
        g1 = g_ref[slot, 1].reshape(tm, D_MODEL)
        x2 = x1_ref[...] + (g0 * mw[:, 0:1] + g1 * mw[:, 1:2])
        ms = jnp.mean(x2 * x2, axis=-1, keepdims=True)
        out_ref[...] = x2 * lax.rsqrt(ms + EPS) * fnw_ref[...]

    @pl.when(i < np_tiles)
    def _():
        finish(x1p_ref, mwp_ref, yp_ref)

    @pl.when(i >= np_tiles)
    def _():
        finish(x1s_ref, mws_ref, ysm_ref)


def _combine(x1_p, mw_p, x1_s, mw_s, fnw_row, ys3, mi_flat):
    tm = MOE_ROWS
    n_p, n_s = x1_p.shape[0], x1_s.shape[0]
    assert n_p % tm == 0 and n_s == tm
    np_tiles = n_p // tm
    ptile = lambda width: pl.BlockSpec((tm, width), lambda i: (jnp.minimum(i, np_tiles - 1), 0))
    stile = lambda width: pl.BlockSpec((tm, width), lambda i: (0, 0))
    return pl.pallas_call(
        functools.partial(_combine_kernel, np_tiles=np_tiles),
        out_shape=(jax.ShapeDtypeStruct((n_p, D_MODEL), f32),
                   jax.ShapeDtypeStruct((n_s, D_MODEL), f32)),
        grid=(np_tiles + 1,),
        in_specs=[pl.BlockSpec((MI_W * tm,), lambda i: (i,), memory_space=pltpu.SMEM),
                  pl.BlockSpec((MI_W * tm,), lambda i: (jnp.minimum(i + 1, np_tiles),), memory_space=pltpu.SMEM),
                  ptile(D_MODEL), ptile(LANE), stile(D_MODEL), stile(LANE),
                  pl.BlockSpec((1, D_MODEL), lambda i: (0, 0)),
                  pl.BlockSpec(memory_space=pl.ANY)],
        out_specs=(ptile(D_MODEL), stile(D_MODEL)),
        scratch_shapes=[pltpu.VMEM((2, 2, tm // SUBLANE, SUBLANE, D_MODEL), f32), pltpu.SemaphoreType.DMA((2,))],
        compiler_params=_cparams(("arbitrary",)),
        name="moe_combine",
    )(mi_flat, mi_flat, x1_p, mw_p, x1_s, mw_s, fnw_row, ys3)


PLAN_ROWS = 256
N_ITEM_FIELDS = 7


def _plan_kernel(cnt_ref, items_ref, rows_ref, *, nblk):
    cnt = cnt_ref[...]
    lane1 = lax.broadcasted_iota(i32, (1, LANE), 1)
    in_e = lane1 < N_EXPERTS
    ri = lax.broadcasted_iota(i32, (LANE, LANE), 0)
    ci = lax.broadcasted_iota(i32, (LANE, LANE), 1)
    upper = jnp.where(ri <= ci, 1.0, 0.0).astype(bf16)

    def cumsum_lanes(v):
        return _dot_lsplit(jnp.broadcast_to(v, (8, LANE)), upper)[0:1, :]

    shift = MOE_ROWS.bit_length() - 1
    ends = cumsum_lanes(cnt)
    starts = ends - cnt
    act = cnt > 0.0
    first_blk = (starts.astype(i32) >> shift).astype(f32)
    last_blk = (jnp.maximum(ends - 1.0, 0.0).astype(i32) >> shift).astype(f32)
    nvis = jnp.where(act, last_blk - first_blk + 1.0, 0.0)
    vis_end = cumsum_lanes(nvis)
    vis_start = vis_end - nvis
    total = jnp.max(vis_end, axis=-1, keepdims=True)
    cum_act = cumsum_lanes(jnp.where(act, 1.0, 0.0))
    n_uniq = jnp.max(cum_act, axis=-1, keepdims=True)

    p = PLAN_ROWS
    lane = lax.broadcasted_iota(i32, (p, LANE), 1)
    idx = lax.broadcasted_iota(i32, (p, LANE), 0).astype(f32)
    idx1 = idx[:, 0:1]
    count_le = lambda row, col: jnp.sum(jnp.where((row <= col) & in_e, 1.0, 0.0), axis=-1, keepdims=True)
    e = jnp.minimum(count_le(vis_end, idx), N_EXPERTS - 1.0)
    onehot = lane.astype(f32) == e
    look = lambda tbl: jnp.sum(jnp.where(onehot, tbl, 0.0), axis=-1, keepdims=True)
    blk = look(first_blk) + idx1 - look(vis_start)
    lo = jnp.maximum(look(starts), blk * MOE_ROWS) - blk * MOE_ROWS
    hi = jnp.minimum(look(ends), (blk + 1.0) * MOE_ROWS) - blk * MOE_ROWS
    valid = idx1 < total
    blk = jnp.where(valid, blk, nblk - 1.0)
    lo = jnp.where(valid, lo, 0.0)
    hi = jnp.where(valid, hi, 0.0)
    rep = lambda c: jnp.broadcast_to(c, (p, LANE))
    prev = lambda c: pltpu.roll(rep(c), 1, axis=0)[:, 0:1]
    is0 = idx1 == 0.0
    first = valid & (is0 | (blk != prev(blk)))
    newe = valid & (is0 | (e != prev(e)))
    rp = lax.broadcasted_iota(i32, (p, p), 0)
    cp = lax.broadcasted_iota(i32, (p, p), 1)
    lower = jnp.where(rp >= cp, 1.0, 0.0).astype(bf16)
    order = _dot(lower, rep(jnp.where(newe, 1.0, 0.0)).astype(bf16))[:, 0:1] - 1.0
    slot = jnp.where(newe, (order.astype(i32) & 1).astype(f32), 0.0)
    k2 = order + 2.0
    pre = jnp.where(newe & (k2 < n_uniq), count_le(cum_act, rep(k2)), -1.0)
    out = jnp.zeros((p, LANE), f32)
    for c, v in enumerate([blk, lo, hi, jnp.where(first, 1.0, 0.0), jnp.where(newe, 1.0, 0.0), slot, pre]):
        out = jnp.where(lane == c, v, out)
    items_ref[...] = out.astype(i32)

    u0 = count_le(cum_act, 0.0)
    u1 = jnp.where(n_uniq > 1.0, count_le(cum_act, 1.0), -1.0)
    rows_ref[...] = jnp.zeros(rows_ref.shape, f32)
    rows_ref[0:1, :] = starts
    rows_ref[1:2, :] = jnp.where(lane1 == 0, u0, jnp.where(lane1 == 1, u1, 0.0))


def _work_items(cnt_row, n_rows):
    nblk = n_rows // MOE_ROWS
    n_items = nblk + N_EXPERTS - 1
    assert n_items <= PLAN_ROWS and n_rows % MOE_ROWS == 0
    items, rows = pl.pallas_call(
        functools.partial(_plan_kernel, nblk=nblk),
        out_shape=(jax.ShapeDtypeStruct((PLAN_ROWS, LANE), i32), jax.ShapeDtypeStruct((8, LANE), f32)),
        compiler_params=pltpu.CompilerParams(vmem_limit_bytes=VMEM_LIMIT),
        name="moe_plan",
    )(cnt_row)
    fields = tuple(items[0:n_items, c] for c in range(N_ITEM_FIELDS))
    return rows[0:1, :], fields + (rows[1, 0:2].astype(i32),)


def kernel(x_prompt, x_sample, state_delta, state_qkv_conv, state_short_conv, norm1_w, w_in, conv_a_w, a_log, dt_bias, out_norm_w, w_branch_a, conv_b_w, w_branch_b, w_o, norm2_w, router_group_w, router_group_b, router_expert_w, router_expert_b, w_gate, w_up, w_down, final_norm_w):
    assert norm1_w.shape[0] == 1, "single-layer trunk"
    bp, tp, d = x_prompt.shape
    bs, ts, _ = x_sample.shape
    assert d == D_MODEL and ts == 1
    n_p = bp * tp
    n_s = bs
    n_all = n_p + n_s

    w_perm = _wprep(jnp.transpose(w_in[0]))
    wa = w_branch_a[0].astype(bf16)
    wb = w_branch_b[0].astype(bf16)
    wo = w_o[0].astype(bf16)
    pad = lambda v: jnp.zeros((1, BA_W), f32).at[0, N_HEADS:2 * N_HEADS].set(v)
    alog_row = pad(a_log[0])
    dtb_row = pad(dt_bias[0])
    onw_row = out_norm_w[0].reshape(1, HEAD)
    cwa = conv_a_w[0]
    cwb = conv_b_w[0]
    r_pad = LANE - N_EXPERTS - N_GROUPS
    rw = jnp.concatenate([router_expert_w[0], router_group_w[0], jnp.zeros((D_MODEL, r_pad), f32)], axis=1)
    rwh = rw.astype(bf16)
    rwl = (rw - rwh.astype(f32)).astype(bf16)
    rb_row = jnp.concatenate([router_expert_b[0], router_group_b[0], jnp.zeros((r_pad,), f32)]).reshape(1, LANE)
    n2_row = norm2_w[0].reshape(1, D_MODEL)

    xp2 = x_prompt.reshape(n_p, D_MODEL)
    proj_p, tails = _inproj_conv(xp2, norm1_w[0], w_perm, cwa, tp)
    tiles_per_seq = tails.shape[0] // bp
    nca_p = tails.reshape(bp, tiles_per_seq, 8, CONV_PAD_W)[:, -1, 8 - (CONV_A - 1):8, 0:QKV_W]
    o_p, y_p, sd_p, ncb_p = _delta_prompt(proj_p.reshape(bp, tp, PROJ_W), cwb, alog_row, dtb_row,
                                          onw_row, nb_step=4 if bp % 4 == 0 else (2 if bp % 2 == 0 else 1))
    cnt0 = jnp.zeros((1, LANE), f32)
    x1_p, h2_p, mi_p, mw_p, cnt_p = _mix_route(xp2, o_p.reshape(n_p, QK_W), y_p.reshape(n_p, SC_W), proj_p,
                                               wa, wb, wo, n2_row, rwh, rwl, rb_row, cnt0)

    xs2 = x_sample.reshape(n_s, D_MODEL)
    proj_s = _inproj(xs2, norm1_w[0], w_perm)
    bufa_t = jnp.transpose(state_qkv_conv[0], (1, 0, 2))
    bufb_t = jnp.transpose(state_short_conv[0], (1, 0, 2))
    q_s, k_s, v_s, beta_s, eg_s, y_s, nbufa_t, nbufb_t = _sample_prep(proj_s, bufa_t, bufb_t, cwa, cwb,
                                                                      alog_row, dtb_row)
    h3 = lambda a: a.reshape(n_s, N_HEADS, HEAD)
    z_s = proj_s[:, COL_Z:COL_Z + QK_W]
    sd_s, o_s = _sample_step(state_delta[0], h3(q_s), h3(k_s), h3(v_s), h3(beta_s), h3(eg_s), h3(z_s), onw_row)
    o_s2 = o_s.reshape(n_s, QK_W).astype(bf16)
    x1_s, h2_s, mi_s, mw_s, cnt = _mix_route(xs2, o_s2, y_s, proj_s, wa, wb, wo, n2_row, rwh, rwl, rb_row, cnt_p)

    starts_row, items = _work_items(cnt, 2 * n_all)
    mi_flat = jnp.concatenate([_dest_rows(mi_p, starts_row)[:, 0:MI_W], _dest_rows(mi_s, starts_row)[:, 0:MI_W]],
                              axis=0).reshape(MI_W * n_all)
    xs_sorted = _dispatch(h2_p, h2_s, mi_flat)
    ys = _moe(xs_sorted.reshape(2 * n_all, D_MODEL), w_gate[0], w_up[0], w_down[0], items)
    y_prompt, y_sample = _combine(x1_p, mw_p, x1_s, mw_s, final_norm_w.reshape(1, D_MODEL),
                                  ys.reshape(2 * n_all // SUBLANE, SUBLANE, D_MODEL), mi_flat)

    return (y_prompt.reshape(bp, tp, D_MODEL),
            y_sample.reshape(bs, ts, D_MODEL),
            sd_p[None],
            nca_p[None],
            ncb_p[None],
            sd_s[None],
            jnp.transpose(nbufa_t, (1, 0, 2))[None],
            jnp.transpose(nbufb_t, (1, 0, 2))[None])
```

```python
import functools

import jax
import jax.numpy as jnp
from jax import lax
from jax.experimental import pallas as pl
from jax.experimental.pallas import tpu as pltpu

f32 = jnp.float32
bf16 = jnp.bfloat16
i32 = jnp.int32

EPS = 1e-6
LANE = 128
D_MODEL = 2048
N_HEADS = 8
HEAD = 128
QK_W = N_HEADS * HEAD
QKV_W = 3 * QK_W
SC_W = 1024
CONV_A = 4
CONV_B = 3
CHUNK = 64
GROUP_HEADS = 4
N_EXPERTS = 64
N_GROUPS = 8
EXPERTS_PER_GROUP = 8
D_FF = 512
MOE_ROWS = 128
TOKEN_TILE = 128

COL_QKV = 0
COL_BCX = 3072
COL_GA = 6144
COL_GB = 8192
COL_Z = 10240
COL_BA = 11264
BA_W = 256
PROJ_W = 11520
PROJ_TN = 1280

VMEM_LIMIT = 56 * 1024 * 1024


def _dot(a, b):
    return jnp.dot(a, b, preferred_element_type=f32)


def _dot_nt(a, b):
    return lax.dot_general(a, b, (((1,), (1,)), ((), ())), preferred_element_type=f32)


def _split(x, n):
    parts = []
    r = x
    for i in range(n):
        p = r.astype(bf16)
        parts.append(p)
        if i + 1 < n:
            r = r - p.astype(f32)
    return parts


def _dot_lsplit(x, m, n=3):
    rows = x.shape[0]
    d = _dot(jnp.concatenate(_split(x, n), axis=0), m)
    acc = d[0:rows]
    for i in range(1, n):
        acc = acc + d[i * rows:(i + 1) * rows]
    return acc


def _dot_rsplit(m, x, n=3):
    cols = x.shape[1]
    d = _dot(m, jnp.concatenate(_split(x, n), axis=1))
    acc = d[:, 0:cols]
    for i in range(1, n):
        acc = acc + d[:, i * cols:(i + 1) * cols]
    return acc


def _silu(x):
    return x * jax.nn.sigmoid(x)


def _softplus(x):
    return jnp.maximum(x, 0.0) + jnp.log(1.0 + jnp.exp(-jnp.abs(x)))


def _cparams(sem):
    return pltpu.CompilerParams(dimension_semantics=sem, vmem_limit_bytes=VMEM_LIMIT)


def _inproj_kernel(x_ref, nw_ref, w_ref, o_ref, h_ref, *, rows):
    @pl.when(pl.program_id(1) == 0)
    def _():
        def body(r, c):
            sl = pl.ds(pl.multiple_of(r * rows, rows), rows)
            x = x_ref[sl, :]
            ms = jnp.mean(x * x, axis=-1, keepdims=True)
            h_ref[sl, :] = (x * lax.rsqrt(ms + EPS) * nw_ref[...]).astype(bf16)
            return c
        lax.fori_loop(0, x_ref.shape[0] // rows, body, 0)

    o_ref[...] = _dot_nt(h_ref[...], w_ref[...])


def _inproj(x2d, norm_w, w_bf16):
    n = x2d.shape[0]
    tm = min(1024, n)
    assert n % tm == 0 and PROJ_W % PROJ_TN == 0
    return pl.pallas_call(
        functools.partial(_inproj_kernel, rows=min(128, tm)),
        out_shape=jax.ShapeDtypeStruct((n, PROJ_W), f32),
        grid=(n // tm, PROJ_W // PROJ_TN),
        in_specs=[pl.BlockSpec((tm, D_MODEL), lambda i, j: (i, 0)),
                  pl.BlockSpec((1, D_MODEL), lambda i, j: (0, 0)),
                  pl.BlockSpec((PROJ_TN, D_MODEL), lambda i, j: (j, 0))],
        out_specs=pl.BlockSpec((tm, PROJ_TN), lambda i, j: (i, j)),
        scratch_shapes=[pltpu.VMEM((tm, D_MODEL), bf16)],
        compiler_params=_cparams(("arbitrary", "arbitrary")),
        name="inproj",
    )(x2d, norm_w.reshape(1, D_MODEL), w_bf16)


CONV_TILES = 3
CONV_COLS = 2 * HEAD
CONV_ROWS = 128
CONV_PAD_W = CONV_TILES * PROJ_TN


def _qkv_kind(col):
    return "q" if col < QK_W else "k" if col < 2 * QK_W else "v" if col < QKV_W else "raw"


def _inproj_conv_kernel(x_ref, nw_ref, w_ref, cw_ref, o_ref, tail_ref, h_ref, hist_ref, raw_ref, *,
                        rows, tiles_per_seq):
    i = pl.program_id(0)
    j = pl.program_id(1)
    tm = x_ref.shape[0]

    @pl.when(j == 0)
    def _():
        def body(r, c):
            sl = pl.ds(pl.multiple_of(r * rows, rows), rows)
            x = x_ref[sl, :]
            ms = jnp.mean(x * x, axis=-1, keepdims=True)
            h_ref[sl, :] = (x * lax.rsqrt(ms + EPS) * nw_ref[...]).astype(bf16)
            return c
        lax.fori_loop(0, tm // rows, body, 0)

    @pl.when((i == 0) & (j == 0))
    def _():
        hist_ref[...] = jnp.zeros(hist_ref.shape, f32)

    @pl.when(j >= CONV_TILES)
    def _():
        o_ref[...] = _dot_nt(h_ref[...], w_ref[...])

    seq_start = lax.rem(i, tiles_per_seq) == 0
    for jj in range(CONV_TILES):
        @pl.when(j == jj)
        def _():
            def matmul_chunk(idx, c0):
                raw_ref[idx % 2] = _dot_nt(h_ref[...], w_ref[c0:c0 + CONV_COLS, :])

            def conv_chunk(idx, c0):
                cs = slice(c0, c0 + CONV_COLS)
                raw = raw_ref.at[idx % 2]
                tail = raw[tm - 8:tm, :]
                tail_ref[0, :, cs] = tail
                kinds = [_qkv_kind(jj * PROJ_TN + c0 + g * HEAD) for g in range(CONV_COLS // HEAD)]
                if kinds[0] == "raw":
                    o_ref[:, cs] = raw[...]
                    return
                hist = jnp.where(seq_start, 0.0, hist_ref[jj, :, cs])
                for rc in range(tm // CONV_ROWS):
                    r0 = rc * CONV_ROWS
                    if rc > 0:
                        xe = raw[r0 - 8:r0 + CONV_ROWS, :]
                    else:
                        xe = jnp.concatenate([hist, raw[0:CONV_ROWS, :]], axis=0)
                    acc = pltpu.roll(xe, 3, axis=0)[8:] * cw_ref[0:1, cs]
                    acc = acc + pltpu.roll(xe, 2, axis=0)[8:] * cw_ref[1:2, cs]
                    acc = acc + pltpu.roll(xe, 1, axis=0)[8:] * cw_ref[2:3, cs]
                    acc = acc + xe[8:] * cw_ref[3:4, cs]
                    act = _silu(acc)
                    for g, kind in enumerate(kinds):
                        ah = act[:, g * HEAD:(g + 1) * HEAD]
                        if kind != "v":
                            ss = jnp.sum(ah * ah, axis=-1, keepdims=True)
                            inv = lax.rsqrt(ss + EPS)
                            ah = ah * (inv * (HEAD ** -0.5) if kind == "q" else inv)
                        o_ref[r0:r0 + CONV_ROWS, c0 + g * HEAD:c0 + (g + 1) * HEAD] = ah
                hist_ref[jj, :, cs] = tail

            chunks = list(range(0, PROJ_TN, CONV_COLS))
            matmul_chunk(0, chunks[0])
            for idx in range(1, len(chunks)):
                matmul_chunk(idx, chunks[idx])
                conv_chunk(idx - 1, chunks[idx - 1])
            conv_chunk(len(chunks) - 1, chunks[-1])


def _inproj_conv(x2d, norm_w, w_bf16, cwa, seq_len):
    n = x2d.shape[0]
    tm = min(1024, seq_len)
    assert n % tm == 0 and seq_len % tm == 0 and PROJ_W % PROJ_TN == 0 and tm % CONV_ROWS == 0
    assert QKV_W % CONV_COLS == 0 and PROJ_TN % CONV_COLS == 0
    cw_pad = jnp.zeros((CONV_A, CONV_PAD_W), f32).at[:, 0:QKV_W].set(cwa)
    last = CONV_TILES - 1
    return pl.pallas_call(
        functools.partial(_inproj_conv_kernel, rows=min(128, tm), tiles_per_seq=seq_len // tm),
        out_shape=(jax.ShapeDtypeStruct((n, PROJ_W), f32),
                   jax.ShapeDtypeStruct((n // tm, 8, CONV_PAD_W), f32)),
        grid=(n // tm, PROJ_W // PROJ_TN),
        in_specs=[pl.BlockSpec((tm, D_MODEL), lambda i, j: (i, 0)),
                  pl.BlockSpec((1, D_MODEL), lambda i, j: (0, 0)),
                  pl.BlockSpec((PROJ_TN, D_MODEL), lambda i, j: (j, 0)),
                  pl.BlockSpec((CONV_A, PROJ_TN), lambda i, j: (0, jnp.minimum(j, last)))],
        out_specs=(pl.BlockSpec((tm, PROJ_TN), lambda i, j: (i, j)),
                   pl.BlockSpec((1, 8, PROJ_TN), lambda i, j: (i, 0, jnp.minimum(j, last)))),
        scratch_shapes=[pltpu.VMEM((tm, D_MODEL), bf16), pltpu.VMEM((CONV_TILES, 8, PROJ_TN), f32),
                        pltpu.VMEM((2, tm, CONV_COLS), f32)],
        compiler_params=_cparams(("arbitrary", "arbitrary")),
        name="inproj_conv",
    )(x2d, norm_w.reshape(1, D_MODEL), w_bf16, cw_pad)


W_IN_COLS = 11280
WPREP_TN = 1024
WPREP_SHIFT = 16


def _wprep_kernel(a_ref, b_ref, o_ref):
    j = pl.program_id(0)
    keep = WPREP_TN - WPREP_SHIFT

    @pl.when((j < 3) | (j == 10))
    def _():
        o_ref[...] = a_ref[...].astype(bf16)

    @pl.when((j >= 3) & (j < 10))
    def _():
        o_ref[0:keep, :] = a_ref[WPREP_SHIFT:WPREP_TN, :].astype(bf16)
        o_ref[keep:WPREP_TN, :] = b_ref[...].astype(bf16)

    @pl.when(j == 11)
    def _():
        o_ref[0:WPREP_SHIFT, :] = a_ref[0:WPREP_SHIFT, :].astype(bf16)
        o_ref[WPREP_SHIFT:WPREP_TN, :] = jnp.zeros((keep, D_MODEL), bf16)


def _wprep(w_in_t):
    assert w_in_t.shape == (W_IN_COLS, D_MODEL) and 2 * N_HEADS == WPREP_SHIFT
    n_blk = pl.cdiv(PROJ_W, WPREP_TN)

    def a_map(j):
        return (jnp.where(j < 3, j, jnp.where(j < 10, j + 1, jnp.where(j == 10, 3, 4))), 0)

    def b_map(j):
        return (jnp.minimum((WPREP_TN // WPREP_SHIFT) * (j + 2), W_IN_COLS // WPREP_SHIFT - 1), 0)

    return pl.pallas_call(
        _wprep_kernel,
        out_shape=jax.ShapeDtypeStruct((PROJ_W, D_MODEL), bf16),
        grid=(n_blk,),
        in_specs=[pl.BlockSpec((WPREP_TN, D_MODEL), a_map),
                  pl.BlockSpec((WPREP_SHIFT, D_MODEL), b_map)],
        out_specs=pl.BlockSpec((WPREP_TN, D_MODEL), lambda j: (j, 0)),
        compiler_params=_cparams(("arbitrary",)),
        name="wprep",
    )(w_in_t, w_in_t)


def _head_l2norm(a, scale):
    outs = []
    for h in range(N_HEADS):
        ah = a[:, h * HEAD:(h + 1) * HEAD]
        ss = jnp.sum(ah * ah, axis=-1, keepdims=True)
        n = ah * lax.rsqrt(ss + EPS)
        outs.append(n * scale if scale != 1.0 else n)
    return outs


def _delta_prompt_kernel(qkv_ref, bcx_ref, z_ref, ba_ref, cwb_ref, alog_ref, dtb_ref, onw_ref,
                         e64_ref,
                         o_ref, y_ref, snew_ref, ncb_ref,
                         s_ref, xb_ref, *, nb_step):
    C = CHUNK
    G = GROUP_HEADS
    R = G * C
    t = pl.program_id(1)
    nt = pl.num_programs(1)

    @pl.when(t == 0)
    def _():
        s_ref[...] = jnp.zeros(s_ref.shape, f32)
        xb_ref[:, 0:8, :] = jnp.zeros((nb_step, 8, SC_W), f32)

    rr = lax.broadcasted_iota(i32, (R, R), 0)
    cc = lax.broadcasted_iota(i32, (R, R), 1)
    same_bf = jnp.where((rr >> 6) == (cc >> 6), 1.0, 0.0).astype(bf16)
    r2 = lax.broadcasted_iota(i32, (R, G * HEAD), 0)
    c2 = lax.broadcasted_iota(i32, (R, G * HEAD), 1)
    bdmask = (r2 >> 6) == (c2 >> 7)
    r3 = lax.broadcasted_iota(i32, (C, C), 0)
    c3 = lax.broadcasted_iota(i32, (C, C), 1)
    ltri = jnp.where(r3 >= c3, 1.0, 0.0).astype(bf16)
    r4 = lax.broadcasted_iota(i32, (C, R), 0)
    c4 = lax.broadcasted_iota(i32, (C, R), 1)
    ident_t = r4 == (c4 & (C - 1))
    incl_p = r4 >= (c4 & (C - 1))
    strict_p = r4 > (c4 & (C - 1))
    hblk = c4 >> 6
    ones8 = jnp.ones((8, C), bf16)

    nbs = range(nb_step)
    units = [(nb, g) for nb in nbs for g in range(N_HEADS // G)]
    heads = lambda g: range(g * G, (g + 1) * G)

    qn = [[qkv_ref[nb, :, h * HEAD:(h + 1) * HEAD] for h in range(N_HEADS)] for nb in nbs]
    kn = [[qkv_ref[nb, :, QK_W + h * HEAD:QK_W + (h + 1) * HEAD] for h in range(N_HEADS)] for nb in nbs]
    vv = [qkv_ref[nb, :, 2 * QK_W:3 * QK_W] for nb in nbs]

    bts = [ba_ref[nb, :, 0:LANE] for nb in nbs]
    beta_all = [jax.nn.sigmoid(bt) for bt in bts]
    g_all = [-(jnp.exp(alog_ref[:, 0:LANE]) * _softplus(bt + dtb_ref[:, 0:LANE])) for bt in bts]
    gc_small = [_dot_rsplit(ltri, ga) for ga in g_all]
    gl_small = [gc[C - 1:C, :] for gc in gc_small]

    k_st, q_st, kb, vb, kbg, qd, kd, gc_col = ({} for _ in range(8))
    for u in units:
        nb, g = u
        hs = heads(g)
        k_st[u] = jnp.concatenate([kn[nb][h] for h in hs], axis=0)
        q_st[u] = jnp.concatenate([qn[nb][h] for h in hs], axis=0)
        v_st = jnp.concatenate([vv[nb][:, h * HEAD:(h + 1) * HEAD] for h in hs], axis=0)
        beta_col = jnp.concatenate([beta_all[nb][:, h:h + 1] for h in hs], axis=0)
        gc_col[u] = jnp.concatenate([gc_small[nb][:, 8 + h:9 + h] for h in hs], axis=0)
        gl_col = jnp.concatenate(
            [jnp.broadcast_to(gl_small[nb][:, 8 + h:9 + h], (C, 1)) for h in hs], axis=0)
        kb[u] = k_st[u] * beta_col
        vb[u] = v_st * beta_col
        egc = jnp.exp(gc_col[u])
        kbg[u] = kb[u] * egc
        qd[u] = q_st[u] * egc
        kd[u] = k_st[u] * jnp.exp(gl_col - gc_col[u])

    gx = {u: _dot_lsplit(gc_small[u[0]], e64_ref[u[1], 0:LANE, :]) for u in units}
    crow = {u: _dot_rsplit(ones8, jnp.where(ident_t, gx[u], 0.0))[0:1, :] for u in units}
    a = {u: _dot_nt(jnp.concatenate([kb[u], q_st[u]], axis=0).astype(bf16), k_st[u].astype(bf16))
         for u in units}
    in_blk = [hblk == h for h in range(G - 1)]

    def pack(x):
        out = x[(G - 1) * C:G * C]
        for h in reversed(range(G - 1)):
            out = jnp.where(in_blk[h], x[h * C:(h + 1) * C], out)
        return out

    def expand(xp):
        return jnp.concatenate([xp.astype(bf16)] * G, axis=0) * same_bf

    dec = {u: jnp.where(incl_p, jnp.exp(jnp.where(incl_p, gx[u] - crow[u], 0.0)), 0.0) for u in units}
    nm = {u: jnp.where(strict_p, -(pack(a[u][0:R]) * dec[u]), 0.0) for u in units}
    qkm = {u: expand(pack(a[u][R:2 * R]) * dec[u]) for u in units}

    p = {u: jnp.where(ident_t, 1.0, 0.0) + nm[u] for u in units}
    nk = {u: _dot(nm[u].astype(bf16), expand(nm[u])) for u in units}
    for _ in range(4):
        for u in units:
            x = _dot(jnp.concatenate([p[u], nk[u]], axis=0).astype(bf16), expand(nk[u]))
            p[u] = p[u] + x[0:C]
            nk[u] = x[C:2 * C]
    for u in units:
        p[u] = p[u] + _dot(p[u].astype(bf16), expand(nk[u]))
    uw = {u: _dot(expand(p[u]), jnp.concatenate([vb[u], kbg[u]], axis=1).astype(bf16)) for u in units}

    ws = {}
    for u in units:
        nb, g = u
        for j, h in enumerate(heads(g)):
            sh = s_ref[nb, :, h * HEAD:(h + 1) * HEAD]
            lhs = jnp.concatenate([uw[u][j * C:(j + 1) * C, HEAD:2 * HEAD], qd[u][j * C:(j + 1) * C]], axis=0)
            ws[u, j] = _dot(lhs.astype(bf16), sh.astype(bf16))
    o_heads = {}
    for u in units:
        nb, g = u
        vnew_st = jnp.concatenate([uw[u][j * C:(j + 1) * C, 0:HEAD] - ws[u, j][0:C] for j in range(G)], axis=0)
        o_st = (jnp.concatenate([ws[u, j][C:2 * C] for j in range(G)], axis=0)
                + _dot(qkm[u], vnew_st.astype(bf16)))
        vbd = jnp.where(bdmask, jnp.concatenate([vnew_st] * G, axis=1), 0.0)
        lo = g * G * HEAD
        hi = lo + G * HEAD
        gl_row = jnp.concatenate(
            [jnp.broadcast_to(jnp.exp(gl_small[nb][:, 8 + h:9 + h]), (1, HEAD)) for h in heads(g)], axis=1)
        s_ref[nb, :, lo:hi] = s_ref[nb, :, lo:hi] * gl_row + _dot(kd[u].T.astype(bf16), vbd.astype(bf16))
        for j, h in enumerate(heads(g)):
            o_heads[nb, h] = o_st[j * C:(j + 1) * C]

    for nb in nbs:
        zt = z_ref[nb]
        for h in range(N_HEADS):
            oh = o_heads[nb, h]
            ms = jnp.mean(oh * oh, axis=-1, keepdims=True)
            zh = zt[:, h * HEAD:(h + 1) * HEAD]
            on = oh * lax.rsqrt(ms + EPS) * onw_ref[...] * _silu(zh)
            o_ref[nb, :, h * HEAD:(h + 1) * HEAD] = on.astype(bf16)

    for nb in nbs:
        bcx = bcx_ref[nb]
        cx = bcx[:, SC_W:2 * SC_W] * bcx[:, 2 * SC_W:3 * SC_W]
        xb_ref[nb, 8:8 + C, :] = cx
        ce = xb_ref[nb]
        cv = pltpu.roll(ce, 2, axis=0)[8:8 + C] * cwb_ref[0:1, :]
        cv = cv + pltpu.roll(ce, 1, axis=0)[8:8 + C] * cwb_ref[1:2, :]
        cv = cv + cx * cwb_ref[2:3, :]
        y_ref[nb] = (bcx[:, 0:SC_W] * cv).astype(bf16)
        xb_ref[nb, 0:8, :] = xb_ref[nb, C:C + 8, :]

    @pl.when(t == nt - 1)
    def _():
        for nb in range(nb_step):
            for h in range(N_HEADS):
                snew_ref[nb, h] = s_ref[nb, :, h * HEAD:(h + 1) * HEAD]
            ncb_ref[nb] = xb_ref[nb, 6:8, :]


def _expand_consts():
    lane = jnp.arange(BA_W)[:, None]
    col = jnp.arange(QK_W)[None, :]
    eb = (lane == (col >> 7)).astype(bf16)
    eg = (lane == (8 + (col >> 7))).astype(bf16)
    col64 = jnp.arange(GROUP_HEADS * CHUNK)[None, :]
    e64 = jnp.stack([(lane == (8 + g * GROUP_HEADS + (col64 >> 6))).astype(bf16)
                     for g in range(N_HEADS // GROUP_HEADS)], axis=0)
    return eb, eg, e64


def _delta_prompt(proj3, cwb, alog_row, dtb_row, onw_row, nb_step):
    b, t, _ = proj3.shape
    assert t % CHUNK == 0 and b % nb_step == 0
    _, _, e64 = _expand_consts()
    c = CHUNK
    const2 = lambda bi, ti: (0, 0)
    outs = pl.pallas_call(
        functools.partial(_delta_prompt_kernel, nb_step=nb_step),
        out_shape=(jax.ShapeDtypeStruct((b, t, QK_W), bf16),
                   jax.ShapeDtypeStruct((b, t, SC_W), bf16),
                   jax.ShapeDtypeStruct((b, N_HEADS, HEAD, HEAD), f32),
                   jax.ShapeDtypeStruct((b, CONV_B - 1, SC_W), f32)),
        grid=(b // nb_step, t // c),
        in_specs=[pl.BlockSpec((nb_step, c, QKV_W), lambda bi, ti: (bi, ti, COL_QKV // QKV_W)),
                  pl.BlockSpec((nb_step, c, QKV_W), lambda bi, ti: (bi, ti, COL_BCX // QKV_W)),
                  pl.BlockSpec((nb_step, c, QK_W), lambda bi, ti: (bi, ti, COL_Z // QK_W)),
                  pl.BlockSpec((nb_step, c, BA_W), lambda bi, ti: (bi, ti, COL_BA // BA_W)),
                  pl.BlockSpec((CONV_B, SC_W), const2),
                  pl.BlockSpec((1, BA_W), const2),
                  pl.BlockSpec((1, BA_W), const2),
                  pl.BlockSpec((1, HEAD), const2),
                  pl.BlockSpec((N_HEADS // GROUP_HEADS, BA_W, GROUP_HEADS * CHUNK), lambda bi, ti: (0, 0, 0))],
        out_specs=(pl.BlockSpec((nb_step, c, QK_W), lambda bi, ti: (bi, ti, 0)),
                   pl.BlockSpec((nb_step, c, SC_W), lambda bi, ti: (bi, ti, 0)),
                   pl.BlockSpec((nb_step, N_HEADS, HEAD, HEAD), lambda bi, ti: (bi, 0, 0, 0)),
                   pl.BlockSpec((nb_step, CONV_B - 1, SC_W), lambda bi, ti: (bi, 0, 0))),
        scratch_shapes=[pltpu.VMEM((nb_step, HEAD, QK_W), f32),
                        pltpu.VMEM((nb_step, 8 + c, SC_W), f32)],
        compiler_params=_cparams(("arbitrary", "arbitrary")),
        name="delta_prompt",
    )(proj3, proj3, proj3, proj3, cwb, alog_row, dtb_row, onw_row, e64)
    return outs


def _sample_prep_kernel(p_ref, bufa_ref, bufb_ref, cwa_ref, cwb_ref, alog_ref, dtb_ref, eb_ref, eg_ref,
                        q_ref, k_ref, v_ref, beta_ref, eg_out_ref, y_ref, nbufa_ref, nbufb_ref):
    def conv_sec(lo):
        hi = lo + QK_W
        raw = p_ref[:, COL_QKV + lo:COL_QKV + hi]
        acc = bufa_ref[0, :, lo:hi] * cwa_ref[0:1, lo:hi]
        acc = acc + bufa_ref[1, :, lo:hi] * cwa_ref[1:2, lo:hi]
        acc = acc + bufa_ref[2, :, lo:hi] * cwa_ref[2:3, lo:hi]
        acc = acc + raw * cwa_ref[3:4, lo:hi]
        nbufa_ref[0, :, lo:hi] = bufa_ref[1, :, lo:hi]
        nbufa_ref[1, :, lo:hi] = bufa_ref[2, :, lo:hi]
        nbufa_ref[2, :, lo:hi] = raw
        return _silu(acc)

    qn = _head_l2norm(conv_sec(0), HEAD ** -0.5)
    kn = _head_l2norm(conv_sec(QK_W), 1.0)
    for h in range(N_HEADS):
        q_ref[:, h * HEAD:(h + 1) * HEAD] = qn[h]
        k_ref[:, h * HEAD:(h + 1) * HEAD] = kn[h]
    v_ref[...] = conv_sec(2 * QK_W)

    bt = p_ref[:, COL_BA:COL_BA + BA_W]
    beta_all = jax.nn.sigmoid(bt)
    g_all = -(jnp.exp(alog_ref[...]) * _softplus(bt + dtb_ref[...]))
    beta_ref[...] = _dot_lsplit(beta_all, eb_ref[...])
    eg_out_ref[...] = jnp.exp(_dot_lsplit(g_all, eg_ref[...]))

    bg = p_ref[:, COL_BCX:COL_BCX + SC_W]
    cx = p_ref[:, COL_BCX + SC_W:COL_BCX + 2 * SC_W] * p_ref[:, COL_BCX + 2 * SC_W:COL_BCX + 3 * SC_W]
    cv = bufb_ref[0] * cwb_ref[0:1, :]
    cv = cv + bufb_ref[1] * cwb_ref[1:2, :]
    cv = cv + cx * cwb_ref[2:3, :]
    y_ref[...] = (bg * cv).astype(bf16)
    nbufb_ref[0] = bufb_ref[1]
    nbufb_ref[1] = cx


def _sample_prep(proj_s, bufa_t, bufb_t, cwa, cwb, alog_row, dtb_row):
    n = proj_s.shape[0]
    eb, eg, _ = _expand_consts()
    row = jax.ShapeDtypeStruct((n, QK_W), f32)
    return pl.pallas_call(
        _sample_prep_kernel,
        out_shape=(row, row, row, row, row,
                   jax.ShapeDtypeStruct((n, SC_W), bf16),
                   jax.ShapeDtypeStruct((CONV_A - 1, n, QKV_W), f32),
                   jax.ShapeDtypeStruct((CONV_B - 1, n, SC_W), f32)),
        compiler_params=pltpu.CompilerParams(vmem_limit_bytes=VMEM_LIMIT),
        name="sample_prep",
    )(proj_s, bufa_t, bufb_t, cwa, cwb, alog_row, dtb_row, eb, eg)


def _sample_step_kernel(s_ref, q_ref, k_ref, v_ref, beta_ref, eg_ref, z_ref, onw_ref,
                        snew_ref, o_ref, *, bb):
    w = N_HEADS * HEAD
    r8 = lax.broadcasted_iota(i32, (N_HEADS, w), 0)
    c8 = lax.broadcasted_iota(i32, (N_HEADS, w), 1)
    mask8 = r8 == (c8 >> 7)
    zpad_k = jnp.zeros((HEAD - N_HEADS, HEAD), f32)
    hb = lambda h: slice(h * HEAD, (h + 1) * HEAD)
    bs = range(bb)
    s_dec, k8s, kts = [], [], []
    for b in bs:
        s_all = jnp.concatenate([s_ref[b, h] for h in range(N_HEADS)], axis=1)
        eg8 = eg_ref[b]
        eg_row = jnp.concatenate([eg8[h:h + 1, :] for h in range(N_HEADS)], axis=1)
        s_dec.append(s_all * eg_row)
        k8s.append(k_ref[b])
        kts.append(jnp.concatenate([k8s[b], zpad_k], axis=0).T)
    xs = [_dot(k8s[b].astype(bf16), s_dec[b].astype(bf16)) for b in bs]
    s_new = []
    for b in bs:
        vb8, bt8 = v_ref[b], beta_ref[b]
        upd = [kts[b][:, h:h + 1] * ((vb8[h:h + 1, :] - xs[b][h:h + 1, hb(h)]) * bt8[h:h + 1, :])
               for h in range(N_HEADS)]
        s_new.append(s_dec[b] + jnp.concatenate(upd, axis=1))
    ys = [_dot(q_ref[b].astype(bf16), s_new[b].astype(bf16)) for b in bs]
    for b in bs:
        yv = jnp.where(mask8, ys[b], 0.0)
        o8 = yv[:, 0:HEAD]
        for j in range(1, N_HEADS):
            o8 = o8 + yv[:, j * HEAD:(j + 1) * HEAD]
        ms = jnp.mean(o8 * o8, axis=-1, keepdims=True)
        o_ref[b] = o8 * lax.rsqrt(ms + EPS) * onw_ref[...] * _silu(z_ref[b])
        for h in range(N_HEADS):
            snew_ref[b, h] = s_new[b][:, hb(h)]


def _sample_step(state, q, k, v, beta, eg, z, onw_row, bb=8):
    n = state.shape[0]
    assert n % bb == 0
    hspec = pl.BlockSpec((bb, N_HEADS, HEAD), lambda i: (i, 0, 0))
    sspec = pl.BlockSpec((bb, N_HEADS, HEAD, HEAD), lambda i: (i, 0, 0, 0))
    return pl.pallas_call(
        functools.partial(_sample_step_kernel, bb=bb),
        out_shape=(jax.ShapeDtypeStruct(state.shape, f32),
                   jax.ShapeDtypeStruct((n, N_HEADS, HEAD), f32)),
        grid=(n // bb,),
        in_specs=[sspec, hspec, hspec, hspec, hspec, hspec, hspec, pl.BlockSpec((1, HEAD), lambda i: (0, 0))],
        out_specs=(sspec, hspec),
        compiler_params=_cparams(("arbitrary",)),
        name="sample_step",
    )(state, q, k, v, beta, eg, z, onw_row)


def _mix_route_kernel(x_ref, o_ref, y_ref, ga_ref, gb_ref, wa_ref, wb_ref, wo_ref, n2_ref,
                      rwh_ref, rwl_ref, rb_ref, cnt_in_ref, x1_ref, h2_ref, mi_ref, mw_ref, cnt_ref):
    i = pl.program_id(0)
    tm = x_ref.shape[0]

    @pl.when(i == 0)
    def _():
        cnt_ref[...] = cnt_in_ref[...]

    oa = _dot(o_ref[...], wa_ref[...])
    ob = _dot(y_ref[...], wb_ref[...])
    merged = jax.nn.sigmoid(ga_ref[...]) * oa + jax.nn.sigmoid(gb_ref[...]) * ob
    x1 = x_ref[...] + _dot(merged.astype(bf16), wo_ref[...])
    x1_ref[...] = x1
    ms = jnp.mean(x1 * x1, axis=-1, keepdims=True)
    h2 = x1 * lax.rsqrt(ms + EPS) * n2_ref[...]
    h2_ref[...] = h2

    h_hi, h_lo = _split(h2, 2)
    logits = _dot(h_hi, rwh_ref[...]) + _dot(h_hi, rwl_ref[...]) + _dot(h_lo, rwh_ref[...]) + rb_ref[...]

    lane = lax.broadcasted_iota(i32, (tm, LANE), 1)
    lanef = lane.astype(f32)
    neg = jnp.float32(-jnp.inf)
    big = jnp.float32(1e9)
    gmask = (lane >= N_EXPERTS) & (lane < N_EXPERTS + N_GROUPS)
    gl = jnp.where(gmask, logits, neg)
    gmax = jnp.max(gl, axis=-1, keepdims=True)
    gidx = jnp.min(jnp.where(gl == gmax, lanef - N_EXPERTS, big), axis=-1, keepdims=True)
    gsum = jnp.sum(jnp.where(gmask, jnp.exp(gl - gmax), 0.0), axis=-1, keepdims=True)
    gprob = 1.0 / gsum

    emask = (lane < N_EXPERTS) & ((lane >> 3).astype(f32) == gidx)
    el = jnp.where(emask, logits, neg)
    emax = jnp.max(el, axis=-1, keepdims=True)
    pe = jnp.where(emask, jnp.exp(el - emax), 0.0)
    eprob = pe / jnp.sum(pe, axis=-1, keepdims=True)
    p1m = jnp.where(emask, eprob, -1.0)
    m1 = jnp.max(p1m, axis=-1, keepdims=True)
    i1 = jnp.min(jnp.where(p1m == m1, lanef, big), axis=-1, keepdims=True)
    p2m = jnp.where(lanef == i1, -1.0, p1m)
    m2 = jnp.max(p2m, axis=-1, keepdims=True)
    i2 = jnp.min(jnp.where(p2m == m2, lanef, big), axis=-1, keepdims=True)
    tot = m1 + m2
    c1 = m1 / tot * gprob
    c2 = m2 / tot * gprob

    oh1 = jnp.where(lanef == i1, 1.0, 0.0)
    oh2 = jnp.where(lanef == i2, 1.0, 0.0)
    ohs = oh1 + oh2
    rt = lax.broadcasted_iota(i32, (tm, tm), 0)
    ct = lax.broadcasted_iota(i32, (tm, tm), 1)
    lstrict = jnp.where(rt > ct, 1.0, 0.0).astype(bf16)
    cs = _dot(lstrict, ohs.astype(bf16)) + cnt_ref[...]
    rank1 = jnp.sum(cs * oh1, axis=-1, keepdims=True)
    rank2 = jnp.sum(cs * oh2, axis=-1, keepdims=True)
    cnt_ref[...] = cnt_ref[...] + jnp.sum(ohs, axis=0, keepdims=True)

    mi = jnp.where(lane == 0, i1, jnp.where(lane == 1, i2, jnp.where(lane == 2, rank1,
                                                                     jnp.where(lane == 3, rank2, 0.0))))
    mi_ref[...] = mi.astype(i32)
    mw_ref[...] = jnp.where(lane == 0, c1, jnp.where(lane == 1, c2, 0.0))


def _mix_route(x2d, o2d, y2d, proj2d, wa, wb, wo, n2_row, rwh, rwl, rb_row, cnt_in):
    n = x2d.shape[0]
    tm = min(256, n)
    assert n % tm == 0
    tok = lambda width: pl.BlockSpec((tm, width), lambda i: (i, 0))
    full = lambda a: pl.BlockSpec(a.shape, lambda i: (0,) * a.ndim)
    in_specs = [tok(D_MODEL), tok(QK_W), tok(SC_W),
                pl.BlockSpec((tm, D_MODEL), lambda i: (i, COL_GA // D_MODEL)),
                pl.BlockSpec((tm, D_MODEL), lambda i: (i, COL_GB // D_MODEL)),
                full(wa), full(wb), full(wo), full(n2_row), full(rwh), full(rwl), full(rb_row), full(cnt_in)]
    out_shape = (jax.ShapeDtypeStruct((n, D_MODEL), f32),
                 jax.ShapeDtypeStruct((n, D_MODEL), f32),
                 jax.ShapeDtypeStruct((n, LANE), i32),
                 jax.ShapeDtypeStruct((n, LANE), f32),
                 jax.ShapeDtypeStruct((1, LANE), f32))
    out_specs = (tok(D_MODEL), tok(D_MODEL), tok(LANE), tok(LANE),
                 pl.BlockSpec((1, LANE), lambda i: (0, 0)))
    return pl.pallas_call(
        _mix_route_kernel,
        out_shape=out_shape,
        grid=(n // tm,),
        in_specs=in_specs,
        out_specs=out_specs,
        compiler_params=_cparams(("arbitrary",)),
        name="mix_route",
    )(x2d, o2d, y2d, proj2d, proj2d, wa, wb, wo, n2_row, rwh, rwl, rb_row, cnt_in)


MI_W = 4
SUBLANE = 8
ROW_DMA_UNROLL = 8


def _dest_kernel(mi_ref, starts_ref, o_ref):
    mi = mi_ref[...]
    lane = lax.broadcasted_iota(i32, mi.shape, 1)
    st = starts_ref[...]

    def first_row(e_col):
        return jnp.sum(jnp.where(lane == e_col, st, 0.0), axis=-1, keepdims=True).astype(i32)

    d0 = first_row(mi[:, 0:1]) + mi[:, 2:3]
    d1 = first_row(mi[:, 1:2]) + mi[:, 3:4]
    o_ref[...] = jnp.where(lane == 0, d0 >> 3, jnp.where(lane == 1, d0 & (SUBLANE - 1),
                           jnp.where(lane == 2, d1 >> 3, jnp.where(lane == 3, d1 & (SUBLANE - 1), 0))))


def _dest_rows(mi, starts_row):
    n = mi.shape[0]
    tm = min(1024, n)
    assert n % tm == 0
    return pl.pallas_call(
        _dest_kernel,
        out_shape=jax.ShapeDtypeStruct((n, LANE), i32),
        grid=(n // tm,),
        in_specs=[pl.BlockSpec((tm, LANE), lambda i: (i, 0)), pl.BlockSpec((1, LANE), lambda i: (0, 0))],
        out_specs=pl.BlockSpec((tm, LANE), lambda i: (i, 0)),
        compiler_params=_cparams(("arbitrary",)),
        name="moe_dest",
    )(mi, starts_row)


def _dispatch_kernel(mi_ref, hp_ref, hs_ref, xs_ref, sem, *, np_tiles, tm):
    i = pl.program_id(0)

    def scatter_rows(h_ref):
        def start(t, c):
            for u in range(SUBLANE):
                rec = MI_W * (SUBLANE * t + u)
                for k in range(2):
                    dst = xs_ref.at[mi_ref[rec + 2 * k], pl.ds(mi_ref[rec + 2 * k + 1], 1)]
                    pltpu.make_async_copy(h_ref.at[t, pl.ds(u, 1)], dst, sem).start(priority=k)
            return c

        lax.fori_loop(0, tm // SUBLANE, start, 0)
        for k in range(2):
            pltpu.make_async_copy(h_ref, xs_ref.at[pl.ds(0, tm // SUBLANE)], sem).wait()

    @pl.when(i < np_tiles)
    def _():
        scatter_rows(hp_ref)

    @pl.when(i >= np_tiles)
    def _():
        scatter_rows(hs_ref)


def _dispatch(h2_p, h2_s, mi_flat):
    tm = TOKEN_TILE
    n_p, n_s = h2_p.shape[0], h2_s.shape[0]
    assert n_p % tm == 0 and n_s == tm and tm % SUBLANE == 0
    np_tiles = n_p // tm
    tiled = lambda a: a.reshape(a.shape[0] // SUBLANE, SUBLANE, D_MODEL)
    blk = (tm // SUBLANE, SUBLANE, D_MODEL)
    return pl.pallas_call(
        functools.partial(_dispatch_kernel, np_tiles=np_tiles, tm=tm),
        out_shape=jax.ShapeDtypeStruct((2 * (n_p + n_s) // SUBLANE, SUBLANE, D_MODEL), f32),
        grid=(np_tiles + 1,),
        in_specs=[pl.BlockSpec((MI_W * tm,), lambda i: (i,), memory_space=pltpu.SMEM),
                  pl.BlockSpec(blk, lambda i: (jnp.minimum(i, np_tiles - 1), 0, 0)),
                  pl.BlockSpec(blk, lambda i: (0, 0, 0))],
        out_specs=pl.BlockSpec(memory_space=pl.ANY),
        scratch_shapes=[pltpu.SemaphoreType.DMA(())],
        compiler_params=_cparams(("arbitrary",)),
        name="moe_dispatch",
    )(mi_flat, tiled(h2_p), tiled(h2_s))


def _cast_rows(src_ref, dst_ref, col0=0, rows=256):
    width = src_ref.shape[1]

    def body(r, c):
        sl = pl.ds(pl.multiple_of(r * rows, rows), rows)
        dst_ref[sl, col0:col0 + width] = src_ref[sl, :].astype(bf16)
        return c
    lax.fori_loop(0, src_ref.shape[0] // rows, body, 0)


def _moe_kernel(blk_ref, lo_ref, hi_ref, first_ref, newe_ref, slot_ref, pre_ref, init_ref,
                x_ref, wg_hbm, wu_hbm, wd_hbm, o_ref,
                wg_f, wu_f, wd_f, wgu_b, wd_b, sem):
    i = pl.program_id(0)
    lo = lo_ref[i]
    hi = hi_ref[i]

    def weight_copies(e, slot):
        return [pltpu.make_async_copy(wg_hbm.at[e], wg_f.at[slot], sem.at[slot, 0]),
                pltpu.make_async_copy(wu_hbm.at[e], wu_f.at[slot], sem.at[slot, 1]),
                pltpu.make_async_copy(wd_hbm.at[e], wd_f.at[slot], sem.at[slot, 2])]

    def start_weights(e, slot):
        for cp, prio in zip(weight_copies(e, slot), (0, 1, 1)):
            cp.start(priority=prio)

    @pl.when(i == 0)
    def _():
        start_weights(init_ref[0], 0)

        @pl.when(init_ref[1] >= 0)
        def _():
            start_weights(init_ref[1], 1)

    @pl.when(newe_ref[i] == 1)
    def _():
        slot = slot_ref[i]
        cg, cu, cd = weight_copies(0, slot)
        cg.wait()
        _cast_rows(wg_f.at[slot], wgu_b, 0)
        cu.wait()
        _cast_rows(wu_f.at[slot], wgu_b, D_FF)
        cd.wait()
        _cast_rows(wd_f.at[slot], wd_b)

        @pl.when(pre_ref[i] >= 0)
        def _():
            start_weights(pre_ref[i], slot)

    @pl.when(hi > lo)
    def _():
        x = x_ref[...].astype(bf16)
        au = _dot(x, wgu_b[...])
        y = _dot((_silu(au[:, 0:D_FF]) * au[:, D_FF:2 * D_FF]).astype(bf16), wd_b[...])
        row = lax.broadcasted_iota(i32, y.shape, 0)
        ym = jnp.where((row >= lo) & (row < hi), y, 0.0)

        @pl.when(first_ref[i] == 1)
        def _():
            o_ref[...] = ym

        @pl.when(first_ref[i] == 0)
        def _():
            o_ref[...] = o_ref[...] + ym


def _moe(xs, w_gate, w_up, w_down, items):
    n_items = items[0].shape[0]
    rows = xs.shape[0]
    n_pref = len(items)
    xmap = lambda i, blk, *_: (blk[i], 0)
    grid_spec = pltpu.PrefetchScalarGridSpec(
        num_scalar_prefetch=n_pref,
        grid=(n_items,),
        in_specs=[pl.BlockSpec((MOE_ROWS, D_MODEL), xmap),
                  pl.BlockSpec(memory_space=pl.ANY),
                  pl.BlockSpec(memory_space=pl.ANY),
                  pl.BlockSpec(memory_space=pl.ANY)],
        out_specs=pl.BlockSpec((MOE_ROWS, D_MODEL), xmap),
        scratch_shapes=[pltpu.VMEM((2, D_MODEL, D_FF), f32), pltpu.VMEM((2, D_MODEL, D_FF), f32),
                        pltpu.VMEM((2, D_FF, D_MODEL), f32),
                        pltpu.VMEM((D_MODEL, 2 * D_FF), bf16), pltpu.VMEM((D_FF, D_MODEL), bf16),
                        pltpu.SemaphoreType.DMA((2, 3))],
    )
    return pl.pallas_call(
        _moe_kernel,
        out_shape=jax.ShapeDtypeStruct((rows, D_MODEL), f32),
        grid_spec=grid_spec,
        compiler_params=_cparams(("arbitrary",)),
        name="moe_experts",
    )(*items, xs, w_gate, w_up, w_down)


def _combine_kernel(mi_ref, mi_next_ref, x1p_ref, mwp_ref, x1s_ref, mws_ref, fnw_ref, ys_ref,
                    yp_ref, ysm_ref, g_ref, sem, *, np_tiles):
    i = pl.program_id(0)
    n = pl.num_programs(0)
    tm = x1p_ref.shape[0]
    slot = lax.rem(i, 2)

    def gather_rows(m_ref, dst_slot):
        def body(t, c):
            for u in range(SUBLANE):
                rec = MI_W * (SUBLANE * t + u)
                for k in range(2):
                    src = ys_ref.at[m_ref[rec + 2 * k], pl.ds(m_ref[rec + 2 * k + 1], 1)]
                    pltpu.make_async_copy(src, g_ref.at[dst_slot, k, t, pl.ds(u, 1)],
                                          sem.at[dst_slot]).start(priority=k)
            return c
        lax.fori_loop(0, tm // SUBLANE, body, 0)

    @pl.when(i == 0)
    def _():
        gather_rows(mi_ref, 0)

    @pl.when(i + 1 < n)
    def _():
        gather_rows(mi_next_ref, 1 - slot)

    for k in range(2):
        pltpu.make_async_copy(ys_ref.at[pl.ds(0, tm // SUBLANE)], g_ref.at[slot, k], sem.at[slot]).wait()

    def finish(x1_ref, mw_ref, out_ref):
        mw = mw_ref[...]
        g0 = g_ref[slot, 0].reshape(tm, D_MODEL)
        g1 = g_ref[slot, 1].reshape(tm, D_MODEL)
        x2 = x1_ref[...] + (g0 * mw[:, 0:1] + g1 * mw[:, 1:2])
        ms = jnp.mean(x2 * x2, axis=-1, keepdims=True)
        out_ref[...] = x2 * lax.rsqrt(ms + EPS) * fnw_ref[...]

    @pl.when(i < np_tiles)
    def _():
        finish(x1p_ref, mwp_ref, yp_ref)

    @pl.when(i >= np_tiles)
    def _():
        finish(x1s_ref, mws_ref, ysm_ref)


def _combine(x1_p, mw_p, x1_s, mw_s, fnw_row, ys3, mi_flat):
    tm = TOKEN_TILE
    n_p, n_s = x1_p.shape[0], x1_s.shape[0]
    assert n_p % tm == 0 and n_s == tm
    np_tiles = n_p // tm
    ptile = lambda width: pl.BlockSpec((tm, width), lambda i: (jnp.minimum(i, np_tiles - 1), 0))
    stile = lambda width: pl.BlockSpec((tm, width), lambda i: (0, 0))
    return pl.pallas_call(
        functools.partial(_combine_kernel, np_tiles=np_tiles),
        out_shape=(jax.ShapeDtypeStruct((n_p, D_MODEL), f32),
                   jax.ShapeDtypeStruct((n_s, D_MODEL), f32)),
        grid=(np_tiles + 1,),
        in_specs=[pl.BlockSpec((MI_W * tm,), lambda i: (i,), memory_space=pltpu.SMEM),
                  pl.BlockSpec((MI_W * tm,), lambda i: (jnp.minimum(i + 1, np_tiles),), memory_space=pltpu.SMEM),
                  ptile(D_MODEL), ptile(LANE), stile(D_MODEL), stile(LANE),
                  pl.BlockSpec((1, D_MODEL), lambda i: (0, 0)),
                  pl.BlockSpec(memory_space=pl.ANY)],
        out_specs=(ptile(D_MODEL), stile(D_MODEL)),
        scratch_shapes=[pltpu.VMEM((2, 2, tm // SUBLANE, SUBLANE, D_MODEL), f32), pltpu.SemaphoreType.DMA((2,))],
        compiler_params=_cparams(("arbitrary",)),
        name="moe_combine",
    )(mi_flat, mi_flat, x1_p, mw_p, x1_s, mw_s, fnw_row, ys3)


PLAN_ROWS = 256
N_ITEM_FIELDS = 7


def _plan_kernel(cnt_ref, items_ref, rows_ref, *, nblk):
    cnt = cnt_ref[...]
    lane1 = lax.broadcasted_iota(i32, (1, LANE), 1)
    in_e = lane1 < N_EXPERTS
    ri = lax.broadcasted_iota(i32, (LANE, LANE), 0)
    ci = lax.broadcasted_iota(i32, (LANE, LANE), 1)
    upper = jnp.where(ri <= ci, 1.0, 0.0).astype(bf16)

    def cumsum_lanes(v):
        return _dot_lsplit(jnp.broadcast_to(v, (8, LANE)), upper)[0:1, :]

    shift = MOE_ROWS.bit_length() - 1
    ends = cumsum_lanes(cnt)
    starts = ends - cnt
    act = cnt > 0.0
    first_blk = (starts.astype(i32) >> shift).astype(f32)
    last_blk = (jnp.maximum(ends - 1.0, 0.0).astype(i32) >> shift).astype(f32)
    nvis = jnp.where(act, last_blk - first_blk + 1.0, 0.0)
    vis_end = cumsum_lanes(nvis)
    vis_start = vis_end - nvis
    total = jnp.max(vis_end, axis=-1, keepdims=True)
    cum_act = cumsum_lanes(jnp.where(act, 1.0, 0.0))
    n_uniq = jnp.max(cum_act, axis=-1, keepdims=True)

    p = PLAN_ROWS
    lane = lax.broadcasted_iota(i32, (p, LANE), 1)
    idx = lax.broadcasted_iota(i32, (p, LANE), 0).astype(f32)
    idx1 = idx[:, 0:1]
    count_le = lambda row, col: jnp.sum(jnp.where((row <= col) & in_e, 1.0, 0.0), axis=-1, keepdims=True)
    e = jnp.minimum(count_le(vis_end, idx), N_EXPERTS - 1.0)
    onehot = lane.astype(f32) == e
    look = lambda tbl: jnp.sum(jnp.where(onehot, tbl, 0.0), axis=-1, keepdims=True)
    blk = look(first_blk) + idx1 - look(vis_start)
    lo = jnp.maximum(look(starts), blk * MOE_ROWS) - blk * MOE_ROWS
    hi = jnp.minimum(look(ends), (blk + 1.0) * MOE_ROWS) - blk * MOE_ROWS
    valid = idx1 < total
    blk = jnp.where(valid, blk, nblk - 1.0)
    lo = jnp.where(valid, lo, 0.0)
    hi = jnp.where(valid, hi, 0.0)
    rep = lambda c: jnp.broadcast_to(c, (p, LANE))
    prev = lambda c: pltpu.roll(rep(c), 1, axis=0)[:, 0:1]
    is0 = idx1 == 0.0
    first = valid & (is0 | (blk != prev(blk)))
    newe = valid & (is0 | (e != prev(e)))
    rp = lax.broadcasted_iota(i32, (p, p), 0)
    cp = lax.broadcasted_iota(i32, (p, p), 1)
    lower = jnp.where(rp >= cp, 1.0, 0.0).astype(bf16)
    order = _dot(lower, rep(jnp.where(newe, 1.0, 0.0)).astype(bf16))[:, 0:1] - 1.0
    slot = jnp.where(newe, (order.astype(i32) & 1).astype(f32), 0.0)
    k2 = order + 2.0
    pre = jnp.where(newe & (k2 < n_uniq), count_le(cum_act, rep(k2)), -1.0)
    out = jnp.zeros((p, LANE), f32)
    for c, v in enumerate([blk, lo, hi, jnp.where(first, 1.0, 0.0), jnp.where(newe, 1.0, 0.0), slot, pre]):
        out = jnp.where(lane == c, v, out)
    items_ref[...] = out.astype(i32)

    u0 = count_le(cum_act, 0.0)
    u1 = jnp.where(n_uniq > 1.0, count_le(cum_act, 1.0), -1.0)
    rows_ref[...] = jnp.zeros(rows_ref.shape, f32)
    rows_ref[0:1, :] = starts
    rows_ref[1:2, :] = jnp.where(lane1 == 0, u0, jnp.where(lane1 == 1, u1, 0.0))


def _work_items(cnt_row, n_rows):
    nblk = n_rows // MOE_ROWS
    n_items = nblk + N_EXPERTS - 1
    assert n_items <= PLAN_ROWS and n_rows % MOE_ROWS == 0
    items, rows = pl.pallas_call(
        functools.partial(_plan_kernel, nblk=nblk),
        out_shape=(jax.ShapeDtypeStruct((PLAN_ROWS, LANE), i32), jax.ShapeDtypeStruct((8, LANE), f32)),
        compiler_params=pltpu.CompilerParams(vmem_limit_bytes=VMEM_LIMIT),
        name="moe_plan",
    )(cnt_row)
    fields = tuple(items[0:n_items, c] for c in range(N_ITEM_FIELDS))
    return rows[0:1, :], fields + (rows[1, 0:2].astype(i32),)


def kernel(x_prompt, x_sample, state_delta, state_qkv_conv, state_short_conv, norm1_w, w_in, conv_a_w, a_log, dt_bias, out_norm_w, w_branch_a, conv_b_w, w_branch_b, w_o, norm2_w, router_group_w, router_group_b, router_expert_w, router_expert_b, w_gate, w_up, w_down, final_norm_w):
    assert norm1_w.shape[0] == 1, "single-layer trunk"
    bp, tp, d = x_prompt.shape
    bs, ts, _ = x_sample.shape
    assert d == D_MODEL and ts == 1
    n_p = bp * tp
    n_s = bs
    n_all = n_p + n_s

    w_perm = _wprep(jnp.transpose(w_in[0]))
    wa = w_branch_a[0].astype(bf16)
    wb = w_branch_b[0].astype(bf16)
    wo = w_o[0].astype(bf16)
    pad = lambda v: jnp.zeros((1, BA_W), f32).at[0, N_HEADS:2 * N_HEADS].set(v)
    alog_row = pad(a_log[0])
    dtb_row = pad(dt_bias[0])
    onw_row = out_norm_w[0].reshape(1, HEAD)
    cwa = conv_a_w[0]
    cwb = conv_b_w[0]
    r_pad = LANE - N_EXPERTS - N_GROUPS
    rw = jnp.concatenate([router_expert_w[0], router_group_w[0], jnp.zeros((D_MODEL, r_pad), f32)], axis=1)
    rwh = rw.astype(bf16)
    rwl = (rw - rwh.astype(f32)).astype(bf16)
    rb_row = jnp.concatenate([router_expert_b[0], router_group_b[0], jnp.zeros((r_pad,), f32)]).reshape(1, LANE)
    n2_row = norm2_w[0].reshape(1, D_MODEL)

    xp2 = x_prompt.reshape(n_p, D_MODEL)
    proj_p, tails = _inproj_conv(xp2, norm1_w[0], w_perm, cwa, tp)
    tiles_per_seq = tails.shape[0] // bp
    nca_p = tails.reshape(bp, tiles_per_seq, 8, CONV_PAD_W)[:, -1, 8 - (CONV_A - 1):8, 0:QKV_W]
    o_p, y_p, sd_p, ncb_p = _delta_prompt(proj_p.reshape(bp, tp, PROJ_W), cwb, alog_row, dtb_row,
                                          onw_row, nb_step=4 if bp % 4 == 0 else (2 if bp % 2 == 0 else 1))
    cnt0 = jnp.zeros((1, LANE), f32)
    x1_p, h2_p, mi_p, mw_p, cnt_p = _mix_route(xp2, o_p.reshape(n_p, QK_W), y_p.reshape(n_p, SC_W), proj_p,
                                               wa, wb, wo, n2_row, rwh, rwl, rb_row, cnt0)

    xs2 = x_sample.reshape(n_s, D_MODEL)
    proj_s = _inproj(xs2, norm1_w[0], w_perm)
    bufa_t = jnp.transpose(state_qkv_conv[0], (1, 0, 2))
    bufb_t = jnp.transpose(state_short_conv[0], (1, 0, 2))
    q_s, k_s, v_s, beta_s, eg_s, y_s, nbufa_t, nbufb_t = _sample_prep(proj_s, bufa_t, bufb_t, cwa, cwb,
                                                                      alog_row, dtb_row)
    h3 = lambda a: a.reshape(n_s, N_HEADS, HEAD)
    z_s = proj_s[:, COL_Z:COL_Z + QK_W]
    sd_s, o_s = _sample_step(state_delta[0], h3(q_s), h3(k_s), h3(v_s), h3(beta_s), h3(eg_s), h3(z_s), onw_row)
    o_s2 = o_s.reshape(n_s, QK_W).astype(bf16)
    x1_s, h2_s, mi_s, mw_s, cnt = _mix_route(xs2, o_s2, y_s, proj_s, wa, wb, wo, n2_row, rwh, rwl, rb_row, cnt_p)

    starts_row, items = _work_items(cnt, 2 * n_all)
    mi_flat = jnp.concatenate([_dest_rows(mi_p, starts_row)[:, 0:MI_W], _dest_rows(mi_s, starts_row)[:, 0:MI_W]],
                              axis=0).reshape(MI_W * n_all)
    xs_sorted = _dispatch(h2_p, h2_s, mi_flat)
    ys = _moe(xs_sorted.reshape(2 * n_all, D_MODEL), w_gate[0], w_up[0], w_down[0], items)
    y_prompt, y_sample = _combine(x1_p, mw_p, x1_s, mw_s, final_norm_w.reshape(1, D_MODEL),
                                  ys.reshape(2 * n_all // SUBLANE, SUBLANE, D_MODEL), mi_flat)

    return (y_prompt.reshape(bp, tp, D_MODEL),
            y_sample.reshape(bs, ts, D_MODEL),
            sd_p[None],
            nca_p[None],
            ncb_p[None],
            sd_s[None],
            jnp.transpose(nbufa_t, (1, 0, 2))[None],
            jnp.transpose(nbufb_t, (1, 0, 2))[None])
```

```python
import functools

import jax
import jax.numpy as jnp
from jax import lax
from jax.experimental import pallas as pl
from jax.experimental.pallas import tpu as pltpu

f32 = jnp.float32
bf16 = jnp.bfloat16
i32 = jnp.int32

EPS = 1e-6
LANE = 128
D_MODEL = 2048
N_HEADS = 8
HEAD = 128
QK_W = N_HEADS * HEAD
QKV_W = 3 * QK_W
SC_W = 1024
CONV_A = 4
CONV_B = 3
CHUNK = 64
GROUP_HEADS = 4
N_EXPERTS = 64
N_GROUPS = 8
EXPERTS_PER_GROUP = 8
D_FF = 512
MOE_ROWS = 128
TOKEN_TILE = 128
W_SLOTS = 3

COL_QKV = 0
COL_BCX = 3072
COL_GA = 6144
COL_GB = 8192
COL_Z = 10240
COL_BA = 11264
BA_W = 256
PROJ_W = 11520
PROJ_TN = 1280

VMEM_LIMIT = 56 * 1024 * 1024


def _dot(a, b):
    return jnp.dot(a, b, preferred_element_type=f32)


def _dot_nt(a, b):
    return lax.dot_general(a, b, (((1,), (1,)), ((), ())), preferred_element_type=f32)


def _split(x, n):
    parts = []
    r = x
    for i in range(n):
        p = r.astype(bf16)
        parts.append(p)
        if i + 1 < n:
            r = r - p.astype(f32)
    return parts


def _dot_lsplit(x, m, n=3):
    rows = x.shape[0]
    d = _dot(jnp.concatenate(_split(x, n), axis=0), m)
    acc = d[0:rows]
    for i in range(1, n):
        acc = acc + d[i * rows:(i + 1) * rows]
    return acc


def _dot_rsplit(m, x, n=3):
    cols = x.shape[1]
    d = _dot(m, jnp.concatenate(_split(x, n), axis=1))
    acc = d[:, 0:cols]
    for i in range(1, n):
        acc = acc + d[:, i * cols:(i + 1) * cols]
    return acc


_sigmoid = jax.nn.sigmoid


def _silu(x):
    return x * _sigmoid(x)


def _softplus(x):
    return jnp.maximum(x, 0.0) + jnp.log(1.0 + jnp.exp(-jnp.abs(x)))


def _cparams(sem):
    return pltpu.CompilerParams(dimension_semantics=sem, vmem_limit_bytes=VMEM_LIMIT)


def _inproj_kernel(x_ref, nw_ref, w_ref, o_ref, h_ref, *, rows):
    @pl.when(pl.program_id(1) == 0)
    def _():
        def body(r, c):
            sl = pl.ds(pl.multiple_of(r * rows, rows), rows)
            x = x_ref[sl, :]
            ms = jnp.mean(x * x, axis=-1, keepdims=True)
            h_ref[sl, :] = (x * lax.rsqrt(ms + EPS) * nw_ref[...]).astype(bf16)
            return c
        lax.fori_loop(0, x_ref.shape[0] // rows, body, 0)

    o_ref[...] = _dot_nt(h_ref[...], w_ref[...])


def _inproj(x2d, norm_w, w_bf16):
    n = x2d.shape[0]
    tm = min(1024, n)
    assert n % tm == 0 and PROJ_W % PROJ_TN == 0
    return pl.pallas_call(
        functools.partial(_inproj_kernel, rows=min(128, tm)),
        out_shape=jax.ShapeDtypeStruct((n, PROJ_W), f32),
        grid=(n // tm, PROJ_W // PROJ_TN),
        in_specs=[pl.BlockSpec((tm, D_MODEL), lambda i, j: (i, 0)),
                  pl.BlockSpec((1, D_MODEL), lambda i, j: (0, 0)),
                  pl.BlockSpec((PROJ_TN, D_MODEL), lambda i, j: (j, 0))],
        out_specs=pl.BlockSpec((tm, PROJ_TN), lambda i, j: (i, j)),
        scratch_shapes=[pltpu.VMEM((tm, D_MODEL), bf16)],
        compiler_params=_cparams(("arbitrary", "arbitrary")),
        name="inproj",
    )(x2d, norm_w.reshape(1, D_MODEL), w_bf16)


CONV_TILES = 3
CONV_COLS = 2 * HEAD
CONV_ROWS = 128
CONV_PAD_W = CONV_TILES * PROJ_TN


def _qkv_kind(col):
    return "q" if col < QK_W else "k" if col < 2 * QK_W else "v" if col < QKV_W else "raw"


def _inproj_conv_kernel(x_ref, nw_ref, w_ref, cw_ref, o_ref, tail_ref, h_ref, hist_ref, raw_ref, *,
                        rows, tiles_per_seq):
    i = pl.program_id(0)
    j = pl.program_id(1)
    tm = x_ref.shape[0]

    @pl.when(j == 0)
    def _():
        def body(r, c):
            sl = pl.ds(pl.multiple_of(r * rows, rows), rows)
            x = x_ref[sl, :]
            ms = jnp.mean(x * x, axis=-1, keepdims=True)
            h_ref[sl, :] = (x * lax.rsqrt(ms + EPS) * nw_ref[...]).astype(bf16)
            return c
        lax.fori_loop(0, tm // rows, body, 0)

    @pl.when((i == 0) & (j == 0))
    def _():
        hist_ref[...] = jnp.zeros(hist_ref.shape, f32)

    @pl.when(j >= CONV_TILES)
    def _():
        o_ref[...] = _dot_nt(h_ref[...], w_ref[...])

    seq_start = lax.rem(i, tiles_per_seq) == 0
    for jj in range(CONV_TILES):
        @pl.when(j == jj)
        def _():
            def matmul_chunk(idx, c0):
                raw_ref[idx % 2] = _dot_nt(h_ref[...], w_ref[c0:c0 + CONV_COLS, :])

            def conv_chunk(idx, c0):
                cs = slice(c0, c0 + CONV_COLS)
                raw = raw_ref.at[idx % 2]
                tail = raw[tm - 8:tm, :]
                tail_ref[0, :, cs] = tail
                kinds = [_qkv_kind(jj * PROJ_TN + c0 + g * HEAD) for g in range(CONV_COLS // HEAD)]
                if kinds[0] == "raw":
                    o_ref[:, cs] = raw[...]
                    return
                hist = jnp.where(seq_start, 0.0, hist_ref[jj, :, cs])
                for rc in range(tm // CONV_ROWS):
                    r0 = rc * CONV_ROWS
                    if rc > 0:
                        xe = raw[r0 - 8:r0 + CONV_ROWS, :]
                    else:
                        xe = jnp.concatenate([hist, raw[0:CONV_ROWS, :]], axis=0)
                    acc = pltpu.roll(xe, 3, axis=0)[8:] * cw_ref[0:1, cs]
                    acc = acc + pltpu.roll(xe, 2, axis=0)[8:] * cw_ref[1:2, cs]
                    acc = acc + pltpu.roll(xe, 1, axis=0)[8:] * cw_ref[2:3, cs]
                    acc = acc + xe[8:] * cw_ref[3:4, cs]
                    act = _silu(acc)
                    for g, kind in enumerate(kinds):
                        ah = act[:, g * HEAD:(g + 1) * HEAD]
                        if kind != "v":
                            ss = jnp.sum(ah * ah, axis=-1, keepdims=True)
                            inv = lax.rsqrt(ss + EPS)
                            ah = ah * (inv * (HEAD ** -0.5) if kind == "q" else inv)
                        o_ref[r0:r0 + CONV_ROWS, c0 + g * HEAD:c0 + (g + 1) * HEAD] = ah
                hist_ref[jj, :, cs] = tail

            chunks = list(range(0, PROJ_TN, CONV_COLS))
            matmul_chunk(0, chunks[0])
            for idx in range(1, len(chunks)):
                matmul_chunk(idx, chunks[idx])
                conv_chunk(idx - 1, chunks[idx - 1])
            conv_chunk(len(chunks) - 1, chunks[-1])


def _inproj_conv(x2d, norm_w, w_bf16, cwa, seq_len):
    n = x2d.shape[0]
    tm = min(1024, seq_len)
    assert n % tm == 0 and seq_len % tm == 0 and PROJ_W % PROJ_TN == 0 and tm % CONV_ROWS == 0
    assert QKV_W % CONV_COLS == 0 and PROJ_TN % CONV_COLS == 0
    cw_pad = jnp.zeros((CONV_A, CONV_PAD_W), f32).at[:, 0:QKV_W].set(cwa)
    last = CONV_TILES - 1
    return pl.pallas_call(
        functools.partial(_inproj_conv_kernel, rows=min(128, tm), tiles_per_seq=seq_len // tm),
        out_shape=(jax.ShapeDtypeStruct((n, PROJ_W), f32),
                   jax.ShapeDtypeStruct((n // tm, 8, CONV_PAD_W), f32)),
        grid=(n // tm, PROJ_W // PROJ_TN),
        in_specs=[pl.BlockSpec((tm, D_MODEL), lambda i, j: (i, 0)),
                  pl.BlockSpec((1, D_MODEL), lambda i, j: (0, 0)),
                  pl.BlockSpec((PROJ_TN, D_MODEL), lambda i, j: (j, 0)),
                  pl.BlockSpec((CONV_A, PROJ_TN), lambda i, j: (0, jnp.minimum(j, last)))],
        out_specs=(pl.BlockSpec((tm, PROJ_TN), lambda i, j: (i, j)),
                   pl.BlockSpec((1, 8, PROJ_TN), lambda i, j: (i, 0, jnp.minimum(j, last)))),
        scratch_shapes=[pltpu.VMEM((tm, D_MODEL), bf16), pltpu.VMEM((CONV_TILES, 8, PROJ_TN), f32),
                        pltpu.VMEM((2, tm, CONV_COLS), f32)],
        compiler_params=_cparams(("arbitrary", "arbitrary")),
        name="inproj_conv",
    )(x2d, norm_w.reshape(1, D_MODEL), w_bf16, cw_pad)


W_IN_COLS = 11280
WPREP_TN = 1024
WPREP_SHIFT = 16


def _wprep_kernel(a_ref, b_ref, o_ref):
    j = pl.program_id(0)
    keep = WPREP_TN - WPREP_SHIFT

    @pl.when((j < 3) | (j == 10))
    def _():
        o_ref[...] = a_ref[...].astype(bf16)

    @pl.when((j >= 3) & (j < 10))
    def _():
        o_ref[0:keep, :] = a_ref[WPREP_SHIFT:WPREP_TN, :].astype(bf16)
        o_ref[keep:WPREP_TN, :] = b_ref[...].astype(bf16)

    @pl.when(j == 11)
    def _():
        o_ref[0:WPREP_SHIFT, :] = a_ref[0:WPREP_SHIFT, :].astype(bf16)
        o_ref[WPREP_SHIFT:WPREP_TN, :] = jnp.zeros((keep, D_MODEL), bf16)


def _wprep(w_in_t):
    assert w_in_t.shape == (W_IN_COLS, D_MODEL) and 2 * N_HEADS == WPREP_SHIFT
    n_blk = pl.cdiv(PROJ_W, WPREP_TN)

    def a_map(j):
        return (jnp.where(j < 3, j, jnp.where(j < 10, j + 1, jnp.where(j == 10, 3, 4))), 0)

    def b_map(j):
        return (jnp.minimum((WPREP_TN // WPREP_SHIFT) * (j + 2), W_IN_COLS // WPREP_SHIFT - 1), 0)

    return pl.pallas_call(
        _wprep_kernel,
        out_shape=jax.ShapeDtypeStruct((PROJ_W, D_MODEL), bf16),
        grid=(n_blk,),
        in_specs=[pl.BlockSpec((WPREP_TN, D_MODEL), a_map),
                  pl.BlockSpec((WPREP_SHIFT, D_MODEL), b_map)],
        out_specs=pl.BlockSpec((WPREP_TN, D_MODEL), lambda j: (j, 0)),
        compiler_params=_cparams(("arbitrary",)),
        name="wprep",
    )(w_in_t, w_in_t)


def _head_l2norm(a, scale):
    outs = []
    for h in range(N_HEADS):
        ah = a[:, h * HEAD:(h + 1) * HEAD]
        ss = jnp.sum(ah * ah, axis=-1, keepdims=True)
        n = ah * lax.rsqrt(ss + EPS)
        outs.append(n * scale if scale != 1.0 else n)
    return outs


def _delta_prompt_kernel(qkv_ref, bcx_ref, z_ref, ba_ref, cwb_ref, alog_ref, dtb_ref, onw_ref,
                         e64_ref,
                         o_ref, y_ref, snew_ref, ncb_ref,
                         s_ref, xb_ref, *, nb_step):
    C = CHUNK
    G = GROUP_HEADS
    R = G * C
    t = pl.program_id(1)
    nt = pl.num_programs(1)

    @pl.when(t == 0)
    def _():
        s_ref[...] = jnp.zeros(s_ref.shape, f32)
        xb_ref[:, 0:8, :] = jnp.zeros((nb_step, 8, SC_W), f32)

    rr = lax.broadcasted_iota(i32, (R, R), 0)
    cc = lax.broadcasted_iota(i32, (R, R), 1)
    same_bf = jnp.where((rr >> 6) == (cc >> 6), 1.0, 0.0).astype(bf16)
    r2 = lax.broadcasted_iota(i32, (R, G * HEAD), 0)
    c2 = lax.broadcasted_iota(i32, (R, G * HEAD), 1)
    bdmask = (r2 >> 6) == (c2 >> 7)
    r3 = lax.broadcasted_iota(i32, (C, C), 0)
    c3 = lax.broadcasted_iota(i32, (C, C), 1)
    ltri = jnp.where(r3 >= c3, 1.0, 0.0).astype(bf16)
    r4 = lax.broadcasted_iota(i32, (C, R), 0)
    c4 = lax.broadcasted_iota(i32, (C, R), 1)
    ident_t = r4 == (c4 & (C - 1))
    incl_p = r4 >= (c4 & (C - 1))
    strict_p = r4 > (c4 & (C - 1))
    hblk = c4 >> 6
    ones8 = jnp.ones((8, C), bf16)

    nbs = range(nb_step)
    units = [(nb, g) for nb in nbs for g in range(N_HEADS // G)]
    heads = lambda g: range(g * G, (g + 1) * G)

    qn = [[qkv_ref[nb, :, h * HEAD:(h + 1) * HEAD] for h in range(N_HEADS)] for nb in nbs]
    kn = [[qkv_ref[nb, :, QK_W + h * HEAD:QK_W + (h + 1) * HEAD] for h in range(N_HEADS)] for nb in nbs]
    vv = [qkv_ref[nb, :, 2 * QK_W:3 * QK_W] for nb in nbs]

    bts = [ba_ref[nb, :, 0:LANE] for nb in nbs]
    beta_all = [_sigmoid(bt) for bt in bts]
    g_all = [-(jnp.exp(alog_ref[:, 0:LANE]) * _softplus(bt + dtb_ref[:, 0:LANE])) for bt in bts]
    gc_small = [_dot_rsplit(ltri, ga) for ga in g_all]
    gl_small = [gc[C - 1:C, :] for gc in gc_small]

    k_st, q_st, kb, vb, kbg, qd, kd, gc_col = ({} for _ in range(8))
    for u in units:
        nb, g = u
        hs = heads(g)
        k_st[u] = jnp.concatenate([kn[nb][h] for h in hs], axis=0)
        q_st[u] = jnp.concatenate([qn[nb][h] for h in hs], axis=0)
        v_st = jnp.concatenate([vv[nb][:, h * HEAD:(h + 1) * HEAD] for h in hs], axis=0)
        beta_col = jnp.concatenate([beta_all[nb][:, h:h + 1] for h in hs], axis=0)
        gc_col[u] = jnp.concatenate([gc_small[nb][:, 8 + h:9 + h] for h in hs], axis=0)
        gl_col = jnp.concatenate(
            [jnp.broadcast_to(gl_small[nb][:, 8 + h:9 + h], (C, 1)) for h in hs], axis=0)
        kb[u] = k_st[u] * beta_col
        vb[u] = v_st * beta_col
        egc = jnp.exp(gc_col[u])
        kbg[u] = kb[u] * egc
        qd[u] = q_st[u] * egc
        kd[u] = k_st[u] * jnp.exp(gl_col - gc_col[u])

    gx = {u: _dot_lsplit(gc_small[u[0]], e64_ref[u[1], 0:LANE, :]) for u in units}
    crow = {u: _dot_rsplit(ones8, jnp.where(ident_t, gx[u], 0.0))[0:1, :] for u in units}
    a = {u: _dot_nt(jnp.concatenate([kb[u], q_st[u]], axis=0).astype(bf16), k_st[u].astype(bf16))
         for u in units}
    in_blk = [hblk == h for h in range(G - 1)]

    def pack(x):
        out = x[(G - 1) * C:G * C]
        for h in reversed(range(G - 1)):
            out = jnp.where(in_blk[h], x[h * C:(h + 1) * C], out)
        return out

    def expand(xp):
        return jnp.concatenate([xp.astype(bf16)] * G, axis=0) * same_bf

    dec = {u: jnp.where(incl_p, jnp.exp(jnp.where(incl_p, gx[u] - crow[u], 0.0)), 0.0) for u in units}
    nm = {u: jnp.where(strict_p, -(pack(a[u][0:R]) * dec[u]), 0.0) for u in units}
    qkm = {u: expand(pack(a[u][R:2 * R]) * dec[u]) for u in units}

    p = {u: jnp.where(ident_t, 1.0, 0.0) + nm[u] for u in units}
    nk = {u: _dot(nm[u].astype(bf16), expand(nm[u])) for u in units}
    for _ in range(4):
        for u in units:
            x = _dot(jnp.concatenate([p[u], nk[u]], axis=0).astype(bf16), expand(nk[u]))
            p[u] = p[u] + x[0:C]
            nk[u] = x[C:2 * C]
    for u in units:
        p[u] = p[u] + _dot(p[u].astype(bf16), expand(nk[u]))
    uw = {u: _dot(expand(p[u]), jnp.concatenate([vb[u], kbg[u]], axis=1).astype(bf16)) for u in units}

    ws = {}
    for u in units:
        nb, g = u
        for j, h in enumerate(heads(g)):
            sh = s_ref[nb, :, h * HEAD:(h + 1) * HEAD]
            lhs = jnp.concatenate([uw[u][j * C:(j + 1) * C, HEAD:2 * HEAD], qd[u][j * C:(j + 1) * C]], axis=0)
            ws[u, j] = _dot(lhs.astype(bf16), sh.astype(bf16))
    o_heads = {}
    for u in units:
        nb, g = u
        vnew_st = jnp.concatenate([uw[u][j * C:(j + 1) * C, 0:HEAD] - ws[u, j][0:C] for j in range(G)], axis=0)
        o_st = (jnp.concatenate([ws[u, j][C:2 * C] for j in range(G)], axis=0)
                + _dot(qkm[u], vnew_st.astype(bf16)))
        vbd = jnp.where(bdmask, jnp.concatenate([vnew_st] * G, axis=1), 0.0)
        lo = g * G * HEAD
        hi = lo + G * HEAD
        gl_row = jnp.concatenate(
            [jnp.broadcast_to(jnp.exp(gl_small[nb][:, 8 + h:9 + h]), (1, HEAD)) for h in heads(g)], axis=1)
        s_ref[nb, :, lo:hi] = s_ref[nb, :, lo:hi] * gl_row + _dot(kd[u].T.astype(bf16), vbd.astype(bf16))
        for j, h in enumerate(heads(g)):
            o_heads[nb, h] = o_st[j * C:(j + 1) * C]

    for nb in nbs:
        zt = z_ref[nb]
        for h in range(N_HEADS):
            oh = o_heads[nb, h]
            ms = jnp.mean(oh * oh, axis=-1, keepdims=True)
            zh = zt[:, h * HEAD:(h + 1) * HEAD]
            on = oh * lax.rsqrt(ms + EPS) * onw_ref[...] * _silu(zh)
            o_ref[nb, :, h * HEAD:(h + 1) * HEAD] = on.astype(bf16)

    for nb in nbs:
        bcx = bcx_ref[nb]
        cx = bcx[:, SC_W:2 * SC_W] * bcx[:, 2 * SC_W:3 * SC_W]
        xb_ref[nb, 8:8 + C, :] = cx
        ce = xb_ref[nb]
        cv = pltpu.roll(ce, 2, axis=0)[8:8 + C] * cwb_ref[0:1, :]
        cv = cv + pltpu.roll(ce, 1, axis=0)[8:8 + C] * cwb_ref[1:2, :]
        cv = cv + cx * cwb_ref[2:3, :]
        y_ref[nb] = (bcx[:, 0:SC_W] * cv).astype(bf16)
        xb_ref[nb, 0:8, :] = xb_ref[nb, C:C + 8, :]

    @pl.when(t == nt - 1)
    def _():
        for nb in range(nb_step):
            for h in range(N_HEADS):
                snew_ref[nb, h] = s_ref[nb, :, h * HEAD:(h + 1) * HEAD]
            ncb_ref[nb] = xb_ref[nb, 6:8, :]


def _expand_consts():
    lane = jnp.arange(BA_W)[:, None]
    col = jnp.arange(QK_W)[None, :]
    eb = (lane == (col >> 7)).astype(bf16)
    eg = (lane == (8 + (col >> 7))).astype(bf16)
    col64 = jnp.arange(GROUP_HEADS * CHUNK)[None, :]
    e64 = jnp.stack([(lane == (8 + g * GROUP_HEADS + (col64 >> 6))).astype(bf16)
                     for g in range(N_HEADS // GROUP_HEADS)], axis=0)
    return eb, eg, e64


def _delta_prompt(proj3, cwb, alog_row, dtb_row, onw_row, nb_step):
    b, t, _ = proj3.shape
    assert t % CHUNK == 0 and b % nb_step == 0
    _, _, e64 = _expand_consts()
    c = CHUNK
    const2 = lambda bi, ti: (0, 0)
    outs = pl.pallas_call(
        functools.partial(_delta_prompt_kernel, nb_step=nb_step),
        out_shape=(jax.ShapeDtypeStruct((b, t, QK_W), bf16),
                   jax.ShapeDtypeStruct((b, t, SC_W), bf16),
                   jax.ShapeDtypeStruct((b, N_HEADS, HEAD, HEAD), f32),
                   jax.ShapeDtypeStruct((b, CONV_B - 1, SC_W), f32)),
        grid=(b // nb_step, t // c),
        in_specs=[pl.BlockSpec((nb_step, c, QKV_W), lambda bi, ti: (bi, ti, COL_QKV // QKV_W)),
                  pl.BlockSpec((nb_step, c, QKV_W), lambda bi, ti: (bi, ti, COL_BCX // QKV_W)),
                  pl.BlockSpec((nb_step, c, QK_W), lambda bi, ti: (bi, ti, COL_Z // QK_W)),
                  pl.BlockSpec((nb_step, c, BA_W), lambda bi, ti: (bi, ti, COL_BA // BA_W)),
                  pl.BlockSpec((CONV_B, SC_W), const2),
                  pl.BlockSpec((1, BA_W), const2),
                  pl.BlockSpec((1, BA_W), const2),
                  pl.BlockSpec((1, HEAD), const2),
                  pl.BlockSpec((N_HEADS // GROUP_HEADS, BA_W, GROUP_HEADS * CHUNK), lambda bi, ti: (0, 0, 0))],
        out_specs=(pl.BlockSpec((nb_step, c, QK_W), lambda bi, ti: (bi, ti, 0)),
                   pl.BlockSpec((nb_step, c, SC_W), lambda bi, ti: (bi, ti, 0)),
                   pl.BlockSpec((nb_step, N_HEADS, HEAD, HEAD), lambda bi, ti: (bi, 0, 0, 0)),
                   pl.BlockSpec((nb_step, CONV_B - 1, SC_W), lambda bi, ti: (bi, 0, 0))),
        scratch_shapes=[pltpu.VMEM((nb_step, HEAD, QK_W), f32),
                        pltpu.VMEM((nb_step, 8 + c, SC_W), f32)],
        compiler_params=_cparams(("arbitrary", "arbitrary")),
        name="delta_prompt",
    )(proj3, proj3, proj3, proj3, cwb, alog_row, dtb_row, onw_row, e64)
    return outs


def _sample_prep_kernel(p_ref, bufa_ref, bufb_ref, cwa_ref, cwb_ref, alog_ref, dtb_ref, eb_ref, eg_ref,
                        q_ref, k_ref, v_ref, beta_ref, eg_out_ref, y_ref, nbufa_ref, nbufb_ref):
    def conv_sec(lo):
        hi = lo + QK_W
        raw = p_ref[:, COL_QKV + lo:COL_QKV + hi]
        acc = bufa_ref[0, :, lo:hi] * cwa_ref[0:1, lo:hi]
        acc = acc + bufa_ref[1, :, lo:hi] * cwa_ref[1:2, lo:hi]
        acc = acc + bufa_ref[2, :, lo:hi] * cwa_ref[2:3, lo:hi]
        acc = acc + raw * cwa_ref[3:4, lo:hi]
        nbufa_ref[0, :, lo:hi] = bufa_ref[1, :, lo:hi]
        nbufa_ref[1, :, lo:hi] = bufa_ref[2, :, lo:hi]
        nbufa_ref[2, :, lo:hi] = raw
        return _silu(acc)

    qn = _head_l2norm(conv_sec(0), HEAD ** -0.5)
    kn = _head_l2norm(conv_sec(QK_W), 1.0)
    for h in range(N_HEADS):
        q_ref[:, h * HEAD:(h + 1) * HEAD] = qn[h]
        k_ref[:, h * HEAD:(h + 1) * HEAD] = kn[h]
    v_ref[...] = conv_sec(2 * QK_W)

    bt = p_ref[:, COL_BA:COL_BA + BA_W]
    beta_all = _sigmoid(bt)
    g_all = -(jnp.exp(alog_ref[...]) * _softplus(bt + dtb_ref[...]))
    beta_ref[...] = _dot_lsplit(beta_all, eb_ref[...])
    eg_out_ref[...] = jnp.exp(_dot_lsplit(g_all, eg_ref[...]))

    bg = p_ref[:, COL_BCX:COL_BCX + SC_W]
    cx = p_ref[:, COL_BCX + SC_W:COL_BCX + 2 * SC_W] * p_ref[:, COL_BCX + 2 * SC_W:COL_BCX + 3 * SC_W]
    cv = bufb_ref[0] * cwb_ref[0:1, :]
    cv = cv + bufb_ref[1] * cwb_ref[1:2, :]
    cv = cv + cx * cwb_ref[2:3, :]
    y_ref[...] = (bg * cv).astype(bf16)
    nbufb_ref[0] = bufb_ref[1]
    nbufb_ref[1] = cx


def _sample_prep(proj_s, bufa_t, bufb_t, cwa, cwb, alog_row, dtb_row):
    n = proj_s.shape[0]
    eb, eg, _ = _expand_consts()
    row = jax.ShapeDtypeStruct((n, QK_W), f32)
    return pl.pallas_call(
        _sample_prep_kernel,
        out_shape=(row, row, row, row, row,
                   jax.ShapeDtypeStruct((n, SC_W), bf16),
                   jax.ShapeDtypeStruct((CONV_A - 1, n, QKV_W), f32),
                   jax.ShapeDtypeStruct((CONV_B - 1, n, SC_W), f32)),
        compiler_params=pltpu.CompilerParams(vmem_limit_bytes=VMEM_LIMIT),
        name="sample_prep",
    )(proj_s, bufa_t, bufb_t, cwa, cwb, alog_row, dtb_row, eb, eg)


def _sample_step_kernel(s_ref, q_ref, k_ref, v_ref, beta_ref, eg_ref, z_ref, onw_ref,
                        snew_ref, o_ref, *, bb):
    w = N_HEADS * HEAD
    r8 = lax.broadcasted_iota(i32, (N_HEADS, w), 0)
    c8 = lax.broadcasted_iota(i32, (N_HEADS, w), 1)
    mask8 = r8 == (c8 >> 7)
    zpad_k = jnp.zeros((HEAD - N_HEADS, HEAD), f32)
    hb = lambda h: slice(h * HEAD, (h + 1) * HEAD)
    bs = range(bb)
    s_dec, k8s, kts = [], [], []
    for b in bs:
        s_all = jnp.concatenate([s_ref[b, h] for h in range(N_HEADS)], axis=1)
        eg8 = eg_ref[b]
        eg_row = jnp.concatenate([eg8[h:h + 1, :] for h in range(N_HEADS)], axis=1)
        s_dec.append(s_all * eg_row)
        k8s.append(k_ref[b])
        kts.append(jnp.concatenate([k8s[b], zpad_k], axis=0).T)
    xs = [_dot(k8s[b].astype(bf16), s_dec[b].astype(bf16)) for b in bs]
    s_new = []
    for b in bs:
        vb8, bt8 = v_ref[b], beta_ref[b]
        upd = [kts[b][:, h:h + 1] * ((vb8[h:h + 1, :] - xs[b][h:h + 1, hb(h)]) * bt8[h:h + 1, :])
               for h in range(N_HEADS)]
        s_new.append(s_dec[b] + jnp.concatenate(upd, axis=1))
    ys = [_dot(q_ref[b].astype(bf16), s_new[b].astype(bf16)) for b in bs]
    for b in bs:
        yv = jnp.where(mask8, ys[b], 0.0)
        o8 = yv[:, 0:HEAD]
        for j in range(1, N_HEADS):
            o8 = o8 + yv[:, j * HEAD:(j + 1) * HEAD]
        ms = jnp.mean(o8 * o8, axis=-1, keepdims=True)
        o_ref[b] = o8 * lax.rsqrt(ms + EPS) * onw_ref[...] * _silu(z_ref[b])
        for h in range(N_HEADS):
            snew_ref[b, h] = s_new[b][:, hb(h)]


def _sample_step(state, q, k, v, beta, eg, z, onw_row, bb=8):
    n = state.shape[0]
    assert n % bb == 0
    hspec = pl.BlockSpec((bb, N_HEADS, HEAD), lambda i: (i, 0, 0))
    sspec = pl.BlockSpec((bb, N_HEADS, HEAD, HEAD), lambda i: (i, 0, 0, 0))
    return pl.pallas_call(
        functools.partial(_sample_step_kernel, bb=bb),
        out_shape=(jax.ShapeDtypeStruct(state.shape, f32),
                   jax.ShapeDtypeStruct((n, N_HEADS, HEAD), f32)),
        grid=(n // bb,),
        in_specs=[sspec, hspec, hspec, hspec, hspec, hspec, hspec, pl.BlockSpec((1, HEAD), lambda i: (0, 0))],
        out_specs=(sspec, hspec),
        compiler_params=_cparams(("arbitrary",)),
        name="sample_step",
    )(state, q, k, v, beta, eg, z, onw_row)


def _mix_route_kernel(x_ref, o_ref, y_ref, ga_ref, gb_ref, wa_ref, wb_ref, wo_ref, n2_ref,
                      rwh_ref, rwl_ref, rb_ref, cnt_in_ref, x1_ref, h2_ref, mi_ref, mw_ref, cnt_ref):
    i = pl.program_id(0)
    tm = x_ref.shape[0]

    @pl.when(i == 0)
    def _():
        cnt_ref[...] = cnt_in_ref[...]

    oa = _dot(o_ref[...], wa_ref[...])
    ob = _dot(y_ref[...], wb_ref[...])
    merged = _sigmoid(ga_ref[...]) * oa + _sigmoid(gb_ref[...]) * ob
    x1 = x_ref[...] + _dot(merged.astype(bf16), wo_ref[...])
    x1_ref[...] = x1
    ms = jnp.mean(x1 * x1, axis=-1, keepdims=True)
    h2 = x1 * lax.rsqrt(ms + EPS) * n2_ref[...]
    h2_ref[...] = h2

    h_hi, h_lo = _split(h2, 2)
    logits = _dot(h_hi, rwh_ref[...]) + _dot(h_hi, rwl_ref[...]) + _dot(h_lo, rwh_ref[...]) + rb_ref[...]

    lane = lax.broadcasted_iota(i32, (tm, LANE), 1)
    lanef = lane.astype(f32)
    neg = jnp.float32(-jnp.inf)
    big = jnp.float32(1e9)
    gmask = (lane >= N_EXPERTS) & (lane < N_EXPERTS + N_GROUPS)
    gl = jnp.where(gmask, logits, neg)
    gmax = jnp.max(gl, axis=-1, keepdims=True)
    gidx = jnp.min(jnp.where(gl == gmax, lanef - N_EXPERTS, big), axis=-1, keepdims=True)
    gsum = jnp.sum(jnp.where(gmask, jnp.exp(gl - gmax), 0.0), axis=-1, keepdims=True)
    gprob = 1.0 / gsum

    emask = (lane < N_EXPERTS) & ((lane >> 3).astype(f32) == gidx)
    el = jnp.where(emask, logits, neg)
    emax = jnp.max(el, axis=-1, keepdims=True)
    pe = jnp.where(emask, jnp.exp(el - emax), 0.0)
    eprob = pe / jnp.sum(pe, axis=-1, keepdims=True)
    p1m = jnp.where(emask, eprob, -1.0)
    m1 = jnp.max(p1m, axis=-1, keepdims=True)
    i1 = jnp.min(jnp.where(p1m == m1, lanef, big), axis=-1, keepdims=True)
    p2m = jnp.where(lanef == i1, -1.0, p1m)
    m2 = jnp.max(p2m, axis=-1, keepdims=True)
    i2 = jnp.min(jnp.where(p2m == m2, lanef, big), axis=-1, keepdims=True)
    tot = m1 + m2
    c1 = m1 / tot * gprob
    c2 = m2 / tot * gprob

    oh1 = jnp.where(lanef == i1, 1.0, 0.0)
    oh2 = jnp.where(lanef == i2, 1.0, 0.0)
    ohs = oh1 + oh2
    rt = lax.broadcasted_iota(i32, (tm, tm), 0)
    ct = lax.broadcasted_iota(i32, (tm, tm), 1)
    lstrict = jnp.where(rt > ct, 1.0, 0.0).astype(bf16)
    cs = _dot(lstrict, ohs.astype(bf16)) + cnt_ref[...]
    rank1 = jnp.sum(cs * oh1, axis=-1, keepdims=True)
    rank2 = jnp.sum(cs * oh2, axis=-1, keepdims=True)
    cnt_ref[...] = cnt_ref[...] + jnp.sum(ohs, axis=0, keepdims=True)

    mi = jnp.where(lane == 0, i1, jnp.where(lane == 1, i2, jnp.where(lane == 2, rank1,
                                                                     jnp.where(lane == 3, rank2, 0.0))))
    mi_ref[...] = mi.astype(i32)
    mw_ref[...] = jnp.where(lane == 0, c1, jnp.where(lane == 1, c2, 0.0))


def _mix_route(x2d, o2d, y2d, proj2d, wa, wb, wo, n2_row, rwh, rwl, rb_row, cnt_in):
    n = x2d.shape[0]
    tm = min(256, n)
    assert n % tm == 0
    tok = lambda width: pl.BlockSpec((tm, width), lambda i: (i, 0))
    full = lambda a: pl.BlockSpec(a.shape, lambda i: (0,) * a.ndim)
    in_specs = [tok(D_MODEL), tok(QK_W), tok(SC_W),
                pl.BlockSpec((tm, D_MODEL), lambda i: (i, COL_GA // D_MODEL)),
                pl.BlockSpec((tm, D_MODEL), lambda i: (i, COL_GB // D_MODEL)),
                full(wa), full(wb), full(wo), full(n2_row), full(rwh), full(rwl), full(rb_row), full(cnt_in)]
    out_shape = (jax.ShapeDtypeStruct((n, D_MODEL), f32),
                 jax.ShapeDtypeStruct((n, D_MODEL), f32),
                 jax.ShapeDtypeStruct((n, LANE), i32),
                 jax.ShapeDtypeStruct((n, LANE), f32),
                 jax.ShapeDtypeStruct((1, LANE), f32))
    out_specs = (tok(D_MODEL), tok(D_MODEL), tok(LANE), tok(LANE),
                 pl.BlockSpec((1, LANE), lambda i: (0, 0)))
    return pl.pallas_call(
        _mix_route_kernel,
        out_shape=out_shape,
        grid=(n // tm,),
        in_specs=in_specs,
        out_specs=out_specs,
        compiler_params=_cparams(("arbitrary",)),
        name="mix_route",
    )(x2d, o2d, y2d, proj2d, proj2d, wa, wb, wo, n2_row, rwh, rwl, rb_row, cnt_in)


MI_W = 4
SUBLANE = 8
ROW_DMA_UNROLL = 8


def _dest_kernel(mi_ref, starts_ref, o_ref):
    mi = mi_ref[...]
    lane = lax.broadcasted_iota(i32, mi.shape, 1)
    st = starts_ref[...]

    def first_row(e_col):
        return jnp.sum(jnp.where(lane == e_col, st, 0.0), axis=-1, keepdims=True).astype(i32)

    d0 = first_row(mi[:, 0:1]) + mi[:, 2:3]
    d1 = first_row(mi[:, 1:2]) + mi[:, 3:4]
    o_ref[...] = jnp.where(lane == 0, d0 >> 3, jnp.where(lane == 1, d0 & (SUBLANE - 1),
                           jnp.where(lane == 2, d1 >> 3, jnp.where(lane == 3, d1 & (SUBLANE - 1), 0))))


def _dest_rows(mi, starts_row):
    n = mi.shape[0]
    tm = min(1024, n)
    assert n % tm == 0
    return pl.pallas_call(
        _dest_kernel,
        out_shape=jax.ShapeDtypeStruct((n, LANE), i32),
        grid=(n // tm,),
        in_specs=[pl.BlockSpec((tm, LANE), lambda i: (i, 0)), pl.BlockSpec((1, LANE), lambda i: (0, 0))],
        out_specs=pl.BlockSpec((tm, LANE), lambda i: (i, 0)),
        compiler_params=_cparams(("arbitrary",)),
        name="moe_dest",
    )(mi, starts_row)


def _dispatch_kernel(mi_ref, hp_ref, hs_ref, xs_ref, sem, *, np_tiles, tm):
    i = pl.program_id(0)

    def scatter_rows(h_ref):
        def start(t, c):
            for u in range(SUBLANE):
                rec = MI_W * (SUBLANE * t + u)
                for k in range(2):
                    dst = xs_ref.at[mi_ref[rec + 2 * k], pl.ds(mi_ref[rec + 2 * k + 1], 1)]
                    pltpu.make_async_copy(h_ref.at[t, pl.ds(u, 1)], dst, sem).start(priority=k)
            return c

        lax.fori_loop(0, tm // SUBLANE, start, 0)
        for k in range(2):
            pltpu.make_async_copy(h_ref, xs_ref.at[pl.ds(0, tm // SUBLANE)], sem).wait()

    @pl.when(i < np_tiles)
    def _():
        scatter_rows(hp_ref)

    @pl.when(i >= np_tiles)
    def _():
        scatter_rows(hs_ref)


def _dispatch(h2_p, h2_s, mi_flat):
    tm = TOKEN_TILE
    n_p, n_s = h2_p.shape[0], h2_s.shape[0]
    assert n_p % tm == 0 and n_s == tm and tm % SUBLANE == 0
    np_tiles = n_p // tm
    tiled = lambda a: a.reshape(a.shape[0] // SUBLANE, SUBLANE, D_MODEL)
    blk = (tm // SUBLANE, SUBLANE, D_MODEL)
    return pl.pallas_call(
        functools.partial(_dispatch_kernel, np_tiles=np_tiles, tm=tm),
        out_shape=jax.ShapeDtypeStruct((2 * (n_p + n_s) // SUBLANE, SUBLANE, D_MODEL), f32),
        grid=(np_tiles + 1,),
        in_specs=[pl.BlockSpec((MI_W * tm,), lambda i: (i,), memory_space=pltpu.SMEM),
                  pl.BlockSpec(blk, lambda i: (jnp.minimum(i, np_tiles - 1), 0, 0)),
                  pl.BlockSpec(blk, lambda i: (0, 0, 0))],
        out_specs=pl.BlockSpec(memory_space=pl.ANY),
        scratch_shapes=[pltpu.SemaphoreType.DMA(())],
        compiler_params=_cparams(("arbitrary",)),
        name="moe_dispatch",
    )(mi_flat, tiled(h2_p), tiled(h2_s))


def _cast_rows(src_ref, dst_ref, col0=0, rows=256):
    width = src_ref.shape[1]

    def body(r, c):
        sl = pl.ds(pl.multiple_of(r * rows, rows), rows)
        dst_ref[sl, col0:col0 + width] = src_ref[sl, :].astype(bf16)
        return c
    lax.fori_loop(0, src_ref.shape[0] // rows, body, 0)


def _moe_kernel(blk_ref, lo_ref, hi_ref, first_ref, newe_ref, slot_ref, pre_ref, init_ref,
                x_ref, wg_hbm, wu_hbm, wd_hbm, o_ref,
                wg_f, wu_f, wd_f, wgu_b, wd_b, sem):
    i = pl.program_id(0)
    lo = lo_ref[i]
    hi = hi_ref[i]

    def weight_copies(e, slot):
        return [pltpu.make_async_copy(wg_hbm.at[e], wg_f.at[slot], sem.at[slot, 0]),
                pltpu.make_async_copy(wu_hbm.at[e], wu_f.at[slot], sem.at[slot, 1]),
                pltpu.make_async_copy(wd_hbm.at[e], wd_f.at[slot], sem.at[slot, 2])]

    def start_weights(e, slot):
        for cp, prio in zip(weight_copies(e, slot), (0, 1, 1)):
            cp.start(priority=prio)

    @pl.when(i == 0)
    def _():
        start_weights(init_ref[0], 0)
        for k in range(1, W_SLOTS):
            @pl.when(init_ref[k] >= 0)
            def _():
                start_weights(init_ref[k], k)

    @pl.when(newe_ref[i] == 1)
    def _():
        slot = slot_ref[i]
        cg, cu, cd = weight_copies(0, slot)
        cg.wait()
        _cast_rows(wg_f.at[slot], wgu_b, 0)
        cu.wait()
        _cast_rows(wu_f.at[slot], wgu_b, D_FF)
        cd.wait()
        _cast_rows(wd_f.at[slot], wd_b)

        @pl.when(pre_ref[i] >= 0)
        def _():
            start_weights(pre_ref[i], slot)

    @pl.when(hi > lo)
    def _():
        x = x_ref[...].astype(bf16)
        au = _dot(x, wgu_b[...])
        y = _dot((_silu(au[:, 0:D_FF]) * au[:, D_FF:2 * D_FF]).astype(bf16), wd_b[...])
        row = lax.broadcasted_iota(i32, y.shape, 0)
        ym = jnp.where((row >= lo) & (row < hi), y, 0.0)

        @pl.when(first_ref[i] == 1)
        def _():
            o_ref[...] = ym

        @pl.when(first_ref[i] == 0)
        def _():
            o_ref[...] = o_ref[...] + ym


def _moe(xs, w_gate, w_up, w_down, items):
    n_items = items[0].shape[0]
    rows = xs.shape[0]
    n_pref = len(items)
    xmap = lambda i, blk, *_: (blk[i], 0)
    grid_spec = pltpu.PrefetchScalarGridSpec(
        num_scalar_prefetch=n_pref,
        grid=(n_items,),
        in_specs=[pl.BlockSpec((MOE_ROWS, D_MODEL), xmap),
                  pl.BlockSpec(memory_space=pl.ANY),
                  pl.BlockSpec(memory_space=pl.ANY),
                  pl.BlockSpec(memory_space=pl.ANY)],
        out_specs=pl.BlockSpec((MOE_ROWS, D_MODEL), xmap),
        scratch_shapes=[pltpu.VMEM((W_SLOTS, D_MODEL, D_FF), f32), pltpu.VMEM((W_SLOTS, D_MODEL, D_FF), f32),
                        pltpu.VMEM((W_SLOTS, D_FF, D_MODEL), f32),
                        pltpu.VMEM((D_MODEL, 2 * D_FF), bf16), pltpu.VMEM((D_FF, D_MODEL), bf16),
                        pltpu.SemaphoreType.DMA((W_SLOTS, 3))],
    )
    return pl.pallas_call(
        _moe_kernel,
        out_shape=jax.ShapeDtypeStruct((rows, D_MODEL), f32),
        grid_spec=grid_spec,
        compiler_params=_cparams(("arbitrary",)),
        name="moe_experts",
    )(*items, xs, w_gate, w_up, w_down)


def _combine_kernel(mi_ref, mi_next_ref, x1p_ref, mwp_ref, x1s_ref, mws_ref, fnw_ref, ys_ref,
                    yp_ref, ysm_ref, g_ref, sem, *, np_tiles):
    i = pl.program_id(0)
    n = pl.num_programs(0)
    tm = x1p_ref.shape[0]
    slot = lax.rem(i, 2)

    def gather_rows(m_ref, dst_slot):
        def body(t, c):
            for u in range(SUBLANE):
                rec = MI_W * (SUBLANE * t + u)
                for k in range(2):
                    src = ys_ref.at[m_ref[rec + 2 * k], pl.ds(m_ref[rec + 2 * k + 1], 1)]
                    pltpu.make_async_copy(src, g_ref.at[dst_slot, k, t, pl.ds(u, 1)],
                                          sem.at[dst_slot]).start(priority=k)
            return c
        lax.fori_loop(0, tm // SUBLANE, body, 0)

    @pl.when(i == 0)
    def _():
        gather_rows(mi_ref, 0)

    @pl.when(i + 1 < n)
    def _():
        gather_rows(mi_next_ref, 1 - slot)

    for k in range(2):
        pltpu.make_async_copy(ys_ref.at[pl.ds(0, tm // SUBLANE)], g_ref.at[slot, k], sem.at[slot]).wait()

    def finish(x1_ref, mw_ref, out_ref):
        mw = mw_ref[...]
        g0 = g_ref[slot, 0].reshape(tm, D_MODEL)
        g1 = g_ref[slot, 1].reshape(tm, D_MODEL)
        x2 = x1_ref[...] + (g0 * mw[:, 0:1] + g1 * mw[:, 1:2])
        ms = jnp.mean(x2 * x2, axis=-1, keepdims=True)
        out_ref[...] = x2 * lax.rsqrt(ms + EPS) * fnw_ref[...]

    @pl.when(i < np_tiles)
    def _():
        finish(x1p_ref, mwp_ref, yp_ref)

    @pl.when(i >= np_tiles)
    def _():
        finish(x1s_ref, mws_ref, ysm_ref)


def _combine(x1_p, mw_p, x1_s, mw_s, fnw_row, ys3, mi_flat):
    tm = TOKEN_TILE
    n_p, n_s = x1_p.shape[0], x1_s.shape[0]
    assert n_p % tm == 0 and n_s == tm
    np_tiles = n_p // tm
    ptile = lambda width: pl.BlockSpec((tm, width), lambda i: (jnp.minimum(i, np_tiles - 1), 0))
    stile = lambda width: pl.BlockSpec((tm, width), lambda i: (0, 0))
    return pl.pallas_call(
        functools.partial(_combine_kernel, np_tiles=np_tiles),
        out_shape=(jax.ShapeDtypeStruct((n_p, D_MODEL), f32),
                   jax.ShapeDtypeStruct((n_s, D_MODEL), f32)),
        grid=(np_tiles + 1,),
        in_specs=[pl.BlockSpec((MI_W * tm,), lambda i: (i,), memory_space=pltpu.SMEM),
                  pl.BlockSpec((MI_W * tm,), lambda i: (jnp.minimum(i + 1, np_tiles),), memory_space=pltpu.SMEM),
                  ptile(D_MODEL), ptile(LANE), stile(D_MODEL), stile(LANE),
                  pl.BlockSpec((1, D_MODEL), lambda i: (0, 0)),
                  pl.BlockSpec(memory_space=pl.ANY)],
        out_specs=(ptile(D_MODEL), stile(D_MODEL)),
        scratch_shapes=[pltpu.VMEM((2, 2, tm // SUBLANE, SUBLANE, D_MODEL), f32), pltpu.SemaphoreType.DMA((2,))],
        compiler_params=_cparams(("arbitrary",)),
        name="moe_combine",
    )(mi_flat, mi_flat, x1_p, mw_p, x1_s, mw_s, fnw_row, ys3)


PLAN_ROWS = 256
N_ITEM_FIELDS = 7


def _plan_kernel(cnt_ref, items_ref, rows_ref, *, nblk):
    cnt = cnt_ref[...]
    lane1 = lax.broadcasted_iota(i32, (1, LANE), 1)
    in_e = lane1 < N_EXPERTS
    ri = lax.broadcasted_iota(i32, (LANE, LANE), 0)
    ci = lax.broadcasted_iota(i32, (LANE, LANE), 1)
    upper = jnp.where(ri <= ci, 1.0, 0.0).astype(bf16)

    def cumsum_lanes(v):
        return _dot_lsplit(jnp.broadcast_to(v, (8, LANE)), upper)[0:1, :]

    shift = MOE_ROWS.bit_length() - 1
    ends = cumsum_lanes(cnt)
    starts = ends - cnt
    act = cnt > 0.0
    first_blk = (starts.astype(i32) >> shift).astype(f32)
    last_blk = (jnp.maximum(ends - 1.0, 0.0).astype(i32) >> shift).astype(f32)
    nvis = jnp.where(act, last_blk - first_blk + 1.0, 0.0)
    vis_end = cumsum_lanes(nvis)
    vis_start = vis_end - nvis
    total = jnp.max(vis_end, axis=-1, keepdims=True)
    cum_act = cumsum_lanes(jnp.where(act, 1.0, 0.0))
    n_uniq = jnp.max(cum_act, axis=-1, keepdims=True)

    p = PLAN_ROWS
    lane = lax.broadcasted_iota(i32, (p, LANE), 1)
    idx = lax.broadcasted_iota(i32, (p, LANE), 0).astype(f32)
    idx1 = idx[:, 0:1]
    count_le = lambda row, col: jnp.sum(jnp.where((row <= col) & in_e, 1.0, 0.0), axis=-1, keepdims=True)
    e = jnp.minimum(count_le(vis_end, idx), N_EXPERTS - 1.0)
    onehot = lane.astype(f32) == e
    look = lambda tbl: jnp.sum(jnp.where(onehot, tbl, 0.0), axis=-1, keepdims=True)
    blk = look(first_blk) + idx1 - look(vis_start)
    lo = jnp.maximum(look(starts), blk * MOE_ROWS) - blk * MOE_ROWS
    hi = jnp.minimum(look(ends), (blk + 1.0) * MOE_ROWS) - blk * MOE_ROWS
    valid = idx1 < total
    blk = jnp.where(valid, blk, nblk - 1.0)
    lo = jnp.where(valid, lo, 0.0)
    hi = jnp.where(valid, hi, 0.0)
    rep = lambda c: jnp.broadcast_to(c, (p, LANE))
    prev = lambda c: pltpu.roll(rep(c), 1, axis=0)[:, 0:1]
    is0 = idx1 == 0.0
    first = valid & (is0 | (blk != prev(blk)))
    newe = valid & (is0 | (e != prev(e)))
    rp = lax.broadcasted_iota(i32, (p, p), 0)
    cp = lax.broadcasted_iota(i32, (p, p), 1)
    lower = jnp.where(rp >= cp, 1.0, 0.0).astype(bf16)
    order = _dot(lower, rep(jnp.where(newe, 1.0, 0.0)).astype(bf16))[:, 0:1] - 1.0
    slot = jnp.where(newe, order - W_SLOTS * jnp.floor((order + 0.5) * (1.0 / W_SLOTS)), 0.0)
    k2 = order + float(W_SLOTS)
    pre = jnp.where(newe & (k2 < n_uniq), count_le(cum_act, rep(k2)), -1.0)
    out = jnp.zeros((p, LANE), f32)
    for c, v in enumerate([blk, lo, hi, jnp.where(first, 1.0, 0.0), jnp.where(newe, 1.0, 0.0), slot, pre]):
        out = jnp.where(lane == c, v, out)
    items_ref[...] = out.astype(i32)

    init_row = jnp.zeros((1, LANE), f32)
    for k in range(W_SLOTS):
        init_row = jnp.where(lane1 == k, jnp.where(n_uniq > float(k), count_le(cum_act, float(k)), -1.0), init_row)
    rows_ref[...] = jnp.zeros(rows_ref.shape, f32)
    rows_ref[0:1, :] = starts
    rows_ref[1:2, :] = init_row


def _work_items(cnt_row, n_rows):
    nblk = n_rows // MOE_ROWS
    n_items = nblk + N_EXPERTS - 1
    assert n_items <= PLAN_ROWS and n_rows % MOE_ROWS == 0
    items, rows = pl.pallas_call(
        functools.partial(_plan_kernel, nblk=nblk),
        out_shape=(jax.ShapeDtypeStruct((PLAN_ROWS, LANE), i32), jax.ShapeDtypeStruct((8, LANE), f32)),
        compiler_params=pltpu.CompilerParams(vmem_limit_bytes=VMEM_LIMIT),
        name="moe_plan",
    )(cnt_row)
    fields = tuple(items[0:n_items, c] for c in range(N_ITEM_FIELDS))
    return rows[0:1, :], fields + (rows[1, 0:W_SLOTS].astype(i32),)


def kernel(x_prompt, x_sample, state_delta, state_qkv_conv, state_short_conv, norm1_w, w_in, conv_a_w, a_log, dt_bias, out_norm_w, w_branch_a, conv_b_w, w_branch_b, w_o, norm2_w, router_group_w, router_group_b, router_expert_w, router_expert_b, w_gate, w_up, w_down, final_norm_w):
    assert norm1_w.shape[0] == 1, "single-layer trunk"
    bp, tp, d = x_prompt.shape
    bs, ts, _ = x_sample.shape
    assert d == D_MODEL and ts == 1
    n_p = bp * tp
    n_s = bs
    n_all = n_p + n_s

    w_perm = _wprep(jnp.transpose(w_in[0]))
    wa = w_branch_a[0].astype(bf16)
    wb = w_branch_b[0].astype(bf16)
    wo = w_o[0].astype(bf16)
    pad = lambda v: jnp.zeros((1, BA_W), f32).at[0, N_HEADS:2 * N_HEADS].set(v)
    alog_row = pad(a_log[0])
    dtb_row = pad(dt_bias[0])
    onw_row = out_norm_w[0].reshape(1, HEAD)
    cwa = conv_a_w[0]
    cwb = conv_b_w[0]
    r_pad = LANE - N_EXPERTS - N_GROUPS
    rw = jnp.concatenate([router_expert_w[0], router_group_w[0], jnp.zeros((D_MODEL, r_pad), f32)], axis=1)
    rwh = rw.astype(bf16)
    rwl = (rw - rwh.astype(f32)).astype(bf16)
    rb_row = jnp.concatenate([router_expert_b[0], router_group_b[0], jnp.zeros((r_pad,), f32)]).reshape(1, LANE)
    n2_row = norm2_w[0].reshape(1, D_MODEL)

    xp2 = x_prompt.reshape(n_p, D_MODEL)
    proj_p, tails = _inproj_conv(xp2, norm1_w[0], w_perm, cwa, tp)
    tiles_per_seq = tails.shape[0] // bp
    nca_p = tails.reshape(bp, tiles_per_seq, 8, CONV_PAD_W)[:, -1, 8 - (CONV_A - 1):8, 0:QKV_W]
    o_p, y_p, sd_p, ncb_p = _delta_prompt(proj_p.reshape(bp, tp, PROJ_W), cwb, alog_row, dtb_row,
                                          onw_row, nb_step=4 if bp % 4 == 0 else (2 if bp % 2 == 0 else 1))
    cnt0 = jnp.zeros((1, LANE), f32)
    x1_p, h2_p, mi_p, mw_p, cnt_p = _mix_route(xp2, o_p.reshape(n_p, QK_W), y_p.reshape(n_p, SC_W), proj_p,
                                               wa, wb, wo, n2_row, rwh, rwl, rb_row, cnt0)

    xs2 = x_sample.reshape(n_s, D_MODEL)
    proj_s = _inproj(xs2, norm1_w[0], w_perm)
    bufa_t = jnp.transpose(state_qkv_conv[0], (1, 0, 2))
    bufb_t = jnp.transpose(state_short_conv[0], (1, 0, 2))
    q_s, k_s, v_s, beta_s, eg_s, y_s, nbufa_t, nbufb_t = _sample_prep(proj_s, bufa_t, bufb_t, cwa, cwb,
                                                                      alog_row, dtb_row)
    h3 = lambda a: a.reshape(n_s, N_HEADS, HEAD)
    z_s = proj_s[:, COL_Z:COL_Z + QK_W]
    sd_s, o_s = _sample_step(state_delta[0], h3(q_s), h3(k_s), h3(v_s), h3(beta_s), h3(eg_s), h3(z_s), onw_row)
    o_s2 = o_s.reshape(n_s, QK_W).astype(bf16)
    x1_s, h2_s, mi_s, mw_s, cnt = _mix_route(xs2, o_s2, y_s, proj_s, wa, wb, wo, n2_row, rwh, rwl, rb_row, cnt_p)

    starts_row, items = _work_items(cnt, 2 * n_all)
    mi_flat = jnp.concatenate([_dest_rows(mi_p, starts_row)[:, 0:MI_W], _dest_rows(mi_s, starts_row)[:, 0:MI_W]],
                              axis=0).reshape(MI_W * n_all)
    xs_sorted = _dispatch(h2_p, h2_s, mi_flat)
    ys = _moe(xs_sorted.reshape(2 * n_all, D_MODEL), w_gate[0], w_up[0], w_down[0], items)
    y_prompt, y_sample = _combine(x1_p, mw_p, x1_s, mw_s, final_norm_w.reshape(1, D_MODEL),
                                  ys.reshape(2 * n_all // SUBLANE, SUBLANE, D_MODEL), mi_flat)

    return (y_prompt.reshape(bp, tp, D_MODEL),
            y_sample.reshape(bs, ts, D_MODEL),
            sd_p[None],
            nca_p[None],
            ncb_p[None],
            sd_s[None],
            jnp.transpose(nbufa_t, (1, 0, 2))[None],
            jnp.transpose(nbufb_t, (1, 0, 2))[None])
```

```python
import functools

import jax
import jax.numpy as jnp
from jax import lax
from jax.experimental import pallas as pl
from jax.experimental.pallas import tpu as pltpu

f32 = jnp.float32
bf16 = jnp.bfloat16
i32 = jnp.int32

EPS = 1e-6
LANE = 128
D_MODEL = 2048
N_HEADS = 8
HEAD = 128
QK_W = N_HEADS * HEAD
QKV_W = 3 * QK_W
SC_W = 1024
CONV_A = 4
CONV_B = 3
CHUNK = 64
GROUP_HEADS = 4
N_EXPERTS = 64
N_GROUPS = 8
EXPERTS_PER_GROUP = 8
D_FF = 512
MOE_ROWS = 128
TOKEN_TILE = 256
W_SLOTS = 2

COL_QKV = 0
COL_BCX = 3072
COL_GA = 6144
COL_GB = 8192
COL_Z = 10240
COL_BA = 11264
BA_W = 256
PROJ_W = 11520
PROJ_TN = 1280

VMEM_LIMIT = 56 * 1024 * 1024


def _dot(a, b):
    return jnp.dot(a, b, preferred_element_type=f32)


def _dot_nt(a, b):
    return lax.dot_general(a, b, (((1,), (1,)), ((), ())), preferred_element_type=f32)


def _split(x, n):
    parts = []
    r = x
    for i in range(n):
        p = r.astype(bf16)
        parts.append(p)
        if i + 1 < n:
            r = r - p.astype(f32)
    return parts


def _dot_lsplit(x, m, n=3):
    rows = x.shape[0]
    d = _dot(jnp.concatenate(_split(x, n), axis=0), m)
    acc = d[0:rows]
    for i in range(1, n):
        acc = acc + d[i * rows:(i + 1) * rows]
    return acc


def _dot_rsplit(m, x, n=3):
    cols = x.shape[1]
    d = _dot(m, jnp.concatenate(_split(x, n), axis=1))
    acc = d[:, 0:cols]
    for i in range(1, n):
        acc = acc + d[:, i * cols:(i + 1) * cols]
    return acc


_sigmoid = jax.nn.sigmoid


def _silu(x):
    return x * _sigmoid(x)


def _softplus(x):
    return jnp.maximum(x, 0.0) + jnp.log(1.0 + jnp.exp(-jnp.abs(x)))


def _cparams(sem):
    return pltpu.CompilerParams(dimension_semantics=sem, vmem_limit_bytes=VMEM_LIMIT)


def _inproj_kernel(x_ref, nw_ref, w_ref, o_ref, h_ref, *, rows):
    @pl.when(pl.program_id(1) == 0)
    def _():
        def body(r, c):
            sl = pl.ds(pl.multiple_of(r * rows, rows), rows)
            x = x_ref[sl, :]
            ms = jnp.mean(x * x, axis=-1, keepdims=True)
            h_ref[sl, :] = (x * lax.rsqrt(ms + EPS) * nw_ref[...]).astype(bf16)
            return c
        lax.fori_loop(0, x_ref.shape[0] // rows, body, 0)

    o_ref[...] = _dot_nt(h_ref[...], w_ref[...])


def _inproj(x2d, norm_w, w_bf16):
    n = x2d.shape[0]
    tm = min(1024, n)
    assert n % tm == 0 and PROJ_W % PROJ_TN == 0
    return pl.pallas_call(
        functools.partial(_inproj_kernel, rows=min(128, tm)),
        out_shape=jax.ShapeDtypeStruct((n, PROJ_W), f32),
        grid=(n // tm, PROJ_W // PROJ_TN),
        in_specs=[pl.BlockSpec((tm, D_MODEL), lambda i, j: (i, 0)),
                  pl.BlockSpec((1, D_MODEL), lambda i, j: (0, 0)),
                  pl.BlockSpec((PROJ_TN, D_MODEL), lambda i, j: (j, 0))],
        out_specs=pl.BlockSpec((tm, PROJ_TN), lambda i, j: (i, j)),
        scratch_shapes=[pltpu.VMEM((tm, D_MODEL), bf16)],
        compiler_params=_cparams(("arbitrary", "arbitrary")),
        name="inproj",
    )(x2d, norm_w.reshape(1, D_MODEL), w_bf16)


CONV_TILES = 3
CONV_COLS = 2 * HEAD
CONV_ROWS = 128
CONV_PAD_W = CONV_TILES * PROJ_TN


def _qkv_kind(col):
    return "q" if col < QK_W else "k" if col < 2 * QK_W else "v" if col < QKV_W else "raw"


def _inproj_conv_kernel(x_ref, nw_ref, w_ref, cw_ref, o_ref, tail_ref, h_ref, hist_ref, raw_ref, *,
                        rows, tiles_per_seq):
    i = pl.program_id(0)
    j = pl.program_id(1)
    tm = x_ref.shape[0]

    @pl.when(j == 0)
    def _():
        def body(r, c):
            sl = pl.ds(pl.multiple_of(r * rows, rows), rows)
            x = x_ref[sl, :]
            ms = jnp.mean(x * x, axis=-1, keepdims=True)
            h_ref[sl, :] = (x * lax.rsqrt(ms + EPS) * nw_ref[...]).astype(bf16)
            return c
        lax.fori_loop(0, tm // rows, body, 0)

    @pl.when((i == 0) & (j == 0))
    def _():
        hist_ref[...] = jnp.zeros(hist_ref.shape, f32)

    @pl.when(j >= CONV_TILES)
    def _():
        o_ref[...] = _dot_nt(h_ref[...], w_ref[...])

    seq_start = lax.rem(i, tiles_per_seq) == 0
    for jj in range(CONV_TILES):
        @pl.when(j == jj)
        def _():
            def matmul_chunk(idx, c0):
                raw_ref[idx % 2] = _dot_nt(h_ref[...], w_ref[c0:c0 + CONV_COLS, :])

            def conv_chunk(idx, c0):
                cs = slice(c0, c0 + CONV_COLS)
                raw = raw_ref.at[idx % 2]
                tail = raw[tm - 8:tm, :]
                tail_ref[0, :, cs] = tail
                kinds = [_qkv_kind(jj * PROJ_TN + c0 + g * HEAD) for g in range(CONV_COLS // HEAD)]
                if kinds[0] == "raw":
                    o_ref[:, cs] = raw[...]
                    return
                hist = jnp.where(seq_start, 0.0, hist_ref[jj, :, cs])
                for rc in range(tm // CONV_ROWS):
                    r0 = rc * CONV_ROWS
                    if rc > 0:
                        xe = raw[r0 - 8:r0 + CONV_ROWS, :]
                    else:
                        xe = jnp.concatenate([hist, raw[0:CONV_ROWS, :]], axis=0)
                    acc = pltpu.roll(xe, 3, axis=0)[8:] * cw_ref[0:1, cs]
                    acc = acc + pltpu.roll(xe, 2, axis=0)[8:] * cw_ref[1:2, cs]
                    acc = acc + pltpu.roll(xe, 1, axis=0)[8:] * cw_ref[2:3, cs]
                    acc = acc + xe[8:] * cw_ref[3:4, cs]
                    act = _silu(acc)
                    for g, kind in enumerate(kinds):
                        ah = act[:, g * HEAD:(g + 1) * HEAD]
                        if kind != "v":
                            ss = jnp.sum(ah * ah, axis=-1, keepdims=True)
                            inv = lax.rsqrt(ss + EPS)
                            ah = ah * (inv * (HEAD ** -0.5) if kind == "q" else inv)
                        o_ref[r0:r0 + CONV_ROWS, c0 + g * HEAD:c0 + (g + 1) * HEAD] = ah
                hist_ref[jj, :, cs] = tail

            chunks = list(range(0, PROJ_TN, CONV_COLS))
            matmul_chunk(0, chunks[0])
            for idx in range(1, len(chunks)):
                matmul_chunk(idx, chunks[idx])
                conv_chunk(idx - 1, chunks[idx - 1])
            conv_chunk(len(chunks) - 1, chunks[-1])


def _inproj_conv(x2d, norm_w, w_bf16, cwa, seq_len):
    n = x2d.shape[0]
    tm = min(1024, seq_len)
    assert n % tm == 0 and seq_len % tm == 0 and PROJ_W % PROJ_TN == 0 and tm % CONV_ROWS == 0
    assert QKV_W % CONV_COLS == 0 and PROJ_TN % CONV_COLS == 0
    cw_pad = jnp.zeros((CONV_A, CONV_PAD_W), f32).at[:, 0:QKV_W].set(cwa)
    last = CONV_TILES - 1
    return pl.pallas_call(
        functools.partial(_inproj_conv_kernel, rows=min(128, tm), tiles_per_seq=seq_len // tm),
        out_shape=(jax.ShapeDtypeStruct((n, PROJ_W), f32),
                   jax.ShapeDtypeStruct((n // tm, 8, CONV_PAD_W), f32)),
        grid=(n // tm, PROJ_W // PROJ_TN),
        in_specs=[pl.BlockSpec((tm, D_MODEL), lambda i, j: (i, 0)),
                  pl.BlockSpec((1, D_MODEL), lambda i, j: (0, 0)),
                  pl.BlockSpec((PROJ_TN, D_MODEL), lambda i, j: (j, 0)),
                  pl.BlockSpec((CONV_A, PROJ_TN), lambda i, j: (0, jnp.minimum(j, last)))],
        out_specs=(pl.BlockSpec((tm, PROJ_TN), lambda i, j: (i, j)),
                   pl.BlockSpec((1, 8, PROJ_TN), lambda i, j: (i, 0, jnp.minimum(j, last)))),
        scratch_shapes=[pltpu.VMEM((tm, D_MODEL), bf16), pltpu.VMEM((CONV_TILES, 8, PROJ_TN), f32),
                        pltpu.VMEM((2, tm, CONV_COLS), f32)],
        compiler_params=_cparams(("arbitrary", "arbitrary")),
        name="inproj_conv",
    )(x2d, norm_w.reshape(1, D_MODEL), w_bf16, cw_pad)


W_IN_COLS = 11280
WPREP_TN = 1024
WPREP_SHIFT = 16


def _wprep_kernel(a_ref, b_ref, o_ref):
    j = pl.program_id(0)
    keep = WPREP_TN - WPREP_SHIFT

    @pl.when((j < 3) | (j == 10))
    def _():
        o_ref[...] = a_ref[...].astype(bf16)

    @pl.when((j >= 3) & (j < 10))
    def _():
        o_ref[0:keep, :] = a_ref[WPREP_SHIFT:WPREP_TN, :].astype(bf16)
        o_ref[keep:WPREP_TN, :] = b_ref[...].astype(bf16)

    @pl.when(j == 11)
    def _():
        o_ref[0:WPREP_SHIFT, :] = a_ref[0:WPREP_SHIFT, :].astype(bf16)
        o_ref[WPREP_SHIFT:WPREP_TN, :] = jnp.zeros((keep, D_MODEL), bf16)


def _wprep(w_in_t):
    assert w_in_t.shape == (W_IN_COLS, D_MODEL) and 2 * N_HEADS == WPREP_SHIFT
    n_blk = pl.cdiv(PROJ_W, WPREP_TN)

    def a_map(j):
        return (jnp.where(j < 3, j, jnp.where(j < 10, j + 1, jnp.where(j == 10, 3, 4))), 0)

    def b_map(j):
        return (jnp.minimum((WPREP_TN // WPREP_SHIFT) * (j + 2), W_IN_COLS // WPREP_SHIFT - 1), 0)

    return pl.pallas_call(
        _wprep_kernel,
        out_shape=jax.ShapeDtypeStruct((PROJ_W, D_MODEL), bf16),
        grid=(n_blk,),
        in_specs=[pl.BlockSpec((WPREP_TN, D_MODEL), a_map),
                  pl.BlockSpec((WPREP_SHIFT, D_MODEL), b_map)],
        out_specs=pl.BlockSpec((WPREP_TN, D_MODEL), lambda j: (j, 0)),
        compiler_params=_cparams(("arbitrary",)),
        name="wprep",
    )(w_in_t, w_in_t)


def _head_l2norm(a, scale):
    outs = []
    for h in range(N_HEADS):
        ah = a[:, h * HEAD:(h + 1) * HEAD]
        ss = jnp.sum(ah * ah, axis=-1, keepdims=True)
        n = ah * lax.rsqrt(ss + EPS)
        outs.append(n * scale if scale != 1.0 else n)
    return outs


def _delta_prompt_kernel(qkv_ref, bcx_ref, z_ref, ba_ref, cwb_ref, alog_ref, dtb_ref, onw_ref,
                         e64_ref,
                         o_ref, y_ref, snew_ref, ncb_ref,
                         s_ref, xb_ref, *, nb_step):
    C = CHUNK
    G = GROUP_HEADS
    R = G * C
    t = pl.program_id(1)
    nt = pl.num_programs(1)

    @pl.when(t == 0)
    def _():
        s_ref[...] = jnp.zeros(s_ref.shape, f32)
        xb_ref[:, 0:8, :] = jnp.zeros((nb_step, 8, SC_W), f32)

    rr = lax.broadcasted_iota(i32, (R, R), 0)
    cc = lax.broadcasted_iota(i32, (R, R), 1)
    same_bf = jnp.where((rr >> 6) == (cc >> 6), 1.0, 0.0).astype(bf16)
    r2 = lax.broadcasted_iota(i32, (R, G * HEAD), 0)
    c2 = lax.broadcasted_iota(i32, (R, G * HEAD), 1)
    bdmask = (r2 >> 6) == (c2 >> 7)
    r3 = lax.broadcasted_iota(i32, (C, C), 0)
    c3 = lax.broadcasted_iota(i32, (C, C), 1)
    ltri = jnp.where(r3 >= c3, 1.0, 0.0).astype(bf16)
    r4 = lax.broadcasted_iota(i32, (C, R), 0)
    c4 = lax.broadcasted_iota(i32, (C, R), 1)
    ident_t = r4 == (c4 & (C - 1))
    incl_p = r4 >= (c4 & (C - 1))
    strict_p = r4 > (c4 & (C - 1))
    hblk = c4 >> 6
    ones8 = jnp.ones((8, C), bf16)

    nbs = range(nb_step)
    units = [(nb, g) for nb in nbs for g in range(N_HEADS // G)]
    heads = lambda g: range(g * G, (g + 1) * G)

    qn = [[qkv_ref[nb, :, h * HEAD:(h + 1) * HEAD] for h in range(N_HEADS)] for nb in nbs]
    kn = [[qkv_ref[nb, :, QK_W + h * HEAD:QK_W + (h + 1) * HEAD] for h in range(N_HEADS)] for nb in nbs]
    vv = [qkv_ref[nb, :, 2 * QK_W:3 * QK_W] for nb in nbs]

    bts = [ba_ref[nb, :, 0:LANE] for nb in nbs]
    beta_all = [_sigmoid(bt) for bt in bts]
    g_all = [-(jnp.exp(alog_ref[:, 0:LANE]) * _softplus(bt + dtb_ref[:, 0:LANE])) for bt in bts]
    gc_small = [_dot_rsplit(ltri, ga) for ga in g_all]
    gl_small = [gc[C - 1:C, :] for gc in gc_small]

    k_st, q_st, kb, vb, kbg, qd, kd, gc_col = ({} for _ in range(8))
    for u in units:
        nb, g = u
        hs = heads(g)
        k_st[u] = jnp.concatenate([kn[nb][h] for h in hs], axis=0)
        q_st[u] = jnp.concatenate([qn[nb][h] for h in hs], axis=0)
        v_st = jnp.concatenate([vv[nb][:, h * HEAD:(h + 1) * HEAD] for h in hs], axis=0)
        beta_col = jnp.concatenate([beta_all[nb][:, h:h + 1] for h in hs], axis=0)
        gc_col[u] = jnp.concatenate([gc_small[nb][:, 8 + h:9 + h] for h in hs], axis=0)
        gl_col = jnp.concatenate(
            [jnp.broadcast_to(gl_small[nb][:, 8 + h:9 + h], (C, 1)) for h in hs], axis=0)
        kb[u] = k_st[u] * beta_col
        vb[u] = v_st * beta_col
        egc = jnp.exp(gc_col[u])
        kbg[u] = kb[u] * egc
        qd[u] = q_st[u] * egc
        kd[u] = k_st[u] * jnp.exp(gl_col - gc_col[u])

    gx = {u: _dot_lsplit(gc_small[u[0]], e64_ref[u[1], 0:LANE, :]) for u in units}
    crow = {u: _dot_rsplit(ones8, jnp.where(ident_t, gx[u], 0.0))[0:1, :] for u in units}
    a = {u: _dot_nt(jnp.concatenate([kb[u], q_st[u]], axis=0).astype(bf16), k_st[u].astype(bf16))
         for u in units}
    in_blk = [hblk == h for h in range(G - 1)]

    def pack(x):
        out = x[(G - 1) * C:G * C]
        for h in reversed(range(G - 1)):
            out = jnp.where(in_blk[h], x[h * C:(h + 1) * C], out)
        return out

    def expand(xp):
        return jnp.concatenate([xp.astype(bf16)] * G, axis=0) * same_bf

    dec = {u: jnp.where(incl_p, jnp.exp(jnp.where(incl_p, gx[u] - crow[u], 0.0)), 0.0) for u in units}
    nm = {u: jnp.where(strict_p, -(pack(a[u][0:R]) * dec[u]), 0.0) for u in units}
    qkm = {u: expand(pack(a[u][R:2 * R]) * dec[u]) for u in units}

    p = {u: jnp.where(ident_t, 1.0, 0.0) + nm[u] for u in units}
    nk = {u: _dot(nm[u].astype(bf16), expand(nm[u])) for u in units}
    for _ in range(4):
        for u in units:
            x = _dot(jnp.concatenate([p[u], nk[u]], axis=0).astype(bf16), expand(nk[u]))
            p[u] = p[u] + x[0:C]
            nk[u] = x[C:2 * C]
    for u in units:
        p[u] = p[u] + _dot(p[u].astype(bf16), expand(nk[u]))
    uw = {u: _dot(expand(p[u]), jnp.concatenate([vb[u], kbg[u]], axis=1).astype(bf16)) for u in units}

    ws = {}
    for u in units:
        nb, g = u
        for j, h in enumerate(heads(g)):
            sh = s_ref[nb, :, h * HEAD:(h + 1) * HEAD]
            lhs = jnp.concatenate([uw[u][j * C:(j + 1) * C, HEAD:2 * HEAD], qd[u][j * C:(j + 1) * C]], axis=0)
            ws[u, j] = _dot(lhs.astype(bf16), sh.astype(bf16))
    o_heads = {}
    for u in units:
        nb, g = u
        vnew_st = jnp.concatenate([uw[u][j * C:(j + 1) * C, 0:HEAD] - ws[u, j][0:C] for j in range(G)], axis=0)
        o_st = (jnp.concatenate([ws[u, j][C:2 * C] for j in range(G)], axis=0)
                + _dot(qkm[u], vnew_st.astype(bf16)))
        vbd = jnp.where(bdmask, jnp.concatenate([vnew_st] * G, axis=1), 0.0)
        lo = g * G * HEAD
        hi = lo + G * HEAD
        gl_row = jnp.concatenate(
            [jnp.broadcast_to(jnp.exp(gl_small[nb][:, 8 + h:9 + h]), (1, HEAD)) for h in heads(g)], axis=1)
        s_ref[nb, :, lo:hi] = s_ref[nb, :, lo:hi] * gl_row + _dot(kd[u].T.astype(bf16), vbd.astype(bf16))
        for j, h in enumerate(heads(g)):
            o_heads[nb, h] = o_st[j * C:(j + 1) * C]

    for nb in nbs:
        zt = z_ref[nb]
        for h in range(N_HEADS):
            oh = o_heads[nb, h]
            ms = jnp.mean(oh * oh, axis=-1, keepdims=True)
            zh = zt[:, h * HEAD:(h + 1) * HEAD]
            on = oh * lax.rsqrt(ms + EPS) * onw_ref[...] * _silu(zh)
            o_ref[nb, :, h * HEAD:(h + 1) * HEAD] = on.astype(bf16)

    for nb in nbs:
        bcx = bcx_ref[nb]
        cx = bcx[:, SC_W:2 * SC_W] * bcx[:, 2 * SC_W:3 * SC_W]
        xb_ref[nb, 8:8 + C, :] = cx
        ce = xb_ref[nb]
        cv = pltpu.roll(ce, 2, axis=0)[8:8 + C] * cwb_ref[0:1, :]
        cv = cv + pltpu.roll(ce, 1, axis=0)[8:8 + C] * cwb_ref[1:2, :]
        cv = cv + cx * cwb_ref[2:3, :]
        y_ref[nb] = (bcx[:, 0:SC_W] * cv).astype(bf16)
        xb_ref[nb, 0:8, :] = xb_ref[nb, C:C + 8, :]

    @pl.when(t == nt - 1)
    def _():
        for nb in range(nb_step):
            for h in range(N_HEADS):
                snew_ref[nb, h] = s_ref[nb, :, h * HEAD:(h + 1) * HEAD]
            ncb_ref[nb] = xb_ref[nb, 6:8, :]


def _expand_consts():
    lane = jnp.arange(BA_W)[:, None]
    col = jnp.arange(QK_W)[None, :]
    eb = (lane == (col >> 7)).astype(bf16)
    eg = (lane == (8 + (col >> 7))).astype(bf16)
    col64 = jnp.arange(GROUP_HEADS * CHUNK)[None, :]
    e64 = jnp.stack([(lane == (8 + g * GROUP_HEADS + (col64 >> 6))).astype(bf16)
                     for g in range(N_HEADS // GROUP_HEADS)], axis=0)
    return eb, eg, e64


def _delta_prompt(proj3, cwb, alog_row, dtb_row, onw_row, nb_step):
    b, t, _ = proj3.shape
    assert t % CHUNK == 0 and b % nb_step == 0
    _, _, e64 = _expand_consts()
    c = CHUNK
    const2 = lambda bi, ti: (0, 0)
    outs = pl.pallas_call(
        functools.partial(_delta_prompt_kernel, nb_step=nb_step),
        out_shape=(jax.ShapeDtypeStruct((b, t, QK_W), bf16),
                   jax.ShapeDtypeStruct((b, t, SC_W), bf16),
                   jax.ShapeDtypeStruct((b, N_HEADS, HEAD, HEAD), f32),
                   jax.ShapeDtypeStruct((b, CONV_B - 1, SC_W), f32)),
        grid=(b // nb_step, t // c),
        in_specs=[pl.BlockSpec((nb_step, c, QKV_W), lambda bi, ti: (bi, ti, COL_QKV // QKV_W)),
                  pl.BlockSpec((nb_step, c, QKV_W), lambda bi, ti: (bi, ti, COL_BCX // QKV_W)),
                  pl.BlockSpec((nb_step, c, QK_W), lambda bi, ti: (bi, ti, COL_Z // QK_W)),
                  pl.BlockSpec((nb_step, c, BA_W), lambda bi, ti: (bi, ti, COL_BA // BA_W)),
                  pl.BlockSpec((CONV_B, SC_W), const2),
                  pl.BlockSpec((1, BA_W), const2),
                  pl.BlockSpec((1, BA_W), const2),
                  pl.BlockSpec((1, HEAD), const2),
                  pl.BlockSpec((N_HEADS // GROUP_HEADS, BA_W, GROUP_HEADS * CHUNK), lambda bi, ti: (0, 0, 0))],
        out_specs=(pl.BlockSpec((nb_step, c, QK_W), lambda bi, ti: (bi, ti, 0)),
                   pl.BlockSpec((nb_step, c, SC_W), lambda bi, ti: (bi, ti, 0)),
                   pl.BlockSpec((nb_step, N_HEADS, HEAD, HEAD), lambda bi, ti: (bi, 0, 0, 0)),
                   pl.BlockSpec((nb_step, CONV_B - 1, SC_W), lambda bi, ti: (bi, 0, 0))),
        scratch_shapes=[pltpu.VMEM((nb_step, HEAD, QK_W), f32),
                        pltpu.VMEM((nb_step, 8 + c, SC_W), f32)],
        compiler_params=_cparams(("arbitrary", "arbitrary")),
        name="delta_prompt",
    )(proj3, proj3, proj3, proj3, cwb, alog_row, dtb_row, onw_row, e64)
    return outs


def _sample_prep_kernel(p_ref, bufa_ref, bufb_ref, cwa_ref, cwb_ref, alog_ref, dtb_ref, eb_ref, eg_ref,
                        q_ref, k_ref, v_ref, beta_ref, eg_out_ref, y_ref, nbufa_ref, nbufb_ref):
    def conv_sec(lo):
        hi = lo + QK_W
        raw = p_ref[:, COL_QKV + lo:COL_QKV + hi]
        acc = bufa_ref[0, :, lo:hi] * cwa_ref[0:1, lo:hi]
        acc = acc + bufa_ref[1, :, lo:hi] * cwa_ref[1:2, lo:hi]
        acc = acc + bufa_ref[2, :, lo:hi] * cwa_ref[2:3, lo:hi]
        acc = acc + raw * cwa_ref[3:4, lo:hi]
        nbufa_ref[0, :, lo:hi] = bufa_ref[1, :, lo:hi]
        nbufa_ref[1, :, lo:hi] = bufa_ref[2, :, lo:hi]
        nbufa_ref[2, :, lo:hi] = raw
        return _silu(acc)

    qn = _head_l2norm(conv_sec(0), HEAD ** -0.5)
    kn = _head_l2norm(conv_sec(QK_W), 1.0)
    for h in range(N_HEADS):
        q_ref[:, h * HEAD:(h + 1) * HEAD] = qn[h]
        k_ref[:, h * HEAD:(h + 1) * HEAD] = kn[h]
    v_ref[...] = conv_sec(2 * QK_W)

    bt = p_ref[:, COL_BA:COL_BA + BA_W]
    beta_all = _sigmoid(bt)
    g_all = -(jnp.exp(alog_ref[...]) * _softplus(bt + dtb_ref[...]))
    beta_ref[...] = _dot_lsplit(beta_all, eb_ref[...])
    eg_out_ref[...] = jnp.exp(_dot_lsplit(g_all, eg_ref[...]))

    bg = p_ref[:, COL_BCX:COL_BCX + SC_W]
    cx = p_ref[:, COL_BCX + SC_W:COL_BCX + 2 * SC_W] * p_ref[:, COL_BCX + 2 * SC_W:COL_BCX + 3 * SC_W]
    cv = bufb_ref[0] * cwb_ref[0:1, :]
    cv = cv + bufb_ref[1] * cwb_ref[1:2, :]
    cv = cv + cx * cwb_ref[2:3, :]
    y_ref[...] = (bg * cv).astype(bf16)
    nbufb_ref[0] = bufb_ref[1]
    nbufb_ref[1] = cx


def _sample_prep(proj_s, bufa_t, bufb_t, cwa, cwb, alog_row, dtb_row):
    n = proj_s.shape[0]
    eb, eg, _ = _expand_consts()
    row = jax.ShapeDtypeStruct((n, QK_W), f32)
    return pl.pallas_call(
        _sample_prep_kernel,
        out_shape=(row, row, row, row, row,
                   jax.ShapeDtypeStruct((n, SC_W), bf16),
                   jax.ShapeDtypeStruct((CONV_A - 1, n, QKV_W), f32),
                   jax.ShapeDtypeStruct((CONV_B - 1, n, SC_W), f32)),
        compiler_params=pltpu.CompilerParams(vmem_limit_bytes=VMEM_LIMIT),
        name="sample_prep",
    )(proj_s, bufa_t, bufb_t, cwa, cwb, alog_row, dtb_row, eb, eg)


def _sample_step_kernel(s_ref, q_ref, k_ref, v_ref, beta_ref, eg_ref, z_ref, onw_ref,
                        snew_ref, o_ref, *, bb):
    w = N_HEADS * HEAD
    r8 = lax.broadcasted_iota(i32, (N_HEADS, w), 0)
    c8 = lax.broadcasted_iota(i32, (N_HEADS, w), 1)
    mask8 = r8 == (c8 >> 7)
    zpad_k = jnp.zeros((HEAD - N_HEADS, HEAD), f32)
    hb = lambda h: slice(h * HEAD, (h + 1) * HEAD)
    bs = range(bb)
    s_dec, k8s, kts = [], [], []
    for b in bs:
        s_all = jnp.concatenate([s_ref[b, h] for h in range(N_HEADS)], axis=1)
        eg8 = eg_ref[b]
        eg_row = jnp.concatenate([eg8[h:h + 1, :] for h in range(N_HEADS)], axis=1)
        s_dec.append(s_all * eg_row)
        k8s.append(k_ref[b])
        kts.append(jnp.concatenate([k8s[b], zpad_k], axis=0).T)
    xs = [_dot(k8s[b].astype(bf16), s_dec[b].astype(bf16)) for b in bs]
    s_new = []
    for b in bs:
        vb8, bt8 = v_ref[b], beta_ref[b]
        upd = [kts[b][:, h:h + 1] * ((vb8[h:h + 1, :] - xs[b][h:h + 1, hb(h)]) * bt8[h:h + 1, :])
               for h in range(N_HEADS)]
        s_new.append(s_dec[b] + jnp.concatenate(upd, axis=1))
    ys = [_dot(q_ref[b].astype(bf16), s_new[b].astype(bf16)) for b in bs]
    for b in bs:
        yv = jnp.where(mask8, ys[b], 0.0)
        o8 = yv[:, 0:HEAD]
        for j in range(1, N_HEADS):
            o8 = o8 + yv[:, j * HEAD:(j + 1) * HEAD]
        ms = jnp.mean(o8 * o8, axis=-1, keepdims=True)
        o_ref[b] = o8 * lax.rsqrt(ms + EPS) * onw_ref[...] * _silu(z_ref[b])
        for h in range(N_HEADS):
            snew_ref[b, h] = s_new[b][:, hb(h)]


def _sample_step(state, q, k, v, beta, eg, z, onw_row, bb=8):
    n = state.shape[0]
    assert n % bb == 0
    hspec = pl.BlockSpec((bb, N_HEADS, HEAD), lambda i: (i, 0, 0))
    sspec = pl.BlockSpec((bb, N_HEADS, HEAD, HEAD), lambda i: (i, 0, 0, 0))
    return pl.pallas_call(
        functools.partial(_sample_step_kernel, bb=bb),
        out_shape=(jax.ShapeDtypeStruct(state.shape, f32),
                   jax.ShapeDtypeStruct((n, N_HEADS, HEAD), f32)),
        grid=(n // bb,),
        in_specs=[sspec, hspec, hspec, hspec, hspec, hspec, hspec, pl.BlockSpec((1, HEAD), lambda i: (0, 0))],
        out_specs=(sspec, hspec),
        compiler_params=_cparams(("arbitrary",)),
        name="sample_step",
    )(state, q, k, v, beta, eg, z, onw_row)


def _mix_route_kernel(x_ref, o_ref, y_ref, ga_ref, gb_ref, wa_ref, wb_ref, wo_ref, n2_ref,
                      rwh_ref, rwl_ref, rb_ref, cnt_in_ref, x1_ref, h2_ref, mi_ref, mw_ref, cnt_ref):
    i = pl.program_id(0)
    tm = x_ref.shape[0]

    @pl.when(i == 0)
    def _():
        cnt_ref[...] = cnt_in_ref[...]

    oa = _dot(o_ref[...], wa_ref[...])
    ob = _dot(y_ref[...], wb_ref[...])
    merged = _sigmoid(ga_ref[...]) * oa + _sigmoid(gb_ref[...]) * ob
    x1 = x_ref[...] + _dot(merged.astype(bf16), wo_ref[...])
    x1_ref[...] = x1
    ms = jnp.mean(x1 * x1, axis=-1, keepdims=True)
    h2 = x1 * lax.rsqrt(ms + EPS) * n2_ref[...]
    h2_ref[...] = h2

    h_hi, h_lo = _split(h2, 2)
    logits = _dot(h_hi, rwh_ref[...]) + _dot(h_hi, rwl_ref[...]) + _dot(h_lo, rwh_ref[...]) + rb_ref[...]

    lane = lax.broadcasted_iota(i32, (tm, LANE), 1)
    lanef = lane.astype(f32)
    neg = jnp.float32(-jnp.inf)
    big = jnp.float32(1e9)
    gmask = (lane >= N_EXPERTS) & (lane < N_EXPERTS + N_GROUPS)
    gl = jnp.where(gmask, logits, neg)
    gmax = jnp.max(gl, axis=-1, keepdims=True)
    gidx = jnp.min(jnp.where(gl == gmax, lanef - N_EXPERTS, big), axis=-1, keepdims=True)
    gsum = jnp.sum(jnp.where(gmask, jnp.exp(gl - gmax), 0.0), axis=-1, keepdims=True)
    gprob = 1.0 / gsum

    emask = (lane < N_EXPERTS) & ((lane >> 3).astype(f32) == gidx)
    el = jnp.where(emask, logits, neg)
    emax = jnp.max(el, axis=-1, keepdims=True)
    pe = jnp.where(emask, jnp.exp(el - emax), 0.0)
    eprob = pe / jnp.sum(pe, axis=-1, keepdims=True)
    p1m = jnp.where(emask, eprob, -1.0)
    m1 = jnp.max(p1m, axis=-1, keepdims=True)
    i1 = jnp.min(jnp.where(p1m == m1, lanef, big), axis=-1, keepdims=True)
    p2m = jnp.where(lanef == i1, -1.0, p1m)
    m2 = jnp.max(p2m, axis=-1, keepdims=True)
    i2 = jnp.min(jnp.where(p2m == m2, lanef, big), axis=-1, keepdims=True)
    tot = m1 + m2
    c1 = m1 / tot * gprob
    c2 = m2 / tot * gprob

    oh1 = jnp.where(lanef == i1, 1.0, 0.0)
    oh2 = jnp.where(lanef == i2, 1.0, 0.0)
    ohs = oh1 + oh2
    rt = lax.broadcasted_iota(i32, (tm, tm), 0)
    ct = lax.broadcasted_iota(i32, (tm, tm), 1)
    lstrict = jnp.where(rt > ct, 1.0, 0.0).astype(bf16)
    cs = _dot(lstrict, ohs.astype(bf16)) + cnt_ref[...]
    rank1 = jnp.sum(cs * oh1, axis=-1, keepdims=True)
    rank2 = jnp.sum(cs * oh2, axis=-1, keepdims=True)
    cnt_ref[...] = cnt_ref[...] + jnp.sum(ohs, axis=0, keepdims=True)

    mi = jnp.where(lane == 0, i1, jnp.where(lane == 1, i2, jnp.where(lane == 2, rank1,
                                                                     jnp.where(lane == 3, rank2, 0.0))))
    mi_ref[...] = mi.astype(i32)
    mw_ref[...] = jnp.where(lane == 0, c1, jnp.where(lane == 1, c2, 0.0))


def _mix_route(x2d, o2d, y2d, proj2d, wa, wb, wo, n2_row, rwh, rwl, rb_row, cnt_in):
    n = x2d.shape[0]
    tm = min(256, n)
    assert n % tm == 0
    tok = lambda width: pl.BlockSpec((tm, width), lambda i: (i, 0))
    full = lambda a: pl.BlockSpec(a.shape, lambda i: (0,) * a.ndim)
    in_specs = [tok(D_MODEL), tok(QK_W), tok(SC_W),
                pl.BlockSpec((tm, D_MODEL), lambda i: (i, COL_GA // D_MODEL)),
                pl.BlockSpec((tm, D_MODEL), lambda i: (i, COL_GB // D_MODEL)),
                full(wa), full(wb), full(wo), full(n2_row), full(rwh), full(rwl), full(rb_row), full(cnt_in)]
    out_shape = (jax.ShapeDtypeStruct((n, D_MODEL), f32),
                 jax.ShapeDtypeStruct((n, D_MODEL), f32),
                 jax.ShapeDtypeStruct((n, LANE), i32),
                 jax.ShapeDtypeStruct((n, LANE), f32),
                 jax.ShapeDtypeStruct((1, LANE), f32))
    out_specs = (tok(D_MODEL), tok(D_MODEL), tok(LANE), tok(LANE),
                 pl.BlockSpec((1, LANE), lambda i: (0, 0)))
    return pl.pallas_call(
        _mix_route_kernel,
        out_shape=out_shape,
        grid=(n // tm,),
        in_specs=in_specs,
        out_specs=out_specs,
        compiler_params=_cparams(("arbitrary",)),
        name="mix_route",
    )(x2d, o2d, y2d, proj2d, proj2d, wa, wb, wo, n2_row, rwh, rwl, rb_row, cnt_in)


MI_W = 4
SUBLANE = 8
ROW_DMA_UNROLL = 8


def _dest_kernel(mi_ref, starts_ref, o_ref):
    mi = mi_ref[...]
    lane = lax.broadcasted_iota(i32, mi.shape, 1)
    st = starts_ref[...]

    def first_row(e_col):
        return jnp.sum(jnp.where(lane == e_col, st, 0.0), axis=-1, keepdims=True).astype(i32)

    d0 = first_row(mi[:, 0:1]) + mi[:, 2:3]
    d1 = first_row(mi[:, 1:2]) + mi[:, 3:4]
    o_ref[...] = jnp.where(lane == 0, d0 >> 3, jnp.where(lane == 1, d0 & (SUBLANE - 1),
                           jnp.where(lane == 2, d1 >> 3, jnp.where(lane == 3, d1 & (SUBLANE - 1), 0))))


def _dest_rows(mi, starts_row):
    n = mi.shape[0]
    tm = min(1024, n)
    assert n % tm == 0
    return pl.pallas_call(
        _dest_kernel,
        out_shape=jax.ShapeDtypeStruct((n, LANE), i32),
        grid=(n // tm,),
        in_specs=[pl.BlockSpec((tm, LANE), lambda i: (i, 0)), pl.BlockSpec((1, LANE), lambda i: (0, 0))],
        out_specs=pl.BlockSpec((tm, LANE), lambda i: (i, 0)),
        compiler_params=_cparams(("arbitrary",)),
        name="moe_dest",
    )(mi, starts_row)


def _dispatch_kernel(mi_ref, hp_ref, hs_ref, xs_ref, sem, *, np_tiles):
    i = pl.program_id(0)

    def scatter_rows(h_ref):
        n_tiles = h_ref.shape[0]

        def start(t, c):
            for u in range(SUBLANE):
                rec = MI_W * (SUBLANE * t + u)
                for k in range(2):
                    dst = xs_ref.at[mi_ref[rec + 2 * k], pl.ds(mi_ref[rec + 2 * k + 1], 1)]
                    pltpu.make_async_copy(h_ref.at[t, pl.ds(u, 1)], dst, sem).start(priority=k)
            return c

        lax.fori_loop(0, n_tiles, start, 0)
        for k in range(2):
            pltpu.make_async_copy(h_ref, xs_ref.at[pl.ds(0, n_tiles)], sem).wait()

    @pl.when(i < np_tiles)
    def _():
        scatter_rows(hp_ref)

    @pl.when(i >= np_tiles)
    def _():
        scatter_rows(hs_ref)


def _dispatch(h2_p, h2_s, mi_flat):
    tm = TOKEN_TILE
    n_p, n_s = h2_p.shape[0], h2_s.shape[0]
    assert n_p % tm == 0 and n_s <= tm and n_s % SUBLANE == 0 and tm % SUBLANE == 0
    np_tiles = n_p // tm
    tiled = lambda a: a.reshape(a.shape[0] // SUBLANE, SUBLANE, D_MODEL)
    return pl.pallas_call(
        functools.partial(_dispatch_kernel, np_tiles=np_tiles),
        out_shape=jax.ShapeDtypeStruct((2 * (n_p + n_s) // SUBLANE, SUBLANE, D_MODEL), f32),
        grid=(np_tiles + 1,),
        in_specs=[pl.BlockSpec((MI_W * tm,), lambda i: (i,), memory_space=pltpu.SMEM),
                  pl.BlockSpec((tm // SUBLANE, SUBLANE, D_MODEL), lambda i: (jnp.minimum(i, np_tiles - 1), 0, 0)),
                  pl.BlockSpec((n_s // SUBLANE, SUBLANE, D_MODEL), lambda i: (0, 0, 0))],
        out_specs=pl.BlockSpec(memory_space=pl.ANY),
        scratch_shapes=[pltpu.SemaphoreType.DMA(())],
        compiler_params=_cparams(("arbitrary",)),
        name="moe_dispatch",
    )(mi_flat, tiled(h2_p), tiled(h2_s))


def _cast_rows(src_ref, dst_ref, col0=0, rows=256):
    width = src_ref.shape[1]

    def body(r, c):
        sl = pl.ds(pl.multiple_of(r * rows, rows), rows)
        dst_ref[sl, col0:col0 + width] = src_ref[sl, :].astype(bf16)
        return c
    lax.fori_loop(0, src_ref.shape[0] // rows, body, 0)


def _moe_kernel(blk_ref, lo_ref, hi_ref, first_ref, newe_ref, slot_ref, pre_ref, init_ref,
                x_ref, wg_hbm, wu_hbm, wd_hbm, o_ref,
                wg_f, wu_f, wd_f, wgu_b, wd_b, sem):
    i = pl.program_id(0)
    lo = lo_ref[i]
    hi = hi_ref[i]

    def weight_copies(e, slot):
        return [pltpu.make_async_copy(wg_hbm.at[e], wg_f.at[slot], sem.at[slot, 0]),
                pltpu.make_async_copy(wu_hbm.at[e], wu_f.at[slot], sem.at[slot, 1]),
                pltpu.make_async_copy(wd_hbm.at[e], wd_f.at[slot], sem.at[slot, 2])]

    def start_weights(e, slot):
        for cp, prio in zip(weight_copies(e, slot), (0, 1, 1)):
            cp.start(priority=prio)

    @pl.when(i == 0)
    def _():
        start_weights(init_ref[0], 0)
        for k in range(1, W_SLOTS):
            @pl.when(init_ref[k] >= 0)
            def _():
                start_weights(init_ref[k], k)

    @pl.when(newe_ref[i] == 1)
    def _():
        slot = slot_ref[i]
        cg, cu, cd = weight_copies(0, slot)
        cg.wait()
        _cast_rows(wg_f.at[slot], wgu_b, 0)
        cu.wait()
        _cast_rows(wu_f.at[slot], wgu_b, D_FF)
        cd.wait()
        _cast_rows(wd_f.at[slot], wd_b)

        @pl.when(pre_ref[i] >= 0)
        def _():
            start_weights(pre_ref[i], slot)

    @pl.when(hi > lo)
    def _():
        x = x_ref[...].astype(bf16)
        au = _dot(x, wgu_b[...])
        y = _dot((_silu(au[:, 0:D_FF]) * au[:, D_FF:2 * D_FF]).astype(bf16), wd_b[...])
        row = lax.broadcasted_iota(i32, y.shape, 0)
        ym = jnp.where((row >= lo) & (row < hi), y, 0.0)

        @pl.when(first_ref[i] == 1)
        def _():
            o_ref[...] = ym

        @pl.when(first_ref[i] == 0)
        def _():
            o_ref[...] = o_ref[...] + ym


def _moe(xs, w_gate, w_up, w_down, items):
    n_items = items[0].shape[0]
    rows = xs.shape[0]
    n_pref = len(items)
    xmap = lambda i, blk, *_: (blk[i], 0)
    grid_spec = pltpu.PrefetchScalarGridSpec(
        num_scalar_prefetch=n_pref,
        grid=(n_items,),
        in_specs=[pl.BlockSpec((MOE_ROWS, D_MODEL), xmap),
                  pl.BlockSpec(memory_space=pl.ANY),
                  pl.BlockSpec(memory_space=pl.ANY),
                  pl.BlockSpec(memory_space=pl.ANY)],
        out_specs=pl.BlockSpec((MOE_ROWS, D_MODEL), xmap),
        scratch_shapes=[pltpu.VMEM((W_SLOTS, D_MODEL, D_FF), f32), pltpu.VMEM((W_SLOTS, D_MODEL, D_FF), f32),
                        pltpu.VMEM((W_SLOTS, D_FF, D_MODEL), f32),
                        pltpu.VMEM((D_MODEL, 2 * D_FF), bf16), pltpu.VMEM((D_FF, D_MODEL), bf16),
                        pltpu.SemaphoreType.DMA((W_SLOTS, 3))],
    )
    return pl.pallas_call(
        _moe_kernel,
        out_shape=jax.ShapeDtypeStruct((rows, D_MODEL), f32),
        grid_spec=grid_spec,
        compiler_params=_cparams(("arbitrary",)),
        name="moe_experts",
    )(*items, xs, w_gate, w_up, w_down)


def _combine_kernel(mi_ref, mi_next_ref, x1p_ref, mwp_ref, x1s_ref, mws_ref, fnw_ref, ys_ref,
                    yp_ref, ysm_ref, g_ref, sem, *, np_tiles):
    i = pl.program_id(0)
    tiles_p = x1p_ref.shape[0] // SUBLANE
    tiles_s = x1s_ref.shape[0] // SUBLANE
    slot = lax.rem(i, 2)

    def gather_rows(m_ref, dst_slot, n_tiles):
        def body(t, c):
            for u in range(SUBLANE):
                rec = MI_W * (SUBLANE * t + u)
                for k in range(2):
                    src = ys_ref.at[m_ref[rec + 2 * k], pl.ds(m_ref[rec + 2 * k + 1], 1)]
                    pltpu.make_async_copy(src, g_ref.at[dst_slot, k, t, pl.ds(u, 1)],
                                          sem.at[dst_slot]).start(priority=k)
            return c
        lax.fori_loop(0, n_tiles, body, 0)

    @pl.when(i == 0)
    def _():
        gather_rows(mi_ref, 0, tiles_p)

    @pl.when(i + 1 < np_tiles)
    def _():
        gather_rows(mi_next_ref, 1 - slot, tiles_p)

    @pl.when(i + 1 == np_tiles)
    def _():
        gather_rows(mi_next_ref, 1 - slot, tiles_s)

    def finish(x1_ref, mw_ref, out_ref, n_tiles):
        for k in range(2):
            pltpu.make_async_copy(ys_ref.at[pl.ds(0, n_tiles)], g_ref.at[slot, k, pl.ds(0, n_tiles)],
                                  sem.at[slot]).wait()
        rows = n_tiles * SUBLANE
        mw = mw_ref[...]
        g0 = g_ref[slot, 0, 0:n_tiles].reshape(rows, D_MODEL)
        g1 = g_ref[slot, 1, 0:n_tiles].reshape(rows, D_MODEL)
        x2 = x1_ref[...] + (g0 * mw[:, 0:1] + g1 * mw[:, 1:2])
        ms = jnp.mean(x2 * x2, axis=-1, keepdims=True)
        out_ref[...] = x2 * lax.rsqrt(ms + EPS) * fnw_ref[...]

    @pl.when(i < np_tiles)
    def _():
        finish(x1p_ref, mwp_ref, yp_ref, tiles_p)

    @pl.when(i >= np_tiles)
    def _():
        finish(x1s_ref, mws_ref, ysm_ref, tiles_s)


def _combine(x1_p, mw_p, x1_s, mw_s, fnw_row, ys3, mi_flat):
    tm = TOKEN_TILE
    n_p, n_s = x1_p.shape[0], x1_s.shape[0]
    assert n_p % tm == 0 and n_s <= tm and n_s % SUBLANE == 0
    np_tiles = n_p // tm
    ptile = lambda width: pl.BlockSpec((tm, width), lambda i: (jnp.minimum(i, np_tiles - 1), 0))
    stile = lambda width: pl.BlockSpec((n_s, width), lambda i: (0, 0))
    return pl.pallas_call(
        functools.partial(_combine_kernel, np_tiles=np_tiles),
        out_shape=(jax.ShapeDtypeStruct((n_p, D_MODEL), f32),
                   jax.ShapeDtypeStruct((n_s, D_MODEL), f32)),
        grid=(np_tiles + 1,),
        in_specs=[pl.BlockSpec((MI_W * tm,), lambda i: (i,), memory_space=pltpu.SMEM),
                  pl.BlockSpec((MI_W * tm,), lambda i: (jnp.minimum(i + 1, np_tiles),), memory_space=pltpu.SMEM),
                  ptile(D_MODEL), ptile(LANE), stile(D_MODEL), stile(LANE),
                  pl.BlockSpec((1, D_MODEL), lambda i: (0, 0)),
                  pl.BlockSpec(memory_space=pl.ANY)],
        out_specs=(ptile(D_MODEL), stile(D_MODEL)),
        scratch_shapes=[pltpu.VMEM((2, 2, tm // SUBLANE, SUBLANE, D_MODEL), f32), pltpu.SemaphoreType.DMA((2,))],
        compiler_params=_cparams(("arbitrary",)),
        name="moe_combine",
    )(mi_flat, mi_flat, x1_p, mw_p, x1_s, mw_s, fnw_row, ys3)


PLAN_ROWS = 256
N_ITEM_FIELDS = 7


def _plan_kernel(cnt_ref, items_ref, rows_ref, *, nblk):
    cnt = cnt_ref[...]
    lane1 = lax.broadcasted_iota(i32, (1, LANE), 1)
    in_e = lane1 < N_EXPERTS
    ri = lax.broadcasted_iota(i32, (LANE, LANE), 0)
    ci = lax.broadcasted_iota(i32, (LANE, LANE), 1)
    upper = jnp.where(ri <= ci, 1.0, 0.0).astype(bf16)

    def cumsum_lanes(v):
        return _dot_lsplit(jnp.broadcast_to(v, (8, LANE)), upper)[0:1, :]

    shift = MOE_ROWS.bit_length() - 1
    ends = cumsum_lanes(cnt)
    starts = ends - cnt
    act = cnt > 0.0
    first_blk = (starts.astype(i32) >> shift).astype(f32)
    last_blk = (jnp.maximum(ends - 1.0, 0.0).astype(i32) >> shift).astype(f32)
    nvis = jnp.where(act, last_blk - first_blk + 1.0, 0.0)
    vis_end = cumsum_lanes(nvis)
    vis_start = vis_end - nvis
    total = jnp.max(vis_end, axis=-1, keepdims=True)
    cum_act = cumsum_lanes(jnp.where(act, 1.0, 0.0))
    n_uniq = jnp.max(cum_act, axis=-1, keepdims=True)

    p = PLAN_ROWS
    lane = lax.broadcasted_iota(i32, (p, LANE), 1)
    idx = lax.broadcasted_iota(i32, (p, LANE), 0).astype(f32)
    idx1 = idx[:, 0:1]
    count_le = lambda row, col: jnp.sum(jnp.where((row <= col) & in_e, 1.0, 0.0), axis=-1, keepdims=True)
    e = jnp.minimum(count_le(vis_end, idx), N_EXPERTS - 1.0)
    onehot = lane.astype(f32) == e
    look = lambda tbl: jnp.sum(jnp.where(onehot, tbl, 0.0), axis=-1, keepdims=True)
    blk = look(first_blk) + idx1 - look(vis_start)
    lo = jnp.maximum(look(starts), blk * MOE_ROWS) - blk * MOE_ROWS
    hi = jnp.minimum(look(ends), (blk + 1.0) * MOE_ROWS) - blk * MOE_ROWS
    valid = idx1 < total
    blk = jnp.where(valid, blk, nblk - 1.0)
    lo = jnp.where(valid, lo, 0.0)
    hi = jnp.where(valid, hi, 0.0)
    rep = lambda c: jnp.broadcast_to(c, (p, LANE))
    prev = lambda c: pltpu.roll(rep(c), 1, axis=0)[:, 0:1]
    is0 = idx1 == 0.0
    first = valid & (is0 | (blk != prev(blk)))
    newe = valid & (is0 | (e != prev(e)))
    rp = lax.broadcasted_iota(i32, (p, p), 0)
    cp = lax.broadcasted_iota(i32, (p, p), 1)
    lower = jnp.where(rp >= cp, 1.0, 0.0).astype(bf16)
    order = _dot(lower, rep(jnp.where(newe, 1.0, 0.0)).astype(bf16))[:, 0:1] - 1.0
    slot = jnp.where(newe, order - W_SLOTS * jnp.floor((order + 0.5) * (1.0 / W_SLOTS)), 0.0)
    k2 = order + float(W_SLOTS)
    pre = jnp.where(newe & (k2 < n_uniq), count_le(cum_act, rep(k2)), -1.0)
    out = jnp.zeros((p, LANE), f32)
    for c, v in enumerate([blk, lo, hi, jnp.where(first, 1.0, 0.0), jnp.where(newe, 1.0, 0.0), slot, pre]):
        out = jnp.where(lane == c, v, out)
    items_ref[...] = out.astype(i32)

    init_row = jnp.zeros((1, LANE), f32)
    for k in range(W_SLOTS):
        init_row = jnp.where(lane1 == k, jnp.where(n_uniq > float(k), count_le(cum_act, float(k)), -1.0), init_row)
    rows_ref[...] = jnp.zeros(rows_ref.shape, f32)
    rows_ref[0:1, :] = starts
    rows_ref[1:2, :] = init_row


def _work_items(cnt_row, n_rows):
    nblk = n_rows // MOE_ROWS
    n_items = nblk + N_EXPERTS - 1
    assert n_items <= PLAN_ROWS and n_rows % MOE_ROWS == 0
    items, rows = pl.pallas_call(
        functools.partial(_plan_kernel, nblk=nblk),
        out_shape=(jax.ShapeDtypeStruct((PLAN_ROWS, LANE), i32), jax.ShapeDtypeStruct((8, LANE), f32)),
        compiler_params=pltpu.CompilerParams(vmem_limit_bytes=VMEM_LIMIT),
        name="moe_plan",
    )(cnt_row)
    fields = tuple(items[0:n_items, c] for c in range(N_ITEM_FIELDS))
    return rows[0:1, :], fields + (rows[1, 0:W_SLOTS].astype(i32),)


def kernel(x_prompt, x_sample, state_delta, state_qkv_conv, state_short_conv, norm1_w, w_in, conv_a_w, a_log, dt_bias, out_norm_w, w_branch_a, conv_b_w, w_branch_b, w_o, norm2_w, router_group_w, router_group_b, router_expert_w, router_expert_b, w_gate, w_up, w_down, final_norm_w):
    assert norm1_w.shape[0] == 1, "single-layer trunk"
    bp, tp, d = x_prompt.shape
    bs, ts, _ = x_sample.shape
    assert d == D_MODEL and ts == 1
    n_p = bp * tp
    n_s = bs
    n_all = n_p + n_s

    w_perm = _wprep(jnp.transpose(w_in[0]))
    wa = w_branch_a[0].astype(bf16)
    wb = w_branch_b[0].astype(bf16)
    wo = w_o[0].astype(bf16)
    pad = lambda v: jnp.zeros((1, BA_W), f32).at[0, N_HEADS:2 * N_HEADS].set(v)
    alog_row = pad(a_log[0])
    dtb_row = pad(dt_bias[0])
    onw_row = out_norm_w[0].reshape(1, HEAD)
    cwa = conv_a_w[0]
    cwb = conv_b_w[0]
    r_pad = LANE - N_EXPERTS - N_GROUPS
    rw = jnp.concatenate([router_expert_w[0], router_group_w[0], jnp.zeros((D_MODEL, r_pad), f32)], axis=1)
    rwh = rw.astype(bf16)
    rwl = (rw - rwh.astype(f32)).astype(bf16)
    rb_row = jnp.concatenate([router_expert_b[0], router_group_b[0], jnp.zeros((r_pad,), f32)]).reshape(1, LANE)
    n2_row = norm2_w[0].reshape(1, D_MODEL)

    xp2 = x_prompt.reshape(n_p, D_MODEL)
    proj_p, tails = _inproj_conv(xp2, norm1_w[0], w_perm, cwa, tp)
    tiles_per_seq = tails.shape[0] // bp
    nca_p = tails.reshape(bp, tiles_per_seq, 8, CONV_PAD_W)[:, -1, 8 - (CONV_A - 1):8, 0:QKV_W]
    o_p, y_p, sd_p, ncb_p = _delta_prompt(proj_p.reshape(bp, tp, PROJ_W), cwb, alog_row, dtb_row,
                                          onw_row, nb_step=4 if bp % 4 == 0 else (2 if bp % 2 == 0 else 1))
    cnt0 = jnp.zeros((1, LANE), f32)
    x1_p, h2_p, mi_p, mw_p, cnt_p = _mix_route(xp2, o_p.reshape(n_p, QK_W), y_p.reshape(n_p, SC_W), proj_p,
                                               wa, wb, wo, n2_row, rwh, rwl, rb_row, cnt0)

    xs2 = x_sample.reshape(n_s, D_MODEL)
    proj_s = _inproj(xs2, norm1_w[0], w_perm)
    bufa_t = jnp.transpose(state_qkv_conv[0], (1, 0, 2))
    bufb_t = jnp.transpose(state_short_conv[0], (1, 0, 2))
    q_s, k_s, v_s, beta_s, eg_s, y_s, nbufa_t, nbufb_t = _sample_prep(proj_s, bufa_t, bufb_t, cwa, cwb,
                                                                      alog_row, dtb_row)
    h3 = lambda a: a.reshape(n_s, N_HEADS, HEAD)
    z_s = proj_s[:, COL_Z:COL_Z + QK_W]
    sd_s, o_s = _sample_step(state_delta[0], h3(q_s), h3(k_s), h3(v_s), h3(beta_s), h3(eg_s), h3(z_s), onw_row)
    o_s2 = o_s.reshape(n_s, QK_W).astype(bf16)
    x1_s, h2_s, mi_s, mw_s, cnt = _mix_route(xs2, o_s2, y_s, proj_s, wa, wb, wo, n2_row, rwh, rwl, rb_row, cnt_p)

    starts_row, items = _work_items(cnt, 2 * n_all)
    mi_flat = jnp.concatenate([_dest_rows(mi_p, starts_row)[:, 0:MI_W], _dest_rows(mi_s, starts_row)[:, 0:MI_W]],
                              axis=0).reshape(MI_W * n_all)
    xs_sorted = _dispatch(h2_p, h2_s, mi_flat)
    ys = _moe(xs_sorted.reshape(2 * n_all, D_MODEL), w_gate[0], w_up[0], w_down[0], items)
    y_prompt, y_sample = _combine(x1_p, mw_p, x1_s, mw_s, final_norm_w.reshape(1, D_MODEL),
                                  ys.reshape(2 * n_all // SUBLANE, SUBLANE, D_MODEL), mi_flat)

    return (y_prompt.reshape(bp, tp, D_MODEL),
            y_sample.reshape(bs, ts, D_MODEL),
            sd_p[None],
            nca_p[None],
            ncb_p[None],
            sd_s[None],
            jnp.transpose(nbufa_t, (1, 0, 2))[None],
            jnp.transpose(nbufb_t, (1, 0, 2))[None])
```

```python
import functools

import jax
import jax.numpy as jnp
from jax import lax
from jax.experimental import pallas as pl
from jax.experimental.pallas import tpu as pltpu

f32 = jnp.float32
bf16 = jnp.bfloat16
i32 = jnp.int32

EPS = 1e-6
LANE = 128
D_MODEL = 2048
N_HEADS = 8
HEAD = 128
QK_W = N_HEADS * HEAD
QKV_W = 3 * QK_W
SC_W = 1024
CONV_A = 4
CONV_B = 3
CHUNK = 64
GROUP_HEADS = 4
N_EXPERTS = 64
N_GROUPS = 8
EXPERTS_PER_GROUP = 8
D_FF = 512
MOE_ROWS = 128
TOKEN_TILE = 512
W_SLOTS = 2

COL_QKV = 0
COL_BCX = 3072
COL_GA = 6144
COL_GB = 8192
COL_Z = 10240
COL_BA = 11264
BA_W = 256
PROJ_W = 11520
PROJ_TN = 1280

VMEM_LIMIT = 56 * 1024 * 1024


def _dot(a, b):
    return jnp.dot(a, b, preferred_element_type=f32)


def _dot_nt(a, b):
    return lax.dot_general(a, b, (((1,), (1,)), ((), ())), preferred_element_type=f32)


def _split(x, n):
    parts = []
    r = x
    for i in range(n):
        p = r.astype(bf16)
        parts.append(p)
        if i + 1 < n:
            r = r - p.astype(f32)
    return parts


def _dot_lsplit(x, m, n=3):
    rows = x.shape[0]
    d = _dot(jnp.concatenate(_split(x, n), axis=0), m)
    acc = d[0:rows]
    for i in range(1, n):
        acc = acc + d[i * rows:(i + 1) * rows]
    return acc


def _dot_rsplit(m, x, n=3):
    cols = x.shape[1]
    d = _dot(m, jnp.concatenate(_split(x, n), axis=1))
    acc = d[:, 0:cols]
    for i in range(1, n):
        acc = acc + d[:, i * cols:(i + 1) * cols]
    return acc


_sigmoid = jax.nn.sigmoid


def _silu(x):
    return x * _sigmoid(x)


def _softplus(x):
    return jnp.maximum(x, 0.0) + jnp.log(1.0 + jnp.exp(-jnp.abs(x)))


def _cparams(sem):
    return pltpu.CompilerParams(dimension_semantics=sem, vmem_limit_bytes=VMEM_LIMIT)


def _inproj_kernel(x_ref, nw_ref, w_ref, o_ref, h_ref, *, rows):
    @pl.when(pl.program_id(1) == 0)
    def _():
        def body(r, c):
            sl = pl.ds(pl.multiple_of(r * rows, rows), rows)
            x = x_ref[sl, :]
            ms = jnp.mean(x * x, axis=-1, keepdims=True)
            h_ref[sl, :] = (x * lax.rsqrt(ms + EPS) * nw_ref[...]).astype(bf16)
            return c
        lax.fori_loop(0, x_ref.shape[0] // rows, body, 0)

    o_ref[...] = _dot_nt(h_ref[...], w_ref[...])


def _inproj(x2d, norm_w, w_bf16):
    n = x2d.shape[0]
    tm = min(1024, n)
    assert n % tm == 0 and PROJ_W % PROJ_TN == 0
    return pl.pallas_call(
        functools.partial(_inproj_kernel, rows=min(128, tm)),
        out_shape=jax.ShapeDtypeStruct((n, PROJ_W), f32),
        grid=(n // tm, PROJ_W // PROJ_TN),
        in_specs=[pl.BlockSpec((tm, D_MODEL), lambda i, j: (i, 0)),
                  pl.BlockSpec((1, D_MODEL), lambda i, j: (0, 0)),
                  pl.BlockSpec((PROJ_TN, D_MODEL), lambda i, j: (j, 0))],
        out_specs=pl.BlockSpec((tm, PROJ_TN), lambda i, j: (i, j)),
        scratch_shapes=[pltpu.VMEM((tm, D_MODEL), bf16)],
        compiler_params=_cparams(("arbitrary", "arbitrary")),
        name="inproj",
    )(x2d, norm_w.reshape(1, D_MODEL), w_bf16)


CONV_TILES = 3
CONV_COLS = 2 * HEAD
CONV_ROWS = 128
CONV_PAD_W = CONV_TILES * PROJ_TN


def _qkv_kind(col):
    return "q" if col < QK_W else "k" if col < 2 * QK_W else "v" if col < QKV_W else "raw"


def _inproj_conv_kernel(x_ref, nw_ref, w_ref, cw_ref, o_ref, tail_ref, h_ref, hist_ref, raw_ref, *,
                        rows, tiles_per_seq):
    i = pl.program_id(0)
    j = pl.program_id(1)
    tm = x_ref.shape[0]

    @pl.when(j == 0)
    def _():
        def body(r, c):
            sl = pl.ds(pl.multiple_of(r * rows, rows), rows)
            x = x_ref[sl, :]
            ms = jnp.mean(x * x, axis=-1, keepdims=True)
            h_ref[sl, :] = (x * lax.rsqrt(ms + EPS) * nw_ref[...]).astype(bf16)
            return c
        lax.fori_loop(0, tm // rows, body, 0)

    @pl.when((i == 0) & (j == 0))
    def _():
        hist_ref[...] = jnp.zeros(hist_ref.shape, f32)

    @pl.when(j >= CONV_TILES)
    def _():
        o_ref[...] = _dot_nt(h_ref[...], w_ref[...])

    seq_start = lax.rem(i, tiles_per_seq) == 0
    for jj in range(CONV_TILES):
        @pl.when(j == jj)
        def _():
            def matmul_chunk(idx, c0):
                raw_ref[idx % 2] = _dot_nt(h_ref[...], w_ref[c0:c0 + CONV_COLS, :])

            def conv_chunk(idx, c0):
                cs = slice(c0, c0 + CONV_COLS)
                raw = raw_ref.at[idx % 2]
                tail = raw[tm - 8:tm, :]
                tail_ref[0, :, cs] = tail
                kinds = [_qkv_kind(jj * PROJ_TN + c0 + g * HEAD) for g in range(CONV_COLS // HEAD)]
                if kinds[0] == "raw":
                    o_ref[:, cs] = raw[...]
                    return
                hist = jnp.where(seq_start, 0.0, hist_ref[jj, :, cs])
                for rc in range(tm // CONV_ROWS):
                    r0 = rc * CONV_ROWS
                    if rc > 0:
                        xe = raw[r0 - 8:r0 + CONV_ROWS, :]
                    else:
                        xe = jnp.concatenate([hist, raw[0:CONV_ROWS, :]], axis=0)
                    acc = pltpu.roll(xe, 3, axis=0)[8:] * cw_ref[0:1, cs]
                    acc = acc + pltpu.roll(xe, 2, axis=0)[8:] * cw_ref[1:2, cs]
                    acc = acc + pltpu.roll(xe, 1, axis=0)[8:] * cw_ref[2:3, cs]
                    acc = acc + xe[8:] * cw_ref[3:4, cs]
                    act = _silu(acc)
                    for g, kind in enumerate(kinds):
                        ah = act[:, g * HEAD:(g + 1) * HEAD]
                        if kind != "v":
                            ss = jnp.sum(ah * ah, axis=-1, keepdims=True)
                            inv = lax.rsqrt(ss + EPS)
                            ah = ah * (inv * (HEAD ** -0.5) if kind == "q" else inv)
                        o_ref[r0:r0 + CONV_ROWS, c0 + g * HEAD:c0 + (g + 1) * HEAD] = ah
                hist_ref[jj, :, cs] = tail

            chunks = list(range(0, PROJ_TN, CONV_COLS))
            matmul_chunk(0, chunks[0])
            for idx in range(1, len(chunks)):
                matmul_chunk(idx, chunks[idx])
                conv_chunk(idx - 1, chunks[idx - 1])
            conv_chunk(len(chunks) - 1, chunks[-1])


def _inproj_conv(x2d, norm_w, w_bf16, cwa, seq_len):
    n = x2d.shape[0]
    tm = min(1024, seq_len)
    assert n % tm == 0 and seq_len % tm == 0 and PROJ_W % PROJ_TN == 0 and tm % CONV_ROWS == 0
    assert QKV_W % CONV_COLS == 0 and PROJ_TN % CONV_COLS == 0
    cw_pad = jnp.zeros((CONV_A, CONV_PAD_W), f32).at[:, 0:QKV_W].set(cwa)
    last = CONV_TILES - 1
    return pl.pallas_call(
        functools.partial(_inproj_conv_kernel, rows=min(128, tm), tiles_per_seq=seq_len // tm),
        out_shape=(jax.ShapeDtypeStruct((n, PROJ_W), f32),
                   jax.ShapeDtypeStruct((n // tm, 8, CONV_PAD_W), f32)),
        grid=(n // tm, PROJ_W // PROJ_TN),
        in_specs=[pl.BlockSpec((tm, D_MODEL), lambda i, j: (i, 0)),
                  pl.BlockSpec((1, D_MODEL), lambda i, j: (0, 0)),
                  pl.BlockSpec((PROJ_TN, D_MODEL), lambda i, j: (j, 0)),
                  pl.BlockSpec((CONV_A, PROJ_TN), lambda i, j: (0, jnp.minimum(j, last)))],
        out_specs=(pl.BlockSpec((tm, PROJ_TN), lambda i, j: (i, j)),
                   pl.BlockSpec((1, 8, PROJ_TN), lambda i, j: (i, 0, jnp.minimum(j, last)))),
        scratch_shapes=[pltpu.VMEM((tm, D_MODEL), bf16), pltpu.VMEM((CONV_TILES, 8, PROJ_TN), f32),
                        pltpu.VMEM((2, tm, CONV_COLS), f32)],
        compiler_params=_cparams(("arbitrary", "arbitrary")),
        name="inproj_conv",
    )(x2d, norm_w.reshape(1, D_MODEL), w_bf16, cw_pad)


W_IN_COLS = 11280
WPREP_TN = 1024
WPREP_SHIFT = 16


def _wprep_kernel(a_ref, b_ref, o_ref):
    j = pl.program_id(0)
    keep = WPREP_TN - WPREP_SHIFT

    @pl.when((j < 3) | (j == 10))
    def _():
        o_ref[...] = a_ref[...].astype(bf16)

    @pl.when((j >= 3) & (j < 10))
    def _():
        o_ref[0:keep, :] = a_ref[WPREP_SHIFT:WPREP_TN, :].astype(bf16)
        o_ref[keep:WPREP_TN, :] = b_ref[...].astype(bf16)

    @pl.when(j == 11)
    def _():
        o_ref[0:WPREP_SHIFT, :] = a_ref[0:WPREP_SHIFT, :].astype(bf16)
        o_ref[WPREP_SHIFT:WPREP_TN, :] = jnp.zeros((keep, D_MODEL), bf16)


def _wprep(w_in_t):
    assert w_in_t.shape == (W_IN_COLS, D_MODEL) and 2 * N_HEADS == WPREP_SHIFT
    n_blk = pl.cdiv(PROJ_W, WPREP_TN)

    def a_map(j):
        return (jnp.where(j < 3, j, jnp.where(j < 10, j + 1, jnp.where(j == 10, 3, 4))), 0)

    def b_map(j):
        return (jnp.minimum((WPREP_TN // WPREP_SHIFT) * (j + 2), W_IN_COLS // WPREP_SHIFT - 1), 0)

    return pl.pallas_call(
        _wprep_kernel,
        out_shape=jax.ShapeDtypeStruct((PROJ_W, D_MODEL), bf16),
        grid=(n_blk,),
        in_specs=[pl.BlockSpec((WPREP_TN, D_MODEL), a_map),
                  pl.BlockSpec((WPREP_SHIFT, D_MODEL), b_map)],
        out_specs=pl.BlockSpec((WPREP_TN, D_MODEL), lambda j: (j, 0)),
        compiler_params=_cparams(("arbitrary",)),
        name="wprep",
    )(w_in_t, w_in_t)


def _head_l2norm(a, scale):
    outs = []
    for h in range(N_HEADS):
        ah = a[:, h * HEAD:(h + 1) * HEAD]
        ss = jnp.sum(ah * ah, axis=-1, keepdims=True)
        n = ah * lax.rsqrt(ss + EPS)
        outs.append(n * scale if scale != 1.0 else n)
    return outs


def _delta_prompt_kernel(qkv_ref, bcx_ref, z_ref, ba_ref, cwb_ref, alog_ref, dtb_ref, onw_ref,
                         e64_ref,
                         o_ref, y_ref, snew_ref, ncb_ref,
                         s_ref, xb_ref, *, nb_step):
    C = CHUNK
    G = GROUP_HEADS
    R = G * C
    t = pl.program_id(1)
    nt = pl.num_programs(1)

    @pl.when(t == 0)
    def _():
        s_ref[...] = jnp.zeros(s_ref.shape, f32)
        xb_ref[:, 0:8, :] = jnp.zeros((nb_step, 8, SC_W), f32)

    rr = lax.broadcasted_iota(i32, (R, R), 0)
    cc = lax.broadcasted_iota(i32, (R, R), 1)
    same_bf = jnp.where((rr >> 6) == (cc >> 6), 1.0, 0.0).astype(bf16)
    r2 = lax.broadcasted_iota(i32, (R, G * HEAD), 0)
    c2 = lax.broadcasted_iota(i32, (R, G * HEAD), 1)
    bdmask = (r2 >> 6) == (c2 >> 7)
    r3 = lax.broadcasted_iota(i32, (C, C), 0)
    c3 = lax.broadcasted_iota(i32, (C, C), 1)
    ltri = jnp.where(r3 >= c3, 1.0, 0.0).astype(bf16)
    r4 = lax.broadcasted_iota(i32, (C, R), 0)
    c4 = lax.broadcasted_iota(i32, (C, R), 1)
    ident_t = r4 == (c4 & (C - 1))
    incl_p = r4 >= (c4 & (C - 1))
    strict_p = r4 > (c4 & (C - 1))
    hblk = c4 >> 6
    ones8 = jnp.ones((8, C), bf16)

    nbs = range(nb_step)
    units = [(nb, g) for nb in nbs for g in range(N_HEADS // G)]
    heads = lambda g: range(g * G, (g + 1) * G)

    qn = [[qkv_ref[nb, :, h * HEAD:(h + 1) * HEAD] for h in range(N_HEADS)] for nb in nbs]
    kn = [[qkv_ref[nb, :, QK_W + h * HEAD:QK_W + (h + 1) * HEAD] for h in range(N_HEADS)] for nb in nbs]
    vv = [qkv_ref[nb, :, 2 * QK_W:3 * QK_W] for nb in nbs]

    bts = [ba_ref[nb, :, 0:LANE] for nb in nbs]
    beta_all = [_sigmoid(bt) for bt in bts]
    g_all = [-(jnp.exp(alog_ref[:, 0:LANE]) * _softplus(bt + dtb_ref[:, 0:LANE])) for bt in bts]
    gc_small = [_dot_rsplit(ltri, ga) for ga in g_all]
    gl_small = [gc[C - 1:C, :] for gc in gc_small]

    k_st, q_st, kb, vb, kbg, qd, kd, gc_col = ({} for _ in range(8))
    for u in units:
        nb, g = u
        hs = heads(g)
        k_st[u] = jnp.concatenate([kn[nb][h] for h in hs], axis=0)
        q_st[u] = jnp.concatenate([qn[nb][h] for h in hs], axis=0)
        v_st = jnp.concatenate([vv[nb][:, h * HEAD:(h + 1) * HEAD] for h in hs], axis=0)
        beta_col = jnp.concatenate([beta_all[nb][:, h:h + 1] for h in hs], axis=0)
        gc_col[u] = jnp.concatenate([gc_small[nb][:, 8 + h:9 + h] for h in hs], axis=0)
        gl_col = jnp.concatenate(
            [jnp.broadcast_to(gl_small[nb][:, 8 + h:9 + h], (C, 1)) for h in hs], axis=0)
        kb[u] = k_st[u] * beta_col
        vb[u] = v_st * beta_col
        egc = jnp.exp(gc_col[u])
        kbg[u] = kb[u] * egc
        qd[u] = q_st[u] * egc
        kd[u] = k_st[u] * jnp.exp(gl_col - gc_col[u])

    gx = {u: _dot_lsplit(gc_small[u[0]], e64_ref[u[1], 0:LANE, :]) for u in units}
    crow = {u: _dot_rsplit(ones8, jnp.where(ident_t, gx[u], 0.0))[0:1, :] for u in units}
    a = {u: _dot_nt(jnp.concatenate([kb[u], q_st[u]], axis=0).astype(bf16), k_st[u].astype(bf16))
         for u in units}
    in_blk = [hblk == h for h in range(G - 1)]

    def pack(x):
        out = x[(G - 1) * C:G * C]
        for h in reversed(range(G - 1)):
            out = jnp.where(in_blk[h], x[h * C:(h + 1) * C], out)
        return out

    def expand(xp):
        return jnp.concatenate([xp.astype(bf16)] * G, axis=0) * same_bf

    dec = {u: jnp.where(incl_p, jnp.exp(jnp.where(incl_p, gx[u] - crow[u], 0.0)), 0.0) for u in units}
    nm = {u: jnp.where(strict_p, -(pack(a[u][0:R]) * dec[u]), 0.0) for u in units}
    qkm = {u: expand(pack(a[u][R:2 * R]) * dec[u]) for u in units}

    p = {u: jnp.where(ident_t, 1.0, 0.0) + nm[u] for u in units}
    nk = {u: _dot(nm[u].astype(bf16), expand(nm[u])) for u in units}
    for _ in range(4):
        for u in units:
            x = _dot(jnp.concatenate([p[u], nk[u]], axis=0).astype(bf16), expand(nk[u]))
            p[u] = p[u] + x[0:C]
            nk[u] = x[C:2 * C]
    for u in units:
        p[u] = p[u] + _dot(p[u].astype(bf16), expand(nk[u]))
    uw = {u: _dot(expand(p[u]), jnp.concatenate([vb[u], kbg[u]], axis=1).astype(bf16)) for u in units}

    ws = {}
    for u in units:
        nb, g = u
        for j, h in enumerate(heads(g)):
            sh = s_ref[nb, :, h * HEAD:(h + 1) * HEAD]
            lhs = jnp.concatenate([uw[u][j * C:(j + 1) * C, HEAD:2 * HEAD], qd[u][j * C:(j + 1) * C]], axis=0)
            ws[u, j] = _dot(lhs.astype(bf16), sh.astype(bf16))
    o_heads = {}
    for u in units:
        nb, g = u
        vnew_st = jnp.concatenate([uw[u][j * C:(j + 1) * C, 0:HEAD] - ws[u, j][0:C] for j in range(G)], axis=0)
        o_st = (jnp.concatenate([ws[u, j][C:2 * C] for j in range(G)], axis=0)
                + _dot(qkm[u], vnew_st.astype(bf16)))
        vbd = jnp.where(bdmask, jnp.concatenate([vnew_st] * G, axis=1), 0.0)
        lo = g * G * HEAD
        hi = lo + G * HEAD
        gl_row = jnp.concatenate(
            [jnp.broadcast_to(jnp.exp(gl_small[nb][:, 8 + h:9 + h]), (1, HEAD)) for h in heads(g)], axis=1)
        s_ref[nb, :, lo:hi] = s_ref[nb, :, lo:hi] * gl_row + _dot(kd[u].T.astype(bf16), vbd.astype(bf16))
        for j, h in enumerate(heads(g)):
            o_heads[nb, h] = o_st[j * C:(j + 1) * C]

    for nb in nbs:
        zt = z_ref[nb]
        for h in range(N_HEADS):
            oh = o_heads[nb, h]
            ms = jnp.mean(oh * oh, axis=-1, keepdims=True)
            zh = zt[:, h * HEAD:(h + 1) * HEAD]
            on = oh * lax.rsqrt(ms + EPS) * onw_ref[...] * _silu(zh)
            o_ref[nb, :, h * HEAD:(h + 1) * HEAD] = on.astype(bf16)

    for nb in nbs:
        bcx = bcx_ref[nb]
        cx = bcx[:, SC_W:2 * SC_W] * bcx[:, 2 * SC_W:3 * SC_W]
        xb_ref[nb, 8:8 + C, :] = cx
        ce = xb_ref[nb]
        cv = pltpu.roll(ce, 2, axis=0)[8:8 + C] * cwb_ref[0:1, :]
        cv = cv + pltpu.roll(ce, 1, axis=0)[8:8 + C] * cwb_ref[1:2, :]
        cv = cv + cx * cwb_ref[2:3, :]
        y_ref[nb] = (bcx[:, 0:SC_W] * cv).astype(bf16)
        xb_ref[nb, 0:8, :] = xb_ref[nb, C:C + 8, :]

    @pl.when(t == nt - 1)
    def _():
        for nb in range(nb_step):
            for h in range(N_HEADS):
                snew_ref[nb, h] = s_ref[nb, :, h * HEAD:(h + 1) * HEAD]
            ncb_ref[nb] = xb_ref[nb, 6:8, :]


def _expand_consts():
    lane = jnp.arange(BA_W)[:, None]
    col = jnp.arange(QK_W)[None, :]
    eb = (lane == (col >> 7)).astype(bf16)
    eg = (lane == (8 + (col >> 7))).astype(bf16)
    col64 = jnp.arange(GROUP_HEADS * CHUNK)[None, :]
    e64 = jnp.stack([(lane == (8 + g * GROUP_HEADS + (col64 >> 6))).astype(bf16)
                     for g in range(N_HEADS // GROUP_HEADS)], axis=0)
    return eb, eg, e64


def _delta_prompt(proj3, cwb, alog_row, dtb_row, onw_row, nb_step):
    b, t, _ = proj3.shape
    assert t % CHUNK == 0 and b % nb_step == 0
    _, _, e64 = _expand_consts()
    c = CHUNK
    const2 = lambda bi, ti: (0, 0)
    outs = pl.pallas_call(
        functools.partial(_delta_prompt_kernel, nb_step=nb_step),
        out_shape=(jax.ShapeDtypeStruct((b, t, QK_W), bf16),
                   jax.ShapeDtypeStruct((b, t, SC_W), bf16),
                   jax.ShapeDtypeStruct((b, N_HEADS, HEAD, HEAD), f32),
                   jax.ShapeDtypeStruct((b, CONV_B - 1, SC_W), f32)),
        grid=(b // nb_step, t // c),
        in_specs=[pl.BlockSpec((nb_step, c, QKV_W), lambda bi, ti: (bi, ti, COL_QKV // QKV_W)),
                  pl.BlockSpec((nb_step, c, QKV_W), lambda bi, ti: (bi, ti, COL_BCX // QKV_W)),
                  pl.BlockSpec((nb_step, c, QK_W), lambda bi, ti: (bi, ti, COL_Z // QK_W)),
                  pl.BlockSpec((nb_step, c, BA_W), lambda bi, ti: (bi, ti, COL_BA // BA_W)),
                  pl.BlockSpec((CONV_B, SC_W), const2),
                  pl.BlockSpec((1, BA_W), const2),
                  pl.BlockSpec((1, BA_W), const2),
                  pl.BlockSpec((1, HEAD), const2),
                  pl.BlockSpec((N_HEADS // GROUP_HEADS, BA_W, GROUP_HEADS * CHUNK), lambda bi, ti: (0, 0, 0))],
        out_specs=(pl.BlockSpec((nb_step, c, QK_W), lambda bi, ti: (bi, ti, 0)),
                   pl.BlockSpec((nb_step, c, SC_W), lambda bi, ti: (bi, ti, 0)),
                   pl.BlockSpec((nb_step, N_HEADS, HEAD, HEAD), lambda bi, ti: (bi, 0, 0, 0)),
                   pl.BlockSpec((nb_step, CONV_B - 1, SC_W), lambda bi, ti: (bi, 0, 0))),
        scratch_shapes=[pltpu.VMEM((nb_step, HEAD, QK_W), f32),
                        pltpu.VMEM((nb_step, 8 + c, SC_W), f32)],
        compiler_params=_cparams(("arbitrary", "arbitrary")),
        name="delta_prompt",
    )(proj3, proj3, proj3, proj3, cwb, alog_row, dtb_row, onw_row, e64)
    return outs


def _sample_prep_kernel(p_ref, bufa_ref, bufb_ref, cwa_ref, cwb_ref, alog_ref, dtb_ref, eb_ref, eg_ref,
                        q_ref, k_ref, v_ref, beta_ref, eg_out_ref, z_ref, y_ref, nbufa_ref, nbufb_ref):
    def put_heads(ref, a):
        for h in range(N_HEADS):
            ref[:, h, :] = a[:, h * HEAD:(h + 1) * HEAD]

    def conv_sec(lo):
        hi = lo + QK_W
        raw = p_ref[:, COL_QKV + lo:COL_QKV + hi]
        acc = bufa_ref[0, :, lo:hi] * cwa_ref[0:1, lo:hi]
        acc = acc + bufa_ref[1, :, lo:hi] * cwa_ref[1:2, lo:hi]
        acc = acc + bufa_ref[2, :, lo:hi] * cwa_ref[2:3, lo:hi]
        acc = acc + raw * cwa_ref[3:4, lo:hi]
        nbufa_ref[0, :, lo:hi] = bufa_ref[1, :, lo:hi]
        nbufa_ref[1, :, lo:hi] = bufa_ref[2, :, lo:hi]
        nbufa_ref[2, :, lo:hi] = raw
        return _silu(acc)

    qn = _head_l2norm(conv_sec(0), HEAD ** -0.5)
    kn = _head_l2norm(conv_sec(QK_W), 1.0)
    for h in range(N_HEADS):
        q_ref[:, h, :] = qn[h]
        k_ref[:, h, :] = kn[h]
    put_heads(v_ref, conv_sec(2 * QK_W))
    put_heads(z_ref, p_ref[:, COL_Z:COL_Z + QK_W])

    bt = p_ref[:, COL_BA:COL_BA + BA_W]
    beta_all = _sigmoid(bt)
    g_all = -(jnp.exp(alog_ref[...]) * _softplus(bt + dtb_ref[...]))
    put_heads(beta_ref, _dot_lsplit(beta_all, eb_ref[...]))
    put_heads(eg_out_ref, jnp.exp(_dot_lsplit(g_all, eg_ref[...])))

    bg = p_ref[:, COL_BCX:COL_BCX + SC_W]
    cx = p_ref[:, COL_BCX + SC_W:COL_BCX + 2 * SC_W] * p_ref[:, COL_BCX + 2 * SC_W:COL_BCX + 3 * SC_W]
    cv = bufb_ref[0] * cwb_ref[0:1, :]
    cv = cv + bufb_ref[1] * cwb_ref[1:2, :]
    cv = cv + cx * cwb_ref[2:3, :]
    y_ref[...] = (bg * cv).astype(bf16)
    nbufb_ref[0] = bufb_ref[1]
    nbufb_ref[1] = cx


def _sample_prep(proj_s, bufa_t, bufb_t, cwa, cwb, alog_row, dtb_row):
    n = proj_s.shape[0]
    eb, eg, _ = _expand_consts()
    row = jax.ShapeDtypeStruct((n, N_HEADS, HEAD), f32)
    return pl.pallas_call(
        _sample_prep_kernel,
        out_shape=(row, row, row, row, row, row,
                   jax.ShapeDtypeStruct((n, SC_W), bf16),
                   jax.ShapeDtypeStruct((CONV_A - 1, n, QKV_W), f32),
                   jax.ShapeDtypeStruct((CONV_B - 1, n, SC_W), f32)),
        compiler_params=pltpu.CompilerParams(vmem_limit_bytes=VMEM_LIMIT),
        name="sample_prep",
    )(proj_s, bufa_t, bufb_t, cwa, cwb, alog_row, dtb_row, eb, eg)


def _sample_step_kernel(s_ref, q_ref, k_ref, v_ref, beta_ref, eg_ref, z_ref, onw_ref,
                        snew_ref, o_ref, *, bb):
    w = N_HEADS * HEAD
    r8 = lax.broadcasted_iota(i32, (N_HEADS, w), 0)
    c8 = lax.broadcasted_iota(i32, (N_HEADS, w), 1)
    mask8 = r8 == (c8 >> 7)
    zpad_k = jnp.zeros((HEAD - N_HEADS, HEAD), f32)
    hb = lambda h: slice(h * HEAD, (h + 1) * HEAD)
    bs = range(bb)
    s_dec, k8s, kts = [], [], []
    for b in bs:
        s_all = jnp.concatenate([s_ref[b, h] for h in range(N_HEADS)], axis=1)
        eg8 = eg_ref[b]
        eg_row = jnp.concatenate([eg8[h:h + 1, :] for h in range(N_HEADS)], axis=1)
        s_dec.append(s_all * eg_row)
        k8s.append(k_ref[b])
        kts.append(jnp.concatenate([k8s[b], zpad_k], axis=0).T)
    xs = [_dot(k8s[b].astype(bf16), s_dec[b].astype(bf16)) for b in bs]
    s_new = []
    for b in bs:
        vb8, bt8 = v_ref[b], beta_ref[b]
        upd = [kts[b][:, h:h + 1] * ((vb8[h:h + 1, :] - xs[b][h:h + 1, hb(h)]) * bt8[h:h + 1, :])
               for h in range(N_HEADS)]
        s_new.append(s_dec[b] + jnp.concatenate(upd, axis=1))
    ys = [_dot(q_ref[b].astype(bf16), s_new[b].astype(bf16)) for b in bs]
    for b in bs:
        yv = jnp.where(mask8, ys[b], 0.0)
        o8 = yv[:, 0:HEAD]
        for j in range(1, N_HEADS):
            o8 = o8 + yv[:, j * HEAD:(j + 1) * HEAD]
        ms = jnp.mean(o8 * o8, axis=-1, keepdims=True)
        o_ref[b] = o8 * lax.rsqrt(ms + EPS) * onw_ref[...] * _silu(z_ref[b])
        for h in range(N_HEADS):
            snew_ref[b, h] = s_new[b][:, hb(h)]


def _sample_step(state, q, k, v, beta, eg, z, onw_row, bb=8):
    n = state.shape[0]
    assert n % bb == 0
    hspec = pl.BlockSpec((bb, N_HEADS, HEAD), lambda i: (i, 0, 0))
    sspec = pl.BlockSpec((bb, N_HEADS, HEAD, HEAD), lambda i: (i, 0, 0, 0))
    return pl.pallas_call(
        functools.partial(_sample_step_kernel, bb=bb),
        out_shape=(jax.ShapeDtypeStruct(state.shape, f32),
                   jax.ShapeDtypeStruct((n, N_HEADS, HEAD), f32)),
        grid=(n // bb,),
        in_specs=[sspec, hspec, hspec, hspec, hspec, hspec, hspec, pl.BlockSpec((1, HEAD), lambda i: (0, 0))],
        out_specs=(sspec, hspec),
        compiler_params=_cparams(("arbitrary",)),
        name="sample_step",
    )(state, q, k, v, beta, eg, z, onw_row)


def _mix_route_kernel(x_ref, o_ref, y_ref, ga_ref, gb_ref, wa_ref, wb_ref, wo_ref, n2_ref,
                      rwh_ref, rwl_ref, rb_ref, cnt_in_ref, x1_ref, h2_ref, mi_ref, mw_ref, cnt_ref):
    i = pl.program_id(0)
    tm = x_ref.shape[0]

    @pl.when(i == 0)
    def _():
        cnt_ref[...] = cnt_in_ref[...]

    oa = _dot(o_ref[...], wa_ref[...])
    ob = _dot(y_ref[...], wb_ref[...])
    merged = _sigmoid(ga_ref[...]) * oa + _sigmoid(gb_ref[...]) * ob
    x1 = x_ref[...] + _dot(merged.astype(bf16), wo_ref[...])
    x1_ref[...] = x1
    ms = jnp.mean(x1 * x1, axis=-1, keepdims=True)
    h2 = x1 * lax.rsqrt(ms + EPS) * n2_ref[...]
    h2_ref[...] = h2

    h_hi, h_lo = _split(h2, 2)
    logits = _dot(h_hi, rwh_ref[...]) + _dot(h_hi, rwl_ref[...]) + _dot(h_lo, rwh_ref[...]) + rb_ref[...]

    lane = lax.broadcasted_iota(i32, (tm, LANE), 1)
    lanef = lane.astype(f32)
    neg = jnp.float32(-jnp.inf)
    big = jnp.float32(1e9)
    gmask = (lane >= N_EXPERTS) & (lane < N_EXPERTS + N_GROUPS)
    gl = jnp.where(gmask, logits, neg)
    gmax = jnp.max(gl, axis=-1, keepdims=True)
    gidx = jnp.min(jnp.where(gl == gmax, lanef - N_EXPERTS, big), axis=-1, keepdims=True)
    gsum = jnp.sum(jnp.where(gmask, jnp.exp(gl - gmax), 0.0), axis=-1, keepdims=True)
    gprob = 1.0 / gsum

    emask = (lane < N_EXPERTS) & ((lane >> 3).astype(f32) == gidx)
    el = jnp.where(emask, logits, neg)
    emax = jnp.max(el, axis=-1, keepdims=True)
    pe = jnp.where(emask, jnp.exp(el - emax), 0.0)
    eprob = pe / jnp.sum(pe, axis=-1, keepdims=True)
    p1m = jnp.where(emask, eprob, -1.0)
    m1 = jnp.max(p1m, axis=-1, keepdims=True)
    i1 = jnp.min(jnp.where(p1m == m1, lanef, big), axis=-1, keepdims=True)
    p2m = jnp.where(lanef == i1, -1.0, p1m)
    m2 = jnp.max(p2m, axis=-1, keepdims=True)
    i2 = jnp.min(jnp.where(p2m == m2, lanef, big), axis=-1, keepdims=True)
    tot = m1 + m2
    c1 = m1 / tot * gprob
    c2 = m2 / tot * gprob

    oh1 = jnp.where(lanef == i1, 1.0, 0.0)
    oh2 = jnp.where(lanef == i2, 1.0, 0.0)
    ohs = oh1 + oh2
    rt = lax.broadcasted_iota(i32, (tm, tm), 0)
    ct = lax.broadcasted_iota(i32, (tm, tm), 1)
    lstrict = jnp.where(rt > ct, 1.0, 0.0).astype(bf16)
    cs = _dot(lstrict, ohs.astype(bf16)) + cnt_ref[...]
    rank1 = jnp.sum(cs * oh1, axis=-1, keepdims=True)
    rank2 = jnp.sum(cs * oh2, axis=-1, keepdims=True)
    cnt_ref[...] = cnt_ref[...] + jnp.sum(ohs, axis=0, keepdims=True)

    mi = jnp.where(lane == 0, i1, jnp.where(lane == 1, i2, jnp.where(lane == 2, rank1,
                                                                     jnp.where(lane == 3, rank2, 0.0))))
    mi_ref[...] = mi.astype(i32)
    mw_ref[...] = jnp.where(lane == 0, c1, jnp.where(lane == 1, c2, 0.0))


def _mix_route(x2d, o2d, y2d, proj2d, wa, wb, wo, n2_row, rwh, rwl, rb_row, cnt_in):
    n = x2d.shape[0]
    tm = min(256, n)
    assert n % tm == 0
    tok = lambda width: pl.BlockSpec((tm, width), lambda i: (i, 0))
    full = lambda a: pl.BlockSpec(a.shape, lambda i: (0,) * a.ndim)
    in_specs = [tok(D_MODEL), tok(QK_W), tok(SC_W),
                pl.BlockSpec((tm, D_MODEL), lambda i: (i, COL_GA // D_MODEL)),
                pl.BlockSpec((tm, D_MODEL), lambda i: (i, COL_GB // D_MODEL)),
                full(wa), full(wb), full(wo), full(n2_row), full(rwh), full(rwl), full(rb_row), full(cnt_in)]
    out_shape = (jax.ShapeDtypeStruct((n, D_MODEL), f32),
                 jax.ShapeDtypeStruct((n, D_MODEL), f32),
                 jax.ShapeDtypeStruct((n, LANE), i32),
                 jax.ShapeDtypeStruct((n, LANE), f32),
                 jax.ShapeDtypeStruct((1, LANE), f32))
    out_specs = (tok(D_MODEL), tok(D_MODEL), tok(LANE), tok(LANE),
                 pl.BlockSpec((1, LANE), lambda i: (0, 0)))
    return pl.pallas_call(
        _mix_route_kernel,
        out_shape=out_shape,
        grid=(n // tm,),
        in_specs=in_specs,
        out_specs=out_specs,
        compiler_params=_cparams(("arbitrary",)),
        name="mix_route",
    )(x2d, o2d, y2d, proj2d, proj2d, wa, wb, wo, n2_row, rwh, rwl, rb_row, cnt_in)


MI_W = 4
SUBLANE = 8
ROW_DMA_UNROLL = 8


def _dest_kernel(mi_ref, starts_ref, o_ref):
    mi = mi_ref[...]
    lane = lax.broadcasted_iota(i32, mi.shape, 1)
    st = starts_ref[...]

    def first_row(e_col):
        return jnp.sum(jnp.where(lane == e_col, st, 0.0), axis=-1, keepdims=True).astype(i32)

    d0 = first_row(mi[:, 0:1]) + mi[:, 2:3]
    d1 = first_row(mi[:, 1:2]) + mi[:, 3:4]
    o_ref[...] = jnp.where(lane == 0, d0 >> 3, jnp.where(lane == 1, d0 & (SUBLANE - 1),
                           jnp.where(lane == 2, d1 >> 3, jnp.where(lane == 3, d1 & (SUBLANE - 1), 0))))


def _dest_rows(mi, starts_row):
    n = mi.shape[0]
    tm = min(1024, n)
    assert n % tm == 0
    return pl.pallas_call(
        _dest_kernel,
        out_shape=jax.ShapeDtypeStruct((n, LANE), i32),
        grid=(n // tm,),
        in_specs=[pl.BlockSpec((tm, LANE), lambda i: (i, 0)), pl.BlockSpec((1, LANE), lambda i: (0, 0))],
        out_specs=pl.BlockSpec((tm, LANE), lambda i: (i, 0)),
        compiler_params=_cparams(("arbitrary",)),
        name="moe_dest",
    )(mi, starts_row)


def _dispatch_kernel(mi_ref, hp_ref, hs_ref, xs_ref, sem, *, np_tiles):
    i = pl.program_id(0)

    def scatter_rows(h_ref):
        n_tiles = h_ref.shape[0]

        def start(t, c):
            for u in range(SUBLANE):
                rec = MI_W * (SUBLANE * t + u)
                for k in range(2):
                    dst = xs_ref.at[mi_ref[rec + 2 * k], pl.ds(mi_ref[rec + 2 * k + 1], 1)]
                    pltpu.make_async_copy(h_ref.at[t, pl.ds(u, 1)], dst, sem).start(priority=k)
            return c

        lax.fori_loop(0, n_tiles, start, 0)
        for k in range(2):
            pltpu.make_async_copy(h_ref, xs_ref.at[pl.ds(0, n_tiles)], sem).wait()

    @pl.when(i < np_tiles)
    def _():
        scatter_rows(hp_ref)

    @pl.when(i >= np_tiles)
    def _():
        scatter_rows(hs_ref)


def _dispatch(h2_p, h2_s, mi_flat):
    tm = TOKEN_TILE
    n_p, n_s = h2_p.shape[0], h2_s.shape[0]
    assert n_p % tm == 0 and n_s <= tm and n_s % SUBLANE == 0 and tm % SUBLANE == 0
    np_tiles = n_p // tm
    tiled = lambda a: a.reshape(a.shape[0] // SUBLANE, SUBLANE, D_MODEL)
    return pl.pallas_call(
        functools.partial(_dispatch_kernel, np_tiles=np_tiles),
        out_shape=jax.ShapeDtypeStruct((2 * (n_p + n_s) // SUBLANE, SUBLANE, D_MODEL), f32),
        grid=(np_tiles + 1,),
        in_specs=[pl.BlockSpec((MI_W * tm,), lambda i: (i,), memory_space=pltpu.SMEM),
                  pl.BlockSpec((tm // SUBLANE, SUBLANE, D_MODEL), lambda i: (jnp.minimum(i, np_tiles - 1), 0, 0)),
                  pl.BlockSpec((n_s // SUBLANE, SUBLANE, D_MODEL), lambda i: (0, 0, 0))],
        out_specs=pl.BlockSpec(memory_space=pl.ANY),
        scratch_shapes=[pltpu.SemaphoreType.DMA(())],
        compiler_params=_cparams(("arbitrary",)),
        name="moe_dispatch",
    )(mi_flat, tiled(h2_p), tiled(h2_s))


def _cast_rows(src_ref, dst_ref, col0=0, rows=256):
    width = src_ref.shape[1]

    def body(r, c):
        sl = pl.ds(pl.multiple_of(r * rows, rows), rows)
        dst_ref[sl, col0:col0 + width] = src_ref[sl, :].astype(bf16)
        return c
    lax.fori_loop(0, src_ref.shape[0] // rows, body, 0)


def _moe_kernel(blk_ref, lo_ref, hi_ref, first_ref, newe_ref, slot_ref, pre_ref, init_ref,
                x_ref, wg_hbm, wu_hbm, wd_hbm, o_ref,
                wg_f, wu_f, wd_f, wgu_b, wd_b, sem):
    i = pl.program_id(0)
    lo = lo_ref[i]
    hi = hi_ref[i]

    def weight_copies(e, slot):
        return [pltpu.make_async_copy(wg_hbm.at[e], wg_f.at[slot], sem.at[slot, 0]),
                pltpu.make_async_copy(wu_hbm.at[e], wu_f.at[slot], sem.at[slot, 1]),
                pltpu.make_async_copy(wd_hbm.at[e], wd_f.at[slot], sem.at[slot, 2])]

    def start_weights(e, slot):
        for cp, prio in zip(weight_copies(e, slot), (0, 1, 1)):
            cp.start(priority=prio)

    @pl.when(i == 0)
    def _():
        start_weights(init_ref[0], 0)
        for k in range(1, W_SLOTS):
            @pl.when(init_ref[k] >= 0)
            def _():
                start_weights(init_ref[k], k)

    @pl.when(newe_ref[i] == 1)
    def _():
        slot = slot_ref[i]
        cg, cu, cd = weight_copies(0, slot)
        cg.wait()
        _cast_rows(wg_f.at[slot], wgu_b, 0)
        cu.wait()
        _cast_rows(wu_f.at[slot], wgu_b, D_FF)
        cd.wait()
        _cast_rows(wd_f.at[slot], wd_b)

        @pl.when(pre_ref[i] >= 0)
        def _():
            start_weights(pre_ref[i], slot)

    @pl.when(hi > lo)
    def _():
        x = x_ref[...].astype(bf16)
        au = _dot(x, wgu_b[...])
        y = _dot((_silu(au[:, 0:D_FF]) * au[:, D_FF:2 * D_FF]).astype(bf16), wd_b[...])
        row = lax.broadcasted_iota(i32, y.shape, 0)
        ym = jnp.where((row >= lo) & (row < hi), y, 0.0)

        @pl.when(first_ref[i] == 1)
        def _():
            o_ref[...] = ym

        @pl.when(first_ref[i] == 0)
        def _():
            o_ref[...] = o_ref[...] + ym


def _moe(xs, w_gate, w_up, w_down, items):
    n_items = items[0].shape[0]
    rows = xs.shape[0]
    n_pref = len(items)
    xmap = lambda i, blk, *_: (blk[i], 0)
    grid_spec = pltpu.PrefetchScalarGridSpec(
        num_scalar_prefetch=n_pref,
        grid=(n_items,),
        in_specs=[pl.BlockSpec((MOE_ROWS, D_MODEL), xmap),
                  pl.BlockSpec(memory_space=pl.ANY),
                  pl.BlockSpec(memory_space=pl.ANY),
                  pl.BlockSpec(memory_space=pl.ANY)],
        out_specs=pl.BlockSpec((MOE_ROWS, D_MODEL), xmap),
        scratch_shapes=[pltpu.VMEM((W_SLOTS, D_MODEL, D_FF), f32), pltpu.VMEM((W_SLOTS, D_MODEL, D_FF), f32),
                        pltpu.VMEM((W_SLOTS, D_FF, D_MODEL), f32),
                        pltpu.VMEM((D_MODEL, 2 * D_FF), bf16), pltpu.VMEM((D_FF, D_MODEL), bf16),
                        pltpu.SemaphoreType.DMA((W_SLOTS, 3))],
    )
    return pl.pallas_call(
        _moe_kernel,
        out_shape=jax.ShapeDtypeStruct((rows, D_MODEL), f32),
        grid_spec=grid_spec,
        compiler_params=_cparams(("arbitrary",)),
        name="moe_experts",
    )(*items, xs, w_gate, w_up, w_down)


def _combine_kernel(mi_ref, mi_next_ref, x1p_ref, mwp_ref, x1s_ref, mws_ref, fnw_ref, ys_ref,
                    yp_ref, ysm_ref, g_ref, sem, *, np_tiles):
    i = pl.program_id(0)
    tiles_p = x1p_ref.shape[0] // SUBLANE
    tiles_s = x1s_ref.shape[0] // SUBLANE
    slot = lax.rem(i, 2)

    def gather_rows(m_ref, dst_slot, n_tiles):
        def body(t, c):
            for u in range(SUBLANE):
                rec = MI_W * (SUBLANE * t + u)
                for k in range(2):
                    src = ys_ref.at[m_ref[rec + 2 * k], pl.ds(m_ref[rec + 2 * k + 1], 1)]
                    pltpu.make_async_copy(src, g_ref.at[dst_slot, k, t, pl.ds(u, 1)],
                                          sem.at[dst_slot]).start(priority=k)
            return c
        lax.fori_loop(0, n_tiles, body, 0)

    @pl.when(i == 0)
    def _():
        gather_rows(mi_ref, 0, tiles_p)

    @pl.when(i + 1 < np_tiles)
    def _():
        gather_rows(mi_next_ref, 1 - slot, tiles_p)

    @pl.when(i + 1 == np_tiles)
    def _():
        gather_rows(mi_next_ref, 1 - slot, tiles_s)

    def finish(x1_ref, mw_ref, out_ref, n_tiles):
        for k in range(2):
            pltpu.make_async_copy(ys_ref.at[pl.ds(0, n_tiles)], g_ref.at[slot, k, pl.ds(0, n_tiles)],
                                  sem.at[slot]).wait()
        rows = n_tiles * SUBLANE
        mw = mw_ref[...]
        g0 = g_ref[slot, 0, 0:n_tiles].reshape(rows, D_MODEL)
        g1 = g_ref[slot, 1, 0:n_tiles].reshape(rows, D_MODEL)
        x2 = x1_ref[...] + (g0 * mw[:, 0:1] + g1 * mw[:, 1:2])
        ms = jnp.mean(x2 * x2, axis=-1, keepdims=True)
        out_ref[...] = x2 * lax.rsqrt(ms + EPS) * fnw_ref[...]

    @pl.when(i < np_tiles)
    def _():
        finish(x1p_ref, mwp_ref, yp_ref, tiles_p)

    @pl.when(i >= np_tiles)
    def _():
        finish(x1s_ref, mws_ref, ysm_ref, tiles_s)


def _combine(x1_p, mw_p, x1_s, mw_s, fnw_row, ys3, mi_flat):
    tm = TOKEN_TILE
    n_p, n_s = x1_p.shape[0], x1_s.shape[0]
    assert n_p % tm == 0 and n_s <= tm and n_s % SUBLANE == 0
    np_tiles = n_p // tm
    ptile = lambda width: pl.BlockSpec((tm, width), lambda i: (jnp.minimum(i, np_tiles - 1), 0))
    stile = lambda width: pl.BlockSpec((n_s, width), lambda i: (0, 0))
    return pl.pallas_call(
        functools.partial(_combine_kernel, np_tiles=np_tiles),
        out_shape=(jax.ShapeDtypeStruct((n_p, D_MODEL), f32),
                   jax.ShapeDtypeStruct((n_s, D_MODEL), f32)),
        grid=(np_tiles + 1,),
        in_specs=[pl.BlockSpec((MI_W * tm,), lambda i: (i,), memory_space=pltpu.SMEM),
                  pl.BlockSpec((MI_W * tm,), lambda i: (jnp.minimum(i + 1, np_tiles),), memory_space=pltpu.SMEM),
                  ptile(D_MODEL), ptile(LANE), stile(D_MODEL), stile(LANE),
                  pl.BlockSpec((1, D_MODEL), lambda i: (0, 0)),
                  pl.BlockSpec(memory_space=pl.ANY)],
        out_specs=(ptile(D_MODEL), stile(D_MODEL)),
        scratch_shapes=[pltpu.VMEM((2, 2, tm // SUBLANE, SUBLANE, D_MODEL), f32), pltpu.SemaphoreType.DMA((2,))],
        compiler_params=_cparams(("arbitrary",)),
        name="moe_combine",
    )(mi_flat, mi_flat, x1_p, mw_p, x1_s, mw_s, fnw_row, ys3)


PLAN_ROWS = 256
N_ITEM_FIELDS = 7


def _plan_kernel(cnt_ref, items_ref, rows_ref, *, nblk):
    cnt = cnt_ref[...]
    lane1 = lax.broadcasted_iota(i32, (1, LANE), 1)
    in_e = lane1 < N_EXPERTS
    ri = lax.broadcasted_iota(i32, (LANE, LANE), 0)
    ci = lax.broadcasted_iota(i32, (LANE, LANE), 1)
    upper = jnp.where(ri <= ci, 1.0, 0.0).astype(bf16)

    def cumsum_lanes(v):
        return _dot_lsplit(jnp.broadcast_to(v, (8, LANE)), upper)[0:1, :]

    shift = MOE_ROWS.bit_length() - 1
    ends = cumsum_lanes(cnt)
    starts = ends - cnt
    act = cnt > 0.0
    first_blk = (starts.astype(i32) >> shift).astype(f32)
    last_blk = (jnp.maximum(ends - 1.0, 0.0).astype(i32) >> shift).astype(f32)
    nvis = jnp.where(act, last_blk - first_blk + 1.0, 0.0)
    vis_end = cumsum_lanes(nvis)
    vis_start = vis_end - nvis
    total = jnp.max(vis_end, axis=-1, keepdims=True)
    cum_act = cumsum_lanes(jnp.where(act, 1.0, 0.0))
    n_uniq = jnp.max(cum_act, axis=-1, keepdims=True)

    p = PLAN_ROWS
    lane = lax.broadcasted_iota(i32, (p, LANE), 1)
    idx = lax.broadcasted_iota(i32, (p, LANE), 0).astype(f32)
    idx1 = idx[:, 0:1]
    count_le = lambda row, col: jnp.sum(jnp.where((row <= col) & in_e, 1.0, 0.0), axis=-1, keepdims=True)
    e = jnp.minimum(count_le(vis_end, idx), N_EXPERTS - 1.0)
    onehot = lane.astype(f32) == e
    look = lambda tbl: jnp.sum(jnp.where(onehot, tbl, 0.0), axis=-1, keepdims=True)
    blk = look(first_blk) + idx1 - look(vis_start)
    lo = jnp.maximum(look(starts), blk * MOE_ROWS) - blk * MOE_ROWS
    hi = jnp.minimum(look(ends), (blk + 1.0) * MOE_ROWS) - blk * MOE_ROWS
    valid = idx1 < total
    blk = jnp.where(valid, blk, nblk - 1.0)
    lo = jnp.where(valid, lo, 0.0)
    hi = jnp.where(valid, hi, 0.0)
    rep = lambda c: jnp.broadcast_to(c, (p, LANE))
    prev = lambda c: pltpu.roll(rep(c), 1, axis=0)[:, 0:1]
    is0 = idx1 == 0.0
    first = valid & (is0 | (blk != prev(blk)))
    newe = valid & (is0 | (e != prev(e)))
    rp = lax.broadcasted_iota(i32, (p, p), 0)
    cp = lax.broadcasted_iota(i32, (p, p), 1)
    lower = jnp.where(rp >= cp, 1.0, 0.0).astype(bf16)
    order = _dot(lower, rep(jnp.where(newe, 1.0, 0.0)).astype(bf16))[:, 0:1] - 1.0
    slot = jnp.where(newe, order - W_SLOTS * jnp.floor((order + 0.5) * (1.0 / W_SLOTS)), 0.0)
    k2 = order + float(W_SLOTS)
    pre = jnp.where(newe & (k2 < n_uniq), count_le(cum_act, rep(k2)), -1.0)
    out = jnp.zeros((p, LANE), f32)
    for c, v in enumerate([blk, lo, hi, jnp.where(first, 1.0, 0.0), jnp.where(newe, 1.0, 0.0), slot, pre]):
        out = jnp.where(lane == c, v, out)
    items_ref[...] = out.astype(i32)

    init_row = jnp.zeros((1, LANE), f32)
    for k in range(W_SLOTS):
        init_row = jnp.where(lane1 == k, jnp.where(n_uniq > float(k), count_le(cum_act, float(k)), -1.0), init_row)
    rows_ref[...] = jnp.zeros(rows_ref.shape, f32)
    rows_ref[0:1, :] = starts
    rows_ref[1:2, :] = init_row


def _work_items(cnt_row, n_rows):
    nblk = n_rows // MOE_ROWS
    n_items = nblk + N_EXPERTS - 1
    assert n_items <= PLAN_ROWS and n_rows % MOE_ROWS == 0
    items, rows = pl.pallas_call(
        functools.partial(_plan_kernel, nblk=nblk),
        out_shape=(jax.ShapeDtypeStruct((PLAN_ROWS, LANE), i32), jax.ShapeDtypeStruct((8, LANE), f32)),
        compiler_params=pltpu.CompilerParams(vmem_limit_bytes=VMEM_LIMIT),
        name="moe_plan",
    )(cnt_row)
    fields = tuple(items[0:n_items, c] for c in range(N_ITEM_FIELDS))
    return rows[0:1, :], fields + (rows[1, 0:W_SLOTS].astype(i32),)


def kernel(x_prompt, x_sample, state_delta, state_qkv_conv, state_short_conv, norm1_w, w_in, conv_a_w, a_log, dt_bias, out_norm_w, w_branch_a, conv_b_w, w_branch_b, w_o, norm2_w, router_group_w, router_group_b, router_expert_w, router_expert_b, w_gate, w_up, w_down, final_norm_w):
    assert norm1_w.shape[0] == 1, "single-layer trunk"
    bp, tp, d = x_prompt.shape
    bs, ts, _ = x_sample.shape
    assert d == D_MODEL and ts == 1
    n_p = bp * tp
    n_s = bs
    n_all = n_p + n_s

    w_perm = _wprep(jnp.transpose(w_in[0]))
    wa = w_branch_a[0].astype(bf16)
    wb = w_branch_b[0].astype(bf16)
    wo = w_o[0].astype(bf16)
    pad = lambda v: jnp.zeros((1, BA_W), f32).at[0, N_HEADS:2 * N_HEADS].set(v)
    alog_row = pad(a_log[0])
    dtb_row = pad(dt_bias[0])
    onw_row = out_norm_w[0].reshape(1, HEAD)
    cwa = conv_a_w[0]
    cwb = conv_b_w[0]
    r_pad = LANE - N_EXPERTS - N_GROUPS
    rw = jnp.concatenate([router_expert_w[0], router_group_w[0], jnp.zeros((D_MODEL, r_pad), f32)], axis=1)
    rwh = rw.astype(bf16)
    rwl = (rw - rwh.astype(f32)).astype(bf16)
    rb_row = jnp.concatenate([router_expert_b[0], router_group_b[0], jnp.zeros((r_pad,), f32)]).reshape(1, LANE)
    n2_row = norm2_w[0].reshape(1, D_MODEL)

    xp2 = x_prompt.reshape(n_p, D_MODEL)
    proj_p, tails = _inproj_conv(xp2, norm1_w[0], w_perm, cwa, tp)
    tiles_per_seq = tails.shape[0] // bp
    nca_p = tails.reshape(bp, tiles_per_seq, 8, CONV_PAD_W)[:, -1, 8 - (CONV_A - 1):8, 0:QKV_W]
    o_p, y_p, sd_p, ncb_p = _delta_prompt(proj_p.reshape(bp, tp, PROJ_W), cwb, alog_row, dtb_row,
                                          onw_row, nb_step=4 if bp % 4 == 0 else (2 if bp % 2 == 0 else 1))
    cnt0 = jnp.zeros((1, LANE), f32)
    x1_p, h2_p, mi_p, mw_p, cnt_p = _mix_route(xp2, o_p.reshape(n_p, QK_W), y_p.reshape(n_p, SC_W), proj_p,
                                               wa, wb, wo, n2_row, rwh, rwl, rb_row, cnt0)

    xs2 = x_sample.reshape(n_s, D_MODEL)
    proj_s = _inproj(xs2, norm1_w[0], w_perm)
    bufa_t = jnp.transpose(state_qkv_conv[0], (1, 0, 2))
    bufb_t = jnp.transpose(state_short_conv[0], (1, 0, 2))
    q_s, k_s, v_s, beta_s, eg_s, z_s, y_s, nbufa_t, nbufb_t = _sample_prep(proj_s, bufa_t, bufb_t, cwa, cwb,
                                                                           alog_row, dtb_row)
    sd_s, o_s = _sample_step(state_delta[0], q_s, k_s, v_s, beta_s, eg_s, z_s, onw_row)
    o_s2 = o_s.reshape(n_s, QK_W).astype(bf16)
    x1_s, h2_s, mi_s, mw_s, cnt = _mix_route(xs2, o_s2, y_s, proj_s, wa, wb, wo, n2_row, rwh, rwl, rb_row, cnt_p)

    starts_row, items = _work_items(cnt, 2 * n_all)
    mi_flat = jnp.concatenate([_dest_rows(mi_p, starts_row)[:, 0:MI_W], _dest_rows(mi_s, starts_row)[:, 0:MI_W]],
                              axis=0).reshape(MI_W * n_all)
    xs_sorted = _dispatch(h2_p, h2_s, mi_flat)
    ys = _moe(xs_sorted.reshape(2 * n_all, D_MODEL), w_gate[0], w_up[0], w_down[0], items)
    y_prompt, y_sample = _combine(x1_p, mw_p, x1_s, mw_s, final_norm_w.reshape(1, D_MODEL),
                                  ys.reshape(2 * n_all // SUBLANE, SUBLANE, D_MODEL), mi_flat)

    return (y_prompt.reshape(bp, tp, D_MODEL),
            y_sample.reshape(bs, ts, D_MODEL),
            sd_p[None],
            nca_p[None],
            ncb_p[None],
            sd_s[None],
            jnp.transpose(nbufa_t, (1, 0, 2))[None],
            jnp.transpose(nbufb_t, (1, 0, 2))[None])
```

```python
import functools

import jax
import jax.numpy as jnp
from jax import lax
from jax.experimental import pallas as pl
from jax.experimental.pallas import tpu as pltpu

f32 = jnp.float32
bf16 = jnp.bfloat16
i32 = jnp.int32

EPS = 1e-6
LANE = 128
D_MODEL = 2048
N_HEADS = 8
HEAD = 128
QK_W = N_HEADS * HEAD
QKV_W = 3 * QK_W
SC_W = 1024
CONV_A = 4
CONV_B = 3
CHUNK = 64
CHUNK_SHIFT = CHUNK.bit_length() - 1
HEAD_SHIFT = HEAD.bit_length() - 1
GROUP_HEADS = 4
DECAY_LANE = N_HEADS
N_EXPERTS = 64
N_GROUPS = 8
EXPERTS_PER_GROUP = 8
GROUP_SHIFT = EXPERTS_PER_GROUP.bit_length() - 1
D_FF = 512
MOE_ROWS = 128
TOKEN_TILE = 512
W_SLOTS = 2

COL_QKV = 0
COL_BCX = 3072
COL_GA = 6144
COL_GB = 8192
COL_Z = 10240
COL_BA = 11264
BA_W = 256
PROJ_W = 11520
PROJ_TN = 1280

VMEM_LIMIT = 56 * 1024 * 1024


def _dot(a, b):
    return jnp.dot(a, b, preferred_element_type=f32)


def _dot_nt(a, b):
    return lax.dot_general(a, b, (((1,), (1,)), ((), ())), preferred_element_type=f32)


def _split(x, n):
    parts = []
    r = x
    for i in range(n):
        p = r.astype(bf16)
        parts.append(p)
        if i + 1 < n:
            r = r - p.astype(f32)
    return parts


def _dot_lsplit(x, m, n=3):
    rows = x.shape[0]
    d = _dot(jnp.concatenate(_split(x, n), axis=0), m)
    acc = d[0:rows]
    for i in range(1, n):
        acc = acc + d[i * rows:(i + 1) * rows]
    return acc


def _dot_rsplit(m, x, n=3):
    cols = x.shape[1]
    d = _dot(m, jnp.concatenate(_split(x, n), axis=1))
    acc = d[:, 0:cols]
    for i in range(1, n):
        acc = acc + d[:, i * cols:(i + 1) * cols]
    return acc


_sigmoid = jax.nn.sigmoid


def _silu(x):
    return x * _sigmoid(x)


def _softplus(x):
    return jnp.maximum(x, 0.0) + jnp.log(1.0 + jnp.exp(-jnp.abs(x)))


def _cparams(sem):
    return pltpu.CompilerParams(dimension_semantics=sem, vmem_limit_bytes=VMEM_LIMIT)


def _inproj_kernel(x_ref, nw_ref, w_ref, o_ref, h_ref, *, rows):
    @pl.when(pl.program_id(1) == 0)
    def _():
        def body(r, c):
            sl = pl.ds(pl.multiple_of(r * rows, rows), rows)
            x = x_ref[sl, :]
            ms = jnp.mean(x * x, axis=-1, keepdims=True)
            h_ref[sl, :] = (x * lax.rsqrt(ms + EPS) * nw_ref[...]).astype(bf16)
            return c
        lax.fori_loop(0, x_ref.shape[0] // rows, body, 0)

    o_ref[...] = _dot_nt(h_ref[...], w_ref[...])


def _inproj(x2d, norm_w, w_bf16):
    n = x2d.shape[0]
    tm = min(1024, n)
    assert n % tm == 0 and PROJ_W % PROJ_TN == 0
    return pl.pallas_call(
        functools.partial(_inproj_kernel, rows=min(128, tm)),
        out_shape=jax.ShapeDtypeStruct((n, PROJ_W), f32),
        grid=(n // tm, PROJ_W // PROJ_TN),
        in_specs=[pl.BlockSpec((tm, D_MODEL), lambda i, j: (i, 0)),
                  pl.BlockSpec((1, D_MODEL), lambda i, j: (0, 0)),
                  pl.BlockSpec((PROJ_TN, D_MODEL), lambda i, j: (j, 0))],
        out_specs=pl.BlockSpec((tm, PROJ_TN), lambda i, j: (i, j)),
        scratch_shapes=[pltpu.VMEM((tm, D_MODEL), bf16)],
        compiler_params=_cparams(("arbitrary", "arbitrary")),
        name="inproj",
    )(x2d, norm_w.reshape(1, D_MODEL), w_bf16)


CONV_TILES = 3
CONV_COLS = 2 * HEAD
CONV_ROWS = 128
CONV_PAD_W = CONV_TILES * PROJ_TN


def _qkv_kind(col):
    return "q" if col < QK_W else "k" if col < 2 * QK_W else "v" if col < QKV_W else "raw"


def _inproj_conv_kernel(x_ref, nw_ref, w_ref, cw_ref, o_ref, tail_ref, h_ref, hist_ref, raw_ref, *,
                        rows, tiles_per_seq):
    i = pl.program_id(0)
    j = pl.program_id(1)
    tm = x_ref.shape[0]

    @pl.when(j == 0)
    def _():
        def body(r, c):
            sl = pl.ds(pl.multiple_of(r * rows, rows), rows)
            x = x_ref[sl, :]
            ms = jnp.mean(x * x, axis=-1, keepdims=True)
            h_ref[sl, :] = (x * lax.rsqrt(ms + EPS) * nw_ref[...]).astype(bf16)
            return c
        lax.fori_loop(0, tm // rows, body, 0)

    @pl.when((i == 0) & (j == 0))
    def _():
        hist_ref[...] = jnp.zeros(hist_ref.shape, f32)

    @pl.when(j >= CONV_TILES)
    def _():
        o_ref[...] = _dot_nt(h_ref[...], w_ref[...])

    seq_start = lax.rem(i, tiles_per_seq) == 0
    for jj in range(CONV_TILES):
        @pl.when(j == jj)
        def _():
            def matmul_chunk(idx, c0):
                raw_ref[idx % 2] = _dot_nt(h_ref[...], w_ref[c0:c0 + CONV_COLS, :])

            def conv_chunk(idx, c0):
                cs = slice(c0, c0 + CONV_COLS)
                raw = raw_ref.at[idx % 2]
                tail = raw[tm - 8:tm, :]
                tail_ref[0, :, cs] = tail
                kinds = [_qkv_kind(jj * PROJ_TN + c0 + g * HEAD) for g in range(CONV_COLS // HEAD)]
                if kinds[0] == "raw":
                    o_ref[:, cs] = raw[...]
                    return
                hist = jnp.where(seq_start, 0.0, hist_ref[jj, :, cs])
                for rc in range(tm // CONV_ROWS):
                    r0 = rc * CONV_ROWS
                    if rc > 0:
                        xe = raw[r0 - 8:r0 + CONV_ROWS, :]
                    else:
                        xe = jnp.concatenate([hist, raw[0:CONV_ROWS, :]], axis=0)
                    acc = pltpu.roll(xe, 3, axis=0)[8:] * cw_ref[0:1, cs]
                    acc = acc + pltpu.roll(xe, 2, axis=0)[8:] * cw_ref[1:2, cs]
                    acc = acc + pltpu.roll(xe, 1, axis=0)[8:] * cw_ref[2:3, cs]
                    acc = acc + xe[8:] * cw_ref[3:4, cs]
                    act = _silu(acc)
                    for g, kind in enumerate(kinds):
                        ah = act[:, g * HEAD:(g + 1) * HEAD]
                        if kind != "v":
                            ss = jnp.sum(ah * ah, axis=-1, keepdims=True)
                            inv = lax.rsqrt(ss + EPS)
                            ah = ah * (inv * (HEAD ** -0.5) if kind == "q" else inv)
                        o_ref[r0:r0 + CONV_ROWS, c0 + g * HEAD:c0 + (g + 1) * HEAD] = ah
                hist_ref[jj, :, cs] = tail

            chunks = list(range(0, PROJ_TN, CONV_COLS))
            matmul_chunk(0, chunks[0])
            for idx in range(1, len(chunks)):
                matmul_chunk(idx, chunks[idx])
                conv_chunk(idx - 1, chunks[idx - 1])
            conv_chunk(len(chunks) - 1, chunks[-1])


def _inproj_conv(x2d, norm_w, w_bf16, cwa, seq_len):
    n = x2d.shape[0]
    tm = min(1024, seq_len)
    assert n % tm == 0 and seq_len % tm == 0 and PROJ_W % PROJ_TN == 0 and tm % CONV_ROWS == 0
    assert QKV_W % CONV_COLS == 0 and PROJ_TN % CONV_COLS == 0
    cw_pad = jnp.zeros((CONV_A, CONV_PAD_W), f32).at[:, 0:QKV_W].set(cwa)
    last = CONV_TILES - 1
    return pl.pallas_call(
        functools.partial(_inproj_conv_kernel, rows=min(128, tm), tiles_per_seq=seq_len // tm),
        out_shape=(jax.ShapeDtypeStruct((n, PROJ_W), f32),
                   jax.ShapeDtypeStruct((n // tm, 8, CONV_PAD_W), f32)),
        grid=(n // tm, PROJ_W // PROJ_TN),
        in_specs=[pl.BlockSpec((tm, D_MODEL), lambda i, j: (i, 0)),
                  pl.BlockSpec((1, D_MODEL), lambda i, j: (0, 0)),
                  pl.BlockSpec((PROJ_TN, D_MODEL), lambda i, j: (j, 0)),
                  pl.BlockSpec((CONV_A, PROJ_TN), lambda i, j: (0, jnp.minimum(j, last)))],
        out_specs=(pl.BlockSpec((tm, PROJ_TN), lambda i, j: (i, j)),
                   pl.BlockSpec((1, 8, PROJ_TN), lambda i, j: (i, 0, jnp.minimum(j, last)))),
        scratch_shapes=[pltpu.VMEM((tm, D_MODEL), bf16), pltpu.VMEM((CONV_TILES, 8, PROJ_TN), f32),
                        pltpu.VMEM((2, tm, CONV_COLS), f32)],
        compiler_params=_cparams(("arbitrary", "arbitrary")),
        name="inproj_conv",
    )(x2d, norm_w.reshape(1, D_MODEL), w_bf16, cw_pad)


W_IN_COLS = 11280
WPREP_TN = 1024
WPREP_SHIFT = 16
WP_BCX = COL_BCX // WPREP_TN
WP_Z = COL_Z // WPREP_TN
WP_BA = COL_BA // WPREP_TN
WP_SRC_Z = QKV_W // WPREP_TN
WP_SRC_BA = WP_SRC_Z + 1


def _wprep_kernel(a_ref, b_ref, o_ref):
    j = pl.program_id(0)
    keep = WPREP_TN - WPREP_SHIFT

    @pl.when((j < WP_BCX) | (j == WP_Z))
    def _():
        o_ref[...] = a_ref[...].astype(bf16)

    @pl.when((j >= WP_BCX) & (j < WP_Z))
    def _():
        o_ref[0:keep, :] = a_ref[WPREP_SHIFT:WPREP_TN, :].astype(bf16)
        o_ref[keep:WPREP_TN, :] = b_ref[...].astype(bf16)

    @pl.when(j == WP_BA)
    def _():
        o_ref[0:WPREP_SHIFT, :] = a_ref[0:WPREP_SHIFT, :].astype(bf16)
        o_ref[WPREP_SHIFT:WPREP_TN, :] = jnp.zeros((keep, D_MODEL), bf16)


def _wprep(w_in_t):
    assert w_in_t.shape == (W_IN_COLS, D_MODEL) and 2 * N_HEADS == WPREP_SHIFT
    n_blk = pl.cdiv(PROJ_W, WPREP_TN)

    def a_map(j):
        return (jnp.where(j < WP_BCX, j, jnp.where(j < WP_Z, j + 1, jnp.where(j == WP_Z, WP_SRC_Z, WP_SRC_BA))), 0)

    def b_map(j):
        return (jnp.minimum((WPREP_TN // WPREP_SHIFT) * (j + 2), W_IN_COLS // WPREP_SHIFT - 1), 0)

    return pl.pallas_call(
        _wprep_kernel,
        out_shape=jax.ShapeDtypeStruct((PROJ_W, D_MODEL), bf16),
        grid=(n_blk,),
        in_specs=[pl.BlockSpec((WPREP_TN, D_MODEL), a_map),
                  pl.BlockSpec((WPREP_SHIFT, D_MODEL), b_map)],
        out_specs=pl.BlockSpec((WPREP_TN, D_MODEL), lambda j: (j, 0)),
        compiler_params=_cparams(("arbitrary",)),
        name="wprep",
    )(w_in_t, w_in_t)


def _head_l2norm(a, scale):
    outs = []
    for h in range(N_HEADS):
        ah = a[:, h * HEAD:(h + 1) * HEAD]
        ss = jnp.sum(ah * ah, axis=-1, keepdims=True)
        n = ah * lax.rsqrt(ss + EPS)
        outs.append(n * scale if scale != 1.0 else n)
    return outs


def _delta_prompt_kernel(qkv_ref, bcx_ref, z_ref, ba_ref, cwb_ref, alog_ref, dtb_ref, onw_ref,
                         e64_ref,
                         o_ref, y_ref, snew_ref, ncb_ref,
                         s_ref, xb_ref, *, nb_step):
    C = CHUNK
    G = GROUP_HEADS
    R = G * C
    t = pl.program_id(1)
    nt = pl.num_programs(1)

    @pl.when(t == 0)
    def _():
        s_ref[...] = jnp.zeros(s_ref.shape, f32)
        xb_ref[:, 0:8, :] = jnp.zeros((nb_step, 8, SC_W), f32)

    rr = lax.broadcasted_iota(i32, (R, R), 0)
    cc = lax.broadcasted_iota(i32, (R, R), 1)
    same_bf = jnp.where((rr >> CHUNK_SHIFT) == (cc >> CHUNK_SHIFT), 1.0, 0.0).astype(bf16)
    r2 = lax.broadcasted_iota(i32, (R, G * HEAD), 0)
    c2 = lax.broadcasted_iota(i32, (R, G * HEAD), 1)
    bdmask = (r2 >> CHUNK_SHIFT) == (c2 >> HEAD_SHIFT)
    r3 = lax.broadcasted_iota(i32, (C, C), 0)
    c3 = lax.broadcasted_iota(i32, (C, C), 1)
    ltri = jnp.where(r3 >= c3, 1.0, 0.0).astype(bf16)
    r4 = lax.broadcasted_iota(i32, (C, R), 0)
    c4 = lax.broadcasted_iota(i32, (C, R), 1)
    ident_t = r4 == (c4 & (C - 1))
    incl_p = r4 >= (c4 & (C - 1))
    strict_p = r4 > (c4 & (C - 1))
    hblk = c4 >> CHUNK_SHIFT
    ones8 = jnp.ones((8, C), bf16)

    nbs = range(nb_step)
    units = [(nb, g) for nb in nbs for g in range(N_HEADS // G)]
    heads = lambda g: range(g * G, (g + 1) * G)

    qn = [[qkv_ref[nb, :, h * HEAD:(h + 1) * HEAD] for h in range(N_HEADS)] for nb in nbs]
    kn = [[qkv_ref[nb, :, QK_W + h * HEAD:QK_W + (h + 1) * HEAD] for h in range(N_HEADS)] for nb in nbs]
    vv = [qkv_ref[nb, :, 2 * QK_W:3 * QK_W] for nb in nbs]

    bts = [ba_ref[nb, :, 0:LANE] for nb in nbs]
    beta_all = [_sigmoid(bt) for bt in bts]
    g_all = [-(jnp.exp(alog_ref[:, 0:LANE]) * _softplus(bt + dtb_ref[:, 0:LANE])) for bt in bts]
    gc_small = [_dot_rsplit(ltri, ga) for ga in g_all]
    gl_small = [gc[C - 1:C, :] for gc in gc_small]

    k_st, q_st, kb, vb, kbg, qd, kd, gc_col = ({} for _ in range(8))
    for u in units:
        nb, g = u
        hs = heads(g)
        k_st[u] = jnp.concatenate([kn[nb][h] for h in hs], axis=0)
        q_st[u] = jnp.concatenate([qn[nb][h] for h in hs], axis=0)
        v_st = jnp.concatenate([vv[nb][:, h * HEAD:(h + 1) * HEAD] for h in hs], axis=0)
        beta_col = jnp.concatenate([beta_all[nb][:, h:h + 1] for h in hs], axis=0)
        gc_col[u] = jnp.concatenate([gc_small[nb][:, DECAY_LANE + h:DECAY_LANE + h + 1] for h in hs], axis=0)
        gl_col = jnp.concatenate(
            [jnp.broadcast_to(gl_small[nb][:, DECAY_LANE + h:DECAY_LANE + h + 1], (C, 1)) for h in hs], axis=0)
        kb[u] = k_st[u] * beta_col
        vb[u] = v_st * beta_col
        egc = jnp.exp(gc_col[u])
        kbg[u] = kb[u] * egc
        qd[u] = q_st[u] * egc
        kd[u] = k_st[u] * jnp.exp(gl_col - gc_col[u])

    gx = {u: _dot_lsplit(gc_small[u[0]], e64_ref[u[1], 0:LANE, :]) for u in units}
    crow = {u: _dot_rsplit(ones8, jnp.where(ident_t, gx[u], 0.0))[0:1, :] for u in units}
    a = {u: _dot_nt(jnp.concatenate([kb[u], q_st[u]], axis=0).astype(bf16), k_st[u].astype(bf16))
         for u in units}
    in_blk = [hblk == h for h in range(G - 1)]

    def pack(x):
        out = x[(G - 1) * C:G * C]
        for h in reversed(range(G - 1)):
            out = jnp.where(in_blk[h], x[h * C:(h + 1) * C], out)
        return out

    def expand(xp):
        return jnp.concatenate([xp.astype(bf16)] * G, axis=0) * same_bf

    dec = {u: jnp.where(incl_p, jnp.exp(jnp.where(incl_p, gx[u] - crow[u], 0.0)), 0.0) for u in units}
    nm = {u: jnp.where(strict_p, -(pack(a[u][0:R]) * dec[u]), 0.0) for u in units}
    qkm = {u: expand(pack(a[u][R:2 * R]) * dec[u]) for u in units}

    p = {u: jnp.where(ident_t, 1.0, 0.0) + nm[u] for u in units}
    nk = {u: _dot(nm[u].astype(bf16), expand(nm[u])) for u in units}
    for _ in range(4):
        for u in units:
            x = _dot(jnp.concatenate([p[u], nk[u]], axis=0).astype(bf16), expand(nk[u]))
            p[u] = p[u] + x[0:C]
            nk[u] = x[C:2 * C]
    for u in units:
        p[u] = p[u] + _dot(p[u].astype(bf16), expand(nk[u]))
    uw = {u: _dot(expand(p[u]), jnp.concatenate([vb[u], kbg[u]], axis=1).astype(bf16)) for u in units}

    ws = {}
    for u in units:
        nb, g = u
        for j, h in enumerate(heads(g)):
            sh = s_ref[nb, :, h * HEAD:(h + 1) * HEAD]
            lhs = jnp.concatenate([uw[u][j * C:(j + 1) * C, HEAD:2 * HEAD], qd[u][j * C:(j + 1) * C]], axis=0)
            ws[u, j] = _dot(lhs.astype(bf16), sh.astype(bf16))
    o_heads = {}
    for u in units:
        nb, g = u
        vnew_st = jnp.concatenate([uw[u][j * C:(j + 1) * C, 0:HEAD] - ws[u, j][0:C] for j in range(G)], axis=0)
        o_st = (jnp.concatenate([ws[u, j][C:2 * C] for j in range(G)], axis=0)
                + _dot(qkm[u], vnew_st.astype(bf16)))
        vbd = jnp.where(bdmask, jnp.concatenate([vnew_st] * G, axis=1), 0.0)
        lo = g * G * HEAD
        hi = lo + G * HEAD
        gl_row = jnp.concatenate(
            [jnp.broadcast_to(jnp.exp(gl_small[nb][:, DECAY_LANE + h:DECAY_LANE + h + 1]), (1, HEAD)) for h in heads(g)], axis=1)
        s_ref[nb, :, lo:hi] = s_ref[nb, :, lo:hi] * gl_row + _dot(kd[u].T.astype(bf16), vbd.astype(bf16))
        for j, h in enumerate(heads(g)):
            o_heads[nb, h] = o_st[j * C:(j + 1) * C]

    for nb in nbs:
        zt = z_ref[nb]
        for h in range(N_HEADS):
            oh = o_heads[nb, h]
            ms = jnp.mean(oh * oh, axis=-1, keepdims=True)
            zh = zt[:, h * HEAD:(h + 1) * HEAD]
            on = oh * lax.rsqrt(ms + EPS) * onw_ref[...] * _silu(zh)
            o_ref[nb, :, h * HEAD:(h + 1) * HEAD] = on.astype(bf16)

    for nb in nbs:
        bcx = bcx_ref[nb]
        cx = bcx[:, SC_W:2 * SC_W] * bcx[:, 2 * SC_W:3 * SC_W]
        xb_ref[nb, 8:8 + C, :] = cx
        ce = xb_ref[nb]
        cv = pltpu.roll(ce, 2, axis=0)[8:8 + C] * cwb_ref[0:1, :]
        cv = cv + pltpu.roll(ce, 1, axis=0)[8:8 + C] * cwb_ref[1:2, :]
        cv = cv + cx * cwb_ref[2:3, :]
        y_ref[nb] = (bcx[:, 0:SC_W] * cv).astype(bf16)
        xb_ref[nb, 0:8, :] = xb_ref[nb, C:C + 8, :]

    @pl.when(t == nt - 1)
    def _():
        for nb in range(nb_step):
            for h in range(N_HEADS):
                snew_ref[nb, h] = s_ref[nb, :, h * HEAD:(h + 1) * HEAD]
            ncb_ref[nb] = xb_ref[nb, 6:8, :]


def _expand_consts():
    lane = jnp.arange(BA_W)[:, None]
    col = jnp.arange(QK_W)[None, :]
    eb = (lane == (col >> HEAD_SHIFT)).astype(bf16)
    eg = (lane == (DECAY_LANE + (col >> HEAD_SHIFT))).astype(bf16)
    col64 = jnp.arange(GROUP_HEADS * CHUNK)[None, :]
    e64 = jnp.stack([(lane == (DECAY_LANE + g * GROUP_HEADS + (col64 >> CHUNK_SHIFT))).astype(bf16)
                     for g in range(N_HEADS // GROUP_HEADS)], axis=0)
    return eb, eg, e64


def _delta_prompt(proj3, cwb, alog_row, dtb_row, onw_row, nb_step):
    b, t, _ = proj3.shape
    assert t % CHUNK == 0 and b % nb_step == 0
    _, _, e64 = _expand_consts()
    c = CHUNK
    const2 = lambda bi, ti: (0, 0)
    outs = pl.pallas_call(
        functools.partial(_delta_prompt_kernel, nb_step=nb_step),
        out_shape=(jax.ShapeDtypeStruct((b, t, QK_W), bf16),
                   jax.ShapeDtypeStruct((b, t, SC_W), bf16),
                   jax.ShapeDtypeStruct((b, N_HEADS, HEAD, HEAD), f32),
                   jax.ShapeDtypeStruct((b, CONV_B - 1, SC_W), f32)),
        grid=(b // nb_step, t // c),
        in_specs=[pl.BlockSpec((nb_step, c, QKV_W), lambda bi, ti: (bi, ti, COL_QKV // QKV_W)),
                  pl.BlockSpec((nb_step, c, QKV_W), lambda bi, ti: (bi, ti, COL_BCX // QKV_W)),
                  pl.BlockSpec((nb_step, c, QK_W), lambda bi, ti: (bi, ti, COL_Z // QK_W)),
                  pl.BlockSpec((nb_step, c, BA_W), lambda bi, ti: (bi, ti, COL_BA // BA_W)),
                  pl.BlockSpec((CONV_B, SC_W), const2),
                  pl.BlockSpec((1, BA_W), const2),
                  pl.BlockSpec((1, BA_W), const2),
                  pl.BlockSpec((1, HEAD), const2),
                  pl.BlockSpec((N_HEADS // GROUP_HEADS, BA_W, GROUP_HEADS * CHUNK), lambda bi, ti: (0, 0, 0))],
        out_specs=(pl.BlockSpec((nb_step, c, QK_W), lambda bi, ti: (bi, ti, 0)),
                   pl.BlockSpec((nb_step, c, SC_W), lambda bi, ti: (bi, ti, 0)),
                   pl.BlockSpec((nb_step, N_HEADS, HEAD, HEAD), lambda bi, ti: (bi, 0, 0, 0)),
                   pl.BlockSpec((nb_step, CONV_B - 1, SC_W), lambda bi, ti: (bi, 0, 0))),
        scratch_shapes=[pltpu.VMEM((nb_step, HEAD, QK_W), f32),
                        pltpu.VMEM((nb_step, 8 + c, SC_W), f32)],
        compiler_params=_cparams(("arbitrary", "arbitrary")),
        name="delta_prompt",
    )(proj3, proj3, proj3, proj3, cwb, alog_row, dtb_row, onw_row, e64)
    return outs


def _sample_prep_kernel(p_ref, bufa_ref, bufb_ref, cwa_ref, cwb_ref, alog_ref, dtb_ref, eb_ref, eg_ref,
                        q_ref, k_ref, v_ref, beta_ref, eg_out_ref, z_ref, y_ref, nbufa_ref, nbufb_ref):
    def put_heads(ref, a):
        for h in range(N_HEADS):
            ref[:, h, :] = a[:, h * HEAD:(h + 1) * HEAD]

    def conv_sec(lo):
        hi = lo + QK_W
        raw = p_ref[:, COL_QKV + lo:COL_QKV + hi]
        acc = bufa_ref[0, :, lo:hi] * cwa_ref[0:1, lo:hi]
        acc = acc + bufa_ref[1, :, lo:hi] * cwa_ref[1:2, lo:hi]
        acc = acc + bufa_ref[2, :, lo:hi] * cwa_ref[2:3, lo:hi]
        acc = acc + raw * cwa_ref[3:4, lo:hi]
        nbufa_ref[0, :, lo:hi] = bufa_ref[1, :, lo:hi]
        nbufa_ref[1, :, lo:hi] = bufa_ref[2, :, lo:hi]
        nbufa_ref[2, :, lo:hi] = raw
        return _silu(acc)

    qn = _head_l2norm(conv_sec(0), HEAD ** -0.5)
    kn = _head_l2norm(conv_sec(QK_W), 1.0)
    for h in range(N_HEADS):
        q_ref[:, h, :] = qn[h]
        k_ref[:, h, :] = kn[h]
    put_heads(v_ref, conv_sec(2 * QK_W))
    put_heads(z_ref, p_ref[:, COL_Z:COL_Z + QK_W])

    bt = p_ref[:, COL_BA:COL_BA + BA_W]
    beta_all = _sigmoid(bt)
    g_all = -(jnp.exp(alog_ref[...]) * _softplus(bt + dtb_ref[...]))
    put_heads(beta_ref, _dot_lsplit(beta_all, eb_ref[...]))
    put_heads(eg_out_ref, jnp.exp(_dot_lsplit(g_all, eg_ref[...])))

    bg = p_ref[:, COL_BCX:COL_BCX + SC_W]
    cx = p_ref[:, COL_BCX + SC_W:COL_BCX + 2 * SC_W] * p_ref[:, COL_BCX + 2 * SC_W:COL_BCX + 3 * SC_W]
    cv = bufb_ref[0] * cwb_ref[0:1, :]
    cv = cv + bufb_ref[1] * cwb_ref[1:2, :]
    cv = cv + cx * cwb_ref[2:3, :]
    y_ref[...] = (bg * cv).astype(bf16)
    nbufb_ref[0] = bufb_ref[1]
    nbufb_ref[1] = cx


def _sample_prep(proj_s, bufa_t, bufb_t, cwa, cwb, alog_row, dtb_row):
    n = proj_s.shape[0]
    eb, eg, _ = _expand_consts()
    row = jax.ShapeDtypeStruct((n, N_HEADS, HEAD), f32)
    return pl.pallas_call(
        _sample_prep_kernel,
        out_shape=(row, row, row, row, row, row,
                   jax.ShapeDtypeStruct((n, SC_W), bf16),
                   jax.ShapeDtypeStruct((CONV_A - 1, n, QKV_W), f32),
                   jax.ShapeDtypeStruct((CONV_B - 1, n, SC_W), f32)),
        compiler_params=pltpu.CompilerParams(vmem_limit_bytes=VMEM_LIMIT),
        name="sample_prep",
    )(proj_s, bufa_t, bufb_t, cwa, cwb, alog_row, dtb_row, eb, eg)


def _sample_step_kernel(s_ref, q_ref, k_ref, v_ref, beta_ref, eg_ref, z_ref, onw_ref,
                        snew_ref, o_ref, *, bb):
    w = N_HEADS * HEAD
    r8 = lax.broadcasted_iota(i32, (N_HEADS, w), 0)
    c8 = lax.broadcasted_iota(i32, (N_HEADS, w), 1)
    mask8 = r8 == (c8 >> HEAD_SHIFT)
    zpad_k = jnp.zeros((HEAD - N_HEADS, HEAD), f32)
    hb = lambda h: slice(h * HEAD, (h + 1) * HEAD)
    bs = range(bb)
    s_dec, k8s, kts = [], [], []
    for b in bs:
        s_all = jnp.concatenate([s_ref[b, h] for h in range(N_HEADS)], axis=1)
        eg8 = eg_ref[b]
        eg_row = jnp.concatenate([eg8[h:h + 1, :] for h in range(N_HEADS)], axis=1)
        s_dec.append(s_all * eg_row)
        k8s.append(k_ref[b])
        kts.append(jnp.concatenate([k8s[b], zpad_k], axis=0).T)
    xs = [_dot(k8s[b].astype(bf16), s_dec[b].astype(bf16)) for b in bs]
    s_new = []
    for b in bs:
        vb8, bt8 = v_ref[b], beta_ref[b]
        upd = [kts[b][:, h:h + 1] * ((vb8[h:h + 1, :] - xs[b][h:h + 1, hb(h)]) * bt8[h:h + 1, :])
               for h in range(N_HEADS)]
        s_new.append(s_dec[b] + jnp.concatenate(upd, axis=1))
    ys = [_dot(q_ref[b].astype(bf16), s_new[b].astype(bf16)) for b in bs]
    for b in bs:
        yv = jnp.where(mask8, ys[b], 0.0)
        o8 = yv[:, 0:HEAD]
        for j in range(1, N_HEADS):
            o8 = o8 + yv[:, j * HEAD:(j + 1) * HEAD]
        ms = jnp.mean(o8 * o8, axis=-1, keepdims=True)
        o_ref[b] = o8 * lax.rsqrt(ms + EPS) * onw_ref[...] * _silu(z_ref[b])
        for h in range(N_HEADS):
            snew_ref[b, h] = s_new[b][:, hb(h)]


def _sample_step(state, q, k, v, beta, eg, z, onw_row, bb=8):
    n = state.shape[0]
    assert n % bb == 0
    hspec = pl.BlockSpec((bb, N_HEADS, HEAD), lambda i: (i, 0, 0))
    sspec = pl.BlockSpec((bb, N_HEADS, HEAD, HEAD), lambda i: (i, 0, 0, 0))
    return pl.pallas_call(
        functools.partial(_sample_step_kernel, bb=bb),
        out_shape=(jax.ShapeDtypeStruct(state.shape, f32),
                   jax.ShapeDtypeStruct((n, N_HEADS, HEAD), f32)),
        grid=(n // bb,),
        in_specs=[sspec, hspec, hspec, hspec, hspec, hspec, hspec, pl.BlockSpec((1, HEAD), lambda i: (0, 0))],
        out_specs=(sspec, hspec),
        compiler_params=_cparams(("arbitrary",)),
        name="sample_step",
    )(state, q, k, v, beta, eg, z, onw_row)


def _mix_route_kernel(x_ref, o_ref, y_ref, ga_ref, gb_ref, wa_ref, wb_ref, wo_ref, n2_ref,
                      rwh_ref, rwl_ref, rb_ref, cnt_in_ref, x1_ref, h2_ref, mi_ref, mw_ref, cnt_ref):
    i = pl.program_id(0)
    tm = x_ref.shape[0]

    @pl.when(i == 0)
    def _():
        cnt_ref[...] = cnt_in_ref[...]

    oa = _dot(o_ref[...], wa_ref[...])
    ob = _dot(y_ref[...], wb_ref[...])
    merged = _sigmoid(ga_ref[...]) * oa + _sigmoid(gb_ref[...]) * ob
    x1 = x_ref[...] + _dot(merged.astype(bf16), wo_ref[...])
    x1_ref[...] = x1
    ms = jnp.mean(x1 * x1, axis=-1, keepdims=True)
    h2 = x1 * lax.rsqrt(ms + EPS) * n2_ref[...]
    h2_ref[...] = h2

    h_hi, h_lo = _split(h2, 2)
    logits = _dot(h_hi, rwh_ref[...]) + _dot(h_hi, rwl_ref[...]) + _dot(h_lo, rwh_ref[...]) + rb_ref[...]

    lane = lax.broadcasted_iota(i32, (tm, LANE), 1)
    lanef = lane.astype(f32)
    neg = jnp.float32(-jnp.inf)
    big = jnp.float32(1e9)
    gmask = (lane >= N_EXPERTS) & (lane < N_EXPERTS + N_GROUPS)
    gl = jnp.where(gmask, logits, neg)
    gmax = jnp.max(gl, axis=-1, keepdims=True)
    gidx = jnp.min(jnp.where(gl == gmax, lanef - N_EXPERTS, big), axis=-1, keepdims=True)
    gsum = jnp.sum(jnp.where(gmask, jnp.exp(gl - gmax), 0.0), axis=-1, keepdims=True)
    gprob = 1.0 / gsum

    emask = (lane < N_EXPERTS) & ((lane >> GROUP_SHIFT).astype(f32) == gidx)
    el = jnp.where(emask, logits, neg)
    emax = jnp.max(el, axis=-1, keepdims=True)
    pe = jnp.where(emask, jnp.exp(el - emax), 0.0)
    eprob = pe / jnp.sum(pe, axis=-1, keepdims=True)
    p1m = jnp.where(emask, eprob, -1.0)
    m1 = jnp.max(p1m, axis=-1, keepdims=True)
    i1 = jnp.min(jnp.where(p1m == m1, lanef, big), axis=-1, keepdims=True)
    p2m = jnp.where(lanef == i1, -1.0, p1m)
    m2 = jnp.max(p2m, axis=-1, keepdims=True)
    i2 = jnp.min(jnp.where(p2m == m2, lanef, big), axis=-1, keepdims=True)
    tot = m1 + m2
    c1 = m1 / tot * gprob
    c2 = m2 / tot * gprob

    oh1 = jnp.where(lanef == i1, 1.0, 0.0)
    oh2 = jnp.where(lanef == i2, 1.0, 0.0)
    ohs = oh1 + oh2
    rt = lax.broadcasted_iota(i32, (tm, tm), 0)
    ct = lax.broadcasted_iota(i32, (tm, tm), 1)
    lstrict = jnp.where(rt > ct, 1.0, 0.0).astype(bf16)
    cs = _dot(lstrict, ohs.astype(bf16)) + cnt_ref[...]
    rank1 = jnp.sum(cs * oh1, axis=-1, keepdims=True)
    rank2 = jnp.sum(cs * oh2, axis=-1, keepdims=True)
    cnt_ref[...] = cnt_ref[...] + jnp.sum(ohs, axis=0, keepdims=True)

    mi = jnp.where(lane == 0, i1, jnp.where(lane == 1, i2, jnp.where(lane == 2, rank1,
                                                                     jnp.where(lane == 3, rank2, 0.0))))
    mi_ref[...] = mi.astype(i32)
    mw_ref[...] = jnp.where(lane == 0, c1, jnp.where(lane == 1, c2, 0.0))


def _mix_route(x2d, o2d, y2d, proj2d, wa, wb, wo, n2_row, rwh, rwl, rb_row, cnt_in):
    n = x2d.shape[0]
    tm = min(256, n)
    assert n % tm == 0
    tok = lambda width: pl.BlockSpec((tm, width), lambda i: (i, 0))
    full = lambda a: pl.BlockSpec(a.shape, lambda i: (0,) * a.ndim)
    in_specs = [tok(D_MODEL), tok(QK_W), tok(SC_W),
                pl.BlockSpec((tm, D_MODEL), lambda i: (i, COL_GA // D_MODEL)),
                pl.BlockSpec((tm, D_MODEL), lambda i: (i, COL_GB // D_MODEL)),
                full(wa), full(wb), full(wo), full(n2_row), full(rwh), full(rwl), full(rb_row), full(cnt_in)]
    out_shape = (jax.ShapeDtypeStruct((n, D_MODEL), f32),
                 jax.ShapeDtypeStruct((n, D_MODEL), f32),
                 jax.ShapeDtypeStruct((n, LANE), i32),
                 jax.ShapeDtypeStruct((n, LANE), f32),
                 jax.ShapeDtypeStruct((1, LANE), f32))
    out_specs = (tok(D_MODEL), tok(D_MODEL), tok(LANE), tok(LANE),
                 pl.BlockSpec((1, LANE), lambda i: (0, 0)))
    return pl.pallas_call(
        _mix_route_kernel,
        out_shape=out_shape,
        grid=(n // tm,),
        in_specs=in_specs,
        out_specs=out_specs,
        compiler_params=_cparams(("arbitrary",)),
        name="mix_route",
    )(x2d, o2d, y2d, proj2d, proj2d, wa, wb, wo, n2_row, rwh, rwl, rb_row, cnt_in)


MI_W = 4
SUBLANE = 8
COMBINE_CHUNK_TILES = 8


def _dest_kernel(mi_ref, starts_ref, o_ref):
    mi = mi_ref[...]
    lane = lax.broadcasted_iota(i32, mi.shape, 1)
    st = starts_ref[...]

    def first_row(e_col):
        return jnp.sum(jnp.where(lane == e_col, st, 0.0), axis=-1, keepdims=True).astype(i32)

    d0 = first_row(mi[:, 0:1]) + mi[:, 2:3]
    d1 = first_row(mi[:, 1:2]) + mi[:, 3:4]
    sh = SUBLANE.bit_length() - 1
    o_ref[...] = jnp.where(lane == 0, d0 >> sh, jnp.where(lane == 1, d0 & (SUBLANE - 1),
                           jnp.where(lane == 2, d1 >> sh, jnp.where(lane == 3, d1 & (SUBLANE - 1), 0))))


def _dest_rows(mi, starts_row):
    n = mi.shape[0]
    tm = min(1024, n)
    assert n % tm == 0
    return pl.pallas_call(
        _dest_kernel,
        out_shape=jax.ShapeDtypeStruct((n, LANE), i32),
        grid=(n // tm,),
        in_specs=[pl.BlockSpec((tm, LANE), lambda i: (i, 0)), pl.BlockSpec((1, LANE), lambda i: (0, 0))],
        out_specs=pl.BlockSpec((tm, LANE), lambda i: (i, 0)),
        compiler_params=_cparams(("arbitrary",)),
        name="moe_dest",
    )(mi, starts_row)


def _dispatch_kernel(mi_ref, hp_ref, hs_ref, xs_ref, sem, *, np_tiles):
    i = pl.program_id(0)

    def scatter_rows(h_ref):
        n_tiles = h_ref.shape[0]

        def start(t, c):
            for u in range(SUBLANE):
                rec = MI_W * (SUBLANE * t + u)
                for k in range(2):
                    dst = xs_ref.at[mi_ref[rec + 2 * k], pl.ds(mi_ref[rec + 2 * k + 1], 1)]
                    pltpu.make_async_copy(h_ref.at[t, pl.ds(u, 1)], dst, sem).start(priority=k)
            return c

        lax.fori_loop(0, n_tiles, start, 0)
        for k in range(2):
            pltpu.make_async_copy(h_ref, xs_ref.at[pl.ds(0, n_tiles)], sem).wait()

    @pl.when(i < np_tiles)
    def _():
        scatter_rows(hp_ref)

    @pl.when(i >= np_tiles)
    def _():
        scatter_rows(hs_ref)


def _dispatch(h2_p, h2_s, mi_flat):
    tm = TOKEN_TILE
    n_p, n_s = h2_p.shape[0], h2_s.shape[0]
    assert n_p % tm == 0 and n_s <= tm and n_s % SUBLANE == 0 and tm % SUBLANE == 0
    np_tiles = n_p // tm
    tiled = lambda a: a.reshape(a.shape[0] // SUBLANE, SUBLANE, D_MODEL)
    return pl.pallas_call(
        functools.partial(_dispatch_kernel, np_tiles=np_tiles),
        out_shape=jax.ShapeDtypeStruct((2 * (n_p + n_s) // SUBLANE, SUBLANE, D_MODEL), f32),
        grid=(np_tiles + 1,),
        in_specs=[pl.BlockSpec((MI_W * tm,), lambda i: (i,), memory_space=pltpu.SMEM),
                  pl.BlockSpec((tm // SUBLANE, SUBLANE, D_MODEL), lambda i: (jnp.minimum(i, np_tiles - 1), 0, 0)),
                  pl.BlockSpec((n_s // SUBLANE, SUBLANE, D_MODEL), lambda i: (0, 0, 0))],
        out_specs=pl.BlockSpec(memory_space=pl.ANY),
        scratch_shapes=[pltpu.SemaphoreType.DMA(())],
        compiler_params=_cparams(("arbitrary",)),
        name="moe_dispatch",
    )(mi_flat, tiled(h2_p), tiled(h2_s))


def _cast_rows(src_ref, dst_ref, col0=0, rows=256):
    width = src_ref.shape[1]

    def body(r, c):
        sl = pl.ds(pl.multiple_of(r * rows, rows), rows)
        dst_ref[sl, col0:col0 + width] = src_ref[sl, :].astype(bf16)
        return c
    lax.fori_loop(0, src_ref.shape[0] // rows, body, 0)


def _moe_kernel(blk_ref, lo_ref, hi_ref, first_ref, newe_ref, slot_ref, pre_ref, init_ref,
                x_ref, wg_hbm, wu_hbm, wd_hbm, o_ref,
                wg_f, wu_f, wd_f, wgu_b, wd_b, sem):
    i = pl.program_id(0)
    lo = lo_ref[i]
    hi = hi_ref[i]

    def weight_copies(e, slot):
        return [pltpu.make_async_copy(wg_hbm.at[e], wg_f.at[slot], sem.at[slot, 0]),
                pltpu.make_async_copy(wu_hbm.at[e], wu_f.at[slot], sem.at[slot, 1]),
                pltpu.make_async_copy(wd_hbm.at[e], wd_f.at[slot], sem.at[slot, 2])]

    def start_weights(e, slot):
        for cp, prio in zip(weight_copies(e, slot), (0, 1, 1)):
            cp.start(priority=prio)

    @pl.when(i == 0)
    def _():
        start_weights(init_ref[0], 0)
        for k in range(1, W_SLOTS):
            @pl.when(init_ref[k] >= 0)
            def _():
                start_weights(init_ref[k], k)

    @pl.when(newe_ref[i] == 1)
    def _():
        slot = slot_ref[i]
        cg, cu, cd = weight_copies(0, slot)
        cg.wait()
        _cast_rows(wg_f.at[slot], wgu_b, 0)
        cu.wait()
        _cast_rows(wu_f.at[slot], wgu_b, D_FF)
        cd.wait()
        _cast_rows(wd_f.at[slot], wd_b)

        @pl.when(pre_ref[i] >= 0)
        def _():
            start_weights(pre_ref[i], slot)

    @pl.when(hi > lo)
    def _():
        x = x_ref[...].astype(bf16)
        au = _dot(x, wgu_b[...])
        y = _dot((_silu(au[:, 0:D_FF]) * au[:, D_FF:2 * D_FF]).astype(bf16), wd_b[...])
        row = lax.broadcasted_iota(i32, y.shape, 0)
        ym = jnp.where((row >= lo) & (row < hi), y, 0.0)

        @pl.when(first_ref[i] == 1)
        def _():
            o_ref[...] = ym

        @pl.when(first_ref[i] == 0)
        def _():
            o_ref[...] = o_ref[...] + ym


def _moe(xs, w_gate, w_up, w_down, items):
    n_items = items[0].shape[0]
    rows = xs.shape[0]
    n_pref = len(items)
    xmap = lambda i, blk, *_: (blk[i], 0)
    grid_spec = pltpu.PrefetchScalarGridSpec(
        num_scalar_prefetch=n_pref,
        grid=(n_items,),
        in_specs=[pl.BlockSpec((MOE_ROWS, D_MODEL), xmap),
                  pl.BlockSpec(memory_space=pl.ANY),
                  pl.BlockSpec(memory_space=pl.ANY),
                  pl.BlockSpec(memory_space=pl.ANY)],
        out_specs=pl.BlockSpec((MOE_ROWS, D_MODEL), xmap),
        scratch_shapes=[pltpu.VMEM((W_SLOTS, D_MODEL, D_FF), f32), pltpu.VMEM((W_SLOTS, D_MODEL, D_FF), f32),
                        pltpu.VMEM((W_SLOTS, D_FF, D_MODEL), f32),
                        pltpu.VMEM((D_MODEL, 2 * D_FF), bf16), pltpu.VMEM((D_FF, D_MODEL), bf16),
                        pltpu.SemaphoreType.DMA((W_SLOTS, 3))],
    )
    return pl.pallas_call(
        _moe_kernel,
        out_shape=jax.ShapeDtypeStruct((rows, D_MODEL), f32),
        grid_spec=grid_spec,
        compiler_params=_cparams(("arbitrary",)),
        name="moe_experts",
    )(*items, xs, w_gate, w_up, w_down)


def _combine_kernel(mi_ref, mi_next_ref, x1p_ref, mwp_ref, x1s_ref, mws_ref, fnw_ref, ys_ref,
                    yp_ref, ysm_ref, g_ref, sem, *, np_tiles):
    i = pl.program_id(0)
    tiles_p = x1p_ref.shape[0] // SUBLANE
    tiles_s = x1s_ref.shape[0] // SUBLANE
    slot = lax.rem(i, 2)

    cg = COMBINE_CHUNK_TILES

    def issue_chunk(m_ref, dst_slot, c):
        for j in range(cg):
            t = c * cg + j
            for u in range(SUBLANE):
                rec = MI_W * (SUBLANE * t + u)
                for k in range(2):
                    src = ys_ref.at[m_ref[rec + 2 * k], pl.ds(m_ref[rec + 2 * k + 1], 1)]
                    pltpu.make_async_copy(src, g_ref.at[dst_slot, k, t, pl.ds(u, 1)],
                                          sem.at[dst_slot]).start(priority=k)

    def loop(lo, hi, fn):
        def body(c, carry):
            fn(c)
            return carry
        lax.fori_loop(lo, hi, body, 0)

    @pl.when(i == 0)
    def _():
        loop(0, tiles_p // cg, lambda c: issue_chunk(mi_ref, 0, c))

    def run(x1_ref, mw_ref, out_ref, n_cur, n_next):
        for k in range(2):
            pltpu.make_async_copy(ys_ref.at[pl.ds(0, n_cur)], g_ref.at[slot, k, pl.ds(0, n_cur)], sem.at[slot]).wait()

        def finish_chunk(c):
            nrow = cg * SUBLANE
            rows = pl.ds(pl.multiple_of(c * nrow, nrow), nrow)
            tiles = pl.ds(c * cg, cg)
            mw = mw_ref[rows, :]
            g0 = g_ref[slot, 0, tiles].reshape(nrow, D_MODEL)
            g1 = g_ref[slot, 1, tiles].reshape(nrow, D_MODEL)
            x2 = x1_ref[rows, :] + (g0 * mw[:, 0:1] + g1 * mw[:, 1:2])
            ms = jnp.mean(x2 * x2, axis=-1, keepdims=True)
            out_ref[rows, :] = x2 * lax.rsqrt(ms + EPS) * fnw_ref[...]

        def both(c):
            issue_chunk(mi_next_ref, 1 - slot, c)
            finish_chunk(c)

        c_cur, c_next = n_cur // cg, n_next // cg
        c_both = min(c_cur, c_next)
        loop(0, c_both, both)
        if c_cur > c_both:
            loop(c_both, c_cur, finish_chunk)
        if c_next > c_both:
            loop(c_both, c_next, lambda c: issue_chunk(mi_next_ref, 1 - slot, c))

    @pl.when(i + 1 < np_tiles)
    def _():
        run(x1p_ref, mwp_ref, yp_ref, tiles_p, tiles_p)

    @pl.when(i + 1 == np_tiles)
    def _():
        run(x1p_ref, mwp_ref, yp_ref, tiles_p, tiles_s)

    @pl.when(i >= np_tiles)
    def _():
        run(x1s_ref, mws_ref, ysm_ref, tiles_s, 0)


def _combine(x1_p, mw_p, x1_s, mw_s, fnw_row, ys3, mi_flat):
    tm = TOKEN_TILE
    n_p, n_s = x1_p.shape[0], x1_s.shape[0]
    chunk_rows = COMBINE_CHUNK_TILES * SUBLANE
    assert n_p % tm == 0 and n_s <= tm and n_s % chunk_rows == 0 and tm % chunk_rows == 0
    np_tiles = n_p // tm
    ptile = lambda width: pl.BlockSpec((tm, width), lambda i: (jnp.minimum(i, np_tiles - 1), 0))
    stile = lambda width: pl.BlockSpec((n_s, width), lambda i: (0, 0))
    return pl.pallas_call(
        functools.partial(_combine_kernel, np_tiles=np_tiles),
        out_shape=(jax.ShapeDtypeStruct((n_p, D_MODEL), f32),
                   jax.ShapeDtypeStruct((n_s, D_MODEL), f32)),
        grid=(np_tiles + 1,),
        in_specs=[pl.BlockSpec((MI_W * tm,), lambda i: (i,), memory_space=pltpu.SMEM),
                  pl.BlockSpec((MI_W * tm,), lambda i: (jnp.minimum(i + 1, np_tiles),), memory_space=pltpu.SMEM),
                  ptile(D_MODEL), ptile(LANE), stile(D_MODEL), stile(LANE),
                  pl.BlockSpec((1, D_MODEL), lambda i: (0, 0)),
                  pl.BlockSpec(memory_space=pl.ANY)],
        out_specs=(ptile(D_MODEL), stile(D_MODEL)),
        scratch_shapes=[pltpu.VMEM((2, 2, tm // SUBLANE, SUBLANE, D_MODEL), f32), pltpu.SemaphoreType.DMA((2,))],
        compiler_params=_cparams(("arbitrary",)),
        name="moe_combine",
    )(mi_flat, mi_flat, x1_p, mw_p, x1_s, mw_s, fnw_row, ys3)


PLAN_ROWS = 256
N_ITEM_FIELDS = 7


def _plan_kernel(cnt_ref, items_ref, rows_ref, *, nblk):
    cnt = cnt_ref[...]
    lane1 = lax.broadcasted_iota(i32, (1, LANE), 1)
    in_e = lane1 < N_EXPERTS
    ri = lax.broadcasted_iota(i32, (LANE, LANE), 0)
    ci = lax.broadcasted_iota(i32, (LANE, LANE), 1)
    upper = jnp.where(ri <= ci, 1.0, 0.0).astype(bf16)

    def cumsum_lanes(v):
        return _dot_lsplit(jnp.broadcast_to(v, (8, LANE)), upper)[0:1, :]

    shift = MOE_ROWS.bit_length() - 1
    ends = cumsum_lanes(cnt)
    starts = ends - cnt
    act = cnt > 0.0
    first_blk = (starts.astype(i32) >> shift).astype(f32)
    last_blk = (jnp.maximum(ends - 1.0, 0.0).astype(i32) >> shift).astype(f32)
    nvis = jnp.where(act, last_blk - first_blk + 1.0, 0.0)
    vis_end = cumsum_lanes(nvis)
    vis_start = vis_end - nvis
    total = jnp.max(vis_end, axis=-1, keepdims=True)
    cum_act = cumsum_lanes(jnp.where(act, 1.0, 0.0))
    n_uniq = jnp.max(cum_act, axis=-1, keepdims=True)

    p = PLAN_ROWS
    lane = lax.broadcasted_iota(i32, (p, LANE), 1)
    idx = lax.broadcasted_iota(i32, (p, LANE), 0).astype(f32)
    idx1 = idx[:, 0:1]
    count_le = lambda row, col: jnp.sum(jnp.where((row <= col) & in_e, 1.0, 0.0), axis=-1, keepdims=True)
    e = jnp.minimum(count_le(vis_end, idx), N_EXPERTS - 1.0)
    onehot = lane.astype(f32) == e
    look = lambda tbl: jnp.sum(jnp.where(onehot, tbl, 0.0), axis=-1, keepdims=True)
    blk = look(first_blk) + idx1 - look(vis_start)
    lo = jnp.maximum(look(starts), blk * MOE_ROWS) - blk * MOE_ROWS
    hi = jnp.minimum(look(ends), (blk + 1.0) * MOE_ROWS) - blk * MOE_ROWS
    valid = idx1 < total
    blk = jnp.where(valid, blk, nblk - 1.0)
    lo = jnp.where(valid, lo, 0.0)
    hi = jnp.where(valid, hi, 0.0)
    rep = lambda c: jnp.broadcast_to(c, (p, LANE))
    prev = lambda c: pltpu.roll(rep(c), 1, axis=0)[:, 0:1]
    is0 = idx1 == 0.0
    first = valid & (is0 | (blk != prev(blk)))
    newe = valid & (is0 | (e != prev(e)))
    rp = lax.broadcasted_iota(i32, (p, p), 0)
    cp = lax.broadcasted_iota(i32, (p, p), 1)
    lower = jnp.where(rp >= cp, 1.0, 0.0).astype(bf16)
    order = _dot(lower, rep(jnp.where(newe, 1.0, 0.0)).astype(bf16))[:, 0:1] - 1.0
    slot = jnp.where(newe, order - W_SLOTS * jnp.floor((order + 0.5) * (1.0 / W_SLOTS)), 0.0)
    k2 = order + float(W_SLOTS)
    pre = jnp.where(newe & (k2 < n_uniq), count_le(cum_act, rep(k2)), -1.0)
    out = jnp.zeros((p, LANE), f32)
    for c, v in enumerate([blk, lo, hi, jnp.where(first, 1.0, 0.0), jnp.where(newe, 1.0, 0.0), slot, pre]):
        out = jnp.where(lane == c, v, out)
    items_ref[...] = out.astype(i32)

    init_row = jnp.zeros((1, LANE), f32)
    for k in range(W_SLOTS):
        init_row = jnp.where(lane1 == k, jnp.where(n_uniq > float(k), count_le(cum_act, float(k)), -1.0), init_row)
    rows_ref[...] = jnp.zeros(rows_ref.shape, f32)
    rows_ref[0:1, :] = starts
    rows_ref[1:2, :] = init_row


def _work_items(cnt_row, n_rows):
    nblk = n_rows // MOE_ROWS
    n_items = nblk + N_EXPERTS - 1
    assert n_items <= PLAN_ROWS and n_rows % MOE_ROWS == 0
    items, rows = pl.pallas_call(
        functools.partial(_plan_kernel, nblk=nblk),
        out_shape=(jax.ShapeDtypeStruct((PLAN_ROWS, LANE), i32), jax.ShapeDtypeStruct((8, LANE), f32)),
        compiler_params=pltpu.CompilerParams(vmem_limit_bytes=VMEM_LIMIT),
        name="moe_plan",
    )(cnt_row)
    fields = tuple(items[0:n_items, c] for c in range(N_ITEM_FIELDS))
    return rows[0:1, :], fields + (rows[1, 0:W_SLOTS].astype(i32),)


def kernel(x_prompt, x_sample, state_delta, state_qkv_conv, state_short_conv, norm1_w, w_in, conv_a_w, a_log, dt_bias, out_norm_w, w_branch_a, conv_b_w, w_branch_b, w_o, norm2_w, router_group_w, router_group_b, router_expert_w, router_expert_b, w_gate, w_up, w_down, final_norm_w):
    assert norm1_w.shape[0] == 1, "single-layer trunk"
    bp, tp, d = x_prompt.shape
    bs, ts, _ = x_sample.shape
    assert d == D_MODEL and ts == 1
    n_p = bp * tp
    n_s = bs
    n_all = n_p + n_s

    w_perm = _wprep(jnp.transpose(w_in[0]))
    wa = w_branch_a[0].astype(bf16)
    wb = w_branch_b[0].astype(bf16)
    wo = w_o[0].astype(bf16)
    pad = lambda v: jnp.zeros((1, BA_W), f32).at[0, DECAY_LANE:DECAY_LANE + N_HEADS].set(v)
    alog_row = pad(a_log[0])
    dtb_row = pad(dt_bias[0])
    onw_row = out_norm_w[0].reshape(1, HEAD)
    cwa = conv_a_w[0]
    cwb = conv_b_w[0]
    r_pad = LANE - N_EXPERTS - N_GROUPS
    rw = jnp.concatenate([router_expert_w[0], router_group_w[0], jnp.zeros((D_MODEL, r_pad), f32)], axis=1)
    rwh = rw.astype(bf16)
    rwl = (rw - rwh.astype(f32)).astype(bf16)
    rb_row = jnp.concatenate([router_expert_b[0], router_group_b[0], jnp.zeros((r_pad,), f32)]).reshape(1, LANE)
    n2_row = norm2_w[0].reshape(1, D_MODEL)

    xp2 = x_prompt.reshape(n_p, D_MODEL)
    proj_p, tails = _inproj_conv(xp2, norm1_w[0], w_perm, cwa, tp)
    tiles_per_seq = tails.shape[0] // bp
    nca_p = tails.reshape(bp, tiles_per_seq, 8, CONV_PAD_W)[:, -1, 8 - (CONV_A - 1):8, 0:QKV_W]
    o_p, y_p, sd_p, ncb_p = _delta_prompt(proj_p.reshape(bp, tp, PROJ_W), cwb, alog_row, dtb_row,
                                          onw_row, nb_step=4 if bp % 4 == 0 else (2 if bp % 2 == 0 else 1))
    cnt0 = jnp.zeros((1, LANE), f32)
    x1_p, h2_p, mi_p, mw_p, cnt_p = _mix_route(xp2, o_p.reshape(n_p, QK_W), y_p.reshape(n_p, SC_W), proj_p,
                                               wa, wb, wo, n2_row, rwh, rwl, rb_row, cnt0)

    xs2 = x_sample.reshape(n_s, D_MODEL)
    proj_s = _inproj(xs2, norm1_w[0], w_perm)
    bufa_t = jnp.transpose(state_qkv_conv[0], (1, 0, 2))
    bufb_t = jnp.transpose(state_short_conv[0], (1, 0, 2))
    q_s, k_s, v_s, beta_s, eg_s, z_s, y_s, nbufa_t, nbufb_t = _sample_prep(proj_s, bufa_t, bufb_t, cwa, cwb,
                                                                           alog_row, dtb_row)
    sd_s, o_s = _sample_step(state_delta[0], q_s, k_s, v_s, beta_s, eg_s, z_s, onw_row)
    o_s2 = o_s.reshape(n_s, QK_W).astype(bf16)
    x1_s, h2_s, mi_s, mw_s, cnt = _mix_route(xs2, o_s2, y_s, proj_s, wa, wb, wo, n2_row, rwh, rwl, rb_row, cnt_p)

    starts_row, items = _work_items(cnt, 2 * n_all)
    mi_flat = jnp.concatenate([_dest_rows(mi_p, starts_row)[:, 0:MI_W], _dest_rows(mi_s, starts_row)[:, 0:MI_W]],
                              axis=0).reshape(MI_W * n_all)
    xs_sorted = _dispatch(h2_p, h2_s, mi_flat)
    ys = _moe(xs_sorted.reshape(2 * n_all, D_MODEL), w_gate[0], w_up[0], w_down[0], items)
    y_prompt, y_sample = _combine(x1_p, mw_p, x1_s, mw_s, final_norm_w.reshape(1, D_MODEL),
                                  ys.reshape(2 * n_all // SUBLANE, SUBLANE, D_MODEL), mi_flat)

    return (y_prompt.reshape(bp, tp, D_MODEL),
            y_sample.reshape(bs, ts, D_MODEL),
            sd_p[None],
            nca_p[None],
            ncb_p[None],
            sd_s[None],
            jnp.transpose(nbufa_t, (1, 0, 2))[None],
            jnp.transpose(nbufb_t, (1, 0, 2))[None])
```

```python
import functools

import jax
import jax.numpy as jnp
from jax import lax
from jax.experimental import pallas as pl
from jax.experimental.pallas import tpu as pltpu

f32 = jnp.float32
bf16 = jnp.bfloat16
i32 = jnp.int32

EPS = 1e-6
LANE = 128
D_MODEL = 2048
N_HEADS = 8
HEAD = 128
QK_W = N_HEADS * HEAD
QKV_W = 3 * QK_W
SC_W = 1024
CONV_A = 4
CONV_B = 3
CHUNK = 64
CHUNK_SHIFT = CHUNK.bit_length() - 1
HEAD_SHIFT = HEAD.bit_length() - 1
GROUP_HEADS = 4
DECAY_LANE = N_HEADS
N_EXPERTS = 64
N_GROUPS = 8
EXPERTS_PER_GROUP = 8
GROUP_SHIFT = EXPERTS_PER_GROUP.bit_length() - 1
D_FF = 512
MOE_ROWS = 128
TOKEN_TILE = 512
W_SLOTS = 2

COL_QKV = 0
COL_BCX = 3072
COL_GA = 6144
COL_GB = 8192
COL_Z = 10240
COL_BA = 11264
BA_W = 256
PROJ_W = 11520
PROJ_TN = 1280

VMEM_LIMIT = 56 * 1024 * 1024


def _dot(a, b):
    return jnp.dot(a, b, preferred_element_type=f32)


def _dot_nt(a, b):
    return lax.dot_general(a, b, (((1,), (1,)), ((), ())), preferred_element_type=f32)


def _split(x, n):
    parts = []
    r = x
    for i in range(n):
        p = r.astype(bf16)
        parts.append(p)
        if i + 1 < n:
            r = r - p.astype(f32)
    return parts


def _dot_lsplit(x, m, n=3):
    rows = x.shape[0]
    d = _dot(jnp.concatenate(_split(x, n), axis=0), m)
    acc = d[0:rows]
    for i in range(1, n):
        acc = acc + d[i * rows:(i + 1) * rows]
    return acc


def _dot_rsplit(m, x, n=3):
    cols = x.shape[1]
    d = _dot(m, jnp.concatenate(_split(x, n), axis=1))
    acc = d[:, 0:cols]
    for i in range(1, n):
        acc = acc + d[:, i * cols:(i + 1) * cols]
    return acc


_sigmoid = jax.nn.sigmoid


def _silu(x):
    return x * _sigmoid(x)


def _softplus(x):
    return jnp.maximum(x, 0.0) + jnp.log(1.0 + jnp.exp(-jnp.abs(x)))


def _cparams(sem):
    return pltpu.CompilerParams(dimension_semantics=sem, vmem_limit_bytes=VMEM_LIMIT)


def _inproj_kernel(x_ref, nw_ref, w_ref, o_ref, h_ref, *, rows):
    @pl.when(pl.program_id(1) == 0)
    def _():
        def body(r, c):
            sl = pl.ds(pl.multiple_of(r * rows, rows), rows)
            x = x_ref[sl, :]
            ms = jnp.mean(x * x, axis=-1, keepdims=True)
            h_ref[sl, :] = (x * lax.rsqrt(ms + EPS) * nw_ref[...]).astype(bf16)
            return c
        lax.fori_loop(0, x_ref.shape[0] // rows, body, 0)

    o_ref[...] = _dot_nt(h_ref[...], w_ref[...])


def _inproj(x2d, norm_w, w_bf16):
    n = x2d.shape[0]
    tm = min(1024, n)
    assert n % tm == 0 and PROJ_W % PROJ_TN == 0
    return pl.pallas_call(
        functools.partial(_inproj_kernel, rows=min(128, tm)),
        out_shape=jax.ShapeDtypeStruct((n, PROJ_W), f32),
        grid=(n // tm, PROJ_W // PROJ_TN),
        in_specs=[pl.BlockSpec((tm, D_MODEL), lambda i, j: (i, 0)),
                  pl.BlockSpec((1, D_MODEL), lambda i, j: (0, 0)),
                  pl.BlockSpec((PROJ_TN, D_MODEL), lambda i, j: (j, 0))],
        out_specs=pl.BlockSpec((tm, PROJ_TN), lambda i, j: (i, j)),
        scratch_shapes=[pltpu.VMEM((tm, D_MODEL), bf16)],
        compiler_params=_cparams(("arbitrary", "arbitrary")),
        name="inproj",
    )(x2d, norm_w.reshape(1, D_MODEL), w_bf16)


CONV_TILES = 3
CONV_COLS = 2 * HEAD
CONV_ROWS = 128
CONV_PAD_W = CONV_TILES * PROJ_TN


def _qkv_kind(col):
    return "q" if col < QK_W else "k" if col < 2 * QK_W else "v" if col < QKV_W else "raw"


def _inproj_conv_kernel(x_ref, nw_ref, w_ref, cw_ref, o_ref, tail_ref, h_ref, hist_ref, raw_ref, *,
                        rows, tiles_per_seq):
    i = pl.program_id(0)
    j = pl.program_id(1)
    tm = x_ref.shape[0]

    @pl.when(j == 0)
    def _():
        def body(r, c):
            sl = pl.ds(pl.multiple_of(r * rows, rows), rows)
            x = x_ref[sl, :]
            ms = jnp.mean(x * x, axis=-1, keepdims=True)
            h_ref[sl, :] = (x * lax.rsqrt(ms + EPS) * nw_ref[...]).astype(bf16)
            return c
        lax.fori_loop(0, tm // rows, body, 0)

    @pl.when((i == 0) & (j == 0))
    def _():
        hist_ref[...] = jnp.zeros(hist_ref.shape, f32)

    @pl.when(j >= CONV_TILES)
    def _():
        o_ref[...] = _dot_nt(h_ref[...], w_ref[...])

    seq_start = lax.rem(i, tiles_per_seq) == 0
    for jj in range(CONV_TILES):
        @pl.when(j == jj)
        def _():
            def matmul_chunk(idx, c0):
                raw_ref[idx % 2] = _dot_nt(h_ref[...], w_ref[c0:c0 + CONV_COLS, :])

            def conv_chunk(idx, c0):
                cs = slice(c0, c0 + CONV_COLS)
                raw = raw_ref.at[idx % 2]
                tail = raw[tm - 8:tm, :]
                tail_ref[0, :, cs] = tail
                kinds = [_qkv_kind(jj * PROJ_TN + c0 + g * HEAD) for g in range(CONV_COLS // HEAD)]
                if kinds[0] == "raw":
                    o_ref[:, cs] = raw[...]
                    return
                hist = jnp.where(seq_start, 0.0, hist_ref[jj, :, cs])
                for rc in range(tm // CONV_ROWS):
                    r0 = rc * CONV_ROWS
                    if rc > 0:
                        xe = raw[r0 - 8:r0 + CONV_ROWS, :]
                    else:
                        xe = jnp.concatenate([hist, raw[0:CONV_ROWS, :]], axis=0)
                    acc = pltpu.roll(xe, 3, axis=0)[8:] * cw_ref[0:1, cs]
                    acc = acc + pltpu.roll(xe, 2, axis=0)[8:] * cw_ref[1:2, cs]
                    acc = acc + pltpu.roll(xe, 1, axis=0)[8:] * cw_ref[2:3, cs]
                    acc = acc + xe[8:] * cw_ref[3:4, cs]
                    act = _silu(acc)
                    for g, kind in enumerate(kinds):
                        ah = act[:, g * HEAD:(g + 1) * HEAD]
                        if kind != "v":
                            ss = jnp.sum(ah * ah, axis=-1, keepdims=True)
                            inv = lax.rsqrt(ss + EPS)
                            ah = ah * (inv * (HEAD ** -0.5) if kind == "q" else inv)
                        o_ref[r0:r0 + CONV_ROWS, c0 + g * HEAD:c0 + (g + 1) * HEAD] = ah
                hist_ref[jj, :, cs] = tail

            chunks = list(range(0, PROJ_TN, CONV_COLS))
            matmul_chunk(0, chunks[0])
            for idx in range(1, len(chunks)):
                matmul_chunk(idx, chunks[idx])
                conv_chunk(idx - 1, chunks[idx - 1])
            conv_chunk(len(chunks) - 1, chunks[-1])


def _inproj_conv(x2d, norm_w, w_bf16, cwa, seq_len):
    n = x2d.shape[0]
    tm = min(1024, seq_len)
    assert n % tm == 0 and seq_len % tm == 0 and PROJ_W % PROJ_TN == 0 and tm % CONV_ROWS == 0
    assert QKV_W % CONV_COLS == 0 and PROJ_TN % CONV_COLS == 0
    cw_pad = jnp.zeros((CONV_A, CONV_PAD_W), f32).at[:, 0:QKV_W].set(cwa)
    last = CONV_TILES - 1
    return pl.pallas_call(
        functools.partial(_inproj_conv_kernel, rows=min(128, tm), tiles_per_seq=seq_len // tm),
        out_shape=(jax.ShapeDtypeStruct((n, PROJ_W), f32),
                   jax.ShapeDtypeStruct((n // tm, 8, CONV_PAD_W), f32)),
        grid=(n // tm, PROJ_W // PROJ_TN),
        in_specs=[pl.BlockSpec((tm, D_MODEL), lambda i, j: (i, 0)),
                  pl.BlockSpec((1, D_MODEL), lambda i, j: (0, 0)),
                  pl.BlockSpec((PROJ_TN, D_MODEL), lambda i, j: (j, 0)),
                  pl.BlockSpec((CONV_A, PROJ_TN), lambda i, j: (0, jnp.minimum(j, last)))],
        out_specs=(pl.BlockSpec((tm, PROJ_TN), lambda i, j: (i, j)),
                   pl.BlockSpec((1, 8, PROJ_TN), lambda i, j: (i, 0, jnp.minimum(j, last)))),
        scratch_shapes=[pltpu.VMEM((tm, D_MODEL), bf16), pltpu.VMEM((CONV_TILES, 8, PROJ_TN), f32),
                        pltpu.VMEM((2, tm, CONV_COLS), f32)],
        compiler_params=_cparams(("arbitrary", "arbitrary")),
        name="inproj_conv",
    )(x2d, norm_w.reshape(1, D_MODEL), w_bf16, cw_pad)


W_IN_COLS = 11280
WPREP_TN = 1024
WPREP_SHIFT = 16
WP_BCX = COL_BCX // WPREP_TN
WP_Z = COL_Z // WPREP_TN
WP_BA = COL_BA // WPREP_TN
WP_SRC_Z = QKV_W // WPREP_TN
WP_SRC_BA = WP_SRC_Z + 1


def _wprep_kernel(a_ref, b_ref, o_ref):
    j = pl.program_id(0)
    keep = WPREP_TN - WPREP_SHIFT

    @pl.when((j < WP_BCX) | (j == WP_Z))
    def _():
        o_ref[...] = a_ref[...].astype(bf16)

    @pl.when((j >= WP_BCX) & (j < WP_Z))
    def _():
        o_ref[0:keep, :] = a_ref[WPREP_SHIFT:WPREP_TN, :].astype(bf16)
        o_ref[keep:WPREP_TN, :] = b_ref[...].astype(bf16)

    @pl.when(j == WP_BA)
    def _():
        o_ref[0:WPREP_SHIFT, :] = a_ref[0:WPREP_SHIFT, :].astype(bf16)
        o_ref[WPREP_SHIFT:WPREP_TN, :] = jnp.zeros((keep, D_MODEL), bf16)


def _wprep(w_in_t):
    assert w_in_t.shape == (W_IN_COLS, D_MODEL) and 2 * N_HEADS == WPREP_SHIFT
    n_blk = pl.cdiv(PROJ_W, WPREP_TN)

    def a_map(j):
        return (jnp.where(j < WP_BCX, j, jnp.where(j < WP_Z, j + 1, jnp.where(j == WP_Z, WP_SRC_Z, WP_SRC_BA))), 0)

    def b_map(j):
        return (jnp.minimum((WPREP_TN // WPREP_SHIFT) * (j + 2), W_IN_COLS // WPREP_SHIFT - 1), 0)

    return pl.pallas_call(
        _wprep_kernel,
        out_shape=jax.ShapeDtypeStruct((PROJ_W, D_MODEL), bf16),
        grid=(n_blk,),
        in_specs=[pl.BlockSpec((WPREP_TN, D_MODEL), a_map),
                  pl.BlockSpec((WPREP_SHIFT, D_MODEL), b_map)],
        out_specs=pl.BlockSpec((WPREP_TN, D_MODEL), lambda j: (j, 0)),
        compiler_params=_cparams(("arbitrary",)),
        name="wprep",
    )(w_in_t, w_in_t)


def _head_l2norm(a, scale):
    outs = []
    for h in range(N_HEADS):
        ah = a[:, h * HEAD:(h + 1) * HEAD]
        ss = jnp.sum(ah * ah, axis=-1, keepdims=True)
        n = ah * lax.rsqrt(ss + EPS)
        outs.append(n * scale if scale != 1.0 else n)
    return outs


def _delta_prompt_kernel(qkv_ref, bcx_ref, z_ref, ba_ref, cwb_ref, alog_ref, dtb_ref, onw_ref,
                         e64_ref,
                         o_ref, y_ref, snew_ref, ncb_ref,
                         s_ref, xb_ref, *, nb_step):
    C = CHUNK
    G = GROUP_HEADS
    R = G * C
    t = pl.program_id(1)
    nt = pl.num_programs(1)

    @pl.when(t == 0)
    def _():
        s_ref[...] = jnp.zeros(s_ref.shape, f32)
        xb_ref[:, 0:8, :] = jnp.zeros((nb_step, 8, SC_W), f32)

    rr = lax.broadcasted_iota(i32, (R, R), 0)
    cc = lax.broadcasted_iota(i32, (R, R), 1)
    same_bf = jnp.where((rr >> CHUNK_SHIFT) == (cc >> CHUNK_SHIFT), 1.0, 0.0).astype(bf16)
    r2 = lax.broadcasted_iota(i32, (R, G * HEAD), 0)
    c2 = lax.broadcasted_iota(i32, (R, G * HEAD), 1)
    bdmask = (r2 >> CHUNK_SHIFT) == (c2 >> HEAD_SHIFT)
    r3 = lax.broadcasted_iota(i32, (C, C), 0)
    c3 = lax.broadcasted_iota(i32, (C, C), 1)
    ltri = jnp.where(r3 >= c3, 1.0, 0.0).astype(bf16)
    r4 = lax.broadcasted_iota(i32, (C, R), 0)
    c4 = lax.broadcasted_iota(i32, (C, R), 1)
    ident_t = r4 == (c4 & (C - 1))
    incl_p = r4 >= (c4 & (C - 1))
    strict_p = r4 > (c4 & (C - 1))
    hblk = c4 >> CHUNK_SHIFT
    ones8 = jnp.ones((8, C), bf16)

    nbs = range(nb_step)
    units = [(nb, g) for nb in nbs for g in range(N_HEADS // G)]
    heads = lambda g: range(g * G, (g + 1) * G)

    qn = [[qkv_ref[nb, :, h * HEAD:(h + 1) * HEAD] for h in range(N_HEADS)] for nb in nbs]
    kn = [[qkv_ref[nb, :, QK_W + h * HEAD:QK_W + (h + 1) * HEAD] for h in range(N_HEADS)] for nb in nbs]
    vv = [qkv_ref[nb, :, 2 * QK_W:3 * QK_W] for nb in nbs]

    bts = [ba_ref[nb, :, 0:LANE] for nb in nbs]
    beta_all = [_sigmoid(bt) for bt in bts]
    g_all = [-(jnp.exp(alog_ref[:, 0:LANE]) * _softplus(bt + dtb_ref[:, 0:LANE])) for bt in bts]
    gc_small = [_dot_rsplit(ltri, ga) for ga in g_all]
    gl_small = [gc[C - 1:C, :] for gc in gc_small]

    k_st, q_st, kb, vb, kbg, qd, kd, gc_col = ({} for _ in range(8))
    for u in units:
        nb, g = u
        hs = heads(g)
        k_st[u] = jnp.concatenate([kn[nb][h] for h in hs], axis=0)
        q_st[u] = jnp.concatenate([qn[nb][h] for h in hs], axis=0)
        v_st = jnp.concatenate([vv[nb][:, h * HEAD:(h + 1) * HEAD] for h in hs], axis=0)
        beta_col = jnp.concatenate([beta_all[nb][:, h:h + 1] for h in hs], axis=0)
        gc_col[u] = jnp.concatenate([gc_small[nb][:, DECAY_LANE + h:DECAY_LANE + h + 1] for h in hs], axis=0)
        gl_col = jnp.concatenate(
            [jnp.broadcast_to(gl_small[nb][:, DECAY_LANE + h:DECAY_LANE + h + 1], (C, 1)) for h in hs], axis=0)
        kb[u] = k_st[u] * beta_col
        vb[u] = v_st * beta_col
        egc = jnp.exp(gc_col[u])
        kbg[u] = kb[u] * egc
        qd[u] = q_st[u] * egc
        kd[u] = k_st[u] * jnp.exp(gl_col - gc_col[u])

    gx = {u: _dot_lsplit(gc_small[u[0]], e64_ref[u[1], 0:LANE, :]) for u in units}
    crow = {u: _dot_rsplit(ones8, jnp.where(ident_t, gx[u], 0.0))[0:1, :] for u in units}
    a = {u: _dot_nt(jnp.concatenate([kb[u], q_st[u]], axis=0).astype(bf16), k_st[u].astype(bf16))
         for u in units}
    in_blk = [hblk == h for h in range(G - 1)]

    def pack(x):
        out = x[(G - 1) * C:G * C]
        for h in reversed(range(G - 1)):
            out = jnp.where(in_blk[h], x[h * C:(h + 1) * C], out)
        return out

    def expand(xp):
        return jnp.concatenate([xp.astype(bf16)] * G, axis=0) * same_bf

    dec = {u: jnp.where(incl_p, jnp.exp(jnp.where(incl_p, gx[u] - crow[u], 0.0)), 0.0) for u in units}
    nm = {u: jnp.where(strict_p, -(pack(a[u][0:R]) * dec[u]), 0.0) for u in units}
    qkm = {u: expand(pack(a[u][R:2 * R]) * dec[u]) for u in units}

    p = {u: jnp.where(ident_t, 1.0, 0.0) + nm[u] for u in units}
    nk = {u: _dot(nm[u].astype(bf16), expand(nm[u])) for u in units}
    for _ in range(4):
        for u in units:
            x = _dot(jnp.concatenate([p[u], nk[u]], axis=0).astype(bf16), expand(nk[u]))
            p[u] = p[u] + x[0:C]
            nk[u] = x[C:2 * C]
    for u in units:
        p[u] = p[u] + _dot(p[u].astype(bf16), expand(nk[u]))
    uw = {u: _dot(expand(p[u]), jnp.concatenate([vb[u], kbg[u]], axis=1).astype(bf16)) for u in units}

    ws = {}
    for u in units:
        nb, g = u
        for j, h in enumerate(heads(g)):
            sh = s_ref[nb, :, h * HEAD:(h + 1) * HEAD]
            lhs = jnp.concatenate([uw[u][j * C:(j + 1) * C, HEAD:2 * HEAD], qd[u][j * C:(j + 1) * C]], axis=0)
            ws[u, j] = _dot(lhs.astype(bf16), sh.astype(bf16))
    o_heads = {}
    for u in units:
        nb, g = u
        vnew_st = jnp.concatenate([uw[u][j * C:(j + 1) * C, 0:HEAD] - ws[u, j][0:C] for j in range(G)], axis=0)
        o_st = (jnp.concatenate([ws[u, j][C:2 * C] for j in range(G)], axis=0)
                + _dot(qkm[u], vnew_st.astype(bf16)))
        vbd = jnp.where(bdmask, jnp.concatenate([vnew_st] * G, axis=1), 0.0)
        lo = g * G * HEAD
        hi = lo + G * HEAD
        gl_row = jnp.concatenate(
            [jnp.broadcast_to(jnp.exp(gl_small[nb][:, DECAY_LANE + h:DECAY_LANE + h + 1]), (1, HEAD)) for h in heads(g)], axis=1)
        s_ref[nb, :, lo:hi] = s_ref[nb, :, lo:hi] * gl_row + _dot(kd[u].T.astype(bf16), vbd.astype(bf16))
        for j, h in enumerate(heads(g)):
            o_heads[nb, h] = o_st[j * C:(j + 1) * C]

    for nb in nbs:
        zt = z_ref[nb]
        for h in range(N_HEADS):
            oh = o_heads[nb, h]
            ms = jnp.mean(oh * oh, axis=-1, keepdims=True)
            zh = zt[:, h * HEAD:(h + 1) * HEAD]
            on = oh * lax.rsqrt(ms + EPS) * onw_ref[...] * _silu(zh)
            o_ref[nb, :, h * HEAD:(h + 1) * HEAD] = on.astype(bf16)

    for nb in nbs:
        bcx = bcx_ref[nb]
        cx = bcx[:, SC_W:2 * SC_W] * bcx[:, 2 * SC_W:3 * SC_W]
        xb_ref[nb, 8:8 + C, :] = cx
        ce = xb_ref[nb]
        cv = pltpu.roll(ce, 2, axis=0)[8:8 + C] * cwb_ref[0:1, :]
        cv = cv + pltpu.roll(ce, 1, axis=0)[8:8 + C] * cwb_ref[1:2, :]
        cv = cv + cx * cwb_ref[2:3, :]
        y_ref[nb] = (bcx[:, 0:SC_W] * cv).astype(bf16)
        xb_ref[nb, 0:8, :] = xb_ref[nb, C:C + 8, :]

    @pl.when(t == nt - 1)
    def _():
        for nb in range(nb_step):
            for h in range(N_HEADS):
                snew_ref[nb, h] = s_ref[nb, :, h * HEAD:(h + 1) * HEAD]
            ncb_ref[nb] = xb_ref[nb, 6:8, :]


def _expand_consts():
    lane = jnp.arange(BA_W)[:, None]
    col = jnp.arange(QK_W)[None, :]
    eb = (lane == (col >> HEAD_SHIFT)).astype(bf16)
    eg = (lane == (DECAY_LANE + (col >> HEAD_SHIFT))).astype(bf16)
    col64 = jnp.arange(GROUP_HEADS * CHUNK)[None, :]
    e64 = jnp.stack([(lane == (DECAY_LANE + g * GROUP_HEADS + (col64 >> CHUNK_SHIFT))).astype(bf16)
                     for g in range(N_HEADS // GROUP_HEADS)], axis=0)
    return eb, eg, e64


def _delta_prompt(proj3, cwb, alog_row, dtb_row, onw_row, nb_step):
    b, t, _ = proj3.shape
    assert t % CHUNK == 0 and b % nb_step == 0
    _, _, e64 = _expand_consts()
    c = CHUNK
    const2 = lambda bi, ti: (0, 0)
    outs = pl.pallas_call(
        functools.partial(_delta_prompt_kernel, nb_step=nb_step),
        out_shape=(jax.ShapeDtypeStruct((b, t, QK_W), bf16),
                   jax.ShapeDtypeStruct((b, t, SC_W), bf16),
                   jax.ShapeDtypeStruct((b, N_HEADS, HEAD, HEAD), f32),
                   jax.ShapeDtypeStruct((b, CONV_B - 1, SC_W), f32)),
        grid=(b // nb_step, t // c),
        in_specs=[pl.BlockSpec((nb_step, c, QKV_W), lambda bi, ti: (bi, ti, COL_QKV // QKV_W)),
                  pl.BlockSpec((nb_step, c, QKV_W), lambda bi, ti: (bi, ti, COL_BCX // QKV_W)),
                  pl.BlockSpec((nb_step, c, QK_W), lambda bi, ti: (bi, ti, COL_Z // QK_W)),
                  pl.BlockSpec((nb_step, c, BA_W), lambda bi, ti: (bi, ti, COL_BA // BA_W)),
                  pl.BlockSpec((CONV_B, SC_W), const2),
                  pl.BlockSpec((1, BA_W), const2),
                  pl.BlockSpec((1, BA_W), const2),
                  pl.BlockSpec((1, HEAD), const2),
                  pl.BlockSpec((N_HEADS // GROUP_HEADS, BA_W, GROUP_HEADS * CHUNK), lambda bi, ti: (0, 0, 0))],
        out_specs=(pl.BlockSpec((nb_step, c, QK_W), lambda bi, ti: (bi, ti, 0)),
                   pl.BlockSpec((nb_step, c, SC_W), lambda bi, ti: (bi, ti, 0)),
                   pl.BlockSpec((nb_step, N_HEADS, HEAD, HEAD), lambda bi, ti: (bi, 0, 0, 0)),
                   pl.BlockSpec((nb_step, CONV_B - 1, SC_W), lambda bi, ti: (bi, 0, 0))),
        scratch_shapes=[pltpu.VMEM((nb_step, HEAD, QK_W), f32),
                        pltpu.VMEM((nb_step, 8 + c, SC_W), f32)],
        compiler_params=_cparams(("arbitrary", "arbitrary")),
        name="delta_prompt",
    )(proj3, proj3, proj3, proj3, cwb, alog_row, dtb_row, onw_row, e64)
    return outs


def _sample_prep_kernel(p_ref, bufa_ref, bufb_ref, cwa_ref, cwb_ref, alog_ref, dtb_ref, eb_ref, eg_ref,
                        q_ref, k_ref, v_ref, beta_ref, eg_out_ref, z_ref, y_ref, nbufa_ref, nbufb_ref):
    def put_heads(ref, a):
        for h in range(N_HEADS):
            ref[:, h, :] = a[:, h * HEAD:(h + 1) * HEAD]

    def conv_sec(lo):
        hi = lo + QK_W
        raw = p_ref[:, COL_QKV + lo:COL_QKV + hi]
        acc = bufa_ref[0, :, lo:hi] * cwa_ref[0:1, lo:hi]
        acc = acc + bufa_ref[1, :, lo:hi] * cwa_ref[1:2, lo:hi]
        acc = acc + bufa_ref[2, :, lo:hi] * cwa_ref[2:3, lo:hi]
        acc = acc + raw * cwa_ref[3:4, lo:hi]
        nbufa_ref[0, :, lo:hi] = bufa_ref[1, :, lo:hi]
        nbufa_ref[1, :, lo:hi] = bufa_ref[2, :, lo:hi]
        nbufa_ref[2, :, lo:hi] = raw
        return _silu(acc)

    qn = _head_l2norm(conv_sec(0), HEAD ** -0.5)
    kn = _head_l2norm(conv_sec(QK_W), 1.0)
    for h in range(N_HEADS):
        q_ref[:, h, :] = qn[h]
        k_ref[:, h, :] = kn[h]
    put_heads(v_ref, conv_sec(2 * QK_W))
    put_heads(z_ref, p_ref[:, COL_Z:COL_Z + QK_W])

    bt = p_ref[:, COL_BA:COL_BA + BA_W]
    beta_all = _sigmoid(bt)
    g_all = -(jnp.exp(alog_ref[...]) * _softplus(bt + dtb_ref[...]))
    put_heads(beta_ref, _dot_lsplit(beta_all, eb_ref[...]))
    put_heads(eg_out_ref, jnp.exp(_dot_lsplit(g_all, eg_ref[...])))

    bg = p_ref[:, COL_BCX:COL_BCX + SC_W]
    cx = p_ref[:, COL_BCX + SC_W:COL_BCX + 2 * SC_W] * p_ref[:, COL_BCX + 2 * SC_W:COL_BCX + 3 * SC_W]
    cv = bufb_ref[0] * cwb_ref[0:1, :]
    cv = cv + bufb_ref[1] * cwb_ref[1:2, :]
    cv = cv + cx * cwb_ref[2:3, :]
    y_ref[...] = (bg * cv).astype(bf16)
    nbufb_ref[0] = bufb_ref[1]
    nbufb_ref[1] = cx


def _sample_prep(proj_s, bufa_t, bufb_t, cwa, cwb, alog_row, dtb_row):
    n = proj_s.shape[0]
    eb, eg, _ = _expand_consts()
    row = jax.ShapeDtypeStruct((n, N_HEADS, HEAD), f32)
    return pl.pallas_call(
        _sample_prep_kernel,
        out_shape=(row, row, row, row, row, row,
                   jax.ShapeDtypeStruct((n, SC_W), bf16),
                   jax.ShapeDtypeStruct((CONV_A - 1, n, QKV_W), f32),
                   jax.ShapeDtypeStruct((CONV_B - 1, n, SC_W), f32)),
        compiler_params=pltpu.CompilerParams(vmem_limit_bytes=VMEM_LIMIT),
        name="sample_prep",
    )(proj_s, bufa_t, bufb_t, cwa, cwb, alog_row, dtb_row, eb, eg)


def _sample_step_kernel(s_ref, q_ref, k_ref, v_ref, beta_ref, eg_ref, z_ref, onw_ref,
                        snew_ref, o_ref, *, bb):
    w = N_HEADS * HEAD
    r8 = lax.broadcasted_iota(i32, (N_HEADS, w), 0)
    c8 = lax.broadcasted_iota(i32, (N_HEADS, w), 1)
    mask8 = r8 == (c8 >> HEAD_SHIFT)
    zpad_k = jnp.zeros((HEAD - N_HEADS, HEAD), f32)
    hb = lambda h: slice(h * HEAD, (h + 1) * HEAD)
    bs = range(bb)
    s_dec, k8s, kts = [], [], []
    for b in bs:
        s_all = jnp.concatenate([s_ref[b, h] for h in range(N_HEADS)], axis=1)
        eg8 = eg_ref[b]
        eg_row = jnp.concatenate([eg8[h:h + 1, :] for h in range(N_HEADS)], axis=1)
        s_dec.append(s_all * eg_row)
        k8s.append(k_ref[b])
        kts.append(jnp.concatenate([k8s[b], zpad_k], axis=0).T)
    xs = [_dot(k8s[b].astype(bf16), s_dec[b].astype(bf16)) for b in bs]
    s_new = []
    for b in bs:
        vb8, bt8 = v_ref[b], beta_ref[b]
        upd = [kts[b][:, h:h + 1] * ((vb8[h:h + 1, :] - xs[b][h:h + 1, hb(h)]) * bt8[h:h + 1, :])
               for h in range(N_HEADS)]
        s_new.append(s_dec[b] + jnp.concatenate(upd, axis=1))
    ys = [_dot(q_ref[b].astype(bf16), s_new[b].astype(bf16)) for b in bs]
    for b in bs:
        yv = jnp.where(mask8, ys[b], 0.0)
        o8 = yv[:, 0:HEAD]
        for j in range(1, N_HEADS):
            o8 = o8 + yv[:, j * HEAD:(j + 1) * HEAD]
        ms = jnp.mean(o8 * o8, axis=-1, keepdims=True)
        o_ref[b] = o8 * lax.rsqrt(ms + EPS) * onw_ref[...] * _silu(z_ref[b])
        for h in range(N_HEADS):
            snew_ref[b, h] = s_new[b][:, hb(h)]


def _sample_step(state, q, k, v, beta, eg, z, onw_row, bb=8):
    n = state.shape[0]
    assert n % bb == 0
    hspec = pl.BlockSpec((bb, N_HEADS, HEAD), lambda i: (i, 0, 0))
    sspec = pl.BlockSpec((bb, N_HEADS, HEAD, HEAD), lambda i: (i, 0, 0, 0))
    return pl.pallas_call(
        functools.partial(_sample_step_kernel, bb=bb),
        out_shape=(jax.ShapeDtypeStruct(state.shape, f32),
                   jax.ShapeDtypeStruct((n, N_HEADS, HEAD), f32)),
        grid=(n // bb,),
        in_specs=[sspec, hspec, hspec, hspec, hspec, hspec, hspec, pl.BlockSpec((1, HEAD), lambda i: (0, 0))],
        out_specs=(sspec, hspec),
        compiler_params=_cparams(("arbitrary",)),
        name="sample_step",
    )(state, q, k, v, beta, eg, z, onw_row)


def _mix_route_kernel(x_ref, o_ref, y_ref, ga_ref, gb_ref, wa_ref, wb_ref, wo_ref, n2_ref,
                      rwh_ref, rwl_ref, rb_ref, cnt_in_ref, x1_ref, h2_ref, mi_ref, mw_ref, cnt_ref):
    i = pl.program_id(0)
    tm = x_ref.shape[0]

    @pl.when(i == 0)
    def _():
        cnt_ref[...] = cnt_in_ref[...]

    oa = _dot(o_ref[...], wa_ref[...])
    ob = _dot(y_ref[...], wb_ref[...])
    merged = _sigmoid(ga_ref[...]) * oa + _sigmoid(gb_ref[...]) * ob
    x1 = x_ref[...] + _dot(merged.astype(bf16), wo_ref[...])
    x1_ref[...] = x1
    ms = jnp.mean(x1 * x1, axis=-1, keepdims=True)
    h2 = x1 * lax.rsqrt(ms + EPS) * n2_ref[...]
    h2_ref[...] = h2

    h_hi, h_lo = _split(h2, 2)
    logits = _dot(h_hi, rwh_ref[...]) + _dot(h_hi, rwl_ref[...]) + _dot(h_lo, rwh_ref[...]) + rb_ref[...]

    lane = lax.broadcasted_iota(i32, (tm, LANE), 1)
    lanef = lane.astype(f32)
    neg = jnp.float32(-jnp.inf)
    big = jnp.float32(1e9)
    gmask = (lane >= N_EXPERTS) & (lane < N_EXPERTS + N_GROUPS)
    gl = jnp.where(gmask, logits, neg)
    gmax = jnp.max(gl, axis=-1, keepdims=True)
    gidx = jnp.min(jnp.where(gl == gmax, lanef - N_EXPERTS, big), axis=-1, keepdims=True)
    gsum = jnp.sum(jnp.where(gmask, jnp.exp(gl - gmax), 0.0), axis=-1, keepdims=True)
    gprob = 1.0 / gsum

    emask = (lane < N_EXPERTS) & ((lane >> GROUP_SHIFT).astype(f32) == gidx)
    el = jnp.where(emask, logits, neg)
    emax = jnp.max(el, axis=-1, keepdims=True)
    pe = jnp.where(emask, jnp.exp(el - emax), 0.0)
    eprob = pe / jnp.sum(pe, axis=-1, keepdims=True)
    p1m = jnp.where(emask, eprob, -1.0)
    m1 = jnp.max(p1m, axis=-1, keepdims=True)
    i1 = jnp.min(jnp.where(p1m == m1, lanef, big), axis=-1, keepdims=True)
    p2m = jnp.where(lanef == i1, -1.0, p1m)
    m2 = jnp.max(p2m, axis=-1, keepdims=True)
    i2 = jnp.min(jnp.where(p2m == m2, lanef, big), axis=-1, keepdims=True)
    tot = m1 + m2
    c1 = m1 / tot * gprob
    c2 = m2 / tot * gprob

    oh1 = jnp.where(lanef == i1, 1.0, 0.0)
    oh2 = jnp.where(lanef == i2, 1.0, 0.0)
    ohs = oh1 + oh2
    rt = lax.broadcasted_iota(i32, (tm, tm), 0)
    ct = lax.broadcasted_iota(i32, (tm, tm), 1)
    lstrict = jnp.where(rt > ct, 1.0, 0.0).astype(bf16)
    cs = _dot(lstrict, ohs.astype(bf16)) + cnt_ref[...]
    rank1 = jnp.sum(cs * oh1, axis=-1, keepdims=True)
    rank2 = jnp.sum(cs * oh2, axis=-1, keepdims=True)
    cnt_ref[...] = cnt_ref[...] + jnp.sum(ohs, axis=0, keepdims=True)

    mi = jnp.where(lane == 0, i1, jnp.where(lane == 1, i2, jnp.where(lane == 2, rank1,
                                                                     jnp.where(lane == 3, rank2, 0.0))))
    mi_ref[...] = mi.astype(i32)
    mw_ref[...] = jnp.where(lane == 0, c1, jnp.where(lane == 1, c2, 0.0))


def _mix_route(x2d, o2d, y2d, proj2d, wa, wb, wo, n2_row, rwh, rwl, rb_row, cnt_in):
    n = x2d.shape[0]
    tm = min(256, n)
    assert n % tm == 0
    tok = lambda width: pl.BlockSpec((tm, width), lambda i: (i, 0))
    full = lambda a: pl.BlockSpec(a.shape, lambda i: (0,) * a.ndim)
    in_specs = [tok(D_MODEL), tok(QK_W), tok(SC_W),
                pl.BlockSpec((tm, D_MODEL), lambda i: (i, COL_GA // D_MODEL)),
                pl.BlockSpec((tm, D_MODEL), lambda i: (i, COL_GB // D_MODEL)),
                full(wa), full(wb), full(wo), full(n2_row), full(rwh), full(rwl), full(rb_row), full(cnt_in)]
    out_shape = (jax.ShapeDtypeStruct((n, D_MODEL), f32),
                 jax.ShapeDtypeStruct((n, D_MODEL), f32),
                 jax.ShapeDtypeStruct((n, LANE), i32),
                 jax.ShapeDtypeStruct((n, LANE), f32),
                 jax.ShapeDtypeStruct((1, LANE), f32))
    out_specs = (tok(D_MODEL), tok(D_MODEL), tok(LANE), tok(LANE),
                 pl.BlockSpec((1, LANE), lambda i: (0, 0)))
    return pl.pallas_call(
        _mix_route_kernel,
        out_shape=out_shape,
        grid=(n // tm,),
        in_specs=in_specs,
        out_specs=out_specs,
        compiler_params=_cparams(("arbitrary",)),
        name="mix_route",
    )(x2d, o2d, y2d, proj2d, proj2d, wa, wb, wo, n2_row, rwh, rwl, rb_row, cnt_in)


MI_W = 4
SUBLANE = 8


def _dest_kernel(mi_ref, starts_ref, o_ref):
    mi = mi_ref[...]
    lane = lax.broadcasted_iota(i32, mi.shape, 1)
    st = starts_ref[...]

    def first_row(e_col):
        return jnp.sum(jnp.where(lane == e_col, st, 0.0), axis=-1, keepdims=True).astype(i32)

    d0 = first_row(mi[:, 0:1]) + mi[:, 2:3]
    d1 = first_row(mi[:, 1:2]) + mi[:, 3:4]
    sh = SUBLANE.bit_length() - 1
    o_ref[...] = jnp.where(lane == 0, d0 >> sh, jnp.where(lane == 1, d0 & (SUBLANE - 1),
                           jnp.where(lane == 2, d1 >> sh, jnp.where(lane == 3, d1 & (SUBLANE - 1), 0))))


def _dest_rows(mi, starts_row):
    n = mi.shape[0]
    tm = min(1024, n)
    assert n % tm == 0
    return pl.pallas_call(
        _dest_kernel,
        out_shape=jax.ShapeDtypeStruct((n, LANE), i32),
        grid=(n // tm,),
        in_specs=[pl.BlockSpec((tm, LANE), lambda i: (i, 0)), pl.BlockSpec((1, LANE), lambda i: (0, 0))],
        out_specs=pl.BlockSpec((tm, LANE), lambda i: (i, 0)),
        compiler_params=_cparams(("arbitrary",)),
        name="moe_dest",
    )(mi, starts_row)


def _dispatch_kernel(mi_ref, hp_ref, hs_ref, xs_ref, sem, *, np_tiles):
    i = pl.program_id(0)

    def scatter_rows(h_ref):
        n_tiles = h_ref.shape[0]

        def start(t, c):
            for u in range(SUBLANE):
                rec = MI_W * (SUBLANE * t + u)
                for k in range(2):
                    dst = xs_ref.at[mi_ref[rec + 2 * k], pl.ds(mi_ref[rec + 2 * k + 1], 1)]
                    pltpu.make_async_copy(h_ref.at[t, pl.ds(u, 1)], dst, sem).start(priority=k)
            return c

        lax.fori_loop(0, n_tiles, start, 0)
        for k in range(2):
            pltpu.make_async_copy(h_ref, xs_ref.at[pl.ds(0, n_tiles)], sem).wait()

    @pl.when(i < np_tiles)
    def _():
        scatter_rows(hp_ref)

    @pl.when(i >= np_tiles)
    def _():
        scatter_rows(hs_ref)


def _dispatch(h2_p, h2_s, mi_flat):
    tm = TOKEN_TILE
    n_p, n_s = h2_p.shape[0], h2_s.shape[0]
    assert n_p % tm == 0 and n_s <= tm and n_s % SUBLANE == 0 and tm % SUBLANE == 0
    np_tiles = n_p // tm
    tiled = lambda a: a.reshape(a.shape[0] // SUBLANE, SUBLANE, D_MODEL)
    return pl.pallas_call(
        functools.partial(_dispatch_kernel, np_tiles=np_tiles),
        out_shape=jax.ShapeDtypeStruct((2 * (n_p + n_s) // SUBLANE, SUBLANE, D_MODEL), f32),
        grid=(np_tiles + 1,),
        in_specs=[pl.BlockSpec((MI_W * tm,), lambda i: (i,), memory_space=pltpu.SMEM),
                  pl.BlockSpec((tm // SUBLANE, SUBLANE, D_MODEL), lambda i: (jnp.minimum(i, np_tiles - 1), 0, 0)),
                  pl.BlockSpec((n_s // SUBLANE, SUBLANE, D_MODEL), lambda i: (0, 0, 0))],
        out_specs=pl.BlockSpec(memory_space=pl.ANY),
        scratch_shapes=[pltpu.SemaphoreType.DMA(())],
        compiler_params=_cparams(("arbitrary",)),
        name="moe_dispatch",
    )(mi_flat, tiled(h2_p), tiled(h2_s))


def _cast_rows(src_ref, dst_ref, col0=0, rows=256):
    width = src_ref.shape[1]

    def body(r, c):
        sl = pl.ds(pl.multiple_of(r * rows, rows), rows)
        dst_ref[sl, col0:col0 + width] = src_ref[sl, :].astype(bf16)
        return c
    lax.fori_loop(0, src_ref.shape[0] // rows, body, 0)


def _moe_kernel(blk_ref, lo_ref, hi_ref, first_ref, newe_ref, slot_ref, pre_ref, init_ref,
                x_ref, wg_hbm, wu_hbm, wd_hbm, o_ref,
                wg_f, wu_f, wd_f, wgu_b, wd_b, sem):
    i = pl.program_id(0)
    lo = lo_ref[i]
    hi = hi_ref[i]

    def weight_copies(e, slot):
        return [pltpu.make_async_copy(wg_hbm.at[e], wg_f.at[slot], sem.at[slot, 0]),
                pltpu.make_async_copy(wu_hbm.at[e], wu_f.at[slot], sem.at[slot, 1]),
                pltpu.make_async_copy(wd_hbm.at[e], wd_f.at[slot], sem.at[slot, 2])]

    def start_weights(e, slot):
        for cp, prio in zip(weight_copies(e, slot), (0, 1, 1)):
            cp.start(priority=prio)

    @pl.when(i == 0)
    def _():
        start_weights(init_ref[0], 0)
        for k in range(1, W_SLOTS):
            @pl.when(init_ref[k] >= 0)
            def _():
                start_weights(init_ref[k], k)

    @pl.when(newe_ref[i] == 1)
    def _():
        slot = slot_ref[i]
        cg, cu, cd = weight_copies(0, slot)
        cg.wait()
        _cast_rows(wg_f.at[slot], wgu_b, 0)
        cu.wait()
        _cast_rows(wu_f.at[slot], wgu_b, D_FF)
        cd.wait()
        _cast_rows(wd_f.at[slot], wd_b)

        @pl.when(pre_ref[i] >= 0)
        def _():
            start_weights(pre_ref[i], slot)

    @pl.when(hi > lo)
    def _():
        x = x_ref[...].astype(bf16)
        au = _dot(x, wgu_b[...])
        y = _dot((_silu(au[:, 0:D_FF]) * au[:, D_FF:2 * D_FF]).astype(bf16), wd_b[...])
        row = lax.broadcasted_iota(i32, y.shape, 0)
        ym = jnp.where((row >= lo) & (row < hi), y, 0.0)

        @pl.when(first_ref[i] == 1)
        def _():
            o_ref[...] = ym

        @pl.when(first_ref[i] == 0)
        def _():
            o_ref[...] = o_ref[...] + ym


def _moe(xs, w_gate, w_up, w_down, items):
    n_items = items[0].shape[0]
    rows = xs.shape[0]
    n_pref = len(items)
    xmap = lambda i, blk, *_: (blk[i], 0)
    grid_spec = pltpu.PrefetchScalarGridSpec(
        num_scalar_prefetch=n_pref,
        grid=(n_items,),
        in_specs=[pl.BlockSpec((MOE_ROWS, D_MODEL), xmap),
                  pl.BlockSpec(memory_space=pl.ANY),
                  pl.BlockSpec(memory_space=pl.ANY),
                  pl.BlockSpec(memory_space=pl.ANY)],
        out_specs=pl.BlockSpec((MOE_ROWS, D_MODEL), xmap),
        scratch_shapes=[pltpu.VMEM((W_SLOTS, D_MODEL, D_FF), f32), pltpu.VMEM((W_SLOTS, D_MODEL, D_FF), f32),
                        pltpu.VMEM((W_SLOTS, D_FF, D_MODEL), f32),
                        pltpu.VMEM((D_MODEL, 2 * D_FF), bf16), pltpu.VMEM((D_FF, D_MODEL), bf16),
                        pltpu.SemaphoreType.DMA((W_SLOTS, 3))],
    )
    return pl.pallas_call(
        _moe_kernel,
        out_shape=jax.ShapeDtypeStruct((rows, D_MODEL), f32),
        grid_spec=grid_spec,
        compiler_params=_cparams(("arbitrary",)),
        name="moe_experts",
    )(*items, xs, w_gate, w_up, w_down)


def _combine_kernel(mi_ref, mi_next_ref, x1p_ref, mwp_ref, x1s_ref, mws_ref, fnw_ref, ys_ref,
                    yp_ref, ysm_ref, g_ref, sem, *, np_tiles):
    i = pl.program_id(0)
    tiles_p = x1p_ref.shape[0] // SUBLANE
    tiles_s = x1s_ref.shape[0] // SUBLANE
    slot = lax.rem(i, 2)

    def gather_rows(m_ref, dst_slot, n_tiles):
        def body(t, c):
            for u in range(SUBLANE):
                rec = MI_W * (SUBLANE * t + u)
                for k in range(2):
                    src = ys_ref.at[m_ref[rec + 2 * k], pl.ds(m_ref[rec + 2 * k + 1], 1)]
                    pltpu.make_async_copy(src, g_ref.at[dst_slot, k, t, pl.ds(u, 1)],
                                          sem.at[dst_slot]).start(priority=k)
            return c
        lax.fori_loop(0, n_tiles, body, 0)

    @pl.when(i == 0)
    def _():
        gather_rows(mi_ref, 0, tiles_p)

    @pl.when(i + 1 < np_tiles)
    def _():
        gather_rows(mi_next_ref, 1 - slot, tiles_p)

    @pl.when(i + 1 == np_tiles)
    def _():
        gather_rows(mi_next_ref, 1 - slot, tiles_s)

    def finish(x1_ref, mw_ref, out_ref, n_tiles):
        for k in range(2):
            pltpu.make_async_copy(ys_ref.at[pl.ds(0, n_tiles)], g_ref.at[slot, k, pl.ds(0, n_tiles)],
                                  sem.at[slot]).wait()
        rows = n_tiles * SUBLANE
        mw = mw_ref[...]
        g0 = g_ref[slot, 0, 0:n_tiles].reshape(rows, D_MODEL)
        g1 = g_ref[slot, 1, 0:n_tiles].reshape(rows, D_MODEL)
        x2 = x1_ref[...] + (g0 * mw[:, 0:1] + g1 * mw[:, 1:2])
        ms = jnp.mean(x2 * x2, axis=-1, keepdims=True)
        out_ref[...] = x2 * lax.rsqrt(ms + EPS) * fnw_ref[...]

    @pl.when(i < np_tiles)
    def _():
        finish(x1p_ref, mwp_ref, yp_ref, tiles_p)

    @pl.when(i >= np_tiles)
    def _():
        finish(x1s_ref, mws_ref, ysm_ref, tiles_s)


def _combine(x1_p, mw_p, x1_s, mw_s, fnw_row, ys3, mi_flat):
    tm = TOKEN_TILE
    n_p, n_s = x1_p.shape[0], x1_s.shape[0]
    assert n_p % tm == 0 and n_s <= tm and n_s % SUBLANE == 0
    np_tiles = n_p // tm
    ptile = lambda width: pl.BlockSpec((tm, width), lambda i: (jnp.minimum(i, np_tiles - 1), 0))
    stile = lambda width: pl.BlockSpec((n_s, width), lambda i: (0, 0))
    return pl.pallas_call(
        functools.partial(_combine_kernel, np_tiles=np_tiles),
        out_shape=(jax.ShapeDtypeStruct((n_p, D_MODEL), f32),
                   jax.ShapeDtypeStruct((n_s, D_MODEL), f32)),
        grid=(np_tiles + 1,),
        in_specs=[pl.BlockSpec((MI_W * tm,), lambda i: (i,), memory_space=pltpu.SMEM),
                  pl.BlockSpec((MI_W * tm,), lambda i: (jnp.minimum(i + 1, np_tiles),), memory_space=pltpu.SMEM),
                  ptile(D_MODEL), ptile(LANE), stile(D_MODEL), stile(LANE),
                  pl.BlockSpec((1, D_MODEL), lambda i: (0, 0)),
                  pl.BlockSpec(memory_space=pl.ANY)],
        out_specs=(ptile(D_MODEL), stile(D_MODEL)),
        scratch_shapes=[pltpu.VMEM((2, 2, tm // SUBLANE, SUBLANE, D_MODEL), f32), pltpu.SemaphoreType.DMA((2,))],
        compiler_params=_cparams(("arbitrary",)),
        name="moe_combine",
    )(mi_flat, mi_flat, x1_p, mw_p, x1_s, mw_s, fnw_row, ys3)


PLAN_ROWS = 256
N_ITEM_FIELDS = 7


def _plan_kernel(cnt_ref, items_ref, rows_ref, *, nblk):
    cnt = cnt_ref[...]
    lane1 = lax.broadcasted_iota(i32, (1, LANE), 1)
    in_e = lane1 < N_EXPERTS
    ri = lax.broadcasted_iota(i32, (LANE, LANE), 0)
    ci = lax.broadcasted_iota(i32, (LANE, LANE), 1)
    upper = jnp.where(ri <= ci, 1.0, 0.0).astype(bf16)

    def cumsum_lanes(v):
        return _dot_lsplit(jnp.broadcast_to(v, (8, LANE)), upper)[0:1, :]

    shift = MOE_ROWS.bit_length() - 1
    ends = cumsum_lanes(cnt)
    starts = ends - cnt
    act = cnt > 0.0
    first_blk = (starts.astype(i32) >> shift).astype(f32)
    last_blk = (jnp.maximum(ends - 1.0, 0.0).astype(i32) >> shift).astype(f32)
    nvis = jnp.where(act, last_blk - first_blk + 1.0, 0.0)
    vis_end = cumsum_lanes(nvis)
    vis_start = vis_end - nvis
    total = jnp.max(vis_end, axis=-1, keepdims=True)
    cum_act = cumsum_lanes(jnp.where(act, 1.0, 0.0))
    n_uniq = jnp.max(cum_act, axis=-1, keepdims=True)

    p = PLAN_ROWS
    lane = lax.broadcasted_iota(i32, (p, LANE), 1)
    idx = lax.broadcasted_iota(i32, (p, LANE), 0).astype(f32)
    idx1 = idx[:, 0:1]
    count_le = lambda row, col: jnp.sum(jnp.where((row <= col) & in_e, 1.0, 0.0), axis=-1, keepdims=True)
    e = jnp.minimum(count_le(vis_end, idx), N_EXPERTS - 1.0)
    onehot = lane.astype(f32) == e
    look = lambda tbl: jnp.sum(jnp.where(onehot, tbl, 0.0), axis=-1, keepdims=True)
    blk = look(first_blk) + idx1 - look(vis_start)
    lo = jnp.maximum(look(starts), blk * MOE_ROWS) - blk * MOE_ROWS
    hi = jnp.minimum(look(ends), (blk + 1.0) * MOE_ROWS) - blk * MOE_ROWS
    valid = idx1 < total
    blk = jnp.where(valid, blk, nblk - 1.0)
    lo = jnp.where(valid, lo, 0.0)
    hi = jnp.where(valid, hi, 0.0)
    rep = lambda c: jnp.broadcast_to(c, (p, LANE))
    prev = lambda c: pltpu.roll(rep(c), 1, axis=0)[:, 0:1]
    is0 = idx1 == 0.0
    first = valid & (is0 | (blk != prev(blk)))
    newe = valid & (is0 | (e != prev(e)))
    rp = lax.broadcasted_iota(i32, (p, p), 0)
    cp = lax.broadcasted_iota(i32, (p, p), 1)
    lower = jnp.where(rp >= cp, 1.0, 0.0).astype(bf16)
    order = _dot(lower, rep(jnp.where(newe, 1.0, 0.0)).astype(bf16))[:, 0:1] - 1.0
    slot = jnp.where(newe, order - W_SLOTS * jnp.floor((order + 0.5) * (1.0 / W_SLOTS)), 0.0)
    k2 = order + float(W_SLOTS)
    pre = jnp.where(newe & (k2 < n_uniq), count_le(cum_act, rep(k2)), -1.0)
    out = jnp.zeros((p, LANE), f32)
    for c, v in enumerate([blk, lo, hi, jnp.where(first, 1.0, 0.0), jnp.where(newe, 1.0, 0.0), slot, pre]):
        out = jnp.where(lane == c, v, out)
    items_ref[...] = out.astype(i32)

    init_row = jnp.zeros((1, LANE), f32)
    for k in range(W_SLOTS):
        init_row = jnp.where(lane1 == k, jnp.where(n_uniq > float(k), count_le(cum_act, float(k)), -1.0), init_row)
    rows_ref[...] = jnp.zeros(rows_ref.shape, f32)
    rows_ref[0:1, :] = starts
    rows_ref[1:2, :] = init_row


def _work_items(cnt_row, n_rows):
    nblk = n_rows // MOE_ROWS
    n_items = nblk + N_EXPERTS - 1
    assert n_items <= PLAN_ROWS and n_rows % MOE_ROWS == 0
    items, rows = pl.pallas_call(
        functools.partial(_plan_kernel, nblk=nblk),
        out_shape=(jax.ShapeDtypeStruct((PLAN_ROWS, LANE), i32), jax.ShapeDtypeStruct((8, LANE), f32)),
        compiler_params=pltpu.CompilerParams(vmem_limit_bytes=VMEM_LIMIT),
        name="moe_plan",
    )(cnt_row)
    fields = tuple(items[0:n_items, c] for c in range(N_ITEM_FIELDS))
    return rows[0:1, :], fields + (rows[1, 0:W_SLOTS].astype(i32),)


def kernel(x_prompt, x_sample, state_delta, state_qkv_conv, state_short_conv, norm1_w, w_in, conv_a_w, a_log, dt_bias, out_norm_w, w_branch_a, conv_b_w, w_branch_b, w_o, norm2_w, router_group_w, router_group_b, router_expert_w, router_expert_b, w_gate, w_up, w_down, final_norm_w):
    assert norm1_w.shape[0] == 1, "single-layer trunk"
    bp, tp, d = x_prompt.shape
    bs, ts, _ = x_sample.shape
    assert d == D_MODEL and ts == 1
    n_p = bp * tp
    n_s = bs
    n_all = n_p + n_s

    w_perm = _wprep(jnp.transpose(w_in[0]))
    wa = w_branch_a[0].astype(bf16)
    wb = w_branch_b[0].astype(bf16)
    wo = w_o[0].astype(bf16)
    pad = lambda v: jnp.zeros((1, BA_W), f32).at[0, DECAY_LANE:DECAY_LANE + N_HEADS].set(v)
    alog_row = pad(a_log[0])
    dtb_row = pad(dt_bias[0])
    onw_row = out_norm_w[0].reshape(1, HEAD)
    cwa = conv_a_w[0]
    cwb = conv_b_w[0]
    r_pad = LANE - N_EXPERTS - N_GROUPS
    rw = jnp.concatenate([router_expert_w[0], router_group_w[0], jnp.zeros((D_MODEL, r_pad), f32)], axis=1)
    rwh = rw.astype(bf16)
    rwl = (rw - rwh.astype(f32)).astype(bf16)
    rb_row = jnp.concatenate([router_expert_b[0], router_group_b[0], jnp.zeros((r_pad,), f32)]).reshape(1, LANE)
    n2_row = norm2_w[0].reshape(1, D_MODEL)

    xp2 = x_prompt.reshape(n_p, D_MODEL)
    proj_p, tails = _inproj_conv(xp2, norm1_w[0], w_perm, cwa, tp)
    tiles_per_seq = tails.shape[0] // bp
    nca_p = tails.reshape(bp, tiles_per_seq, 8, CONV_PAD_W)[:, -1, 8 - (CONV_A - 1):8, 0:QKV_W]
    o_p, y_p, sd_p, ncb_p = _delta_prompt(proj_p.reshape(bp, tp, PROJ_W), cwb, alog_row, dtb_row,
                                          onw_row, nb_step=4 if bp % 4 == 0 else (2 if bp % 2 == 0 else 1))
    cnt0 = jnp.zeros((1, LANE), f32)
    x1_p, h2_p, mi_p, mw_p, cnt_p = _mix_route(xp2, o_p.reshape(n_p, QK_W), y_p.reshape(n_p, SC_W), proj_p,
                                               wa, wb, wo, n2_row, rwh, rwl, rb_row, cnt0)

    xs2 = x_sample.reshape(n_s, D_MODEL)
    proj_s = _inproj(xs2, norm1_w[0], w_perm)
    bufa_t = jnp.transpose(state_qkv_conv[0], (1, 0, 2))
    bufb_t = jnp.transpose(state_short_conv[0], (1, 0, 2))
    q_s, k_s, v_s, beta_s, eg_s, z_s, y_s, nbufa_t, nbufb_t = _sample_prep(proj_s, bufa_t, bufb_t, cwa, cwb,
                                                                           alog_row, dtb_row)
    sd_s, o_s = _sample_step(state_delta[0], q_s, k_s, v_s, beta_s, eg_s, z_s, onw_row)
    o_s2 = o_s.reshape(n_s, QK_W).astype(bf16)
    x1_s, h2_s, mi_s, mw_s, cnt = _mix_route(xs2, o_s2, y_s, proj_s, wa, wb, wo, n2_row, rwh, rwl, rb_row, cnt_p)

    starts_row, items = _work_items(cnt, 2 * n_all)
    mi_flat = jnp.concatenate([_dest_rows(mi_p, starts_row)[:, 0:MI_W], _dest_rows(mi_s, starts_row)[:, 0:MI_W]],
                              axis=0).reshape(MI_W * n_all)
    xs_sorted = _dispatch(h2_p, h2_s, mi_flat)
    ys = _moe(xs_sorted.reshape(2 * n_all, D_MODEL), w_gate[0], w_up[0], w_down[0], items)
    y_prompt, y_sample = _combine(x1_p, mw_p, x1_s, mw_s, final_norm_w.reshape(1, D_MODEL),
                                  ys.reshape(2 * n_all // SUBLANE, SUBLANE, D_MODEL), mi_flat)

    return (y_prompt.reshape(bp, tp, D_MODEL),
            y_sample.reshape(bs, ts, D_MODEL),
            sd_p[None],
            nca_p[None],
            ncb_p[None],
            sd_s[None],
            jnp.transpose(nbufa_t, (1, 0, 2))[None],
            jnp.transpose(nbufb_t, (1, 0, 2))[None])
```

```python
import functools

import jax
import jax.numpy as jnp
from jax import lax
from jax.experimental import pallas as pl
from jax.experimental.pallas import tpu as pltpu

f32 = jnp.float32
bf16 = jnp.bfloat16
i32 = jnp.int32

EPS = 1e-6
LANE = 128
D_MODEL = 2048
N_HEADS = 8
HEAD = 128
QK_W = N_HEADS * HEAD
QKV_W = 3 * QK_W
SC_W = 1024
CONV_A = 4
CONV_B = 3
CHUNK = 64
CHUNK_SHIFT = CHUNK.bit_length() - 1
HEAD_SHIFT = HEAD.bit_length() - 1
GROUP_HEADS = 4
DECAY_LANE = N_HEADS
N_EXPERTS = 64
N_GROUPS = 8
EXPERTS_PER_GROUP = 8
GROUP_SHIFT = EXPERTS_PER_GROUP.bit_length() - 1
D_FF = 512
MOE_ROWS = 128
TOKEN_TILE = 512
W_SLOTS = 2

COL_QKV = 0
COL_BCX = 3072
COL_GA = 6144
COL_GB = 8192
COL_Z = 10240
COL_BA = 11264
BA_W = 256
PROJ_W = 11520
PROJ_TN = 1280
PROJ_TN_SMALL_M = 2304

VMEM_LIMIT = 56 * 1024 * 1024


def _dot(a, b):
    return jnp.dot(a, b, preferred_element_type=f32)


def _dot_nt(a, b):
    return lax.dot_general(a, b, (((1,), (1,)), ((), ())), preferred_element_type=f32)


def _split(x, n):
    parts = []
    r = x
    for i in range(n):
        p = r.astype(bf16)
        parts.append(p)
        if i + 1 < n:
            r = r - p.astype(f32)
    return parts


def _dot_lsplit(x, m, n=3):
    rows = x.shape[0]
    d = _dot(jnp.concatenate(_split(x, n), axis=0), m)
    acc = d[0:rows]
    for i in range(1, n):
        acc = acc + d[i * rows:(i + 1) * rows]
    return acc


def _dot_rsplit(m, x, n=3):
    cols = x.shape[1]
    d = _dot(m, jnp.concatenate(_split(x, n), axis=1))
    acc = d[:, 0:cols]
    for i in range(1, n):
        acc = acc + d[:, i * cols:(i + 1) * cols]
    return acc


_sigmoid = jax.nn.sigmoid


def _silu(x):
    return x * _sigmoid(x)


def _softplus(x):
    return jnp.maximum(x, 0.0) + jnp.log(1.0 + jnp.exp(-jnp.abs(x)))


def _cparams(sem):
    return pltpu.CompilerParams(dimension_semantics=sem, vmem_limit_bytes=VMEM_LIMIT)


def _inproj_kernel(x_ref, nw_ref, w_ref, o_ref, h_ref, *, rows):
    @pl.when(pl.program_id(1) == 0)
    def _():
        def body(r, c):
            sl = pl.ds(pl.multiple_of(r * rows, rows), rows)
            x = x_ref[sl, :]
            ms = jnp.mean(x * x, axis=-1, keepdims=True)
            h_ref[sl, :] = (x * lax.rsqrt(ms + EPS) * nw_ref[...]).astype(bf16)
            return c
        lax.fori_loop(0, x_ref.shape[0] // rows, body, 0)

    o_ref[...] = _dot_nt(h_ref[...], w_ref[...])


def _inproj(x2d, norm_w, w_bf16):
    n = x2d.shape[0]
    tm = min(1024, n)
    tn = PROJ_TN if tm == 1024 else PROJ_TN_SMALL_M
    assert n % tm == 0 and PROJ_W % tn == 0
    return pl.pallas_call(
        functools.partial(_inproj_kernel, rows=min(128, tm)),
        out_shape=jax.ShapeDtypeStruct((n, PROJ_W), f32),
        grid=(n // tm, PROJ_W // tn),
        in_specs=[pl.BlockSpec((tm, D_MODEL), lambda i, j: (i, 0)),
                  pl.BlockSpec((1, D_MODEL), lambda i, j: (0, 0)),
                  pl.BlockSpec((tn, D_MODEL), lambda i, j: (j, 0))],
        out_specs=pl.BlockSpec((tm, tn), lambda i, j: (i, j)),
        scratch_shapes=[pltpu.VMEM((tm, D_MODEL), bf16)],
        compiler_params=_cparams(("arbitrary", "arbitrary")),
        name="inproj",
    )(x2d, norm_w.reshape(1, D_MODEL), w_bf16)


CONV_TILES = 3
CONV_COLS = 2 * HEAD
CONV_ROWS = 128
CONV_PAD_W = CONV_TILES * PROJ_TN


def _qkv_kind(col):
    return "q" if col < QK_W else "k" if col < 2 * QK_W else "v" if col < QKV_W else "raw"


def _inproj_conv_kernel(x_ref, nw_ref, w_ref, cw_ref, o_ref, tail_ref, h_ref, hist_ref, raw_ref, *,
                        rows, tiles_per_seq):
    i = pl.program_id(0)
    j = pl.program_id(1)
    tm = x_ref.shape[0]

    @pl.when(j == 0)
    def _():
        def body(r, c):
            sl = pl.ds(pl.multiple_of(r * rows, rows), rows)
            x = x_ref[sl, :]
            ms = jnp.mean(x * x, axis=-1, keepdims=True)
            h_ref[sl, :] = (x * lax.rsqrt(ms + EPS) * nw_ref[...]).astype(bf16)
            return c
        lax.fori_loop(0, tm // rows, body, 0)

    @pl.when((i == 0) & (j == 0))
    def _():
        hist_ref[...] = jnp.zeros(hist_ref.shape, f32)

    @pl.when(j >= CONV_TILES)
    def _():
        o_ref[...] = _dot_nt(h_ref[...], w_ref[...])

    seq_start = lax.rem(i, tiles_per_seq) == 0
    for jj in range(CONV_TILES):
        @pl.when(j == jj)
        def _():
            def matmul_chunk(idx, c0):
                raw_ref[idx % 2] = _dot_nt(h_ref[...], w_ref[c0:c0 + CONV_COLS, :])

            def conv_chunk(idx, c0):
                cs = slice(c0, c0 + CONV_COLS)
                raw = raw_ref.at[idx % 2]
                tail = raw[tm - 8:tm, :]
                tail_ref[0, :, cs] = tail
                kinds = [_qkv_kind(jj * PROJ_TN + c0 + g * HEAD) for g in range(CONV_COLS // HEAD)]
                if kinds[0] == "raw":
                    o_ref[:, cs] = raw[...]
                    return
                hist = jnp.where(seq_start, 0.0, hist_ref[jj, :, cs])
                for rc in range(tm // CONV_ROWS):
                    r0 = rc * CONV_ROWS
                    if rc > 0:
                        xe = raw[r0 - 8:r0 + CONV_ROWS, :]
                    else:
                        xe = jnp.concatenate([hist, raw[0:CONV_ROWS, :]], axis=0)
                    acc = pltpu.roll(xe, 3, axis=0)[8:] * cw_ref[0:1, cs]
                    acc = acc + pltpu.roll(xe, 2, axis=0)[8:] * cw_ref[1:2, cs]
                    acc = acc + pltpu.roll(xe, 1, axis=0)[8:] * cw_ref[2:3, cs]
                    acc = acc + xe[8:] * cw_ref[3:4, cs]
                    act = _silu(acc)
                    for g, kind in enumerate(kinds):
                        ah = act[:, g * HEAD:(g + 1) * HEAD]
                        if kind != "v":
                            ss = jnp.sum(ah * ah, axis=-1, keepdims=True)
                            inv = lax.rsqrt(ss + EPS)
                            ah = ah * (inv * (HEAD ** -0.5) if kind == "q" else inv)
                        o_ref[r0:r0 + CONV_ROWS, c0 + g * HEAD:c0 + (g + 1) * HEAD] = ah
                hist_ref[jj, :, cs] = tail

            chunks = list(range(0, PROJ_TN, CONV_COLS))
            matmul_chunk(0, chunks[0])
            for idx in range(1, len(chunks)):
                matmul_chunk(idx, chunks[idx])
                conv_chunk(idx - 1, chunks[idx - 1])
            conv_chunk(len(chunks) - 1, chunks[-1])


def _inproj_conv(x2d, norm_w, w_bf16, cwa, seq_len):
    n = x2d.shape[0]
    tm = min(1024, seq_len)
    assert n % tm == 0 and seq_len % tm == 0 and PROJ_W % PROJ_TN == 0 and tm % CONV_ROWS == 0
    assert QKV_W % CONV_COLS == 0 and PROJ_TN % CONV_COLS == 0
    cw_pad = jnp.zeros((CONV_A, CONV_PAD_W), f32).at[:, 0:QKV_W].set(cwa)
    last = CONV_TILES - 1
    return pl.pallas_call(
        functools.partial(_inproj_conv_kernel, rows=min(128, tm), tiles_per_seq=seq_len // tm),
        out_shape=(jax.ShapeDtypeStruct((n, PROJ_W), f32),
                   jax.ShapeDtypeStruct((n // tm, 8, CONV_PAD_W), f32)),
        grid=(n // tm, PROJ_W // PROJ_TN),
        in_specs=[pl.BlockSpec((tm, D_MODEL), lambda i, j: (i, 0)),
                  pl.BlockSpec((1, D_MODEL), lambda i, j: (0, 0)),
                  pl.BlockSpec((PROJ_TN, D_MODEL), lambda i, j: (j, 0)),
                  pl.BlockSpec((CONV_A, PROJ_TN), lambda i, j: (0, jnp.minimum(j, last)))],
        out_specs=(pl.BlockSpec((tm, PROJ_TN), lambda i, j: (i, j)),
                   pl.BlockSpec((1, 8, PROJ_TN), lambda i, j: (i, 0, jnp.minimum(j, last)))),
        scratch_shapes=[pltpu.VMEM((tm, D_MODEL), bf16), pltpu.VMEM((CONV_TILES, 8, PROJ_TN), f32),
                        pltpu.VMEM((2, tm, CONV_COLS), f32)],
        compiler_params=_cparams(("arbitrary", "arbitrary")),
        name="inproj_conv",
    )(x2d, norm_w.reshape(1, D_MODEL), w_bf16, cw_pad)


W_IN_COLS = 11280
WPREP_TN = 1024
WPREP_SHIFT = 16
WP_BCX = COL_BCX // WPREP_TN
WP_Z = COL_Z // WPREP_TN
WP_BA = COL_BA // WPREP_TN
WP_SRC_Z = QKV_W // WPREP_TN
WP_SRC_BA = WP_SRC_Z + 1


def _wprep_kernel(a_ref, b_ref, o_ref):
    j = pl.program_id(0)
    keep = WPREP_TN - WPREP_SHIFT

    @pl.when((j < WP_BCX) | (j == WP_Z))
    def _():
        o_ref[...] = a_ref[...].astype(bf16)

    @pl.when((j >= WP_BCX) & (j < WP_Z))
    def _():
        o_ref[0:keep, :] = a_ref[WPREP_SHIFT:WPREP_TN, :].astype(bf16)
        o_ref[keep:WPREP_TN, :] = b_ref[...].astype(bf16)

    @pl.when(j == WP_BA)
    def _():
        o_ref[0:WPREP_SHIFT, :] = a_ref[0:WPREP_SHIFT, :].astype(bf16)
        o_ref[WPREP_SHIFT:WPREP_TN, :] = jnp.zeros((keep, D_MODEL), bf16)


def _wprep(w_in_t):
    assert w_in_t.shape == (W_IN_COLS, D_MODEL) and 2 * N_HEADS == WPREP_SHIFT
    n_blk = pl.cdiv(PROJ_W, WPREP_TN)

    def a_map(j):
        return (jnp.where(j < WP_BCX, j, jnp.where(j < WP_Z, j + 1, jnp.where(j == WP_Z, WP_SRC_Z, WP_SRC_BA))), 0)

    def b_map(j):
        return (jnp.minimum((WPREP_TN // WPREP_SHIFT) * (j + 2), W_IN_COLS // WPREP_SHIFT - 1), 0)

    return pl.pallas_call(
        _wprep_kernel,
        out_shape=jax.ShapeDtypeStruct((PROJ_W, D_MODEL), bf16),
        grid=(n_blk,),
        in_specs=[pl.BlockSpec((WPREP_TN, D_MODEL), a_map),
                  pl.BlockSpec((WPREP_SHIFT, D_MODEL), b_map)],
        out_specs=pl.BlockSpec((WPREP_TN, D_MODEL), lambda j: (j, 0)),
        compiler_params=_cparams(("arbitrary",)),
        name="wprep",
    )(w_in_t, w_in_t)


def _head_l2norm(a, scale):
    outs = []
    for h in range(N_HEADS):
        ah = a[:, h * HEAD:(h + 1) * HEAD]
        ss = jnp.sum(ah * ah, axis=-1, keepdims=True)
        n = ah * lax.rsqrt(ss + EPS)
        outs.append(n * scale if scale != 1.0 else n)
    return outs


def _delta_prompt_kernel(qkv_ref, bcx_ref, z_ref, ba_ref, cwb_ref, alog_ref, dtb_ref, onw_ref,
                         e64_ref,
                         o_ref, y_ref, snew_ref, ncb_ref,
                         s_ref, xb_ref, *, nb_step):
    C = CHUNK
    G = GROUP_HEADS
    R = G * C
    t = pl.program_id(1)
    nt = pl.num_programs(1)

    @pl.when(t == 0)
    def _():
        s_ref[...] = jnp.zeros(s_ref.shape, f32)
        xb_ref[:, 0:8, :] = jnp.zeros((nb_step, 8, SC_W), f32)

    rr = lax.broadcasted_iota(i32, (R, R), 0)
    cc = lax.broadcasted_iota(i32, (R, R), 1)
    same_bf = jnp.where((rr >> CHUNK_SHIFT) == (cc >> CHUNK_SHIFT), 1.0, 0.0).astype(bf16)
    r2 = lax.broadcasted_iota(i32, (R, G * HEAD), 0)
    c2 = lax.broadcasted_iota(i32, (R, G * HEAD), 1)
    bdmask = (r2 >> CHUNK_SHIFT) == (c2 >> HEAD_SHIFT)
    r3 = lax.broadcasted_iota(i32, (C, C), 0)
    c3 = lax.broadcasted_iota(i32, (C, C), 1)
    ltri = jnp.where(r3 >= c3, 1.0, 0.0).astype(bf16)
    r4 = lax.broadcasted_iota(i32, (C, R), 0)
    c4 = lax.broadcasted_iota(i32, (C, R), 1)
    ident_t = r4 == (c4 & (C - 1))
    incl_p = r4 >= (c4 & (C - 1))
    strict_p = r4 > (c4 & (C - 1))
    hblk = c4 >> CHUNK_SHIFT
    ones8 = jnp.ones((8, C), bf16)

    nbs = range(nb_step)
    units = [(nb, g) for nb in nbs for g in range(N_HEADS // G)]
    heads = lambda g: range(g * G, (g + 1) * G)

    qn = [[qkv_ref[nb, :, h * HEAD:(h + 1) * HEAD] for h in range(N_HEADS)] for nb in nbs]
    kn = [[qkv_ref[nb, :, QK_W + h * HEAD:QK_W + (h + 1) * HEAD] for h in range(N_HEADS)] for nb in nbs]
    vv = [qkv_ref[nb, :, 2 * QK_W:3 * QK_W] for nb in nbs]

    bts = [ba_ref[nb, :, 0:LANE] for nb in nbs]
    beta_all = [_sigmoid(bt) for bt in bts]
    g_all = [-(jnp.exp(alog_ref[:, 0:LANE]) * _softplus(bt + dtb_ref[:, 0:LANE])) for bt in bts]
    gc_small = [_dot_rsplit(ltri, ga) for ga in g_all]
    gl_small = [gc[C - 1:C, :] for gc in gc_small]

    k_st, q_st, kb, vb, kbg, qd, kd, gc_col = ({} for _ in range(8))
    for u in units:
        nb, g = u
        hs = heads(g)
        k_st[u] = jnp.concatenate([kn[nb][h] for h in hs], axis=0)
        q_st[u] = jnp.concatenate([qn[nb][h] for h in hs], axis=0)
        v_st = jnp.concatenate([vv[nb][:, h * HEAD:(h + 1) * HEAD] for h in hs], axis=0)
        beta_col = jnp.concatenate([beta_all[nb][:, h:h + 1] for h in hs], axis=0)
        gc_col[u] = jnp.concatenate([gc_small[nb][:, DECAY_LANE + h:DECAY_LANE + h + 1] for h in hs], axis=0)
        gl_col = jnp.concatenate(
            [jnp.broadcast_to(gl_small[nb][:, DECAY_LANE + h:DECAY_LANE + h + 1], (C, 1)) for h in hs], axis=0)
        kb[u] = k_st[u] * beta_col
        vb[u] = v_st * beta_col
        egc = jnp.exp(gc_col[u])
        kbg[u] = kb[u] * egc
        qd[u] = q_st[u] * egc
        kd[u] = k_st[u] * jnp.exp(gl_col - gc_col[u])

    gx = {u: _dot_lsplit(gc_small[u[0]], e64_ref[u[1], 0:LANE, :]) for u in units}
    crow = {u: _dot_rsplit(ones8, jnp.where(ident_t, gx[u], 0.0))[0:1, :] for u in units}
    a = {u: _dot_nt(jnp.concatenate([kb[u], q_st[u]], axis=0).astype(bf16), k_st[u].astype(bf16))
         for u in units}
    in_blk = [hblk == h for h in range(G - 1)]

    def pack(x):
        out = x[(G - 1) * C:G * C]
        for h in reversed(range(G - 1)):
            out = jnp.where(in_blk[h], x[h * C:(h + 1) * C], out)
        return out

    def expand(xp):
        return jnp.concatenate([xp.astype(bf16)] * G, axis=0) * same_bf

    dec = {u: jnp.where(incl_p, jnp.exp(jnp.where(incl_p, gx[u] - crow[u], 0.0)), 0.0) for u in units}
    nm = {u: jnp.where(strict_p, -(pack(a[u][0:R]) * dec[u]), 0.0) for u in units}
    qkm = {u: expand(pack(a[u][R:2 * R]) * dec[u]) for u in units}

    p = {u: jnp.where(ident_t, 1.0, 0.0) + nm[u] for u in units}
    nk = {u: _dot(nm[u].astype(bf16), expand(nm[u])) for u in units}
    for _ in range(4):
        for u in units:
            x = _dot(jnp.concatenate([p[u], nk[u]], axis=0).astype(bf16), expand(nk[u]))
            p[u] = p[u] + x[0:C]
            nk[u] = x[C:2 * C]
    for u in units:
        p[u] = p[u] + _dot(p[u].astype(bf16), expand(nk[u]))
    uw = {u: _dot(expand(p[u]), jnp.concatenate([vb[u], kbg[u]], axis=1).astype(bf16)) for u in units}

    ws = {}
    for u in units:
        nb, g = u
        for j, h in enumerate(heads(g)):
            sh = s_ref[nb, :, h * HEAD:(h + 1) * HEAD]
            lhs = jnp.concatenate([uw[u][j * C:(j + 1) * C, HEAD:2 * HEAD], qd[u][j * C:(j + 1) * C]], axis=0)
            ws[u, j] = _dot(lhs.astype(bf16), sh.astype(bf16))
    o_heads = {}
    for u in units:
        nb, g = u
        vnew_st = jnp.concatenate([uw[u][j * C:(j + 1) * C, 0:HEAD] - ws[u, j][0:C] for j in range(G)], axis=0)
        o_st = (jnp.concatenate([ws[u, j][C:2 * C] for j in range(G)], axis=0)
                + _dot(qkm[u], vnew_st.astype(bf16)))
        vbd = jnp.where(bdmask, jnp.concatenate([vnew_st] * G, axis=1), 0.0)
        lo = g * G * HEAD
        hi = lo + G * HEAD
        gl_row = jnp.concatenate(
            [jnp.broadcast_to(jnp.exp(gl_small[nb][:, DECAY_LANE + h:DECAY_LANE + h + 1]), (1, HEAD)) for h in heads(g)], axis=1)
        s_ref[nb, :, lo:hi] = s_ref[nb, :, lo:hi] * gl_row + _dot(kd[u].T.astype(bf16), vbd.astype(bf16))
        for j, h in enumerate(heads(g)):
            o_heads[nb, h] = o_st[j * C:(j + 1) * C]

    for nb in nbs:
        zt = z_ref[nb]
        for h in range(N_HEADS):
            oh = o_heads[nb, h]
            ms = jnp.mean(oh * oh, axis=-1, keepdims=True)
            zh = zt[:, h * HEAD:(h + 1) * HEAD]
            on = oh * lax.rsqrt(ms + EPS) * onw_ref[...] * _silu(zh)
            o_ref[nb, :, h * HEAD:(h + 1) * HEAD] = on.astype(bf16)

    for nb in nbs:
        bcx = bcx_ref[nb]
        cx = bcx[:, SC_W:2 * SC_W] * bcx[:, 2 * SC_W:3 * SC_W]
        xb_ref[nb, 8:8 + C, :] = cx
        ce = xb_ref[nb]
        cv = pltpu.roll(ce, 2, axis=0)[8:8 + C] * cwb_ref[0:1, :]
        cv = cv + pltpu.roll(ce, 1, axis=0)[8:8 + C] * cwb_ref[1:2, :]
        cv = cv + cx * cwb_ref[2:3, :]
        y_ref[nb] = (bcx[:, 0:SC_W] * cv).astype(bf16)
        xb_ref[nb, 0:8, :] = xb_ref[nb, C:C + 8, :]

    @pl.when(t == nt - 1)
    def _():
        for nb in range(nb_step):
            for h in range(N_HEADS):
                snew_ref[nb, h] = s_ref[nb, :, h * HEAD:(h + 1) * HEAD]
            ncb_ref[nb] = xb_ref[nb, 6:8, :]


def _expand_consts():
    lane = jnp.arange(BA_W)[:, None]
    col = jnp.arange(QK_W)[None, :]
    eb = (lane == (col >> HEAD_SHIFT)).astype(bf16)
    eg = (lane == (DECAY_LANE + (col >> HEAD_SHIFT))).astype(bf16)
    col64 = jnp.arange(GROUP_HEADS * CHUNK)[None, :]
    e64 = jnp.stack([(lane == (DECAY_LANE + g * GROUP_HEADS + (col64 >> CHUNK_SHIFT))).astype(bf16)
                     for g in range(N_HEADS // GROUP_HEADS)], axis=0)
    return eb, eg, e64


def _delta_prompt(proj3, cwb, alog_row, dtb_row, onw_row, nb_step):
    b, t, _ = proj3.shape
    assert t % CHUNK == 0 and b % nb_step == 0
    _, _, e64 = _expand_consts()
    c = CHUNK
    const2 = lambda bi, ti: (0, 0)
    outs = pl.pallas_call(
        functools.partial(_delta_prompt_kernel, nb_step=nb_step),
        out_shape=(jax.ShapeDtypeStruct((b, t, QK_W), bf16),
                   jax.ShapeDtypeStruct((b, t, SC_W), bf16),
                   jax.ShapeDtypeStruct((b, N_HEADS, HEAD, HEAD), f32),
                   jax.ShapeDtypeStruct((b, CONV_B - 1, SC_W), f32)),
        grid=(b // nb_step, t // c),
        in_specs=[pl.BlockSpec((nb_step, c, QKV_W), lambda bi, ti: (bi, ti, COL_QKV // QKV_W)),
                  pl.BlockSpec((nb_step, c, QKV_W), lambda bi, ti: (bi, ti, COL_BCX // QKV_W)),
                  pl.BlockSpec((nb_step, c, QK_W), lambda bi, ti: (bi, ti, COL_Z // QK_W)),
                  pl.BlockSpec((nb_step, c, BA_W), lambda bi, ti: (bi, ti, COL_BA // BA_W)),
                  pl.BlockSpec((CONV_B, SC_W), const2),
                  pl.BlockSpec((1, BA_W), const2),
                  pl.BlockSpec((1, BA_W), const2),
                  pl.BlockSpec((1, HEAD), const2),
                  pl.BlockSpec((N_HEADS // GROUP_HEADS, BA_W, GROUP_HEADS * CHUNK), lambda bi, ti: (0, 0, 0))],
        out_specs=(pl.BlockSpec((nb_step, c, QK_W), lambda bi, ti: (bi, ti, 0)),
                   pl.BlockSpec((nb_step, c, SC_W), lambda bi, ti: (bi, ti, 0)),
                   pl.BlockSpec((nb_step, N_HEADS, HEAD, HEAD), lambda bi, ti: (bi, 0, 0, 0)),
                   pl.BlockSpec((nb_step, CONV_B - 1, SC_W), lambda bi, ti: (bi, 0, 0))),
        scratch_shapes=[pltpu.VMEM((nb_step, HEAD, QK_W), f32),
                        pltpu.VMEM((nb_step, 8 + c, SC_W), f32)],
        compiler_params=_cparams(("arbitrary", "arbitrary")),
        name="delta_prompt",
    )(proj3, proj3, proj3, proj3, cwb, alog_row, dtb_row, onw_row, e64)
    return outs


def _sample_prep_kernel(p_ref, bufa_ref, bufb_ref, cwa_ref, cwb_ref, alog_ref, dtb_ref, eb_ref, eg_ref,
                        q_ref, k_ref, v_ref, beta_ref, eg_out_ref, z_ref, y_ref, nbufa_ref, nbufb_ref):
    def put_heads(ref, a):
        for h in range(N_HEADS):
            ref[:, h, :] = a[:, h * HEAD:(h + 1) * HEAD]

    def conv_sec(lo):
        hi = lo + QK_W
        raw = p_ref[:, COL_QKV + lo:COL_QKV + hi]
        acc = bufa_ref[0, :, lo:hi] * cwa_ref[0:1, lo:hi]
        acc = acc + bufa_ref[1, :, lo:hi] * cwa_ref[1:2, lo:hi]
        acc = acc + bufa_ref[2, :, lo:hi] * cwa_ref[2:3, lo:hi]
        acc = acc + raw * cwa_ref[3:4, lo:hi]
        nbufa_ref[0, :, lo:hi] = bufa_ref[1, :, lo:hi]
        nbufa_ref[1, :, lo:hi] = bufa_ref[2, :, lo:hi]
        nbufa_ref[2, :, lo:hi] = raw
        return _silu(acc)

    qn = _head_l2norm(conv_sec(0), HEAD ** -0.5)
    kn = _head_l2norm(conv_sec(QK_W), 1.0)
    for h in range(N_HEADS):
        q_ref[:, h, :] = qn[h]
        k_ref[:, h, :] = kn[h]
    put_heads(v_ref, conv_sec(2 * QK_W))
    put_heads(z_ref, p_ref[:, COL_Z:COL_Z + QK_W])

    bt = p_ref[:, COL_BA:COL_BA + BA_W]
    beta_all = _sigmoid(bt)
    g_all = -(jnp.exp(alog_ref[...]) * _softplus(bt + dtb_ref[...]))
    put_heads(beta_ref, _dot_lsplit(beta_all, eb_ref[...]))
    put_heads(eg_out_ref, jnp.exp(_dot_lsplit(g_all, eg_ref[...])))

    bg = p_ref[:, COL_BCX:COL_BCX + SC_W]
    cx = p_ref[:, COL_BCX + SC_W:COL_BCX + 2 * SC_W] * p_ref[:, COL_BCX + 2 * SC_W:COL_BCX + 3 * SC_W]
    cv = bufb_ref[0] * cwb_ref[0:1, :]
    cv = cv + bufb_ref[1] * cwb_ref[1:2, :]
    cv = cv + cx * cwb_ref[2:3, :]
    y_ref[...] = (bg * cv).astype(bf16)
    nbufb_ref[0] = bufb_ref[1]
    nbufb_ref[1] = cx


def _sample_prep(proj_s, bufa_t, bufb_t, cwa, cwb, alog_row, dtb_row):
    n = proj_s.shape[0]
    eb, eg, _ = _expand_consts()
    row = jax.ShapeDtypeStruct((n, N_HEADS, HEAD), f32)
    return pl.pallas_call(
        _sample_prep_kernel,
        out_shape=(row, row, row, row, row, row,
                   jax.ShapeDtypeStruct((n, SC_W), bf16),
                   jax.ShapeDtypeStruct((CONV_A - 1, n, QKV_W), f32),
                   jax.ShapeDtypeStruct((CONV_B - 1, n, SC_W), f32)),
        compiler_params=pltpu.CompilerParams(vmem_limit_bytes=VMEM_LIMIT),
        name="sample_prep",
    )(proj_s, bufa_t, bufb_t, cwa, cwb, alog_row, dtb_row, eb, eg)


def _sample_step_kernel(s_ref, q_ref, k_ref, v_ref, beta_ref, eg_ref, z_ref, onw_ref,
                        snew_ref, o_ref, *, bb):
    w = N_HEADS * HEAD
    r8 = lax.broadcasted_iota(i32, (N_HEADS, w), 0)
    c8 = lax.broadcasted_iota(i32, (N_HEADS, w), 1)
    mask8 = r8 == (c8 >> HEAD_SHIFT)
    zpad_k = jnp.zeros((HEAD - N_HEADS, HEAD), f32)
    hb = lambda h: slice(h * HEAD, (h + 1) * HEAD)
    bs = range(bb)
    s_dec, k8s, kts = [], [], []
    for b in bs:
        s_all = jnp.concatenate([s_ref[b, h] for h in range(N_HEADS)], axis=1)
        eg8 = eg_ref[b]
        eg_row = jnp.concatenate([eg8[h:h + 1, :] for h in range(N_HEADS)], axis=1)
        s_dec.append(s_all * eg_row)
        k8s.append(k_ref[b])
        kts.append(jnp.concatenate([k8s[b], zpad_k], axis=0).T)
    xs = [_dot(k8s[b].astype(bf16), s_dec[b].astype(bf16)) for b in bs]
    s_new = []
    for b in bs:
        vb8, bt8 = v_ref[b], beta_ref[b]
        upd = [kts[b][:, h:h + 1] * ((vb8[h:h + 1, :] - xs[b][h:h + 1, hb(h)]) * bt8[h:h + 1, :])
               for h in range(N_HEADS)]
        s_new.append(s_dec[b] + jnp.concatenate(upd, axis=1))
    ys = [_dot(q_ref[b].astype(bf16), s_new[b].astype(bf16)) for b in bs]
    for b in bs:
        yv = jnp.where(mask8, ys[b], 0.0)
        o8 = yv[:, 0:HEAD]
        for j in range(1, N_HEADS):
            o8 = o8 + yv[:, j * HEAD:(j + 1) * HEAD]
        ms = jnp.mean(o8 * o8, axis=-1, keepdims=True)
        o_ref[b] = o8 * lax.rsqrt(ms + EPS) * onw_ref[...] * _silu(z_ref[b])
        for h in range(N_HEADS):
            snew_ref[b, h] = s_new[b][:, hb(h)]


def _sample_step(state, q, k, v, beta, eg, z, onw_row, bb=8):
    n = state.shape[0]
    assert n % bb == 0
    hspec = pl.BlockSpec((bb, N_HEADS, HEAD), lambda i: (i, 0, 0))
    sspec = pl.BlockSpec((bb, N_HEADS, HEAD, HEAD), lambda i: (i, 0, 0, 0))
    return pl.pallas_call(
        functools.partial(_sample_step_kernel, bb=bb),
        out_shape=(jax.ShapeDtypeStruct(state.shape, f32),
                   jax.ShapeDtypeStruct((n, N_HEADS, HEAD), f32)),
        grid=(n // bb,),
        in_specs=[sspec, hspec, hspec, hspec, hspec, hspec, hspec, pl.BlockSpec((1, HEAD), lambda i: (0, 0))],
        out_specs=(sspec, hspec),
        compiler_params=_cparams(("arbitrary",)),
        name="sample_step",
    )(state, q, k, v, beta, eg, z, onw_row)


def _mix_route_kernel(x_ref, o_ref, y_ref, ga_ref, gb_ref, wa_ref, wb_ref, wo_ref, n2_ref,
                      rwh_ref, rwl_ref, rb_ref, cnt_in_ref, x1_ref, h2_ref, mi_ref, mw_ref, cnt_ref):
    i = pl.program_id(0)
    tm = x_ref.shape[0]

    @pl.when(i == 0)
    def _():
        cnt_ref[...] = cnt_in_ref[...]

    oa = _dot(o_ref[...], wa_ref[...])
    ob = _dot(y_ref[...], wb_ref[...])
    merged = _sigmoid(ga_ref[...]) * oa + _sigmoid(gb_ref[...]) * ob
    x1 = x_ref[...] + _dot(merged.astype(bf16), wo_ref[...])
    x1_ref[...] = x1
    ms = jnp.mean(x1 * x1, axis=-1, keepdims=True)
    h2 = x1 * lax.rsqrt(ms + EPS) * n2_ref[...]
    h2_ref[...] = h2

    h_hi, h_lo = _split(h2, 2)
    logits = _dot(h_hi, rwh_ref[...]) + _dot(h_hi, rwl_ref[...]) + _dot(h_lo, rwh_ref[...]) + rb_ref[...]

    lane = lax.broadcasted_iota(i32, (tm, LANE), 1)
    lanef = lane.astype(f32)
    neg = jnp.float32(-jnp.inf)
    big = jnp.float32(1e9)
    gmask = (lane >= N_EXPERTS) & (lane < N_EXPERTS + N_GROUPS)
    gl = jnp.where(gmask, logits, neg)
    gmax = jnp.max(gl, axis=-1, keepdims=True)
    gidx = jnp.min(jnp.where(gl == gmax, lanef - N_EXPERTS, big), axis=-1, keepdims=True)
    gsum = jnp.sum(jnp.where(gmask, jnp.exp(gl - gmax), 0.0), axis=-1, keepdims=True)
    gprob = 1.0 / gsum

    emask = (lane < N_EXPERTS) & ((lane >> GROUP_SHIFT).astype(f32) == gidx)
    el = jnp.where(emask, logits, neg)
    emax = jnp.max(el, axis=-1, keepdims=True)
    pe = jnp.where(emask, jnp.exp(el - emax), 0.0)
    eprob = pe / jnp.sum(pe, axis=-1, keepdims=True)
    p1m = jnp.where(emask, eprob, -1.0)
    m1 = jnp.max(p1m, axis=-1, keepdims=True)
    i1 = jnp.min(jnp.where(p1m == m1, lanef, big), axis=-1, keepdims=True)
    p2m = jnp.where(lanef == i1, -1.0, p1m)
    m2 = jnp.max(p2m, axis=-1, keepdims=True)
    i2 = jnp.min(jnp.where(p2m == m2, lanef, big), axis=-1, keepdims=True)
    tot = m1 + m2
    c1 = m1 / tot * gprob
    c2 = m2 / tot * gprob

    oh1 = jnp.where(lanef == i1, 1.0, 0.0)
    oh2 = jnp.where(lanef == i2, 1.0, 0.0)
    ohs = oh1 + oh2
    rt = lax.broadcasted_iota(i32, (tm, tm), 0)
    ct = lax.broadcasted_iota(i32, (tm, tm), 1)
    lstrict = jnp.where(rt > ct, 1.0, 0.0).astype(bf16)
    cs = _dot(lstrict, ohs.astype(bf16)) + cnt_ref[...]
    rank1 = jnp.sum(cs * oh1, axis=-1, keepdims=True)
    rank2 = jnp.sum(cs * oh2, axis=-1, keepdims=True)
    cnt_ref[...] = cnt_ref[...] + jnp.sum(ohs, axis=0, keepdims=True)

    mi = jnp.where(lane == 0, i1, jnp.where(lane == 1, i2, jnp.where(lane == 2, rank1,
                                                                     jnp.where(lane == 3, rank2, 0.0))))
    mi_ref[...] = mi.astype(i32)
    mw_ref[...] = jnp.where(lane == 0, c1, jnp.where(lane == 1, c2, 0.0))


def _mix_route(x2d, o2d, y2d, proj2d, wa, wb, wo, n2_row, rwh, rwl, rb_row, cnt_in):
    n = x2d.shape[0]
    tm = min(256, n)
    assert n % tm == 0
    tok = lambda width: pl.BlockSpec((tm, width), lambda i: (i, 0))
    full = lambda a: pl.BlockSpec(a.shape, lambda i: (0,) * a.ndim)
    in_specs = [tok(D_MODEL), tok(QK_W), tok(SC_W),
                pl.BlockSpec((tm, D_MODEL), lambda i: (i, COL_GA // D_MODEL)),
                pl.BlockSpec((tm, D_MODEL), lambda i: (i, COL_GB // D_MODEL)),
                full(wa), full(wb), full(wo), full(n2_row), full(rwh), full(rwl), full(rb_row), full(cnt_in)]
    out_shape = (jax.ShapeDtypeStruct((n, D_MODEL), f32),
                 jax.ShapeDtypeStruct((n, D_MODEL), f32),
                 jax.ShapeDtypeStruct((n, LANE), i32),
                 jax.ShapeDtypeStruct((n, LANE), f32),
                 jax.ShapeDtypeStruct((1, LANE), f32))
    out_specs = (tok(D_MODEL), tok(D_MODEL), tok(LANE), tok(LANE),
                 pl.BlockSpec((1, LANE), lambda i: (0, 0)))
    return pl.pallas_call(
        _mix_route_kernel,
        out_shape=out_shape,
        grid=(n // tm,),
        in_specs=in_specs,
        out_specs=out_specs,
        compiler_params=_cparams(("arbitrary",)),
        name="mix_route",
    )(x2d, o2d, y2d, proj2d, proj2d, wa, wb, wo, n2_row, rwh, rwl, rb_row, cnt_in)


MI_W = 4
SUBLANE = 8


def _dest_kernel(mi_ref, starts_ref, o_ref):
    mi = mi_ref[...]
    lane = lax.broadcasted_iota(i32, mi.shape, 1)
    st = starts_ref[...]

    def first_row(e_col):
        return jnp.sum(jnp.where(lane == e_col, st, 0.0), axis=-1, keepdims=True).astype(i32)

    d0 = first_row(mi[:, 0:1]) + mi[:, 2:3]
    d1 = first_row(mi[:, 1:2]) + mi[:, 3:4]
    sh = SUBLANE.bit_length() - 1
    o_ref[...] = jnp.where(lane == 0, d0 >> sh, jnp.where(lane == 1, d0 & (SUBLANE - 1),
                           jnp.where(lane == 2, d1 >> sh, jnp.where(lane == 3, d1 & (SUBLANE - 1), 0))))


def _dest_rows(mi, starts_row):
    n = mi.shape[0]
    tm = min(1024, n)
    assert n % tm == 0
    return pl.pallas_call(
        _dest_kernel,
        out_shape=jax.ShapeDtypeStruct((n, LANE), i32),
        grid=(n // tm,),
        in_specs=[pl.BlockSpec((tm, LANE), lambda i: (i, 0)), pl.BlockSpec((1, LANE), lambda i: (0, 0))],
        out_specs=pl.BlockSpec((tm, LANE), lambda i: (i, 0)),
        compiler_params=_cparams(("arbitrary",)),
        name="moe_dest",
    )(mi, starts_row)


def _dispatch_kernel(mi_ref, hp_ref, hs_ref, xs_ref, sem, *, np_tiles):
    i = pl.program_id(0)

    def scatter_rows(h_ref):
        n_tiles = h_ref.shape[0]

        def start(t, c):
            for u in range(SUBLANE):
                rec = MI_W * (SUBLANE * t + u)
                for k in range(2):
                    dst = xs_ref.at[mi_ref[rec + 2 * k], pl.ds(mi_ref[rec + 2 * k + 1], 1)]
                    pltpu.make_async_copy(h_ref.at[t, pl.ds(u, 1)], dst, sem).start(priority=k)
            return c

        lax.fori_loop(0, n_tiles, start, 0)
        for k in range(2):
            pltpu.make_async_copy(h_ref, xs_ref.at[pl.ds(0, n_tiles)], sem).wait()

    @pl.when(i < np_tiles)
    def _():
        scatter_rows(hp_ref)

    @pl.when(i >= np_tiles)
    def _():
        scatter_rows(hs_ref)


def _dispatch(h2_p, h2_s, mi_flat):
    tm = TOKEN_TILE
    n_p, n_s = h2_p.shape[0], h2_s.shape[0]
    assert n_p % tm == 0 and n_s <= tm and n_s % SUBLANE == 0 and tm % SUBLANE == 0
    np_tiles = n_p // tm
    tiled = lambda a: a.reshape(a.shape[0] // SUBLANE, SUBLANE, D_MODEL)
    return pl.pallas_call(
        functools.partial(_dispatch_kernel, np_tiles=np_tiles),
        out_shape=jax.ShapeDtypeStruct((2 * (n_p + n_s) // SUBLANE, SUBLANE, D_MODEL), f32),
        grid=(np_tiles + 1,),
        in_specs=[pl.BlockSpec((MI_W * tm,), lambda i: (i,), memory_space=pltpu.SMEM),
                  pl.BlockSpec((tm // SUBLANE, SUBLANE, D_MODEL), lambda i: (jnp.minimum(i, np_tiles - 1), 0, 0)),
                  pl.BlockSpec((n_s // SUBLANE, SUBLANE, D_MODEL), lambda i: (0, 0, 0))],
        out_specs=pl.BlockSpec(memory_space=pl.ANY),
        scratch_shapes=[pltpu.SemaphoreType.DMA(())],
        compiler_params=_cparams(("arbitrary",)),
        name="moe_dispatch",
    )(mi_flat, tiled(h2_p), tiled(h2_s))


def _cast_rows(src_ref, dst_ref, col0=0, rows=256):
    width = src_ref.shape[1]

    def body(r, c):
        sl = pl.ds(pl.multiple_of(r * rows, rows), rows)
        dst_ref[sl, col0:col0 + width] = src_ref[sl, :].astype(bf16)
        return c
    lax.fori_loop(0, src_ref.shape[0] // rows, body, 0)


def _moe_kernel(blk_ref, lo_ref, hi_ref, first_ref, newe_ref, slot_ref, pre_ref, init_ref,
                x_ref, wg_hbm, wu_hbm, wd_hbm, o_ref,
                wg_f, wu_f, wd_f, wgu_b, wd_b, sem):
    i = pl.program_id(0)
    lo = lo_ref[i]
    hi = hi_ref[i]

    def weight_copies(e, slot):
        return [pltpu.make_async_copy(wg_hbm.at[e], wg_f.at[slot], sem.at[slot, 0]),
                pltpu.make_async_copy(wu_hbm.at[e], wu_f.at[slot], sem.at[slot, 1]),
                pltpu.make_async_copy(wd_hbm.at[e], wd_f.at[slot], sem.at[slot, 2])]

    def start_weights(e, slot):
        for cp, prio in zip(weight_copies(e, slot), (0, 1, 1)):
            cp.start(priority=prio)

    @pl.when(i == 0)
    def _():
        start_weights(init_ref[0], 0)
        for k in range(1, W_SLOTS):
            @pl.when(init_ref[k] >= 0)
            def _():
                start_weights(init_ref[k], k)

    @pl.when(newe_ref[i] == 1)
    def _():
        slot = slot_ref[i]
        cg, cu, cd = weight_copies(0, slot)
        cg.wait()
        _cast_rows(wg_f.at[slot], wgu_b, 0)
        cu.wait()
        _cast_rows(wu_f.at[slot], wgu_b, D_FF)
        cd.wait()
        _cast_rows(wd_f.at[slot], wd_b)

        @pl.when(pre_ref[i] >= 0)
        def _():
            start_weights(pre_ref[i], slot)

    @pl.when(hi > lo)
    def _():
        x = x_ref[...].astype(bf16)
        au = _dot(x, wgu_b[...])
        y = _dot((_silu(au[:, 0:D_FF]) * au[:, D_FF:2 * D_FF]).astype(bf16), wd_b[...])
        whole = (lo == 0) & (hi == MOE_ROWS)

        def masked():
            row = lax.broadcasted_iota(i32, y.shape, 0)
            return jnp.where((row >= lo) & (row < hi), y, 0.0)

        @pl.when(whole)
        def _():
            o_ref[...] = y

        @pl.when(jnp.logical_not(whole) & (first_ref[i] == 1))
        def _():
            o_ref[...] = masked()

        @pl.when(jnp.logical_not(whole) & (first_ref[i] == 0))
        def _():
            o_ref[...] = o_ref[...] + masked()


def _moe(xs, w_gate, w_up, w_down, items):
    n_items = items[0].shape[0]
    rows = xs.shape[0]
    n_pref = len(items)
    xmap = lambda i, blk, *_: (blk[i], 0)
    grid_spec = pltpu.PrefetchScalarGridSpec(
        num_scalar_prefetch=n_pref,
        grid=(n_items,),
        in_specs=[pl.BlockSpec((MOE_ROWS, D_MODEL), xmap),
                  pl.BlockSpec(memory_space=pl.ANY),
                  pl.BlockSpec(memory_space=pl.ANY),
                  pl.BlockSpec(memory_space=pl.ANY)],
        out_specs=pl.BlockSpec((MOE_ROWS, D_MODEL), xmap),
        scratch_shapes=[pltpu.VMEM((W_SLOTS, D_MODEL, D_FF), f32), pltpu.VMEM((W_SLOTS, D_MODEL, D_FF), f32),
                        pltpu.VMEM((W_SLOTS, D_FF, D_MODEL), f32),
                        pltpu.VMEM((D_MODEL, 2 * D_FF), bf16), pltpu.VMEM((D_FF, D_MODEL), bf16),
                        pltpu.SemaphoreType.DMA((W_SLOTS, 3))],
    )
    return pl.pallas_call(
        _moe_kernel,
        out_shape=jax.ShapeDtypeStruct((rows, D_MODEL), f32),
        grid_spec=grid_spec,
        compiler_params=_cparams(("arbitrary",)),
        name="moe_experts",
    )(*items, xs, w_gate, w_up, w_down)


def _combine_kernel(mi_ref, mi_next_ref, x1p_ref, mwp_ref, x1s_ref, mws_ref, fnw_ref, ys_ref,
                    yp_ref, ysm_ref, g_ref, sem, *, np_tiles):
    i = pl.program_id(0)
    tiles_p = x1p_ref.shape[0] // SUBLANE
    tiles_s = x1s_ref.shape[0] // SUBLANE
    slot = lax.rem(i, 2)

    def gather_rows(m_ref, dst_slot, n_tiles):
        def body(t, c):
            for u in range(SUBLANE):
                rec = MI_W * (SUBLANE * t + u)
                for k in range(2):
                    src = ys_ref.at[m_ref[rec + 2 * k], pl.ds(m_ref[rec + 2 * k + 1], 1)]
                    pltpu.make_async_copy(src, g_ref.at[dst_slot, k, t, pl.ds(u, 1)],
                                          sem.at[dst_slot]).start(priority=k)
            return c
        lax.fori_loop(0, n_tiles, body, 0)

    @pl.when(i == 0)
    def _():
        gather_rows(mi_ref, 0, tiles_p)

    @pl.when(i + 1 < np_tiles)
    def _():
        gather_rows(mi_next_ref, 1 - slot, tiles_p)

    @pl.when(i + 1 == np_tiles)
    def _():
        gather_rows(mi_next_ref, 1 - slot, tiles_s)

    def finish(x1_ref, mw_ref, out_ref, n_tiles):
        for k in range(2):
            pltpu.make_async_copy(ys_ref.at[pl.ds(0, n_tiles)], g_ref.at[slot, k, pl.ds(0, n_tiles)],
                                  sem.at[slot]).wait()
        rows = n_tiles * SUBLANE
        mw = mw_ref[...]
        g0 = g_ref[slot, 0, 0:n_tiles].reshape(rows, D_MODEL)
        g1 = g_ref[slot, 1, 0:n_tiles].reshape(rows, D_MODEL)
        x2 = x1_ref[...] + (g0 * mw[:, 0:1] + g1 * mw[:, 1:2])
        ms = jnp.mean(x2 * x2, axis=-1, keepdims=True)
        out_ref[...] = x2 * lax.rsqrt(ms + EPS) * fnw_ref[...]

    @pl.when(i < np_tiles)
    def _():
        finish(x1p_ref, mwp_ref, yp_ref, tiles_p)

    @pl.when(i >= np_tiles)
    def _():
        finish(x1s_ref, mws_ref, ysm_ref, tiles_s)


def _combine(x1_p, mw_p, x1_s, mw_s, fnw_row, ys3, mi_flat):
    tm = TOKEN_TILE
    n_p, n_s = x1_p.shape[0], x1_s.shape[0]
    assert n_p % tm == 0 and n_s <= tm and n_s % SUBLANE == 0
    np_tiles = n_p // tm
    ptile = lambda width: pl.BlockSpec((tm, width), lambda i: (jnp.minimum(i, np_tiles - 1), 0))
    stile = lambda width: pl.BlockSpec((n_s, width), lambda i: (0, 0))
    return pl.pallas_call(
        functools.partial(_combine_kernel, np_tiles=np_tiles),
        out_shape=(jax.ShapeDtypeStruct((n_p, D_MODEL), f32),
                   jax.ShapeDtypeStruct((n_s, D_MODEL), f32)),
        grid=(np_tiles + 1,),
        in_specs=[pl.BlockSpec((MI_W * tm,), lambda i: (i,), memory_space=pltpu.SMEM),
                  pl.BlockSpec((MI_W * tm,), lambda i: (jnp.minimum(i + 1, np_tiles),), memory_space=pltpu.SMEM),
                  ptile(D_MODEL), ptile(LANE), stile(D_MODEL), stile(LANE),
                  pl.BlockSpec((1, D_MODEL), lambda i: (0, 0)),
                  pl.BlockSpec(memory_space=pl.ANY)],
        out_specs=(ptile(D_MODEL), stile(D_MODEL)),
        scratch_shapes=[pltpu.VMEM((2, 2, tm // SUBLANE, SUBLANE, D_MODEL), f32), pltpu.SemaphoreType.DMA((2,))],
        compiler_params=_cparams(("arbitrary",)),
        name="moe_combine",
    )(mi_flat, mi_flat, x1_p, mw_p, x1_s, mw_s, fnw_row, ys3)


PLAN_ROWS = 256
N_ITEM_FIELDS = 7


def _plan_kernel(cnt_ref, items_ref, rows_ref, *, nblk):
    cnt = cnt_ref[...]
    lane1 = lax.broadcasted_iota(i32, (1, LANE), 1)
    in_e = lane1 < N_EXPERTS
    ri = lax.broadcasted_iota(i32, (LANE, LANE), 0)
    ci = lax.broadcasted_iota(i32, (LANE, LANE), 1)
    upper = jnp.where(ri <= ci, 1.0, 0.0).astype(bf16)

    def cumsum_lanes(v):
        return _dot_lsplit(jnp.broadcast_to(v, (8, LANE)), upper)[0:1, :]

    shift = MOE_ROWS.bit_length() - 1
    ends = cumsum_lanes(cnt)
    starts = ends - cnt
    act = cnt > 0.0
    first_blk = (starts.astype(i32) >> shift).astype(f32)
    last_blk = (jnp.maximum(ends - 1.0, 0.0).astype(i32) >> shift).astype(f32)
    nvis = jnp.where(act, last_blk - first_blk + 1.0, 0.0)
    vis_end = cumsum_lanes(nvis)
    vis_start = vis_end - nvis
    total = jnp.max(vis_end, axis=-1, keepdims=True)
    cum_act = cumsum_lanes(jnp.where(act, 1.0, 0.0))
    n_uniq = jnp.max(cum_act, axis=-1, keepdims=True)

    p = PLAN_ROWS
    lane = lax.broadcasted_iota(i32, (p, LANE), 1)
    idx = lax.broadcasted_iota(i32, (p, LANE), 0).astype(f32)
    idx1 = idx[:, 0:1]
    count_le = lambda row, col: jnp.sum(jnp.where((row <= col) & in_e, 1.0, 0.0), axis=-1, keepdims=True)
    e = jnp.minimum(count_le(vis_end, idx), N_EXPERTS - 1.0)
    onehot = lane.astype(f32) == e
    look = lambda tbl: jnp.sum(jnp.where(onehot, tbl, 0.0), axis=-1, keepdims=True)
    blk = look(first_blk) + idx1 - look(vis_start)
    lo = jnp.maximum(look(starts), blk * MOE_ROWS) - blk * MOE_ROWS
    hi = jnp.minimum(look(ends), (blk + 1.0) * MOE_ROWS) - blk * MOE_ROWS
    valid = idx1 < total
    blk = jnp.where(valid, blk, nblk - 1.0)
    lo = jnp.where(valid, lo, 0.0)
    hi = jnp.where(valid, hi, 0.0)
    rep = lambda c: jnp.broadcast_to(c, (p, LANE))
    prev = lambda c: pltpu.roll(rep(c), 1, axis=0)[:, 0:1]
    is0 = idx1 == 0.0
    first = valid & (is0 | (blk != prev(blk)))
    newe = valid & (is0 | (e != prev(e)))
    rp = lax.broadcasted_iota(i32, (p, p), 0)
    cp = lax.broadcasted_iota(i32, (p, p), 1)
    lower = jnp.where(rp >= cp, 1.0, 0.0).astype(bf16)
    order = _dot(lower, rep(jnp.where(newe, 1.0, 0.0)).astype(bf16))[:, 0:1] - 1.0
    slot = jnp.where(newe, order - W_SLOTS * jnp.floor((order + 0.5) * (1.0 / W_SLOTS)), 0.0)
    k2 = order + float(W_SLOTS)
    pre = jnp.where(newe & (k2 < n_uniq), count_le(cum_act, rep(k2)), -1.0)
    out = jnp.zeros((p, LANE), f32)
    for c, v in enumerate([blk, lo, hi, jnp.where(first, 1.0, 0.0), jnp.where(newe, 1.0, 0.0), slot, pre]):
        out = jnp.where(lane == c, v, out)
    items_ref[...] = out.astype(i32)

    init_row = jnp.zeros((1, LANE), f32)
    for k in range(W_SLOTS):
        init_row = jnp.where(lane1 == k, jnp.where(n_uniq > float(k), count_le(cum_act, float(k)), -1.0), init_row)
    rows_ref[...] = jnp.zeros(rows_ref.shape, f32)
    rows_ref[0:1, :] = starts
    rows_ref[1:2, :] = init_row


def _work_items(cnt_row, n_rows):
    nblk = n_rows // MOE_ROWS
    n_items = nblk + N_EXPERTS - 1
    assert n_items <= PLAN_ROWS and n_rows % MOE_ROWS == 0
    items, rows = pl.pallas_call(
        functools.partial(_plan_kernel, nblk=nblk),
        out_shape=(jax.ShapeDtypeStruct((PLAN_ROWS, LANE), i32), jax.ShapeDtypeStruct((8, LANE), f32)),
        compiler_params=pltpu.CompilerParams(vmem_limit_bytes=VMEM_LIMIT),
        name="moe_plan",
    )(cnt_row)
    fields = tuple(items[0:n_items, c] for c in range(N_ITEM_FIELDS))
    return rows[0:1, :], fields + (rows[1, 0:W_SLOTS].astype(i32),)


def kernel(x_prompt, x_sample, state_delta, state_qkv_conv, state_short_conv, norm1_w, w_in, conv_a_w, a_log, dt_bias, out_norm_w, w_branch_a, conv_b_w, w_branch_b, w_o, norm2_w, router_group_w, router_group_b, router_expert_w, router_expert_b, w_gate, w_up, w_down, final_norm_w):
    assert norm1_w.shape[0] == 1, "single-layer trunk"
    bp, tp, d = x_prompt.shape
    bs, ts, _ = x_sample.shape
    assert d == D_MODEL and ts == 1
    n_p = bp * tp
    n_s = bs
    n_all = n_p + n_s

    w_perm = _wprep(jnp.transpose(w_in[0]))
    wa = w_branch_a[0].astype(bf16)
    wb = w_branch_b[0].astype(bf16)
    wo = w_o[0].astype(bf16)
    pad = lambda v: jnp.zeros((1, BA_W), f32).at[0, DECAY_LANE:DECAY_LANE + N_HEADS].set(v)
    alog_row = pad(a_log[0])
    dtb_row = pad(dt_bias[0])
    onw_row = out_norm_w[0].reshape(1, HEAD)
    cwa = conv_a_w[0]
    cwb = conv_b_w[0]
    r_pad = LANE - N_EXPERTS - N_GROUPS
    rw = jnp.concatenate([router_expert_w[0], router_group_w[0], jnp.zeros((D_MODEL, r_pad), f32)], axis=1)
    rwh = rw.astype(bf16)
    rwl = (rw - rwh.astype(f32)).astype(bf16)
    rb_row = jnp.concatenate([router_expert_b[0], router_group_b[0], jnp.zeros((r_pad,), f32)]).reshape(1, LANE)
    n2_row = norm2_w[0].reshape(1, D_MODEL)

    xp2 = x_prompt.reshape(n_p, D_MODEL)
    proj_p, tails = _inproj_conv(xp2, norm1_w[0], w_perm, cwa, tp)
    tiles_per_seq = tails.shape[0] // bp
    nca_p = tails.reshape(bp, tiles_per_seq, 8, CONV_PAD_W)[:, -1, 8 - (CONV_A - 1):8, 0:QKV_W]
    o_p, y_p, sd_p, ncb_p = _delta_prompt(proj_p.reshape(bp, tp, PROJ_W), cwb, alog_row, dtb_row,
                                          onw_row, nb_step=4 if bp % 4 == 0 else (2 if bp % 2 == 0 else 1))
    cnt0 = jnp.zeros((1, LANE), f32)
    x1_p, h2_p, mi_p, mw_p, cnt_p = _mix_route(xp2, o_p.reshape(n_p, QK_W), y_p.reshape(n_p, SC_W), proj_p,
                                               wa, wb, wo, n2_row, rwh, rwl, rb_row, cnt0)

    xs2 = x_sample.reshape(n_s, D_MODEL)
    proj_s = _inproj(xs2, norm1_w[0], w_perm)
    bufa_t = jnp.transpose(state_qkv_conv[0], (1, 0, 2))
    bufb_t = jnp.transpose(state_short_conv[0], (1, 0, 2))
    q_s, k_s, v_s, beta_s, eg_s, z_s, y_s, nbufa_t, nbufb_t = _sample_prep(proj_s, bufa_t, bufb_t, cwa, cwb,
                                                                           alog_row, dtb_row)
    sd_s, o_s = _sample_step(state_delta[0], q_s, k_s, v_s, beta_s, eg_s, z_s, onw_row)
    o_s2 = o_s.reshape(n_s, QK_W).astype(bf16)
    x1_s, h2_s, mi_s, mw_s, cnt = _mix_route(xs2, o_s2, y_s, proj_s, wa, wb, wo, n2_row, rwh, rwl, rb_row, cnt_p)

    starts_row, items = _work_items(cnt, 2 * n_all)
    mi_flat = jnp.concatenate([_dest_rows(mi_p, starts_row)[:, 0:MI_W], _dest_rows(mi_s, starts_row)[:, 0:MI_W]],
                              axis=0).reshape(MI_W * n_all)
    xs_sorted = _dispatch(h2_p, h2_s, mi_flat)
    ys = _moe(xs_sorted.reshape(2 * n_all, D_MODEL), w_gate[0], w_up[0], w_down[0], items)
    y_prompt, y_sample = _combine(x1_p, mw_p, x1_s, mw_s, final_norm_w.reshape(1, D_MODEL),
                                  ys.reshape(2 * n_all // SUBLANE, SUBLANE, D_MODEL), mi_flat)

    return (y_prompt.reshape(bp, tp, D_MODEL),
            y_sample.reshape(bs, ts, D_MODEL),
            sd_p[None],
            nca_p[None],
            ncb_p[None],
            sd_s[None],
            jnp.transpose(nbufa_t, (1, 0, 2))[None],
            jnp.transpose(nbufb_t, (1, 0, 2))[None])
```

```python
import functools

import jax
import jax.numpy as jnp
from jax import lax
from jax.experimental import pallas as pl
from jax.experimental.pallas import tpu as pltpu

f32 = jnp.float32
bf16 = jnp.bfloat16
i32 = jnp.int32

EPS = 1e-6
LANE = 128
D_MODEL = 2048
N_HEADS = 8
HEAD = 128
QK_W = N_HEADS * HEAD
QKV_W = 3 * QK_W
SC_W = 1024
CONV_A = 4
CONV_B = 3
CHUNK = 64
CHUNK_SHIFT = CHUNK.bit_length() - 1
HEAD_SHIFT = HEAD.bit_length() - 1
GROUP_HEADS = 4
DECAY_LANE = N_HEADS
N_EXPERTS = 64
N_GROUPS = 8
EXPERTS_PER_GROUP = 8
GROUP_SHIFT = EXPERTS_PER_GROUP.bit_length() - 1
D_FF = 512
MOE_ROWS = 128
TOKEN_TILE = 512
W_SLOTS = 2
X_SLOTS = 3

COL_QKV = 0
COL_BCX = 3072
COL_GA = 6144
COL_GB = 8192
COL_Z = 10240
COL_BA = 11264
BA_W = 256
PROJ_W = 11520
PROJ_TN = 1280

VMEM_LIMIT = 56 * 1024 * 1024


def _dot(a, b):
    return jnp.dot(a, b, preferred_element_type=f32)


def _dot_nt(a, b):
    return lax.dot_general(a, b, (((1,), (1,)), ((), ())), preferred_element_type=f32)


def _split(x, n):
    parts = []
    r = x
    for i in range(n):
        p = r.astype(bf16)
        parts.append(p)
        if i + 1 < n:
            r = r - p.astype(f32)
    return parts


def _dot_lsplit(x, m, n=3):
    rows = x.shape[0]
    d = _dot(jnp.concatenate(_split(x, n), axis=0), m)
    acc = d[0:rows]
    for i in range(1, n):
        acc = acc + d[i * rows:(i + 1) * rows]
    return acc


def _dot_rsplit(m, x, n=3):
    cols = x.shape[1]
    d = _dot(m, jnp.concatenate(_split(x, n), axis=1))
    acc = d[:, 0:cols]
    for i in range(1, n):
        acc = acc + d[:, i * cols:(i + 1) * cols]
    return acc


_sigmoid = jax.nn.sigmoid


def _silu(x):
    return x * _sigmoid(x)


def _softplus(x):
    return jnp.maximum(x, 0.0) + jnp.log(1.0 + jnp.exp(-jnp.abs(x)))


def _cparams(sem):
    return pltpu.CompilerParams(dimension_semantics=sem, vmem_limit_bytes=VMEM_LIMIT)


def _inproj_kernel(x_ref, nw_ref, w_ref, o_ref, h_ref, *, rows):
    @pl.when(pl.program_id(1) == 0)
    def _():
        def body(r, c):
            sl = pl.ds(pl.multiple_of(r * rows, rows), rows)
            x = x_ref[sl, :]
            ms = jnp.mean(x * x, axis=-1, keepdims=True)
            h_ref[sl, :] = (x * lax.rsqrt(ms + EPS) * nw_ref[...]).astype(bf16)
            return c
        lax.fori_loop(0, x_ref.shape[0] // rows, body, 0)

    o_ref[...] = _dot_nt(h_ref[...], w_ref[...])


def _inproj(x2d, norm_w, w_bf16):
    n = x2d.shape[0]
    tm = min(1024, n)
    assert n % tm == 0 and PROJ_W % PROJ_TN == 0
    return pl.pallas_call(
        functools.partial(_inproj_kernel, rows=min(128, tm)),
        out_shape=jax.ShapeDtypeStruct((n, PROJ_W), f32),
        grid=(n // tm, PROJ_W // PROJ_TN),
        in_specs=[pl.BlockSpec((tm, D_MODEL), lambda i, j: (i, 0)),
                  pl.BlockSpec((1, D_MODEL), lambda i, j: (0, 0)),
                  pl.BlockSpec((PROJ_TN, D_MODEL), lambda i, j: (j, 0))],
        out_specs=pl.BlockSpec((tm, PROJ_TN), lambda i, j: (i, j)),
        scratch_shapes=[pltpu.VMEM((tm, D_MODEL), bf16)],
        compiler_params=_cparams(("arbitrary", "arbitrary")),
        name="inproj",
    )(x2d, norm_w.reshape(1, D_MODEL), w_bf16)


CONV_TILES = 3
CONV_COLS = 2 * HEAD
CONV_ROWS = 128
CONV_PAD_W = CONV_TILES * PROJ_TN


def _qkv_kind(col):
    return "q" if col < QK_W else "k" if col < 2 * QK_W else "v" if col < QKV_W else "raw"


def _inproj_conv_kernel(x_ref, nw_ref, w_ref, cw_ref, o_ref, tail_ref, h_ref, hist_ref, raw_ref, *,
                        rows, tiles_per_seq):
    i = pl.program_id(0)
    j = pl.program_id(1)
    tm = x_ref.shape[0]

    @pl.when(j == 0)
    def _():
        def body(r, c):
            sl = pl.ds(pl.multiple_of(r * rows, rows), rows)
            x = x_ref[sl, :]
            ms = jnp.mean(x * x, axis=-1, keepdims=True)
            h_ref[sl, :] = (x * lax.rsqrt(ms + EPS) * nw_ref[...]).astype(bf16)
            return c
        lax.fori_loop(0, tm // rows, body, 0)

    @pl.when((i == 0) & (j == 0))
    def _():
        hist_ref[...] = jnp.zeros(hist_ref.shape, f32)

    @pl.when(j >= CONV_TILES)
    def _():
        o_ref[...] = _dot_nt(h_ref[...], w_ref[...])

    seq_start = lax.rem(i, tiles_per_seq) == 0
    for jj in range(CONV_TILES):
        @pl.when(j == jj)
        def _():
            def matmul_chunk(idx, c0):
                raw_ref[idx % 2] = _dot_nt(h_ref[...], w_ref[c0:c0 + CONV_COLS, :])

            def conv_chunk(idx, c0):
                cs = slice(c0, c0 + CONV_COLS)
                raw = raw_ref.at[idx % 2]
                tail = raw[tm - 8:tm, :]
                tail_ref[0, :, cs] = tail
                kinds = [_qkv_kind(jj * PROJ_TN + c0 + g * HEAD) for g in range(CONV_COLS // HEAD)]
                if kinds[0] == "raw":
                    o_ref[:, cs] = raw[...]
                    return
                hist = jnp.where(seq_start, 0.0, hist_ref[jj, :, cs])
                for rc in range(tm // CONV_ROWS):
                    r0 = rc * CONV_ROWS
                    if rc > 0:
                        xe = raw[r0 - 8:r0 + CONV_ROWS, :]
                    else:
                        xe = jnp.concatenate([hist, raw[0:CONV_ROWS, :]], axis=0)
                    acc = pltpu.roll(xe, 3, axis=0)[8:] * cw_ref[0:1, cs]
                    acc = acc + pltpu.roll(xe, 2, axis=0)[8:] * cw_ref[1:2, cs]
                    acc = acc + pltpu.roll(xe, 1, axis=0)[8:] * cw_ref[2:3, cs]
                    acc = acc + xe[8:] * cw_ref[3:4, cs]
                    act = _silu(acc)
                    for g, kind in enumerate(kinds):
                        ah = act[:, g * HEAD:(g + 1) * HEAD]
                        if kind != "v":
                            ss = jnp.sum(ah * ah, axis=-1, keepdims=True)
                            inv = lax.rsqrt(ss + EPS)
                            ah = ah * (inv * (HEAD ** -0.5) if kind == "q" else inv)
                        o_ref[r0:r0 + CONV_ROWS, c0 + g * HEAD:c0 + (g + 1) * HEAD] = ah
                hist_ref[jj, :, cs] = tail

            chunks = list(range(0, PROJ_TN, CONV_COLS))
            matmul_chunk(0, chunks[0])
            for idx in range(1, len(chunks)):
                matmul_chunk(idx, chunks[idx])
                conv_chunk(idx - 1, chunks[idx - 1])
            conv_chunk(len(chunks) - 1, chunks[-1])


def _inproj_conv(x2d, norm_w, w_bf16, cwa, seq_len):
    n = x2d.shape[0]
    tm = min(1024, seq_len)
    assert n % tm == 0 and seq_len % tm == 0 and PROJ_W % PROJ_TN == 0 and tm % CONV_ROWS == 0
    assert QKV_W % CONV_COLS == 0 and PROJ_TN % CONV_COLS == 0
    cw_pad = jnp.zeros((CONV_A, CONV_PAD_W), f32).at[:, 0:QKV_W].set(cwa)
    last = CONV_TILES - 1
    return pl.pallas_call(
        functools.partial(_inproj_conv_kernel, rows=min(128, tm), tiles_per_seq=seq_len // tm),
        out_shape=(jax.ShapeDtypeStruct((n, PROJ_W), f32),
                   jax.ShapeDtypeStruct((n // tm, 8, CONV_PAD_W), f32)),
        grid=(n // tm, PROJ_W // PROJ_TN),
        in_specs=[pl.BlockSpec((tm, D_MODEL), lambda i, j: (i, 0)),
                  pl.BlockSpec((1, D_MODEL), lambda i, j: (0, 0)),
                  pl.BlockSpec((PROJ_TN, D_MODEL), lambda i, j: (j, 0)),
                  pl.BlockSpec((CONV_A, PROJ_TN), lambda i, j: (0, jnp.minimum(j, last)))],
        out_specs=(pl.BlockSpec((tm, PROJ_TN), lambda i, j: (i, j)),
                   pl.BlockSpec((1, 8, PROJ_TN), lambda i, j: (i, 0, jnp.minimum(j, last)))),
        scratch_shapes=[pltpu.VMEM((tm, D_MODEL), bf16), pltpu.VMEM((CONV_TILES, 8, PROJ_TN), f32),
                        pltpu.VMEM((2, tm, CONV_COLS), f32)],
        compiler_params=_cparams(("arbitrary", "arbitrary")),
        name="inproj_conv",
    )(x2d, norm_w.reshape(1, D_MODEL), w_bf16, cw_pad)


W_IN_COLS = 11280
WPREP_TN = 1024
WPREP_SHIFT = 16
WP_BCX = COL_BCX // WPREP_TN
WP_Z = COL_Z // WPREP_TN
WP_BA = COL_BA // WPREP_TN
WP_SRC_Z = QKV_W // WPREP_TN
WP_SRC_BA = WP_SRC_Z + 1


def _wprep_kernel(a_ref, b_ref, o_ref):
    j = pl.program_id(0)
    keep = WPREP_TN - WPREP_SHIFT

    @pl.when((j < WP_BCX) | (j == WP_Z))
    def _():
        o_ref[...] = a_ref[...].astype(bf16)

    @pl.when((j >= WP_BCX) & (j < WP_Z))
    def _():
        o_ref[0:keep, :] = a_ref[WPREP_SHIFT:WPREP_TN, :].astype(bf16)
        o_ref[keep:WPREP_TN, :] = b_ref[...].astype(bf16)

    @pl.when(j == WP_BA)
    def _():
        o_ref[0:WPREP_SHIFT, :] = a_ref[0:WPREP_SHIFT, :].astype(bf16)
        o_ref[WPREP_SHIFT:WPREP_TN, :] = jnp.zeros((keep, D_MODEL), bf16)


def _wprep(w_in_t):
    assert w_in_t.shape == (W_IN_COLS, D_MODEL) and 2 * N_HEADS == WPREP_SHIFT
    n_blk = pl.cdiv(PROJ_W, WPREP_TN)

    def a_map(j):
        return (jnp.where(j < WP_BCX, j, jnp.where(j < WP_Z, j + 1, jnp.where(j == WP_Z, WP_SRC_Z, WP_SRC_BA))), 0)

    def b_map(j):
        return (jnp.minimum((WPREP_TN // WPREP_SHIFT) * (j + 2), W_IN_COLS // WPREP_SHIFT - 1), 0)

    return pl.pallas_call(
        _wprep_kernel,
        out_shape=jax.ShapeDtypeStruct((PROJ_W, D_MODEL), bf16),
        grid=(n_blk,),
        in_specs=[pl.BlockSpec((WPREP_TN, D_MODEL), a_map),
                  pl.BlockSpec((WPREP_SHIFT, D_MODEL), b_map)],
        out_specs=pl.BlockSpec((WPREP_TN, D_MODEL), lambda j: (j, 0)),
        compiler_params=_cparams(("arbitrary",)),
        name="wprep",
    )(w_in_t, w_in_t)


def _head_l2norm(a, scale):
    outs = []
    for h in range(N_HEADS):
        ah = a[:, h * HEAD:(h + 1) * HEAD]
        ss = jnp.sum(ah * ah, axis=-1, keepdims=True)
        n = ah * lax.rsqrt(ss + EPS)
        outs.append(n * scale if scale != 1.0 else n)
    return outs


def _delta_prompt_kernel(qkv_ref, bcx_ref, z_ref, ba_ref, cwb_ref, alog_ref, dtb_ref, onw_ref,
                         e64_ref,
                         o_ref, y_ref, snew_ref, ncb_ref,
                         s_ref, xb_ref, *, nb_step):
    C = CHUNK
    G = GROUP_HEADS
    R = G * C
    t = pl.program_id(1)
    nt = pl.num_programs(1)

    @pl.when(t == 0)
    def _():
        s_ref[...] = jnp.zeros(s_ref.shape, f32)
        xb_ref[:, 0:8, :] = jnp.zeros((nb_step, 8, SC_W), f32)

    rr = lax.broadcasted_iota(i32, (R, R), 0)
    cc = lax.broadcasted_iota(i32, (R, R), 1)
    same_bf = jnp.where((rr >> CHUNK_SHIFT) == (cc >> CHUNK_SHIFT), 1.0, 0.0).astype(bf16)
    r2 = lax.broadcasted_iota(i32, (R, G * HEAD), 0)
    c2 = lax.broadcasted_iota(i32, (R, G * HEAD), 1)
    bdmask = (r2 >> CHUNK_SHIFT) == (c2 >> HEAD_SHIFT)
    r3 = lax.broadcasted_iota(i32, (C, C), 0)
    c3 = lax.broadcasted_iota(i32, (C, C), 1)
    ltri = jnp.where(r3 >= c3, 1.0, 0.0).astype(bf16)
    r4 = lax.broadcasted_iota(i32, (C, R), 0)
    c4 = lax.broadcasted_iota(i32, (C, R), 1)
    ident_t = r4 == (c4 & (C - 1))
    incl_p = r4 >= (c4 & (C - 1))
    strict_p = r4 > (c4 & (C - 1))
    hblk = c4 >> CHUNK_SHIFT
    ones8 = jnp.ones((8, C), bf16)

    nbs = range(nb_step)
    units = [(nb, g) for nb in nbs for g in range(N_HEADS // G)]
    heads = lambda g: range(g * G, (g + 1) * G)

    qn = [[qkv_ref[nb, :, h * HEAD:(h + 1) * HEAD] for h in range(N_HEADS)] for nb in nbs]
    kn = [[qkv_ref[nb, :, QK_W + h * HEAD:QK_W + (h + 1) * HEAD] for h in range(N_HEADS)] for nb in nbs]
    vv = [qkv_ref[nb, :, 2 * QK_W:3 * QK_W] for nb in nbs]

    bts = [ba_ref[nb, :, 0:LANE] for nb in nbs]
    beta_all = [_sigmoid(bt) for bt in bts]
    g_all = [-(jnp.exp(alog_ref[:, 0:LANE]) * _softplus(bt + dtb_ref[:, 0:LANE])) for bt in bts]
    gc_small = [_dot_rsplit(ltri, ga) for ga in g_all]
    gl_small = [gc[C - 1:C, :] for gc in gc_small]

    k_st, q_st, kb, vb, kbg, qd, kd, gc_col = ({} for _ in range(8))
    for u in units:
        nb, g = u
        hs = heads(g)
        k_st[u] = jnp.concatenate([kn[nb][h] for h in hs], axis=0)
        q_st[u] = jnp.concatenate([qn[nb][h] for h in hs], axis=0)
        v_st = jnp.concatenate([vv[nb][:, h * HEAD:(h + 1) * HEAD] for h in hs], axis=0)
        beta_col = jnp.concatenate([beta_all[nb][:, h:h + 1] for h in hs], axis=0)
        gc_col[u] = jnp.concatenate([gc_small[nb][:, DECAY_LANE + h:DECAY_LANE + h + 1] for h in hs], axis=0)
        gl_col = jnp.concatenate(
            [jnp.broadcast_to(gl_small[nb][:, DECAY_LANE + h:DECAY_LANE + h + 1], (C, 1)) for h in hs], axis=0)
        kb[u] = k_st[u] * beta_col
        vb[u] = v_st * beta_col
        egc = jnp.exp(gc_col[u])
        kbg[u] = kb[u] * egc
        qd[u] = q_st[u] * egc
        kd[u] = k_st[u] * jnp.exp(gl_col - gc_col[u])

    gx = {u: _dot_lsplit(gc_small[u[0]], e64_ref[u[1], 0:LANE, :]) for u in units}
    crow = {u: _dot_rsplit(ones8, jnp.where(ident_t, gx[u], 0.0))[0:1, :] for u in units}
    a = {u: _dot_nt(jnp.concatenate([kb[u], q_st[u]], axis=0).astype(bf16), k_st[u].astype(bf16))
         for u in units}
    in_blk = [hblk == h for h in range(G - 1)]

    def pack(x):
        out = x[(G - 1) * C:G * C]
        for h in reversed(range(G - 1)):
            out = jnp.where(in_blk[h], x[h * C:(h + 1) * C], out)
        return out

    def expand(xp):
        return jnp.concatenate([xp.astype(bf16)] * G, axis=0) * same_bf

    dec = {u: jnp.where(incl_p, jnp.exp(jnp.where(incl_p, gx[u] - crow[u], 0.0)), 0.0) for u in units}
    nm = {u: jnp.where(strict_p, -(pack(a[u][0:R]) * dec[u]), 0.0) for u in units}
    qkm = {u: expand(pack(a[u][R:2 * R]) * dec[u]) for u in units}

    p = {u: jnp.where(ident_t, 1.0, 0.0) + nm[u] for u in units}
    nk = {u: _dot(nm[u].astype(bf16), expand(nm[u])) for u in units}
    for _ in range(4):
        for u in units:
            x = _dot(jnp.concatenate([p[u], nk[u]], axis=0).astype(bf16), expand(nk[u]))
            p[u] = p[u] + x[0:C]
            nk[u] = x[C:2 * C]
    for u in units:
        p[u] = p[u] + _dot(p[u].astype(bf16), expand(nk[u]))
    uw = {u: _dot(expand(p[u]), jnp.concatenate([vb[u], kbg[u]], axis=1).astype(bf16)) for u in units}

    ws = {}
    for u in units:
        nb, g = u
        for j, h in enumerate(heads(g)):
            sh = s_ref[nb, :, h * HEAD:(h + 1) * HEAD]
            lhs = jnp.concatenate([uw[u][j * C:(j + 1) * C, HEAD:2 * HEAD], qd[u][j * C:(j + 1) * C]], axis=0)
            ws[u, j] = _dot(lhs.astype(bf16), sh.astype(bf16))
    o_heads = {}
    for u in units:
        nb, g = u
        vnew_st = jnp.concatenate([uw[u][j * C:(j + 1) * C, 0:HEAD] - ws[u, j][0:C] for j in range(G)], axis=0)
        o_st = (jnp.concatenate([ws[u, j][C:2 * C] for j in range(G)], axis=0)
                + _dot(qkm[u], vnew_st.astype(bf16)))
        vbd = jnp.where(bdmask, jnp.concatenate([vnew_st] * G, axis=1), 0.0)
        lo = g * G * HEAD
        hi = lo + G * HEAD
        gl_row = jnp.concatenate(
            [jnp.broadcast_to(jnp.exp(gl_small[nb][:, DECAY_LANE + h:DECAY_LANE + h + 1]), (1, HEAD)) for h in heads(g)], axis=1)
        s_ref[nb, :, lo:hi] = s_ref[nb, :, lo:hi] * gl_row + _dot(kd[u].T.astype(bf16), vbd.astype(bf16))
        for j, h in enumerate(heads(g)):
            o_heads[nb, h] = o_st[j * C:(j + 1) * C]

    for nb in nbs:
        zt = z_ref[nb]
        for h in range(N_HEADS):
            oh = o_heads[nb, h]
            ms = jnp.mean(oh * oh, axis=-1, keepdims=True)
            zh = zt[:, h * HEAD:(h + 1) * HEAD]
            on = oh * lax.rsqrt(ms + EPS) * onw_ref[...] * _silu(zh)
            o_ref[nb, :, h * HEAD:(h + 1) * HEAD] = on.astype(bf16)

    for nb in nbs:
        bcx = bcx_ref[nb]
        cx = bcx[:, SC_W:2 * SC_W] * bcx[:, 2 * SC_W:3 * SC_W]
        xb_ref[nb, 8:8 + C, :] = cx
        ce = xb_ref[nb]
        cv = pltpu.roll(ce, 2, axis=0)[8:8 + C] * cwb_ref[0:1, :]
        cv = cv + pltpu.roll(ce, 1, axis=0)[8:8 + C] * cwb_ref[1:2, :]
        cv = cv + cx * cwb_ref[2:3, :]
        y_ref[nb] = (bcx[:, 0:SC_W] * cv).astype(bf16)
        xb_ref[nb, 0:8, :] = xb_ref[nb, C:C + 8, :]

    @pl.when(t == nt - 1)
    def _():
        for nb in range(nb_step):
            for h in range(N_HEADS):
                snew_ref[nb, h] = s_ref[nb, :, h * HEAD:(h + 1) * HEAD]
            ncb_ref[nb] = xb_ref[nb, 6:8, :]


def _expand_consts():
    lane = jnp.arange(BA_W)[:, None]
    col = jnp.arange(QK_W)[None, :]
    eb = (lane == (col >> HEAD_SHIFT)).astype(bf16)
    eg = (lane == (DECAY_LANE + (col >> HEAD_SHIFT))).astype(bf16)
    col64 = jnp.arange(GROUP_HEADS * CHUNK)[None, :]
    e64 = jnp.stack([(lane == (DECAY_LANE + g * GROUP_HEADS + (col64 >> CHUNK_SHIFT))).astype(bf16)
                     for g in range(N_HEADS // GROUP_HEADS)], axis=0)
    return eb, eg, e64


def _delta_prompt(proj3, cwb, alog_row, dtb_row, onw_row, nb_step):
    b, t, _ = proj3.shape
    assert t % CHUNK == 0 and b % nb_step == 0
    _, _, e64 = _expand_consts()
    c = CHUNK
    const2 = lambda bi, ti: (0, 0)
    outs = pl.pallas_call(
        functools.partial(_delta_prompt_kernel, nb_step=nb_step),
        out_shape=(jax.ShapeDtypeStruct((b, t, QK_W), bf16),
                   jax.ShapeDtypeStruct((b, t, SC_W), bf16),
                   jax.ShapeDtypeStruct((b, N_HEADS, HEAD, HEAD), f32),
                   jax.ShapeDtypeStruct((b, CONV_B - 1, SC_W), f32)),
        grid=(b // nb_step, t // c),
        in_specs=[pl.BlockSpec((nb_step, c, QKV_W), lambda bi, ti: (bi, ti, COL_QKV // QKV_W)),
                  pl.BlockSpec((nb_step, c, QKV_W), lambda bi, ti: (bi, ti, COL_BCX // QKV_W)),
                  pl.BlockSpec((nb_step, c, QK_W), lambda bi, ti: (bi, ti, COL_Z // QK_W)),
                  pl.BlockSpec((nb_step, c, BA_W), lambda bi, ti: (bi, ti, COL_BA // BA_W)),
                  pl.BlockSpec((CONV_B, SC_W), const2),
                  pl.BlockSpec((1, BA_W), const2),
                  pl.BlockSpec((1, BA_W), const2),
                  pl.BlockSpec((1, HEAD), const2),
                  pl.BlockSpec((N_HEADS // GROUP_HEADS, BA_W, GROUP_HEADS * CHUNK), lambda bi, ti: (0, 0, 0))],
        out_specs=(pl.BlockSpec((nb_step, c, QK_W), lambda bi, ti: (bi, ti, 0)),
                   pl.BlockSpec((nb_step, c, SC_W), lambda bi, ti: (bi, ti, 0)),
                   pl.BlockSpec((nb_step, N_HEADS, HEAD, HEAD), lambda bi, ti: (bi, 0, 0, 0)),
                   pl.BlockSpec((nb_step, CONV_B - 1, SC_W), lambda bi, ti: (bi, 0, 0))),
        scratch_shapes=[pltpu.VMEM((nb_step, HEAD, QK_W), f32),
                        pltpu.VMEM((nb_step, 8 + c, SC_W), f32)],
        compiler_params=_cparams(("arbitrary", "arbitrary")),
        name="delta_prompt",
    )(proj3, proj3, proj3, proj3, cwb, alog_row, dtb_row, onw_row, e64)
    return outs


def _sample_prep_kernel(p_ref, bufa_ref, bufb_ref, cwa_ref, cwb_ref, alog_ref, dtb_ref, eb_ref, eg_ref,
                        q_ref, k_ref, v_ref, beta_ref, eg_out_ref, z_ref, y_ref, nbufa_ref, nbufb_ref):
    def put_heads(ref, a):
        for h in range(N_HEADS):
            ref[:, h, :] = a[:, h * HEAD:(h + 1) * HEAD]

    def conv_sec(lo):
        hi = lo + QK_W
        raw = p_ref[:, COL_QKV + lo:COL_QKV + hi]
        acc = bufa_ref[0, :, lo:hi] * cwa_ref[0:1, lo:hi]
        acc = acc + bufa_ref[1, :, lo:hi] * cwa_ref[1:2, lo:hi]
        acc = acc + bufa_ref[2, :, lo:hi] * cwa_ref[2:3, lo:hi]
        acc = acc + raw * cwa_ref[3:4, lo:hi]
        nbufa_ref[0, :, lo:hi] = bufa_ref[1, :, lo:hi]
        nbufa_ref[1, :, lo:hi] = bufa_ref[2, :, lo:hi]
        nbufa_ref[2, :, lo:hi] = raw
        return _silu(acc)

    qn = _head_l2norm(conv_sec(0), HEAD ** -0.5)
    kn = _head_l2norm(conv_sec(QK_W), 1.0)
    for h in range(N_HEADS):
        q_ref[:, h, :] = qn[h]
        k_ref[:, h, :] = kn[h]
    put_heads(v_ref, conv_sec(2 * QK_W))
    put_heads(z_ref, p_ref[:, COL_Z:COL_Z + QK_W])

    bt = p_ref[:, COL_BA:COL_BA + BA_W]
    beta_all = _sigmoid(bt)
    g_all = -(jnp.exp(alog_ref[...]) * _softplus(bt + dtb_ref[...]))
    put_heads(beta_ref, _dot_lsplit(beta_all, eb_ref[...]))
    put_heads(eg_out_ref, jnp.exp(_dot_lsplit(g_all, eg_ref[...])))

    bg = p_ref[:, COL_BCX:COL_BCX + SC_W]
    cx = p_ref[:, COL_BCX + SC_W:COL_BCX + 2 * SC_W] * p_ref[:, COL_BCX + 2 * SC_W:COL_BCX + 3 * SC_W]
    cv = bufb_ref[0] * cwb_ref[0:1, :]
    cv = cv + bufb_ref[1] * cwb_ref[1:2, :]
    cv = cv + cx * cwb_ref[2:3, :]
    y_ref[...] = (bg * cv).astype(bf16)
    nbufb_ref[0] = bufb_ref[1]
    nbufb_ref[1] = cx


def _sample_prep(proj_s, bufa_t, bufb_t, cwa, cwb, alog_row, dtb_row):
    n = proj_s.shape[0]
    eb, eg, _ = _expand_consts()
    row = jax.ShapeDtypeStruct((n, N_HEADS, HEAD), f32)
    return pl.pallas_call(
        _sample_prep_kernel,
        out_shape=(row, row, row, row, row, row,
                   jax.ShapeDtypeStruct((n, SC_W), bf16),
                   jax.ShapeDtypeStruct((CONV_A - 1, n, QKV_W), f32),
                   jax.ShapeDtypeStruct((CONV_B - 1, n, SC_W), f32)),
        compiler_params=pltpu.CompilerParams(vmem_limit_bytes=VMEM_LIMIT),
        name="sample_prep",
    )(proj_s, bufa_t, bufb_t, cwa, cwb, alog_row, dtb_row, eb, eg)


def _sample_step_kernel(s_ref, q_ref, k_ref, v_ref, beta_ref, eg_ref, z_ref, onw_ref,
                        snew_ref, o_ref, *, bb):
    w = N_HEADS * HEAD
    r8 = lax.broadcasted_iota(i32, (N_HEADS, w), 0)
    c8 = lax.broadcasted_iota(i32, (N_HEADS, w), 1)
    mask8 = r8 == (c8 >> HEAD_SHIFT)
    zpad_k = jnp.zeros((HEAD - N_HEADS, HEAD), f32)
    hb = lambda h: slice(h * HEAD, (h + 1) * HEAD)
    bs = range(bb)
    s_dec, k8s, kts = [], [], []
    for b in bs:
        s_all = jnp.concatenate([s_ref[b, h] for h in range(N_HEADS)], axis=1)
        eg8 = eg_ref[b]
        eg_row = jnp.concatenate([eg8[h:h + 1, :] for h in range(N_HEADS)], axis=1)
        s_dec.append(s_all * eg_row)
        k8s.append(k_ref[b])
        kts.append(jnp.concatenate([k8s[b], zpad_k], axis=0).T)
    xs = [_dot(k8s[b].astype(bf16), s_dec[b].astype(bf16)) for b in bs]
    s_new = []
    for b in bs:
        vb8, bt8 = v_ref[b], beta_ref[b]
        upd = [kts[b][:, h:h + 1] * ((vb8[h:h + 1, :] - xs[b][h:h + 1, hb(h)]) * bt8[h:h + 1, :])
               for h in range(N_HEADS)]
        s_new.append(s_dec[b] + jnp.concatenate(upd, axis=1))
    ys = [_dot(q_ref[b].astype(bf16), s_new[b].astype(bf16)) for b in bs]
    for b in bs:
        yv = jnp.where(mask8, ys[b], 0.0)
        o8 = yv[:, 0:HEAD]
        for j in range(1, N_HEADS):
            o8 = o8 + yv[:, j * HEAD:(j + 1) * HEAD]
        ms = jnp.mean(o8 * o8, axis=-1, keepdims=True)
        o_ref[b] = o8 * lax.rsqrt(ms + EPS) * onw_ref[...] * _silu(z_ref[b])
        for h in range(N_HEADS):
            snew_ref[b, h] = s_new[b][:, hb(h)]


def _sample_step(state, q, k, v, beta, eg, z, onw_row, bb=8):
    n = state.shape[0]
    assert n % bb == 0
    hspec = pl.BlockSpec((bb, N_HEADS, HEAD), lambda i: (i, 0, 0))
    sspec = pl.BlockSpec((bb, N_HEADS, HEAD, HEAD), lambda i: (i, 0, 0, 0))
    return pl.pallas_call(
        functools.partial(_sample_step_kernel, bb=bb),
        out_shape=(jax.ShapeDtypeStruct(state.shape, f32),
                   jax.ShapeDtypeStruct((n, N_HEADS, HEAD), f32)),
        grid=(n // bb,),
        in_specs=[sspec, hspec, hspec, hspec, hspec, hspec, hspec, pl.BlockSpec((1, HEAD), lambda i: (0, 0))],
        out_specs=(sspec, hspec),
        compiler_params=_cparams(("arbitrary",)),
        name="sample_step",
    )(state, q, k, v, beta, eg, z, onw_row)


def _mix_route_kernel(x_ref, o_ref, y_ref, ga_ref, gb_ref, wa_ref, wb_ref, wo_ref, n2_ref,
                      rwh_ref, rwl_ref, rb_ref, cnt_in_ref, x1_ref, h2_ref, mi_ref, mw_ref, cnt_ref):
    i = pl.program_id(0)
    tm = x_ref.shape[0]

    @pl.when(i == 0)
    def _():
        cnt_ref[...] = cnt_in_ref[...]

    oa = _dot(o_ref[...], wa_ref[...])
    ob = _dot(y_ref[...], wb_ref[...])
    merged = _sigmoid(ga_ref[...]) * oa + _sigmoid(gb_ref[...]) * ob
    x1 = x_ref[...] + _dot(merged.astype(bf16), wo_ref[...])
    x1_ref[...] = x1
    ms = jnp.mean(x1 * x1, axis=-1, keepdims=True)
    h2 = x1 * lax.rsqrt(ms + EPS) * n2_ref[...]
    h2_ref[...] = h2

    h_hi, h_lo = _split(h2, 2)
    logits = _dot(h_hi, rwh_ref[...]) + _dot(h_hi, rwl_ref[...]) + _dot(h_lo, rwh_ref[...]) + rb_ref[...]

    lane = lax.broadcasted_iota(i32, (tm, LANE), 1)
    lanef = lane.astype(f32)
    neg = jnp.float32(-jnp.inf)
    big = jnp.float32(1e9)
    gmask = (lane >= N_EXPERTS) & (lane < N_EXPERTS + N_GROUPS)
    gl = jnp.where(gmask, logits, neg)
    gmax = jnp.max(gl, axis=-1, keepdims=True)
    gidx = jnp.min(jnp.where(gl == gmax, lanef - N_EXPERTS, big), axis=-1, keepdims=True)
    gsum = jnp.sum(jnp.where(gmask, jnp.exp(gl - gmax), 0.0), axis=-1, keepdims=True)
    gprob = 1.0 / gsum

    emask = (lane < N_EXPERTS) & ((lane >> GROUP_SHIFT).astype(f32) == gidx)
    el = jnp.where(emask, logits, neg)
    emax = jnp.max(el, axis=-1, keepdims=True)
    pe = jnp.where(emask, jnp.exp(el - emax), 0.0)
    eprob = pe / jnp.sum(pe, axis=-1, keepdims=True)
    p1m = jnp.where(emask, eprob, -1.0)
    m1 = jnp.max(p1m, axis=-1, keepdims=True)
    i1 = jnp.min(jnp.where(p1m == m1, lanef, big), axis=-1, keepdims=True)
    p2m = jnp.where(lanef == i1, -1.0, p1m)
    m2 = jnp.max(p2m, axis=-1, keepdims=True)
    i2 = jnp.min(jnp.where(p2m == m2, lanef, big), axis=-1, keepdims=True)
    tot = m1 + m2
    c1 = m1 / tot * gprob
    c2 = m2 / tot * gprob

    oh1 = jnp.where(lanef == i1, 1.0, 0.0)
    oh2 = jnp.where(lanef == i2, 1.0, 0.0)
    ohs = oh1 + oh2
    rt = lax.broadcasted_iota(i32, (tm, tm), 0)
    ct = lax.broadcasted_iota(i32, (tm, tm), 1)
    lstrict = jnp.where(rt > ct, 1.0, 0.0).astype(bf16)
    cs = _dot(lstrict, ohs.astype(bf16)) + cnt_ref[...]
    rank1 = jnp.sum(cs * oh1, axis=-1, keepdims=True)
    rank2 = jnp.sum(cs * oh2, axis=-1, keepdims=True)
    cnt_ref[...] = cnt_ref[...] + jnp.sum(ohs, axis=0, keepdims=True)

    mi = jnp.where(lane == 0, i1, jnp.where(lane == 1, i2, jnp.where(lane == 2, rank1,
                                                                     jnp.where(lane == 3, rank2, 0.0))))
    mi_ref[...] = mi.astype(i32)
    mw_ref[...] = jnp.where(lane == 0, c1, jnp.where(lane == 1, c2, 0.0))


def _mix_route(x2d, o2d, y2d, proj2d, wa, wb, wo, n2_row, rwh, rwl, rb_row, cnt_in):
    n = x2d.shape[0]
    tm = min(256, n)
    assert n % tm == 0
    tok = lambda width: pl.BlockSpec((tm, width), lambda i: (i, 0))
    full = lambda a: pl.BlockSpec(a.shape, lambda i: (0,) * a.ndim)
    in_specs = [tok(D_MODEL), tok(QK_W), tok(SC_W),
                pl.BlockSpec((tm, D_MODEL), lambda i: (i, COL_GA // D_MODEL)),
                pl.BlockSpec((tm, D_MODEL), lambda i: (i, COL_GB // D_MODEL)),
                full(wa), full(wb), full(wo), full(n2_row), full(rwh), full(rwl), full(rb_row), full(cnt_in)]
    out_shape = (jax.ShapeDtypeStruct((n, D_MODEL), f32),
                 jax.ShapeDtypeStruct((n, D_MODEL), f32),
                 jax.ShapeDtypeStruct((n, LANE), i32),
                 jax.ShapeDtypeStruct((n, LANE), f32),
                 jax.ShapeDtypeStruct((1, LANE), f32))
    out_specs = (tok(D_MODEL), tok(D_MODEL), tok(LANE), tok(LANE),
                 pl.BlockSpec((1, LANE), lambda i: (0, 0)))
    return pl.pallas_call(
        _mix_route_kernel,
        out_shape=out_shape,
        grid=(n // tm,),
        in_specs=in_specs,
        out_specs=out_specs,
        compiler_params=_cparams(("arbitrary",)),
        name="mix_route",
    )(x2d, o2d, y2d, proj2d, proj2d, wa, wb, wo, n2_row, rwh, rwl, rb_row, cnt_in)


MI_W = 4
SUBLANE = 8


def _dest_kernel(mi_ref, starts_ref, o_ref):
    mi = mi_ref[...]
    lane = lax.broadcasted_iota(i32, mi.shape, 1)
    st = starts_ref[...]

    def first_row(e_col):
        return jnp.sum(jnp.where(lane == e_col, st, 0.0), axis=-1, keepdims=True).astype(i32)

    d0 = first_row(mi[:, 0:1]) + mi[:, 2:3]
    d1 = first_row(mi[:, 1:2]) + mi[:, 3:4]
    sh = SUBLANE.bit_length() - 1
    o_ref[...] = jnp.where(lane == 0, d0 >> sh, jnp.where(lane == 1, d0 & (SUBLANE - 1),
                           jnp.where(lane == 2, d1 >> sh, jnp.where(lane == 3, d1 & (SUBLANE - 1), 0))))


def _dest_rows(mi, starts_row):
    n = mi.shape[0]
    tm = min(1024, n)
    assert n % tm == 0
    return pl.pallas_call(
        _dest_kernel,
        out_shape=jax.ShapeDtypeStruct((n, LANE), i32),
        grid=(n // tm,),
        in_specs=[pl.BlockSpec((tm, LANE), lambda i: (i, 0)), pl.BlockSpec((1, LANE), lambda i: (0, 0))],
        out_specs=pl.BlockSpec((tm, LANE), lambda i: (i, 0)),
        compiler_params=_cparams(("arbitrary",)),
        name="moe_dest",
    )(mi, starts_row)


def _dispatch_kernel(mi_ref, hp_ref, hs_ref, xs_ref, sem, *, np_tiles):
    i = pl.program_id(0)

    def scatter_rows(h_ref):
        n_tiles = h_ref.shape[0]

        def start(t, c):
            for u in range(SUBLANE):
                rec = MI_W * (SUBLANE * t + u)
                for k in range(2):
                    dst = xs_ref.at[mi_ref[rec + 2 * k], pl.ds(mi_ref[rec + 2 * k + 1], 1)]
                    pltpu.make_async_copy(h_ref.at[t, pl.ds(u, 1)], dst, sem).start(priority=k)
            return c

        lax.fori_loop(0, n_tiles, start, 0)
        for k in range(2):
            pltpu.make_async_copy(h_ref, xs_ref.at[pl.ds(0, n_tiles)], sem).wait()

    @pl.when(i < np_tiles)
    def _():
        scatter_rows(hp_ref)

    @pl.when(i >= np_tiles)
    def _():
        scatter_rows(hs_ref)


def _dispatch(h2_p, h2_s, mi_flat):
    tm = TOKEN_TILE
    n_p, n_s = h2_p.shape[0], h2_s.shape[0]
    assert n_p % tm == 0 and n_s <= tm and n_s % SUBLANE == 0 and tm % SUBLANE == 0
    np_tiles = n_p // tm
    tiled = lambda a: a.reshape(a.shape[0] // SUBLANE, SUBLANE, D_MODEL)
    return pl.pallas_call(
        functools.partial(_dispatch_kernel, np_tiles=np_tiles),
        out_shape=jax.ShapeDtypeStruct((2 * (n_p + n_s) // SUBLANE, SUBLANE, D_MODEL), f32),
        grid=(np_tiles + 1,),
        in_specs=[pl.BlockSpec((MI_W * tm,), lambda i: (i,), memory_space=pltpu.SMEM),
                  pl.BlockSpec((tm // SUBLANE, SUBLANE, D_MODEL), lambda i: (jnp.minimum(i, np_tiles - 1), 0, 0)),
                  pl.BlockSpec((n_s // SUBLANE, SUBLANE, D_MODEL), lambda i: (0, 0, 0))],
        out_specs=pl.BlockSpec(memory_space=pl.ANY),
        scratch_shapes=[pltpu.SemaphoreType.DMA(())],
        compiler_params=_cparams(("arbitrary",)),
        name="moe_dispatch",
    )(mi_flat, tiled(h2_p), tiled(h2_s))


def _cast_rows(src_ref, dst_ref, col0=0, rows=256):
    width = src_ref.shape[1]

    def body(r, c):
        sl = pl.ds(pl.multiple_of(r * rows, rows), rows)
        dst_ref[sl, col0:col0 + width] = src_ref[sl, :].astype(bf16)
        return c
    lax.fori_loop(0, src_ref.shape[0] // rows, body, 0)


def _moe_kernel(blk_ref, lo_ref, hi_ref, first_ref, newe_ref, slot_ref, pre_ref, init_ref,
                x_hbm, wg_hbm, wu_hbm, wd_hbm, o_ref,
                wg_f, wu_f, wd_f, wgu_b, wd_b, sem, x_buf, x_sem):
    i = pl.program_id(0)
    n = pl.num_programs(0)
    lo = lo_ref[i]
    hi = hi_ref[i]

    def x_copy(item, slot):
        rows = pl.ds(pl.multiple_of(blk_ref[item] * MOE_ROWS, MOE_ROWS), MOE_ROWS)
        return pltpu.make_async_copy(x_hbm.at[rows], x_buf.at[slot], x_sem.at[slot])

    @pl.when(i == 0)
    def _():
        for a in range(X_SLOTS - 1):
            @pl.when(a < n)
            def _():
                x_copy(a, a).start()

    ahead = i + (X_SLOTS - 1)

    @pl.when(ahead < n)
    def _():
        x_copy(ahead, lax.rem(ahead, X_SLOTS)).start()

    x_slot = lax.rem(i, X_SLOTS)
    x_copy(i, x_slot).wait()

    def weight_copies(e, slot):
        return [pltpu.make_async_copy(wg_hbm.at[e], wg_f.at[slot], sem.at[slot, 0]),
                pltpu.make_async_copy(wu_hbm.at[e], wu_f.at[slot], sem.at[slot, 1]),
                pltpu.make_async_copy(wd_hbm.at[e], wd_f.at[slot], sem.at[slot, 2])]

    def start_weights(e, slot):
        for cp, prio in zip(weight_copies(e, slot), (0, 1, 1)):
            cp.start(priority=prio)

    @pl.when(i == 0)
    def _():
        start_weights(init_ref[0], 0)
        for k in range(1, W_SLOTS):
            @pl.when(init_ref[k] >= 0)
            def _():
                start_weights(init_ref[k], k)

    @pl.when(newe_ref[i] == 1)
    def _():
        slot = slot_ref[i]
        cg, cu, cd = weight_copies(0, slot)
        cg.wait()
        _cast_rows(wg_f.at[slot], wgu_b, 0)
        cu.wait()
        _cast_rows(wu_f.at[slot], wgu_b, D_FF)
        cd.wait()
        _cast_rows(wd_f.at[slot], wd_b)

        @pl.when(pre_ref[i] >= 0)
        def _():
            start_weights(pre_ref[i], slot)

    @pl.when(hi > lo)
    def _():
        x = x_buf[x_slot].astype(bf16)
        au = _dot(x, wgu_b[...])
        y = _dot((_silu(au[:, 0:D_FF]) * au[:, D_FF:2 * D_FF]).astype(bf16), wd_b[...])
        row = lax.broadcasted_iota(i32, y.shape, 0)
        ym = jnp.where((row >= lo) & (row < hi), y, 0.0)

        @pl.when(first_ref[i] == 1)
        def _():
            o_ref[...] = ym

        @pl.when(first_ref[i] == 0)
        def _():
            o_ref[...] = o_ref[...] + ym


def _moe(xs, w_gate, w_up, w_down, items):
    n_items = items[0].shape[0]
    rows = xs.shape[0]
    n_pref = len(items)
    xmap = lambda i, blk, *_: (blk[i], 0)
    grid_spec = pltpu.PrefetchScalarGridSpec(
        num_scalar_prefetch=n_pref,
        grid=(n_items,),
        in_specs=[pl.BlockSpec(memory_space=pl.ANY),
                  pl.BlockSpec(memory_space=pl.ANY),
                  pl.BlockSpec(memory_space=pl.ANY),
                  pl.BlockSpec(memory_space=pl.ANY)],
        out_specs=pl.BlockSpec((MOE_ROWS, D_MODEL), xmap),
        scratch_shapes=[pltpu.VMEM((W_SLOTS, D_MODEL, D_FF), f32), pltpu.VMEM((W_SLOTS, D_MODEL, D_FF), f32),
                        pltpu.VMEM((W_SLOTS, D_FF, D_MODEL), f32),
                        pltpu.VMEM((D_MODEL, 2 * D_FF), bf16), pltpu.VMEM((D_FF, D_MODEL), bf16),
                        pltpu.SemaphoreType.DMA((W_SLOTS, 3)),
                        pltpu.VMEM((X_SLOTS, MOE_ROWS, D_MODEL), f32), pltpu.SemaphoreType.DMA((X_SLOTS,))],
    )
    return pl.pallas_call(
        _moe_kernel,
        out_shape=jax.ShapeDtypeStruct((rows, D_MODEL), f32),
        grid_spec=grid_spec,
        compiler_params=_cparams(("arbitrary",)),
        name="moe_experts",
    )(*items, xs, w_gate, w_up, w_down)


def _combine_kernel(mi_ref, mi_next_ref, x1p_ref, mwp_ref, x1s_ref, mws_ref, fnw_ref, ys_ref,
                    yp_ref, ysm_ref, g_ref, sem, *, np_tiles):
    i = pl.program_id(0)
    tiles_p = x1p_ref.shape[0] // SUBLANE
    tiles_s = x1s_ref.shape[0] // SUBLANE
    slot = lax.rem(i, 2)

    def gather_rows(m_ref, dst_slot, n_tiles):
        def body(t, c):
            for u in range(SUBLANE):
                rec = MI_W * (SUBLANE * t + u)
                for k in range(2):
                    src = ys_ref.at[m_ref[rec + 2 * k], pl.ds(m_ref[rec + 2 * k + 1], 1)]
                    pltpu.make_async_copy(src, g_ref.at[dst_slot, k, t, pl.ds(u, 1)],
                                          sem.at[dst_slot]).start(priority=k)
            return c
        lax.fori_loop(0, n_tiles, body, 0)

    @pl.when(i == 0)
    def _():
        gather_rows(mi_ref, 0, tiles_p)

    @pl.when(i + 1 < np_tiles)
    def _():
        gather_rows(mi_next_ref, 1 - slot, tiles_p)

    @pl.when(i + 1 == np_tiles)
    def _():
        gather_rows(mi_next_ref, 1 - slot, tiles_s)

    def finish(x1_ref, mw_ref, out_ref, n_tiles):
        for k in range(2):
            pltpu.make_async_copy(ys_ref.at[pl.ds(0, n_tiles)], g_ref.at[slot, k, pl.ds(0, n_tiles)],
                                  sem.at[slot]).wait()
        rows = n_tiles * SUBLANE
        mw = mw_ref[...]
        g0 = g_ref[slot, 0, 0:n_tiles].reshape(rows, D_MODEL)
        g1 = g_ref[slot, 1, 0:n_tiles].reshape(rows, D_MODEL)
        x2 = x1_ref[...] + (g0 * mw[:, 0:1] + g1 * mw[:, 1:2])
        ms = jnp.mean(x2 * x2, axis=-1, keepdims=True)
        out_ref[...] = x2 * lax.rsqrt(ms + EPS) * fnw_ref[...]

    @pl.when(i < np_tiles)
    def _():
        finish(x1p_ref, mwp_ref, yp_ref, tiles_p)

    @pl.when(i >= np_tiles)
    def _():
        finish(x1s_ref, mws_ref, ysm_ref, tiles_s)


def _combine(x1_p, mw_p, x1_s, mw_s, fnw_row, ys3, mi_flat):
    tm = TOKEN_TILE
    n_p, n_s = x1_p.shape[0], x1_s.shape[0]
    assert n_p % tm == 0 and n_s <= tm and n_s % SUBLANE == 0
    np_tiles = n_p // tm
    ptile = lambda width: pl.BlockSpec((tm, width), lambda i: (jnp.minimum(i, np_tiles - 1), 0))
    stile = lambda width: pl.BlockSpec((n_s, width), lambda i: (0, 0))
    return pl.pallas_call(
        functools.partial(_combine_kernel, np_tiles=np_tiles),
        out_shape=(jax.ShapeDtypeStruct((n_p, D_MODEL), f32),
                   jax.ShapeDtypeStruct((n_s, D_MODEL), f32)),
        grid=(np_tiles + 1,),
        in_specs=[pl.BlockSpec((MI_W * tm,), lambda i: (i,), memory_space=pltpu.SMEM),
                  pl.BlockSpec((MI_W * tm,), lambda i: (jnp.minimum(i + 1, np_tiles),), memory_space=pltpu.SMEM),
                  ptile(D_MODEL), ptile(LANE), stile(D_MODEL), stile(LANE),
                  pl.BlockSpec((1, D_MODEL), lambda i: (0, 0)),
                  pl.BlockSpec(memory_space=pl.ANY)],
        out_specs=(ptile(D_MODEL), stile(D_MODEL)),
        scratch_shapes=[pltpu.VMEM((2, 2, tm // SUBLANE, SUBLANE, D_MODEL), f32), pltpu.SemaphoreType.DMA((2,))],
        compiler_params=_cparams(("arbitrary",)),
        name="moe_combine",
    )(mi_flat, mi_flat, x1_p, mw_p, x1_s, mw_s, fnw_row, ys3)


PLAN_ROWS = 256
N_ITEM_FIELDS = 7


def _plan_kernel(cnt_ref, items_ref, rows_ref, *, nblk):
    cnt = cnt_ref[...]
    lane1 = lax.broadcasted_iota(i32, (1, LANE), 1)
    in_e = lane1 < N_EXPERTS
    ri = lax.broadcasted_iota(i32, (LANE, LANE), 0)
    ci = lax.broadcasted_iota(i32, (LANE, LANE), 1)
    upper = jnp.where(ri <= ci, 1.0, 0.0).astype(bf16)

    def cumsum_lanes(v):
        return _dot_lsplit(jnp.broadcast_to(v, (8, LANE)), upper)[0:1, :]

    shift = MOE_ROWS.bit_length() - 1
    ends = cumsum_lanes(cnt)
    starts = ends - cnt
    act = cnt > 0.0
    first_blk = (starts.astype(i32) >> shift).astype(f32)
    last_blk = (jnp.maximum(ends - 1.0, 0.0).astype(i32) >> shift).astype(f32)
    nvis = jnp.where(act, last_blk - first_blk + 1.0, 0.0)
    vis_end = cumsum_lanes(nvis)
    vis_start = vis_end - nvis
    total = jnp.max(vis_end, axis=-1, keepdims=True)
    cum_act = cumsum_lanes(jnp.where(act, 1.0, 0.0))
    n_uniq = jnp.max(cum_act, axis=-1, keepdims=True)

    p = PLAN_ROWS
    lane = lax.broadcasted_iota(i32, (p, LANE), 1)
    idx = lax.broadcasted_iota(i32, (p, LANE), 0).astype(f32)
    idx1 = idx[:, 0:1]
    count_le = lambda row, col: jnp.sum(jnp.where((row <= col) & in_e, 1.0, 0.0), axis=-1, keepdims=True)
    e = jnp.minimum(count_le(vis_end, idx), N_EXPERTS - 1.0)
    onehot = lane.astype(f32) == e
    look = lambda tbl: jnp.sum(jnp.where(onehot, tbl, 0.0), axis=-1, keepdims=True)
    blk = look(first_blk) + idx1 - look(vis_start)
    lo = jnp.maximum(look(starts), blk * MOE_ROWS) - blk * MOE_ROWS
    hi = jnp.minimum(look(ends), (blk + 1.0) * MOE_ROWS) - blk * MOE_ROWS
    valid = idx1 < total
    blk = jnp.where(valid, blk, nblk - 1.0)
    lo = jnp.where(valid, lo, 0.0)
    hi = jnp.where(valid, hi, 0.0)
    rep = lambda c: jnp.broadcast_to(c, (p, LANE))
    prev = lambda c: pltpu.roll(rep(c), 1, axis=0)[:, 0:1]
    is0 = idx1 == 0.0
    first = valid & (is0 | (blk != prev(blk)))
    newe = valid & (is0 | (e != prev(e)))
    rp = lax.broadcasted_iota(i32, (p, p), 0)
    cp = lax.broadcasted_iota(i32, (p, p), 1)
    lower = jnp.where(rp >= cp, 1.0, 0.0).astype(bf16)
    order = _dot(lower, rep(jnp.where(newe, 1.0, 0.0)).astype(bf16))[:, 0:1] - 1.0
    slot = jnp.where(newe, order - W_SLOTS * jnp.floor((order + 0.5) * (1.0 / W_SLOTS)), 0.0)
    k2 = order + float(W_SLOTS)
    pre = jnp.where(newe & (k2 < n_uniq), count_le(cum_act, rep(k2)), -1.0)
    out = jnp.zeros((p, LANE), f32)
    for c, v in enumerate([blk, lo, hi, jnp.where(first, 1.0, 0.0), jnp.where(newe, 1.0, 0.0), slot, pre]):
        out = jnp.where(lane == c, v, out)
    items_ref[...] = out.astype(i32)

    init_row = jnp.zeros((1, LANE), f32)
    for k in range(W_SLOTS):
        init_row = jnp.where(lane1 == k, jnp.where(n_uniq > float(k), count_le(cum_act, float(k)), -1.0), init_row)
    rows_ref[...] = jnp.zeros(rows_ref.shape, f32)
    rows_ref[0:1, :] = starts
    rows_ref[1:2, :] = init_row


def _work_items(cnt_row, n_rows):
    nblk = n_rows // MOE_ROWS
    n_items = nblk + N_EXPERTS - 1
    assert n_items <= PLAN_ROWS and n_rows % MOE_ROWS == 0
    items, rows = pl.pallas_call(
        functools.partial(_plan_kernel, nblk=nblk),
        out_shape=(jax.ShapeDtypeStruct((PLAN_ROWS, LANE), i32), jax.ShapeDtypeStruct((8, LANE), f32)),
        compiler_params=pltpu.CompilerParams(vmem_limit_bytes=VMEM_LIMIT),
        name="moe_plan",
    )(cnt_row)
    fields = tuple(items[0:n_items, c] for c in range(N_ITEM_FIELDS))
    return rows[0:1, :], fields + (rows[1, 0:W_SLOTS].astype(i32),)


def kernel(x_prompt, x_sample, state_delta, state_qkv_conv, state_short_conv, norm1_w, w_in, conv_a_w, a_log, dt_bias, out_norm_w, w_branch_a, conv_b_w, w_branch_b, w_o, norm2_w, router_group_w, router_group_b, router_expert_w, router_expert_b, w_gate, w_up, w_down, final_norm_w):
    assert norm1_w.shape[0] == 1, "single-layer trunk"
    bp, tp, d = x_prompt.shape
    bs, ts, _ = x_sample.shape
    assert d == D_MODEL and ts == 1
    n_p = bp * tp
    n_s = bs
    n_all = n_p + n_s

    w_perm = _wprep(jnp.transpose(w_in[0]))
    wa = w_branch_a[0].astype(bf16)
    wb = w_branch_b[0].astype(bf16)
    wo = w_o[0].astype(bf16)
    pad = lambda v: jnp.zeros((1, BA_W), f32).at[0, DECAY_LANE:DECAY_LANE + N_HEADS].set(v)
    alog_row = pad(a_log[0])
    dtb_row = pad(dt_bias[0])
    onw_row = out_norm_w[0].reshape(1, HEAD)
    cwa = conv_a_w[0]
    cwb = conv_b_w[0]
    r_pad = LANE - N_EXPERTS - N_GROUPS
    rw = jnp.concatenate([router_expert_w[0], router_group_w[0], jnp.zeros((D_MODEL, r_pad), f32)], axis=1)
    rwh = rw.astype(bf16)
    rwl = (rw - rwh.astype(f32)).astype(bf16)
    rb_row = jnp.concatenate([router_expert_b[0], router_group_b[0], jnp.zeros((r_pad,), f32)]).reshape(1, LANE)
    n2_row = norm2_w[0].reshape(1, D_MODEL)

    xp2 = x_prompt.reshape(n_p, D_MODEL)
    proj_p, tails = _inproj_conv(xp2, norm1_w[0], w_perm, cwa, tp)
    tiles_per_seq = tails.shape[0] // bp
    nca_p = tails.reshape(bp, tiles_per_seq, 8, CONV_PAD_W)[:, -1, 8 - (CONV_A - 1):8, 0:QKV_W]
    o_p, y_p, sd_p, ncb_p = _delta_prompt(proj_p.reshape(bp, tp, PROJ_W), cwb, alog_row, dtb_row,
                                          onw_row, nb_step=4 if bp % 4 == 0 else (2 if bp % 2 == 0 else 1))
    cnt0 = jnp.zeros((1, LANE), f32)
    x1_p, h2_p, mi_p, mw_p, cnt_p = _mix_route(xp2, o_p.reshape(n_p, QK_W), y_p.reshape(n_p, SC_W), proj_p,
                                               wa, wb, wo, n2_row, rwh, rwl, rb_row, cnt0)

    xs2 = x_sample.reshape(n_s, D_MODEL)
    proj_s = _inproj(xs2, norm1_w[0], w_perm)
    bufa_t = jnp.transpose(state_qkv_conv[0], (1, 0, 2))
    bufb_t = jnp.transpose(state_short_conv[0], (1, 0, 2))
    q_s, k_s, v_s, beta_s, eg_s, z_s, y_s, nbufa_t, nbufb_t = _sample_prep(proj_s, bufa_t, bufb_t, cwa, cwb,
                                                                           alog_row, dtb_row)
    sd_s, o_s = _sample_step(state_delta[0], q_s, k_s, v_s, beta_s, eg_s, z_s, onw_row)
    o_s2 = o_s.reshape(n_s, QK_W).astype(bf16)
    x1_s, h2_s, mi_s, mw_s, cnt = _mix_route(xs2, o_s2, y_s, proj_s, wa, wb, wo, n2_row, rwh, rwl, rb_row, cnt_p)

    starts_row, items = _work_items(cnt, 2 * n_all)
    mi_flat = jnp.concatenate([_dest_rows(mi_p, starts_row)[:, 0:MI_W], _dest_rows(mi_s, starts_row)[:, 0:MI_W]],
                              axis=0).reshape(MI_W * n_all)
    xs_sorted = _dispatch(h2_p, h2_s, mi_flat)
    ys = _moe(xs_sorted.reshape(2 * n_all, D_MODEL), w_gate[0], w_up[0], w_down[0], items)
    y_prompt, y_sample = _combine(x1_p, mw_p, x1_s, mw_s, final_norm_w.reshape(1, D_MODEL),
                                  ys.reshape(2 * n_all // SUBLANE, SUBLANE, D_MODEL), mi_flat)

    return (y_prompt.reshape(bp, tp, D_MODEL),
            y_sample.reshape(bs, ts, D_MODEL),
            sd_p[None],
            nca_p[None],
            ncb_p[None],
            sd_s[None],
            jnp.transpose(nbufa_t, (1, 0, 2))[None],
            jnp.transpose(nbufb_t, (1, 0, 2))[None])
```

```python
import functools

import jax
import jax.numpy as jnp
from jax import lax
from jax.experimental import pallas as pl
from jax.experimental.pallas import tpu as pltpu

f32 = jnp.float32
bf16 = jnp.bfloat16
i32 = jnp.int32

EPS = 1e-6
LANE = 128
D_MODEL = 2048
N_HEADS = 8
HEAD = 128
QK_W = N_HEADS * HEAD
QKV_W = 3 * QK_W
SC_W = 1024
CONV_A = 4
CONV_B = 3
CHUNK = 64
CHUNK_SHIFT = CHUNK.bit_length() - 1
HEAD_SHIFT = HEAD.bit_length() - 1
GROUP_HEADS = 4
DECAY_LANE = N_HEADS
N_EXPERTS = 64
N_GROUPS = 8
EXPERTS_PER_GROUP = 8
GROUP_SHIFT = EXPERTS_PER_GROUP.bit_length() - 1
D_FF = 512
MOE_ROWS = 128
TOKEN_TILE = 512
W_SLOTS = 2
X_SLOTS = 3

COL_QKV = 0
COL_BCX = 3072
COL_GA = 6144
COL_GB = 8192
COL_Z = 10240
COL_BA = 11264
BA_W = 256
PROJ_W = 11520
PROJ_TN = 1280

VMEM_LIMIT = 56 * 1024 * 1024


def _dot(a, b):
    return jnp.dot(a, b, preferred_element_type=f32)


def _dot_nt(a, b):
    return lax.dot_general(a, b, (((1,), (1,)), ((), ())), preferred_element_type=f32)


def _split(x, n):
    parts = []
    r = x
    for i in range(n):
        p = r.astype(bf16)
        parts.append(p)
        if i + 1 < n:
            r = r - p.astype(f32)
    return parts


def _dot_lsplit(x, m, n=3):
    rows = x.shape[0]
    d = _dot(jnp.concatenate(_split(x, n), axis=0), m)
    acc = d[0:rows]
    for i in range(1, n):
        acc = acc + d[i * rows:(i + 1) * rows]
    return acc


def _dot_rsplit(m, x, n=3):
    cols = x.shape[1]
    d = _dot(m, jnp.concatenate(_split(x, n), axis=1))
    acc = d[:, 0:cols]
    for i in range(1, n):
        acc = acc + d[:, i * cols:(i + 1) * cols]
    return acc


_sigmoid = jax.nn.sigmoid


def _silu(x):
    return x * _sigmoid(x)


def _softplus(x):
    return jnp.maximum(x, 0.0) + jnp.log(1.0 + jnp.exp(-jnp.abs(x)))


def _cparams(sem):
    return pltpu.CompilerParams(dimension_semantics=sem, vmem_limit_bytes=VMEM_LIMIT)


def _inproj_kernel(x_ref, nw_ref, w_ref, o_ref, h_ref, *, rows):
    @pl.when(pl.program_id(1) == 0)
    def _():
        def body(r, c):
            sl = pl.ds(pl.multiple_of(r * rows, rows), rows)
            x = x_ref[sl, :]
            ms = jnp.mean(x * x, axis=-1, keepdims=True)
            h_ref[sl, :] = (x * lax.rsqrt(ms + EPS) * nw_ref[...]).astype(bf16)
            return c
        lax.fori_loop(0, x_ref.shape[0] // rows, body, 0)

    o_ref[...] = _dot_nt(h_ref[...], w_ref[...])


def _inproj(x2d, norm_w, w_bf16):
    n = x2d.shape[0]
    tm = min(1024, n)
    assert n % tm == 0 and PROJ_W % PROJ_TN == 0
    return pl.pallas_call(
        functools.partial(_inproj_kernel, rows=min(128, tm)),
        out_shape=jax.ShapeDtypeStruct((n, PROJ_W), f32),
        grid=(n // tm, PROJ_W // PROJ_TN),
        in_specs=[pl.BlockSpec((tm, D_MODEL), lambda i, j: (i, 0)),
                  pl.BlockSpec((1, D_MODEL), lambda i, j: (0, 0)),
                  pl.BlockSpec((PROJ_TN, D_MODEL), lambda i, j: (j, 0))],
        out_specs=pl.BlockSpec((tm, PROJ_TN), lambda i, j: (i, j)),
        scratch_shapes=[pltpu.VMEM((tm, D_MODEL), bf16)],
        compiler_params=_cparams(("arbitrary", "arbitrary")),
        name="inproj",
    )(x2d, norm_w.reshape(1, D_MODEL), w_bf16)


CONV_TILES = 3
CONV_COLS = 2 * HEAD
CONV_ROWS = 128
CONV_PAD_W = CONV_TILES * PROJ_TN


def _qkv_kind(col):
    return "q" if col < QK_W else "k" if col < 2 * QK_W else "v" if col < QKV_W else "raw"


def _inproj_conv_kernel(x_ref, nw_ref, w_hbm, cw_ref, o_ref, tail_ref, h_ref, hist_ref, raw_ref, w_buf, w_sem, *,
                        rows, tiles_per_seq):
    i = pl.program_id(0)
    j = pl.program_id(1)
    nj = pl.num_programs(1)
    tm = x_ref.shape[0]
    step = i * nj + j
    n_steps = pl.num_programs(0) * nj

    def w_copy(st, slot):
        w_rows = pl.ds(pl.multiple_of(lax.rem(st, nj) * PROJ_TN, PROJ_TN), PROJ_TN)
        return pltpu.make_async_copy(w_hbm.at[w_rows], w_buf.at[slot], w_sem.at[slot])

    @pl.when(step == 0)
    def _():
        for a in range(X_SLOTS - 1):
            @pl.when(a < n_steps)
            def _():
                w_copy(a, a).start()

    @pl.when(step + (X_SLOTS - 1) < n_steps)
    def _():
        w_copy(step + (X_SLOTS - 1), lax.rem(step + (X_SLOTS - 1), X_SLOTS)).start()

    w_slot = lax.rem(step, X_SLOTS)
    w_copy(step, w_slot).wait()
    w_ref = w_buf.at[w_slot]

    @pl.when(j == 0)
    def _():
        def body(r, c):
            sl = pl.ds(pl.multiple_of(r * rows, rows), rows)
            x = x_ref[sl, :]
            ms = jnp.mean(x * x, axis=-1, keepdims=True)
            h_ref[sl, :] = (x * lax.rsqrt(ms + EPS) * nw_ref[...]).astype(bf16)
            return c
        lax.fori_loop(0, tm // rows, body, 0)

    @pl.when((i == 0) & (j == 0))
    def _():
        hist_ref[...] = jnp.zeros(hist_ref.shape, f32)

    @pl.when(j >= CONV_TILES)
    def _():
        o_ref[...] = _dot_nt(h_ref[...], w_ref[...])

    seq_start = lax.rem(i, tiles_per_seq) == 0
    for jj in range(CONV_TILES):
        @pl.when(j == jj)
        def _():
            def matmul_chunk(idx, c0):
                raw_ref[idx % 2] = _dot_nt(h_ref[...], w_ref[c0:c0 + CONV_COLS, :])

            def conv_chunk(idx, c0):
                cs = slice(c0, c0 + CONV_COLS)
                raw = raw_ref.at[idx % 2]
                tail = raw[tm - 8:tm, :]
                tail_ref[0, :, cs] = tail
                kinds = [_qkv_kind(jj * PROJ_TN + c0 + g * HEAD) for g in range(CONV_COLS // HEAD)]
                if kinds[0] == "raw":
                    o_ref[:, cs] = raw[...]
                    return
                hist = jnp.where(seq_start, 0.0, hist_ref[jj, :, cs])
                for rc in range(tm // CONV_ROWS):
                    r0 = rc * CONV_ROWS
                    if rc > 0:
                        xe = raw[r0 - 8:r0 + CONV_ROWS, :]
                    else:
                        xe = jnp.concatenate([hist, raw[0:CONV_ROWS, :]], axis=0)
                    acc = pltpu.roll(xe, 3, axis=0)[8:] * cw_ref[0:1, cs]
                    acc = acc + pltpu.roll(xe, 2, axis=0)[8:] * cw_ref[1:2, cs]
                    acc = acc + pltpu.roll(xe, 1, axis=0)[8:] * cw_ref[2:3, cs]
                    acc = acc + xe[8:] * cw_ref[3:4, cs]
                    act = _silu(acc)
                    for g, kind in enumerate(kinds):
                        ah = act[:, g * HEAD:(g + 1) * HEAD]
                        if kind != "v":
                            ss = jnp.sum(ah * ah, axis=-1, keepdims=True)
                            inv = lax.rsqrt(ss + EPS)
                            ah = ah * (inv * (HEAD ** -0.5) if kind == "q" else inv)
                        o_ref[r0:r0 + CONV_ROWS, c0 + g * HEAD:c0 + (g + 1) * HEAD] = ah
                hist_ref[jj, :, cs] = tail

            chunks = list(range(0, PROJ_TN, CONV_COLS))
            matmul_chunk(0, chunks[0])
            for idx in range(1, len(chunks)):
                matmul_chunk(idx, chunks[idx])
                conv_chunk(idx - 1, chunks[idx - 1])
            conv_chunk(len(chunks) - 1, chunks[-1])


def _inproj_conv(x2d, norm_w, w_bf16, cwa, seq_len):
    n = x2d.shape[0]
    tm = min(1024, seq_len)
    assert n % tm == 0 and seq_len % tm == 0 and PROJ_W % PROJ_TN == 0 and tm % CONV_ROWS == 0
    assert QKV_W % CONV_COLS == 0 and PROJ_TN % CONV_COLS == 0
    cw_pad = jnp.zeros((CONV_A, CONV_PAD_W), f32).at[:, 0:QKV_W].set(cwa)
    last = CONV_TILES - 1
    return pl.pallas_call(
        functools.partial(_inproj_conv_kernel, rows=min(128, tm), tiles_per_seq=seq_len // tm),
        out_shape=(jax.ShapeDtypeStruct((n, PROJ_W), f32),
                   jax.ShapeDtypeStruct((n // tm, 8, CONV_PAD_W), f32)),
        grid=(n // tm, PROJ_W // PROJ_TN),
        in_specs=[pl.BlockSpec((tm, D_MODEL), lambda i, j: (i, 0)),
                  pl.BlockSpec((1, D_MODEL), lambda i, j: (0, 0)),
                  pl.BlockSpec(memory_space=pl.ANY),
                  pl.BlockSpec((CONV_A, PROJ_TN), lambda i, j: (0, jnp.minimum(j, last)))],
        out_specs=(pl.BlockSpec((tm, PROJ_TN), lambda i, j: (i, j)),
                   pl.BlockSpec((1, 8, PROJ_TN), lambda i, j: (i, 0, jnp.minimum(j, last)))),
        scratch_shapes=[pltpu.VMEM((tm, D_MODEL), bf16), pltpu.VMEM((CONV_TILES, 8, PROJ_TN), f32),
                        pltpu.VMEM((2, tm, CONV_COLS), f32),
                        pltpu.VMEM((X_SLOTS, PROJ_TN, D_MODEL), bf16), pltpu.SemaphoreType.DMA((X_SLOTS,))],
        compiler_params=_cparams(("arbitrary", "arbitrary")),
        name="inproj_conv",
    )(x2d, norm_w.reshape(1, D_MODEL), w_bf16, cw_pad)


W_IN_COLS = 11280
WPREP_TN = 1024
WPREP_SHIFT = 16
WP_BCX = COL_BCX // WPREP_TN
WP_Z = COL_Z // WPREP_TN
WP_BA = COL_BA // WPREP_TN
WP_SRC_Z = QKV_W // WPREP_TN
WP_SRC_BA = WP_SRC_Z + 1


def _wprep_kernel(a_ref, b_ref, o_ref):
    j = pl.program_id(0)
    keep = WPREP_TN - WPREP_SHIFT

    @pl.when((j < WP_BCX) | (j == WP_Z))
    def _():
        o_ref[...] = a_ref[...].astype(bf16)

    @pl.when((j >= WP_BCX) & (j < WP_Z))
    def _():
        o_ref[0:keep, :] = a_ref[WPREP_SHIFT:WPREP_TN, :].astype(bf16)
        o_ref[keep:WPREP_TN, :] = b_ref[...].astype(bf16)

    @pl.when(j == WP_BA)
    def _():
        o_ref[0:WPREP_SHIFT, :] = a_ref[0:WPREP_SHIFT, :].astype(bf16)
        o_ref[WPREP_SHIFT:WPREP_TN, :] = jnp.zeros((keep, D_MODEL), bf16)


def _wprep(w_in_t):
    assert w_in_t.shape == (W_IN_COLS, D_MODEL) and 2 * N_HEADS == WPREP_SHIFT
    n_blk = pl.cdiv(PROJ_W, WPREP_TN)

    def a_map(j):
        return (jnp.where(j < WP_BCX, j, jnp.where(j < WP_Z, j + 1, jnp.where(j == WP_Z, WP_SRC_Z, WP_SRC_BA))), 0)

    def b_map(j):
        return (jnp.minimum((WPREP_TN // WPREP_SHIFT) * (j + 2), W_IN_COLS // WPREP_SHIFT - 1), 0)

    return pl.pallas_call(
        _wprep_kernel,
        out_shape=jax.ShapeDtypeStruct((PROJ_W, D_MODEL), bf16),
        grid=(n_blk,),
        in_specs=[pl.BlockSpec((WPREP_TN, D_MODEL), a_map),
                  pl.BlockSpec((WPREP_SHIFT, D_MODEL), b_map)],
        out_specs=pl.BlockSpec((WPREP_TN, D_MODEL), lambda j: (j, 0)),
        compiler_params=_cparams(("arbitrary",)),
        name="wprep",
    )(w_in_t, w_in_t)


def _head_l2norm(a, scale):
    outs = []
    for h in range(N_HEADS):
        ah = a[:, h * HEAD:(h + 1) * HEAD]
        ss = jnp.sum(ah * ah, axis=-1, keepdims=True)
        n = ah * lax.rsqrt(ss + EPS)
        outs.append(n * scale if scale != 1.0 else n)
    return outs


def _delta_prompt_kernel(qkv_ref, bcx_ref, z_ref, ba_ref, cwb_ref, alog_ref, dtb_ref, onw_ref,
                         e64_ref,
                         o_ref, y_ref, snew_ref, ncb_ref,
                         s_ref, xb_ref, *, nb_step):
    C = CHUNK
    G = GROUP_HEADS
    R = G * C
    t = pl.program_id(1)
    nt = pl.num_programs(1)

    @pl.when(t == 0)
    def _():
        s_ref[...] = jnp.zeros(s_ref.shape, f32)
        xb_ref[:, 0:8, :] = jnp.zeros((nb_step, 8, SC_W), f32)

    rr = lax.broadcasted_iota(i32, (R, R), 0)
    cc = lax.broadcasted_iota(i32, (R, R), 1)
    same_bf = jnp.where((rr >> CHUNK_SHIFT) == (cc >> CHUNK_SHIFT), 1.0, 0.0).astype(bf16)
    r2 = lax.broadcasted_iota(i32, (R, G * HEAD), 0)
    c2 = lax.broadcasted_iota(i32, (R, G * HEAD), 1)
    bdmask = (r2 >> CHUNK_SHIFT) == (c2 >> HEAD_SHIFT)
    r3 = lax.broadcasted_iota(i32, (C, C), 0)
    c3 = lax.broadcasted_iota(i32, (C, C), 1)
    ltri = jnp.where(r3 >= c3, 1.0, 0.0).astype(bf16)
    r4 = lax.broadcasted_iota(i32, (C, R), 0)
    c4 = lax.broadcasted_iota(i32, (C, R), 1)
    ident_t = r4 == (c4 & (C - 1))
    incl_p = r4 >= (c4 & (C - 1))
    strict_p = r4 > (c4 & (C - 1))
    hblk = c4 >> CHUNK_SHIFT
    ones8 = jnp.ones((8, C), bf16)

    nbs = range(nb_step)
    units = [(nb, g) for nb in nbs for g in range(N_HEADS // G)]
    heads = lambda g: range(g * G, (g + 1) * G)

    qn = [[qkv_ref[nb, :, h * HEAD:(h + 1) * HEAD] for h in range(N_HEADS)] for nb in nbs]
    kn = [[qkv_ref[nb, :, QK_W + h * HEAD:QK_W + (h + 1) * HEAD] for h in range(N_HEADS)] for nb in nbs]
    vv = [qkv_ref[nb, :, 2 * QK_W:3 * QK_W] for nb in nbs]

    bts = [ba_ref[nb, :, 0:LANE] for nb in nbs]
    beta_all = [_sigmoid(bt) for bt in bts]
    g_all = [-(jnp.exp(alog_ref[:, 0:LANE]) * _softplus(bt + dtb_ref[:, 0:LANE])) for bt in bts]
    gc_small = [_dot_rsplit(ltri, ga) for ga in g_all]
    gl_small = [gc[C - 1:C, :] for gc in gc_small]

    k_st, q_st, kb, vb, kbg, qd, kd, gc_col = ({} for _ in range(8))
    for u in units:
        nb, g = u
        hs = heads(g)
        k_st[u] = jnp.concatenate([kn[nb][h] for h in hs], axis=0)
        q_st[u] = jnp.concatenate([qn[nb][h] for h in hs], axis=0)
        v_st = jnp.concatenate([vv[nb][:, h * HEAD:(h + 1) * HEAD] for h in hs], axis=0)
        beta_col = jnp.concatenate([beta_all[nb][:, h:h + 1] for h in hs], axis=0)
        gc_col[u] = jnp.concatenate([gc_small[nb][:, DECAY_LANE + h:DECAY_LANE + h + 1] for h in hs], axis=0)
        gl_col = jnp.concatenate(
            [jnp.broadcast_to(gl_small[nb][:, DECAY_LANE + h:DECAY_LANE + h + 1], (C, 1)) for h in hs], axis=0)
        kb[u] = k_st[u] * beta_col
        vb[u] = v_st * beta_col
        egc = jnp.exp(gc_col[u])
        kbg[u] = kb[u] * egc
        qd[u] = q_st[u] * egc
        kd[u] = k_st[u] * jnp.exp(gl_col - gc_col[u])

    gx = {u: _dot_lsplit(gc_small[u[0]], e64_ref[u[1], 0:LANE, :]) for u in units}
    crow = {u: _dot_rsplit(ones8, jnp.where(ident_t, gx[u], 0.0))[0:1, :] for u in units}
    a = {u: _dot_nt(jnp.concatenate([kb[u], q_st[u]], axis=0).astype(bf16), k_st[u].astype(bf16))
         for u in units}
    in_blk = [hblk == h for h in range(G - 1)]

    def pack(x):
        out = x[(G - 1) * C:G * C]
        for h in reversed(range(G - 1)):
            out = jnp.where(in_blk[h], x[h * C:(h + 1) * C], out)
        return out

    def expand(xp):
        return jnp.concatenate([xp.astype(bf16)] * G, axis=0) * same_bf

    dec = {u: jnp.where(incl_p, jnp.exp(jnp.where(incl_p, gx[u] - crow[u], 0.0)), 0.0) for u in units}
    nm = {u: jnp.where(strict_p, -(pack(a[u][0:R]) * dec[u]), 0.0) for u in units}
    qkm = {u: expand(pack(a[u][R:2 * R]) * dec[u]) for u in units}

    p = {u: jnp.where(ident_t, 1.0, 0.0) + nm[u] for u in units}
    nk = {u: _dot(nm[u].astype(bf16), expand(nm[u])) for u in units}
    for _ in range(4):
        for u in units:
            x = _dot(jnp.concatenate([p[u], nk[u]], axis=0).astype(bf16), expand(nk[u]))
            p[u] = p[u] + x[0:C]
            nk[u] = x[C:2 * C]
    for u in units:
        p[u] = p[u] + _dot(p[u].astype(bf16), expand(nk[u]))
    uw = {u: _dot(expand(p[u]), jnp.concatenate([vb[u], kbg[u]], axis=1).astype(bf16)) for u in units}

    ws = {}
    for u in units:
        nb, g = u
        for j, h in enumerate(heads(g)):
            sh = s_ref[nb, :, h * HEAD:(h + 1) * HEAD]
            lhs = jnp.concatenate([uw[u][j * C:(j + 1) * C, HEAD:2 * HEAD], qd[u][j * C:(j + 1) * C]], axis=0)
            ws[u, j] = _dot(lhs.astype(bf16), sh.astype(bf16))
    o_heads = {}
    for u in units:
        nb, g = u
        vnew_st = jnp.concatenate([uw[u][j * C:(j + 1) * C, 0:HEAD] - ws[u, j][0:C] for j in range(G)], axis=0)
        o_st = (jnp.concatenate([ws[u, j][C:2 * C] for j in range(G)], axis=0)
                + _dot(qkm[u], vnew_st.astype(bf16)))
        vbd = jnp.where(bdmask, jnp.concatenate([vnew_st] * G, axis=1), 0.0)
        lo = g * G * HEAD
        hi = lo + G * HEAD
        gl_row = jnp.concatenate(
            [jnp.broadcast_to(jnp.exp(gl_small[nb][:, DECAY_LANE + h:DECAY_LANE + h + 1]), (1, HEAD)) for h in heads(g)], axis=1)
        s_ref[nb, :, lo:hi] = s_ref[nb, :, lo:hi] * gl_row + _dot(kd[u].T.astype(bf16), vbd.astype(bf16))
        for j, h in enumerate(heads(g)):
            o_heads[nb, h] = o_st[j * C:(j + 1) * C]

    for nb in nbs:
        zt = z_ref[nb]
        for h in range(N_HEADS):
            oh = o_heads[nb, h]
            ms = jnp.mean(oh * oh, axis=-1, keepdims=True)
            zh = zt[:, h * HEAD:(h + 1) * HEAD]
            on = oh * lax.rsqrt(ms + EPS) * onw_ref[...] * _silu(zh)
            o_ref[nb, :, h * HEAD:(h + 1) * HEAD] = on.astype(bf16)

    for nb in nbs:
        bcx = bcx_ref[nb]
        cx = bcx[:, SC_W:2 * SC_W] * bcx[:, 2 * SC_W:3 * SC_W]
        xb_ref[nb, 8:8 + C, :] = cx
        ce = xb_ref[nb]
        cv = pltpu.roll(ce, 2, axis=0)[8:8 + C] * cwb_ref[0:1, :]
        cv = cv + pltpu.roll(ce, 1, axis=0)[8:8 + C] * cwb_ref[1:2, :]
        cv = cv + cx * cwb_ref[2:3, :]
        y_ref[nb] = (bcx[:, 0:SC_W] * cv).astype(bf16)
        xb_ref[nb, 0:8, :] = xb_ref[nb, C:C + 8, :]

    @pl.when(t == nt - 1)
    def _():
        for nb in range(nb_step):
            for h in range(N_HEADS):
                snew_ref[nb, h] = s_ref[nb, :, h * HEAD:(h + 1) * HEAD]
            ncb_ref[nb] = xb_ref[nb, 6:8, :]


def _expand_consts():
    lane = jnp.arange(BA_W)[:, None]
    col = jnp.arange(QK_W)[None, :]
    eb = (lane == (col >> HEAD_SHIFT)).astype(bf16)
    eg = (lane == (DECAY_LANE + (col >> HEAD_SHIFT))).astype(bf16)
    col64 = jnp.arange(GROUP_HEADS * CHUNK)[None, :]
    e64 = jnp.stack([(lane == (DECAY_LANE + g * GROUP_HEADS + (col64 >> CHUNK_SHIFT))).astype(bf16)
                     for g in range(N_HEADS // GROUP_HEADS)], axis=0)
    return eb, eg, e64


def _delta_prompt(proj3, cwb, alog_row, dtb_row, onw_row, nb_step):
    b, t, _ = proj3.shape
    assert t % CHUNK == 0 and b % nb_step == 0
    _, _, e64 = _expand_consts()
    c = CHUNK
    const2 = lambda bi, ti: (0, 0)
    outs = pl.pallas_call(
        functools.partial(_delta_prompt_kernel, nb_step=nb_step),
        out_shape=(jax.ShapeDtypeStruct((b, t, QK_W), bf16),
                   jax.ShapeDtypeStruct((b, t, SC_W), bf16),
                   jax.ShapeDtypeStruct((b, N_HEADS, HEAD, HEAD), f32),
                   jax.ShapeDtypeStruct((b, CONV_B - 1, SC_W), f32)),
        grid=(b // nb_step, t // c),
        in_specs=[pl.BlockSpec((nb_step, c, QKV_W), lambda bi, ti: (bi, ti, COL_QKV // QKV_W)),
                  pl.BlockSpec((nb_step, c, QKV_W), lambda bi, ti: (bi, ti, COL_BCX // QKV_W)),
                  pl.BlockSpec((nb_step, c, QK_W), lambda bi, ti: (bi, ti, COL_Z // QK_W)),
                  pl.BlockSpec((nb_step, c, BA_W), lambda bi, ti: (bi, ti, COL_BA // BA_W)),
                  pl.BlockSpec((CONV_B, SC_W), const2),
                  pl.BlockSpec((1, BA_W), const2),
                  pl.BlockSpec((1, BA_W), const2),
                  pl.BlockSpec((1, HEAD), const2),
                  pl.BlockSpec((N_HEADS // GROUP_HEADS, BA_W, GROUP_HEADS * CHUNK), lambda bi, ti: (0, 0, 0))],
        out_specs=(pl.BlockSpec((nb_step, c, QK_W), lambda bi, ti: (bi, ti, 0)),
                   pl.BlockSpec((nb_step, c, SC_W), lambda bi, ti: (bi, ti, 0)),
                   pl.BlockSpec((nb_step, N_HEADS, HEAD, HEAD), lambda bi, ti: (bi, 0, 0, 0)),
                   pl.BlockSpec((nb_step, CONV_B - 1, SC_W), lambda bi, ti: (bi, 0, 0))),
        scratch_shapes=[pltpu.VMEM((nb_step, HEAD, QK_W), f32),
                        pltpu.VMEM((nb_step, 8 + c, SC_W), f32)],
        compiler_params=_cparams(("arbitrary", "arbitrary")),
        name="delta_prompt",
    )(proj3, proj3, proj3, proj3, cwb, alog_row, dtb_row, onw_row, e64)
    return outs


def _sample_prep_kernel(p_ref, bufa_ref, bufb_ref, cwa_ref, cwb_ref, alog_ref, dtb_ref, eb_ref, eg_ref,
                        q_ref, k_ref, v_ref, beta_ref, eg_out_ref, z_ref, y_ref, nbufa_ref, nbufb_ref):
    def put_heads(ref, a):
        for h in range(N_HEADS):
            ref[:, h, :] = a[:, h * HEAD:(h + 1) * HEAD]

    def conv_sec(lo):
        hi = lo + QK_W
        raw = p_ref[:, COL_QKV + lo:COL_QKV + hi]
        acc = bufa_ref[0, :, lo:hi] * cwa_ref[0:1, lo:hi]
        acc = acc + bufa_ref[1, :, lo:hi] * cwa_ref[1:2, lo:hi]
        acc = acc + bufa_ref[2, :, lo:hi] * cwa_ref[2:3, lo:hi]
        acc = acc + raw * cwa_ref[3:4, lo:hi]
        nbufa_ref[0, :, lo:hi] = bufa_ref[1, :, lo:hi]
        nbufa_ref[1, :, lo:hi] = bufa_ref[2, :, lo:hi]
        nbufa_ref[2, :, lo:hi] = raw
        return _silu(acc)

    qn = _head_l2norm(conv_sec(0), HEAD ** -0.5)
    kn = _head_l2norm(conv_sec(QK_W), 1.0)
    for h in range(N_HEADS):
        q_ref[:, h, :] = qn[h]
        k_ref[:, h, :] = kn[h]
    put_heads(v_ref, conv_sec(2 * QK_W))
    put_heads(z_ref, p_ref[:, COL_Z:COL_Z + QK_W])

    bt = p_ref[:, COL_BA:COL_BA + BA_W]
    beta_all = _sigmoid(bt)
    g_all = -(jnp.exp(alog_ref[...]) * _softplus(bt + dtb_ref[...]))
    put_heads(beta_ref, _dot_lsplit(beta_all, eb_ref[...]))
    put_heads(eg_out_ref, jnp.exp(_dot_lsplit(g_all, eg_ref[...])))

    bg = p_ref[:, COL_BCX:COL_BCX + SC_W]
    cx = p_ref[:, COL_BCX + SC_W:COL_BCX + 2 * SC_W] * p_ref[:, COL_BCX + 2 * SC_W:COL_BCX + 3 * SC_W]
    cv = bufb_ref[0] * cwb_ref[0:1, :]
    cv = cv + bufb_ref[1] * cwb_ref[1:2, :]
    cv = cv + cx * cwb_ref[2:3, :]
    y_ref[...] = (bg * cv).astype(bf16)
    nbufb_ref[0] = bufb_ref[1]
    nbufb_ref[1] = cx


def _sample_prep(proj_s, bufa_t, bufb_t, cwa, cwb, alog_row, dtb_row):
    n = proj_s.shape[0]
    eb, eg, _ = _expand_consts()
    row = jax.ShapeDtypeStruct((n, N_HEADS, HEAD), f32)
    return pl.pallas_call(
        _sample_prep_kernel,
        out_shape=(row, row, row, row, row, row,
                   jax.ShapeDtypeStruct((n, SC_W), bf16),
                   jax.ShapeDtypeStruct((CONV_A - 1, n, QKV_W), f32),
                   jax.ShapeDtypeStruct((CONV_B - 1, n, SC_W), f32)),
        compiler_params=pltpu.CompilerParams(vmem_limit_bytes=VMEM_LIMIT),
        name="sample_prep",
    )(proj_s, bufa_t, bufb_t, cwa, cwb, alog_row, dtb_row, eb, eg)


def _sample_step_kernel(s_ref, q_ref, k_ref, v_ref, beta_ref, eg_ref, z_ref, onw_ref,
                        snew_ref, o_ref, *, bb):
    w = N_HEADS * HEAD
    r8 = lax.broadcasted_iota(i32, (N_HEADS, w), 0)
    c8 = lax.broadcasted_iota(i32, (N_HEADS, w), 1)
    mask8 = r8 == (c8 >> HEAD_SHIFT)
    zpad_k = jnp.zeros((HEAD - N_HEADS, HEAD), f32)
    hb = lambda h: slice(h * HEAD, (h + 1) * HEAD)
    bs = range(bb)
    s_dec, k8s, kts = [], [], []
    for b in bs:
        s_all = jnp.concatenate([s_ref[b, h] for h in range(N_HEADS)], axis=1)
        eg8 = eg_ref[b]
        eg_row = jnp.concatenate([eg8[h:h + 1, :] for h in range(N_HEADS)], axis=1)
        s_dec.append(s_all * eg_row)
        k8s.append(k_ref[b])
        kts.append(jnp.concatenate([k8s[b], zpad_k], axis=0).T)
    xs = [_dot(k8s[b].astype(bf16), s_dec[b].astype(bf16)) for b in bs]
    s_new = []
    for b in bs:
        vb8, bt8 = v_ref[b], beta_ref[b]
        upd = [kts[b][:, h:h + 1] * ((vb8[h:h + 1, :] - xs[b][h:h + 1, hb(h)]) * bt8[h:h + 1, :])
               for h in range(N_HEADS)]
        s_new.append(s_dec[b] + jnp.concatenate(upd, axis=1))
    ys = [_dot(q_ref[b].astype(bf16), s_new[b].astype(bf16)) for b in bs]
    for b in bs:
        yv = jnp.where(mask8, ys[b], 0.0)
        o8 = yv[:, 0:HEAD]
        for j in range(1, N_HEADS):
            o8 = o8 + yv[:, j * HEAD:(j + 1) * HEAD]
        ms = jnp.mean(o8 * o8, axis=-1, keepdims=True)
        o_ref[b] = o8 * lax.rsqrt(ms + EPS) * onw_ref[...] * _silu(z_ref[b])
        for h in range(N_HEADS):
            snew_ref[b, h] = s_new[b][:, hb(h)]


def _sample_step(state, q, k, v, beta, eg, z, onw_row, bb=8):
    n = state.shape[0]
    assert n % bb == 0
    hspec = pl.BlockSpec((bb, N_HEADS, HEAD), lambda i: (i, 0, 0))
    sspec = pl.BlockSpec((bb, N_HEADS, HEAD, HEAD), lambda i: (i, 0, 0, 0))
    return pl.pallas_call(
        functools.partial(_sample_step_kernel, bb=bb),
        out_shape=(jax.ShapeDtypeStruct(state.shape, f32),
                   jax.ShapeDtypeStruct((n, N_HEADS, HEAD), f32)),
        grid=(n // bb,),
        in_specs=[sspec, hspec, hspec, hspec, hspec, hspec, hspec, pl.BlockSpec((1, HEAD), lambda i: (0, 0))],
        out_specs=(sspec, hspec),
        compiler_params=_cparams(("arbitrary",)),
        name="sample_step",
    )(state, q, k, v, beta, eg, z, onw_row)


def _mix_route_kernel(x_ref, o_ref, y_ref, ga_ref, gb_ref, wa_ref, wb_ref, wo_ref, n2_ref,
                      rwh_ref, rwl_ref, rb_ref, cnt_in_ref, x1_ref, h2_ref, mi_ref, mw_ref, cnt_ref):
    i = pl.program_id(0)
    tm = x_ref.shape[0]

    @pl.when(i == 0)
    def _():
        cnt_ref[...] = cnt_in_ref[...]

    oa = _dot(o_ref[...], wa_ref[...])
    ob = _dot(y_ref[...], wb_ref[...])
    merged = _sigmoid(ga_ref[...]) * oa + _sigmoid(gb_ref[...]) * ob
    x1 = x_ref[...] + _dot(merged.astype(bf16), wo_ref[...])
    x1_ref[...] = x1
    ms = jnp.mean(x1 * x1, axis=-1, keepdims=True)
    h2 = x1 * lax.rsqrt(ms + EPS) * n2_ref[...]
    h2_ref[...] = h2

    h_hi, h_lo = _split(h2, 2)
    logits = _dot(h_hi, rwh_ref[...]) + _dot(h_hi, rwl_ref[...]) + _dot(h_lo, rwh_ref[...]) + rb_ref[...]

    lane = lax.broadcasted_iota(i32, (tm, LANE), 1)
    lanef = lane.astype(f32)
    neg = jnp.float32(-jnp.inf)
    big = jnp.float32(1e9)
    gmask = (lane >= N_EXPERTS) & (lane < N_EXPERTS + N_GROUPS)
    gl = jnp.where(gmask, logits, neg)
    gmax = jnp.max(gl, axis=-1, keepdims=True)
    gidx = jnp.min(jnp.where(gl == gmax, lanef - N_EXPERTS, big), axis=-1, keepdims=True)
    gsum = jnp.sum(jnp.where(gmask, jnp.exp(gl - gmax), 0.0), axis=-1, keepdims=True)
    gprob = 1.0 / gsum

    emask = (lane < N_EXPERTS) & ((lane >> GROUP_SHIFT).astype(f32) == gidx)
    el = jnp.where(emask, logits, neg)
    emax = jnp.max(el, axis=-1, keepdims=True)
    pe = jnp.where(emask, jnp.exp(el - emax), 0.0)
    eprob = pe / jnp.sum(pe, axis=-1, keepdims=True)
    p1m = jnp.where(emask, eprob, -1.0)
    m1 = jnp.max(p1m, axis=-1, keepdims=True)
    i1 = jnp.min(jnp.where(p1m == m1, lanef, big), axis=-1, keepdims=True)
    p2m = jnp.where(lanef == i1, -1.0, p1m)
    m2 = jnp.max(p2m, axis=-1, keepdims=True)
    i2 = jnp.min(jnp.where(p2m == m2, lanef, big), axis=-1, keepdims=True)
    tot = m1 + m2
    c1 = m1 / tot * gprob
    c2 = m2 / tot * gprob

    oh1 = jnp.where(lanef == i1, 1.0, 0.0)
    oh2 = jnp.where(lanef == i2, 1.0, 0.0)
    ohs = oh1 + oh2
    rt = lax.broadcasted_iota(i32, (tm, tm), 0)
    ct = lax.broadcasted_iota(i32, (tm, tm), 1)
    lstrict = jnp.where(rt > ct, 1.0, 0.0).astype(bf16)
    cs = _dot(lstrict, ohs.astype(bf16)) + cnt_ref[...]
    rank1 = jnp.sum(cs * oh1, axis=-1, keepdims=True)
    rank2 = jnp.sum(cs * oh2, axis=-1, keepdims=True)
    cnt_ref[...] = cnt_ref[...] + jnp.sum(ohs, axis=0, keepdims=True)

    mi = jnp.where(lane == 0, i1, jnp.where(lane == 1, i2, jnp.where(lane == 2, rank1,
                                                                     jnp.where(lane == 3, rank2, 0.0))))
    mi_ref[...] = mi.astype(i32)
    mw_ref[...] = jnp.where(lane == 0, c1, jnp.where(lane == 1, c2, 0.0))


def _mix_route(x2d, o2d, y2d, proj2d, wa, wb, wo, n2_row, rwh, rwl, rb_row, cnt_in):
    n = x2d.shape[0]
    tm = min(256, n)
    assert n % tm == 0
    tok = lambda width: pl.BlockSpec((tm, width), lambda i: (i, 0))
    full = lambda a: pl.BlockSpec(a.shape, lambda i: (0,) * a.ndim)
    in_specs = [tok(D_MODEL), tok(QK_W), tok(SC_W),
                pl.BlockSpec((tm, D_MODEL), lambda i: (i, COL_GA // D_MODEL)),
                pl.BlockSpec((tm, D_MODEL), lambda i: (i, COL_GB // D_MODEL)),
                full(wa), full(wb), full(wo), full(n2_row), full(rwh), full(rwl), full(rb_row), full(cnt_in)]
    out_shape = (jax.ShapeDtypeStruct((n, D_MODEL), f32),
                 jax.ShapeDtypeStruct((n, D_MODEL), f32),
                 jax.ShapeDtypeStruct((n, LANE), i32),
                 jax.ShapeDtypeStruct((n, LANE), f32),
                 jax.ShapeDtypeStruct((1, LANE), f32))
    out_specs = (tok(D_MODEL), tok(D_MODEL), tok(LANE), tok(LANE),
                 pl.BlockSpec((1, LANE), lambda i: (0, 0)))
    return pl.pallas_call(
        _mix_route_kernel,
        out_shape=out_shape,
        grid=(n // tm,),
        in_specs=in_specs,
        out_specs=out_specs,
        compiler_params=_cparams(("arbitrary",)),
        name="mix_route",
    )(x2d, o2d, y2d, proj2d, proj2d, wa, wb, wo, n2_row, rwh, rwl, rb_row, cnt_in)


MI_W = 4
SUBLANE = 8


def _dest_kernel(mi_ref, starts_ref, o_ref):
    mi = mi_ref[...]
    lane = lax.broadcasted_iota(i32, mi.shape, 1)
    st = starts_ref[...]

    def first_row(e_col):
        return jnp.sum(jnp.where(lane == e_col, st, 0.0), axis=-1, keepdims=True).astype(i32)

    d0 = first_row(mi[:, 0:1]) + mi[:, 2:3]
    d1 = first_row(mi[:, 1:2]) + mi[:, 3:4]
    sh = SUBLANE.bit_length() - 1
    o_ref[...] = jnp.where(lane == 0, d0 >> sh, jnp.where(lane == 1, d0 & (SUBLANE - 1),
                           jnp.where(lane == 2, d1 >> sh, jnp.where(lane == 3, d1 & (SUBLANE - 1), 0))))


def _dest_rows(mi, starts_row):
    n = mi.shape[0]
    tm = min(1024, n)
    assert n % tm == 0
    return pl.pallas_call(
        _dest_kernel,
        out_shape=jax.ShapeDtypeStruct((n, LANE), i32),
        grid=(n // tm,),
        in_specs=[pl.BlockSpec((tm, LANE), lambda i: (i, 0)), pl.BlockSpec((1, LANE), lambda i: (0, 0))],
        out_specs=pl.BlockSpec((tm, LANE), lambda i: (i, 0)),
        compiler_params=_cparams(("arbitrary",)),
        name="moe_dest",
    )(mi, starts_row)


def _dispatch_kernel(mi_ref, hp_ref, hs_ref, xs_ref, sem, *, np_tiles):
    i = pl.program_id(0)

    def scatter_rows(h_ref):
        n_tiles = h_ref.shape[0]

        def start(t, c):
            for u in range(SUBLANE):
                rec = MI_W * (SUBLANE * t + u)
                for k in range(2):
                    dst = xs_ref.at[mi_ref[rec + 2 * k], pl.ds(mi_ref[rec + 2 * k + 1], 1)]
                    pltpu.make_async_copy(h_ref.at[t, pl.ds(u, 1)], dst, sem).start(priority=k)
            return c

        lax.fori_loop(0, n_tiles, start, 0)
        for k in range(2):
            pltpu.make_async_copy(h_ref, xs_ref.at[pl.ds(0, n_tiles)], sem).wait()

    @pl.when(i < np_tiles)
    def _():
        scatter_rows(hp_ref)

    @pl.when(i >= np_tiles)
    def _():
        scatter_rows(hs_ref)


def _dispatch(h2_p, h2_s, mi_flat):
    tm = TOKEN_TILE
    n_p, n_s = h2_p.shape[0], h2_s.shape[0]
    assert n_p % tm == 0 and n_s <= tm and n_s % SUBLANE == 0 and tm % SUBLANE == 0
    np_tiles = n_p // tm
    tiled = lambda a: a.reshape(a.shape[0] // SUBLANE, SUBLANE, D_MODEL)
    return pl.pallas_call(
        functools.partial(_dispatch_kernel, np_tiles=np_tiles),
        out_shape=jax.ShapeDtypeStruct((2 * (n_p + n_s) // SUBLANE, SUBLANE, D_MODEL), f32),
        grid=(np_tiles + 1,),
        in_specs=[pl.BlockSpec((MI_W * tm,), lambda i: (i,), memory_space=pltpu.SMEM),
                  pl.BlockSpec((tm // SUBLANE, SUBLANE, D_MODEL), lambda i: (jnp.minimum(i, np_tiles - 1), 0, 0)),
                  pl.BlockSpec((n_s // SUBLANE, SUBLANE, D_MODEL), lambda i: (0, 0, 0))],
        out_specs=pl.BlockSpec(memory_space=pl.ANY),
        scratch_shapes=[pltpu.SemaphoreType.DMA(())],
        compiler_params=_cparams(("arbitrary",)),
        name="moe_dispatch",
    )(mi_flat, tiled(h2_p), tiled(h2_s))


def _cast_rows(src_ref, dst_ref, col0=0, rows=256):
    width = src_ref.shape[1]

    def body(r, c):
        sl = pl.ds(pl.multiple_of(r * rows, rows), rows)
        dst_ref[sl, col0:col0 + width] = src_ref[sl, :].astype(bf16)
        return c
    lax.fori_loop(0, src_ref.shape[0] // rows, body, 0)


def _moe_kernel(blk_ref, lo_ref, hi_ref, first_ref, newe_ref, slot_ref, pre_ref, init_ref,
                x_hbm, wg_hbm, wu_hbm, wd_hbm, o_ref,
                wg_f, wu_f, wd_f, wgu_b, wd_b, sem, x_buf, x_sem):
    i = pl.program_id(0)
    n = pl.num_programs(0)
    lo = lo_ref[i]
    hi = hi_ref[i]

    def x_copy(item, slot):
        rows = pl.ds(pl.multiple_of(blk_ref[item] * MOE_ROWS, MOE_ROWS), MOE_ROWS)
        return pltpu.make_async_copy(x_hbm.at[rows], x_buf.at[slot], x_sem.at[slot])

    @pl.when(i == 0)
    def _():
        for a in range(X_SLOTS - 1):
            @pl.when(a < n)
            def _():
                x_copy(a, a).start()

    ahead = i + (X_SLOTS - 1)

    @pl.when(ahead < n)
    def _():
        x_copy(ahead, lax.rem(ahead, X_SLOTS)).start()

    x_slot = lax.rem(i, X_SLOTS)
    x_copy(i, x_slot).wait()

    def weight_copies(e, slot):
        return [pltpu.make_async_copy(wg_hbm.at[e], wg_f.at[slot], sem.at[slot, 0]),
                pltpu.make_async_copy(wu_hbm.at[e], wu_f.at[slot], sem.at[slot, 1]),
                pltpu.make_async_copy(wd_hbm.at[e], wd_f.at[slot], sem.at[slot, 2])]

    def start_weights(e, slot):
        for cp, prio in zip(weight_copies(e, slot), (0, 1, 1)):
            cp.start(priority=prio)

    @pl.when(i == 0)
    def _():
        start_weights(init_ref[0], 0)
        for k in range(1, W_SLOTS):
            @pl.when(init_ref[k] >= 0)
            def _():
                start_weights(init_ref[k], k)

    @pl.when(newe_ref[i] == 1)
    def _():
        slot = slot_ref[i]
        cg, cu, cd = weight_copies(0, slot)
        cg.wait()
        _cast_rows(wg_f.at[slot], wgu_b, 0)
        cu.wait()
        _cast_rows(wu_f.at[slot], wgu_b, D_FF)
        cd.wait()
        _cast_rows(wd_f.at[slot], wd_b)

        @pl.when(pre_ref[i] >= 0)
        def _():
            start_weights(pre_ref[i], slot)

    @pl.when(hi > lo)
    def _():
        x = x_buf[x_slot].astype(bf16)
        au = _dot(x, wgu_b[...])
        y = _dot((_silu(au[:, 0:D_FF]) * au[:, D_FF:2 * D_FF]).astype(bf16), wd_b[...])
        row = lax.broadcasted_iota(i32, y.shape, 0)
        ym = jnp.where((row >= lo) & (row < hi), y, 0.0)

        @pl.when(first_ref[i] == 1)
        def _():
            o_ref[...] = ym

        @pl.when(first_ref[i] == 0)
        def _():
            o_ref[...] = o_ref[...] + ym


def _moe(xs, w_gate, w_up, w_down, items):
    n_items = items[0].shape[0]
    rows = xs.shape[0]
    n_pref = len(items)
    xmap = lambda i, blk, *_: (blk[i], 0)
    grid_spec = pltpu.PrefetchScalarGridSpec(
        num_scalar_prefetch=n_pref,
        grid=(n_items,),
        in_specs=[pl.BlockSpec(memory_space=pl.ANY),
                  pl.BlockSpec(memory_space=pl.ANY),
                  pl.BlockSpec(memory_space=pl.ANY),
                  pl.BlockSpec(memory_space=pl.ANY)],
        out_specs=pl.BlockSpec((MOE_ROWS, D_MODEL), xmap),
        scratch_shapes=[pltpu.VMEM((W_SLOTS, D_MODEL, D_FF), f32), pltpu.VMEM((W_SLOTS, D_MODEL, D_FF), f32),
                        pltpu.VMEM((W_SLOTS, D_FF, D_MODEL), f32),
                        pltpu.VMEM((D_MODEL, 2 * D_FF), bf16), pltpu.VMEM((D_FF, D_MODEL), bf16),
                        pltpu.SemaphoreType.DMA((W_SLOTS, 3)),
                        pltpu.VMEM((X_SLOTS, MOE_ROWS, D_MODEL), f32), pltpu.SemaphoreType.DMA((X_SLOTS,))],
    )
    return pl.pallas_call(
        _moe_kernel,
        out_shape=jax.ShapeDtypeStruct((rows, D_MODEL), f32),
        grid_spec=grid_spec,
        compiler_params=_cparams(("arbitrary",)),
        name="moe_experts",
    )(*items, xs, w_gate, w_up, w_down)


def _combine_kernel(mi_ref, mi_next_ref, x1p_ref, mwp_ref, x1s_ref, mws_ref, fnw_ref, ys_ref,
                    yp_ref, ysm_ref, g_ref, sem, *, np_tiles):
    i = pl.program_id(0)
    tiles_p = x1p_ref.shape[0] // SUBLANE
    tiles_s = x1s_ref.shape[0] // SUBLANE
    slot = lax.rem(i, 2)

    def gather_rows(m_ref, dst_slot, n_tiles):
        def body(t, c):
            for u in range(SUBLANE):
                rec = MI_W * (SUBLANE * t + u)
                for k in range(2):
                    src = ys_ref.at[m_ref[rec + 2 * k], pl.ds(m_ref[rec + 2 * k + 1], 1)]
                    pltpu.make_async_copy(src, g_ref.at[dst_slot, k, t, pl.ds(u, 1)],
                                          sem.at[dst_slot]).start(priority=k)
            return c
        lax.fori_loop(0, n_tiles, body, 0)

    @pl.when(i == 0)
    def _():
        gather_rows(mi_ref, 0, tiles_p)

    @pl.when(i + 1 < np_tiles)
    def _():
        gather_rows(mi_next_ref, 1 - slot, tiles_p)

    @pl.when(i + 1 == np_tiles)
    def _():
        gather_rows(mi_next_ref, 1 - slot, tiles_s)

    def finish(x1_ref, mw_ref, out_ref, n_tiles):
        for k in range(2):
            pltpu.make_async_copy(ys_ref.at[pl.ds(0, n_tiles)], g_ref.at[slot, k, pl.ds(0, n_tiles)],
                                  sem.at[slot]).wait()
        rows = n_tiles * SUBLANE
        mw = mw_ref[...]
        g0 = g_ref[slot, 0, 0:n_tiles].reshape(rows, D_MODEL)
        g1 = g_ref[slot, 1, 0:n_tiles].reshape(rows, D_MODEL)
        x2 = x1_ref[...] + (g0 * mw[:, 0:1] + g1 * mw[:, 1:2])
        ms = jnp.mean(x2 * x2, axis=-1, keepdims=True)
        out_ref[...] = x2 * lax.rsqrt(ms + EPS) * fnw_ref[...]

    @pl.when(i < np_tiles)
    def _():
        finish(x1p_ref, mwp_ref, yp_ref, tiles_p)

    @pl.when(i >= np_tiles)
    def _():
        finish(x1s_ref, mws_ref, ysm_ref, tiles_s)


def _combine(x1_p, mw_p, x1_s, mw_s, fnw_row, ys3, mi_flat):
    tm = TOKEN_TILE
    n_p, n_s = x1_p.shape[0], x1_s.shape[0]
    assert n_p % tm == 0 and n_s <= tm and n_s % SUBLANE == 0
    np_tiles = n_p // tm
    ptile = lambda width: pl.BlockSpec((tm, width), lambda i: (jnp.minimum(i, np_tiles - 1), 0))
    stile = lambda width: pl.BlockSpec((n_s, width), lambda i: (0, 0))
    return pl.pallas_call(
        functools.partial(_combine_kernel, np_tiles=np_tiles),
        out_shape=(jax.ShapeDtypeStruct((n_p, D_MODEL), f32),
                   jax.ShapeDtypeStruct((n_s, D_MODEL), f32)),
        grid=(np_tiles + 1,),
        in_specs=[pl.BlockSpec((MI_W * tm,), lambda i: (i,), memory_space=pltpu.SMEM),
                  pl.BlockSpec((MI_W * tm,), lambda i: (jnp.minimum(i + 1, np_tiles),), memory_space=pltpu.SMEM),
                  ptile(D_MODEL), ptile(LANE), stile(D_MODEL), stile(LANE),
                  pl.BlockSpec((1, D_MODEL), lambda i: (0, 0)),
                  pl.BlockSpec(memory_space=pl.ANY)],
        out_specs=(ptile(D_MODEL), stile(D_MODEL)),
        scratch_shapes=[pltpu.VMEM((2, 2, tm // SUBLANE, SUBLANE, D_MODEL), f32), pltpu.SemaphoreType.DMA((2,))],
        compiler_params=_cparams(("arbitrary",)),
        name="moe_combine",
    )(mi_flat, mi_flat, x1_p, mw_p, x1_s, mw_s, fnw_row, ys3)


PLAN_ROWS = 256
N_ITEM_FIELDS = 7


def _plan_kernel(cnt_ref, items_ref, rows_ref, *, nblk):
    cnt = cnt_ref[...]
    lane1 = lax.broadcasted_iota(i32, (1, LANE), 1)
    in_e = lane1 < N_EXPERTS
    ri = lax.broadcasted_iota(i32, (LANE, LANE), 0)
    ci = lax.broadcasted_iota(i32, (LANE, LANE), 1)
    upper = jnp.where(ri <= ci, 1.0, 0.0).astype(bf16)

    def cumsum_lanes(v):
        return _dot_lsplit(jnp.broadcast_to(v, (8, LANE)), upper)[0:1, :]

    shift = MOE_ROWS.bit_length() - 1
    ends = cumsum_lanes(cnt)
    starts = ends - cnt
    act = cnt > 0.0
    first_blk = (starts.astype(i32) >> shift).astype(f32)
    last_blk = (jnp.maximum(ends - 1.0, 0.0).astype(i32) >> shift).astype(f32)
    nvis = jnp.where(act, last_blk - first_blk + 1.0, 0.0)
    vis_end = cumsum_lanes(nvis)
    vis_start = vis_end - nvis
    total = jnp.max(vis_end, axis=-1, keepdims=True)
    cum_act = cumsum_lanes(jnp.where(act, 1.0, 0.0))
    n_uniq = jnp.max(cum_act, axis=-1, keepdims=True)

    p = PLAN_ROWS
    lane = lax.broadcasted_iota(i32, (p, LANE), 1)
    idx = lax.broadcasted_iota(i32, (p, LANE), 0).astype(f32)
    idx1 = idx[:, 0:1]
    count_le = lambda row, col: jnp.sum(jnp.where((row <= col) & in_e, 1.0, 0.0), axis=-1, keepdims=True)
    e = jnp.minimum(count_le(vis_end, idx), N_EXPERTS - 1.0)
    onehot = lane.astype(f32) == e
    look = lambda tbl: jnp.sum(jnp.where(onehot, tbl, 0.0), axis=-1, keepdims=True)
    blk = look(first_blk) + idx1 - look(vis_start)
    lo = jnp.maximum(look(starts), blk * MOE_ROWS) - blk * MOE_ROWS
    hi = jnp.minimum(look(ends), (blk + 1.0) * MOE_ROWS) - blk * MOE_ROWS
    valid = idx1 < total
    blk = jnp.where(valid, blk, nblk - 1.0)
    lo = jnp.where(valid, lo, 0.0)
    hi = jnp.where(valid, hi, 0.0)
    rep = lambda c: jnp.broadcast_to(c, (p, LANE))
    prev = lambda c: pltpu.roll(rep(c), 1, axis=0)[:, 0:1]
    is0 = idx1 == 0.0
    first = valid & (is0 | (blk != prev(blk)))
    newe = valid & (is0 | (e != prev(e)))
    rp = lax.broadcasted_iota(i32, (p, p), 0)
    cp = lax.broadcasted_iota(i32, (p, p), 1)
    lower = jnp.where(rp >= cp, 1.0, 0.0).astype(bf16)
    order = _dot(lower, rep(jnp.where(newe, 1.0, 0.0)).astype(bf16))[:, 0:1] - 1.0
    slot = jnp.where(newe, order - W_SLOTS * jnp.floor((order + 0.5) * (1.0 / W_SLOTS)), 0.0)
    k2 = order + float(W_SLOTS)
    pre = jnp.where(newe & (k2 < n_uniq), count_le(cum_act, rep(k2)), -1.0)
    out = jnp.zeros((p, LANE), f32)
    for c, v in enumerate([blk, lo, hi, jnp.where(first, 1.0, 0.0), jnp.where(newe, 1.0, 0.0), slot, pre]):
        out = jnp.where(lane == c, v, out)
    items_ref[...] = out.astype(i32)

    init_row = jnp.zeros((1, LANE), f32)
    for k in range(W_SLOTS):
        init_row = jnp.where(lane1 == k, jnp.where(n_uniq > float(k), count_le(cum_act, float(k)), -1.0), init_row)
    rows_ref[...] = jnp.zeros(rows_ref.shape, f32)
    rows_ref[0:1, :] = starts
    rows_ref[1:2, :] = init_row


def _work_items(cnt_row, n_rows):
    nblk = n_rows // MOE_ROWS
    n_items = nblk + N_EXPERTS - 1
    assert n_items <= PLAN_ROWS and n_rows % MOE_ROWS == 0
    items, rows = pl.pallas_call(
        functools.partial(_plan_kernel, nblk=nblk),
        out_shape=(jax.ShapeDtypeStruct((PLAN_ROWS, LANE), i32), jax.ShapeDtypeStruct((8, LANE), f32)),
        compiler_params=pltpu.CompilerParams(vmem_limit_bytes=VMEM_LIMIT),
        name="moe_plan",
    )(cnt_row)
    fields = tuple(items[0:n_items, c] for c in range(N_ITEM_FIELDS))
    return rows[0:1, :], fields + (rows[1, 0:W_SLOTS].astype(i32),)


def kernel(x_prompt, x_sample, state_delta, state_qkv_conv, state_short_conv, norm1_w, w_in, conv_a_w, a_log, dt_bias, out_norm_w, w_branch_a, conv_b_w, w_branch_b, w_o, norm2_w, router_group_w, router_group_b, router_expert_w, router_expert_b, w_gate, w_up, w_down, final_norm_w):
    assert norm1_w.shape[0] == 1, "single-layer trunk"
    bp, tp, d = x_prompt.shape
    bs, ts, _ = x_sample.shape
    assert d == D_MODEL and ts == 1
    n_p = bp * tp
    n_s = bs
    n_all = n_p + n_s

    w_perm = _wprep(jnp.transpose(w_in[0]))
    wa = w_branch_a[0].astype(bf16)
    wb = w_branch_b[0].astype(bf16)
    wo = w_o[0].astype(bf16)
    pad = lambda v: jnp.zeros((1, BA_W), f32).at[0, DECAY_LANE:DECAY_LANE + N_HEADS].set(v)
    alog_row = pad(a_log[0])
    dtb_row = pad(dt_bias[0])
    onw_row = out_norm_w[0].reshape(1, HEAD)
    cwa = conv_a_w[0]
    cwb = conv_b_w[0]
    r_pad = LANE - N_EXPERTS - N_GROUPS
    rw = jnp.concatenate([router_expert_w[0], router_group_w[0], jnp.zeros((D_MODEL, r_pad), f32)], axis=1)
    rwh = rw.astype(bf16)
    rwl = (rw - rwh.astype(f32)).astype(bf16)
    rb_row = jnp.concatenate([router_expert_b[0], router_group_b[0], jnp.zeros((r_pad,), f32)]).reshape(1, LANE)
    n2_row = norm2_w[0].reshape(1, D_MODEL)

    xp2 = x_prompt.reshape(n_p, D_MODEL)
    proj_p, tails = _inproj_conv(xp2, norm1_w[0], w_perm, cwa, tp)
    tiles_per_seq = tails.shape[0] // bp
    nca_p = tails.reshape(bp, tiles_per_seq, 8, CONV_PAD_W)[:, -1, 8 - (CONV_A - 1):8, 0:QKV_W]
    o_p, y_p, sd_p, ncb_p = _delta_prompt(proj_p.reshape(bp, tp, PROJ_W), cwb, alog_row, dtb_row,
                                          onw_row, nb_step=4 if bp % 4 == 0 else (2 if bp % 2 == 0 else 1))
    cnt0 = jnp.zeros((1, LANE), f32)
    x1_p, h2_p, mi_p, mw_p, cnt_p = _mix_route(xp2, o_p.reshape(n_p, QK_W), y_p.reshape(n_p, SC_W), proj_p,
                                               wa, wb, wo, n2_row, rwh, rwl, rb_row, cnt0)

    xs2 = x_sample.reshape(n_s, D_MODEL)
    proj_s = _inproj(xs2, norm1_w[0], w_perm)
    bufa_t = jnp.transpose(state_qkv_conv[0], (1, 0, 2))
    bufb_t = jnp.transpose(state_short_conv[0], (1, 0, 2))
    q_s, k_s, v_s, beta_s, eg_s, z_s, y_s, nbufa_t, nbufb_t = _sample_prep(proj_s, bufa_t, bufb_t, cwa, cwb,
                                                                           alog_row, dtb_row)
    sd_s, o_s = _sample_step(state_delta[0], q_s, k_s, v_s, beta_s, eg_s, z_s, onw_row)
    o_s2 = o_s.reshape(n_s, QK_W).astype(bf16)
    x1_s, h2_s, mi_s, mw_s, cnt = _mix_route(xs2, o_s2, y_s, proj_s, wa, wb, wo, n2_row, rwh, rwl, rb_row, cnt_p)

    starts_row, items = _work_items(cnt, 2 * n_all)
    mi_flat = jnp.concatenate([_dest_rows(mi_p, starts_row)[:, 0:MI_W], _dest_rows(mi_s, starts_row)[:, 0:MI_W]],
                              axis=0).reshape(MI_W * n_all)
    xs_sorted = _dispatch(h2_p, h2_s, mi_flat)
    ys = _moe(xs_sorted.reshape(2 * n_all, D_MODEL), w_gate[0], w_up[0], w_down[0], items)
    y_prompt, y_sample = _combine(x1_p, mw_p, x1_s, mw_s, final_norm_w.reshape(1, D_MODEL),
                                  ys.reshape(2 * n_all // SUBLANE, SUBLANE, D_MODEL), mi_flat)

    return (y_prompt.reshape(bp, tp, D_MODEL),
            y_sample.reshape(bs, ts, D_MODEL),
            sd_p[None],
            nca_p[None],
            ncb_p[None],
            sd_s[None],
            jnp.transpose(nbufa_t, (1, 0, 2))[None],
            jnp.transpose(nbufb_t, (1, 0, 2))[None])
```
